```python
import jax
import jax.numpy as jnp
from jax import lax
import numpy as np

D_MODEL = 1024
BATCH = 2
SEQ = 8192
DEPTH = 1

NORM_EPS = 1e-6
HG_KDIM = 128
HG_VDIM = 128
HG_HEADS = (D_MODEL // 2) // HG_VDIM
HG_FDIM = HG_HEADS * HG_KDIM
HG_WIDTH = HG_HEADS * HG_VDIM
HG_CHUNK = 64
RW_HEAD = 64
RW_WIDTH = D_MODEL // 2
RW_HEADS = RW_WIDTH // RW_HEAD
RW_DECAY_LORA = 32
RW_AAA_LORA = 32
RW_GATE_LORA = 96
RW_SHIFT_COLS = 3 * RW_WIDTH + RW_DECAY_LORA + RW_AAA_LORA + RW_GATE_LORA
RW_GN_EPS = 64e-5
RW_SPLITS = (RW_WIDTH, 2 * RW_WIDTH, 3 * RW_WIDTH, 3 * RW_WIDTH + RW_DECAY_LORA,
             3 * RW_WIDTH + RW_DECAY_LORA + RW_AAA_LORA)
IN_SPLITS = (HG_FDIM, 2 * HG_FDIM, 2 * HG_FDIM + HG_WIDTH, 2 * HG_FDIM + 2 * HG_WIDTH,
             2 * HG_FDIM + 2 * HG_WIDTH + RW_SHIFT_COLS)
IN_COLS = 2 * HG_FDIM + 2 * HG_WIDTH + RW_SHIFT_COLS + 2 * D_MODEL
N_GROUPS = 4
EXPERTS_PER_GROUP = 8
N_EXPERTS = N_GROUPS * EXPERTS_PER_GROUP
TOP_K = 2
D_FF_EXPERT = 512
MOE_BLOCK = 128

kernel_name = 'hgrn2_rwkv7_gated_hier_moe_block'


def rmsnorm(x, gain, eps=NORM_EPS):
    xf = x.astype(jnp.float32)
    y = xf * lax.rsqrt(jnp.mean(xf * xf, axis=-1, keepdims=True) + eps)
    return (y * gain.astype(jnp.float32)).astype(x.dtype)


def hgrn2_mixer(q, f_raw, i, g, lb, norm_g):
    B, S, _ = q.shape
    nc = S // HG_CHUNK
    f32 = jnp.float32
    f = lb + (1.0 - lb) * jax.nn.sigmoid(f_raw.astype(f32))
    log_f = jnp.log(f)
    k = 1.0 - f
    qf = jax.nn.silu(q.astype(f32))

    def to_chunks(t, d):
        return t.reshape(B, nc, HG_CHUNK, HG_HEADS, d).transpose(1, 0, 3, 2, 4)

    qc, kc, lc = to_chunks(qf, HG_KDIM), to_chunks(k, HG_KDIM), to_chunks(log_f, HG_KDIM)
    vc = to_chunks(i.astype(f32), HG_VDIM)
    causal = jnp.tril(jnp.ones((HG_CHUNK, HG_CHUNK), dtype=bool))[None, None, :, :, None]

    def chunk_step(state, inp):
        q_t, k_t, v_t, l_t = inp
        cum = jnp.cumsum(l_t, axis=2)
        o_inter = jnp.einsum('bhtk,bhkv->bhtv', q_t * jnp.exp(cum), state)
        rel = jnp.where(causal, cum[:, :, :, None, :] - cum[:, :, None, :, :], -jnp.inf)
        scores = jnp.einsum('bhtk,bhtsk,bhsk->bhts', q_t, jnp.exp(rel), k_t)
        o = o_inter + jnp.einsum('bhts,bhsv->bhtv', scores, v_t)
        last = cum[:, :, -1:, :]
        state = (jnp.exp(last[:, :, 0, :])[..., None] * state
                 + jnp.einsum('bhsk,bhsv->bhkv', k_t * jnp.exp(last - cum), v_t))
        return state, o

    s0 = jnp.zeros((B, HG_HEADS, HG_KDIM, HG_VDIM), f32)
    _, o = lax.scan(chunk_step, s0, (qc, kc, vc, lc))
    o = o.transpose(1, 0, 3, 2, 4).reshape(B, S, HG_HEADS, HG_VDIM)
    o = (o * lax.rsqrt(jnp.mean(o * o, axis=-1, keepdims=True) + NORM_EPS)
         * norm_g.astype(f32).reshape(HG_HEADS, HG_VDIM))
    return (o.reshape(B, S, HG_WIDTH) * jax.nn.silu(g.astype(f32))).astype(g.dtype)


def rwkv7_mixer(p, mu, w0, w2, a0, a2, g2, k_k, k_a, r_k, gn_w, gn_b):
    B, S, _ = p.shape
    dt = p.dtype
    f32 = jnp.float32
    p = p.astype(f32)
    prev = jnp.pad(p, ((0, 0), (1, 0), (0, 0)))[:, :-1]
    p = p + mu.astype(f32) * (prev - p)
    r, k, v, wd, ad, gd = jnp.split(p, RW_SPLITS, axis=-1)
    w = -jax.nn.softplus(-(w0.astype(f32) + jnp.tanh(wd) @ w2.astype(f32))) - 0.5
    decay = jnp.exp(-jnp.exp(w))
    a = jax.nn.sigmoid(a0.astype(f32) + ad @ a2.astype(f32))
    g = jax.nn.sigmoid(gd) @ g2.astype(f32)

    def heads(t):
        return t.reshape(B, S, RW_HEADS, RW_HEAD)

    kk = heads(k * k_k.astype(f32))
    kk = kk * lax.rsqrt(jnp.maximum(jnp.sum(kk * kk, axis=-1, keepdims=True), 1e-24))
    k = k * (1.0 + (a - 1.0) * k_a.astype(f32))
    r_h, k_h, v_h, w_h, a_h = heads(r), heads(k), heads(v), heads(decay), heads(a)

    def seq(t):
        return jnp.swapaxes(t, 0, 1)

    def time_step(state, inp):
        r_t, w_t, k_t, v_t, a_t, b_t = inp
        sa = jnp.einsum('bhij,bhj->bhi', state, a_t)
        state = (state * w_t[:, :, None, :] + sa[..., None] * b_t[:, :, None, :]
                 + v_t[..., None] * k_t[:, :, None, :])
        return state, jnp.einsum('bhij,bhj->bhi', state, r_t)

    s0 = jnp.zeros((B, RW_HEADS, RW_HEAD, RW_HEAD), f32)
    _, y = lax.scan(time_step, s0, (seq(r_h), seq(w_h), seq(k_h), seq(v_h), seq(-kk), seq(kk * a_h)))
    y = seq(y)
    mean = jnp.mean(y, axis=-1, keepdims=True)
    var = jnp.mean(jnp.square(y - mean), axis=-1, keepdims=True)
    y = ((y - mean) * lax.rsqrt(var + RW_GN_EPS)).reshape(B, S, RW_WIDTH) * gn_w.astype(f32) + gn_b.astype(f32)
    bonus = jnp.sum(r_h * k_h * r_k.astype(f32), axis=-1, keepdims=True) * v_h
    return ((y + bonus.reshape(B, S, RW_WIDTH)) * g).astype(dt)


def hier_moe(h, wg_r, bg_r, we_r, be_r, w_gate, w_up, w_down):
    B, S, D = h.shape
    N = B * S
    f32 = jnp.float32
    xf = h.reshape(N, D)
    lg = (xf @ wg_r).astype(f32) + bg_r.astype(f32)
    pg = jax.nn.softmax(lg, axis=-1)
    _, grp_idx = lax.top_k(lg, 1)
    p_grp = jnp.take_along_axis(pg, grp_idx, axis=1)[:, 0]
    le = ((xf @ we_r).astype(f32) + be_r.astype(f32)).reshape(N, N_GROUPS, EXPERTS_PER_GROUP)
    sel = jnp.broadcast_to(grp_idx[:, :, None], (N, 1, EXPERTS_PER_GROUP))
    pe = jax.nn.softmax(jnp.take_along_axis(le, sel, axis=1)[:, 0], axis=-1)
    top_p, top_i = lax.top_k(pe, TOP_K)
    wts = p_grp[:, None] * top_p / jnp.sum(top_p, axis=-1, keepdims=True)
    eid = (grp_idx * EXPERTS_PER_GROUP + top_i).astype(jnp.int32)

    A = N * TOP_K
    eid_flat = eid.reshape(A)
    tok_flat = jnp.repeat(jnp.arange(N, dtype=jnp.int32), TOP_K)
    w_flat = wts.reshape(A)
    sorted_e, order = lax.sort_key_val(eid_flat, jnp.arange(A, dtype=jnp.int32))
    counts = jnp.bincount(eid_flat, length=N_EXPERTS).astype(jnp.int32)
    starts = jnp.cumsum(counts) - counts
    padded = (counts + MOE_BLOCK - 1) // MOE_BLOCK * MOE_BLOCK
    pad_end = jnp.cumsum(padded)
    pad_start = pad_end - padded
    dest = pad_start[sorted_e] + (jnp.arange(A, dtype=jnp.int32) - starts[sorted_e])
    n_blocks = -(-(A + N_EXPERTS * MOE_BLOCK) // MOE_BLOCK)
    P = n_blocks * MOE_BLOCK
    buf_tok = jnp.zeros((P,), jnp.int32).at[dest].set(tok_flat[order])
    buf_w = jnp.zeros((P,), f32).at[dest].set(w_flat[order])
    blk_e = jnp.minimum(jnp.searchsorted(pad_end, jnp.arange(n_blocks, dtype=jnp.int32) * MOE_BLOCK,
                                         side='right'), N_EXPERTS - 1).astype(jnp.int32)
    xb = xf[buf_tok].reshape(n_blocks, MOE_BLOCK, D)

    def expert_block(args):
        xblk, e = args
        hid = jax.nn.silu(xblk @ w_gate[e]) * (xblk @ w_up[e])
        return hid @ w_down[e]

    yb = lax.map(expert_block, (xb, blk_e)).reshape(P, D)
    out = jnp.zeros((N, D), f32).at[buf_tok].add(yb.astype(f32) * buf_w[:, None])
    return out.reshape(B, S, D).astype(h.dtype)


def setup_inputs(seed: int = 0) -> dict:
    key = jax.random.key(seed)
    ks = iter(jax.random.split(key, 40))
    L, D = DEPTH, D_MODEL

    def nrm(shape, scale):
        return scale * jax.random.normal(next(ks), shape, jnp.float32)

    return {
        'x': nrm((BATCH, SEQ, D), 1.0),
        'c': nrm((BATCH, D), 1.0),
        'ada_w': nrm((L, D, 6 * D), 0.5 * D ** -0.5),
        'ada_b': nrm((L, 6 * D), 0.01),
        'norm1_g': 1.0 + nrm((L, D), 0.02),
        'w_in': nrm((L, D, IN_COLS), D ** -0.5),
        'hg_lb': nrm((L + 1, HG_FDIM), 0.5),
        'hg_norm_g': 1.0 + nrm((L, HG_WIDTH), 0.02),
        'rw_mu': jax.random.uniform(next(ks), (L, RW_SHIFT_COLS), jnp.float32),
        'rw_w0': jnp.linspace(-6.0, -1.0, RW_WIDTH, dtype=jnp.float32) + nrm((L, RW_WIDTH), 0.1),
        'rw_w2': nrm((L, RW_DECAY_LORA, RW_WIDTH), 0.1 * RW_DECAY_LORA ** -0.5),
        'rw_a0': nrm((L, RW_WIDTH), 0.1),
        'rw_a2': nrm((L, RW_AAA_LORA, RW_WIDTH), RW_AAA_LORA ** -0.5),
        'rw_g2': nrm((L, RW_GATE_LORA, RW_WIDTH), RW_GATE_LORA ** -0.5),
        'rw_kk': 0.85 + nrm((L, RW_WIDTH), 0.02),
        'rw_ka': 1.0 + nrm((L, RW_WIDTH), 0.02),
        'rw_rk': -0.04 + nrm((L, RW_HEADS, RW_HEAD), 0.02),
        'rw_gn_w': 1.0 + nrm((L, RW_WIDTH), 0.02),
        'rw_gn_b': nrm((L, RW_WIDTH), 0.01),
        'w_proj_a': nrm((L, HG_WIDTH, D), HG_WIDTH ** -0.5),
        'w_proj_b': nrm((L, RW_WIDTH, D), RW_WIDTH ** -0.5),
        'w_out': nrm((L, D, D), D ** -0.5),
        'norm2_g': 1.0 + nrm((L, D), 0.02),
        'router_g_w': nrm((L, D, N_GROUPS), D ** -0.5),
        'router_g_b': nrm((L, N_GROUPS), 0.01),
        'router_e_w': nrm((L, D, N_EXPERTS), D ** -0.5),
        'router_e_b': nrm((L, N_EXPERTS), 0.01),
        'exp_w_gate': nrm((L, N_EXPERTS, D, D_FF_EXPERT), D ** -0.5),
        'exp_w_up': nrm((L, N_EXPERTS, D, D_FF_EXPERT), D ** -0.5),
        'exp_w_down': nrm((L, N_EXPERTS, D_FF_EXPERT, D), D_FF_EXPERT ** -0.5),
        'final_g': 1.0 + nrm((D,), 0.02),
    }


def reference(x, c, ada_w, ada_b, norm1_g, w_in, hg_lb, hg_norm_g, rw_mu, rw_w0, rw_w2, rw_a0, rw_a2,
              rw_g2, rw_kk, rw_ka, rw_rk, rw_gn_w, rw_gn_b, w_proj_a, w_proj_b, w_out, norm2_g,
              router_g_w, router_g_b, router_e_w, router_e_b, exp_w_gate, exp_w_up, exp_w_down, final_g):
    lb_all = jnp.cumsum(jax.nn.softmax(hg_lb.astype(jnp.float32), axis=0), axis=0)
    cs = jax.nn.silu(c)
    for l in range(DEPTH):
        mod = cs @ ada_w[l] + ada_b[l]
        sh1, sc1, gt1, sh2, sc2, gt2 = jnp.split(mod[:, None, :], 6, axis=-1)
        h = rmsnorm(x, norm1_g[l]) * (1.0 + sc1) + sh1
        proj = h @ w_in[l]
        hq, hf, hi, hg, rw_p, gates = jnp.split(proj, IN_SPLITS, axis=-1)
        o_a = hgrn2_mixer(hq, hf, hi, hg, lb_all[l], hg_norm_g[l])
        o_b = rwkv7_mixer(rw_p, rw_mu[l], rw_w0[l], rw_w2[l], rw_a0[l], rw_a2[l], rw_g2[l],
                          rw_kk[l], rw_ka[l], rw_rk[l], rw_gn_w[l], rw_gn_b[l])
        gate_a, gate_b = jnp.split(jax.nn.sigmoid(gates), 2, axis=-1)
        mixed = gate_a * (o_a @ w_proj_a[l]) + gate_b * (o_b @ w_proj_b[l])
        x = x + gt1 * (mixed @ w_out[l])
        h2 = rmsnorm(x, norm2_g[l]) * (1.0 + sc2) + sh2
        x = x + gt2 * hier_moe(h2, router_g_w[l], router_g_b[l], router_e_w[l], router_e_b[l],
                               exp_w_gate[l], exp_w_up[l], exp_w_down[l])
    return rmsnorm(x, final_g)
```

```python
import functools

import numpy as np
import jax
import jax.numpy as jnp
from jax import lax
from jax.experimental import pallas as pl
from jax.experimental.pallas import tpu as pltpu

F32 = jnp.float32
BF16 = jnp.bfloat16
HIGHEST = lax.Precision.HIGHEST

NORM_EPS = 1e-6
HG_HEAD = 128
RW_HEAD = 64
RW_GN_EPS = 64e-5
TOP_K = 2
CHUNK = 64
LANES = 128
VMEM_LIMIT = 56 * 1024 * 1024

NT = (((1,), (1,)), ((), ()))
TN = (((0,), (0,)), ((), ()))


def _dot(a, b, dims=None, precision=None):
    if dims is None:
        return jnp.dot(a, b, preferred_element_type=F32, precision=precision)
    return lax.dot_general(a, b, dims, preferred_element_type=F32, precision=precision)


def _split3(x):
    hi = x.astype(BF16)
    r1 = x - hi.astype(F32)
    mid = r1.astype(BF16)
    lo = (r1 - mid.astype(F32)).astype(BF16)
    return hi, mid, lo


def _dot_exact_lhs(m_bf16, x):
    hi, mid, lo = _split3(x)
    return _dot(m_bf16, hi) + _dot(m_bf16, mid) + _dot(m_bf16, lo)


def _dot_exact_rhs(x, m_bf16):
    hi, mid, lo = _split3(x)
    return _dot(hi, m_bf16) + _dot(mid, m_bf16) + _dot(lo, m_bf16)


def _sigmoid(x):
    return 1.0 / (1.0 + jnp.exp(-x))


def _silu(x):
    return x * _sigmoid(x)


def _cparams(sem):
    return pltpu.CompilerParams(dimension_semantics=sem, vmem_limit_bytes=VMEM_LIMIT)


def _ada_kernel(c_ref, w_ref, b_ref, o_ref):
    c = c_ref[...]
    o_ref[...] = _dot(_silu(c), w_ref[...], precision=HIGHEST) + b_ref[...]


def _ada_mod(c, w, b):
    bsz, d = c.shape
    n = w.shape[1]
    rows = 8
    cp = jnp.zeros((rows, d), F32).at[:bsz].set(c)
    tn = 1536
    out = pl.pallas_call(
        _ada_kernel,
        out_shape=jax.ShapeDtypeStruct((rows, n), F32),
        grid=(n // tn,),
        in_specs=[pl.BlockSpec((rows, d), lambda j: (0, 0)),
                  pl.BlockSpec((d, tn), lambda j: (0, j)),
                  pl.BlockSpec((1, tn), lambda j: (0, j))],
        out_specs=pl.BlockSpec((rows, tn), lambda j: (0, j)),
        compiler_params=_cparams(("arbitrary",)),
        name="ada_mod",
    )(cp, w, b.reshape(1, n))
    return out[:bsz]


def _in_proj_kernel(x_ref, sh_ref, sc_ref, g_ref, whg_ref, wrw_ref, wgt_ref, hg_ref, rw_ref, gt_ref):
    x = x_ref[0]
    ms = jnp.mean(x * x, axis=-1, keepdims=True)
    h = (x * lax.rsqrt(ms + NORM_EPS) * g_ref[...]) * (1.0 + sc_ref[0]) + sh_ref[0]
    hb = h.astype(BF16)
    step = 512
    for n0 in range(0, whg_ref.shape[1], step):
        hg_ref[0, :, n0:n0 + step] = _dot(hb, whg_ref[:, n0:n0 + step])
    for n0 in range(0, wrw_ref.shape[1], 256):
        rw_ref[0, :, n0:n0 + 256] = _dot(hb, wrw_ref[:, n0:n0 + 256])
    for n0 in range(0, wgt_ref.shape[1], step):
        gt_ref[0, :, n0:n0 + step] = _sigmoid(_dot(hb, wgt_ref[:, n0:n0 + step])).astype(BF16)


def _in_proj(x, mod3, norm_g, w_hg, w_rw, w_gt, tm):
    bsz, s, d = x.shape
    n_hg, n_rw, n_gt = w_hg.shape[1], w_rw.shape[1], w_gt.shape[1]
    const = lambda b, i: (0, 0)
    return pl.pallas_call(
        _in_proj_kernel,
        out_shape=(jax.ShapeDtypeStruct((bsz, s, n_hg), F32),
                   jax.ShapeDtypeStruct((bsz, s, n_rw), F32),
                   jax.ShapeDtypeStruct((bsz, s, n_gt), BF16)),
        grid=(bsz, s // tm),
        in_specs=[pl.BlockSpec((1, tm, d), lambda b, i: (b, i, 0)),
                  pl.BlockSpec((1, 1, d), lambda b, i: (b * 6 + 0, 0, 0)),
                  pl.BlockSpec((1, 1, d), lambda b, i: (b * 6 + 1, 0, 0)),
                  pl.BlockSpec((1, d), const),
                  pl.BlockSpec((d, n_hg), const),
                  pl.BlockSpec((d, n_rw), const),
                  pl.BlockSpec((d, n_gt), const)],
        out_specs=(pl.BlockSpec((1, tm, n_hg), lambda b, i: (b, i, 0)),
                   pl.BlockSpec((1, tm, n_rw), lambda b, i: (b, i, 0)),
                   pl.BlockSpec((1, tm, n_gt), lambda b, i: (b, i, 0))),
        compiler_params=_cparams(("arbitrary", "arbitrary")),
        name="in_proj",
    )(x, mod3, mod3, norm_g.reshape(1, d), w_hg, w_rw, w_gt)


_HG_LEVELS = (32, 16, 8, 4, 2, 1)


def _hgrn2_consts():
    c = CHUNK
    t = np.arange(c)[:, None]
    s = np.arange(c)[None, :]
    blocks = [(s <= t), (s > t)]
    lvl_masks = []
    right = []
    for h in _HG_LEVELS:
        m = (t // (2 * h)) * 2 * h + h
        is_r = (t & h) != 0
        blk = np.where(is_r, (s >= m) & (s <= t), (s > t) & (s <= m - 1))
        blocks.append(blk)
        lvl_masks.append(is_r & ((s & h) == 0) & ((t // (2 * h)) == (s // (2 * h))))
        right.append(np.broadcast_to(is_r, (c, HG_HEAD)))
    mst = np.concatenate(blocks, axis=0).astype(np.float32)
    lm = np.stack([np.eye(c, dtype=bool)] + lvl_masks).astype(np.float32)
    rm = np.stack(right).astype(np.float32)
    return jnp.asarray(mst, BF16), jnp.asarray(lm, F32), jnp.asarray(rm, F32)


def _hgrn2_kernel(q_ref, f_ref, i_ref, g_ref, lb_ref, ng_ref, mst_ref, lm_ref, rm_ref, o_ref, st_ref):
    c = CHUNK
    n_heads = q_ref.shape[2] // HG_HEAD
    n_chunks = q_ref.shape[1] // c

    @pl.when(pl.program_id(1) == 0)
    def _():
        st_ref[...] = jnp.zeros_like(st_ref)

    mst = mst_ref[...]
    for hd in range(n_heads):
        ls = slice(hd * HG_HEAD, (hd + 1) * HG_HEAD)
        lb = lb_ref[:, ls]
        ng = ng_ref[:, ls]

        def chunk_body(ci, carry, ls=ls, lb=lb, ng=ng, hd=hd):
            r0 = pl.multiple_of(ci * c, c)
            rows = pl.ds(r0, c)
            q = _silu(q_ref[0, rows, ls])
            f = lb + (1.0 - lb) * _sigmoid(f_ref[0, rows, ls])
            lf = jnp.log(f)
            k = 1.0 - f
            v = i_ref[0, rows, ls]
            ex = jnp.exp(_dot_exact_lhs(mst, lf))
            ex_cum = ex[0:c]
            ex_rem = ex[c:2 * c]
            st = st_ref[hd]
            o = _dot((q * ex_cum).astype(BF16), st.astype(BF16), NT)
            scores = lm_ref[0] * _dot(q.astype(BF16), k.astype(BF16), NT)
            for li in range(len(_HG_LEVELS)):
                rmask = rm_ref[li]
                qk = k + rmask * (q - k)
                g_l = (qk * ex[(2 + li) * c:(3 + li) * c]).astype(BF16)
                scores = scores + lm_ref[li + 1] * _dot(g_l, g_l, NT)
            o = o + _dot(scores.astype(BF16), v.astype(BF16))
            st_ref[hd] = st * ex_cum[c - 1:c, :] + _dot(v.astype(BF16), (k * ex_rem).astype(BF16), TN)
            on = o * lax.rsqrt(jnp.mean(o * o, axis=-1, keepdims=True) + NORM_EPS) * ng
            o_ref[0, rows, ls] = (on * _silu(g_ref[0, rows, ls])).astype(o_ref.dtype)
            return carry

        lax.fori_loop(0, n_chunks, chunk_body, 0)


def _hgrn2(hg, lb, norm_g, ts):
    bsz, s, n4 = hg.shape
    w = n4 // 4
    mst, lm, rm = _hgrn2_consts()
    n_heads = w // HG_HEAD
    const2 = lambda b, i: (0, 0)
    const3 = lambda b, i: (0, 0, 0)
    return pl.pallas_call(
        _hgrn2_kernel,
        out_shape=jax.ShapeDtypeStruct((bsz, s, w), BF16),
        grid=(bsz, s // ts),
        in_specs=[pl.BlockSpec((1, ts, w), lambda b, i: (b, i, 0)),
                  pl.BlockSpec((1, ts, w), lambda b, i: (b, i, 1)),
                  pl.BlockSpec((1, ts, w), lambda b, i: (b, i, 2)),
                  pl.BlockSpec((1, ts, w), lambda b, i: (b, i, 3)),
                  pl.BlockSpec((1, w), const2),
                  pl.BlockSpec((1, w), const2),
                  pl.BlockSpec(mst.shape, const2),
                  pl.BlockSpec(lm.shape, const3),
                  pl.BlockSpec(rm.shape, const3)],
        out_specs=pl.BlockSpec((1, ts, w), lambda b, i: (b, i, 0)),
        scratch_shapes=[pltpu.VMEM((n_heads, HG_HEAD, HG_HEAD), F32)],
        compiler_params=_cparams(("arbitrary", "arbitrary")),
        name="hgrn2",
    )(hg, hg, hg, hg, lb.reshape(1, w), norm_g.reshape(1, w), mst, lm, rm)


def _rwkv_consts(width):
    c = CHUNK
    t = np.arange(c)[:, None]
    s = np.arange(c)[None, :]
    tri = (s <= t).astype(np.float32)
    tt = np.arange(2 * c)[:, None]
    ss = np.arange(2 * c)[None, :]
    same = (tt // c) == (ss // c)
    strict = same & ((ss % c) < (tt % c))
    incl = same & ((ss % c) <= (tt % c))
    hsum = (np.arange(width)[:, None] // RW_HEAD) == (np.arange(width)[None, :] // RW_HEAD)
    return (jnp.asarray(tri, BF16), jnp.asarray(strict.astype(np.float32), F32),
            jnp.asarray(incl.astype(np.float32), F32), jnp.asarray(hsum.astype(np.float32), BF16))


def _rwkv7_kernel(p_ref, mu_ref, w0_ref, a0_ref, kk_ref, ka_ref, rk_ref, gnw_ref, gnb_ref,
                  w2_ref, a2_ref, g2_ref, tri_ref, sm_ref, im_ref, hs_ref,
                  o_ref, carry_ref, zt_ref):
    c = CHUNK
    width = o_ref.shape[2]
    n_pairs = width // LANES

    @pl.when(pl.program_id(1) == 0)
    def _():
        carry_ref[...] = jnp.zeros_like(carry_ref)
        zt_ref[...] = jnp.zeros_like(zt_ref)

    p = p_ref[0]
    row = lax.broadcasted_iota(jnp.int32, p.shape, 0)
    prev = jnp.where(row == 0, carry_ref[...], pltpu.roll(p, 1, 0))
    carry_ref[...] = p[c - 1:c, :]
    xs = p + mu_ref[...] * (prev - p)

    r = xs[:, 0:width]
    k = xs[:, width:2 * width]
    v = xs[:, 2 * width:3 * width]
    slab = xs[:, 3 * width:]
    z = w0_ref[...] + _dot(jnp.tanh(slab).astype(BF16), w2_ref[...])
    nz = -z
    softplus = jnp.maximum(nz, 0.0) + jnp.log(1.0 + jnp.exp(-jnp.abs(nz)))
    ld = -jnp.exp(-softplus - 0.5)
    a = _sigmoid(a0_ref[...] + _dot(slab.astype(BF16), a2_ref[...]))
    g = _dot(_sigmoid(slab).astype(BF16), g2_ref[...])

    hs = hs_ref[...]
    kk0 = k * kk_ref[...]
    kk = kk0 * lax.rsqrt(jnp.maximum(_dot_exact_rhs(kk0 * kk0, hs), 1e-24))
    k2 = k * (1.0 + (a - 1.0) * ka_ref[...])
    a_in = -kk
    b_in = kk * a

    tri = tri_ref[...]
    smask = sm_ref[...]
    imask = im_ref[...]
    lane = lax.broadcasted_iota(jnp.int32, (c, LANES), 1)
    m0 = (lane < RW_HEAD).astype(F32)
    m1 = 1.0 - m0

    def stack(x):
        return jnp.concatenate([x * m0, x * m1], axis=0)

    ys = []
    for pi in range(n_pairs):
        ls = slice(pi * LANES, (pi + 1) * LANES)
        ldp = ld[:, ls]
        cum = _dot_exact_lhs(tri, ldp)
        cum_t = cum[c - 1:c, :]
        e_c = jnp.exp(cum)
        e_nc = jnp.exp(-cum)
        e_cm = jnp.exp(cum - ldp)
        e_rem = jnp.exp(cum_t - cum)
        at = stack(a_in[:, ls] * e_cm).astype(BF16)
        rt = stack(r[:, ls] * e_c).astype(BF16)
        kt = stack(k2[:, ls] * e_nc).astype(BF16)
        bt = stack(b_in[:, ls] * e_nc).astype(BF16)
        kh = stack(k2[:, ls] * e_rem).astype(BF16)
        bh = stack(b_in[:, ls] * e_rem).astype(BF16)
        vs = stack(v[:, ls])
        vsb = vs.astype(BF16)

        lhs = jnp.concatenate([at, rt], axis=0)
        gk = _dot(lhs, kt, NT)
        gb = _dot(lhs, bt, NT)
        a_ak = jnp.where(smask > 0, gk[:2 * c], 0.0)
        a_rk = jnp.where(imask > 0, gk[2 * c:], 0.0)
        a_ab = jnp.where(smask > 0, gb[:2 * c], 0.0)
        a_rb = jnp.where(imask > 0, gb[2 * c:], 0.0)

        av = _dot(jnp.concatenate([a_ak, a_rk], axis=0).astype(BF16), vsb)
        akv = av[:2 * c]
        arkv = av[2 * c:]

        pw = [a_ab.astype(BF16)]
        n_sq = int(np.log2(c)) - 1
        for _ in range(n_sq):
            pw.append(_dot(pw[-1], pw[-1]).astype(BF16))
        xcat = jnp.concatenate([at.astype(F32), akv], axis=1)
        for m in reversed(pw):
            xcat = xcat + _dot(m, xcat.astype(BF16))
        w_mat = xcat[:, :LANES]
        u_loc = xcat[:, LANES:]

        zt = zt_ref[pi]
        ztb = zt.astype(BF16)
        uy = _dot(jnp.concatenate([w_mat.astype(BF16), rt], axis=0), ztb, NT)
        u = uy[:2 * c] + u_loc
        y = uy[2 * c:] + arkv + _dot(a_rb.astype(BF16), u.astype(BF16))
        ys.append(y[:c] + y[c:])
        upd = _dot(jnp.concatenate([u.astype(BF16), vsb], axis=0),
                   jnp.concatenate([bh, kh], axis=0), TN)
        zt_ref[pi] = zt * jnp.exp(cum_t) + upd

    y = jnp.concatenate(ys, axis=1)
    inv_n = 1.0 / RW_HEAD
    mean = _dot_exact_rhs(y, hs) * inv_n
    d = y - mean
    var = _dot_exact_rhs(d * d, hs) * inv_n
    yn = d * lax.rsqrt(var + RW_GN_EPS) * gnw_ref[...] + gnb_ref[...]
    bonus = _dot_exact_rhs(r * k2 * rk_ref[...], hs) * v
    o_ref[0] = ((yn + bonus) * g).astype(o_ref.dtype)


def _rwkv7(rw, mu, w0, w2, a0, a2, g2, k_k, k_a, r_k, gn_w, gn_b):
    bsz, s, cols = rw.shape
    width = w0.shape[-1]
    n_pairs = width // LANES
    slab = cols - 3 * width
    dl, al, gl = w2.shape[0], a2.shape[0], g2.shape[0]
    w2f = jnp.zeros((slab, width), F32).at[0:dl].set(w2).astype(BF16)
    a2f = jnp.zeros((slab, width), F32).at[dl:dl + al].set(a2).astype(BF16)
    g2f = jnp.zeros((slab, width), F32).at[dl + al:dl + al + gl].set(g2).astype(BF16)
    mup = jnp.zeros((1, cols), F32).at[0, :mu.shape[-1]].set(mu)
    tri, sm, im, hs = _rwkv_consts(width)
    row = lambda x: x.reshape(1, width)
    const = lambda b, i: (0, 0)
    vec = pl.BlockSpec((1, width), const)
    return pl.pallas_call(
        _rwkv7_kernel,
        out_shape=jax.ShapeDtypeStruct((bsz, s, width), BF16),
        grid=(bsz, s // CHUNK),
        in_specs=[pl.BlockSpec((1, CHUNK, cols), lambda b, i: (b, i, 0)),
                  pl.BlockSpec((1, cols), const),
                  vec, vec, vec, vec, vec, vec, vec,
                  pl.BlockSpec((slab, width), const),
                  pl.BlockSpec((slab, width), const),
                  pl.BlockSpec((slab, width), const),
                  pl.BlockSpec(tri.shape, const),
                  pl.BlockSpec(sm.shape, const),
                  pl.BlockSpec(im.shape, const),
                  pl.BlockSpec(hs.shape, const)],
        out_specs=pl.BlockSpec((1, CHUNK, width), lambda b, i: (b, i, 0)),
        scratch_shapes=[pltpu.VMEM((1, cols), F32),
                        pltpu.VMEM((n_pairs, LANES, LANES), F32)],
        compiler_params=_cparams(("arbitrary", "arbitrary")),
        name="rwkv7",
    )(rw, mup, row(w0), row(a0), row(k_k), row(k_a), row(r_k), row(gn_w), row(gn_b),
      w2f, a2f, g2f, tri, sm, im, hs)


def _out_proj_kernel(n_groups, n_experts,
                     x_ref, oa_ref, ob_ref, ga_ref, gb_ref, gt1_ref, sc2_ref, sh2_ref, g2_ref,
                     wa_ref, wb_ref, wo_ref, wr_ref, br_ref, tril_ref,
                     x1_ref, h2_ref, route_ref, cnt_ref, carry_ref):
    tm = x_ref.shape[1]
    first = (pl.program_id(0) == 0) & (pl.program_id(1) == 0)

    @pl.when(first)
    def _():
        carry_ref[...] = jnp.zeros_like(carry_ref)

    pa = _dot(oa_ref[0], wa_ref[...])
    pb = _dot(ob_ref[0], wb_ref[...])
    mixed = ga_ref[0].astype(F32) * pa + gb_ref[0].astype(F32) * pb
    x1 = x_ref[0] + gt1_ref[0] * _dot(mixed.astype(BF16), wo_ref[...])
    x1_ref[0] = x1
    ms = jnp.mean(x1 * x1, axis=-1, keepdims=True)
    h2 = (x1 * lax.rsqrt(ms + NORM_EPS) * g2_ref[...]) * (1.0 + sc2_ref[0]) + sh2_ref[0]
    h2_ref[0] = h2

    logits = _dot(h2, wr_ref[...], precision=HIGHEST) + br_ref[...]
    lane = lax.broadcasted_iota(jnp.int32, logits.shape, 1)
    neg = jnp.float32(-jnp.inf)
    big = jnp.int32(1 << 20)
    eg = n_experts // n_groups
    is_g = (lane >= n_experts) & (lane < n_experts + n_groups)
    lg = jnp.where(is_g, logits, neg)
    mg = jnp.max(lg, axis=-1, keepdims=True)
    p_grp = 1.0 / jnp.sum(jnp.where(is_g, jnp.exp(lg - mg), 0.0), axis=-1, keepdims=True)
    gidx = jnp.min(jnp.where(lg == mg, lane, big), axis=-1, keepdims=True) - n_experts
    sel = (lane >= gidx * eg) & (lane < gidx * eg + eg)
    le = jnp.where(sel, logits, neg)
    me = jnp.max(le, axis=-1, keepdims=True)
    pe_un = jnp.where(sel, jnp.exp(le - me), 0.0)
    pe = jnp.where(sel, pe_un / jnp.sum(pe_un, axis=-1, keepdims=True), -1.0)
    v1 = jnp.max(pe, axis=-1, keepdims=True)
    i1 = jnp.min(jnp.where(pe == v1, lane, big), axis=-1, keepdims=True)
    pe2 = jnp.where(lane == i1, -1.0, pe)
    v2 = jnp.max(pe2, axis=-1, keepdims=True)
    i2 = jnp.min(jnp.where(pe2 == v2, lane, big), axis=-1, keepdims=True)
    wsum = v1 + v2
    w1 = p_grp * v1 / wsum
    w2 = p_grp * v2 / wsum

    oh1 = (lane == i1).astype(F32)
    oh2 = (lane == i2).astype(F32)
    both = oh1 + oh2
    before = _dot(tril_ref[...], both.astype(BF16)) + carry_ref[...]
    rank1 = jnp.sum(oh1 * before, axis=-1, keepdims=True)
    rank2 = jnp.sum(oh2 * before, axis=-1, keepdims=True)
    carry_ref[...] = carry_ref[...] + jnp.sum(both, axis=0, keepdims=True)
    cnt_ref[...] = carry_ref[...]

    out = jnp.where(lane == 0, w1, 0.0)
    out = jnp.where(lane == 1, w2, out)
    out = jnp.where(lane == 2, i1.astype(F32), out)
    out = jnp.where(lane == 3, i2.astype(F32), out)
    out = jnp.where(lane == 4, rank1, out)
    out = jnp.where(lane == 5, rank2, out)
    route_ref[0] = out


def _out_proj(x, o_a, o_b, gates, mod3, norm2_g, wa, wb, wo, wr, br, n_groups, n_experts, tm):
    bsz, s, d = x.shape
    wdt = o_a.shape[-1]
    tril = jnp.asarray(np.tril(np.ones((tm, tm), np.float32), -1), BF16)
    const = lambda b, i: (0, 0)
    tile = lambda b, i: (b, i, 0)
    kern = functools.partial(_out_proj_kernel, n_groups, n_experts)
    return pl.pallas_call(
        kern,
        out_shape=(jax.ShapeDtypeStruct((bsz, s, d), F32),
                   jax.ShapeDtypeStruct((bsz, s, d), F32),
                   jax.ShapeDtypeStruct((bsz, s, LANES), F32),
                   jax.ShapeDtypeStruct((1, LANES), F32)),
        grid=(bsz, s // tm),
        in_specs=[pl.BlockSpec((1, tm, d), tile),
                  pl.BlockSpec((1, tm, wdt), tile),
                  pl.BlockSpec((1, tm, wdt), tile),
                  pl.BlockSpec((1, tm, d), lambda b, i: (b, i, 0)),
                  pl.BlockSpec((1, tm, d), lambda b, i: (b, i, 1)),
                  pl.BlockSpec((1, 1, d), lambda b, i: (b * 6 + 2, 0, 0)),
                  pl.BlockSpec((1, 1, d), lambda b, i: (b * 6 + 4, 0, 0)),
                  pl.BlockSpec((1, 1, d), lambda b, i: (b * 6 + 3, 0, 0)),
                  pl.BlockSpec((1, d), const),
                  pl.BlockSpec(wa.shape, const),
                  pl.BlockSpec(wb.shape, const),
                  pl.BlockSpec(wo.shape, const),
                  pl.BlockSpec(wr.shape, const),
                  pl.BlockSpec((1, LANES), const),
                  pl.BlockSpec((tm, tm), const)],
        out_specs=(pl.BlockSpec((1, tm, d), tile),
                   pl.BlockSpec((1, tm, d), tile),
                   pl.BlockSpec((1, tm, LANES), tile),
                   pl.BlockSpec((1, LANES), const)),
        scratch_shapes=[pltpu.VMEM((1, LANES), F32)],
        compiler_params=_cparams(("arbitrary", "arbitrary")),
        name="out_proj",
    )(x, o_a, o_b, gates, gates, mod3, mod3, mod3, norm2_g.reshape(1, d), wa, wb, wo, wr, br, tril)


def _moe_expert_kernel(blk_e_ref, nused_ref, tok_ref, h_ref, wg_ref, wu_ref, wd_ref, y_ref,
                       xbuf, wgb, wub, wdb, sem):
    i = pl.program_id(0)
    blk = y_ref.shape[0]
    n_used = nused_ref[0]

    def row_copy(b, r, slot):
        tok = tok_ref[b * blk + r]
        return pltpu.make_async_copy(h_ref.at[pl.ds(tok, 1), :], xbuf.at[slot, pl.ds(r, 1), :], sem.at[slot])

    def issue(b, slot):
        def body(r, carry):
            row_copy(b, r, slot).start()
            return carry
        lax.fori_loop(0, blk, body, 0)

    def drain(b, slot):
        def body(r, carry):
            row_copy(b, r, slot).wait()
            return carry
        lax.fori_loop(0, blk, body, 0)

    @pl.when((i == 0) & (n_used > 0))
    def _():
        issue(0, 0)

    @pl.when(i + 1 < n_used)
    def _():
        issue(i + 1, (i + 1) % 2)

    new_expert = (i == 0) | (blk_e_ref[i] != blk_e_ref[jnp.maximum(i - 1, 0)])

    @pl.when((i < n_used) & new_expert)
    def _():
        wgb[...] = wg_ref[0].astype(BF16)
        wub[...] = wu_ref[0].astype(BF16)
        wdb[...] = wd_ref[0].astype(BF16)

    @pl.when(i < n_used)
    def _():
        slot = i % 2
        drain(i, slot)
        xb = xbuf[slot].astype(BF16)
        hg = _dot(xb, wgb[...])
        hu = _dot(xb, wub[...])
        hid = (_silu(hg) * hu).astype(BF16)
        y_ref[...] = _dot(hid, wdb[...])

    @pl.when(i >= n_used)
    def _():
        y_ref[...] = jnp.zeros_like(y_ref)


def _moe_experts(h2, slot_tok, blk_e, n_used, w_gate, w_up, w_down, blk):
    n, d = h2.shape
    p = slot_tok.shape[0]
    nb = p // blk
    f = w_gate.shape[-1]
    grid_spec = pltpu.PrefetchScalarGridSpec(
        num_scalar_prefetch=3,
        grid=(nb,),
        in_specs=[pl.BlockSpec(memory_space=pl.ANY),
                  pl.BlockSpec((1, d, f), lambda i, be, nu, tk: (be[i], 0, 0)),
                  pl.BlockSpec((1, d, f), lambda i, be, nu, tk: (be[i], 0, 0)),
                  pl.BlockSpec((1, f, d), lambda i, be, nu, tk: (be[i], 0, 0))],
        out_specs=pl.BlockSpec((blk, d), lambda i, be, nu, tk: (i, 0)),
        scratch_shapes=[pltpu.VMEM((2, blk, d), F32),
                        pltpu.VMEM((d, f), BF16),
                        pltpu.VMEM((d, f), BF16),
                        pltpu.VMEM((f, d), BF16),
                        pltpu.SemaphoreType.DMA((2,))],
    )
    return pl.pallas_call(
        _moe_expert_kernel,
        out_shape=jax.ShapeDtypeStruct((p, d), F32),
        grid_spec=grid_spec,
        compiler_params=_cparams(("arbitrary",)),
        name="moe_experts",
    )(blk_e, n_used, slot_tok, h2, w_gate, w_up, w_down)


def _moe_combine_kernel(dest_ref, x1_ref, route_ref, gt2_ref, fg_ref, y_ref, o_ref, ybuf, sem):
    i = pl.program_id(0)
    n_steps = pl.num_programs(0)
    tm = x1_ref.shape[0]

    def row_copy(step, r, kslot, slot):
        d = dest_ref[(step * tm + r) * TOP_K + kslot]
        return pltpu.make_async_copy(y_ref.at[pl.ds(d, 1), :], ybuf.at[slot, kslot, pl.ds(r, 1), :], sem.at[slot])

    def issue(step, slot):
        def body(r, carry):
            for kslot in range(TOP_K):
                row_copy(step, r, kslot, slot).start()
            return carry
        lax.fori_loop(0, tm, body, 0)

    def drain(step, slot):
        def body(r, carry):
            for kslot in range(TOP_K):
                row_copy(step, r, kslot, slot).wait()
            return carry
        lax.fori_loop(0, tm, body, 0)

    @pl.when(i == 0)
    def _():
        issue(0, 0)

    @pl.when(i + 1 < n_steps)
    def _():
        issue(i + 1, (i + 1) % 2)

    slot = i % 2
    drain(i, slot)
    route = route_ref[...]
    w1 = route[:, 0:1]
    w2 = route[:, 1:2]
    moe = w1 * ybuf[slot, 0] + w2 * ybuf[slot, 1]
    xo = x1_ref[...] + gt2_ref[0] * moe
    ms = jnp.mean(xo * xo, axis=-1, keepdims=True)
    o_ref[...] = xo * lax.rsqrt(ms + NORM_EPS) * fg_ref[...]


def _moe_combine(x1, route, dest, yb, mod3, final_g, s, tm):
    n, d = x1.shape
    tiles_per_batch = s // tm
    grid_spec = pltpu.PrefetchScalarGridSpec(
        num_scalar_prefetch=1,
        grid=(n // tm,),
        in_specs=[pl.BlockSpec((tm, d), lambda i, ds_: (i, 0)),
                  pl.BlockSpec((tm, LANES), lambda i, ds_: (i, 0)),
                  pl.BlockSpec((1, 1, d), lambda i, ds_: ((i // tiles_per_batch) * 6 + 5, 0, 0)),
                  pl.BlockSpec((1, d), lambda i, ds_: (0, 0)),
                  pl.BlockSpec(memory_space=pl.ANY)],
        out_specs=pl.BlockSpec((tm, d), lambda i, ds_: (i, 0)),
        scratch_shapes=[pltpu.VMEM((2, TOP_K, tm, d), F32),
                        pltpu.SemaphoreType.DMA((2,))],
    )
    return pl.pallas_call(
        _moe_combine_kernel,
        out_shape=jax.ShapeDtypeStruct((n, d), F32),
        grid_spec=grid_spec,
        compiler_params=_cparams(("arbitrary",)),
        name="moe_combine",
    )(dest, x1, route, mod3, final_g.reshape(1, d), yb)


def _pick(n, candidates):
    for t in candidates:
        if n % t == 0:
            return t
    raise ValueError(f"no tile in {candidates} divides {n}")


def kernel(x, c, ada_w, ada_b, norm1_g, w_in, hg_lb, hg_norm_g, rw_mu, rw_w0, rw_w2, rw_a0, rw_a2, rw_g2, rw_kk, rw_ka, rw_rk, rw_gn_w, rw_gn_b, w_proj_a, w_proj_b, w_out, norm2_g, router_g_w, router_g_b, router_e_w, router_e_b, exp_w_gate, exp_w_up, exp_w_down, final_g):
    bsz, s, d = x.shape
    depth = ada_w.shape[0]
    hg_f = hg_lb.shape[-1]
    hg_w = hg_norm_g.shape[-1]
    rw_w = rw_w0.shape[-1]
    rw_cols = rw_mu.shape[-1]
    n_groups = router_g_w.shape[-1]
    n_experts = router_e_w.shape[-1]
    assert hg_f == hg_w and s % CHUNK == 0 and n_experts + n_groups <= LANES

    lb_all = jnp.cumsum(jax.nn.softmax(hg_lb.astype(F32), axis=0), axis=0)
    n = bsz * s
    blk = 256
    n_blocks = (n * TOP_K + n_experts * blk) // blk
    for l in range(depth):
        mod = _ada_mod(c, ada_w[l], ada_b[l])
        mod3 = mod.reshape(bsz * 6, 1, d)

        hg_cols = 2 * hg_f + 2 * hg_w
        rw_pad = -(-rw_cols // 256) * 256
        wl = w_in[l]
        w_hg = wl[:, :hg_cols].astype(BF16)
        w_rw = jnp.zeros((d, rw_pad), BF16).at[:, :rw_cols].set(wl[:, hg_cols:hg_cols + rw_cols].astype(BF16))
        w_gt = wl[:, hg_cols + rw_cols:].astype(BF16)
        hg, rw, gates = _in_proj(x, mod3, norm1_g[l], w_hg, w_rw, w_gt, _pick(s, (512, 256, 128, 64)))

        o_a = _hgrn2(hg, lb_all[l], hg_norm_g[l], _pick(s, (512, 256, 128, 64)))
        o_b = _rwkv7(rw, rw_mu[l], rw_w0[l], rw_w2[l], rw_a0[l], rw_a2[l], rw_g2[l],
                     rw_kk[l], rw_ka[l], rw_rk[l].reshape(-1), rw_gn_w[l], rw_gn_b[l])

        wr = jnp.zeros((d, LANES), F32).at[:, :n_experts].set(router_e_w[l])
        wr = wr.at[:, n_experts:n_experts + n_groups].set(router_g_w[l])
        br = jnp.zeros((1, LANES), F32).at[0, :n_experts].set(router_e_b[l])
        br = br.at[0, n_experts:n_experts + n_groups].set(router_g_b[l])
        x1, h2, route, counts = _out_proj(
            x, o_a, o_b, gates, mod3, norm2_g[l],
            w_proj_a[l].astype(BF16), w_proj_b[l].astype(BF16), w_out[l].astype(BF16),
            wr, br, n_groups, n_experts, _pick(s, (256, 128, 64)))

        route2 = route.reshape(n, LANES)
        eid = route2[:, 2:4].astype(jnp.int32)
        rank = route2[:, 4:6].astype(jnp.int32)
        cnt = counts[0, :n_experts].astype(jnp.int32)
        padded = (cnt + blk - 1) // blk * blk
        pad_end = jnp.cumsum(padded)
        pad_start = pad_end - padded
        dest = pad_start[eid] + rank
        n_used = (pad_end[-1] // blk).astype(jnp.int32).reshape(1)
        blk_e = jnp.minimum(jnp.searchsorted(pad_end, jnp.arange(n_blocks, dtype=jnp.int32) * blk, side='right'),
                            n_experts - 1).astype(jnp.int32)
        tok = jnp.broadcast_to(jnp.arange(n, dtype=jnp.int32)[:, None], (n, TOP_K))
        slot_tok = jnp.zeros((n_blocks * blk,), jnp.int32).at[dest.reshape(-1)].set(tok.reshape(-1))

        yb = _moe_experts(h2.reshape(n, d), slot_tok, blk_e, n_used,
                          exp_w_gate[l], exp_w_up[l], exp_w_down[l], blk)
        last = l == depth - 1
        assert last, "the final RMSNorm is fused into the last layer's combine"
        out = _moe_combine(x1.reshape(n, d), route2, dest.reshape(-1), yb, mod3, final_g,
                           s, _pick(s, (256, 128, 64)))
        x = out.reshape(bsz, s, d)
    return x
```

```python
import functools

import numpy as np
import jax
import jax.numpy as jnp
from jax import lax
from jax.experimental import pallas as pl
from jax.experimental.pallas import tpu as pltpu

F32 = jnp.float32
BF16 = jnp.bfloat16
HIGHEST = lax.Precision.HIGHEST

NORM_EPS = 1e-6
HG_HEAD = 128
RW_HEAD = 64
RW_GN_EPS = 64e-5
TOP_K = 2
CHUNK = 64
LANES = 128
VMEM_LIMIT = 56 * 1024 * 1024

NT = (((1,), (1,)), ((), ()))
TN = (((0,), (0,)), ((), ()))


def _dot(a, b, dims=None, precision=None):
    if dims is None:
        return jnp.dot(a, b, preferred_element_type=F32, precision=precision)
    return lax.dot_general(a, b, dims, preferred_element_type=F32, precision=precision)


def _split3(x):
    hi = x.astype(BF16)
    r1 = x - hi.astype(F32)
    mid = r1.astype(BF16)
    lo = (r1 - mid.astype(F32)).astype(BF16)
    return hi, mid, lo


def _dot_exact_lhs(m_bf16, x):
    hi, mid, lo = _split3(x)
    return _dot(m_bf16, hi) + _dot(m_bf16, mid) + _dot(m_bf16, lo)


def _dot_exact_rhs(x, m_bf16):
    hi, mid, lo = _split3(x)
    return _dot(hi, m_bf16) + _dot(mid, m_bf16) + _dot(lo, m_bf16)


def _sigmoid(x):
    return 1.0 / (1.0 + jnp.exp(-x))


def _silu(x):
    return x * _sigmoid(x)


def _cparams(sem):
    return pltpu.CompilerParams(dimension_semantics=sem, vmem_limit_bytes=VMEM_LIMIT)


def _ada_kernel(c_ref, w_ref, b_ref, o_ref):
    c = c_ref[...]
    o_ref[...] = _dot(_silu(c), w_ref[...], precision=HIGHEST) + b_ref[...]


def _ada_mod(c, w, b):
    bsz, d = c.shape
    n = w.shape[1]
    rows = 8
    cp = jnp.zeros((rows, d), F32).at[:bsz].set(c)
    tn = 1536
    out = pl.pallas_call(
        _ada_kernel,
        out_shape=jax.ShapeDtypeStruct((rows, n), F32),
        grid=(n // tn,),
        in_specs=[pl.BlockSpec((rows, d), lambda j: (0, 0)),
                  pl.BlockSpec((d, tn), lambda j: (0, j)),
                  pl.BlockSpec((1, tn), lambda j: (0, j))],
        out_specs=pl.BlockSpec((rows, tn), lambda j: (0, j)),
        compiler_params=_cparams(("arbitrary",)),
        name="ada_mod",
    )(cp, w, b.reshape(1, n))
    return out[:bsz]


def _in_proj_kernel(x_ref, sh_ref, sc_ref, g_ref, whg_ref, wrw_ref, wgt_ref, hg_ref, rw_ref, gt_ref):
    x = x_ref[0]
    ms = jnp.mean(x * x, axis=-1, keepdims=True)
    h = (x * lax.rsqrt(ms + NORM_EPS) * g_ref[...]) * (1.0 + sc_ref[0]) + sh_ref[0]
    hb = h.astype(BF16)
    step = 512
    for n0 in range(0, whg_ref.shape[1], step):
        hg_ref[0, :, n0:n0 + step] = _dot(hb, whg_ref[:, n0:n0 + step])
    for n0 in range(0, wrw_ref.shape[1], 256):
        rw_ref[0, :, n0:n0 + 256] = _dot(hb, wrw_ref[:, n0:n0 + 256])
    for n0 in range(0, wgt_ref.shape[1], step):
        gt_ref[0, :, n0:n0 + step] = _sigmoid(_dot(hb, wgt_ref[:, n0:n0 + step])).astype(BF16)


def _in_proj(x, mod3, norm_g, w_hg, w_rw, w_gt, tm):
    bsz, s, d = x.shape
    n_hg, n_rw, n_gt = w_hg.shape[1], w_rw.shape[1], w_gt.shape[1]
    const = lambda b, i: (0, 0)
    return pl.pallas_call(
        _in_proj_kernel,
        out_shape=(jax.ShapeDtypeStruct((bsz, s, n_hg), F32),
                   jax.ShapeDtypeStruct((bsz, s, n_rw), F32),
                   jax.ShapeDtypeStruct((bsz, s, n_gt), BF16)),
        grid=(bsz, s // tm),
        in_specs=[pl.BlockSpec((1, tm, d), lambda b, i: (b, i, 0)),
                  pl.BlockSpec((1, 1, d), lambda b, i: (b * 6 + 0, 0, 0)),
                  pl.BlockSpec((1, 1, d), lambda b, i: (b * 6 + 1, 0, 0)),
                  pl.BlockSpec((1, d), const),
                  pl.BlockSpec((d, n_hg), const),
                  pl.BlockSpec((d, n_rw), const),
                  pl.BlockSpec((d, n_gt), const)],
        out_specs=(pl.BlockSpec((1, tm, n_hg), lambda b, i: (b, i, 0)),
                   pl.BlockSpec((1, tm, n_rw), lambda b, i: (b, i, 0)),
                   pl.BlockSpec((1, tm, n_gt), lambda b, i: (b, i, 0))),
        compiler_params=_cparams(("arbitrary", "arbitrary")),
        name="in_proj",
    )(x, mod3, mod3, norm_g.reshape(1, d), w_hg, w_rw, w_gt)


_HG_LEVELS = (32, 16, 8, 4, 2, 1)


def _hgrn2_consts(width):
    c = CHUNK
    t = np.arange(c)[:, None]
    s = np.arange(c)[None, :]
    blocks = [(s <= t), (s > t)]
    lvl_masks = []
    right = []
    for h in _HG_LEVELS:
        m = (t // (2 * h)) * 2 * h + h
        is_r = (t & h) != 0
        blk = np.where(is_r, (s >= m) & (s <= t), (s > t) & (s <= m - 1))
        blocks.append(blk)
        lvl_masks.append(is_r & ((s & h) == 0) & ((t // (2 * h)) == (s // (2 * h))))
        right.append(np.broadcast_to(is_r, (c, width)))
    mst = np.concatenate(blocks, axis=0).astype(np.float32)
    lm = np.stack([np.eye(c, dtype=bool)] + lvl_masks).astype(np.float32)
    rm = np.stack(right).astype(np.float32)
    return jnp.asarray(mst, BF16), jnp.asarray(lm, F32), jnp.asarray(rm, F32)


def _hgrn2_kernel(q_ref, f_ref, i_ref, g_ref, lb_ref, ng_ref, mst_ref, lm_ref, rm_ref, o_ref, st_ref):
    c = CHUNK
    n_heads = q_ref.shape[2] // HG_HEAD
    n_chunks = q_ref.shape[1] // c

    @pl.when(pl.program_id(1) == 0)
    def _():
        st_ref[...] = jnp.zeros_like(st_ref)

    mst = mst_ref[...]
    lb = lb_ref[...]
    ng = ng_ref[...]
    heads = [slice(hd * HG_HEAD, (hd + 1) * HG_HEAD) for hd in range(n_heads)]

    def chunk_body(ci, carry):
        r0 = pl.multiple_of(ci * c, c)
        rows = pl.ds(r0, c)
        q = _silu(q_ref[0, rows, :])
        f = lb + (1.0 - lb) * _sigmoid(f_ref[0, rows, :])
        lf = jnp.log(f)
        k = 1.0 - f
        vb = i_ref[0, rows, :].astype(BF16)
        ex = jnp.exp(_dot_exact_lhs(mst, lf))
        ex_cum = ex[0:c]
        qd = (q * ex_cum).astype(BF16)
        kr = (k * ex[c:2 * c]).astype(BF16)
        qb = q.astype(BF16)
        kb = k.astype(BF16)
        sts = [st_ref[hd] for hd in range(n_heads)]
        o = [_dot(qd[:, ls], st.astype(BF16), NT) for ls, st in zip(heads, sts)]
        sc = [lm_ref[0] * _dot(qb[:, ls], kb[:, ls], NT) for ls in heads]
        dqk = q - k
        for li in range(len(_HG_LEVELS)):
            g_l = ((k + rm_ref[li] * dqk) * ex[(2 + li) * c:(3 + li) * c]).astype(BF16)
            sc = [s_h + lm_ref[li + 1] * _dot(g_l[:, ls], g_l[:, ls], NT) for s_h, ls in zip(sc, heads)]
        o = [o_h + _dot(s_h.astype(BF16), vb[:, ls]) for o_h, s_h, ls in zip(o, sc, heads)]
        for hd, ls in enumerate(heads):
            st_ref[hd] = sts[hd] * ex_cum[c - 1:c, ls] + _dot(vb[:, ls], kr[:, ls], TN)
        on = [o_h * lax.rsqrt(jnp.mean(o_h * o_h, axis=-1, keepdims=True) + NORM_EPS) for o_h in o]
        o_full = jnp.concatenate(on, axis=1) * ng
        o_ref[0, rows, :] = (o_full * _silu(g_ref[0, rows, :])).astype(o_ref.dtype)
        return carry

    lax.fori_loop(0, n_chunks, chunk_body, 0)


def _hgrn2(hg, lb, norm_g, ts):
    bsz, s, n4 = hg.shape
    w = n4 // 4
    mst, lm, rm = _hgrn2_consts(w)
    n_heads = w // HG_HEAD
    const2 = lambda b, i: (0, 0)
    const3 = lambda b, i: (0, 0, 0)
    return pl.pallas_call(
        _hgrn2_kernel,
        out_shape=jax.ShapeDtypeStruct((bsz, s, w), BF16),
        grid=(bsz, s // ts),
        in_specs=[pl.BlockSpec((1, ts, w), lambda b, i: (b, i, 0)),
                  pl.BlockSpec((1, ts, w), lambda b, i: (b, i, 1)),
                  pl.BlockSpec((1, ts, w), lambda b, i: (b, i, 2)),
                  pl.BlockSpec((1, ts, w), lambda b, i: (b, i, 3)),
                  pl.BlockSpec((1, w), const2),
                  pl.BlockSpec((1, w), const2),
                  pl.BlockSpec(mst.shape, const2),
                  pl.BlockSpec(lm.shape, const3),
                  pl.BlockSpec(rm.shape, const3)],
        out_specs=pl.BlockSpec((1, ts, w), lambda b, i: (b, i, 0)),
        scratch_shapes=[pltpu.VMEM((n_heads, HG_HEAD, HG_HEAD), F32)],
        compiler_params=_cparams(("arbitrary", "arbitrary")),
        name="hgrn2",
    )(hg, hg, hg, hg, lb.reshape(1, w), norm_g.reshape(1, w), mst, lm, rm)


def _rwkv_consts(width):
    c = CHUNK
    t = np.arange(c)[:, None]
    s = np.arange(c)[None, :]
    tri = (s <= t).astype(np.float32)
    tt = np.arange(2 * c)[:, None]
    ss = np.arange(2 * c)[None, :]
    same = (tt // c) == (ss // c)
    strict = same & ((ss % c) < (tt % c))
    incl = same & ((ss % c) <= (tt % c))
    hsum = (np.arange(width)[:, None] // RW_HEAD) == (np.arange(width)[None, :] // RW_HEAD)
    return (jnp.asarray(tri, BF16), jnp.asarray(strict.astype(np.float32), F32),
            jnp.asarray(incl.astype(np.float32), F32), jnp.asarray(hsum.astype(np.float32), BF16))


def _rwkv7_kernel(p_ref, mu_ref, w0_ref, a0_ref, kk_ref, ka_ref, rk_ref, gnw_ref, gnb_ref,
                  w2_ref, a2_ref, g2_ref, tri_ref, sm_ref, im_ref, hs_ref,
                  o_ref, carry_ref, zt_ref):
    c = CHUNK
    nb = p_ref.shape[0]
    width = o_ref.shape[2]
    n_pairs = width // LANES

    @pl.when(pl.program_id(0) == 0)
    def _():
        carry_ref[...] = jnp.zeros_like(carry_ref)
        zt_ref[...] = jnp.zeros_like(zt_ref)

    hs = hs_ref[...]
    tri = tri_ref[...]
    smask = sm_ref[...] > 0
    imask = im_ref[...] > 0
    lane = lax.broadcasted_iota(jnp.int32, (c, LANES), 1)
    m0 = (lane < RW_HEAD).astype(F32)
    m1 = 1.0 - m0

    def stack(x):
        return jnp.concatenate([x * m0, x * m1], axis=0)

    rs, k2s, vs_, gs = [], [], [], []
    units = []
    for b in range(nb):
        p = p_ref[b]
        row = lax.broadcasted_iota(jnp.int32, p.shape, 0)
        prev = jnp.where(row == 0, carry_ref[b], pltpu.roll(p, 1, 0))
        carry_ref[b] = p[c - 1:c, :]
        xs = p + mu_ref[...] * (prev - p)
        r = xs[:, 0:width]
        k = xs[:, width:2 * width]
        v = xs[:, 2 * width:3 * width]
        slab = xs[:, 3 * width:]
        nz = -(w0_ref[...] + _dot(jnp.tanh(slab).astype(BF16), w2_ref[...]))
        softplus = jnp.maximum(nz, 0.0) + jnp.log(1.0 + jnp.exp(-jnp.abs(nz)))
        ld = -jnp.exp(-softplus - 0.5)
        a = _sigmoid(a0_ref[...] + _dot(slab.astype(BF16), a2_ref[...]))
        gs.append(_dot(_sigmoid(slab).astype(BF16), g2_ref[...]))
        kk0 = k * kk_ref[...]
        kk = kk0 * lax.rsqrt(jnp.maximum(_dot_exact_rhs(kk0 * kk0, hs), 1e-24))
        k2 = k * (1.0 + (a - 1.0) * ka_ref[...])
        a_in = -kk
        b_in = kk * a
        cum = _dot_exact_lhs(tri, ld)
        cum_t = cum[c - 1:c, :]
        e_c = jnp.exp(cum)
        e_nc = jnp.exp(-cum)
        e_rem = jnp.exp(cum_t - cum)
        at_f = a_in * jnp.exp(cum - ld)
        rt_f = r * e_c
        kt_f = k2 * e_nc
        bt_f = b_in * e_nc
        kh_f = k2 * e_rem
        bh_f = b_in * e_rem
        p_t = jnp.exp(cum_t)
        rs.append(r)
        k2s.append(k2)
        vs_.append(v)
        for pi in range(n_pairs):
            ls = slice(pi * LANES, (pi + 1) * LANES)
            units.append(dict(
                b=b, pi=pi,
                at=stack(at_f[:, ls]).astype(BF16), rt=stack(rt_f[:, ls]).astype(BF16),
                kt=stack(kt_f[:, ls]).astype(BF16), bt=stack(bt_f[:, ls]).astype(BF16),
                kh=stack(kh_f[:, ls]).astype(BF16), bh=stack(bh_f[:, ls]).astype(BF16),
                vs=stack(v[:, ls]).astype(BF16), p_t=p_t[:, ls]))

    for u in units:
        lhs = jnp.concatenate([u['at'], u['rt']], axis=0)
        u['gk'] = _dot(lhs, u['kt'], NT)
        u['gb'] = _dot(lhs, u['bt'], NT)
    for u in units:
        gk, gb = u.pop('gk'), u.pop('gb')
        a_kk = jnp.concatenate([jnp.where(smask, gk[:2 * c], 0.0), jnp.where(imask, gk[2 * c:], 0.0)], axis=0)
        u['pw'] = jnp.where(smask, gb[:2 * c], 0.0).astype(BF16)
        u['a_rb'] = jnp.where(imask, gb[2 * c:], 0.0).astype(BF16)
        u['a_kk'] = a_kk.astype(BF16)
    for u in units:
        av = _dot(u.pop('a_kk'), u['vs'])
        u['arkv'] = av[2 * c:]
        u['x'] = jnp.concatenate([u['at'].astype(F32), av[:2 * c]], axis=1)
    n_lvl = int(np.log2(c))
    for lvl in range(n_lvl):
        for u in units:
            u['x'] = u['x'] + _dot(u['pw'], u['x'].astype(BF16))
        if lvl + 1 < n_lvl:
            for u in units:
                u['pw'] = _dot(u['pw'], u['pw']).astype(BF16)
    for u in units:
        x = u.pop('x')
        u['zt'] = zt_ref[u['b'], u['pi']]
        u['uy'] = _dot(jnp.concatenate([x[:, :LANES].astype(BF16), u['rt']], axis=0), u['zt'].astype(BF16), NT)
        u['u_loc'] = x[:, LANES:]
    for u in units:
        uy = u.pop('uy')
        u['u'] = (uy[:2 * c] + u.pop('u_loc')).astype(BF16)
        u['y0'] = uy[2 * c:] + u.pop('arkv')
    for u in units:
        u['y'] = u.pop('y0') + _dot(u['a_rb'], u['u'])
        upd = _dot(jnp.concatenate([u['u'], u['vs']], axis=0), jnp.concatenate([u['bh'], u['kh']], axis=0), TN)
        zt_ref[u['b'], u['pi']] = u['zt'] * u['p_t'] + upd

    inv_n = 1.0 / RW_HEAD
    for b in range(nb):
        ys = [u['y'] for u in units if u['b'] == b]
        y = jnp.concatenate([yy[:c] + yy[c:] for yy in ys], axis=1)
        mean = _dot_exact_rhs(y, hs) * inv_n
        d = y - mean
        var = _dot_exact_rhs(d * d, hs) * inv_n
        yn = d * lax.rsqrt(var + RW_GN_EPS) * gnw_ref[...] + gnb_ref[...]
        bonus = _dot_exact_rhs(rs[b] * k2s[b] * rk_ref[...], hs) * vs_[b]
        o_ref[b] = ((yn + bonus) * gs[b]).astype(o_ref.dtype)


def _rwkv7(rw, mu, w0, w2, a0, a2, g2, k_k, k_a, r_k, gn_w, gn_b):
    bsz, s, cols = rw.shape
    width = w0.shape[-1]
    n_pairs = width // LANES
    slab = cols - 3 * width
    dl, al, gl = w2.shape[0], a2.shape[0], g2.shape[0]
    w2f = jnp.zeros((slab, width), F32).at[0:dl].set(w2).astype(BF16)
    a2f = jnp.zeros((slab, width), F32).at[dl:dl + al].set(a2).astype(BF16)
    g2f = jnp.zeros((slab, width), F32).at[dl + al:dl + al + gl].set(g2).astype(BF16)
    mup = jnp.zeros((1, cols), F32).at[0, :mu.shape[-1]].set(mu)
    tri, sm, im, hs = _rwkv_consts(width)
    row = lambda x: x.reshape(1, width)
    const = lambda i: (0, 0)
    vec = pl.BlockSpec((1, width), const)
    return pl.pallas_call(
        _rwkv7_kernel,
        out_shape=jax.ShapeDtypeStruct((bsz, s, width), BF16),
        grid=(s // CHUNK,),
        in_specs=[pl.BlockSpec((bsz, CHUNK, cols), lambda i: (0, i, 0)),
                  pl.BlockSpec((1, cols), const),
                  vec, vec, vec, vec, vec, vec, vec,
                  pl.BlockSpec((slab, width), const),
                  pl.BlockSpec((slab, width), const),
                  pl.BlockSpec((slab, width), const),
                  pl.BlockSpec(tri.shape, const),
                  pl.BlockSpec(sm.shape, const),
                  pl.BlockSpec(im.shape, const),
                  pl.BlockSpec(hs.shape, const)],
        out_specs=pl.BlockSpec((bsz, CHUNK, width), lambda i: (0, i, 0)),
        scratch_shapes=[pltpu.VMEM((bsz, 1, cols), F32),
                        pltpu.VMEM((bsz, n_pairs, LANES, LANES), F32)],
        compiler_params=_cparams(("arbitrary",)),
        name="rwkv7",
    )(rw, mup, row(w0), row(a0), row(k_k), row(k_a), row(r_k), row(gn_w), row(gn_b),
      w2f, a2f, g2f, tri, sm, im, hs)


def _out_proj_kernel(n_groups, n_experts,
                     x_ref, oa_ref, ob_ref, ga_ref, gb_ref, gt1_ref, sc2_ref, sh2_ref, g2_ref,
                     wa_ref, wb_ref, wo_ref, wr_ref, br_ref, tril_ref,
                     x1_ref, h2_ref, route_ref, cnt_ref, carry_ref):
    tm = x_ref.shape[1]
    first = (pl.program_id(0) == 0) & (pl.program_id(1) == 0)

    @pl.when(first)
    def _():
        carry_ref[...] = jnp.zeros_like(carry_ref)

    pa = _dot(oa_ref[0], wa_ref[...])
    pb = _dot(ob_ref[0], wb_ref[...])
    mixed = ga_ref[0].astype(F32) * pa + gb_ref[0].astype(F32) * pb
    x1 = x_ref[0] + gt1_ref[0] * _dot(mixed.astype(BF16), wo_ref[...])
    x1_ref[0] = x1
    ms = jnp.mean(x1 * x1, axis=-1, keepdims=True)
    h2 = (x1 * lax.rsqrt(ms + NORM_EPS) * g2_ref[...]) * (1.0 + sc2_ref[0]) + sh2_ref[0]
    h2_ref[0] = h2

    logits = _dot(h2, wr_ref[...], precision=HIGHEST) + br_ref[...]
    lane = lax.broadcasted_iota(jnp.int32, logits.shape, 1)
    neg = jnp.float32(-jnp.inf)
    big = jnp.int32(1 << 20)
    eg = n_experts // n_groups
    is_g = (lane >= n_experts) & (lane < n_experts + n_groups)
    lg = jnp.where(is_g, logits, neg)
    mg = jnp.max(lg, axis=-1, keepdims=True)
    p_grp = 1.0 / jnp.sum(jnp.where(is_g, jnp.exp(lg - mg), 0.0), axis=-1, keepdims=True)
    gidx = jnp.min(jnp.where(lg == mg, lane, big), axis=-1, keepdims=True) - n_experts
    sel = (lane >= gidx * eg) & (lane < gidx * eg + eg)
    le = jnp.where(sel, logits, neg)
    me = jnp.max(le, axis=-1, keepdims=True)
    pe_un = jnp.where(sel, jnp.exp(le - me), 0.0)
    pe = jnp.where(sel, pe_un / jnp.sum(pe_un, axis=-1, keepdims=True), -1.0)
    v1 = jnp.max(pe, axis=-1, keepdims=True)
    i1 = jnp.min(jnp.where(pe == v1, lane, big), axis=-1, keepdims=True)
    pe2 = jnp.where(lane == i1, -1.0, pe)
    v2 = jnp.max(pe2, axis=-1, keepdims=True)
    i2 = jnp.min(jnp.where(pe2 == v2, lane, big), axis=-1, keepdims=True)
    wsum = v1 + v2
    w1 = p_grp * v1 / wsum
    w2 = p_grp * v2 / wsum

    oh1 = (lane == i1).astype(F32)
    oh2 = (lane == i2).astype(F32)
    both = oh1 + oh2
    before = _dot(tril_ref[...], both.astype(BF16)) + carry_ref[...]
    rank1 = jnp.sum(oh1 * before, axis=-1, keepdims=True)
    rank2 = jnp.sum(oh2 * before, axis=-1, keepdims=True)
    carry_ref[...] = carry_ref[...] + jnp.sum(both, axis=0, keepdims=True)
    cnt_ref[...] = carry_ref[...]

    out = jnp.where(lane == 0, w1, 0.0)
    out = jnp.where(lane == 1, w2, out)
    out = jnp.where(lane == 2, i1.astype(F32), out)
    out = jnp.where(lane == 3, i2.astype(F32), out)
    out = jnp.where(lane == 4, rank1, out)
    out = jnp.where(lane == 5, rank2, out)
    route_ref[0] = out


def _out_proj(x, o_a, o_b, gates, mod3, norm2_g, wa, wb, wo, wr, br, n_groups, n_experts, tm):
    bsz, s, d = x.shape
    wdt = o_a.shape[-1]
    tril = jnp.asarray(np.tril(np.ones((tm, tm), np.float32), -1), BF16)
    const = lambda b, i: (0, 0)
    tile = lambda b, i: (b, i, 0)
    kern = functools.partial(_out_proj_kernel, n_groups, n_experts)
    return pl.pallas_call(
        kern,
        out_shape=(jax.ShapeDtypeStruct((bsz, s, d), F32),
                   jax.ShapeDtypeStruct((bsz, s, d), F32),
                   jax.ShapeDtypeStruct((bsz, s, LANES), F32),
                   jax.ShapeDtypeStruct((1, LANES), F32)),
        grid=(bsz, s // tm),
        in_specs=[pl.BlockSpec((1, tm, d), tile),
                  pl.BlockSpec((1, tm, wdt), tile),
                  pl.BlockSpec((1, tm, wdt), tile),
                  pl.BlockSpec((1, tm, d), lambda b, i: (b, i, 0)),
                  pl.BlockSpec((1, tm, d), lambda b, i: (b, i, 1)),
                  pl.BlockSpec((1, 1, d), lambda b, i: (b * 6 + 2, 0, 0)),
                  pl.BlockSpec((1, 1, d), lambda b, i: (b * 6 + 4, 0, 0)),
                  pl.BlockSpec((1, 1, d), lambda b, i: (b * 6 + 3, 0, 0)),
                  pl.BlockSpec((1, d), const),
                  pl.BlockSpec(wa.shape, const),
                  pl.BlockSpec(wb.shape, const),
                  pl.BlockSpec(wo.shape, const),
                  pl.BlockSpec(wr.shape, const),
                  pl.BlockSpec((1, LANES), const),
                  pl.BlockSpec((tm, tm), const)],
        out_specs=(pl.BlockSpec((1, tm, d), tile),
                   pl.BlockSpec((1, tm, d), tile),
                   pl.BlockSpec((1, tm, LANES), tile),
                   pl.BlockSpec((1, LANES), const)),
        scratch_shapes=[pltpu.VMEM((1, LANES), F32)],
        compiler_params=_cparams(("arbitrary", "arbitrary")),
        name="out_proj",
    )(x, o_a, o_b, gates, gates, mod3, mod3, mod3, norm2_g.reshape(1, d), wa, wb, wo, wr, br, tril)


def _moe_expert_kernel(blk_e_ref, nused_ref, tok_ref, h_ref, wg_ref, wu_ref, wd_ref, y_ref,
                       xbuf, wgb, wub, wdb, sem):
    i = pl.program_id(0)
    blk = y_ref.shape[0]
    n_used = nused_ref[0]

    def row_copy(b, r, slot):
        tok = tok_ref[b * blk + r]
        return pltpu.make_async_copy(h_ref.at[pl.ds(tok, 1), :], xbuf.at[slot, pl.ds(r, 1), :], sem.at[slot])

    def issue(b, slot):
        def body(r, carry):
            row_copy(b, r, slot).start()
            return carry
        lax.fori_loop(0, blk, body, 0)

    def drain(b, slot):
        def body(r, carry):
            row_copy(b, r, slot).wait()
            return carry
        lax.fori_loop(0, blk, body, 0)

    @pl.when((i == 0) & (n_used > 0))
    def _():
        issue(0, 0)

    @pl.when(i + 1 < n_used)
    def _():
        issue(i + 1, (i + 1) % 2)

    new_expert = (i == 0) | (blk_e_ref[i] != blk_e_ref[jnp.maximum(i - 1, 0)])

    @pl.when((i < n_used) & new_expert)
    def _():
        wgb[...] = wg_ref[0].astype(BF16)
        wub[...] = wu_ref[0].astype(BF16)
        wdb[...] = wd_ref[0].astype(BF16)

    @pl.when(i < n_used)
    def _():
        slot = i % 2
        drain(i, slot)
        xb = xbuf[slot].astype(BF16)
        hg = _dot(xb, wgb[...])
        hu = _dot(xb, wub[...])
        hid = (_silu(hg) * hu).astype(BF16)
        y_ref[...] = _dot(hid, wdb[...])

    @pl.when(i >= n_used)
    def _():
        y_ref[...] = jnp.zeros_like(y_ref)


def _moe_experts(h2, slot_tok, blk_e, n_used, w_gate, w_up, w_down, blk):
    n, d = h2.shape
    p = slot_tok.shape[0]
    nb = p // blk
    f = w_gate.shape[-1]
    grid_spec = pltpu.PrefetchScalarGridSpec(
        num_scalar_prefetch=3,
        grid=(nb,),
        in_specs=[pl.BlockSpec(memory_space=pl.ANY),
                  pl.BlockSpec((1, d, f), lambda i, be, nu, tk: (be[i], 0, 0)),
                  pl.BlockSpec((1, d, f), lambda i, be, nu, tk: (be[i], 0, 0)),
                  pl.BlockSpec((1, f, d), lambda i, be, nu, tk: (be[i], 0, 0))],
        out_specs=pl.BlockSpec((blk, d), lambda i, be, nu, tk: (i, 0)),
        scratch_shapes=[pltpu.VMEM((2, blk, d), F32),
                        pltpu.VMEM((d, f), BF16),
                        pltpu.VMEM((d, f), BF16),
                        pltpu.VMEM((f, d), BF16),
                        pltpu.SemaphoreType.DMA((2,))],
    )
    return pl.pallas_call(
        _moe_expert_kernel,
        out_shape=jax.ShapeDtypeStruct((p, d), F32),
        grid_spec=grid_spec,
        compiler_params=_cparams(("arbitrary",)),
        name="moe_experts",
    )(blk_e, n_used, slot_tok, h2, w_gate, w_up, w_down)


def _moe_combine_kernel(dest_ref, x1_ref, route_ref, gt2_ref, fg_ref, y_ref, o_ref, ybuf, sem):
    i = pl.program_id(0)
    n_steps = pl.num_programs(0)
    tm = x1_ref.shape[0]

    def row_copy(step, r, kslot, slot):
        d = dest_ref[(step * tm + r) * TOP_K + kslot]
        return pltpu.make_async_copy(y_ref.at[pl.ds(d, 1), :], ybuf.at[slot, kslot, pl.ds(r, 1), :], sem.at[slot])

    def issue(step, slot):
        def body(r, carry):
            for kslot in range(TOP_K):
                row_copy(step, r, kslot, slot).start()
            return carry
        lax.fori_loop(0, tm, body, 0)

    def drain(step, slot):
        def body(r, carry):
            for kslot in range(TOP_K):
                row_copy(step, r, kslot, slot).wait()
            return carry
        lax.fori_loop(0, tm, body, 0)

    @pl.when(i == 0)
    def _():
        issue(0, 0)

    @pl.when(i + 1 < n_steps)
    def _():
        issue(i + 1, (i + 1) % 2)

    slot = i % 2
    drain(i, slot)
    route = route_ref[...]
    w1 = route[:, 0:1]
    w2 = route[:, 1:2]
    moe = w1 * ybuf[slot, 0] + w2 * ybuf[slot, 1]
    xo = x1_ref[...] + gt2_ref[0] * moe
    ms = jnp.mean(xo * xo, axis=-1, keepdims=True)
    o_ref[...] = xo * lax.rsqrt(ms + NORM_EPS) * fg_ref[...]


def _moe_combine(x1, route, dest, yb, mod3, final_g, s, tm):
    n, d = x1.shape
    tiles_per_batch = s // tm
    grid_spec = pltpu.PrefetchScalarGridSpec(
        num_scalar_prefetch=1,
        grid=(n // tm,),
        in_specs=[pl.BlockSpec((tm, d), lambda i, ds_: (i, 0)),
                  pl.BlockSpec((tm, LANES), lambda i, ds_: (i, 0)),
                  pl.BlockSpec((1, 1, d), lambda i, ds_: ((i // tiles_per_batch) * 6 + 5, 0, 0)),
                  pl.BlockSpec((1, d), lambda i, ds_: (0, 0)),
                  pl.BlockSpec(memory_space=pl.ANY)],
        out_specs=pl.BlockSpec((tm, d), lambda i, ds_: (i, 0)),
        scratch_shapes=[pltpu.VMEM((2, TOP_K, tm, d), F32),
                        pltpu.SemaphoreType.DMA((2,))],
    )
    return pl.pallas_call(
        _moe_combine_kernel,
        out_shape=jax.ShapeDtypeStruct((n, d), F32),
        grid_spec=grid_spec,
        compiler_params=_cparams(("arbitrary",)),
        name="moe_combine",
    )(dest, x1, route, mod3, final_g.reshape(1, d), yb)


def _pick(n, candidates):
    for t in candidates:
        if n % t == 0:
            return t
    raise ValueError(f"no tile in {candidates} divides {n}")


def kernel(x, c, ada_w, ada_b, norm1_g, w_in, hg_lb, hg_norm_g, rw_mu, rw_w0, rw_w2, rw_a0, rw_a2, rw_g2, rw_kk, rw_ka, rw_rk, rw_gn_w, rw_gn_b, w_proj_a, w_proj_b, w_out, norm2_g, router_g_w, router_g_b, router_e_w, router_e_b, exp_w_gate, exp_w_up, exp_w_down, final_g):
    bsz, s, d = x.shape
    depth = ada_w.shape[0]
    hg_f = hg_lb.shape[-1]
    hg_w = hg_norm_g.shape[-1]
    rw_w = rw_w0.shape[-1]
    rw_cols = rw_mu.shape[-1]
    n_groups = router_g_w.shape[-1]
    n_experts = router_e_w.shape[-1]
    assert hg_f == hg_w and s % CHUNK == 0 and n_experts + n_groups <= LANES

    lb_all = jnp.cumsum(jax.nn.softmax(hg_lb.astype(F32), axis=0), axis=0)
    n = bsz * s
    blk = 256
    n_blocks = (n * TOP_K + n_experts * blk) // blk
    for l in range(depth):
        mod = _ada_mod(c, ada_w[l], ada_b[l])
        mod3 = mod.reshape(bsz * 6, 1, d)

        hg_cols = 2 * hg_f + 2 * hg_w
        rw_pad = -(-rw_cols // 256) * 256
        wl = w_in[l]
        w_hg = wl[:, :hg_cols].astype(BF16)
        w_rw = jnp.zeros((d, rw_pad), BF16).at[:, :rw_cols].set(wl[:, hg_cols:hg_cols + rw_cols].astype(BF16))
        w_gt = wl[:, hg_cols + rw_cols:].astype(BF16)
        hg, rw, gates = _in_proj(x, mod3, norm1_g[l], w_hg, w_rw, w_gt, _pick(s, (512, 256, 128, 64)))

        o_a = _hgrn2(hg, lb_all[l], hg_norm_g[l], _pick(s, (512, 256, 128, 64)))
        o_b = _rwkv7(rw, rw_mu[l], rw_w0[l], rw_w2[l], rw_a0[l], rw_a2[l], rw_g2[l],
                     rw_kk[l], rw_ka[l], rw_rk[l].reshape(-1), rw_gn_w[l], rw_gn_b[l])

        wr = jnp.zeros((d, LANES), F32).at[:, :n_experts].set(router_e_w[l])
        wr = wr.at[:, n_experts:n_experts + n_groups].set(router_g_w[l])
        br = jnp.zeros((1, LANES), F32).at[0, :n_experts].set(router_e_b[l])
        br = br.at[0, n_experts:n_experts + n_groups].set(router_g_b[l])
        x1, h2, route, counts = _out_proj(
            x, o_a, o_b, gates, mod3, norm2_g[l],
            w_proj_a[l].astype(BF16), w_proj_b[l].astype(BF16), w_out[l].astype(BF16),
            wr, br, n_groups, n_experts, _pick(s, (256, 128, 64)))

        route2 = route.reshape(n, LANES)
        eid = route2[:, 2:4].astype(jnp.int32)
        rank = route2[:, 4:6].astype(jnp.int32)
        cnt = counts[0, :n_experts].astype(jnp.int32)
        padded = (cnt + blk - 1) // blk * blk
        pad_end = jnp.cumsum(padded)
        pad_start = pad_end - padded
        dest = pad_start[eid] + rank
        n_used = (pad_end[-1] // blk).astype(jnp.int32).reshape(1)
        blk_e = jnp.minimum(jnp.searchsorted(pad_end, jnp.arange(n_blocks, dtype=jnp.int32) * blk, side='right'),
                            n_experts - 1).astype(jnp.int32)
        tok = jnp.broadcast_to(jnp.arange(n, dtype=jnp.int32)[:, None], (n, TOP_K))
        slot_tok = jnp.zeros((n_blocks * blk,), jnp.int32).at[dest.reshape(-1)].set(tok.reshape(-1))

        yb = _moe_experts(h2.reshape(n, d), slot_tok, blk_e, n_used,
                          exp_w_gate[l], exp_w_up[l], exp_w_down[l], blk)
        last = l == depth - 1
        assert last, "the final RMSNorm is fused into the last layer's combine"
        out = _moe_combine(x1.reshape(n, d), route2, dest.reshape(-1), yb, mod3, final_g,
                           s, _pick(s, (256, 128, 64)))
        x = out.reshape(bsz, s, d)
    return x
```

```python
import functools

import numpy as np
import jax
import jax.numpy as jnp
from jax import lax
from jax.experimental import pallas as pl
from jax.experimental.pallas import tpu as pltpu

F32 = jnp.float32
BF16 = jnp.bfloat16
HIGHEST = lax.Precision.HIGHEST

NORM_EPS = 1e-6
HG_HEAD = 128
RW_HEAD = 64
RW_GN_EPS = 64e-5
TOP_K = 2
CHUNK = 64
LANES = 128
VMEM_LIMIT = 56 * 1024 * 1024

NT = (((1,), (1,)), ((), ()))
TN = (((0,), (0,)), ((), ()))


def _dot(a, b, dims=None, precision=None):
    if dims is None:
        return jnp.dot(a, b, preferred_element_type=F32, precision=precision)
    return lax.dot_general(a, b, dims, preferred_element_type=F32, precision=precision)


def _split3(x):
    hi = x.astype(BF16)
    r1 = x - hi.astype(F32)
    mid = r1.astype(BF16)
    lo = (r1 - mid.astype(F32)).astype(BF16)
    return hi, mid, lo


def _dot_exact_lhs(m_bf16, x):
    hi, mid, lo = _split3(x)
    return _dot(m_bf16, hi) + _dot(m_bf16, mid) + _dot(m_bf16, lo)


def _dot_exact_rhs(x, m_bf16):
    hi, mid, lo = _split3(x)
    return _dot(hi, m_bf16) + _dot(mid, m_bf16) + _dot(lo, m_bf16)


def _sigmoid(x):
    return 1.0 / (1.0 + jnp.exp(-x))


def _silu(x):
    return x * _sigmoid(x)


def _cparams(sem):
    return pltpu.CompilerParams(dimension_semantics=sem, vmem_limit_bytes=VMEM_LIMIT)


def _ada_kernel(c_ref, w_ref, b_ref, o_ref):
    c = c_ref[...]
    o_ref[...] = _dot(_silu(c), w_ref[...], precision=HIGHEST) + b_ref[...]


def _ada_mod(c, w, b):
    bsz, d = c.shape
    n = w.shape[1]
    rows = 8
    cp = jnp.zeros((rows, d), F32).at[:bsz].set(c)
    tn = 1536
    out = pl.pallas_call(
        _ada_kernel,
        out_shape=jax.ShapeDtypeStruct((rows, n), F32),
        grid=(n // tn,),
        in_specs=[pl.BlockSpec((rows, d), lambda j: (0, 0)),
                  pl.BlockSpec((d, tn), lambda j: (0, j)),
                  pl.BlockSpec((1, tn), lambda j: (0, j))],
        out_specs=pl.BlockSpec((rows, tn), lambda j: (0, j)),
        compiler_params=_cparams(("arbitrary",)),
        name="ada_mod",
    )(cp, w, b.reshape(1, n))
    return out[:bsz]


def _in_proj_kernel(x_ref, sh_ref, sc_ref, g_ref, whg_ref, wrw_ref, wgt_ref, hg_ref, rw_ref, gt_ref):
    x = x_ref[0]
    ms = jnp.mean(x * x, axis=-1, keepdims=True)
    h = (x * lax.rsqrt(ms + NORM_EPS) * g_ref[...]) * (1.0 + sc_ref[0]) + sh_ref[0]
    hb = h.astype(BF16)
    step = 512
    for n0 in range(0, whg_ref.shape[1], step):
        hg_ref[0, :, n0:n0 + step] = _dot(hb, whg_ref[:, n0:n0 + step])
    for n0 in range(0, wrw_ref.shape[1], 256):
        rw_ref[0, :, n0:n0 + 256] = _dot(hb, wrw_ref[:, n0:n0 + 256])
    for n0 in range(0, wgt_ref.shape[1], step):
        gt_ref[0, :, n0:n0 + step] = _sigmoid(_dot(hb, wgt_ref[:, n0:n0 + step])).astype(BF16)


def _in_proj(x, mod3, norm_g, w_hg, w_rw, w_gt, tm):
    bsz, s, d = x.shape
    n_hg, n_rw, n_gt = w_hg.shape[1], w_rw.shape[1], w_gt.shape[1]
    const = lambda b, i: (0, 0)
    return pl.pallas_call(
        _in_proj_kernel,
        out_shape=(jax.ShapeDtypeStruct((bsz, s, n_hg), F32),
                   jax.ShapeDtypeStruct((bsz, s, n_rw), F32),
                   jax.ShapeDtypeStruct((bsz, s, n_gt), BF16)),
        grid=(bsz, s // tm),
        in_specs=[pl.BlockSpec((1, tm, d), lambda b, i: (b, i, 0)),
                  pl.BlockSpec((1, 1, d), lambda b, i: (b * 6 + 0, 0, 0)),
                  pl.BlockSpec((1, 1, d), lambda b, i: (b * 6 + 1, 0, 0)),
                  pl.BlockSpec((1, d), const),
                  pl.BlockSpec((d, n_hg), const),
                  pl.BlockSpec((d, n_rw), const),
                  pl.BlockSpec((d, n_gt), const)],
        out_specs=(pl.BlockSpec((1, tm, n_hg), lambda b, i: (b, i, 0)),
                   pl.BlockSpec((1, tm, n_rw), lambda b, i: (b, i, 0)),
                   pl.BlockSpec((1, tm, n_gt), lambda b, i: (b, i, 0))),
        compiler_params=_cparams(("arbitrary", "arbitrary")),
        name="in_proj",
    )(x, mod3, mod3, norm_g.reshape(1, d), w_hg, w_rw, w_gt)


_HG_LEVELS = (32, 16, 8, 4, 2, 1)


def _hgrn2_consts(width):
    c = CHUNK
    t = np.arange(c)[:, None]
    s = np.arange(c)[None, :]
    blocks = [(s <= t), (s > t)]
    lvl_masks = []
    right = []
    for h in _HG_LEVELS:
        m = (t // (2 * h)) * 2 * h + h
        is_r = (t & h) != 0
        blk = np.where(is_r, (s >= m) & (s <= t), (s > t) & (s <= m - 1))
        blocks.append(blk)
        lvl_masks.append(is_r & ((s & h) == 0) & ((t // (2 * h)) == (s // (2 * h))))
        right.append(np.broadcast_to(is_r, (c, width)))
    mst = np.concatenate(blocks, axis=0).astype(np.float32)
    lm = np.stack([np.eye(c, dtype=bool)] + lvl_masks).astype(np.float32)
    rm = np.stack(right).astype(np.float32)
    return jnp.asarray(mst, BF16), jnp.asarray(lm, F32), jnp.asarray(rm, F32)


def _hgrn2_kernel(q_ref, f_ref, i_ref, g_ref, lb_ref, ng_ref, mst_ref, lm_ref, rm_ref, o_ref, st_ref):
    c = CHUNK
    n_heads = q_ref.shape[2] // HG_HEAD
    n_chunks = q_ref.shape[1] // c

    @pl.when(pl.program_id(1) == 0)
    def _():
        st_ref[...] = jnp.zeros_like(st_ref)

    mst = mst_ref[...]
    lb = lb_ref[...]
    ng = ng_ref[...]
    heads = [slice(hd * HG_HEAD, (hd + 1) * HG_HEAD) for hd in range(n_heads)]

    def chunk_body(ci, carry):
        r0 = pl.multiple_of(ci * c, c)
        rows = pl.ds(r0, c)
        q = _silu(q_ref[0, rows, :])
        f = lb + (1.0 - lb) * _sigmoid(f_ref[0, rows, :])
        lf = jnp.log(f)
        k = 1.0 - f
        vb = i_ref[0, rows, :].astype(BF16)
        ex = jnp.exp(_dot_exact_lhs(mst, lf))
        ex_cum = ex[0:c]
        qd = (q * ex_cum).astype(BF16)
        kr = (k * ex[c:2 * c]).astype(BF16)
        qb = q.astype(BF16)
        kb = k.astype(BF16)
        sts = [st_ref[hd] for hd in range(n_heads)]
        o = [_dot(qd[:, ls], st.astype(BF16), NT) for ls, st in zip(heads, sts)]
        sc = [lm_ref[0] * _dot(qb[:, ls], kb[:, ls], NT) for ls in heads]
        dqk = q - k
        for li in range(len(_HG_LEVELS)):
            g_l = ((k + rm_ref[li] * dqk) * ex[(2 + li) * c:(3 + li) * c]).astype(BF16)
            sc = [s_h + lm_ref[li + 1] * _dot(g_l[:, ls], g_l[:, ls], NT) for s_h, ls in zip(sc, heads)]
        o = [o_h + _dot(s_h.astype(BF16), vb[:, ls]) for o_h, s_h, ls in zip(o, sc, heads)]
        for hd, ls in enumerate(heads):
            st_ref[hd] = sts[hd] * ex_cum[c - 1:c, ls] + _dot(vb[:, ls], kr[:, ls], TN)
        on = [o_h * lax.rsqrt(jnp.mean(o_h * o_h, axis=-1, keepdims=True) + NORM_EPS) for o_h in o]
        o_full = jnp.concatenate(on, axis=1) * ng
        o_ref[0, rows, :] = (o_full * _silu(g_ref[0, rows, :])).astype(o_ref.dtype)
        return carry

    lax.fori_loop(0, n_chunks, chunk_body, 0)


def _hgrn2(hg, lb, norm_g, ts):
    bsz, s, n4 = hg.shape
    w = n4 // 4
    mst, lm, rm = _hgrn2_consts(w)
    n_heads = w // HG_HEAD
    const2 = lambda b, i: (0, 0)
    const3 = lambda b, i: (0, 0, 0)
    return pl.pallas_call(
        _hgrn2_kernel,
        out_shape=jax.ShapeDtypeStruct((bsz, s, w), BF16),
        grid=(bsz, s // ts),
        in_specs=[pl.BlockSpec((1, ts, w), lambda b, i: (b, i, 0)),
                  pl.BlockSpec((1, ts, w), lambda b, i: (b, i, 1)),
                  pl.BlockSpec((1, ts, w), lambda b, i: (b, i, 2)),
                  pl.BlockSpec((1, ts, w), lambda b, i: (b, i, 3)),
                  pl.BlockSpec((1, w), const2),
                  pl.BlockSpec((1, w), const2),
                  pl.BlockSpec(mst.shape, const2),
                  pl.BlockSpec(lm.shape, const3),
                  pl.BlockSpec(rm.shape, const3)],
        out_specs=pl.BlockSpec((1, ts, w), lambda b, i: (b, i, 0)),
        scratch_shapes=[pltpu.VMEM((n_heads, HG_HEAD, HG_HEAD), F32)],
        compiler_params=_cparams(("arbitrary", "arbitrary")),
        name="hgrn2",
    )(hg, hg, hg, hg, lb.reshape(1, w), norm_g.reshape(1, w), mst, lm, rm)


def _rwkv_consts(width):
    c = CHUNK
    t = np.arange(c)[:, None]
    s = np.arange(c)[None, :]
    tri = (s <= t).astype(np.float32)
    tt = np.arange(2 * c)[:, None]
    ss = np.arange(2 * c)[None, :]
    same = (tt // c) == (ss // c)
    strict = same & ((ss % c) < (tt % c))
    incl = same & ((ss % c) <= (tt % c))
    hsum = (np.arange(width)[:, None] // RW_HEAD) == (np.arange(width)[None, :] // RW_HEAD)
    return (jnp.asarray(tri, BF16), jnp.asarray(strict.astype(np.float32), F32),
            jnp.asarray(incl.astype(np.float32), F32), jnp.asarray(hsum.astype(np.float32), BF16))


def _rwkv7_kernel(p_ref, mu_ref, w0_ref, a0_ref, kk_ref, ka_ref, rk_ref, gnw_ref, gnb_ref,
                  w2_ref, a2_ref, g2_ref, tri_ref, sm_ref, im_ref, hs_ref,
                  o_ref, carry_ref, zt_ref):
    c = CHUNK
    nb = p_ref.shape[0]
    width = o_ref.shape[2]
    n_pairs = width // LANES

    @pl.when(pl.program_id(0) == 0)
    def _():
        carry_ref[...] = jnp.zeros_like(carry_ref)
        zt_ref[...] = jnp.zeros_like(zt_ref)

    hs = hs_ref[...]
    tri = tri_ref[...]
    smask = sm_ref[...] > 0
    imask = im_ref[...] > 0
    lane = lax.broadcasted_iota(jnp.int32, (c, LANES), 1)
    m0 = (lane < RW_HEAD).astype(F32)
    m1 = 1.0 - m0

    def stack(x):
        return jnp.concatenate([x * m0, x * m1], axis=0)

    rs, k2s, vs_, gs = [], [], [], []
    units = []
    for b in range(nb):
        p = p_ref[b]
        row = lax.broadcasted_iota(jnp.int32, p.shape, 0)
        prev = jnp.where(row == 0, carry_ref[b], pltpu.roll(p, 1, 0))
        carry_ref[b] = p[c - 1:c, :]
        xs = p + mu_ref[...] * (prev - p)
        r = xs[:, 0:width]
        k = xs[:, width:2 * width]
        v = xs[:, 2 * width:3 * width]
        slab = xs[:, 3 * width:]
        nz = -(w0_ref[...] + _dot(jnp.tanh(slab).astype(BF16), w2_ref[...]))
        softplus = jnp.maximum(nz, 0.0) + jnp.log(1.0 + jnp.exp(-jnp.abs(nz)))
        ld = -jnp.exp(-softplus - 0.5)
        a = _sigmoid(a0_ref[...] + _dot(slab.astype(BF16), a2_ref[...]))
        gs.append(_dot(_sigmoid(slab).astype(BF16), g2_ref[...]))
        kk0 = k * kk_ref[...]
        kk = kk0 * lax.rsqrt(jnp.maximum(_dot_exact_rhs(kk0 * kk0, hs), 1e-24))
        k2 = k * (1.0 + (a - 1.0) * ka_ref[...])
        a_in = -kk
        b_in = kk * a
        cum = _dot_exact_lhs(tri, ld)
        cum_t = cum[c - 1:c, :]
        e_c = jnp.exp(cum)
        e_nc = jnp.exp(-cum)
        e_rem = jnp.exp(cum_t - cum)
        at_f = a_in * jnp.exp(cum - ld)
        rt_f = r * e_c
        kt_f = k2 * e_nc
        bt_f = b_in * e_nc
        kh_f = k2 * e_rem
        bh_f = b_in * e_rem
        p_t = jnp.exp(cum_t)
        rs.append(r)
        k2s.append(k2)
        vs_.append(v)
        for pi in range(n_pairs):
            ls = slice(pi * LANES, (pi + 1) * LANES)
            units.append(dict(
                b=b, pi=pi,
                at=stack(at_f[:, ls]).astype(BF16), rt=stack(rt_f[:, ls]).astype(BF16),
                kt=stack(kt_f[:, ls]).astype(BF16), bt=stack(bt_f[:, ls]).astype(BF16),
                kh=stack(kh_f[:, ls]).astype(BF16), bh=stack(bh_f[:, ls]).astype(BF16),
                vs=stack(v[:, ls]).astype(BF16), p_t=p_t[:, ls]))

    for u in units:
        lhs = jnp.concatenate([u['at'], u['rt']], axis=0)
        u['gk'] = _dot(lhs, u['kt'], NT)
        u['gb'] = _dot(lhs, u['bt'], NT)
    for u in units:
        gk, gb = u.pop('gk'), u.pop('gb')
        a_kk = jnp.concatenate([jnp.where(smask, gk[:2 * c], 0.0), jnp.where(imask, gk[2 * c:], 0.0)], axis=0)
        u['pw'] = jnp.where(smask, gb[:2 * c], 0.0).astype(BF16)
        u['a_rb'] = jnp.where(imask, gb[2 * c:], 0.0).astype(BF16)
        u['a_kk'] = a_kk.astype(BF16)
    for u in units:
        av = _dot(u.pop('a_kk'), u['vs'])
        u['arkv'] = av[2 * c:]
        u['x'] = jnp.concatenate([u['at'].astype(F32), av[:2 * c]], axis=1)
    n_lvl = int(np.log2(c))
    for lvl in range(n_lvl):
        for u in units:
            u['x'] = u['x'] + _dot(u['pw'], u['x'].astype(BF16))
        if lvl + 1 < n_lvl:
            for u in units:
                u['pw'] = _dot(u['pw'], u['pw']).astype(BF16)
    for u in units:
        x = u.pop('x')
        u['zt'] = zt_ref[u['b'], u['pi']]
        u['uy'] = _dot(jnp.concatenate([x[:, :LANES].astype(BF16), u['rt']], axis=0), u['zt'].astype(BF16), NT)
        u['u_loc'] = x[:, LANES:]
    for u in units:
        uy = u.pop('uy')
        u['u'] = (uy[:2 * c] + u.pop('u_loc')).astype(BF16)
        u['y0'] = uy[2 * c:] + u.pop('arkv')
    for u in units:
        u['y'] = u.pop('y0') + _dot(u['a_rb'], u['u'])
        upd = _dot(jnp.concatenate([u['u'], u['vs']], axis=0), jnp.concatenate([u['bh'], u['kh']], axis=0), TN)
        zt_ref[u['b'], u['pi']] = u['zt'] * u['p_t'] + upd

    inv_n = 1.0 / RW_HEAD
    for b in range(nb):
        ys = [u['y'] for u in units if u['b'] == b]
        y = jnp.concatenate([yy[:c] + yy[c:] for yy in ys], axis=1)
        mean = _dot_exact_rhs(y, hs) * inv_n
        d = y - mean
        var = _dot_exact_rhs(d * d, hs) * inv_n
        yn = d * lax.rsqrt(var + RW_GN_EPS) * gnw_ref[...] + gnb_ref[...]
        bonus = _dot_exact_rhs(rs[b] * k2s[b] * rk_ref[...], hs) * vs_[b]
        o_ref[b] = ((yn + bonus) * gs[b]).astype(o_ref.dtype)


def _rwkv7(rw, mu, w0, w2, a0, a2, g2, k_k, k_a, r_k, gn_w, gn_b):
    bsz, s, cols = rw.shape
    width = w0.shape[-1]
    n_pairs = width // LANES
    slab = cols - 3 * width
    dl, al, gl = w2.shape[0], a2.shape[0], g2.shape[0]
    w2f = jnp.zeros((slab, width), F32).at[0:dl].set(w2).astype(BF16)
    a2f = jnp.zeros((slab, width), F32).at[dl:dl + al].set(a2).astype(BF16)
    g2f = jnp.zeros((slab, width), F32).at[dl + al:dl + al + gl].set(g2).astype(BF16)
    mup = jnp.zeros((1, cols), F32).at[0, :mu.shape[-1]].set(mu)
    tri, sm, im, hs = _rwkv_consts(width)
    row = lambda x: x.reshape(1, width)
    const = lambda i: (0, 0)
    vec = pl.BlockSpec((1, width), const)
    return pl.pallas_call(
        _rwkv7_kernel,
        out_shape=jax.ShapeDtypeStruct((bsz, s, width), BF16),
        grid=(s // CHUNK,),
        in_specs=[pl.BlockSpec((bsz, CHUNK, cols), lambda i: (0, i, 0)),
                  pl.BlockSpec((1, cols), const),
                  vec, vec, vec, vec, vec, vec, vec,
                  pl.BlockSpec((slab, width), const),
                  pl.BlockSpec((slab, width), const),
                  pl.BlockSpec((slab, width), const),
                  pl.BlockSpec(tri.shape, const),
                  pl.BlockSpec(sm.shape, const),
                  pl.BlockSpec(im.shape, const),
                  pl.BlockSpec(hs.shape, const)],
        out_specs=pl.BlockSpec((bsz, CHUNK, width), lambda i: (0, i, 0)),
        scratch_shapes=[pltpu.VMEM((bsz, 1, cols), F32),
                        pltpu.VMEM((bsz, n_pairs, LANES, LANES), F32)],
        compiler_params=_cparams(("arbitrary",)),
        name="rwkv7",
    )(rw, mup, row(w0), row(a0), row(k_k), row(k_a), row(r_k), row(gn_w), row(gn_b),
      w2f, a2f, g2f, tri, sm, im, hs)


def _out_proj_kernel(n_groups, n_experts,
                     x_ref, oa_ref, ob_ref, ga_ref, gb_ref, gt1_ref, sc2_ref, sh2_ref, g2_ref,
                     wa_ref, wb_ref, wo_ref, wr_ref, br_ref, tril_ref,
                     x1_ref, h2_ref, route_ref, cnt_ref, carry_ref):
    tm = x_ref.shape[1]
    first = (pl.program_id(0) == 0) & (pl.program_id(1) == 0)

    @pl.when(first)
    def _():
        carry_ref[...] = jnp.zeros_like(carry_ref)

    pa = _dot(oa_ref[0], wa_ref[...])
    pb = _dot(ob_ref[0], wb_ref[...])
    mixed = ga_ref[0].astype(F32) * pa + gb_ref[0].astype(F32) * pb
    x1 = x_ref[0] + gt1_ref[0] * _dot(mixed.astype(BF16), wo_ref[...])
    x1_ref[0] = x1
    ms = jnp.mean(x1 * x1, axis=-1, keepdims=True)
    h2 = (x1 * lax.rsqrt(ms + NORM_EPS) * g2_ref[...]) * (1.0 + sc2_ref[0]) + sh2_ref[0]
    h2_ref[0] = h2

    logits = _dot(h2, wr_ref[...], precision=HIGHEST) + br_ref[...]
    lane = lax.broadcasted_iota(jnp.int32, logits.shape, 1)
    neg = jnp.float32(-jnp.inf)
    big = jnp.int32(1 << 20)
    eg = n_experts // n_groups
    is_g = (lane >= n_experts) & (lane < n_experts + n_groups)
    lg = jnp.where(is_g, logits, neg)
    mg = jnp.max(lg, axis=-1, keepdims=True)
    p_grp = 1.0 / jnp.sum(jnp.where(is_g, jnp.exp(lg - mg), 0.0), axis=-1, keepdims=True)
    gidx = jnp.min(jnp.where(lg == mg, lane, big), axis=-1, keepdims=True) - n_experts
    sel = (lane >= gidx * eg) & (lane < gidx * eg + eg)
    le = jnp.where(sel, logits, neg)
    me = jnp.max(le, axis=-1, keepdims=True)
    pe_un = jnp.where(sel, jnp.exp(le - me), 0.0)
    pe = jnp.where(sel, pe_un / jnp.sum(pe_un, axis=-1, keepdims=True), -1.0)
    v1 = jnp.max(pe, axis=-1, keepdims=True)
    i1 = jnp.min(jnp.where(pe == v1, lane, big), axis=-1, keepdims=True)
    pe2 = jnp.where(lane == i1, -1.0, pe)
    v2 = jnp.max(pe2, axis=-1, keepdims=True)
    i2 = jnp.min(jnp.where(pe2 == v2, lane, big), axis=-1, keepdims=True)
    wsum = v1 + v2
    w1 = p_grp * v1 / wsum
    w2 = p_grp * v2 / wsum

    oh1 = (lane == i1).astype(F32)
    oh2 = (lane == i2).astype(F32)
    both = oh1 + oh2
    before = _dot(tril_ref[...], both.astype(BF16)) + carry_ref[...]
    rank1 = jnp.sum(oh1 * before, axis=-1, keepdims=True)
    rank2 = jnp.sum(oh2 * before, axis=-1, keepdims=True)
    carry_ref[...] = carry_ref[...] + jnp.sum(both, axis=0, keepdims=True)
    cnt_ref[...] = carry_ref[...]

    out = jnp.where(lane == 0, w1, 0.0)
    out = jnp.where(lane == 1, w2, out)
    out = jnp.where(lane == 2, i1.astype(F32), out)
    out = jnp.where(lane == 3, i2.astype(F32), out)
    out = jnp.where(lane == 4, rank1, out)
    out = jnp.where(lane == 5, rank2, out)
    route_ref[0] = out


def _out_proj(x, o_a, o_b, gates, mod3, norm2_g, wa, wb, wo, wr, br, n_groups, n_experts, tm):
    bsz, s, d = x.shape
    wdt = o_a.shape[-1]
    tril = jnp.asarray(np.tril(np.ones((tm, tm), np.float32), -1), BF16)
    const = lambda b, i: (0, 0)
    tile = lambda b, i: (b, i, 0)
    kern = functools.partial(_out_proj_kernel, n_groups, n_experts)
    return pl.pallas_call(
        kern,
        out_shape=(jax.ShapeDtypeStruct((bsz, s, d), F32),
                   jax.ShapeDtypeStruct((bsz, s, d), F32),
                   jax.ShapeDtypeStruct((bsz, s, LANES), F32),
                   jax.ShapeDtypeStruct((1, LANES), F32)),
        grid=(bsz, s // tm),
        in_specs=[pl.BlockSpec((1, tm, d), tile),
                  pl.BlockSpec((1, tm, wdt), tile),
                  pl.BlockSpec((1, tm, wdt), tile),
                  pl.BlockSpec((1, tm, d), lambda b, i: (b, i, 0)),
                  pl.BlockSpec((1, tm, d), lambda b, i: (b, i, 1)),
                  pl.BlockSpec((1, 1, d), lambda b, i: (b * 6 + 2, 0, 0)),
                  pl.BlockSpec((1, 1, d), lambda b, i: (b * 6 + 4, 0, 0)),
                  pl.BlockSpec((1, 1, d), lambda b, i: (b * 6 + 3, 0, 0)),
                  pl.BlockSpec((1, d), const),
                  pl.BlockSpec(wa.shape, const),
                  pl.BlockSpec(wb.shape, const),
                  pl.BlockSpec(wo.shape, const),
                  pl.BlockSpec(wr.shape, const),
                  pl.BlockSpec((1, LANES), const),
                  pl.BlockSpec((tm, tm), const)],
        out_specs=(pl.BlockSpec((1, tm, d), tile),
                   pl.BlockSpec((1, tm, d), tile),
                   pl.BlockSpec((1, tm, LANES), tile),
                   pl.BlockSpec((1, LANES), const)),
        scratch_shapes=[pltpu.VMEM((1, LANES), F32)],
        compiler_params=_cparams(("arbitrary", "arbitrary")),
        name="out_proj",
    )(x, o_a, o_b, gates, gates, mod3, mod3, mod3, norm2_g.reshape(1, d), wa, wb, wo, wr, br, tril)


def _moe_expert_kernel(blk_e_ref, tok_ref, h_ref, wg_ref, wu_ref, wd_ref, y_ref,
                       xbuf, wgb, wub, wdb, sem):
    i = pl.program_id(0)
    nb = pl.num_programs(0)
    blk = y_ref.shape[0]

    def issue(b, slot, rows):
        for r in rows:
            tok = tok_ref[b * blk + r]
            pltpu.make_async_copy(h_ref.at[pl.ds(tok, 1), :], xbuf.at[slot, pl.ds(r, 1), :], sem.at[slot]).start()

    def wait_block(slot):
        pltpu.make_async_copy(h_ref.at[pl.ds(0, blk), :], xbuf.at[slot], sem.at[slot]).wait()

    @pl.when(i == 0)
    def _():
        issue(0, 0, range(blk))

    new_expert = (i == 0) | (blk_e_ref[i] != blk_e_ref[jnp.maximum(i - 1, 0)])

    @pl.when(new_expert)
    def _():
        wgb[...] = wg_ref[0].astype(BF16)
        wub[...] = wu_ref[0].astype(BF16)
        wdb[...] = wd_ref[0].astype(BF16)

    slot = i % 2
    nslot = (i + 1) % 2
    nxt = jnp.minimum(i + 1, nb - 1)
    q = blk // 4
    wait_block(slot)
    xb = xbuf[slot].astype(BF16)
    issue(nxt, nslot, range(0, q))
    hg = _dot(xb, wgb[...])
    issue(nxt, nslot, range(q, 2 * q))
    hu = _dot(xb, wub[...])
    issue(nxt, nslot, range(2 * q, 3 * q))
    hid = (_silu(hg) * hu).astype(BF16)
    y = _dot(hid, wdb[...])
    issue(nxt, nslot, range(3 * q, blk))
    y_ref[...] = y

    @pl.when(i == nb - 1)
    def _():
        wait_block(nslot)


def _moe_experts(h2, slot_tok, blk_e, w_gate, w_up, w_down, blk):
    n, d = h2.shape
    p = slot_tok.shape[0]
    nb = p // blk
    f = w_gate.shape[-1]
    grid_spec = pltpu.PrefetchScalarGridSpec(
        num_scalar_prefetch=2,
        grid=(nb,),
        in_specs=[pl.BlockSpec(memory_space=pl.ANY),
                  pl.BlockSpec((1, d, f), lambda i, be, tk: (be[i], 0, 0)),
                  pl.BlockSpec((1, d, f), lambda i, be, tk: (be[i], 0, 0)),
                  pl.BlockSpec((1, f, d), lambda i, be, tk: (be[i], 0, 0))],
        out_specs=pl.BlockSpec((blk, d), lambda i, be, tk: (i, 0)),
        scratch_shapes=[pltpu.VMEM((2, blk, d), F32),
                        pltpu.VMEM((d, f), BF16),
                        pltpu.VMEM((d, f), BF16),
                        pltpu.VMEM((f, d), BF16),
                        pltpu.SemaphoreType.DMA((2,))],
    )
    return pl.pallas_call(
        _moe_expert_kernel,
        out_shape=jax.ShapeDtypeStruct((p, d), F32),
        grid_spec=grid_spec,
        compiler_params=_cparams(("arbitrary",)),
        name="moe_experts",
    )(blk_e, slot_tok, h2, w_gate, w_up, w_down)


def _moe_combine_kernel(dest_ref, x1_ref, route_ref, gt2_ref, fg_ref, y_ref, o_ref, ybuf, sem):
    i = pl.program_id(0)
    n_steps = pl.num_programs(0)
    tm = x1_ref.shape[0]

    def issue(step, slot, rows):
        for r in rows:
            for kslot in range(TOP_K):
                d = dest_ref[(step * tm + r) * TOP_K + kslot]
                pltpu.make_async_copy(y_ref.at[pl.ds(d, 1), :], ybuf.at[slot, pl.ds(kslot * tm + r, 1), :],
                                      sem.at[slot]).start()

    def wait_tile(slot):
        pltpu.make_async_copy(y_ref.at[pl.ds(0, TOP_K * tm), :], ybuf.at[slot], sem.at[slot]).wait()

    @pl.when(i == 0)
    def _():
        issue(0, 0, range(tm))

    slot = i % 2
    nslot = (i + 1) % 2
    nxt = jnp.minimum(i + 1, n_steps - 1)
    wait_tile(slot)
    issue(nxt, nslot, range(0, tm // 2))
    route = route_ref[...]
    w1 = route[:, 0:1]
    w2 = route[:, 1:2]
    moe = w1 * ybuf[slot, 0:tm, :] + w2 * ybuf[slot, tm:2 * tm, :]
    xo = x1_ref[...] + gt2_ref[0] * moe
    ms = jnp.mean(xo * xo, axis=-1, keepdims=True)
    out = xo * lax.rsqrt(ms + NORM_EPS) * fg_ref[...]
    issue(nxt, nslot, range(tm // 2, tm))
    o_ref[...] = out

    @pl.when(i == n_steps - 1)
    def _():
        wait_tile(nslot)


def _moe_combine(x1, route, dest, yb, mod3, final_g, s, tm):
    n, d = x1.shape
    tiles_per_batch = s // tm
    grid_spec = pltpu.PrefetchScalarGridSpec(
        num_scalar_prefetch=1,
        grid=(n // tm,),
        in_specs=[pl.BlockSpec((tm, d), lambda i, ds_: (i, 0)),
                  pl.BlockSpec((tm, LANES), lambda i, ds_: (i, 0)),
                  pl.BlockSpec((1, 1, d), lambda i, ds_: ((i // tiles_per_batch) * 6 + 5, 0, 0)),
                  pl.BlockSpec((1, d), lambda i, ds_: (0, 0)),
                  pl.BlockSpec(memory_space=pl.ANY)],
        out_specs=pl.BlockSpec((tm, d), lambda i, ds_: (i, 0)),
        scratch_shapes=[pltpu.VMEM((2, TOP_K * tm, d), F32),
                        pltpu.SemaphoreType.DMA((2,))],
    )
    return pl.pallas_call(
        _moe_combine_kernel,
        out_shape=jax.ShapeDtypeStruct((n, d), F32),
        grid_spec=grid_spec,
        compiler_params=_cparams(("arbitrary",)),
        name="moe_combine",
    )(dest, x1, route, mod3, final_g.reshape(1, d), yb)


def _pick(n, candidates):
    for t in candidates:
        if n % t == 0:
            return t
    raise ValueError(f"no tile in {candidates} divides {n}")


def kernel(x, c, ada_w, ada_b, norm1_g, w_in, hg_lb, hg_norm_g, rw_mu, rw_w0, rw_w2, rw_a0, rw_a2, rw_g2, rw_kk, rw_ka, rw_rk, rw_gn_w, rw_gn_b, w_proj_a, w_proj_b, w_out, norm2_g, router_g_w, router_g_b, router_e_w, router_e_b, exp_w_gate, exp_w_up, exp_w_down, final_g):
    bsz, s, d = x.shape
    depth = ada_w.shape[0]
    hg_f = hg_lb.shape[-1]
    hg_w = hg_norm_g.shape[-1]
    rw_w = rw_w0.shape[-1]
    rw_cols = rw_mu.shape[-1]
    n_groups = router_g_w.shape[-1]
    n_experts = router_e_w.shape[-1]
    assert hg_f == hg_w and s % CHUNK == 0 and n_experts + n_groups <= LANES

    lb_all = jnp.cumsum(jax.nn.softmax(hg_lb.astype(F32), axis=0), axis=0)
    n = bsz * s
    blk = 256
    n_blocks = (n * TOP_K + n_experts * blk) // blk
    for l in range(depth):
        mod = _ada_mod(c, ada_w[l], ada_b[l])
        mod3 = mod.reshape(bsz * 6, 1, d)

        hg_cols = 2 * hg_f + 2 * hg_w
        rw_pad = -(-rw_cols // 256) * 256
        wl = w_in[l]
        w_hg = wl[:, :hg_cols].astype(BF16)
        w_rw = jnp.zeros((d, rw_pad), BF16).at[:, :rw_cols].set(wl[:, hg_cols:hg_cols + rw_cols].astype(BF16))
        w_gt = wl[:, hg_cols + rw_cols:].astype(BF16)
        hg, rw, gates = _in_proj(x, mod3, norm1_g[l], w_hg, w_rw, w_gt, _pick(s, (512, 256, 128, 64)))

        o_a = _hgrn2(hg, lb_all[l], hg_norm_g[l], _pick(s, (512, 256, 128, 64)))
        o_b = _rwkv7(rw, rw_mu[l], rw_w0[l], rw_w2[l], rw_a0[l], rw_a2[l], rw_g2[l],
                     rw_kk[l], rw_ka[l], rw_rk[l].reshape(-1), rw_gn_w[l], rw_gn_b[l])

        wr = jnp.zeros((d, LANES), F32).at[:, :n_experts].set(router_e_w[l])
        wr = wr.at[:, n_experts:n_experts + n_groups].set(router_g_w[l])
        br = jnp.zeros((1, LANES), F32).at[0, :n_experts].set(router_e_b[l])
        br = br.at[0, n_experts:n_experts + n_groups].set(router_g_b[l])
        x1, h2, route, counts = _out_proj(
            x, o_a, o_b, gates, mod3, norm2_g[l],
            w_proj_a[l].astype(BF16), w_proj_b[l].astype(BF16), w_out[l].astype(BF16),
            wr, br, n_groups, n_experts, _pick(s, (256, 128, 64)))

        route2 = route.reshape(n, LANES)
        eid = route2[:, 2:4].astype(jnp.int32)
        rank = route2[:, 4:6].astype(jnp.int32)
        cnt = counts[0, :n_experts].astype(jnp.int32)
        padded = (cnt + blk - 1) // blk * blk
        pad_end = jnp.cumsum(padded)
        pad_start = pad_end - padded
        dest = pad_start[eid] + rank
        blk_start = jnp.arange(n_blocks, dtype=jnp.int32) * blk
        blk_e = jnp.minimum(jnp.sum((pad_end[None, :] <= blk_start[:, None]).astype(jnp.int32), axis=1),
                            n_experts - 1)
        tok = jnp.broadcast_to(jnp.arange(n, dtype=jnp.int32)[:, None], (n, TOP_K))
        slot_tok = jnp.zeros((n_blocks * blk,), jnp.int32).at[dest.reshape(-1)].set(tok.reshape(-1))

        yb = _moe_experts(h2.reshape(n, d), slot_tok, blk_e,
                          exp_w_gate[l], exp_w_up[l], exp_w_down[l], blk)
        last = l == depth - 1
        assert last, "the final RMSNorm is fused into the last layer's combine"
        out = _moe_combine(x1.reshape(n, d), route2, dest.reshape(-1), yb, mod3, final_g,
                           s, _pick(s, (256, 128, 64)))
        x = out.reshape(bsz, s, d)
    return x
```

```python
import functools

import numpy as np
import jax
import jax.numpy as jnp
from jax import lax
from jax.experimental import pallas as pl
from jax.experimental.pallas import tpu as pltpu

F32 = jnp.float32
BF16 = jnp.bfloat16
HIGHEST = lax.Precision.HIGHEST

NORM_EPS = 1e-6
HG_HEAD = 128
RW_HEAD = 64
RW_GN_EPS = 64e-5
TOP_K = 2
CHUNK = 64
LANES = 128
SUB = 8
VMEM_LIMIT = 56 * 1024 * 1024

NT = (((1,), (1,)), ((), ()))
TN = (((0,), (0,)), ((), ()))


def _dot(a, b, dims=None, precision=None):
    if dims is None:
        return jnp.dot(a, b, preferred_element_type=F32, precision=precision)
    return lax.dot_general(a, b, dims, preferred_element_type=F32, precision=precision)


def _split3(x):
    hi = x.astype(BF16)
    r1 = x - hi.astype(F32)
    mid = r1.astype(BF16)
    lo = (r1 - mid.astype(F32)).astype(BF16)
    return hi, mid, lo


def _dot_exact_lhs(m_bf16, x):
    hi, mid, lo = _split3(x)
    return _dot(m_bf16, hi) + _dot(m_bf16, mid) + _dot(m_bf16, lo)


def _dot_exact_rhs(x, m_bf16):
    hi, mid, lo = _split3(x)
    return _dot(hi, m_bf16) + _dot(mid, m_bf16) + _dot(lo, m_bf16)


def _sigmoid(x):
    return 1.0 / (1.0 + jnp.exp(-x))


def _silu(x):
    return x * _sigmoid(x)


def _rows_to_tiles(ref, val):
    m = val.shape[0]
    for j in range(SUB):
        ref[pl.ds(j, m, stride=SUB), :] = val[:, j * LANES:(j + 1) * LANES]


def _tiles_to_rows(ref, m, base=0):
    return jnp.concatenate([ref[pl.ds(base * SUB + j, m, stride=SUB), :] for j in range(SUB)], axis=1)


def _cparams(sem):
    return pltpu.CompilerParams(dimension_semantics=sem, vmem_limit_bytes=VMEM_LIMIT)


def _ada_kernel(c_ref, w_ref, b_ref, o_ref):
    c = c_ref[...]
    o_ref[...] = _dot(_silu(c), w_ref[...], precision=HIGHEST) + b_ref[...]


def _ada_mod(c, w, b):
    bsz, d = c.shape
    n = w.shape[1]
    rows = 8
    cp = jnp.zeros((rows, d), F32).at[:bsz].set(c)
    tn = 1536
    out = pl.pallas_call(
        _ada_kernel,
        out_shape=jax.ShapeDtypeStruct((rows, n), F32),
        grid=(n // tn,),
        in_specs=[pl.BlockSpec((rows, d), lambda j: (0, 0)),
                  pl.BlockSpec((d, tn), lambda j: (0, j)),
                  pl.BlockSpec((1, tn), lambda j: (0, j))],
        out_specs=pl.BlockSpec((rows, tn), lambda j: (0, j)),
        compiler_params=_cparams(("arbitrary",)),
        name="ada_mod",
    )(cp, w, b.reshape(1, n))
    return out[:bsz]


def _in_proj_kernel(x_ref, sh_ref, sc_ref, g_ref, whg_ref, wrw_ref, wgt_ref, hg_ref, rw_ref, gt_ref):
    x = x_ref[0]
    ms = jnp.mean(x * x, axis=-1, keepdims=True)
    h = (x * lax.rsqrt(ms + NORM_EPS) * g_ref[...]) * (1.0 + sc_ref[0]) + sh_ref[0]
    hb = h.astype(BF16)
    step = 512
    for n0 in range(0, whg_ref.shape[1], step):
        hg_ref[0, :, n0:n0 + step] = _dot(hb, whg_ref[:, n0:n0 + step])
    for n0 in range(0, wrw_ref.shape[1], 256):
        rw_ref[0, :, n0:n0 + 256] = _dot(hb, wrw_ref[:, n0:n0 + 256])
    for n0 in range(0, wgt_ref.shape[1], step):
        gt_ref[0, :, n0:n0 + step] = _sigmoid(_dot(hb, wgt_ref[:, n0:n0 + step])).astype(BF16)


def _in_proj(x, mod3, norm_g, w_hg, w_rw, w_gt, tm):
    bsz, s, d = x.shape
    n_hg, n_rw, n_gt = w_hg.shape[1], w_rw.shape[1], w_gt.shape[1]
    const = lambda b, i: (0, 0)
    return pl.pallas_call(
        _in_proj_kernel,
        out_shape=(jax.ShapeDtypeStruct((bsz, s, n_hg), F32),
                   jax.ShapeDtypeStruct((bsz, s, n_rw), F32),
                   jax.ShapeDtypeStruct((bsz, s, n_gt), BF16)),
        grid=(bsz, s // tm),
        in_specs=[pl.BlockSpec((1, tm, d), lambda b, i: (b, i, 0)),
                  pl.BlockSpec((1, 1, d), lambda b, i: (b * 6 + 0, 0, 0)),
                  pl.BlockSpec((1, 1, d), lambda b, i: (b * 6 + 1, 0, 0)),
                  pl.BlockSpec((1, d), const),
                  pl.BlockSpec((d, n_hg), const),
                  pl.BlockSpec((d, n_rw), const),
                  pl.BlockSpec((d, n_gt), const)],
        out_specs=(pl.BlockSpec((1, tm, n_hg), lambda b, i: (b, i, 0)),
                   pl.BlockSpec((1, tm, n_rw), lambda b, i: (b, i, 0)),
                   pl.BlockSpec((1, tm, n_gt), lambda b, i: (b, i, 0))),
        compiler_params=_cparams(("arbitrary", "arbitrary")),
        name="in_proj",
    )(x, mod3, mod3, norm_g.reshape(1, d), w_hg, w_rw, w_gt)


_HG_LEVELS = (32, 16, 8, 4, 2, 1)


def _hgrn2_consts(width):
    c = CHUNK
    t = np.arange(c)[:, None]
    s = np.arange(c)[None, :]
    blocks = [(s <= t), (s > t)]
    lvl_masks = []
    right = []
    for h in _HG_LEVELS:
        m = (t // (2 * h)) * 2 * h + h
        is_r = (t & h) != 0
        blk = np.where(is_r, (s >= m) & (s <= t), (s > t) & (s <= m - 1))
        blocks.append(blk)
        lvl_masks.append(is_r & ((s & h) == 0) & ((t // (2 * h)) == (s // (2 * h))))
        right.append(np.broadcast_to(is_r, (c, width)))
    mst = np.concatenate(blocks, axis=0).astype(np.float32)
    lm = np.stack([np.eye(c, dtype=bool)] + lvl_masks).astype(np.float32)
    rm = np.stack(right).astype(np.float32)
    return jnp.asarray(mst, BF16), jnp.asarray(lm, F32), jnp.asarray(rm, F32)


def _hgrn2_kernel(q_ref, f_ref, i_ref, g_ref, lb_ref, ng_ref, mst_ref, lm_ref, rm_ref, o_ref, st_ref):
    c = CHUNK
    n_heads = q_ref.shape[2] // HG_HEAD
    n_chunks = q_ref.shape[1] // c

    @pl.when(pl.program_id(1) == 0)
    def _():
        st_ref[...] = jnp.zeros_like(st_ref)

    mst = mst_ref[...]
    lb = lb_ref[...]
    ng = ng_ref[...]
    heads = [slice(hd * HG_HEAD, (hd + 1) * HG_HEAD) for hd in range(n_heads)]

    def chunk_body(ci, carry):
        r0 = pl.multiple_of(ci * c, c)
        rows = pl.ds(r0, c)
        q = _silu(q_ref[0, rows, :])
        f = lb + (1.0 - lb) * _sigmoid(f_ref[0, rows, :])
        lf = jnp.log(f)
        k = 1.0 - f
        vb = i_ref[0, rows, :].astype(BF16)
        ex = jnp.exp(_dot_exact_lhs(mst, lf))
        ex_cum = ex[0:c]
        qd = (q * ex_cum).astype(BF16)
        kr = (k * ex[c:2 * c]).astype(BF16)
        qb = q.astype(BF16)
        kb = k.astype(BF16)
        sts = [st_ref[hd] for hd in range(n_heads)]
        o = [_dot(qd[:, ls], st.astype(BF16), NT) for ls, st in zip(heads, sts)]
        sc = [lm_ref[0] * _dot(qb[:, ls], kb[:, ls], NT) for ls in heads]
        dqk = q - k
        for li in range(len(_HG_LEVELS)):
            g_l = ((k + rm_ref[li] * dqk) * ex[(2 + li) * c:(3 + li) * c]).astype(BF16)
            sc = [s_h + lm_ref[li + 1] * _dot(g_l[:, ls], g_l[:, ls], NT) for s_h, ls in zip(sc, heads)]
        o = [o_h + _dot(s_h.astype(BF16), vb[:, ls]) for o_h, s_h, ls in zip(o, sc, heads)]
        for hd, ls in enumerate(heads):
            st_ref[hd] = sts[hd] * ex_cum[c - 1:c, ls] + _dot(vb[:, ls], kr[:, ls], TN)
        on = [o_h * lax.rsqrt(jnp.mean(o_h * o_h, axis=-1, keepdims=True) + NORM_EPS) for o_h in o]
        o_full = jnp.concatenate(on, axis=1) * ng
        o_ref[0, rows, :] = (o_full * _silu(g_ref[0, rows, :])).astype(o_ref.dtype)
        return carry

    lax.fori_loop(0, n_chunks, chunk_body, 0)


def _hgrn2(hg, lb, norm_g, ts):
    bsz, s, n4 = hg.shape
    w = n4 // 4
    mst, lm, rm = _hgrn2_consts(w)
    n_heads = w // HG_HEAD
    const2 = lambda b, i: (0, 0)
    const3 = lambda b, i: (0, 0, 0)
    return pl.pallas_call(
        _hgrn2_kernel,
        out_shape=jax.ShapeDtypeStruct((bsz, s, w), BF16),
        grid=(bsz, s // ts),
        in_specs=[pl.BlockSpec((1, ts, w), lambda b, i: (b, i, 0)),
                  pl.BlockSpec((1, ts, w), lambda b, i: (b, i, 1)),
                  pl.BlockSpec((1, ts, w), lambda b, i: (b, i, 2)),
                  pl.BlockSpec((1, ts, w), lambda b, i: (b, i, 3)),
                  pl.BlockSpec((1, w), const2),
                  pl.BlockSpec((1, w), const2),
                  pl.BlockSpec(mst.shape, const2),
                  pl.BlockSpec(lm.shape, const3),
                  pl.BlockSpec(rm.shape, const3)],
        out_specs=pl.BlockSpec((1, ts, w), lambda b, i: (b, i, 0)),
        scratch_shapes=[pltpu.VMEM((n_heads, HG_HEAD, HG_HEAD), F32)],
        compiler_params=_cparams(("arbitrary", "arbitrary")),
        name="hgrn2",
    )(hg, hg, hg, hg, lb.reshape(1, w), norm_g.reshape(1, w), mst, lm, rm)


def _rwkv_consts(width):
    c = CHUNK
    t = np.arange(c)[:, None]
    s = np.arange(c)[None, :]
    tri = (s <= t).astype(np.float32)
    tt = np.arange(2 * c)[:, None]
    ss = np.arange(2 * c)[None, :]
    same = (tt // c) == (ss // c)
    strict = same & ((ss % c) < (tt % c))
    incl = same & ((ss % c) <= (tt % c))
    hsum = (np.arange(width)[:, None] // RW_HEAD) == (np.arange(width)[None, :] // RW_HEAD)
    return (jnp.asarray(tri, BF16), jnp.asarray(strict.astype(np.float32), F32),
            jnp.asarray(incl.astype(np.float32), F32), jnp.asarray(hsum.astype(np.float32), BF16))


def _rwkv7_kernel(p_ref, mu_ref, w0_ref, a0_ref, kk_ref, ka_ref, rk_ref, gnw_ref, gnb_ref,
                  w2_ref, a2_ref, g2_ref, tri_ref, sm_ref, im_ref, hs_ref,
                  o_ref, carry_ref, zt_ref):
    c = CHUNK
    nb = p_ref.shape[0]
    width = o_ref.shape[2]
    n_pairs = width // LANES

    @pl.when(pl.program_id(0) == 0)
    def _():
        carry_ref[...] = jnp.zeros_like(carry_ref)
        zt_ref[...] = jnp.zeros_like(zt_ref)

    hs = hs_ref[...]
    tri = tri_ref[...]
    smask = sm_ref[...] > 0
    imask = im_ref[...] > 0
    lane = lax.broadcasted_iota(jnp.int32, (c, LANES), 1)
    m0 = (lane < RW_HEAD).astype(F32)
    m1 = 1.0 - m0

    def stack(x):
        return jnp.concatenate([x * m0, x * m1], axis=0)

    rs, k2s, vs_, gs = [], [], [], []
    units = []
    for b in range(nb):
        p = p_ref[b]
        row = lax.broadcasted_iota(jnp.int32, p.shape, 0)
        prev = jnp.where(row == 0, carry_ref[b], pltpu.roll(p, 1, 0))
        carry_ref[b] = p[c - 1:c, :]
        xs = p + mu_ref[...] * (prev - p)
        r = xs[:, 0:width]
        k = xs[:, width:2 * width]
        v = xs[:, 2 * width:3 * width]
        slab = xs[:, 3 * width:]
        nz = -(w0_ref[...] + _dot(jnp.tanh(slab).astype(BF16), w2_ref[...]))
        softplus = jnp.maximum(nz, 0.0) + jnp.log(1.0 + jnp.exp(-jnp.abs(nz)))
        ld = -jnp.exp(-softplus - 0.5)
        a = _sigmoid(a0_ref[...] + _dot(slab.astype(BF16), a2_ref[...]))
        gs.append(_dot(_sigmoid(slab).astype(BF16), g2_ref[...]))
        kk0 = k * kk_ref[...]
        kk = kk0 * lax.rsqrt(jnp.maximum(_dot_exact_rhs(kk0 * kk0, hs), 1e-24))
        k2 = k * (1.0 + (a - 1.0) * ka_ref[...])
        a_in = -kk
        b_in = kk * a
        cum = _dot_exact_lhs(tri, ld)
        cum_t = cum[c - 1:c, :]
        e_c = jnp.exp(cum)
        e_nc = jnp.exp(-cum)
        e_rem = jnp.exp(cum_t - cum)
        at_f = a_in * jnp.exp(cum - ld)
        rt_f = r * e_c
        kt_f = k2 * e_nc
        bt_f = b_in * e_nc
        kh_f = k2 * e_rem
        bh_f = b_in * e_rem
        p_t = jnp.exp(cum_t)
        rs.append(r)
        k2s.append(k2)
        vs_.append(v)
        for pi in range(n_pairs):
            ls = slice(pi * LANES, (pi + 1) * LANES)
            units.append(dict(
                b=b, pi=pi,
                at=stack(at_f[:, ls]).astype(BF16), rt=stack(rt_f[:, ls]).astype(BF16),
                kt=stack(kt_f[:, ls]).astype(BF16), bt=stack(bt_f[:, ls]).astype(BF16),
                kh=stack(kh_f[:, ls]).astype(BF16), bh=stack(bh_f[:, ls]).astype(BF16),
                vs=stack(v[:, ls]).astype(BF16), p_t=p_t[:, ls]))

    for u in units:
        lhs = jnp.concatenate([u['at'], u['rt']], axis=0)
        u['gk'] = _dot(lhs, u['kt'], NT)
        u['gb'] = _dot(lhs, u['bt'], NT)
    for u in units:
        gk, gb = u.pop('gk'), u.pop('gb')
        a_kk = jnp.concatenate([jnp.where(smask, gk[:2 * c], 0.0), jnp.where(imask, gk[2 * c:], 0.0)], axis=0)
        u['pw'] = jnp.where(smask, gb[:2 * c], 0.0).astype(BF16)
        u['a_rb'] = jnp.where(imask, gb[2 * c:], 0.0).astype(BF16)
        u['a_kk'] = a_kk.astype(BF16)
    for u in units:
        av = _dot(u.pop('a_kk'), u['vs'])
        u['arkv'] = av[2 * c:]
        u['x'] = jnp.concatenate([u['at'].astype(F32), av[:2 * c]], axis=1)
    n_lvl = int(np.log2(c))
    for lvl in range(n_lvl):
        for u in units:
            u['x'] = u['x'] + _dot(u['pw'], u['x'].astype(BF16))
        if lvl + 1 < n_lvl:
            for u in units:
                u['pw'] = _dot(u['pw'], u['pw']).astype(BF16)
    for u in units:
        x = u.pop('x')
        u['zt'] = zt_ref[u['b'], u['pi']]
        u['uy'] = _dot(jnp.concatenate([x[:, :LANES].astype(BF16), u['rt']], axis=0), u['zt'].astype(BF16), NT)
        u['u_loc'] = x[:, LANES:]
    for u in units:
        uy = u.pop('uy')
        u['u'] = (uy[:2 * c] + u.pop('u_loc')).astype(BF16)
        u['y0'] = uy[2 * c:] + u.pop('arkv')
    for u in units:
        u['y'] = u.pop('y0') + _dot(u['a_rb'], u['u'])
        upd = _dot(jnp.concatenate([u['u'], u['vs']], axis=0), jnp.concatenate([u['bh'], u['kh']], axis=0), TN)
        zt_ref[u['b'], u['pi']] = u['zt'] * u['p_t'] + upd

    inv_n = 1.0 / RW_HEAD
    for b in range(nb):
        ys = [u['y'] for u in units if u['b'] == b]
        y = jnp.concatenate([yy[:c] + yy[c:] for yy in ys], axis=1)
        mean = _dot_exact_rhs(y, hs) * inv_n
        d = y - mean
        var = _dot_exact_rhs(d * d, hs) * inv_n
        yn = d * lax.rsqrt(var + RW_GN_EPS) * gnw_ref[...] + gnb_ref[...]
        bonus = _dot_exact_rhs(rs[b] * k2s[b] * rk_ref[...], hs) * vs_[b]
        o_ref[b] = ((yn + bonus) * gs[b]).astype(o_ref.dtype)


def _rwkv7(rw, mu, w0, w2, a0, a2, g2, k_k, k_a, r_k, gn_w, gn_b):
    bsz, s, cols = rw.shape
    width = w0.shape[-1]
    n_pairs = width // LANES
    slab = cols - 3 * width
    dl, al, gl = w2.shape[0], a2.shape[0], g2.shape[0]
    w2f = jnp.zeros((slab, width), F32).at[0:dl].set(w2).astype(BF16)
    a2f = jnp.zeros((slab, width), F32).at[dl:dl + al].set(a2).astype(BF16)
    g2f = jnp.zeros((slab, width), F32).at[dl + al:dl + al + gl].set(g2).astype(BF16)
    mup = jnp.zeros((1, cols), F32).at[0, :mu.shape[-1]].set(mu)
    tri, sm, im, hs = _rwkv_consts(width)
    row = lambda x: x.reshape(1, width)
    const = lambda i: (0, 0)
    vec = pl.BlockSpec((1, width), const)
    return pl.pallas_call(
        _rwkv7_kernel,
        out_shape=jax.ShapeDtypeStruct((bsz, s, width), BF16),
        grid=(s // CHUNK,),
        in_specs=[pl.BlockSpec((bsz, CHUNK, cols), lambda i: (0, i, 0)),
                  pl.BlockSpec((1, cols), const),
                  vec, vec, vec, vec, vec, vec, vec,
                  pl.BlockSpec((slab, width), const),
                  pl.BlockSpec((slab, width), const),
                  pl.BlockSpec((slab, width), const),
                  pl.BlockSpec(tri.shape, const),
                  pl.BlockSpec(sm.shape, const),
                  pl.BlockSpec(im.shape, const),
                  pl.BlockSpec(hs.shape, const)],
        out_specs=pl.BlockSpec((bsz, CHUNK, width), lambda i: (0, i, 0)),
        scratch_shapes=[pltpu.VMEM((bsz, 1, cols), F32),
                        pltpu.VMEM((bsz, n_pairs, LANES, LANES), F32)],
        compiler_params=_cparams(("arbitrary",)),
        name="rwkv7",
    )(rw, mup, row(w0), row(a0), row(k_k), row(k_a), row(r_k), row(gn_w), row(gn_b),
      w2f, a2f, g2f, tri, sm, im, hs)


def _out_proj_kernel(n_groups, n_experts,
                     x_ref, oa_ref, ob_ref, ga_ref, gb_ref, gt1_ref, sc2_ref, sh2_ref, g2_ref,
                     wa_ref, wb_ref, wo_ref, wr_ref, br_ref, tril_ref,
                     x1_ref, h2_ref, route_ref, cnt_ref, carry_ref):
    tm = x_ref.shape[1]
    first = (pl.program_id(0) == 0) & (pl.program_id(1) == 0)

    @pl.when(first)
    def _():
        carry_ref[...] = jnp.zeros_like(carry_ref)

    pa = _dot(oa_ref[0], wa_ref[...])
    pb = _dot(ob_ref[0], wb_ref[...])
    mixed = ga_ref[0].astype(F32) * pa + gb_ref[0].astype(F32) * pb
    x1 = x_ref[0] + gt1_ref[0] * _dot(mixed.astype(BF16), wo_ref[...])
    x1_ref[0] = x1
    ms = jnp.mean(x1 * x1, axis=-1, keepdims=True)
    h2 = (x1 * lax.rsqrt(ms + NORM_EPS) * g2_ref[...]) * (1.0 + sc2_ref[0]) + sh2_ref[0]
    _rows_to_tiles(h2_ref, h2)

    logits = _dot(h2, wr_ref[...], precision=HIGHEST) + br_ref[...]
    lane = lax.broadcasted_iota(jnp.int32, logits.shape, 1)
    neg = jnp.float32(-jnp.inf)
    big = jnp.int32(1 << 20)
    eg = n_experts // n_groups
    is_g = (lane >= n_experts) & (lane < n_experts + n_groups)
    lg = jnp.where(is_g, logits, neg)
    mg = jnp.max(lg, axis=-1, keepdims=True)
    p_grp = 1.0 / jnp.sum(jnp.where(is_g, jnp.exp(lg - mg), 0.0), axis=-1, keepdims=True)
    gidx = jnp.min(jnp.where(lg == mg, lane, big), axis=-1, keepdims=True) - n_experts
    sel = (lane >= gidx * eg) & (lane < gidx * eg + eg)
    le = jnp.where(sel, logits, neg)
    me = jnp.max(le, axis=-1, keepdims=True)
    pe_un = jnp.where(sel, jnp.exp(le - me), 0.0)
    pe = jnp.where(sel, pe_un / jnp.sum(pe_un, axis=-1, keepdims=True), -1.0)
    v1 = jnp.max(pe, axis=-1, keepdims=True)
    i1 = jnp.min(jnp.where(pe == v1, lane, big), axis=-1, keepdims=True)
    pe2 = jnp.where(lane == i1, -1.0, pe)
    v2 = jnp.max(pe2, axis=-1, keepdims=True)
    i2 = jnp.min(jnp.where(pe2 == v2, lane, big), axis=-1, keepdims=True)
    wsum = v1 + v2
    w1 = p_grp * v1 / wsum
    w2 = p_grp * v2 / wsum

    oh1 = (lane == i1).astype(F32)
    oh2 = (lane == i2).astype(F32)
    both = oh1 + oh2
    before = _dot(tril_ref[...], both.astype(BF16)) + carry_ref[...]
    rank1 = jnp.sum(oh1 * before, axis=-1, keepdims=True)
    rank2 = jnp.sum(oh2 * before, axis=-1, keepdims=True)
    carry_ref[...] = carry_ref[...] + jnp.sum(both, axis=0, keepdims=True)
    cnt_ref[...] = carry_ref[...]

    out = jnp.where(lane == 0, w1, 0.0)
    out = jnp.where(lane == 1, w2, out)
    out = jnp.where(lane == 2, i1.astype(F32), out)
    out = jnp.where(lane == 3, i2.astype(F32), out)
    out = jnp.where(lane == 4, rank1, out)
    out = jnp.where(lane == 5, rank2, out)
    route_ref[0] = out


def _out_proj(x, o_a, o_b, gates, mod3, norm2_g, wa, wb, wo, wr, br, n_groups, n_experts, tm):
    bsz, s, d = x.shape
    wdt = o_a.shape[-1]
    tril = jnp.asarray(np.tril(np.ones((tm, tm), np.float32), -1), BF16)
    const = lambda b, i: (0, 0)
    tile = lambda b, i: (b, i, 0)
    kern = functools.partial(_out_proj_kernel, n_groups, n_experts)
    return pl.pallas_call(
        kern,
        out_shape=(jax.ShapeDtypeStruct((bsz, s, d), F32),
                   jax.ShapeDtypeStruct((bsz * s * SUB, LANES), F32),
                   jax.ShapeDtypeStruct((bsz, s, LANES), F32),
                   jax.ShapeDtypeStruct((1, LANES), F32)),
        grid=(bsz, s // tm),
        in_specs=[pl.BlockSpec((1, tm, d), tile),
                  pl.BlockSpec((1, tm, wdt), tile),
                  pl.BlockSpec((1, tm, wdt), tile),
                  pl.BlockSpec((1, tm, d), lambda b, i: (b, i, 0)),
                  pl.BlockSpec((1, tm, d), lambda b, i: (b, i, 1)),
                  pl.BlockSpec((1, 1, d), lambda b, i: (b * 6 + 2, 0, 0)),
                  pl.BlockSpec((1, 1, d), lambda b, i: (b * 6 + 4, 0, 0)),
                  pl.BlockSpec((1, 1, d), lambda b, i: (b * 6 + 3, 0, 0)),
                  pl.BlockSpec((1, d), const),
                  pl.BlockSpec(wa.shape, const),
                  pl.BlockSpec(wb.shape, const),
                  pl.BlockSpec(wo.shape, const),
                  pl.BlockSpec(wr.shape, const),
                  pl.BlockSpec((1, LANES), const),
                  pl.BlockSpec((tm, tm), const)],
        out_specs=(pl.BlockSpec((1, tm, d), tile),
                   pl.BlockSpec((tm * SUB, LANES), lambda b, i: (b * (s // tm) + i, 0)),
                   pl.BlockSpec((1, tm, LANES), tile),
                   pl.BlockSpec((1, LANES), const)),
        scratch_shapes=[pltpu.VMEM((1, LANES), F32)],
        compiler_params=_cparams(("arbitrary", "arbitrary")),
        name="out_proj",
    )(x, o_a, o_b, gates, gates, mod3, mod3, mod3, norm2_g.reshape(1, d), wa, wb, wo, wr, br, tril)


def _moe_expert_kernel(blk_e_ref, tok_ref, h_ref, wg_ref, wu_ref, wd_ref, y_ref,
                       xbuf, wgb, wub, wdb, sem):
    i = pl.program_id(0)
    nb = pl.num_programs(0)
    blk = y_ref.shape[0] // SUB

    def issue(b, slot, rows):
        for r in rows:
            src = pl.multiple_of(tok_ref[b * blk + r] * SUB, SUB)
            pltpu.make_async_copy(h_ref.at[pl.ds(src, SUB), :], xbuf.at[slot, pl.ds(r * SUB, SUB), :],
                                  sem.at[slot]).start()

    def wait_block(slot):
        pltpu.make_async_copy(h_ref.at[pl.ds(0, blk * SUB), :], xbuf.at[slot], sem.at[slot]).wait()

    @pl.when(i == 0)
    def _():
        issue(0, 0, range(blk))

    new_expert = (i == 0) | (blk_e_ref[i] != blk_e_ref[jnp.maximum(i - 1, 0)])

    @pl.when(new_expert)
    def _():
        wgb[...] = wg_ref[0].astype(BF16)
        wub[...] = wu_ref[0].astype(BF16)
        wdb[...] = wd_ref[0].astype(BF16)

    slot = i % 2
    nslot = (i + 1) % 2
    nxt = jnp.minimum(i + 1, nb - 1)
    q = blk // 4
    wait_block(slot)
    xb = _tiles_to_rows(xbuf.at[slot], blk).astype(BF16)
    issue(nxt, nslot, range(0, q))
    hg = _dot(xb, wgb[...])
    issue(nxt, nslot, range(q, 2 * q))
    hu = _dot(xb, wub[...])
    issue(nxt, nslot, range(2 * q, 3 * q))
    hid = (_silu(hg) * hu).astype(BF16)
    y = _dot(hid, wdb[...])
    issue(nxt, nslot, range(3 * q, blk))
    _rows_to_tiles(y_ref, y)

    @pl.when(i == nb - 1)
    def _():
        wait_block(nslot)


def _moe_experts(h2, slot_tok, blk_e, w_gate, w_up, w_down, blk):
    d = w_gate.shape[1]
    assert d == SUB * LANES and h2.shape[1] == LANES
    p = slot_tok.shape[0]
    nb = p // blk
    f = w_gate.shape[-1]
    grid_spec = pltpu.PrefetchScalarGridSpec(
        num_scalar_prefetch=2,
        grid=(nb,),
        in_specs=[pl.BlockSpec(memory_space=pl.ANY),
                  pl.BlockSpec((1, d, f), lambda i, be, tk: (be[i], 0, 0)),
                  pl.BlockSpec((1, d, f), lambda i, be, tk: (be[i], 0, 0)),
                  pl.BlockSpec((1, f, d), lambda i, be, tk: (be[i], 0, 0))],
        out_specs=pl.BlockSpec((blk * SUB, LANES), lambda i, be, tk: (i, 0)),
        scratch_shapes=[pltpu.VMEM((2, blk * SUB, LANES), F32),
                        pltpu.VMEM((d, f), BF16),
                        pltpu.VMEM((d, f), BF16),
                        pltpu.VMEM((f, d), BF16),
                        pltpu.SemaphoreType.DMA((2,))],
    )
    return pl.pallas_call(
        _moe_expert_kernel,
        out_shape=jax.ShapeDtypeStruct((p * SUB, LANES), F32),
        grid_spec=grid_spec,
        compiler_params=_cparams(("arbitrary",)),
        name="moe_experts",
    )(blk_e, slot_tok, h2, w_gate, w_up, w_down)


def _moe_combine_kernel(dest_ref, x1_ref, route_ref, gt2_ref, fg_ref, y_ref, o_ref, ybuf, sem):
    i = pl.program_id(0)
    n_steps = pl.num_programs(0)
    tm = x1_ref.shape[0]

    def issue(step, slot, rows):
        for r in rows:
            for kslot in range(TOP_K):
                src = pl.multiple_of(dest_ref[(step * tm + r) * TOP_K + kslot] * SUB, SUB)
                pltpu.make_async_copy(y_ref.at[pl.ds(src, SUB), :],
                                      ybuf.at[slot, pl.ds((kslot * tm + r) * SUB, SUB), :], sem.at[slot]).start()

    def wait_tile(slot):
        pltpu.make_async_copy(y_ref.at[pl.ds(0, TOP_K * tm * SUB), :], ybuf.at[slot], sem.at[slot]).wait()

    @pl.when(i == 0)
    def _():
        issue(0, 0, range(tm))

    slot = i % 2
    nslot = (i + 1) % 2
    nxt = jnp.minimum(i + 1, n_steps - 1)
    wait_tile(slot)
    issue(nxt, nslot, range(0, tm // 2))
    route = route_ref[...]
    w1 = route[:, 0:1]
    w2 = route[:, 1:2]
    moe = w1 * _tiles_to_rows(ybuf.at[slot], tm) + w2 * _tiles_to_rows(ybuf.at[slot], tm, base=tm)
    xo = x1_ref[...] + gt2_ref[0] * moe
    ms = jnp.mean(xo * xo, axis=-1, keepdims=True)
    out = xo * lax.rsqrt(ms + NORM_EPS) * fg_ref[...]
    issue(nxt, nslot, range(tm // 2, tm))
    o_ref[...] = out

    @pl.when(i == n_steps - 1)
    def _():
        wait_tile(nslot)


def _moe_combine(x1, route, dest, yb, mod3, final_g, s, tm):
    n, d = x1.shape
    tiles_per_batch = s // tm
    grid_spec = pltpu.PrefetchScalarGridSpec(
        num_scalar_prefetch=1,
        grid=(n // tm,),
        in_specs=[pl.BlockSpec((tm, d), lambda i, ds_: (i, 0)),
                  pl.BlockSpec((tm, LANES), lambda i, ds_: (i, 0)),
                  pl.BlockSpec((1, 1, d), lambda i, ds_: ((i // tiles_per_batch) * 6 + 5, 0, 0)),
                  pl.BlockSpec((1, d), lambda i, ds_: (0, 0)),
                  pl.BlockSpec(memory_space=pl.ANY)],
        out_specs=pl.BlockSpec((tm, d), lambda i, ds_: (i, 0)),
        scratch_shapes=[pltpu.VMEM((2, TOP_K * tm * SUB, LANES), F32),
                        pltpu.SemaphoreType.DMA((2,))],
    )
    return pl.pallas_call(
        _moe_combine_kernel,
        out_shape=jax.ShapeDtypeStruct((n, d), F32),
        grid_spec=grid_spec,
        compiler_params=_cparams(("arbitrary",)),
        name="moe_combine",
    )(dest, x1, route, mod3, final_g.reshape(1, d), yb)


def _pick(n, candidates):
    for t in candidates:
        if n % t == 0:
            return t
    raise ValueError(f"no tile in {candidates} divides {n}")


def kernel(x, c, ada_w, ada_b, norm1_g, w_in, hg_lb, hg_norm_g, rw_mu, rw_w0, rw_w2, rw_a0, rw_a2, rw_g2, rw_kk, rw_ka, rw_rk, rw_gn_w, rw_gn_b, w_proj_a, w_proj_b, w_out, norm2_g, router_g_w, router_g_b, router_e_w, router_e_b, exp_w_gate, exp_w_up, exp_w_down, final_g):
    bsz, s, d = x.shape
    depth = ada_w.shape[0]
    hg_f = hg_lb.shape[-1]
    hg_w = hg_norm_g.shape[-1]
    rw_w = rw_w0.shape[-1]
    rw_cols = rw_mu.shape[-1]
    n_groups = router_g_w.shape[-1]
    n_experts = router_e_w.shape[-1]
    assert hg_f == hg_w and s % CHUNK == 0 and n_experts + n_groups <= LANES and d == SUB * LANES

    lb_all = jnp.cumsum(jax.nn.softmax(hg_lb.astype(F32), axis=0), axis=0)
    n = bsz * s
    blk = 256
    n_blocks = (n * TOP_K + n_experts * blk) // blk
    for l in range(depth):
        mod = _ada_mod(c, ada_w[l], ada_b[l])
        mod3 = mod.reshape(bsz * 6, 1, d)

        hg_cols = 2 * hg_f + 2 * hg_w
        rw_pad = -(-rw_cols // 256) * 256
        wl = w_in[l]
        w_hg = wl[:, :hg_cols].astype(BF16)
        w_rw = jnp.zeros((d, rw_pad), BF16).at[:, :rw_cols].set(wl[:, hg_cols:hg_cols + rw_cols].astype(BF16))
        w_gt = wl[:, hg_cols + rw_cols:].astype(BF16)
        hg, rw, gates = _in_proj(x, mod3, norm1_g[l], w_hg, w_rw, w_gt, _pick(s, (512, 256, 128, 64)))

        o_a = _hgrn2(hg, lb_all[l], hg_norm_g[l], _pick(s, (512, 256, 128, 64)))
        o_b = _rwkv7(rw, rw_mu[l], rw_w0[l], rw_w2[l], rw_a0[l], rw_a2[l], rw_g2[l],
                     rw_kk[l], rw_ka[l], rw_rk[l].reshape(-1), rw_gn_w[l], rw_gn_b[l])

        wr = jnp.zeros((d, LANES), F32).at[:, :n_experts].set(router_e_w[l])
        wr = wr.at[:, n_experts:n_experts + n_groups].set(router_g_w[l])
        br = jnp.zeros((1, LANES), F32).at[0, :n_experts].set(router_e_b[l])
        br = br.at[0, n_experts:n_experts + n_groups].set(router_g_b[l])
        x1, h2, route, counts = _out_proj(
            x, o_a, o_b, gates, mod3, norm2_g[l],
            w_proj_a[l].astype(BF16), w_proj_b[l].astype(BF16), w_out[l].astype(BF16),
            wr, br, n_groups, n_experts, _pick(s, (256, 128, 64)))

        route2 = route.reshape(n, LANES)
        eid = route2[:, 2:4].astype(jnp.int32)
        rank = route2[:, 4:6].astype(jnp.int32)
        cnt = counts[0, :n_experts].astype(jnp.int32)
        padded = (cnt + blk - 1) // blk * blk
        pad_end = jnp.cumsum(padded)
        pad_start = pad_end - padded
        dest = pad_start[eid] + rank
        blk_start = jnp.arange(n_blocks, dtype=jnp.int32) * blk
        blk_e = jnp.minimum(jnp.sum((pad_end[None, :] <= blk_start[:, None]).astype(jnp.int32), axis=1),
                            n_experts - 1)
        tok = jnp.broadcast_to(jnp.arange(n, dtype=jnp.int32)[:, None], (n, TOP_K))
        slot_tok = jnp.zeros((n_blocks * blk,), jnp.int32).at[dest.reshape(-1)].set(tok.reshape(-1))

        yb = _moe_experts(h2, slot_tok, blk_e,
                          exp_w_gate[l], exp_w_up[l], exp_w_down[l], blk)
        last = l == depth - 1
        assert last, "the final RMSNorm is fused into the last layer's combine"
        out = _moe_combine(x1.reshape(n, d), route2, dest.reshape(-1), yb, mod3, final_g,
                           s, _pick(s, (256, 128, 64)))
        x = out.reshape(bsz, s, d)
    return x
```

```python
import functools

import numpy as np
import jax
import jax.numpy as jnp
from jax import lax
from jax.experimental import pallas as pl
from jax.experimental.pallas import tpu as pltpu

F32 = jnp.float32
BF16 = jnp.bfloat16
HIGHEST = lax.Precision.HIGHEST

NORM_EPS = 1e-6
HG_HEAD = 128
RW_HEAD = 64
RW_GN_EPS = 64e-5
TOP_K = 2
CHUNK = 64
LANES = 128
SUB = 8
VMEM_LIMIT = 56 * 1024 * 1024

NT = (((1,), (1,)), ((), ()))
TN = (((0,), (0,)), ((), ()))


def _dot(a, b, dims=None, precision=None):
    if dims is None:
        return jnp.dot(a, b, preferred_element_type=F32, precision=precision)
    return lax.dot_general(a, b, dims, preferred_element_type=F32, precision=precision)


def _split3(x):
    hi = x.astype(BF16)
    r1 = x - hi.astype(F32)
    mid = r1.astype(BF16)
    lo = (r1 - mid.astype(F32)).astype(BF16)
    return hi, mid, lo


def _dot_exact_lhs(m_bf16, x):
    hi, mid, lo = _split3(x)
    return _dot(m_bf16, hi) + _dot(m_bf16, mid) + _dot(m_bf16, lo)


def _dot_exact_rhs(x, m_bf16):
    hi, mid, lo = _split3(x)
    return _dot(hi, m_bf16) + _dot(mid, m_bf16) + _dot(lo, m_bf16)


def _sigmoid(x):
    return 1.0 / (1.0 + jnp.exp(-x))


def _silu(x):
    return x * _sigmoid(x)


def _rows_to_tiles(ref, val):
    m = val.shape[0]
    for j in range(SUB):
        ref[pl.ds(j, m, stride=SUB), :] = val[:, j * LANES:(j + 1) * LANES]


def _tiles_to_rows(ref, m, base=0):
    return jnp.concatenate([ref[pl.ds(base * SUB + j, m, stride=SUB), :] for j in range(SUB)], axis=1)


def _cparams(sem):
    return pltpu.CompilerParams(dimension_semantics=sem, vmem_limit_bytes=VMEM_LIMIT)


def _ada_kernel(c_ref, w_ref, b_ref, o_ref):
    c = c_ref[...]
    o_ref[...] = _dot(_silu(c), w_ref[...], precision=HIGHEST) + b_ref[...]


def _ada_mod(c, w, b):
    bsz, d = c.shape
    n = w.shape[1]
    rows = 8
    cp = jnp.zeros((rows, d), F32).at[:bsz].set(c)
    tn = 1536
    out = pl.pallas_call(
        _ada_kernel,
        out_shape=jax.ShapeDtypeStruct((rows, n), F32),
        grid=(n // tn,),
        in_specs=[pl.BlockSpec((rows, d), lambda j: (0, 0)),
                  pl.BlockSpec((d, tn), lambda j: (0, j)),
                  pl.BlockSpec((1, tn), lambda j: (0, j))],
        out_specs=pl.BlockSpec((rows, tn), lambda j: (0, j)),
        compiler_params=_cparams(("arbitrary",)),
        name="ada_mod",
    )(cp, w, b.reshape(1, n))
    return out[:bsz]


def _in_proj_kernel(x_ref, sh_ref, sc_ref, g_ref, whg_ref, wrw_ref, wgt_ref, hg_ref, rw_ref, gt_ref):
    x = x_ref[0]
    ms = jnp.mean(x * x, axis=-1, keepdims=True)
    h = (x * lax.rsqrt(ms + NORM_EPS) * g_ref[...]) * (1.0 + sc_ref[0]) + sh_ref[0]
    hb = h.astype(BF16)
    step = 512
    for n0 in range(0, whg_ref.shape[1], step):
        hg_ref[0, :, n0:n0 + step] = _dot(hb, whg_ref[:, n0:n0 + step])
    for n0 in range(0, wrw_ref.shape[1], 256):
        rw_ref[0, :, n0:n0 + 256] = _dot(hb, wrw_ref[:, n0:n0 + 256])
    for n0 in range(0, wgt_ref.shape[1], step):
        gt_ref[0, :, n0:n0 + step] = _sigmoid(_dot(hb, wgt_ref[:, n0:n0 + step])).astype(BF16)


def _in_proj(x, mod3, norm_g, w_hg, w_rw, w_gt, tm):
    bsz, s, d = x.shape
    n_hg, n_rw, n_gt = w_hg.shape[1], w_rw.shape[1], w_gt.shape[1]
    const = lambda b, i: (0, 0)
    return pl.pallas_call(
        _in_proj_kernel,
        out_shape=(jax.ShapeDtypeStruct((bsz, s, n_hg), F32),
                   jax.ShapeDtypeStruct((bsz, s, n_rw), F32),
                   jax.ShapeDtypeStruct((bsz, s, n_gt), BF16)),
        grid=(bsz, s // tm),
        in_specs=[pl.BlockSpec((1, tm, d), lambda b, i: (b, i, 0)),
                  pl.BlockSpec((1, 1, d), lambda b, i: (b * 6 + 0, 0, 0)),
                  pl.BlockSpec((1, 1, d), lambda b, i: (b * 6 + 1, 0, 0)),
                  pl.BlockSpec((1, d), const),
                  pl.BlockSpec((d, n_hg), const),
                  pl.BlockSpec((d, n_rw), const),
                  pl.BlockSpec((d, n_gt), const)],
        out_specs=(pl.BlockSpec((1, tm, n_hg), lambda b, i: (b, i, 0)),
                   pl.BlockSpec((1, tm, n_rw), lambda b, i: (b, i, 0)),
                   pl.BlockSpec((1, tm, n_gt), lambda b, i: (b, i, 0))),
        compiler_params=_cparams(("arbitrary", "arbitrary")),
        name="in_proj",
    )(x, mod3, mod3, norm_g.reshape(1, d), w_hg, w_rw, w_gt)


_HG_LEVELS = (32, 16, 8, 4, 2, 1)


def _hgrn2_consts(width):
    c = CHUNK
    t = np.arange(c)[:, None]
    s = np.arange(c)[None, :]
    blocks = [(s <= t), (s > t)]
    lvl_masks = []
    right = []
    for h in _HG_LEVELS:
        m = (t // (2 * h)) * 2 * h + h
        is_r = (t & h) != 0
        blk = np.where(is_r, (s >= m) & (s <= t), (s > t) & (s <= m - 1))
        blocks.append(blk)
        lvl_masks.append(is_r & ((s & h) == 0) & ((t // (2 * h)) == (s // (2 * h))))
        right.append(np.broadcast_to(is_r, (c, width)))
    mst = np.concatenate(blocks, axis=0).astype(np.float32)
    lm = np.stack([np.eye(c, dtype=bool)] + lvl_masks).astype(np.float32)
    rm = np.stack(right).astype(np.float32)
    return jnp.asarray(mst, BF16), jnp.asarray(lm, F32), jnp.asarray(rm, F32)


def _hgrn2_kernel(q_ref, f_ref, i_ref, g_ref, lb_ref, ng_ref, mst_ref, lm_ref, rm_ref, o_ref, st_ref):
    c = CHUNK
    n_heads = q_ref.shape[2] // HG_HEAD
    n_chunks = q_ref.shape[1] // c

    @pl.when(pl.program_id(1) == 0)
    def _():
        st_ref[...] = jnp.zeros_like(st_ref)

    mst = mst_ref[...]
    lb = lb_ref[...]
    ng = ng_ref[...]
    heads = [slice(hd * HG_HEAD, (hd + 1) * HG_HEAD) for hd in range(n_heads)]

    def chunk_body(ci, carry):
        r0 = pl.multiple_of(ci * c, c)
        rows = pl.ds(r0, c)
        q = _silu(q_ref[0, rows, :])
        f = lb + (1.0 - lb) * _sigmoid(f_ref[0, rows, :])
        lf = jnp.log(f)
        k = 1.0 - f
        vb = i_ref[0, rows, :].astype(BF16)
        ex = jnp.exp(_dot_exact_lhs(mst, lf))
        ex_cum = ex[0:c]
        qd = (q * ex_cum).astype(BF16)
        kr = (k * ex[c:2 * c]).astype(BF16)
        qb = q.astype(BF16)
        kb = k.astype(BF16)
        sts = [st_ref[hd] for hd in range(n_heads)]
        o = [_dot(qd[:, ls], st.astype(BF16), NT) for ls, st in zip(heads, sts)]
        sc = [lm_ref[0] * _dot(qb[:, ls], kb[:, ls], NT) for ls in heads]
        dqk = q - k
        for li in range(len(_HG_LEVELS)):
            g_l = ((k + rm_ref[li] * dqk) * ex[(2 + li) * c:(3 + li) * c]).astype(BF16)
            sc = [s_h + lm_ref[li + 1] * _dot(g_l[:, ls], g_l[:, ls], NT) for s_h, ls in zip(sc, heads)]
        o = [o_h + _dot(s_h.astype(BF16), vb[:, ls]) for o_h, s_h, ls in zip(o, sc, heads)]
        for hd, ls in enumerate(heads):
            st_ref[hd] = sts[hd] * ex_cum[c - 1:c, ls] + _dot(vb[:, ls], kr[:, ls], TN)
        on = [o_h * lax.rsqrt(jnp.mean(o_h * o_h, axis=-1, keepdims=True) + NORM_EPS) for o_h in o]
        o_full = jnp.concatenate(on, axis=1) * ng
        o_ref[0, rows, :] = (o_full * _silu(g_ref[0, rows, :])).astype(o_ref.dtype)
        return carry

    lax.fori_loop(0, n_chunks, chunk_body, 0)


def _hgrn2(hg, lb, norm_g, ts):
    bsz, s, n4 = hg.shape
    w = n4 // 4
    mst, lm, rm = _hgrn2_consts(w)
    n_heads = w // HG_HEAD
    const2 = lambda b, i: (0, 0)
    const3 = lambda b, i: (0, 0, 0)
    return pl.pallas_call(
        _hgrn2_kernel,
        out_shape=jax.ShapeDtypeStruct((bsz, s, w), BF16),
        grid=(bsz, s // ts),
        in_specs=[pl.BlockSpec((1, ts, w), lambda b, i: (b, i, 0)),
                  pl.BlockSpec((1, ts, w), lambda b, i: (b, i, 1)),
                  pl.BlockSpec((1, ts, w), lambda b, i: (b, i, 2)),
                  pl.BlockSpec((1, ts, w), lambda b, i: (b, i, 3)),
                  pl.BlockSpec((1, w), const2),
                  pl.BlockSpec((1, w), const2),
                  pl.BlockSpec(mst.shape, const2),
                  pl.BlockSpec(lm.shape, const3),
                  pl.BlockSpec(rm.shape, const3)],
        out_specs=pl.BlockSpec((1, ts, w), lambda b, i: (b, i, 0)),
        scratch_shapes=[pltpu.VMEM((n_heads, HG_HEAD, HG_HEAD), F32)],
        compiler_params=_cparams(("arbitrary", "arbitrary")),
        name="hgrn2",
    )(hg, hg, hg, hg, lb.reshape(1, w), norm_g.reshape(1, w), mst, lm, rm)


def _rwkv_consts(width):
    c = CHUNK
    t = np.arange(c)[:, None]
    s = np.arange(c)[None, :]
    tri = (s <= t).astype(np.float32)
    tt = np.arange(2 * c)[:, None]
    ss = np.arange(2 * c)[None, :]
    same = (tt // c) == (ss // c)
    strict = same & ((ss % c) < (tt % c))
    incl = same & ((ss % c) <= (tt % c))
    hsum = (np.arange(width)[:, None] // RW_HEAD) == (np.arange(width)[None, :] // RW_HEAD)
    return (jnp.asarray(tri, BF16), jnp.asarray(strict.astype(np.float32), F32),
            jnp.asarray(incl.astype(np.float32), F32), jnp.asarray(hsum.astype(np.float32), BF16))


def _rwkv7_kernel(p_ref, mu_ref, w0_ref, a0_ref, kk_ref, ka_ref, rk_ref, gnw_ref, gnb_ref,
                  w2_ref, a2_ref, g2_ref, tri_ref, sm_ref, im_ref, hs_ref,
                  o_ref, carry_ref, zt_ref):
    c = CHUNK
    nb = p_ref.shape[0]
    width = o_ref.shape[2]
    n_pairs = width // LANES

    @pl.when(pl.program_id(0) == 0)
    def _():
        carry_ref[...] = jnp.zeros_like(carry_ref)
        zt_ref[...] = jnp.zeros_like(zt_ref)

    hs = hs_ref[...]
    tri = tri_ref[...]
    smask = sm_ref[...] > 0
    imask = im_ref[...] > 0
    lane = lax.broadcasted_iota(jnp.int32, (c, LANES), 1)
    m0 = (lane < RW_HEAD).astype(F32)
    m1 = 1.0 - m0

    def stack(x):
        return jnp.concatenate([x * m0, x * m1], axis=0)

    rs, k2s, vs_, gs = [], [], [], []
    units = []
    for b in range(nb):
        p = p_ref[b]
        row = lax.broadcasted_iota(jnp.int32, p.shape, 0)
        prev = jnp.where(row == 0, carry_ref[b], pltpu.roll(p, 1, 0))
        carry_ref[b] = p[c - 1:c, :]
        xs = p + mu_ref[...] * (prev - p)
        r = xs[:, 0:width]
        k = xs[:, width:2 * width]
        v = xs[:, 2 * width:3 * width]
        slab = xs[:, 3 * width:]
        nz = -(w0_ref[...] + _dot(jnp.tanh(slab).astype(BF16), w2_ref[...]))
        softplus = jnp.maximum(nz, 0.0) + jnp.log(1.0 + jnp.exp(-jnp.abs(nz)))
        ld = -jnp.exp(-softplus - 0.5)
        a = _sigmoid(a0_ref[...] + _dot(slab.astype(BF16), a2_ref[...]))
        gs.append(_dot(_sigmoid(slab).astype(BF16), g2_ref[...]))
        kk0 = k * kk_ref[...]
        kk = kk0 * lax.rsqrt(jnp.maximum(_dot_exact_rhs(kk0 * kk0, hs), 1e-24))
        k2 = k * (1.0 + (a - 1.0) * ka_ref[...])
        a_in = -kk
        b_in = kk * a
        cum = _dot_exact_lhs(tri, ld)
        cum_t = cum[c - 1:c, :]
        e_c = jnp.exp(cum)
        e_nc = jnp.exp(-cum)
        e_rem = jnp.exp(cum_t - cum)
        at_f = a_in * jnp.exp(cum - ld)
        rt_f = r * e_c
        kt_f = k2 * e_nc
        bt_f = b_in * e_nc
        kh_f = k2 * e_rem
        bh_f = b_in * e_rem
        p_t = jnp.exp(cum_t)
        rs.append(r)
        k2s.append(k2)
        vs_.append(v)
        for pi in range(n_pairs):
            ls = slice(pi * LANES, (pi + 1) * LANES)
            units.append(dict(
                b=b, pi=pi,
                at=stack(at_f[:, ls]).astype(BF16), rt=stack(rt_f[:, ls]).astype(BF16),
                kt=stack(kt_f[:, ls]).astype(BF16), bt=stack(bt_f[:, ls]).astype(BF16),
                kh=stack(kh_f[:, ls]).astype(BF16), bh=stack(bh_f[:, ls]).astype(BF16),
                vs=stack(v[:, ls]).astype(BF16), p_t=p_t[:, ls]))

    for u in units:
        lhs = jnp.concatenate([u['at'], u['rt']], axis=0)
        u['gk'] = _dot(lhs, u['kt'], NT)
        u['gb'] = _dot(lhs, u['bt'], NT)
    for u in units:
        gk, gb = u.pop('gk'), u.pop('gb')
        a_kk = jnp.concatenate([jnp.where(smask, gk[:2 * c], 0.0), jnp.where(imask, gk[2 * c:], 0.0)], axis=0)
        u['pw'] = jnp.where(smask, gb[:2 * c], 0.0).astype(BF16)
        u['a_rb'] = jnp.where(imask, gb[2 * c:], 0.0).astype(BF16)
        u['a_kk'] = a_kk.astype(BF16)
    for u in units:
        av = _dot(u.pop('a_kk'), u['vs'])
        u['arkv'] = av[2 * c:]
        u['x'] = jnp.concatenate([u['at'].astype(F32), av[:2 * c]], axis=1)
    n_lvl = int(np.log2(c))
    for lvl in range(n_lvl):
        for u in units:
            u['x'] = u['x'] + _dot(u['pw'], u['x'].astype(BF16))
        if lvl + 1 < n_lvl:
            for u in units:
                u['pw'] = _dot(u['pw'], u['pw']).astype(BF16)
    for u in units:
        x = u.pop('x')
        u['zt'] = zt_ref[u['b'], u['pi']]
        u['uy'] = _dot(jnp.concatenate([x[:, :LANES].astype(BF16), u['rt']], axis=0), u['zt'].astype(BF16), NT)
        u['u_loc'] = x[:, LANES:]
    for u in units:
        uy = u.pop('uy')
        u['u'] = (uy[:2 * c] + u.pop('u_loc')).astype(BF16)
        u['y0'] = uy[2 * c:] + u.pop('arkv')
    for u in units:
        u['y'] = u.pop('y0') + _dot(u['a_rb'], u['u'])
        upd = _dot(jnp.concatenate([u['u'], u['vs']], axis=0), jnp.concatenate([u['bh'], u['kh']], axis=0), TN)
        zt_ref[u['b'], u['pi']] = u['zt'] * u['p_t'] + upd

    inv_n = 1.0 / RW_HEAD
    for b in range(nb):
        ys = [u['y'] for u in units if u['b'] == b]
        y = jnp.concatenate([yy[:c] + yy[c:] for yy in ys], axis=1)
        mean = _dot_exact_rhs(y, hs) * inv_n
        d = y - mean
        var = _dot_exact_rhs(d * d, hs) * inv_n
        yn = d * lax.rsqrt(var + RW_GN_EPS) * gnw_ref[...] + gnb_ref[...]
        bonus = _dot_exact_rhs(rs[b] * k2s[b] * rk_ref[...], hs) * vs_[b]
        o_ref[b] = ((yn + bonus) * gs[b]).astype(o_ref.dtype)


def _rwkv7(rw, mu, w0, w2, a0, a2, g2, k_k, k_a, r_k, gn_w, gn_b):
    bsz, s, cols = rw.shape
    width = w0.shape[-1]
    n_pairs = width // LANES
    slab = cols - 3 * width
    dl, al, gl = w2.shape[0], a2.shape[0], g2.shape[0]
    w2f = jnp.zeros((slab, width), F32).at[0:dl].set(w2).astype(BF16)
    a2f = jnp.zeros((slab, width), F32).at[dl:dl + al].set(a2).astype(BF16)
    g2f = jnp.zeros((slab, width), F32).at[dl + al:dl + al + gl].set(g2).astype(BF16)
    mup = jnp.zeros((1, cols), F32).at[0, :mu.shape[-1]].set(mu)
    tri, sm, im, hs = _rwkv_consts(width)
    row = lambda x: x.reshape(1, width)
    const = lambda i: (0, 0)
    vec = pl.BlockSpec((1, width), const)
    return pl.pallas_call(
        _rwkv7_kernel,
        out_shape=jax.ShapeDtypeStruct((bsz, s, width), BF16),
        grid=(s // CHUNK,),
        in_specs=[pl.BlockSpec((bsz, CHUNK, cols), lambda i: (0, i, 0)),
                  pl.BlockSpec((1, cols), const),
                  vec, vec, vec, vec, vec, vec, vec,
                  pl.BlockSpec((slab, width), const),
                  pl.BlockSpec((slab, width), const),
                  pl.BlockSpec((slab, width), const),
                  pl.BlockSpec(tri.shape, const),
                  pl.BlockSpec(sm.shape, const),
                  pl.BlockSpec(im.shape, const),
                  pl.BlockSpec(hs.shape, const)],
        out_specs=pl.BlockSpec((bsz, CHUNK, width), lambda i: (0, i, 0)),
        scratch_shapes=[pltpu.VMEM((bsz, 1, cols), F32),
                        pltpu.VMEM((bsz, n_pairs, LANES, LANES), F32)],
        compiler_params=_cparams(("arbitrary",)),
        name="rwkv7",
    )(rw, mup, row(w0), row(a0), row(k_k), row(k_a), row(r_k), row(gn_w), row(gn_b),
      w2f, a2f, g2f, tri, sm, im, hs)


def _out_proj_kernel(n_groups, n_experts,
                     x_ref, oa_ref, ob_ref, ga_ref, gb_ref, gt1_ref, sc2_ref, sh2_ref, g2_ref,
                     wa_ref, wb_ref, wo_ref, wr_ref, wrl_ref, br_ref, tril_ref,
                     x1_ref, h2_ref, route_ref, cnt_ref, carry_ref):
    tm = x_ref.shape[1]
    first = (pl.program_id(0) == 0) & (pl.program_id(1) == 0)

    @pl.when(first)
    def _():
        carry_ref[...] = jnp.zeros_like(carry_ref)

    pa = _dot(oa_ref[0], wa_ref[...])
    pb = _dot(ob_ref[0], wb_ref[...])
    mixed = ga_ref[0].astype(F32) * pa + gb_ref[0].astype(F32) * pb
    x1 = x_ref[0] + gt1_ref[0] * _dot(mixed.astype(BF16), wo_ref[...])
    x1_ref[0] = x1
    ms = jnp.mean(x1 * x1, axis=-1, keepdims=True)
    h2 = (x1 * lax.rsqrt(ms + NORM_EPS) * g2_ref[...]) * (1.0 + sc2_ref[0]) + sh2_ref[0]
    _rows_to_tiles(h2_ref, h2)

    h2_hi = h2.astype(BF16)
    h2_lo = (h2 - h2_hi.astype(F32)).astype(BF16)
    logits = (_dot(h2_hi, wr_ref[...]) + _dot(h2_lo, wr_ref[...]) + _dot(h2_hi, wrl_ref[...])) + br_ref[...]
    lane = lax.broadcasted_iota(jnp.int32, logits.shape, 1)
    neg = jnp.float32(-jnp.inf)
    big = jnp.int32(1 << 20)
    eg = n_experts // n_groups
    is_g = (lane >= n_experts) & (lane < n_experts + n_groups)
    lg = jnp.where(is_g, logits, neg)
    mg = jnp.max(lg, axis=-1, keepdims=True)
    p_grp = 1.0 / jnp.sum(jnp.where(is_g, jnp.exp(lg - mg), 0.0), axis=-1, keepdims=True)
    gidx = jnp.min(jnp.where(lg == mg, lane, big), axis=-1, keepdims=True) - n_experts
    sel = (lane >= gidx * eg) & (lane < gidx * eg + eg)
    le = jnp.where(sel, logits, neg)
    me = jnp.max(le, axis=-1, keepdims=True)
    pe_un = jnp.where(sel, jnp.exp(le - me), 0.0)
    pe = jnp.where(sel, pe_un / jnp.sum(pe_un, axis=-1, keepdims=True), -1.0)
    v1 = jnp.max(pe, axis=-1, keepdims=True)
    i1 = jnp.min(jnp.where(pe == v1, lane, big), axis=-1, keepdims=True)
    pe2 = jnp.where(lane == i1, -1.0, pe)
    v2 = jnp.max(pe2, axis=-1, keepdims=True)
    i2 = jnp.min(jnp.where(pe2 == v2, lane, big), axis=-1, keepdims=True)
    wsum = v1 + v2
    w1 = p_grp * v1 / wsum
    w2 = p_grp * v2 / wsum

    oh1 = (lane == i1).astype(F32)
    oh2 = (lane == i2).astype(F32)
    both = oh1 + oh2
    before = _dot(tril_ref[...], both.astype(BF16)) + carry_ref[...]
    rank1 = jnp.sum(oh1 * before, axis=-1, keepdims=True)
    rank2 = jnp.sum(oh2 * before, axis=-1, keepdims=True)
    carry_ref[...] = carry_ref[...] + jnp.sum(both, axis=0, keepdims=True)
    cnt_ref[...] = carry_ref[...]

    out = jnp.where(lane == 0, w1, 0.0)
    out = jnp.where(lane == 1, w2, out)
    out = jnp.where(lane == 2, i1.astype(F32), out)
    out = jnp.where(lane == 3, i2.astype(F32), out)
    out = jnp.where(lane == 4, rank1, out)
    out = jnp.where(lane == 5, rank2, out)
    route_ref[0] = out


def _out_proj(x, o_a, o_b, gates, mod3, norm2_g, wa, wb, wo, wr, br, n_groups, n_experts, tm):
    bsz, s, d = x.shape
    wdt = o_a.shape[-1]
    tril = jnp.asarray(np.tril(np.ones((tm, tm), np.float32), -1), BF16)
    wr_hi = wr.astype(BF16)
    wr_lo = (wr - wr_hi.astype(F32)).astype(BF16)
    const = lambda b, i: (0, 0)
    tile = lambda b, i: (b, i, 0)
    kern = functools.partial(_out_proj_kernel, n_groups, n_experts)
    return pl.pallas_call(
        kern,
        out_shape=(jax.ShapeDtypeStruct((bsz, s, d), F32),
                   jax.ShapeDtypeStruct((bsz * s * SUB, LANES), F32),
                   jax.ShapeDtypeStruct((bsz, s, LANES), F32),
                   jax.ShapeDtypeStruct((1, LANES), F32)),
        grid=(bsz, s // tm),
        in_specs=[pl.BlockSpec((1, tm, d), tile),
                  pl.BlockSpec((1, tm, wdt), tile),
                  pl.BlockSpec((1, tm, wdt), tile),
                  pl.BlockSpec((1, tm, d), lambda b, i: (b, i, 0)),
                  pl.BlockSpec((1, tm, d), lambda b, i: (b, i, 1)),
                  pl.BlockSpec((1, 1, d), lambda b, i: (b * 6 + 2, 0, 0)),
                  pl.BlockSpec((1, 1, d), lambda b, i: (b * 6 + 4, 0, 0)),
                  pl.BlockSpec((1, 1, d), lambda b, i: (b * 6 + 3, 0, 0)),
                  pl.BlockSpec((1, d), const),
                  pl.BlockSpec(wa.shape, const),
                  pl.BlockSpec(wb.shape, const),
                  pl.BlockSpec(wo.shape, const),
                  pl.BlockSpec(wr.shape, const),
                  pl.BlockSpec(wr.shape, const),
                  pl.BlockSpec((1, LANES), const),
                  pl.BlockSpec((tm, tm), const)],
        out_specs=(pl.BlockSpec((1, tm, d), tile),
                   pl.BlockSpec((tm * SUB, LANES), lambda b, i: (b * (s // tm) + i, 0)),
                   pl.BlockSpec((1, tm, LANES), tile),
                   pl.BlockSpec((1, LANES), const)),
        scratch_shapes=[pltpu.VMEM((1, LANES), F32)],
        compiler_params=_cparams(("arbitrary", "arbitrary")),
        name="out_proj",
    )(x, o_a, o_b, gates, gates, mod3, mod3, mod3, norm2_g.reshape(1, d), wa, wb, wo, wr_hi, wr_lo, br, tril)


def _moe_dispatch_kernel(tm, n_tok, blk, dest_ref, zstart_ref, zcnt_ref, nused_ref, h_ref, xb_ref, smap_ref,
                         zbuf, sem, zsem):
    i = pl.program_id(0)
    n_steps = pl.num_programs(0)
    n_slots = smap_ref.shape[0]
    n_experts = zcnt_ref.shape[0]

    @pl.when(i == 0)
    def _():
        def init(s_, carry):
            smap_ref[s_] = 0
            return carry
        lax.fori_loop(0, n_slots, init, 0)
        zbuf[...] = jnp.zeros_like(zbuf)

        def zero_row(e, j):
            dst = pl.multiple_of((zstart_ref[e] + j) * SUB, SUB)
            return pltpu.make_async_copy(zbuf.at[pl.ds(0, SUB), :], xb_ref.at[pl.ds(dst, SUB), :], zsem)

        def zero_block(b):
            dst = pl.multiple_of(b * (blk * SUB), blk * SUB)
            return pltpu.make_async_copy(zbuf, xb_ref.at[pl.ds(dst, blk * SUB), :], zsem)

        for e in range(n_experts):
            def zstart(j, carry, e=e):
                zero_row(e, j).start()
                smap_ref[zstart_ref[e] + j] = TOP_K * n_tok + e * blk + j
                return carry
            lax.fori_loop(0, zcnt_ref[e], zstart, 0)

        def bstart(b, carry):
            zero_block(b).start()
            return carry
        lax.fori_loop(nused_ref[0], n_slots // blk, bstart, 0)
        for e in range(n_experts):
            def zwait(j, carry, e=e):
                zero_row(e, j).wait()
                return carry
            lax.fori_loop(0, zcnt_ref[e], zwait, 0)

        def bwait(b, carry):
            zero_block(b).wait()
            return carry
        lax.fori_loop(nused_ref[0], n_slots // blk, bwait, 0)

    slot = i % 2
    for r in range(tm):
        tok = i * tm + r
        src = pl.multiple_of(tok * SUB, SUB)
        for k in range(TOP_K):
            d = dest_ref[tok * TOP_K + k]
            dst = pl.multiple_of(d * SUB, SUB)
            pltpu.make_async_copy(h_ref.at[pl.ds(src, SUB), :], xb_ref.at[pl.ds(dst, SUB), :], sem.at[slot]).start()
            smap_ref[d] = k * n_tok + tok

    def wait_step(s_):
        for _ in range(TOP_K):
            pltpu.make_async_copy(h_ref.at[pl.ds(0, tm * SUB), :], xb_ref.at[pl.ds(0, tm * SUB), :],
                                  sem.at[s_]).wait()

    @pl.when(i > 0)
    def _():
        wait_step(1 - slot)

    @pl.when(i == n_steps - 1)
    def _():
        wait_step(slot)


def _moe_dispatch(h2, dest, zstart, zcnt, n_used, n_slots, blk, tm):
    n_tok = h2.shape[0] // SUB
    grid_spec = pltpu.PrefetchScalarGridSpec(
        num_scalar_prefetch=4,
        grid=(n_tok // tm,),
        in_specs=[pl.BlockSpec(memory_space=pl.ANY)],
        out_specs=(pl.BlockSpec(memory_space=pl.ANY),
                   pl.BlockSpec(memory_space=pltpu.SMEM)),
        scratch_shapes=[pltpu.VMEM((blk * SUB, LANES), F32),
                        pltpu.SemaphoreType.DMA((2,)),
                        pltpu.SemaphoreType.DMA],
    )
    return pl.pallas_call(
        functools.partial(_moe_dispatch_kernel, tm, n_tok, blk),
        out_shape=(jax.ShapeDtypeStruct((n_slots * SUB, LANES), F32),
                   jax.ShapeDtypeStruct((n_slots,), jnp.int32)),
        grid_spec=grid_spec,
        compiler_params=_cparams(("arbitrary",)),
        name="moe_dispatch",
    )(dest, zstart, zcnt, n_used, h2)


def _moe_expert_kernel(n_tok, blk_e_ref, nused_ref, smap_ref, x_ref, wg_ref, wu_ref, wd_ref, y_ref,
                       ystage, wgb, wub, wdb, sem):
    i = pl.program_id(0)
    n_used = nused_ref[0]
    blk = x_ref.shape[0] // SUB
    first_real_blocks = TOP_K * n_tok // blk
    first_spare = y_ref.shape[0] // SUB - blk

    def issue(b, slot, rows):
        for r in rows:
            t = jnp.where(b >= 0, smap_ref[jnp.maximum(b, 0) * blk + r], first_spare + r)
            dst = pl.multiple_of(t * SUB, SUB)
            pltpu.make_async_copy(ystage.at[slot, pl.ds(r * SUB, SUB), :], y_ref.at[pl.ds(dst, SUB), :],
                                  sem.at[slot]).start()

    def wait_block(slot):
        pltpu.make_async_copy(ystage.at[slot], y_ref.at[pl.ds(0, blk * SUB), :], sem.at[slot]).wait()

    @pl.when(i == 0)
    def _():
        ystage[...] = jnp.zeros_like(ystage)
        n_spare_blocks = y_ref.shape[0] // (blk * SUB) - first_real_blocks

        def spare_copy(c):
            dst = (first_real_blocks + c) * blk * SUB
            return pltpu.make_async_copy(ystage.at[0], y_ref.at[pl.ds(dst, blk * SUB), :], sem.at[0])

        for c in range(n_spare_blocks):
            spare_copy(c).start()
        for c in range(n_spare_blocks):
            spare_copy(c).wait()

    new_expert = (i == 0) | (blk_e_ref[i] != blk_e_ref[jnp.maximum(i - 1, 0)])

    @pl.when((i < n_used) & new_expert)
    def _():
        wgb[...] = wg_ref[0].astype(BF16)
        wub[...] = wu_ref[0].astype(BF16)
        wdb[...] = wd_ref[0].astype(BF16)

    @pl.when(i < n_used)
    def _():
        slot = i % 2
        pslot = 1 - slot

        @pl.when(i > 0)
        def _():
            wait_block(slot)

        q = blk // 4
        xb = _tiles_to_rows(x_ref, blk).astype(BF16)
        issue(i - 1, pslot, range(0, q))
        hg = _dot(xb, wgb[...])
        issue(i - 1, pslot, range(q, 2 * q))
        hu = _dot(xb, wub[...])
        issue(i - 1, pslot, range(2 * q, 3 * q))
        hid = (_silu(hg) * hu).astype(BF16)
        y = _dot(hid, wdb[...])
        issue(i - 1, pslot, range(3 * q, blk))
        _rows_to_tiles(ystage.at[slot], y)

        @pl.when(i == n_used - 1)
        def _():
            issue(i, slot, range(blk))
            wait_block(pslot)
            wait_block(slot)


def _moe_experts(xb, smap, blk_e, n_used, w_gate, w_up, w_down, blk, n_tok):
    d = w_gate.shape[1]
    assert d == SUB * LANES and xb.shape[1] == LANES
    n_slots = smap.shape[0]
    nb = n_slots // blk
    f = w_gate.shape[-1]
    n_experts = w_gate.shape[0]
    assert (TOP_K * n_tok) % blk == 0
    n_tiles = TOP_K * n_tok + (n_experts + 1) * blk
    grid_spec = pltpu.PrefetchScalarGridSpec(
        num_scalar_prefetch=3,
        grid=(nb,),
        in_specs=[pl.BlockSpec((blk * SUB, LANES), lambda i, be, nu, sm: (jnp.minimum(i, nu[0] - 1), 0)),
                  pl.BlockSpec((1, d, f), lambda i, be, nu, sm: (be[i], 0, 0)),
                  pl.BlockSpec((1, d, f), lambda i, be, nu, sm: (be[i], 0, 0)),
                  pl.BlockSpec((1, f, d), lambda i, be, nu, sm: (be[i], 0, 0))],
        out_specs=pl.BlockSpec(memory_space=pl.ANY),
        scratch_shapes=[pltpu.VMEM((2, blk * SUB, LANES), F32),
                        pltpu.VMEM((d, f), BF16),
                        pltpu.VMEM((d, f), BF16),
                        pltpu.VMEM((f, d), BF16),
                        pltpu.SemaphoreType.DMA((2,))],
    )
    return pl.pallas_call(
        functools.partial(_moe_expert_kernel, n_tok),
        out_shape=jax.ShapeDtypeStruct((n_tiles * SUB, LANES), F32),
        grid_spec=grid_spec,
        compiler_params=_cparams(("arbitrary",)),
        name="moe_experts",
    )(blk_e, n_used, smap, xb, w_gate, w_up, w_down)


def _moe_combine_kernel(x1_ref, route_ref, gt2_ref, fg_ref, y1_ref, y2_ref, o_ref):
    tm = x1_ref.shape[0]
    route = route_ref[...]
    moe = route[:, 0:1] * _tiles_to_rows(y1_ref, tm) + route[:, 1:2] * _tiles_to_rows(y2_ref, tm)
    xo = x1_ref[...] + gt2_ref[0] * moe
    ms = jnp.mean(xo * xo, axis=-1, keepdims=True)
    o_ref[...] = xo * lax.rsqrt(ms + NORM_EPS) * fg_ref[...]


def _moe_combine(x1, route, yt, mod3, final_g, s, tm):
    n, d = x1.shape
    tiles_per_batch = s // tm
    n_steps = n // tm
    return pl.pallas_call(
        _moe_combine_kernel,
        out_shape=jax.ShapeDtypeStruct((n, d), F32),
        grid=(n_steps,),
        in_specs=[pl.BlockSpec((tm, d), lambda i: (i, 0)),
                  pl.BlockSpec((tm, LANES), lambda i: (i, 0)),
                  pl.BlockSpec((1, 1, d), lambda i: ((i // tiles_per_batch) * 6 + 5, 0, 0)),
                  pl.BlockSpec((1, d), lambda i: (0, 0)),
                  pl.BlockSpec((tm * SUB, LANES), lambda i: (i, 0)),
                  pl.BlockSpec((tm * SUB, LANES), lambda i: (n_steps + i, 0))],
        out_specs=pl.BlockSpec((tm, d), lambda i: (i, 0)),
        compiler_params=_cparams(("arbitrary",)),
        name="moe_combine",
    )(x1, route, mod3, final_g.reshape(1, d), yt, yt)


def _pick(n, candidates):
    for t in candidates:
        if n % t == 0:
            return t
    raise ValueError(f"no tile in {candidates} divides {n}")


def kernel(x, c, ada_w, ada_b, norm1_g, w_in, hg_lb, hg_norm_g, rw_mu, rw_w0, rw_w2, rw_a0, rw_a2, rw_g2, rw_kk, rw_ka, rw_rk, rw_gn_w, rw_gn_b, w_proj_a, w_proj_b, w_out, norm2_g, router_g_w, router_g_b, router_e_w, router_e_b, exp_w_gate, exp_w_up, exp_w_down, final_g):
    bsz, s, d = x.shape
    depth = ada_w.shape[0]
    hg_f = hg_lb.shape[-1]
    hg_w = hg_norm_g.shape[-1]
    rw_w = rw_w0.shape[-1]
    rw_cols = rw_mu.shape[-1]
    n_groups = router_g_w.shape[-1]
    n_experts = router_e_w.shape[-1]
    assert hg_f == hg_w and s % CHUNK == 0 and n_experts + n_groups <= LANES and d == SUB * LANES

    lb_all = jnp.cumsum(jax.nn.softmax(hg_lb.astype(F32), axis=0), axis=0)
    n = bsz * s
    blk = 256
    n_blocks = (n * TOP_K + n_experts * blk) // blk
    for l in range(depth):
        mod = _ada_mod(c, ada_w[l], ada_b[l])
        mod3 = mod.reshape(bsz * 6, 1, d)

        hg_cols = 2 * hg_f + 2 * hg_w
        rw_pad = -(-rw_cols // 256) * 256
        wl = w_in[l]
        w_hg = wl[:, :hg_cols].astype(BF16)
        w_rw = jnp.zeros((d, rw_pad), BF16).at[:, :rw_cols].set(wl[:, hg_cols:hg_cols + rw_cols].astype(BF16))
        w_gt = wl[:, hg_cols + rw_cols:].astype(BF16)
        hg, rw, gates = _in_proj(x, mod3, norm1_g[l], w_hg, w_rw, w_gt, _pick(s, (512, 256, 128, 64)))

        o_a = _hgrn2(hg, lb_all[l], hg_norm_g[l], _pick(s, (512, 256, 128, 64)))
        o_b = _rwkv7(rw, rw_mu[l], rw_w0[l], rw_w2[l], rw_a0[l], rw_a2[l], rw_g2[l],
                     rw_kk[l], rw_ka[l], rw_rk[l].reshape(-1), rw_gn_w[l], rw_gn_b[l])

        wr = jnp.zeros((d, LANES), F32).at[:, :n_experts].set(router_e_w[l])
        wr = wr.at[:, n_experts:n_experts + n_groups].set(router_g_w[l])
        br = jnp.zeros((1, LANES), F32).at[0, :n_experts].set(router_e_b[l])
        br = br.at[0, n_experts:n_experts + n_groups].set(router_g_b[l])
        x1, h2, route, counts = _out_proj(
            x, o_a, o_b, gates, mod3, norm2_g[l],
            w_proj_a[l].astype(BF16), w_proj_b[l].astype(BF16), w_out[l].astype(BF16),
            wr, br, n_groups, n_experts, _pick(s, (512, 256, 128, 64)))

        route2 = route.reshape(n, LANES)
        eid = route2[:, 2:4].astype(jnp.int32)
        rank = route2[:, 4:6].astype(jnp.int32)
        cnt = counts[0, :n_experts].astype(jnp.int32)
        padded = (cnt + blk - 1) // blk * blk
        pad_end = jnp.cumsum(padded)
        pad_start = pad_end - padded
        dest = pad_start[eid] + rank
        blk_start = jnp.arange(n_blocks, dtype=jnp.int32) * blk
        blk_e = jnp.minimum(jnp.sum((pad_end[None, :] <= blk_start[:, None]).astype(jnp.int32), axis=1),
                            n_experts - 1)
        n_used = (pad_end[-1:] // blk).astype(jnp.int32)
        tm = _pick(s, (256, 128, 64))
        xb, smap = _moe_dispatch(h2, dest.reshape(-1), pad_start + cnt, padded - cnt, n_used,
                                 n_blocks * blk, blk, tm)
        yt = _moe_experts(xb, smap, blk_e, n_used, exp_w_gate[l], exp_w_up[l], exp_w_down[l], blk, n)
        last = l == depth - 1
        assert last, "the final RMSNorm is fused into the last layer's combine"
        out = _moe_combine(x1.reshape(n, d), route2, yt, mod3, final_g, s, tm)
        x = out.reshape(bsz, s, d)
    return x
```

```python
import functools

import numpy as np
import jax
import jax.numpy as jnp
from jax import lax
from jax.experimental import pallas as pl
from jax.experimental.pallas import tpu as pltpu

F32 = jnp.float32
BF16 = jnp.bfloat16
HIGHEST = lax.Precision.HIGHEST

NORM_EPS = 1e-6
HG_HEAD = 128
RW_HEAD = 64
RW_GN_EPS = 64e-5
TOP_K = 2
CHUNK = 64
LANES = 128
SUB = 8
VMEM_LIMIT = 56 * 1024 * 1024

NT = (((1,), (1,)), ((), ()))
TN = (((0,), (0,)), ((), ()))


def _dot(a, b, dims=None, precision=None):
    if dims is None:
        return jnp.dot(a, b, preferred_element_type=F32, precision=precision)
    return lax.dot_general(a, b, dims, preferred_element_type=F32, precision=precision)


def _split3(x):
    hi = x.astype(BF16)
    r1 = x - hi.astype(F32)
    mid = r1.astype(BF16)
    lo = (r1 - mid.astype(F32)).astype(BF16)
    return hi, mid, lo


def _dot_exact_lhs(m_bf16, x):
    hi, mid, lo = _split3(x)
    return _dot(m_bf16, hi) + _dot(m_bf16, mid) + _dot(m_bf16, lo)


def _dot_exact_rhs(x, m_bf16):
    hi, mid, lo = _split3(x)
    return _dot(hi, m_bf16) + _dot(mid, m_bf16) + _dot(lo, m_bf16)


def _sigmoid(x):
    return 1.0 / (1.0 + jnp.exp(-x))


def _silu(x):
    return x * _sigmoid(x)


def _rows_to_tiles(ref, val):
    m = val.shape[0]
    for j in range(SUB):
        ref[pl.ds(j, m, stride=SUB), :] = val[:, j * LANES:(j + 1) * LANES]


def _tiles_to_rows(ref, m, base=0):
    return jnp.concatenate([ref[pl.ds(base * SUB + j, m, stride=SUB), :] for j in range(SUB)], axis=1)


def _cparams(sem):
    return pltpu.CompilerParams(dimension_semantics=sem, vmem_limit_bytes=VMEM_LIMIT)


def _ada_kernel(c_ref, w_ref, b_ref, o_ref):
    c = c_ref[...]
    o_ref[...] = _dot(_silu(c), w_ref[...], precision=HIGHEST) + b_ref[...]


def _ada_mod(c, w, b):
    bsz, d = c.shape
    n = w.shape[1]
    rows = 8
    cp = jnp.zeros((rows, d), F32).at[:bsz].set(c)
    tn = 1536
    out = pl.pallas_call(
        _ada_kernel,
        out_shape=jax.ShapeDtypeStruct((rows, n), F32),
        grid=(n // tn,),
        in_specs=[pl.BlockSpec((rows, d), lambda j: (0, 0)),
                  pl.BlockSpec((d, tn), lambda j: (0, j)),
                  pl.BlockSpec((1, tn), lambda j: (0, j))],
        out_specs=pl.BlockSpec((rows, tn), lambda j: (0, j)),
        compiler_params=_cparams(("arbitrary",)),
        name="ada_mod",
    )(cp, w, b.reshape(1, n))
    return out[:bsz]


def _in_proj_kernel(x_ref, sh_ref, sc_ref, g_ref, whg_ref, wrw_ref, wgt_ref, hg_ref, rw_ref, gt_ref):
    x = x_ref[0]
    ms = jnp.mean(x * x, axis=-1, keepdims=True)
    h = (x * lax.rsqrt(ms + NORM_EPS) * g_ref[...]) * (1.0 + sc_ref[0]) + sh_ref[0]
    hb = h.astype(BF16)
    step = 512
    for n0 in range(0, whg_ref.shape[1], step):
        hg_ref[0, :, n0:n0 + step] = _dot(hb, whg_ref[:, n0:n0 + step])
    for n0 in range(0, wrw_ref.shape[1], 256):
        rw_ref[0, :, n0:n0 + 256] = _dot(hb, wrw_ref[:, n0:n0 + 256])
    for n0 in range(0, wgt_ref.shape[1], step):
        gt_ref[0, :, n0:n0 + step] = _sigmoid(_dot(hb, wgt_ref[:, n0:n0 + step])).astype(BF16)


def _in_proj(x, mod3, norm_g, w_hg, w_rw, w_gt, tm):
    bsz, s, d = x.shape
    n_hg, n_rw, n_gt = w_hg.shape[1], w_rw.shape[1], w_gt.shape[1]
    const = lambda b, i: (0, 0)
    return pl.pallas_call(
        _in_proj_kernel,
        out_shape=(jax.ShapeDtypeStruct((bsz, s, n_hg), F32),
                   jax.ShapeDtypeStruct((bsz, s, n_rw), F32),
                   jax.ShapeDtypeStruct((bsz, s, n_gt), BF16)),
        grid=(bsz, s // tm),
        in_specs=[pl.BlockSpec((1, tm, d), lambda b, i: (b, i, 0)),
                  pl.BlockSpec((1, 1, d), lambda b, i: (b * 6 + 0, 0, 0)),
                  pl.BlockSpec((1, 1, d), lambda b, i: (b * 6 + 1, 0, 0)),
                  pl.BlockSpec((1, d), const),
                  pl.BlockSpec((d, n_hg), const),
                  pl.BlockSpec((d, n_rw), const),
                  pl.BlockSpec((d, n_gt), const)],
        out_specs=(pl.BlockSpec((1, tm, n_hg), lambda b, i: (b, i, 0)),
                   pl.BlockSpec((1, tm, n_rw), lambda b, i: (b, i, 0)),
                   pl.BlockSpec((1, tm, n_gt), lambda b, i: (b, i, 0))),
        compiler_params=_cparams(("arbitrary", "arbitrary")),
        name="in_proj",
    )(x, mod3, mod3, norm_g.reshape(1, d), w_hg, w_rw, w_gt)


_HG_LEVELS = (32, 16, 8, 4, 2, 1)


def _hgrn2_consts(width):
    c = CHUNK
    t = np.arange(c)[:, None]
    s = np.arange(c)[None, :]
    blocks = [(s <= t), (s > t)]
    lvl_masks = []
    right = []
    for h in _HG_LEVELS:
        m = (t // (2 * h)) * 2 * h + h
        is_r = (t & h) != 0
        blk = np.where(is_r, (s >= m) & (s <= t), (s > t) & (s <= m - 1))
        blocks.append(blk)
        lvl_masks.append(is_r & ((s & h) == 0) & ((t // (2 * h)) == (s // (2 * h))))
        right.append(np.broadcast_to(is_r, (c, width)))
    mst = np.concatenate(blocks, axis=0).astype(np.float32)
    lm = np.stack([np.eye(c, dtype=bool)] + lvl_masks).astype(np.float32)
    rm = np.stack(right).astype(np.float32)
    return jnp.asarray(mst, BF16), jnp.asarray(lm, F32), jnp.asarray(rm, F32)


def _hgrn2_kernel(q_ref, f_ref, i_ref, g_ref, lb_ref, ng_ref, mst_ref, lm_ref, rm_ref, o_ref, st_ref):
    c = CHUNK
    n_heads = q_ref.shape[2] // HG_HEAD
    n_chunks = q_ref.shape[1] // c

    @pl.when(pl.program_id(1) == 0)
    def _():
        st_ref[...] = jnp.zeros_like(st_ref)

    mst = mst_ref[...]
    lb = lb_ref[...]
    ng = ng_ref[...]
    heads = [slice(hd * HG_HEAD, (hd + 1) * HG_HEAD) for hd in range(n_heads)]

    def chunk_body(ci, carry):
        r0 = pl.multiple_of(ci * c, c)
        rows = pl.ds(r0, c)
        q = _silu(q_ref[0, rows, :])
        f = lb + (1.0 - lb) * _sigmoid(f_ref[0, rows, :])
        lf = jnp.log(f)
        k = 1.0 - f
        vb = i_ref[0, rows, :].astype(BF16)
        ex = jnp.exp(_dot_exact_lhs(mst, lf))
        ex_cum = ex[0:c]
        qd = (q * ex_cum).astype(BF16)
        kr = (k * ex[c:2 * c]).astype(BF16)
        qb = q.astype(BF16)
        kb = k.astype(BF16)
        sts = [st_ref[hd] for hd in range(n_heads)]
        o = [_dot(qd[:, ls], st.astype(BF16), NT) for ls, st in zip(heads, sts)]
        sc = [lm_ref[0] * _dot(qb[:, ls], kb[:, ls], NT) for ls in heads]
        dqk = q - k
        for li in range(len(_HG_LEVELS)):
            g_l = ((k + rm_ref[li] * dqk) * ex[(2 + li) * c:(3 + li) * c]).astype(BF16)
            sc = [s_h + lm_ref[li + 1] * _dot(g_l[:, ls], g_l[:, ls], NT) for s_h, ls in zip(sc, heads)]
        o = [o_h + _dot(s_h.astype(BF16), vb[:, ls]) for o_h, s_h, ls in zip(o, sc, heads)]
        for hd, ls in enumerate(heads):
            st_ref[hd] = sts[hd] * ex_cum[c - 1:c, ls] + _dot(vb[:, ls], kr[:, ls], TN)
        on = [o_h * lax.rsqrt(jnp.mean(o_h * o_h, axis=-1, keepdims=True) + NORM_EPS) for o_h in o]
        o_full = jnp.concatenate(on, axis=1) * ng
        o_ref[0, rows, :] = (o_full * _silu(g_ref[0, rows, :])).astype(o_ref.dtype)
        return carry

    lax.fori_loop(0, n_chunks, chunk_body, 0)


def _hgrn2(hg, lb, norm_g, ts):
    bsz, s, n4 = hg.shape
    w = n4 // 4
    mst, lm, rm = _hgrn2_consts(w)
    n_heads = w // HG_HEAD
    const2 = lambda b, i: (0, 0)
    const3 = lambda b, i: (0, 0, 0)
    return pl.pallas_call(
        _hgrn2_kernel,
        out_shape=jax.ShapeDtypeStruct((bsz, s, w), BF16),
        grid=(bsz, s // ts),
        in_specs=[pl.BlockSpec((1, ts, w), lambda b, i: (b, i, 0)),
                  pl.BlockSpec((1, ts, w), lambda b, i: (b, i, 1)),
                  pl.BlockSpec((1, ts, w), lambda b, i: (b, i, 2)),
                  pl.BlockSpec((1, ts, w), lambda b, i: (b, i, 3)),
                  pl.BlockSpec((1, w), const2),
                  pl.BlockSpec((1, w), const2),
                  pl.BlockSpec(mst.shape, const2),
                  pl.BlockSpec(lm.shape, const3),
                  pl.BlockSpec(rm.shape, const3)],
        out_specs=pl.BlockSpec((1, ts, w), lambda b, i: (b, i, 0)),
        scratch_shapes=[pltpu.VMEM((n_heads, HG_HEAD, HG_HEAD), F32)],
        compiler_params=_cparams(("arbitrary", "arbitrary")),
        name="hgrn2",
    )(hg, hg, hg, hg, lb.reshape(1, w), norm_g.reshape(1, w), mst, lm, rm)


def _rwkv_consts(width):
    c = CHUNK
    t = np.arange(c)[:, None]
    s = np.arange(c)[None, :]
    tri = (s <= t).astype(np.float32)
    tt = np.arange(2 * c)[:, None]
    ss = np.arange(2 * c)[None, :]
    same = (tt // c) == (ss // c)
    strict = same & ((ss % c) < (tt % c))
    incl = same & ((ss % c) <= (tt % c))
    hsum = (np.arange(width)[:, None] // RW_HEAD) == (np.arange(width)[None, :] // RW_HEAD)
    return (jnp.asarray(tri, BF16), jnp.asarray(strict.astype(np.float32), F32),
            jnp.asarray(incl.astype(np.float32), F32), jnp.asarray(hsum.astype(np.float32), BF16))


def _rwkv7_kernel(p_ref, mu_ref, w0_ref, a0_ref, kk_ref, ka_ref, rk_ref, gnw_ref, gnb_ref,
                  w2_ref, a2_ref, g2_ref, tri_ref, sm_ref, im_ref, hs_ref,
                  o_ref, carry_ref, zt_ref):
    c = CHUNK
    nb = p_ref.shape[0]
    width = o_ref.shape[2]
    n_pairs = width // LANES

    @pl.when(pl.program_id(0) == 0)
    def _():
        carry_ref[...] = jnp.zeros_like(carry_ref)
        zt_ref[...] = jnp.zeros_like(zt_ref)

    hs = hs_ref[...]
    tri = tri_ref[...]
    smask = sm_ref[...] > 0
    imask = im_ref[...] > 0
    lane = lax.broadcasted_iota(jnp.int32, (c, LANES), 1)
    m0 = (lane < RW_HEAD).astype(F32)
    m1 = 1.0 - m0

    def stack(x):
        return jnp.concatenate([x * m0, x * m1], axis=0)

    rs, k2s, vs_, gs = [], [], [], []
    units = []
    for b in range(nb):
        p = p_ref[b]
        row = lax.broadcasted_iota(jnp.int32, p.shape, 0)
        prev = jnp.where(row == 0, carry_ref[b], pltpu.roll(p, 1, 0))
        carry_ref[b] = p[c - 1:c, :]
        xs = p + mu_ref[...] * (prev - p)
        r = xs[:, 0:width]
        k = xs[:, width:2 * width]
        v = xs[:, 2 * width:3 * width]
        slab = xs[:, 3 * width:]
        nz = -(w0_ref[...] + _dot(jnp.tanh(slab).astype(BF16), w2_ref[...]))
        softplus = jnp.maximum(nz, 0.0) + jnp.log(1.0 + jnp.exp(-jnp.abs(nz)))
        ld = -jnp.exp(-softplus - 0.5)
        a = _sigmoid(a0_ref[...] + _dot(slab.astype(BF16), a2_ref[...]))
        gs.append(_dot(_sigmoid(slab).astype(BF16), g2_ref[...]))
        kk0 = k * kk_ref[...]
        kk = kk0 * lax.rsqrt(jnp.maximum(_dot_exact_rhs(kk0 * kk0, hs), 1e-24))
        k2 = k * (1.0 + (a - 1.0) * ka_ref[...])
        a_in = -kk
        b_in = kk * a
        cum = _dot_exact_lhs(tri, ld)
        cum_t = cum[c - 1:c, :]
        e_c = jnp.exp(cum)
        e_nc = jnp.exp(-cum)
        e_rem = jnp.exp(cum_t - cum)
        at_f = a_in * jnp.exp(cum - ld)
        rt_f = r * e_c
        kt_f = k2 * e_nc
        bt_f = b_in * e_nc
        kh_f = k2 * e_rem
        bh_f = b_in * e_rem
        p_t = jnp.exp(cum_t)
        rs.append(r)
        k2s.append(k2)
        vs_.append(v)
        for pi in range(n_pairs):
            ls = slice(pi * LANES, (pi + 1) * LANES)
            units.append(dict(
                b=b, pi=pi,
                at=stack(at_f[:, ls]).astype(BF16), rt=stack(rt_f[:, ls]).astype(BF16),
                kt=stack(kt_f[:, ls]).astype(BF16), bt=stack(bt_f[:, ls]).astype(BF16),
                kh=stack(kh_f[:, ls]).astype(BF16), bh=stack(bh_f[:, ls]).astype(BF16),
                vs=stack(v[:, ls]).astype(BF16), p_t=p_t[:, ls]))

    for u in units:
        lhs = jnp.concatenate([u['at'], u['rt']], axis=0)
        u['gk'] = _dot(lhs, u['kt'], NT)
        u['gb'] = _dot(lhs, u['bt'], NT)
    for u in units:
        gk, gb = u.pop('gk'), u.pop('gb')
        a_kk = jnp.concatenate([jnp.where(smask, gk[:2 * c], 0.0), jnp.where(imask, gk[2 * c:], 0.0)], axis=0)
        u['pw'] = jnp.where(smask, gb[:2 * c], 0.0).astype(BF16)
        u['a_rb'] = jnp.where(imask, gb[2 * c:], 0.0).astype(BF16)
        u['a_kk'] = a_kk.astype(BF16)
    for u in units:
        av = _dot(u.pop('a_kk'), u['vs'])
        u['arkv'] = av[2 * c:]
        u['x'] = jnp.concatenate([u['at'].astype(F32), av[:2 * c]], axis=1)
    n_lvl = int(np.log2(c))
    for lvl in range(n_lvl):
        for u in units:
            u['x'] = u['x'] + _dot(u['pw'], u['x'].astype(BF16))
        if lvl + 1 < n_lvl:
            for u in units:
                u['pw'] = _dot(u['pw'], u['pw']).astype(BF16)
    for u in units:
        x = u.pop('x')
        u['zt'] = zt_ref[u['b'], u['pi']]
        u['uy'] = _dot(jnp.concatenate([x[:, :LANES].astype(BF16), u['rt']], axis=0), u['zt'].astype(BF16), NT)
        u['u_loc'] = x[:, LANES:]
    for u in units:
        uy = u.pop('uy')
        u['u'] = (uy[:2 * c] + u.pop('u_loc')).astype(BF16)
        u['y0'] = uy[2 * c:] + u.pop('arkv')
    for u in units:
        u['y'] = u.pop('y0') + _dot(u['a_rb'], u['u'])
        upd = _dot(jnp.concatenate([u['u'], u['vs']], axis=0), jnp.concatenate([u['bh'], u['kh']], axis=0), TN)
        zt_ref[u['b'], u['pi']] = u['zt'] * u['p_t'] + upd

    inv_n = 1.0 / RW_HEAD
    for b in range(nb):
        ys = [u['y'] for u in units if u['b'] == b]
        y = jnp.concatenate([yy[:c] + yy[c:] for yy in ys], axis=1)
        mean = _dot_exact_rhs(y, hs) * inv_n
        d = y - mean
        var = _dot_exact_rhs(d * d, hs) * inv_n
        yn = d * lax.rsqrt(var + RW_GN_EPS) * gnw_ref[...] + gnb_ref[...]
        bonus = _dot_exact_rhs(rs[b] * k2s[b] * rk_ref[...], hs) * vs_[b]
        o_ref[b] = ((yn + bonus) * gs[b]).astype(o_ref.dtype)


def _rwkv7(rw, mu, w0, w2, a0, a2, g2, k_k, k_a, r_k, gn_w, gn_b):
    bsz, s, cols = rw.shape
    width = w0.shape[-1]
    n_pairs = width // LANES
    slab = cols - 3 * width
    dl, al, gl = w2.shape[0], a2.shape[0], g2.shape[0]
    w2f = jnp.zeros((slab, width), F32).at[0:dl].set(w2).astype(BF16)
    a2f = jnp.zeros((slab, width), F32).at[dl:dl + al].set(a2).astype(BF16)
    g2f = jnp.zeros((slab, width), F32).at[dl + al:dl + al + gl].set(g2).astype(BF16)
    mup = jnp.zeros((1, cols), F32).at[0, :mu.shape[-1]].set(mu)
    tri, sm, im, hs = _rwkv_consts(width)
    row = lambda x: x.reshape(1, width)
    const = lambda i: (0, 0)
    vec = pl.BlockSpec((1, width), const)
    return pl.pallas_call(
        _rwkv7_kernel,
        out_shape=jax.ShapeDtypeStruct((bsz, s, width), BF16),
        grid=(s // CHUNK,),
        in_specs=[pl.BlockSpec((bsz, CHUNK, cols), lambda i: (0, i, 0)),
                  pl.BlockSpec((1, cols), const),
                  vec, vec, vec, vec, vec, vec, vec,
                  pl.BlockSpec((slab, width), const),
                  pl.BlockSpec((slab, width), const),
                  pl.BlockSpec((slab, width), const),
                  pl.BlockSpec(tri.shape, const),
                  pl.BlockSpec(sm.shape, const),
                  pl.BlockSpec(im.shape, const),
                  pl.BlockSpec(hs.shape, const)],
        out_specs=pl.BlockSpec((bsz, CHUNK, width), lambda i: (0, i, 0)),
        scratch_shapes=[pltpu.VMEM((bsz, 1, cols), F32),
                        pltpu.VMEM((bsz, n_pairs, LANES, LANES), F32)],
        compiler_params=_cparams(("arbitrary",)),
        name="rwkv7",
    )(rw, mup, row(w0), row(a0), row(k_k), row(k_a), row(r_k), row(gn_w), row(gn_b),
      w2f, a2f, g2f, tri, sm, im, hs)


def _out_proj_kernel(n_groups, n_experts,
                     x_ref, oa_ref, ob_ref, ga_ref, gb_ref, gt1_ref, sc2_ref, sh2_ref, g2_ref,
                     wa_ref, wb_ref, wo_ref, wr_ref, wrl_ref, br_ref, tril_ref,
                     x1_ref, h2_ref, route_ref, cnt_ref, carry_ref):
    tm = x_ref.shape[1]
    first = (pl.program_id(0) == 0) & (pl.program_id(1) == 0)

    @pl.when(first)
    def _():
        carry_ref[...] = jnp.zeros_like(carry_ref)

    pa = _dot(oa_ref[0], wa_ref[...])
    pb = _dot(ob_ref[0], wb_ref[...])
    mixed = ga_ref[0].astype(F32) * pa + gb_ref[0].astype(F32) * pb
    x1 = x_ref[0] + gt1_ref[0] * _dot(mixed.astype(BF16), wo_ref[...])
    x1_ref[0] = x1
    ms = jnp.mean(x1 * x1, axis=-1, keepdims=True)
    h2 = (x1 * lax.rsqrt(ms + NORM_EPS) * g2_ref[...]) * (1.0 + sc2_ref[0]) + sh2_ref[0]
    _rows_to_tiles(h2_ref, h2)

    h2_hi = h2.astype(BF16)
    h2_lo = (h2 - h2_hi.astype(F32)).astype(BF16)
    logits = (_dot(h2_hi, wr_ref[...]) + _dot(h2_lo, wr_ref[...]) + _dot(h2_hi, wrl_ref[...])) + br_ref[...]
    lane = lax.broadcasted_iota(jnp.int32, logits.shape, 1)
    neg = jnp.float32(-jnp.inf)
    big = jnp.int32(1 << 20)
    eg = n_experts // n_groups
    is_g = (lane >= n_experts) & (lane < n_experts + n_groups)
    lg = jnp.where(is_g, logits, neg)
    mg = jnp.max(lg, axis=-1, keepdims=True)
    p_grp = 1.0 / jnp.sum(jnp.where(is_g, jnp.exp(lg - mg), 0.0), axis=-1, keepdims=True)
    gidx = jnp.min(jnp.where(lg == mg, lane, big), axis=-1, keepdims=True) - n_experts
    sel = (lane >= gidx * eg) & (lane < gidx * eg + eg)
    le = jnp.where(sel, logits, neg)
    me = jnp.max(le, axis=-1, keepdims=True)
    pe_un = jnp.where(sel, jnp.exp(le - me), 0.0)
    pe = jnp.where(sel, pe_un / jnp.sum(pe_un, axis=-1, keepdims=True), -1.0)
    v1 = jnp.max(pe, axis=-1, keepdims=True)
    i1 = jnp.min(jnp.where(pe == v1, lane, big), axis=-1, keepdims=True)
    pe2 = jnp.where(lane == i1, -1.0, pe)
    v2 = jnp.max(pe2, axis=-1, keepdims=True)
    i2 = jnp.min(jnp.where(pe2 == v2, lane, big), axis=-1, keepdims=True)
    wsum = v1 + v2
    w1 = p_grp * v1 / wsum
    w2 = p_grp * v2 / wsum

    oh1 = (lane == i1).astype(F32)
    oh2 = (lane == i2).astype(F32)
    both = oh1 + oh2
    before = _dot(tril_ref[...], both.astype(BF16)) + carry_ref[...]
    rank1 = jnp.sum(oh1 * before, axis=-1, keepdims=True)
    rank2 = jnp.sum(oh2 * before, axis=-1, keepdims=True)
    carry_ref[...] = carry_ref[...] + jnp.sum(both, axis=0, keepdims=True)
    cnt_ref[...] = carry_ref[...]

    out = jnp.where(lane == 0, w1, 0.0)
    out = jnp.where(lane == 1, w2, out)
    out = jnp.where(lane == 2, i1.astype(F32), out)
    out = jnp.where(lane == 3, i2.astype(F32), out)
    out = jnp.where(lane == 4, rank1, out)
    out = jnp.where(lane == 5, rank2, out)
    route_ref[0] = out


def _out_proj(x, o_a, o_b, gates, mod3, norm2_g, wa, wb, wo, wr, br, n_groups, n_experts, tm):
    bsz, s, d = x.shape
    wdt = o_a.shape[-1]
    tril = jnp.asarray(np.tril(np.ones((tm, tm), np.float32), -1), BF16)
    wr_hi = wr.astype(BF16)
    wr_lo = (wr - wr_hi.astype(F32)).astype(BF16)
    const = lambda b, i: (0, 0)
    tile = lambda b, i: (b, i, 0)
    kern = functools.partial(_out_proj_kernel, n_groups, n_experts)
    return pl.pallas_call(
        kern,
        out_shape=(jax.ShapeDtypeStruct((bsz, s, d), F32),
                   jax.ShapeDtypeStruct((bsz * s * SUB, LANES), F32),
                   jax.ShapeDtypeStruct((bsz, s, LANES), F32),
                   jax.ShapeDtypeStruct((1, LANES), F32)),
        grid=(bsz, s // tm),
        in_specs=[pl.BlockSpec((1, tm, d), tile),
                  pl.BlockSpec((1, tm, wdt), tile),
                  pl.BlockSpec((1, tm, wdt), tile),
                  pl.BlockSpec((1, tm, d), lambda b, i: (b, i, 0)),
                  pl.BlockSpec((1, tm, d), lambda b, i: (b, i, 1)),
                  pl.BlockSpec((1, 1, d), lambda b, i: (b * 6 + 2, 0, 0)),
                  pl.BlockSpec((1, 1, d), lambda b, i: (b * 6 + 4, 0, 0)),
                  pl.BlockSpec((1, 1, d), lambda b, i: (b * 6 + 3, 0, 0)),
                  pl.BlockSpec((1, d), const),
                  pl.BlockSpec(wa.shape, const),
                  pl.BlockSpec(wb.shape, const),
                  pl.BlockSpec(wo.shape, const),
                  pl.BlockSpec(wr.shape, const),
                  pl.BlockSpec(wr.shape, const),
                  pl.BlockSpec((1, LANES), const),
                  pl.BlockSpec((tm, tm), const)],
        out_specs=(pl.BlockSpec((1, tm, d), tile),
                   pl.BlockSpec((tm * SUB, LANES), lambda b, i: (b * (s // tm) + i, 0)),
                   pl.BlockSpec((1, tm, LANES), tile),
                   pl.BlockSpec((1, LANES), const)),
        scratch_shapes=[pltpu.VMEM((1, LANES), F32)],
        compiler_params=_cparams(("arbitrary", "arbitrary")),
        name="out_proj",
    )(x, o_a, o_b, gates, gates, mod3, mod3, mod3, norm2_g.reshape(1, d), wa, wb, wo, wr_hi, wr_lo, br, tril)


def _moe_dispatch_kernel(tm, n_tok, blk, dest_ref, zstart_ref, zcnt_ref, nused_ref, h_ref, xb_ref, smap_ref,
                         zbuf, stage, sem, zsem):
    i = pl.program_id(0)
    n_steps = pl.num_programs(0)
    n_slots = smap_ref.shape[0]
    n_experts = zcnt_ref.shape[0]

    @pl.when(i == 0)
    def _():
        def init(s_, carry):
            smap_ref[s_] = 0
            return carry
        lax.fori_loop(0, n_slots, init, 0)
        zbuf[...] = jnp.zeros_like(zbuf)

        def zero_row(e, j):
            dst = pl.multiple_of((zstart_ref[e] + j) * SUB, SUB)
            return pltpu.make_async_copy(zbuf.at[pl.ds(0, SUB), :], xb_ref.at[pl.ds(dst, SUB), :], zsem)

        def zero_block(b):
            dst = pl.multiple_of(b * (blk * SUB), blk * SUB)
            return pltpu.make_async_copy(zbuf, xb_ref.at[pl.ds(dst, blk * SUB), :], zsem)

        for e in range(n_experts):
            def zstart(j, carry, e=e):
                zero_row(e, j).start()
                smap_ref[zstart_ref[e] + j] = TOP_K * n_tok + e * blk + j
                return carry
            lax.fori_loop(0, zcnt_ref[e], zstart, 0)

        def bstart(b, carry):
            zero_block(b).start()
            return carry
        lax.fori_loop(nused_ref[0], n_slots // blk, bstart, 0)
        for e in range(n_experts):
            def zwait(j, carry, e=e):
                zero_row(e, j).wait()
                return carry
            lax.fori_loop(0, zcnt_ref[e], zwait, 0)

        def bwait(b, carry):
            zero_block(b).wait()
            return carry
        lax.fori_loop(nused_ref[0], n_slots // blk, bwait, 0)

    slot = i % 2
    stage[slot] = h_ref[...]
    for r in range(tm):
        tok = i * tm + r
        for k in range(TOP_K):
            d = dest_ref[tok * TOP_K + k]
            dst = pl.multiple_of(d * SUB, SUB)
            pltpu.make_async_copy(stage.at[slot, pl.ds(r * SUB, SUB), :], xb_ref.at[pl.ds(dst, SUB), :],
                                  sem.at[slot]).start()
            smap_ref[d] = k * n_tok + tok

    def wait_step(s_):
        for _ in range(TOP_K):
            pltpu.make_async_copy(stage.at[s_], xb_ref.at[pl.ds(0, tm * SUB), :], sem.at[s_]).wait()

    @pl.when(i > 0)
    def _():
        wait_step(1 - slot)

    @pl.when(i == n_steps - 1)
    def _():
        wait_step(slot)


def _moe_dispatch(h2, dest, zstart, zcnt, n_used, n_slots, blk, tm):
    n_tok = h2.shape[0] // SUB
    grid_spec = pltpu.PrefetchScalarGridSpec(
        num_scalar_prefetch=4,
        grid=(n_tok // tm,),
        in_specs=[pl.BlockSpec((tm * SUB, LANES), lambda i, *_: (i, 0))],
        out_specs=(pl.BlockSpec(memory_space=pl.ANY),
                   pl.BlockSpec(memory_space=pltpu.SMEM)),
        scratch_shapes=[pltpu.VMEM((blk * SUB, LANES), F32),
                        pltpu.VMEM((2, tm * SUB, LANES), F32),
                        pltpu.SemaphoreType.DMA((2,)),
                        pltpu.SemaphoreType.DMA],
    )
    return pl.pallas_call(
        functools.partial(_moe_dispatch_kernel, tm, n_tok, blk),
        out_shape=(jax.ShapeDtypeStruct((n_slots * SUB, LANES), F32),
                   jax.ShapeDtypeStruct((n_slots,), jnp.int32)),
        grid_spec=grid_spec,
        compiler_params=_cparams(("arbitrary",)),
        name="moe_dispatch",
    )(dest, zstart, zcnt, n_used, h2)


def _moe_expert_kernel(n_tok, blk_e_ref, nused_ref, smap_ref, x_ref, wg_ref, wu_ref, wd_ref, y_ref,
                       ystage, wgb, wub, wdb, sem):
    i = pl.program_id(0)
    n_used = nused_ref[0]
    blk = x_ref.shape[0] // SUB
    first_real_blocks = TOP_K * n_tok // blk
    first_spare = y_ref.shape[0] // SUB - blk

    def issue(b, slot, rows):
        for r in rows:
            t = jnp.where(b >= 0, smap_ref[jnp.maximum(b, 0) * blk + r], first_spare + r)
            dst = pl.multiple_of(t * SUB, SUB)
            pltpu.make_async_copy(ystage.at[slot, pl.ds(r * SUB, SUB), :], y_ref.at[pl.ds(dst, SUB), :],
                                  sem.at[slot]).start()

    def wait_block(slot):
        pltpu.make_async_copy(ystage.at[slot], y_ref.at[pl.ds(0, blk * SUB), :], sem.at[slot]).wait()

    @pl.when(i == 0)
    def _():
        ystage[...] = jnp.zeros_like(ystage)
        n_spare_blocks = y_ref.shape[0] // (blk * SUB) - first_real_blocks

        def spare_copy(c):
            dst = (first_real_blocks + c) * blk * SUB
            return pltpu.make_async_copy(ystage.at[0], y_ref.at[pl.ds(dst, blk * SUB), :], sem.at[0])

        for c in range(n_spare_blocks):
            spare_copy(c).start()
        for c in range(n_spare_blocks):
            spare_copy(c).wait()

    new_expert = (i == 0) | (blk_e_ref[i] != blk_e_ref[jnp.maximum(i - 1, 0)])

    @pl.when((i < n_used) & new_expert)
    def _():
        wgb[...] = wg_ref[0].astype(BF16)
        wub[...] = wu_ref[0].astype(BF16)
        wdb[...] = wd_ref[0].astype(BF16)

    @pl.when(i < n_used)
    def _():
        slot = i % 2
        pslot = 1 - slot

        @pl.when(i > 0)
        def _():
            wait_block(slot)

        q = blk // 4
        xb = _tiles_to_rows(x_ref, blk).astype(BF16)
        issue(i - 1, pslot, range(0, q))
        hg = _dot(xb, wgb[...])
        issue(i - 1, pslot, range(q, 2 * q))
        hu = _dot(xb, wub[...])
        issue(i - 1, pslot, range(2 * q, 3 * q))
        hid = (_silu(hg) * hu).astype(BF16)
        y = _dot(hid, wdb[...])
        issue(i - 1, pslot, range(3 * q, blk))
        _rows_to_tiles(ystage.at[slot], y)

        @pl.when(i == n_used - 1)
        def _():
            issue(i, slot, range(blk))
            wait_block(pslot)
            wait_block(slot)


def _moe_experts(xb, smap, blk_e, n_used, w_gate, w_up, w_down, blk, n_tok):
    d = w_gate.shape[1]
    assert d == SUB * LANES and xb.shape[1] == LANES
    n_slots = smap.shape[0]
    nb = n_slots // blk
    f = w_gate.shape[-1]
    n_experts = w_gate.shape[0]
    assert (TOP_K * n_tok) % blk == 0
    n_tiles = TOP_K * n_tok + (n_experts + 1) * blk
    grid_spec = pltpu.PrefetchScalarGridSpec(
        num_scalar_prefetch=3,
        grid=(nb,),
        in_specs=[pl.BlockSpec((blk * SUB, LANES), lambda i, be, nu, sm: (jnp.minimum(i, nu[0] - 1), 0)),
                  pl.BlockSpec((1, d, f), lambda i, be, nu, sm: (be[i], 0, 0)),
                  pl.BlockSpec((1, d, f), lambda i, be, nu, sm: (be[i], 0, 0)),
                  pl.BlockSpec((1, f, d), lambda i, be, nu, sm: (be[i], 0, 0))],
        out_specs=pl.BlockSpec(memory_space=pl.ANY),
        scratch_shapes=[pltpu.VMEM((2, blk * SUB, LANES), F32),
                        pltpu.VMEM((d, f), BF16),
                        pltpu.VMEM((d, f), BF16),
                        pltpu.VMEM((f, d), BF16),
                        pltpu.SemaphoreType.DMA((2,))],
    )
    return pl.pallas_call(
        functools.partial(_moe_expert_kernel, n_tok),
        out_shape=jax.ShapeDtypeStruct((n_tiles * SUB, LANES), F32),
        grid_spec=grid_spec,
        compiler_params=_cparams(("arbitrary",)),
        name="moe_experts",
    )(blk_e, n_used, smap, xb, w_gate, w_up, w_down)


def _moe_combine_kernel(x1_ref, route_ref, gt2_ref, fg_ref, y1_ref, y2_ref, o_ref):
    tm = x1_ref.shape[0]
    route = route_ref[...]
    moe = route[:, 0:1] * _tiles_to_rows(y1_ref, tm) + route[:, 1:2] * _tiles_to_rows(y2_ref, tm)
    xo = x1_ref[...] + gt2_ref[0] * moe
    ms = jnp.mean(xo * xo, axis=-1, keepdims=True)
    o_ref[...] = xo * lax.rsqrt(ms + NORM_EPS) * fg_ref[...]


def _moe_combine(x1, route, yt, mod3, final_g, s, tm):
    n, d = x1.shape
    tiles_per_batch = s // tm
    n_steps = n // tm
    return pl.pallas_call(
        _moe_combine_kernel,
        out_shape=jax.ShapeDtypeStruct((n, d), F32),
        grid=(n_steps,),
        in_specs=[pl.BlockSpec((tm, d), lambda i: (i, 0)),
                  pl.BlockSpec((tm, LANES), lambda i: (i, 0)),
                  pl.BlockSpec((1, 1, d), lambda i: ((i // tiles_per_batch) * 6 + 5, 0, 0)),
                  pl.BlockSpec((1, d), lambda i: (0, 0)),
                  pl.BlockSpec((tm * SUB, LANES), lambda i: (i, 0)),
                  pl.BlockSpec((tm * SUB, LANES), lambda i: (n_steps + i, 0))],
        out_specs=pl.BlockSpec((tm, d), lambda i: (i, 0)),
        compiler_params=_cparams(("arbitrary",)),
        name="moe_combine",
    )(x1, route, mod3, final_g.reshape(1, d), yt, yt)


def _pick(n, candidates):
    for t in candidates:
        if n % t == 0:
            return t
    raise ValueError(f"no tile in {candidates} divides {n}")


def kernel(x, c, ada_w, ada_b, norm1_g, w_in, hg_lb, hg_norm_g, rw_mu, rw_w0, rw_w2, rw_a0, rw_a2, rw_g2, rw_kk, rw_ka, rw_rk, rw_gn_w, rw_gn_b, w_proj_a, w_proj_b, w_out, norm2_g, router_g_w, router_g_b, router_e_w, router_e_b, exp_w_gate, exp_w_up, exp_w_down, final_g):
    bsz, s, d = x.shape
    depth = ada_w.shape[0]
    hg_f = hg_lb.shape[-1]
    hg_w = hg_norm_g.shape[-1]
    rw_w = rw_w0.shape[-1]
    rw_cols = rw_mu.shape[-1]
    n_groups = router_g_w.shape[-1]
    n_experts = router_e_w.shape[-1]
    assert hg_f == hg_w and s % CHUNK == 0 and n_experts + n_groups <= LANES and d == SUB * LANES

    lb_all = jnp.cumsum(jax.nn.softmax(hg_lb.astype(F32), axis=0), axis=0)
    n = bsz * s
    blk = 256
    n_blocks = (n * TOP_K + n_experts * blk) // blk
    for l in range(depth):
        mod = _ada_mod(c, ada_w[l], ada_b[l])
        mod3 = mod.reshape(bsz * 6, 1, d)

        hg_cols = 2 * hg_f + 2 * hg_w
        rw_pad = -(-rw_cols // 256) * 256
        wl = w_in[l]
        w_hg = wl[:, :hg_cols].astype(BF16)
        w_rw = jnp.zeros((d, rw_pad), BF16).at[:, :rw_cols].set(wl[:, hg_cols:hg_cols + rw_cols].astype(BF16))
        w_gt = wl[:, hg_cols + rw_cols:].astype(BF16)
        hg, rw, gates = _in_proj(x, mod3, norm1_g[l], w_hg, w_rw, w_gt, _pick(s, (512, 256, 128, 64)))

        o_a = _hgrn2(hg, lb_all[l], hg_norm_g[l], _pick(s, (512, 256, 128, 64)))
        o_b = _rwkv7(rw, rw_mu[l], rw_w0[l], rw_w2[l], rw_a0[l], rw_a2[l], rw_g2[l],
                     rw_kk[l], rw_ka[l], rw_rk[l].reshape(-1), rw_gn_w[l], rw_gn_b[l])

        wr = jnp.zeros((d, LANES), F32).at[:, :n_experts].set(router_e_w[l])
        wr = wr.at[:, n_experts:n_experts + n_groups].set(router_g_w[l])
        br = jnp.zeros((1, LANES), F32).at[0, :n_experts].set(router_e_b[l])
        br = br.at[0, n_experts:n_experts + n_groups].set(router_g_b[l])
        x1, h2, route, counts = _out_proj(
            x, o_a, o_b, gates, mod3, norm2_g[l],
            w_proj_a[l].astype(BF16), w_proj_b[l].astype(BF16), w_out[l].astype(BF16),
            wr, br, n_groups, n_experts, _pick(s, (512, 256, 128, 64)))

        route2 = route.reshape(n, LANES)
        eid = route2[:, 2:4].astype(jnp.int32)
        rank = route2[:, 4:6].astype(jnp.int32)
        cnt = counts[0, :n_experts].astype(jnp.int32)
        padded = (cnt + blk - 1) // blk * blk
        pad_end = jnp.cumsum(padded)
        pad_start = pad_end - padded
        dest = pad_start[eid] + rank
        blk_start = jnp.arange(n_blocks, dtype=jnp.int32) * blk
        blk_e = jnp.minimum(jnp.sum((pad_end[None, :] <= blk_start[:, None]).astype(jnp.int32), axis=1),
                            n_experts - 1)
        n_used = (pad_end[-1:] // blk).astype(jnp.int32)
        tm = _pick(s, (256, 128, 64))
        xb, smap = _moe_dispatch(h2, dest.reshape(-1), pad_start + cnt, padded - cnt, n_used,
                                 n_blocks * blk, blk, tm)
        yt = _moe_experts(xb, smap, blk_e, n_used, exp_w_gate[l], exp_w_up[l], exp_w_down[l], blk, n)
        last = l == depth - 1
        assert last, "the final RMSNorm is fused into the last layer's combine"
        out = _moe_combine(x1.reshape(n, d), route2, yt, mod3, final_g, s, tm)
        x = out.reshape(bsz, s, d)
    return x
```

```python
import functools

import numpy as np
import jax
import jax.numpy as jnp
from jax import lax
from jax.experimental import pallas as pl
from jax.experimental.pallas import tpu as pltpu

F32 = jnp.float32
BF16 = jnp.bfloat16
HIGHEST = lax.Precision.HIGHEST

NORM_EPS = 1e-6
HG_HEAD = 128
RW_HEAD = 64
RW_GN_EPS = 64e-5
TOP_K = 2
CHUNK = 64
LANES = 128
SUB = 4
U32 = jnp.uint32
VMEM_LIMIT = 56 * 1024 * 1024

NT = (((1,), (1,)), ((), ()))
TN = (((0,), (0,)), ((), ()))


def _dot(a, b, dims=None, precision=None):
    if dims is None:
        return jnp.dot(a, b, preferred_element_type=F32, precision=precision)
    return lax.dot_general(a, b, dims, preferred_element_type=F32, precision=precision)


def _split3(x):
    hi = x.astype(BF16)
    r1 = x - hi.astype(F32)
    mid = r1.astype(BF16)
    lo = (r1 - mid.astype(F32)).astype(BF16)
    return hi, mid, lo


def _dot_exact_lhs(m_bf16, x):
    hi, mid, lo = _split3(x)
    return _dot(m_bf16, hi) + _dot(m_bf16, mid) + _dot(m_bf16, lo)


def _dot_exact_rhs(x, m_bf16):
    hi, mid, lo = _split3(x)
    return _dot(hi, m_bf16) + _dot(mid, m_bf16) + _dot(lo, m_bf16)


def _sigmoid(x):
    return 1.0 / (1.0 + jnp.exp(-x))


def _silu(x):
    return x * _sigmoid(x)


def _rows_to_tiles(ref, val):
    m, half = val.shape[0], val.shape[1] // 2
    hi = lax.bitcast_convert_type(val[:, :half].astype(BF16).astype(F32), U32)
    lo = lax.bitcast_convert_type(val[:, half:].astype(BF16).astype(F32), U32)
    w = (hi & jnp.uint32(0xFFFF0000)) | (lo >> 16)
    for j in range(SUB):
        ref[pl.ds(j, m, stride=SUB), :] = w[:, j * LANES:(j + 1) * LANES]


def _tiles_to_rows(ref, m, base=0):
    w = jnp.concatenate([ref[pl.ds(base * SUB + j, m, stride=SUB), :] for j in range(SUB)], axis=1)
    hi = lax.bitcast_convert_type(w & jnp.uint32(0xFFFF0000), F32)
    lo = lax.bitcast_convert_type(w << 16, F32)
    return jnp.concatenate([hi, lo], axis=1)


def _cparams(sem):
    return pltpu.CompilerParams(dimension_semantics=sem, vmem_limit_bytes=VMEM_LIMIT)


def _ada_kernel(c_ref, w_ref, b_ref, o_ref):
    c = c_ref[...]
    o_ref[...] = _dot(_silu(c), w_ref[...], precision=HIGHEST) + b_ref[...]


def _ada_mod(c, w, b):
    bsz, d = c.shape
    n = w.shape[1]
    rows = 8
    cp = jnp.zeros((rows, d), F32).at[:bsz].set(c)
    tn = 1536
    out = pl.pallas_call(
        _ada_kernel,
        out_shape=jax.ShapeDtypeStruct((rows, n), F32),
        grid=(n // tn,),
        in_specs=[pl.BlockSpec((rows, d), lambda j: (0, 0)),
                  pl.BlockSpec((d, tn), lambda j: (0, j)),
                  pl.BlockSpec((1, tn), lambda j: (0, j))],
        out_specs=pl.BlockSpec((rows, tn), lambda j: (0, j)),
        compiler_params=_cparams(("arbitrary",)),
        name="ada_mod",
    )(cp, w, b.reshape(1, n))
    return out[:bsz]


def _in_proj_kernel(x_ref, sh_ref, sc_ref, g_ref, whg_ref, wrw_ref, wgt_ref, hg_ref, rw_ref, gt_ref):
    x = x_ref[0]
    ms = jnp.mean(x * x, axis=-1, keepdims=True)
    h = (x * lax.rsqrt(ms + NORM_EPS) * g_ref[...]) * (1.0 + sc_ref[0]) + sh_ref[0]
    hb = h.astype(BF16)
    step = 512
    for n0 in range(0, whg_ref.shape[1], step):
        hg_ref[0, :, n0:n0 + step] = _dot(hb, whg_ref[:, n0:n0 + step])
    for n0 in range(0, wrw_ref.shape[1], 256):
        rw_ref[0, :, n0:n0 + 256] = _dot(hb, wrw_ref[:, n0:n0 + 256])
    for n0 in range(0, wgt_ref.shape[1], step):
        gt_ref[0, :, n0:n0 + step] = _sigmoid(_dot(hb, wgt_ref[:, n0:n0 + step])).astype(BF16)


def _in_proj(x, mod3, norm_g, w_hg, w_rw, w_gt, tm):
    bsz, s, d = x.shape
    n_hg, n_rw, n_gt = w_hg.shape[1], w_rw.shape[1], w_gt.shape[1]
    const = lambda b, i: (0, 0)
    return pl.pallas_call(
        _in_proj_kernel,
        out_shape=(jax.ShapeDtypeStruct((bsz, s, n_hg), F32),
                   jax.ShapeDtypeStruct((bsz, s, n_rw), F32),
                   jax.ShapeDtypeStruct((bsz, s, n_gt), BF16)),
        grid=(bsz, s // tm),
        in_specs=[pl.BlockSpec((1, tm, d), lambda b, i: (b, i, 0)),
                  pl.BlockSpec((1, 1, d), lambda b, i: (b * 6 + 0, 0, 0)),
                  pl.BlockSpec((1, 1, d), lambda b, i: (b * 6 + 1, 0, 0)),
                  pl.BlockSpec((1, d), const),
                  pl.BlockSpec((d, n_hg), const),
                  pl.BlockSpec((d, n_rw), const),
                  pl.BlockSpec((d, n_gt), const)],
        out_specs=(pl.BlockSpec((1, tm, n_hg), lambda b, i: (b, i, 0)),
                   pl.BlockSpec((1, tm, n_rw), lambda b, i: (b, i, 0)),
                   pl.BlockSpec((1, tm, n_gt), lambda b, i: (b, i, 0))),
        compiler_params=_cparams(("arbitrary", "arbitrary")),
        name="in_proj",
    )(x, mod3, mod3, norm_g.reshape(1, d), w_hg, w_rw, w_gt)


_HG_LEVELS = (32, 16, 8, 4, 2, 1)


def _hgrn2_consts(width):
    c = CHUNK
    t = np.arange(c)[:, None]
    s = np.arange(c)[None, :]
    blocks = [(s <= t), (s > t)]
    lvl_masks = []
    right = []
    for h in _HG_LEVELS:
        m = (t // (2 * h)) * 2 * h + h
        is_r = (t & h) != 0
        blk = np.where(is_r, (s >= m) & (s <= t), (s > t) & (s <= m - 1))
        blocks.append(blk)
        lvl_masks.append(is_r & ((s & h) == 0) & ((t // (2 * h)) == (s // (2 * h))))
        right.append(np.broadcast_to(is_r, (c, width)))
    mst = np.concatenate(blocks, axis=0).astype(np.float32)
    lm = np.stack([np.eye(c, dtype=bool)] + lvl_masks).astype(np.float32)
    rm = np.stack(right).astype(np.float32)
    return jnp.asarray(mst, BF16), jnp.asarray(lm, F32), jnp.asarray(rm, F32)


def _hgrn2_kernel(q_ref, f_ref, i_ref, g_ref, lb_ref, ng_ref, mst_ref, lm_ref, rm_ref, o_ref, st_ref):
    c = CHUNK
    n_heads = q_ref.shape[2] // HG_HEAD
    n_chunks = q_ref.shape[1] // c

    @pl.when(pl.program_id(1) == 0)
    def _():
        st_ref[...] = jnp.zeros_like(st_ref)

    mst = mst_ref[...]
    lb = lb_ref[...]
    ng = ng_ref[...]
    heads = [slice(hd * HG_HEAD, (hd + 1) * HG_HEAD) for hd in range(n_heads)]

    def chunk_body(ci, carry):
        r0 = pl.multiple_of(ci * c, c)
        rows = pl.ds(r0, c)
        q = _silu(q_ref[0, rows, :])
        f = lb + (1.0 - lb) * _sigmoid(f_ref[0, rows, :])
        lf = jnp.log(f)
        k = 1.0 - f
        vb = i_ref[0, rows, :].astype(BF16)
        ex = jnp.exp(_dot_exact_lhs(mst, lf))
        ex_cum = ex[0:c]
        qd = (q * ex_cum).astype(BF16)
        kr = (k * ex[c:2 * c]).astype(BF16)
        qb = q.astype(BF16)
        kb = k.astype(BF16)
        sts = [st_ref[hd] for hd in range(n_heads)]
        o = [_dot(qd[:, ls], st.astype(BF16), NT) for ls, st in zip(heads, sts)]
        sc = [lm_ref[0] * _dot(qb[:, ls], kb[:, ls], NT) for ls in heads]
        dqk = q - k
        for li in range(len(_HG_LEVELS)):
            g_l = ((k + rm_ref[li] * dqk) * ex[(2 + li) * c:(3 + li) * c]).astype(BF16)
            sc = [s_h + lm_ref[li + 1] * _dot(g_l[:, ls], g_l[:, ls], NT) for s_h, ls in zip(sc, heads)]
        o = [o_h + _dot(s_h.astype(BF16), vb[:, ls]) for o_h, s_h, ls in zip(o, sc, heads)]
        for hd, ls in enumerate(heads):
            st_ref[hd] = sts[hd] * ex_cum[c - 1:c, ls] + _dot(vb[:, ls], kr[:, ls], TN)
        on = [o_h * lax.rsqrt(jnp.mean(o_h * o_h, axis=-1, keepdims=True) + NORM_EPS) for o_h in o]
        o_full = jnp.concatenate(on, axis=1) * ng
        o_ref[0, rows, :] = (o_full * _silu(g_ref[0, rows, :])).astype(o_ref.dtype)
        return carry

    lax.fori_loop(0, n_chunks, chunk_body, 0)


def _hgrn2(hg, lb, norm_g, ts):
    bsz, s, n4 = hg.shape
    w = n4 // 4
    mst, lm, rm = _hgrn2_consts(w)
    n_heads = w // HG_HEAD
    const2 = lambda b, i: (0, 0)
    const3 = lambda b, i: (0, 0, 0)
    return pl.pallas_call(
        _hgrn2_kernel,
        out_shape=jax.ShapeDtypeStruct((bsz, s, w), BF16),
        grid=(bsz, s // ts),
        in_specs=[pl.BlockSpec((1, ts, w), lambda b, i: (b, i, 0)),
                  pl.BlockSpec((1, ts, w), lambda b, i: (b, i, 1)),
                  pl.BlockSpec((1, ts, w), lambda b, i: (b, i, 2)),
                  pl.BlockSpec((1, ts, w), lambda b, i: (b, i, 3)),
                  pl.BlockSpec((1, w), const2),
                  pl.BlockSpec((1, w), const2),
                  pl.BlockSpec(mst.shape, const2),
                  pl.BlockSpec(lm.shape, const3),
                  pl.BlockSpec(rm.shape, const3)],
        out_specs=pl.BlockSpec((1, ts, w), lambda b, i: (b, i, 0)),
        scratch_shapes=[pltpu.VMEM((n_heads, HG_HEAD, HG_HEAD), F32)],
        compiler_params=_cparams(("arbitrary", "arbitrary")),
        name="hgrn2",
    )(hg, hg, hg, hg, lb.reshape(1, w), norm_g.reshape(1, w), mst, lm, rm)


def _rwkv_consts(width):
    c = CHUNK
    t = np.arange(c)[:, None]
    s = np.arange(c)[None, :]
    tri = (s <= t).astype(np.float32)
    tt = np.arange(2 * c)[:, None]
    ss = np.arange(2 * c)[None, :]
    same = (tt // c) == (ss // c)
    strict = same & ((ss % c) < (tt % c))
    incl = same & ((ss % c) <= (tt % c))
    hsum = (np.arange(width)[:, None] // RW_HEAD) == (np.arange(width)[None, :] // RW_HEAD)
    return (jnp.asarray(tri, BF16), jnp.asarray(strict.astype(np.float32), F32),
            jnp.asarray(incl.astype(np.float32), F32), jnp.asarray(hsum.astype(np.float32), BF16))


def _rwkv7_kernel(p_ref, mu_ref, w0_ref, a0_ref, kk_ref, ka_ref, rk_ref, gnw_ref, gnb_ref,
                  w2_ref, a2_ref, g2_ref, tri_ref, sm_ref, im_ref, hs_ref,
                  o_ref, carry_ref, zt_ref):
    c = CHUNK
    nb = p_ref.shape[0]
    width = o_ref.shape[2]
    n_pairs = width // LANES

    @pl.when(pl.program_id(0) == 0)
    def _():
        carry_ref[...] = jnp.zeros_like(carry_ref)
        zt_ref[...] = jnp.zeros_like(zt_ref)

    hs = hs_ref[...]
    tri = tri_ref[...]
    smask = sm_ref[...] > 0
    imask = im_ref[...] > 0
    lane = lax.broadcasted_iota(jnp.int32, (c, LANES), 1)
    m0 = (lane < RW_HEAD).astype(F32)
    m1 = 1.0 - m0

    def stack(x):
        return jnp.concatenate([x * m0, x * m1], axis=0)

    rs, k2s, vs_, gs = [], [], [], []
    units = []
    for b in range(nb):
        p = p_ref[b]
        row = lax.broadcasted_iota(jnp.int32, p.shape, 0)
        prev = jnp.where(row == 0, carry_ref[b], pltpu.roll(p, 1, 0))
        carry_ref[b] = p[c - 1:c, :]
        xs = p + mu_ref[...] * (prev - p)
        r = xs[:, 0:width]
        k = xs[:, width:2 * width]
        v = xs[:, 2 * width:3 * width]
        slab = xs[:, 3 * width:]
        nz = -(w0_ref[...] + _dot(jnp.tanh(slab).astype(BF16), w2_ref[...]))
        softplus = jnp.maximum(nz, 0.0) + jnp.log(1.0 + jnp.exp(-jnp.abs(nz)))
        ld = -jnp.exp(-softplus - 0.5)
        a = _sigmoid(a0_ref[...] + _dot(slab.astype(BF16), a2_ref[...]))
        gs.append(_dot(_sigmoid(slab).astype(BF16), g2_ref[...]))
        kk0 = k * kk_ref[...]
        kk = kk0 * lax.rsqrt(jnp.maximum(_dot_exact_rhs(kk0 * kk0, hs), 1e-24))
        k2 = k * (1.0 + (a - 1.0) * ka_ref[...])
        a_in = -kk
        b_in = kk * a
        cum = _dot_exact_lhs(tri, ld)
        cum_t = cum[c - 1:c, :]
        e_c = jnp.exp(cum)
        e_nc = jnp.exp(-cum)
        e_rem = jnp.exp(cum_t - cum)
        at_f = a_in * jnp.exp(cum - ld)
        rt_f = r * e_c
        kt_f = k2 * e_nc
        bt_f = b_in * e_nc
        kh_f = k2 * e_rem
        bh_f = b_in * e_rem
        p_t = jnp.exp(cum_t)
        rs.append(r)
        k2s.append(k2)
        vs_.append(v)
        for pi in range(n_pairs):
            ls = slice(pi * LANES, (pi + 1) * LANES)
            units.append(dict(
                b=b, pi=pi,
                at=stack(at_f[:, ls]).astype(BF16), rt=stack(rt_f[:, ls]).astype(BF16),
                kt=stack(kt_f[:, ls]).astype(BF16), bt=stack(bt_f[:, ls]).astype(BF16),
                kh=stack(kh_f[:, ls]).astype(BF16), bh=stack(bh_f[:, ls]).astype(BF16),
                vs=stack(v[:, ls]).astype(BF16), p_t=p_t[:, ls]))

    for u in units:
        lhs = jnp.concatenate([u['at'], u['rt']], axis=0)
        u['gk'] = _dot(lhs, u['kt'], NT)
        u['gb'] = _dot(lhs, u['bt'], NT)
    for u in units:
        gk, gb = u.pop('gk'), u.pop('gb')
        a_kk = jnp.concatenate([jnp.where(smask, gk[:2 * c], 0.0), jnp.where(imask, gk[2 * c:], 0.0)], axis=0)
        u['pw'] = jnp.where(smask, gb[:2 * c], 0.0).astype(BF16)
        u['a_rb'] = jnp.where(imask, gb[2 * c:], 0.0).astype(BF16)
        u['a_kk'] = a_kk.astype(BF16)
    for u in units:
        av = _dot(u.pop('a_kk'), u['vs'])
        u['arkv'] = av[2 * c:]
        u['x'] = jnp.concatenate([u['at'].astype(F32), av[:2 * c]], axis=1)
    n_lvl = int(np.log2(c))
    for lvl in range(n_lvl):
        for u in units:
            u['x'] = u['x'] + _dot(u['pw'], u['x'].astype(BF16))
        if lvl + 1 < n_lvl:
            for u in units:
                u['pw'] = _dot(u['pw'], u['pw']).astype(BF16)
    for u in units:
        x = u.pop('x')
        u['zt'] = zt_ref[u['b'], u['pi']]
        u['uy'] = _dot(jnp.concatenate([x[:, :LANES].astype(BF16), u['rt']], axis=0), u['zt'].astype(BF16), NT)
        u['u_loc'] = x[:, LANES:]
    for u in units:
        uy = u.pop('uy')
        u['u'] = (uy[:2 * c] + u.pop('u_loc')).astype(BF16)
        u['y0'] = uy[2 * c:] + u.pop('arkv')
    for u in units:
        u['y'] = u.pop('y0') + _dot(u['a_rb'], u['u'])
        upd = _dot(jnp.concatenate([u['u'], u['vs']], axis=0), jnp.concatenate([u['bh'], u['kh']], axis=0), TN)
        zt_ref[u['b'], u['pi']] = u['zt'] * u['p_t'] + upd

    inv_n = 1.0 / RW_HEAD
    for b in range(nb):
        ys = [u['y'] for u in units if u['b'] == b]
        y = jnp.concatenate([yy[:c] + yy[c:] for yy in ys], axis=1)
        mean = _dot_exact_rhs(y, hs) * inv_n
        d = y - mean
        var = _dot_exact_rhs(d * d, hs) * inv_n
        yn = d * lax.rsqrt(var + RW_GN_EPS) * gnw_ref[...] + gnb_ref[...]
        bonus = _dot_exact_rhs(rs[b] * k2s[b] * rk_ref[...], hs) * vs_[b]
        o_ref[b] = ((yn + bonus) * gs[b]).astype(o_ref.dtype)


def _rwkv7(rw, mu, w0, w2, a0, a2, g2, k_k, k_a, r_k, gn_w, gn_b):
    bsz, s, cols = rw.shape
    width = w0.shape[-1]
    n_pairs = width // LANES
    slab = cols - 3 * width
    dl, al, gl = w2.shape[0], a2.shape[0], g2.shape[0]
    w2f = jnp.zeros((slab, width), F32).at[0:dl].set(w2).astype(BF16)
    a2f = jnp.zeros((slab, width), F32).at[dl:dl + al].set(a2).astype(BF16)
    g2f = jnp.zeros((slab, width), F32).at[dl + al:dl + al + gl].set(g2).astype(BF16)
    mup = jnp.zeros((1, cols), F32).at[0, :mu.shape[-1]].set(mu)
    tri, sm, im, hs = _rwkv_consts(width)
    row = lambda x: x.reshape(1, width)
    const = lambda i: (0, 0)
    vec = pl.BlockSpec((1, width), const)
    return pl.pallas_call(
        _rwkv7_kernel,
        out_shape=jax.ShapeDtypeStruct((bsz, s, width), BF16),
        grid=(s // CHUNK,),
        in_specs=[pl.BlockSpec((bsz, CHUNK, cols), lambda i: (0, i, 0)),
                  pl.BlockSpec((1, cols), const),
                  vec, vec, vec, vec, vec, vec, vec,
                  pl.BlockSpec((slab, width), const),
                  pl.BlockSpec((slab, width), const),
                  pl.BlockSpec((slab, width), const),
                  pl.BlockSpec(tri.shape, const),
                  pl.BlockSpec(sm.shape, const),
                  pl.BlockSpec(im.shape, const),
                  pl.BlockSpec(hs.shape, const)],
        out_specs=pl.BlockSpec((bsz, CHUNK, width), lambda i: (0, i, 0)),
        scratch_shapes=[pltpu.VMEM((bsz, 1, cols), F32),
                        pltpu.VMEM((bsz, n_pairs, LANES, LANES), F32)],
        compiler_params=_cparams(("arbitrary",)),
        name="rwkv7",
    )(rw, mup, row(w0), row(a0), row(k_k), row(k_a), row(r_k), row(gn_w), row(gn_b),
      w2f, a2f, g2f, tri, sm, im, hs)


def _out_proj_kernel(n_groups, n_experts,
                     x_ref, oa_ref, ob_ref, ga_ref, gb_ref, gt1_ref, sc2_ref, sh2_ref, g2_ref,
                     wa_ref, wb_ref, wo_ref, wr_ref, wrl_ref, br_ref, tril_ref,
                     x1_ref, h2_ref, route_ref, cnt_ref, carry_ref):
    tm = x_ref.shape[1]
    first = (pl.program_id(0) == 0) & (pl.program_id(1) == 0)

    @pl.when(first)
    def _():
        carry_ref[...] = jnp.zeros_like(carry_ref)

    pa = _dot(oa_ref[0], wa_ref[...])
    pb = _dot(ob_ref[0], wb_ref[...])
    mixed = ga_ref[0].astype(F32) * pa + gb_ref[0].astype(F32) * pb
    x1 = x_ref[0] + gt1_ref[0] * _dot(mixed.astype(BF16), wo_ref[...])
    x1_ref[0] = x1
    ms = jnp.mean(x1 * x1, axis=-1, keepdims=True)
    h2 = (x1 * lax.rsqrt(ms + NORM_EPS) * g2_ref[...]) * (1.0 + sc2_ref[0]) + sh2_ref[0]
    _rows_to_tiles(h2_ref, h2)

    h2_hi = h2.astype(BF16)
    h2_lo = (h2 - h2_hi.astype(F32)).astype(BF16)
    logits = (_dot(h2_hi, wr_ref[...]) + _dot(h2_lo, wr_ref[...]) + _dot(h2_hi, wrl_ref[...])) + br_ref[...]
    lane = lax.broadcasted_iota(jnp.int32, logits.shape, 1)
    neg = jnp.float32(-jnp.inf)
    big = jnp.int32(1 << 20)
    eg = n_experts // n_groups
    is_g = (lane >= n_experts) & (lane < n_experts + n_groups)
    lg = jnp.where(is_g, logits, neg)
    mg = jnp.max(lg, axis=-1, keepdims=True)
    p_grp = 1.0 / jnp.sum(jnp.where(is_g, jnp.exp(lg - mg), 0.0), axis=-1, keepdims=True)
    gidx = jnp.min(jnp.where(lg == mg, lane, big), axis=-1, keepdims=True) - n_experts
    sel = (lane >= gidx * eg) & (lane < gidx * eg + eg)
    le = jnp.where(sel, logits, neg)
    me = jnp.max(le, axis=-1, keepdims=True)
    pe_un = jnp.where(sel, jnp.exp(le - me), 0.0)
    pe = jnp.where(sel, pe_un / jnp.sum(pe_un, axis=-1, keepdims=True), -1.0)
    v1 = jnp.max(pe, axis=-1, keepdims=True)
    i1 = jnp.min(jnp.where(pe == v1, lane, big), axis=-1, keepdims=True)
    pe2 = jnp.where(lane == i1, -1.0, pe)
    v2 = jnp.max(pe2, axis=-1, keepdims=True)
    i2 = jnp.min(jnp.where(pe2 == v2, lane, big), axis=-1, keepdims=True)
    wsum = v1 + v2
    w1 = p_grp * v1 / wsum
    w2 = p_grp * v2 / wsum

    oh1 = (lane == i1).astype(F32)
    oh2 = (lane == i2).astype(F32)
    both = oh1 + oh2
    before = _dot(tril_ref[...], both.astype(BF16)) + carry_ref[...]
    rank1 = jnp.sum(oh1 * before, axis=-1, keepdims=True)
    rank2 = jnp.sum(oh2 * before, axis=-1, keepdims=True)
    carry_ref[...] = carry_ref[...] + jnp.sum(both, axis=0, keepdims=True)
    cnt_ref[...] = carry_ref[...]

    out = jnp.where(lane == 0, w1, 0.0)
    out = jnp.where(lane == 1, w2, out)
    out = jnp.where(lane == 2, i1.astype(F32), out)
    out = jnp.where(lane == 3, i2.astype(F32), out)
    out = jnp.where(lane == 4, rank1, out)
    out = jnp.where(lane == 5, rank2, out)
    route_ref[0] = out


def _out_proj(x, o_a, o_b, gates, mod3, norm2_g, wa, wb, wo, wr, br, n_groups, n_experts, tm):
    bsz, s, d = x.shape
    wdt = o_a.shape[-1]
    tril = jnp.asarray(np.tril(np.ones((tm, tm), np.float32), -1), BF16)
    wr_hi = wr.astype(BF16)
    wr_lo = (wr - wr_hi.astype(F32)).astype(BF16)
    const = lambda b, i: (0, 0)
    tile = lambda b, i: (b, i, 0)
    kern = functools.partial(_out_proj_kernel, n_groups, n_experts)
    return pl.pallas_call(
        kern,
        out_shape=(jax.ShapeDtypeStruct((bsz, s, d), F32),
                   jax.ShapeDtypeStruct((bsz * s * SUB, LANES), U32),
                   jax.ShapeDtypeStruct((bsz, s, LANES), F32),
                   jax.ShapeDtypeStruct((1, LANES), F32)),
        grid=(bsz, s // tm),
        in_specs=[pl.BlockSpec((1, tm, d), tile),
                  pl.BlockSpec((1, tm, wdt), tile),
                  pl.BlockSpec((1, tm, wdt), tile),
                  pl.BlockSpec((1, tm, d), lambda b, i: (b, i, 0)),
                  pl.BlockSpec((1, tm, d), lambda b, i: (b, i, 1)),
                  pl.BlockSpec((1, 1, d), lambda b, i: (b * 6 + 2, 0, 0)),
                  pl.BlockSpec((1, 1, d), lambda b, i: (b * 6 + 4, 0, 0)),
                  pl.BlockSpec((1, 1, d), lambda b, i: (b * 6 + 3, 0, 0)),
                  pl.BlockSpec((1, d), const),
                  pl.BlockSpec(wa.shape, const),
                  pl.BlockSpec(wb.shape, const),
                  pl.BlockSpec(wo.shape, const),
                  pl.BlockSpec(wr.shape, const),
                  pl.BlockSpec(wr.shape, const),
                  pl.BlockSpec((1, LANES), const),
                  pl.BlockSpec((tm, tm), const)],
        out_specs=(pl.BlockSpec((1, tm, d), tile),
                   pl.BlockSpec((tm * SUB, LANES), lambda b, i: (b * (s // tm) + i, 0)),
                   pl.BlockSpec((1, tm, LANES), tile),
                   pl.BlockSpec((1, LANES), const)),
        scratch_shapes=[pltpu.VMEM((1, LANES), F32)],
        compiler_params=_cparams(("arbitrary", "arbitrary")),
        name="out_proj",
    )(x, o_a, o_b, gates, gates, mod3, mod3, mod3, norm2_g.reshape(1, d), wa, wb, wo, wr_hi, wr_lo, br, tril)


def _moe_dispatch_kernel(tm, n_tok, blk, dest_ref, zstart_ref, zcnt_ref, nused_ref, h_ref, xb_ref, smap_ref,
                         zbuf, stage, sem, zsem):
    i = pl.program_id(0)
    n_steps = pl.num_programs(0)
    n_slots = smap_ref.shape[0]
    n_experts = zcnt_ref.shape[0]

    @pl.when(i == 0)
    def _():
        def init(s_, carry):
            smap_ref[s_] = 0
            return carry
        lax.fori_loop(0, n_slots, init, 0)
        zbuf[...] = jnp.zeros_like(zbuf)

        def zero_row(e, j):
            dst = pl.multiple_of((zstart_ref[e] + j) * SUB, SUB)
            return pltpu.make_async_copy(zbuf.at[pl.ds(0, SUB), :], xb_ref.at[pl.ds(dst, SUB), :], zsem)

        def zero_block(b):
            dst = pl.multiple_of(b * (blk * SUB), blk * SUB)
            return pltpu.make_async_copy(zbuf, xb_ref.at[pl.ds(dst, blk * SUB), :], zsem)

        for e in range(n_experts):
            def zstart(j, carry, e=e):
                zero_row(e, j).start()
                smap_ref[zstart_ref[e] + j] = TOP_K * n_tok + e * blk + j
                return carry
            lax.fori_loop(0, zcnt_ref[e], zstart, 0)

        def bstart(b, carry):
            zero_block(b).start()
            return carry
        lax.fori_loop(nused_ref[0], n_slots // blk, bstart, 0)
        for e in range(n_experts):
            def zwait(j, carry, e=e):
                zero_row(e, j).wait()
                return carry
            lax.fori_loop(0, zcnt_ref[e], zwait, 0)

        def bwait(b, carry):
            zero_block(b).wait()
            return carry
        lax.fori_loop(nused_ref[0], n_slots // blk, bwait, 0)

    slot = i % 2
    stage[slot] = h_ref[...]
    for r in range(tm):
        tok = i * tm + r
        for k in range(TOP_K):
            d = dest_ref[tok * TOP_K + k]
            dst = pl.multiple_of(d * SUB, SUB)
            pltpu.make_async_copy(stage.at[slot, pl.ds(r * SUB, SUB), :], xb_ref.at[pl.ds(dst, SUB), :],
                                  sem.at[slot]).start()
            smap_ref[d] = k * n_tok + tok

    def wait_step(s_):
        for _ in range(TOP_K):
            pltpu.make_async_copy(stage.at[s_], xb_ref.at[pl.ds(0, tm * SUB), :], sem.at[s_]).wait()

    @pl.when(i > 0)
    def _():
        wait_step(1 - slot)

    @pl.when(i == n_steps - 1)
    def _():
        wait_step(slot)


def _moe_dispatch(h2, dest, zstart, zcnt, n_used, n_slots, blk, tm):
    n_tok = h2.shape[0] // SUB
    grid_spec = pltpu.PrefetchScalarGridSpec(
        num_scalar_prefetch=4,
        grid=(n_tok // tm,),
        in_specs=[pl.BlockSpec((tm * SUB, LANES), lambda i, *_: (i, 0))],
        out_specs=(pl.BlockSpec(memory_space=pl.ANY),
                   pl.BlockSpec(memory_space=pltpu.SMEM)),
        scratch_shapes=[pltpu.VMEM((blk * SUB, LANES), U32),
                        pltpu.VMEM((2, tm * SUB, LANES), U32),
                        pltpu.SemaphoreType.DMA((2,)),
                        pltpu.SemaphoreType.DMA],
    )
    return pl.pallas_call(
        functools.partial(_moe_dispatch_kernel, tm, n_tok, blk),
        out_shape=(jax.ShapeDtypeStruct((n_slots * SUB, LANES), U32),
                   jax.ShapeDtypeStruct((n_slots,), jnp.int32)),
        grid_spec=grid_spec,
        compiler_params=_cparams(("arbitrary",)),
        name="moe_dispatch",
    )(dest, zstart, zcnt, n_used, h2)


def _moe_expert_kernel(n_tok, blk_e_ref, nused_ref, smap_ref, x_ref, wg_ref, wu_ref, wd_ref, y_ref,
                       ystage, wgb, wub, wdb, sem):
    i = pl.program_id(0)
    n_used = nused_ref[0]
    blk = x_ref.shape[0] // SUB
    first_real_blocks = TOP_K * n_tok // blk
    first_spare = y_ref.shape[0] // SUB - blk

    def issue(b, slot, rows):
        for r in rows:
            t = jnp.where(b >= 0, smap_ref[jnp.maximum(b, 0) * blk + r], first_spare + r)
            dst = pl.multiple_of(t * SUB, SUB)
            pltpu.make_async_copy(ystage.at[slot, pl.ds(r * SUB, SUB), :], y_ref.at[pl.ds(dst, SUB), :],
                                  sem.at[slot]).start()

    def wait_block(slot):
        pltpu.make_async_copy(ystage.at[slot], y_ref.at[pl.ds(0, blk * SUB), :], sem.at[slot]).wait()

    @pl.when(i == 0)
    def _():
        ystage[...] = jnp.zeros_like(ystage)
        n_spare_blocks = y_ref.shape[0] // (blk * SUB) - first_real_blocks

        def spare_copy(c):
            dst = (first_real_blocks + c) * blk * SUB
            return pltpu.make_async_copy(ystage.at[0], y_ref.at[pl.ds(dst, blk * SUB), :], sem.at[0])

        for c in range(n_spare_blocks):
            spare_copy(c).start()
        for c in range(n_spare_blocks):
            spare_copy(c).wait()

    new_expert = (i == 0) | (blk_e_ref[i] != blk_e_ref[jnp.maximum(i - 1, 0)])

    @pl.when((i < n_used) & new_expert)
    def _():
        wgb[...] = wg_ref[0].astype(BF16)
        wub[...] = wu_ref[0].astype(BF16)
        wdb[...] = wd_ref[0].astype(BF16)

    @pl.when(i < n_used)
    def _():
        slot = i % 2
        pslot = 1 - slot

        @pl.when(i > 0)
        def _():
            wait_block(slot)

        q = blk // 4
        xb = _tiles_to_rows(x_ref, blk).astype(BF16)
        issue(i - 1, pslot, range(0, q))
        hg = _dot(xb, wgb[...])
        issue(i - 1, pslot, range(q, 2 * q))
        hu = _dot(xb, wub[...])
        issue(i - 1, pslot, range(2 * q, 3 * q))
        hid = (_silu(hg) * hu).astype(BF16)
        y = _dot(hid, wdb[...])
        issue(i - 1, pslot, range(3 * q, blk))
        _rows_to_tiles(ystage.at[slot], y)

        @pl.when(i == n_used - 1)
        def _():
            issue(i, slot, range(blk))
            wait_block(pslot)
            wait_block(slot)


def _moe_experts(xb, smap, blk_e, n_used, w_gate, w_up, w_down, blk, n_tok):
    d = w_gate.shape[1]
    assert d == 2 * SUB * LANES and xb.shape[1] == LANES
    n_slots = smap.shape[0]
    nb = n_slots // blk
    f = w_gate.shape[-1]
    n_experts = w_gate.shape[0]
    assert (TOP_K * n_tok) % blk == 0
    n_tiles = TOP_K * n_tok + (n_experts + 1) * blk
    grid_spec = pltpu.PrefetchScalarGridSpec(
        num_scalar_prefetch=3,
        grid=(nb,),
        in_specs=[pl.BlockSpec((blk * SUB, LANES), lambda i, be, nu, sm: (jnp.minimum(i, nu[0] - 1), 0)),
                  pl.BlockSpec((1, d, f), lambda i, be, nu, sm: (be[i], 0, 0)),
                  pl.BlockSpec((1, d, f), lambda i, be, nu, sm: (be[i], 0, 0)),
                  pl.BlockSpec((1, f, d), lambda i, be, nu, sm: (be[i], 0, 0))],
        out_specs=pl.BlockSpec(memory_space=pl.ANY),
        scratch_shapes=[pltpu.VMEM((2, blk * SUB, LANES), U32),
                        pltpu.VMEM((d, f), BF16),
                        pltpu.VMEM((d, f), BF16),
                        pltpu.VMEM((f, d), BF16),
                        pltpu.SemaphoreType.DMA((2,))],
    )
    return pl.pallas_call(
        functools.partial(_moe_expert_kernel, n_tok),
        out_shape=jax.ShapeDtypeStruct((n_tiles * SUB, LANES), U32),
        grid_spec=grid_spec,
        compiler_params=_cparams(("arbitrary",)),
        name="moe_experts",
    )(blk_e, n_used, smap, xb, w_gate, w_up, w_down)


def _moe_combine_kernel(x1_ref, route_ref, gt2_ref, fg_ref, y1_ref, y2_ref, o_ref):
    tm = x1_ref.shape[0]
    route = route_ref[...]
    moe = route[:, 0:1] * _tiles_to_rows(y1_ref, tm) + route[:, 1:2] * _tiles_to_rows(y2_ref, tm)
    xo = x1_ref[...] + gt2_ref[0] * moe
    ms = jnp.mean(xo * xo, axis=-1, keepdims=True)
    o_ref[...] = xo * lax.rsqrt(ms + NORM_EPS) * fg_ref[...]


def _moe_combine(x1, route, yt, mod3, final_g, s, tm):
    n, d = x1.shape
    tiles_per_batch = s // tm
    n_steps = n // tm
    return pl.pallas_call(
        _moe_combine_kernel,
        out_shape=jax.ShapeDtypeStruct((n, d), F32),
        grid=(n_steps,),
        in_specs=[pl.BlockSpec((tm, d), lambda i: (i, 0)),
                  pl.BlockSpec((tm, LANES), lambda i: (i, 0)),
                  pl.BlockSpec((1, 1, d), lambda i: ((i // tiles_per_batch) * 6 + 5, 0, 0)),
                  pl.BlockSpec((1, d), lambda i: (0, 0)),
                  pl.BlockSpec((tm * SUB, LANES), lambda i: (i, 0)),
                  pl.BlockSpec((tm * SUB, LANES), lambda i: (n_steps + i, 0))],
        out_specs=pl.BlockSpec((tm, d), lambda i: (i, 0)),
        compiler_params=_cparams(("arbitrary",)),
        name="moe_combine",
    )(x1, route, mod3, final_g.reshape(1, d), yt, yt)


def _pick(n, candidates):
    for t in candidates:
        if n % t == 0:
            return t
    raise ValueError(f"no tile in {candidates} divides {n}")


def kernel(x, c, ada_w, ada_b, norm1_g, w_in, hg_lb, hg_norm_g, rw_mu, rw_w0, rw_w2, rw_a0, rw_a2, rw_g2, rw_kk, rw_ka, rw_rk, rw_gn_w, rw_gn_b, w_proj_a, w_proj_b, w_out, norm2_g, router_g_w, router_g_b, router_e_w, router_e_b, exp_w_gate, exp_w_up, exp_w_down, final_g):
    bsz, s, d = x.shape
    depth = ada_w.shape[0]
    hg_f = hg_lb.shape[-1]
    hg_w = hg_norm_g.shape[-1]
    rw_w = rw_w0.shape[-1]
    rw_cols = rw_mu.shape[-1]
    n_groups = router_g_w.shape[-1]
    n_experts = router_e_w.shape[-1]
    assert hg_f == hg_w and s % CHUNK == 0 and n_experts + n_groups <= LANES and d == 2 * SUB * LANES

    lb_all = jnp.cumsum(jax.nn.softmax(hg_lb.astype(F32), axis=0), axis=0)
    n = bsz * s
    blk = 256
    n_blocks = (n * TOP_K + n_experts * blk) // blk
    for l in range(depth):
        mod = _ada_mod(c, ada_w[l], ada_b[l])
        mod3 = mod.reshape(bsz * 6, 1, d)

        hg_cols = 2 * hg_f + 2 * hg_w
        rw_pad = -(-rw_cols // 256) * 256
        wl = w_in[l]
        w_hg = wl[:, :hg_cols].astype(BF16)
        w_rw = jnp.zeros((d, rw_pad), BF16).at[:, :rw_cols].set(wl[:, hg_cols:hg_cols + rw_cols].astype(BF16))
        w_gt = wl[:, hg_cols + rw_cols:].astype(BF16)
        hg, rw, gates = _in_proj(x, mod3, norm1_g[l], w_hg, w_rw, w_gt, _pick(s, (512, 256, 128, 64)))

        o_a = _hgrn2(hg, lb_all[l], hg_norm_g[l], _pick(s, (512, 256, 128, 64)))
        o_b = _rwkv7(rw, rw_mu[l], rw_w0[l], rw_w2[l], rw_a0[l], rw_a2[l], rw_g2[l],
                     rw_kk[l], rw_ka[l], rw_rk[l].reshape(-1), rw_gn_w[l], rw_gn_b[l])

        wr = jnp.zeros((d, LANES), F32).at[:, :n_experts].set(router_e_w[l])
        wr = wr.at[:, n_experts:n_experts + n_groups].set(router_g_w[l])
        br = jnp.zeros((1, LANES), F32).at[0, :n_experts].set(router_e_b[l])
        br = br.at[0, n_experts:n_experts + n_groups].set(router_g_b[l])
        x1, h2, route, counts = _out_proj(
            x, o_a, o_b, gates, mod3, norm2_g[l],
            w_proj_a[l].astype(BF16), w_proj_b[l].astype(BF16), w_out[l].astype(BF16),
            wr, br, n_groups, n_experts, _pick(s, (512, 256, 128, 64)))

        route2 = route.reshape(n, LANES)
        eid = route2[:, 2:4].astype(jnp.int32)
        rank = route2[:, 4:6].astype(jnp.int32)
        cnt = counts[0, :n_experts].astype(jnp.int32)
        padded = (cnt + blk - 1) // blk * blk
        pad_end = jnp.cumsum(padded)
        pad_start = pad_end - padded
        dest = pad_start[eid] + rank
        blk_start = jnp.arange(n_blocks, dtype=jnp.int32) * blk
        blk_e = jnp.minimum(jnp.sum((pad_end[None, :] <= blk_start[:, None]).astype(jnp.int32), axis=1),
                            n_experts - 1)
        n_used = (pad_end[-1:] // blk).astype(jnp.int32)
        tm = _pick(s, (256, 128, 64))
        xb, smap = _moe_dispatch(h2, dest.reshape(-1), pad_start + cnt, padded - cnt, n_used,
                                 n_blocks * blk, blk, tm)
        yt = _moe_experts(xb, smap, blk_e, n_used, exp_w_gate[l], exp_w_up[l], exp_w_down[l], blk, n)
        last = l == depth - 1
        assert last, "the final RMSNorm is fused into the last layer's combine"
        out = _moe_combine(x1.reshape(n, d), route2, yt, mod3, final_g, s, tm)
        x = out.reshape(bsz, s, d)
    return x
```

```python
import functools

import numpy as np
import jax
import jax.numpy as jnp
from jax import lax
from jax.experimental import pallas as pl
from jax.experimental.pallas import tpu as pltpu

F32 = jnp.float32
BF16 = jnp.bfloat16
HIGHEST = lax.Precision.HIGHEST

NORM_EPS = 1e-6
HG_HEAD = 128
RW_HEAD = 64
RW_GN_EPS = 64e-5
TOP_K = 2
CHUNK = 64
LANES = 128
SUB = 4
U32 = jnp.uint32
VMEM_LIMIT = 56 * 1024 * 1024

NT = (((1,), (1,)), ((), ()))
TN = (((0,), (0,)), ((), ()))


def _dot(a, b, dims=None, precision=None):
    if dims is None:
        return jnp.dot(a, b, preferred_element_type=F32, precision=precision)
    return lax.dot_general(a, b, dims, preferred_element_type=F32, precision=precision)


def _split3(x):
    hi = x.astype(BF16)
    r1 = x - hi.astype(F32)
    mid = r1.astype(BF16)
    lo = (r1 - mid.astype(F32)).astype(BF16)
    return hi, mid, lo


def _dot_exact_lhs(m_bf16, x):
    hi, mid, lo = _split3(x)
    return _dot(m_bf16, hi) + _dot(m_bf16, mid) + _dot(m_bf16, lo)


def _dot_exact_rhs(x, m_bf16):
    hi, mid, lo = _split3(x)
    return _dot(hi, m_bf16) + _dot(mid, m_bf16) + _dot(lo, m_bf16)


def _sigmoid(x):
    return 1.0 / (1.0 + jnp.exp(-x))


def _silu(x):
    return x * _sigmoid(x)


def _rows_to_tiles(ref, val):
    m, half = val.shape[0], val.shape[1] // 2
    hi = lax.bitcast_convert_type(val[:, :half].astype(BF16).astype(F32), U32)
    lo = lax.bitcast_convert_type(val[:, half:].astype(BF16).astype(F32), U32)
    w = (hi & jnp.uint32(0xFFFF0000)) | (lo >> 16)
    for j in range(SUB):
        ref[pl.ds(j, m, stride=SUB), :] = w[:, j * LANES:(j + 1) * LANES]


def _tiles_to_rows(ref, m, base=0):
    w = jnp.concatenate([ref[pl.ds(base * SUB + j, m, stride=SUB), :] for j in range(SUB)], axis=1)
    hi = lax.bitcast_convert_type(w & jnp.uint32(0xFFFF0000), F32)
    lo = lax.bitcast_convert_type(w << 16, F32)
    return jnp.concatenate([hi, lo], axis=1)


def _cparams(sem):
    return pltpu.CompilerParams(dimension_semantics=sem, vmem_limit_bytes=VMEM_LIMIT)


def _ada_kernel(c_ref, w_ref, b_ref, o_ref):
    c = c_ref[...]
    o_ref[...] = _dot(_silu(c), w_ref[...], precision=HIGHEST) + b_ref[...]


def _ada_mod(c, w, b):
    bsz, d = c.shape
    n = w.shape[1]
    rows = 8
    cp = jnp.zeros((rows, d), F32).at[:bsz].set(c)
    tn = 1536
    out = pl.pallas_call(
        _ada_kernel,
        out_shape=jax.ShapeDtypeStruct((rows, n), F32),
        grid=(n // tn,),
        in_specs=[pl.BlockSpec((rows, d), lambda j: (0, 0)),
                  pl.BlockSpec((d, tn), lambda j: (0, j)),
                  pl.BlockSpec((1, tn), lambda j: (0, j))],
        out_specs=pl.BlockSpec((rows, tn), lambda j: (0, j)),
        compiler_params=_cparams(("arbitrary",)),
        name="ada_mod",
    )(cp, w, b.reshape(1, n))
    return out[:bsz]


def _in_proj_kernel(x_ref, sh_ref, sc_ref, g_ref, whg_ref, wrw_ref, wgt_ref, hg_ref, rw_ref, gt_ref):
    x = x_ref[0]
    ms = jnp.mean(x * x, axis=-1, keepdims=True)
    h = (x * lax.rsqrt(ms + NORM_EPS) * g_ref[...]) * (1.0 + sc_ref[0]) + sh_ref[0]
    hb = h.astype(BF16)
    step = 512
    for n0 in range(0, whg_ref.shape[1], step):
        hg_ref[0, :, n0:n0 + step] = _dot(hb, whg_ref[:, n0:n0 + step])
    for n0 in range(0, wrw_ref.shape[1], 256):
        rw_ref[0, :, n0:n0 + 256] = _dot(hb, wrw_ref[:, n0:n0 + 256])
    for n0 in range(0, wgt_ref.shape[1], step):
        gt_ref[0, :, n0:n0 + step] = _sigmoid(_dot(hb, wgt_ref[:, n0:n0 + step])).astype(BF16)


def _in_proj(x, mod3, norm_g, w_hg, w_rw, w_gt, tm):
    bsz, s, d = x.shape
    n_hg, n_rw, n_gt = w_hg.shape[1], w_rw.shape[1], w_gt.shape[1]
    const = lambda b, i: (0, 0)
    return pl.pallas_call(
        _in_proj_kernel,
        out_shape=(jax.ShapeDtypeStruct((bsz, s, n_hg), F32),
                   jax.ShapeDtypeStruct((bsz, s, n_rw), F32),
                   jax.ShapeDtypeStruct((bsz, s, n_gt), BF16)),
        grid=(bsz, s // tm),
        in_specs=[pl.BlockSpec((1, tm, d), lambda b, i: (b, i, 0)),
                  pl.BlockSpec((1, 1, d), lambda b, i: (b * 6 + 0, 0, 0)),
                  pl.BlockSpec((1, 1, d), lambda b, i: (b * 6 + 1, 0, 0)),
                  pl.BlockSpec((1, d), const),
                  pl.BlockSpec((d, n_hg), const),
                  pl.BlockSpec((d, n_rw), const),
                  pl.BlockSpec((d, n_gt), const)],
        out_specs=(pl.BlockSpec((1, tm, n_hg), lambda b, i: (b, i, 0)),
                   pl.BlockSpec((1, tm, n_rw), lambda b, i: (b, i, 0)),
                   pl.BlockSpec((1, tm, n_gt), lambda b, i: (b, i, 0))),
        compiler_params=_cparams(("arbitrary", "arbitrary")),
        name="in_proj",
    )(x, mod3, mod3, norm_g.reshape(1, d), w_hg, w_rw, w_gt)


_HG_LEVELS = (32, 16, 8, 4, 2, 1)


def _hgrn2_consts(width):
    c = CHUNK
    t = np.arange(c)[:, None]
    s = np.arange(c)[None, :]
    blocks = [(s <= t), (s > t)]
    lvl_masks = []
    right = []
    for h in _HG_LEVELS:
        m = (t // (2 * h)) * 2 * h + h
        is_r = (t & h) != 0
        blk = np.where(is_r, (s >= m) & (s <= t), (s > t) & (s <= m - 1))
        blocks.append(blk)
        lvl_masks.append(is_r & ((s & h) == 0) & ((t // (2 * h)) == (s // (2 * h))))
        right.append(np.broadcast_to(is_r, (c, width)))
    mst = np.concatenate(blocks, axis=0).astype(np.float32)
    lm = np.stack([np.eye(c, dtype=bool)] + lvl_masks).astype(np.float32)
    rm = np.stack(right).astype(np.float32)
    return jnp.asarray(mst, BF16), jnp.asarray(lm, F32), jnp.asarray(rm, F32)


def _hgrn2_kernel(q_ref, f_ref, i_ref, g_ref, lb_ref, ng_ref, mst_ref, lm_ref, rm_ref, o_ref, st_ref):
    c = CHUNK
    n_heads = q_ref.shape[2] // HG_HEAD
    n_chunks = q_ref.shape[1] // c

    @pl.when(pl.program_id(1) == 0)
    def _():
        st_ref[...] = jnp.zeros_like(st_ref)

    mst = mst_ref[...]
    lb = lb_ref[...]
    ng = ng_ref[...]
    heads = [slice(hd * HG_HEAD, (hd + 1) * HG_HEAD) for hd in range(n_heads)]

    def chunk_body(ci, carry):
        r0 = pl.multiple_of(ci * c, c)
        rows = pl.ds(r0, c)
        q = _silu(q_ref[0, rows, :])
        f = lb + (1.0 - lb) * _sigmoid(f_ref[0, rows, :])
        lf = jnp.log(f)
        k = 1.0 - f
        vb = i_ref[0, rows, :].astype(BF16)
        ex = jnp.exp(_dot_exact_lhs(mst, lf))
        ex_cum = ex[0:c]
        qd = (q * ex_cum).astype(BF16)
        kr = (k * ex[c:2 * c]).astype(BF16)
        qb = q.astype(BF16)
        kb = k.astype(BF16)
        sts = [st_ref[hd] for hd in range(n_heads)]
        o = [_dot(qd[:, ls], st.astype(BF16), NT) for ls, st in zip(heads, sts)]
        sc = [lm_ref[0] * _dot(qb[:, ls], kb[:, ls], NT) for ls in heads]
        dqk = q - k
        for li in range(len(_HG_LEVELS)):
            g_l = ((k + rm_ref[li] * dqk) * ex[(2 + li) * c:(3 + li) * c]).astype(BF16)
            sc = [s_h + lm_ref[li + 1] * _dot(g_l[:, ls], g_l[:, ls], NT) for s_h, ls in zip(sc, heads)]
        o = [o_h + _dot(s_h.astype(BF16), vb[:, ls]) for o_h, s_h, ls in zip(o, sc, heads)]
        for hd, ls in enumerate(heads):
            st_ref[hd] = sts[hd] * ex_cum[c - 1:c, ls] + _dot(vb[:, ls], kr[:, ls], TN)
        on = [o_h * lax.rsqrt(jnp.mean(o_h * o_h, axis=-1, keepdims=True) + NORM_EPS) for o_h in o]
        o_full = jnp.concatenate(on, axis=1) * ng
        o_ref[0, rows, :] = (o_full * _silu(g_ref[0, rows, :])).astype(o_ref.dtype)
        return carry

    lax.fori_loop(0, n_chunks, chunk_body, 0)


def _hgrn2(hg, lb, norm_g, ts):
    bsz, s, n4 = hg.shape
    w = n4 // 4
    mst, lm, rm = _hgrn2_consts(w)
    n_heads = w // HG_HEAD
    const2 = lambda b, i: (0, 0)
    const3 = lambda b, i: (0, 0, 0)
    return pl.pallas_call(
        _hgrn2_kernel,
        out_shape=jax.ShapeDtypeStruct((bsz, s, w), BF16),
        grid=(bsz, s // ts),
        in_specs=[pl.BlockSpec((1, ts, w), lambda b, i: (b, i, 0)),
                  pl.BlockSpec((1, ts, w), lambda b, i: (b, i, 1)),
                  pl.BlockSpec((1, ts, w), lambda b, i: (b, i, 2)),
                  pl.BlockSpec((1, ts, w), lambda b, i: (b, i, 3)),
                  pl.BlockSpec((1, w), const2),
                  pl.BlockSpec((1, w), const2),
                  pl.BlockSpec(mst.shape, const2),
                  pl.BlockSpec(lm.shape, const3),
                  pl.BlockSpec(rm.shape, const3)],
        out_specs=pl.BlockSpec((1, ts, w), lambda b, i: (b, i, 0)),
        scratch_shapes=[pltpu.VMEM((n_heads, HG_HEAD, HG_HEAD), F32)],
        compiler_params=_cparams(("arbitrary", "arbitrary")),
        name="hgrn2",
    )(hg, hg, hg, hg, lb.reshape(1, w), norm_g.reshape(1, w), mst, lm, rm)


def _rwkv_consts(width):
    c = CHUNK
    t = np.arange(c)[:, None]
    s = np.arange(c)[None, :]
    tri = (s <= t).astype(np.float32)
    tt = np.arange(2 * c)[:, None]
    ss = np.arange(2 * c)[None, :]
    same = (tt // c) == (ss // c)
    strict = same & ((ss % c) < (tt % c))
    incl = same & ((ss % c) <= (tt % c))
    hsum = (np.arange(width)[:, None] // RW_HEAD) == (np.arange(width)[None, :] // RW_HEAD)
    return (jnp.asarray(tri, BF16), jnp.asarray(strict.astype(np.float32), F32),
            jnp.asarray(incl.astype(np.float32), F32), jnp.asarray(hsum.astype(np.float32), BF16))


def _rwkv7_kernel(p_ref, mu_ref, w0_ref, a0_ref, kk_ref, ka_ref, rk_ref, gnw_ref, gnb_ref,
                  w2_ref, a2_ref, g2_ref, tri_ref, sm_ref, im_ref, hs_ref,
                  o_ref, carry_ref, zt_ref):
    c = CHUNK
    nb = p_ref.shape[0]
    width = o_ref.shape[2]
    n_pairs = width // LANES

    @pl.when(pl.program_id(0) == 0)
    def _():
        carry_ref[...] = jnp.zeros_like(carry_ref)
        zt_ref[...] = jnp.zeros_like(zt_ref)

    hs = hs_ref[...]
    tri = tri_ref[...]
    smask = sm_ref[...] > 0
    imask = im_ref[...] > 0
    lane = lax.broadcasted_iota(jnp.int32, (c, LANES), 1)
    m0 = (lane < RW_HEAD).astype(F32)
    m1 = 1.0 - m0

    def stack(x):
        return jnp.concatenate([x * m0, x * m1], axis=0)

    rs, k2s, vs_, gs = [], [], [], []
    units = []
    for b in range(nb):
        p = p_ref[b]
        row = lax.broadcasted_iota(jnp.int32, p.shape, 0)
        prev = jnp.where(row == 0, carry_ref[b], pltpu.roll(p, 1, 0))
        carry_ref[b] = p[c - 1:c, :]
        xs = p + mu_ref[...] * (prev - p)
        r = xs[:, 0:width]
        k = xs[:, width:2 * width]
        v = xs[:, 2 * width:3 * width]
        slab = xs[:, 3 * width:]
        nz = -(w0_ref[...] + _dot(jnp.tanh(slab).astype(BF16), w2_ref[...]))
        softplus = jnp.maximum(nz, 0.0) + jnp.log(1.0 + jnp.exp(-jnp.abs(nz)))
        ld = -jnp.exp(-softplus - 0.5)
        a = _sigmoid(a0_ref[...] + _dot(slab.astype(BF16), a2_ref[...]))
        gs.append(_dot(_sigmoid(slab).astype(BF16), g2_ref[...]))
        kk0 = k * kk_ref[...]
        kk = kk0 * lax.rsqrt(jnp.maximum(_dot_exact_rhs(kk0 * kk0, hs), 1e-24))
        k2 = k * (1.0 + (a - 1.0) * ka_ref[...])
        a_in = -kk
        b_in = kk * a
        cum = _dot_exact_lhs(tri, ld)
        cum_t = cum[c - 1:c, :]
        e_c = jnp.exp(cum)
        e_nc = jnp.exp(-cum)
        e_rem = jnp.exp(cum_t - cum)
        at_f = a_in * jnp.exp(cum - ld)
        rt_f = r * e_c
        kt_f = k2 * e_nc
        bt_f = b_in * e_nc
        kh_f = k2 * e_rem
        bh_f = b_in * e_rem
        p_t = jnp.exp(cum_t)
        rs.append(r)
        k2s.append(k2)
        vs_.append(v)
        for pi in range(n_pairs):
            ls = slice(pi * LANES, (pi + 1) * LANES)
            units.append(dict(
                b=b, pi=pi,
                at=stack(at_f[:, ls]).astype(BF16), rt=stack(rt_f[:, ls]).astype(BF16),
                kt=stack(kt_f[:, ls]).astype(BF16), bt=stack(bt_f[:, ls]).astype(BF16),
                kh=stack(kh_f[:, ls]).astype(BF16), bh=stack(bh_f[:, ls]).astype(BF16),
                vs=stack(v[:, ls]).astype(BF16), p_t=p_t[:, ls]))

    for u in units:
        lhs = jnp.concatenate([u['at'], u['rt']], axis=0)
        u['gk'] = _dot(lhs, u['kt'], NT)
        u['gb'] = _dot(lhs, u['bt'], NT)
    for u in units:
        gk, gb = u.pop('gk'), u.pop('gb')
        a_kk = jnp.concatenate([jnp.where(smask, gk[:2 * c], 0.0), jnp.where(imask, gk[2 * c:], 0.0)], axis=0)
        u['pw'] = jnp.where(smask, gb[:2 * c], 0.0).astype(BF16)
        u['a_rb'] = jnp.where(imask, gb[2 * c:], 0.0).astype(BF16)
        u['a_kk'] = a_kk.astype(BF16)
    for u in units:
        av = _dot(u.pop('a_kk'), u['vs'])
        u['arkv'] = av[2 * c:]
        u['x'] = jnp.concatenate([u['at'].astype(F32), av[:2 * c]], axis=1)
    n_lvl = int(np.log2(c))
    for lvl in range(n_lvl):
        for u in units:
            u['x'] = u['x'] + _dot(u['pw'], u['x'].astype(BF16))
        if lvl + 1 < n_lvl:
            for u in units:
                u['pw'] = _dot(u['pw'], u['pw']).astype(BF16)
    for u in units:
        x = u.pop('x')
        u['zt'] = zt_ref[u['b'], u['pi']]
        u['uy'] = _dot(jnp.concatenate([x[:, :LANES].astype(BF16), u['rt']], axis=0), u['zt'].astype(BF16), NT)
        u['u_loc'] = x[:, LANES:]
    for u in units:
        uy = u.pop('uy')
        u['u'] = (uy[:2 * c] + u.pop('u_loc')).astype(BF16)
        u['y0'] = uy[2 * c:] + u.pop('arkv')
    for u in units:
        u['y'] = u.pop('y0') + _dot(u['a_rb'], u['u'])
        upd = _dot(jnp.concatenate([u['u'], u['vs']], axis=0), jnp.concatenate([u['bh'], u['kh']], axis=0), TN)
        zt_ref[u['b'], u['pi']] = u['zt'] * u['p_t'] + upd

    inv_n = 1.0 / RW_HEAD
    for b in range(nb):
        ys = [u['y'] for u in units if u['b'] == b]
        y = jnp.concatenate([yy[:c] + yy[c:] for yy in ys], axis=1)
        mean = _dot_exact_rhs(y, hs) * inv_n
        d = y - mean
        var = _dot_exact_rhs(d * d, hs) * inv_n
        yn = d * lax.rsqrt(var + RW_GN_EPS) * gnw_ref[...] + gnb_ref[...]
        bonus = _dot_exact_rhs(rs[b] * k2s[b] * rk_ref[...], hs) * vs_[b]
        o_ref[b] = ((yn + bonus) * gs[b]).astype(o_ref.dtype)


def _rwkv7(rw, mu, w0, w2, a0, a2, g2, k_k, k_a, r_k, gn_w, gn_b):
    bsz, s, cols = rw.shape
    width = w0.shape[-1]
    n_pairs = width // LANES
    slab = cols - 3 * width
    dl, al, gl = w2.shape[0], a2.shape[0], g2.shape[0]
    w2f = jnp.zeros((slab, width), F32).at[0:dl].set(w2).astype(BF16)
    a2f = jnp.zeros((slab, width), F32).at[dl:dl + al].set(a2).astype(BF16)
    g2f = jnp.zeros((slab, width), F32).at[dl + al:dl + al + gl].set(g2).astype(BF16)
    mup = jnp.zeros((1, cols), F32).at[0, :mu.shape[-1]].set(mu)
    tri, sm, im, hs = _rwkv_consts(width)
    row = lambda x: x.reshape(1, width)
    const = lambda i: (0, 0)
    vec = pl.BlockSpec((1, width), const)
    return pl.pallas_call(
        _rwkv7_kernel,
        out_shape=jax.ShapeDtypeStruct((bsz, s, width), BF16),
        grid=(s // CHUNK,),
        in_specs=[pl.BlockSpec((bsz, CHUNK, cols), lambda i: (0, i, 0)),
                  pl.BlockSpec((1, cols), const),
                  vec, vec, vec, vec, vec, vec, vec,
                  pl.BlockSpec((slab, width), const),
                  pl.BlockSpec((slab, width), const),
                  pl.BlockSpec((slab, width), const),
                  pl.BlockSpec(tri.shape, const),
                  pl.BlockSpec(sm.shape, const),
                  pl.BlockSpec(im.shape, const),
                  pl.BlockSpec(hs.shape, const)],
        out_specs=pl.BlockSpec((bsz, CHUNK, width), lambda i: (0, i, 0)),
        scratch_shapes=[pltpu.VMEM((bsz, 1, cols), F32),
                        pltpu.VMEM((bsz, n_pairs, LANES, LANES), F32)],
        compiler_params=_cparams(("arbitrary",)),
        name="rwkv7",
    )(rw, mup, row(w0), row(a0), row(k_k), row(k_a), row(r_k), row(gn_w), row(gn_b),
      w2f, a2f, g2f, tri, sm, im, hs)


def _out_proj_kernel(n_groups, n_experts,
                     x_ref, oa_ref, ob_ref, ga_ref, gb_ref, gt1_ref, sc2_ref, sh2_ref, g2_ref,
                     wa_ref, wb_ref, wo_ref, wr_ref, wrl_ref, br_ref, tril_ref,
                     x1_ref, h2_ref, route_ref, cnt_ref, carry_ref):
    tm = x_ref.shape[1]
    first = (pl.program_id(0) == 0) & (pl.program_id(1) == 0)

    @pl.when(first)
    def _():
        carry_ref[...] = jnp.zeros_like(carry_ref)

    pa = _dot(oa_ref[0], wa_ref[...])
    pb = _dot(ob_ref[0], wb_ref[...])
    mixed = ga_ref[0].astype(F32) * pa + gb_ref[0].astype(F32) * pb
    x1 = x_ref[0] + gt1_ref[0] * _dot(mixed.astype(BF16), wo_ref[...])
    x1_ref[0] = x1
    ms = jnp.mean(x1 * x1, axis=-1, keepdims=True)
    h2 = (x1 * lax.rsqrt(ms + NORM_EPS) * g2_ref[...]) * (1.0 + sc2_ref[0]) + sh2_ref[0]
    _rows_to_tiles(h2_ref, h2)

    h2_hi = h2.astype(BF16)
    h2_lo = (h2 - h2_hi.astype(F32)).astype(BF16)
    logits = (_dot(h2_hi, wr_ref[...]) + _dot(h2_lo, wr_ref[...]) + _dot(h2_hi, wrl_ref[...])) + br_ref[...]
    lane = lax.broadcasted_iota(jnp.int32, logits.shape, 1)
    neg = jnp.float32(-jnp.inf)
    big = jnp.int32(1 << 20)
    eg = n_experts // n_groups
    is_g = (lane >= n_experts) & (lane < n_experts + n_groups)
    lg = jnp.where(is_g, logits, neg)
    mg = jnp.max(lg, axis=-1, keepdims=True)
    p_grp = 1.0 / jnp.sum(jnp.where(is_g, jnp.exp(lg - mg), 0.0), axis=-1, keepdims=True)
    gidx = jnp.min(jnp.where(lg == mg, lane, big), axis=-1, keepdims=True) - n_experts
    sel = (lane >= gidx * eg) & (lane < gidx * eg + eg)
    le = jnp.where(sel, logits, neg)
    me = jnp.max(le, axis=-1, keepdims=True)
    pe_un = jnp.where(sel, jnp.exp(le - me), 0.0)
    pe = jnp.where(sel, pe_un / jnp.sum(pe_un, axis=-1, keepdims=True), -1.0)
    v1 = jnp.max(pe, axis=-1, keepdims=True)
    i1 = jnp.min(jnp.where(pe == v1, lane, big), axis=-1, keepdims=True)
    pe2 = jnp.where(lane == i1, -1.0, pe)
    v2 = jnp.max(pe2, axis=-1, keepdims=True)
    i2 = jnp.min(jnp.where(pe2 == v2, lane, big), axis=-1, keepdims=True)
    wsum = v1 + v2
    w1 = p_grp * v1 / wsum
    w2 = p_grp * v2 / wsum

    oh1 = (lane == i1).astype(F32)
    oh2 = (lane == i2).astype(F32)
    both = oh1 + oh2
    before = _dot(tril_ref[...], both.astype(BF16)) + carry_ref[...]
    rank1 = jnp.sum(oh1 * before, axis=-1, keepdims=True)
    rank2 = jnp.sum(oh2 * before, axis=-1, keepdims=True)
    carry_ref[...] = carry_ref[...] + jnp.sum(both, axis=0, keepdims=True)
    cnt_ref[...] = carry_ref[...]

    out = jnp.where(lane == 0, w1, 0.0)
    out = jnp.where(lane == 1, w2, out)
    out = jnp.where(lane == 2, i1.astype(F32), out)
    out = jnp.where(lane == 3, i2.astype(F32), out)
    out = jnp.where(lane == 4, rank1, out)
    out = jnp.where(lane == 5, rank2, out)
    route_ref[0] = out


def _out_proj(x, o_a, o_b, gates, mod3, norm2_g, wa, wb, wo, wr, br, n_groups, n_experts, tm):
    bsz, s, d = x.shape
    wdt = o_a.shape[-1]
    tril = jnp.asarray(np.tril(np.ones((tm, tm), np.float32), -1), BF16)
    wr_hi = wr.astype(BF16)
    wr_lo = (wr - wr_hi.astype(F32)).astype(BF16)
    const = lambda b, i: (0, 0)
    tile = lambda b, i: (b, i, 0)
    kern = functools.partial(_out_proj_kernel, n_groups, n_experts)
    return pl.pallas_call(
        kern,
        out_shape=(jax.ShapeDtypeStruct((bsz, s, d), F32),
                   jax.ShapeDtypeStruct((bsz * s * SUB, LANES), U32),
                   jax.ShapeDtypeStruct((bsz, s, LANES), F32),
                   jax.ShapeDtypeStruct((1, LANES), F32)),
        grid=(bsz, s // tm),
        in_specs=[pl.BlockSpec((1, tm, d), tile),
                  pl.BlockSpec((1, tm, wdt), tile),
                  pl.BlockSpec((1, tm, wdt), tile),
                  pl.BlockSpec((1, tm, d), lambda b, i: (b, i, 0)),
                  pl.BlockSpec((1, tm, d), lambda b, i: (b, i, 1)),
                  pl.BlockSpec((1, 1, d), lambda b, i: (b * 6 + 2, 0, 0)),
                  pl.BlockSpec((1, 1, d), lambda b, i: (b * 6 + 4, 0, 0)),
                  pl.BlockSpec((1, 1, d), lambda b, i: (b * 6 + 3, 0, 0)),
                  pl.BlockSpec((1, d), const),
                  pl.BlockSpec(wa.shape, const),
                  pl.BlockSpec(wb.shape, const),
                  pl.BlockSpec(wo.shape, const),
                  pl.BlockSpec(wr.shape, const),
                  pl.BlockSpec(wr.shape, const),
                  pl.BlockSpec((1, LANES), const),
                  pl.BlockSpec((tm, tm), const)],
        out_specs=(pl.BlockSpec((1, tm, d), tile),
                   pl.BlockSpec((tm * SUB, LANES), lambda b, i: (b * (s // tm) + i, 0)),
                   pl.BlockSpec((1, tm, LANES), tile),
                   pl.BlockSpec((1, LANES), const)),
        scratch_shapes=[pltpu.VMEM((1, LANES), F32)],
        compiler_params=_cparams(("arbitrary", "arbitrary")),
        name="out_proj",
    )(x, o_a, o_b, gates, gates, mod3, mod3, mod3, norm2_g.reshape(1, d), wa, wb, wo, wr_hi, wr_lo, br, tril)


def _moe_dispatch_kernel(tm, n_tok, blk, dest_ref, zstart_ref, zcnt_ref, nused_ref, h_ref, xb_ref, smap_ref,
                         zbuf, stage, sem, zsem):
    i = pl.program_id(0)
    n_steps = pl.num_programs(0)
    n_slots = smap_ref.shape[0]
    n_experts = zcnt_ref.shape[0]

    @pl.when(i == 0)
    def _():
        def init(s_, carry):
            smap_ref[s_] = 0
            return carry
        lax.fori_loop(nused_ref[0] * blk, n_slots, init, 0)
        zbuf[...] = jnp.zeros_like(zbuf)

        def zero_row(e, j):
            dst = pl.multiple_of((zstart_ref[e] + j) * SUB, SUB)
            return pltpu.make_async_copy(zbuf.at[pl.ds(0, SUB), :], xb_ref.at[pl.ds(dst, SUB), :], zsem)

        def zero_block(b):
            dst = pl.multiple_of(b * (blk * SUB), blk * SUB)
            return pltpu.make_async_copy(zbuf, xb_ref.at[pl.ds(dst, blk * SUB), :], zsem)

        for e in range(n_experts):
            def zstart(j, carry, e=e):
                zero_row(e, j).start()
                smap_ref[zstart_ref[e] + j] = TOP_K * n_tok + e * blk + j
                return carry
            lax.fori_loop(0, zcnt_ref[e], zstart, 0)

        def bstart(b, carry):
            zero_block(b).start()
            return carry
        lax.fori_loop(nused_ref[0], n_slots // blk, bstart, 0)
        for e in range(n_experts):
            def zwait(j, carry, e=e):
                zero_row(e, j).wait()
                return carry
            lax.fori_loop(0, zcnt_ref[e], zwait, 0)

        def bwait(b, carry):
            zero_block(b).wait()
            return carry
        lax.fori_loop(nused_ref[0], n_slots // blk, bwait, 0)

    slot = i % 2
    stage[slot] = h_ref[...]
    for r in range(tm):
        tok = i * tm + r
        for k in range(TOP_K):
            d = dest_ref[tok * TOP_K + k]
            dst = pl.multiple_of(d * SUB, SUB)
            pltpu.make_async_copy(stage.at[slot, pl.ds(r * SUB, SUB), :], xb_ref.at[pl.ds(dst, SUB), :],
                                  sem.at[slot]).start(priority=k)
            smap_ref[d] = k * n_tok + tok

    def wait_step(s_):
        for _ in range(TOP_K):
            pltpu.make_async_copy(stage.at[s_], xb_ref.at[pl.ds(0, tm * SUB), :], sem.at[s_]).wait()

    @pl.when(i > 0)
    def _():
        wait_step(1 - slot)

    @pl.when(i == n_steps - 1)
    def _():
        wait_step(slot)


def _moe_dispatch(h2, dest, zstart, zcnt, n_used, n_slots, blk, tm):
    n_tok = h2.shape[0] // SUB
    grid_spec = pltpu.PrefetchScalarGridSpec(
        num_scalar_prefetch=4,
        grid=(n_tok // tm,),
        in_specs=[pl.BlockSpec((tm * SUB, LANES), lambda i, *_: (i, 0))],
        out_specs=(pl.BlockSpec(memory_space=pl.ANY),
                   pl.BlockSpec(memory_space=pltpu.SMEM)),
        scratch_shapes=[pltpu.VMEM((blk * SUB, LANES), U32),
                        pltpu.VMEM((2, tm * SUB, LANES), U32),
                        pltpu.SemaphoreType.DMA((2,)),
                        pltpu.SemaphoreType.DMA],
    )
    return pl.pallas_call(
        functools.partial(_moe_dispatch_kernel, tm, n_tok, blk),
        out_shape=(jax.ShapeDtypeStruct((n_slots * SUB, LANES), U32),
                   jax.ShapeDtypeStruct((n_slots,), jnp.int32)),
        grid_spec=grid_spec,
        compiler_params=_cparams(("arbitrary",)),
        name="moe_dispatch",
    )(dest, zstart, zcnt, n_used, h2)


def _moe_expert_kernel(n_tok, blk_e_ref, nused_ref, smap_ref, x_ref, wg_ref, wu_ref, wd_ref, y_ref,
                       ystage, wgb, wub, wdb, sem):
    i = pl.program_id(0)
    n_used = nused_ref[0]
    blk = x_ref.shape[0] // SUB
    first_real_blocks = TOP_K * n_tok // blk
    first_spare = y_ref.shape[0] // SUB - blk

    def issue(b, slot, rows):
        for r in rows:
            t = jnp.where(b >= 0, smap_ref[jnp.maximum(b, 0) * blk + r], first_spare + r)
            dst = pl.multiple_of(t * SUB, SUB)
            pltpu.make_async_copy(ystage.at[slot, pl.ds(r * SUB, SUB), :], y_ref.at[pl.ds(dst, SUB), :],
                                  sem.at[slot]).start(priority=r % 2)

    def wait_block(slot):
        pltpu.make_async_copy(ystage.at[slot], y_ref.at[pl.ds(0, blk * SUB), :], sem.at[slot]).wait()

    @pl.when(i == 0)
    def _():
        ystage[...] = jnp.zeros_like(ystage)
        n_spare_blocks = y_ref.shape[0] // (blk * SUB) - first_real_blocks

        def spare_copy(c):
            dst = (first_real_blocks + c) * blk * SUB
            return pltpu.make_async_copy(ystage.at[0], y_ref.at[pl.ds(dst, blk * SUB), :], sem.at[0])

        for c in range(n_spare_blocks):
            spare_copy(c).start()
        for c in range(n_spare_blocks):
            spare_copy(c).wait()

    new_expert = (i == 0) | (blk_e_ref[i] != blk_e_ref[jnp.maximum(i - 1, 0)])

    @pl.when((i < n_used) & new_expert)
    def _():
        wgb[...] = wg_ref[0].astype(BF16)
        wub[...] = wu_ref[0].astype(BF16)
        wdb[...] = wd_ref[0].astype(BF16)

    @pl.when(i < n_used)
    def _():
        slot = i % 2
        pslot = 1 - slot

        @pl.when(i > 0)
        def _():
            wait_block(slot)

        q = blk // 4
        xb = _tiles_to_rows(x_ref, blk).astype(BF16)
        issue(i - 1, pslot, range(0, q))
        hg = _dot(xb, wgb[...])
        issue(i - 1, pslot, range(q, 2 * q))
        hu = _dot(xb, wub[...])
        issue(i - 1, pslot, range(2 * q, 3 * q))
        hid = (_silu(hg) * hu).astype(BF16)
        y = _dot(hid, wdb[...])
        issue(i - 1, pslot, range(3 * q, blk))
        _rows_to_tiles(ystage.at[slot], y)

        @pl.when(i == n_used - 1)
        def _():
            issue(i, slot, range(blk))
            wait_block(pslot)
            wait_block(slot)


def _moe_experts(xb, smap, blk_e, n_used, w_gate, w_up, w_down, blk, n_tok):
    d = w_gate.shape[1]
    assert d == 2 * SUB * LANES and xb.shape[1] == LANES
    n_slots = smap.shape[0]
    nb = n_slots // blk
    f = w_gate.shape[-1]
    n_experts = w_gate.shape[0]
    assert (TOP_K * n_tok) % blk == 0
    n_tiles = TOP_K * n_tok + (n_experts + 1) * blk
    grid_spec = pltpu.PrefetchScalarGridSpec(
        num_scalar_prefetch=3,
        grid=(nb,),
        in_specs=[pl.BlockSpec((blk * SUB, LANES), lambda i, be, nu, sm: (jnp.minimum(i, nu[0] - 1), 0)),
                  pl.BlockSpec((1, d, f), lambda i, be, nu, sm: (be[i], 0, 0)),
                  pl.BlockSpec((1, d, f), lambda i, be, nu, sm: (be[i], 0, 0)),
                  pl.BlockSpec((1, f, d), lambda i, be, nu, sm: (be[i], 0, 0))],
        out_specs=pl.BlockSpec(memory_space=pl.ANY),
        scratch_shapes=[pltpu.VMEM((2, blk * SUB, LANES), U32),
                        pltpu.VMEM((d, f), BF16),
                        pltpu.VMEM((d, f), BF16),
                        pltpu.VMEM((f, d), BF16),
                        pltpu.SemaphoreType.DMA((2,))],
    )
    return pl.pallas_call(
        functools.partial(_moe_expert_kernel, n_tok),
        out_shape=jax.ShapeDtypeStruct((n_tiles * SUB, LANES), U32),
        grid_spec=grid_spec,
        compiler_params=_cparams(("arbitrary",)),
        name="moe_experts",
    )(blk_e, n_used, smap, xb, w_gate, w_up, w_down)


def _moe_combine_kernel(x1_ref, route_ref, gt2_ref, fg_ref, y1_ref, y2_ref, o_ref):
    tm = x1_ref.shape[0]
    route = route_ref[...]
    moe = route[:, 0:1] * _tiles_to_rows(y1_ref, tm) + route[:, 1:2] * _tiles_to_rows(y2_ref, tm)
    xo = x1_ref[...] + gt2_ref[0] * moe
    ms = jnp.mean(xo * xo, axis=-1, keepdims=True)
    o_ref[...] = xo * lax.rsqrt(ms + NORM_EPS) * fg_ref[...]


def _moe_combine(x1, route, yt, mod3, final_g, s, tm):
    n, d = x1.shape
    tiles_per_batch = s // tm
    n_steps = n // tm
    return pl.pallas_call(
        _moe_combine_kernel,
        out_shape=jax.ShapeDtypeStruct((n, d), F32),
        grid=(n_steps,),
        in_specs=[pl.BlockSpec((tm, d), lambda i: (i, 0)),
                  pl.BlockSpec((tm, LANES), lambda i: (i, 0)),
                  pl.BlockSpec((1, 1, d), lambda i: ((i // tiles_per_batch) * 6 + 5, 0, 0)),
                  pl.BlockSpec((1, d), lambda i: (0, 0)),
                  pl.BlockSpec((tm * SUB, LANES), lambda i: (i, 0)),
                  pl.BlockSpec((tm * SUB, LANES), lambda i: (n_steps + i, 0))],
        out_specs=pl.BlockSpec((tm, d), lambda i: (i, 0)),
        compiler_params=_cparams(("arbitrary",)),
        name="moe_combine",
    )(x1, route, mod3, final_g.reshape(1, d), yt, yt)


def _pick(n, candidates):
    for t in candidates:
        if n % t == 0:
            return t
    raise ValueError(f"no tile in {candidates} divides {n}")


def kernel(x, c, ada_w, ada_b, norm1_g, w_in, hg_lb, hg_norm_g, rw_mu, rw_w0, rw_w2, rw_a0, rw_a2, rw_g2, rw_kk, rw_ka, rw_rk, rw_gn_w, rw_gn_b, w_proj_a, w_proj_b, w_out, norm2_g, router_g_w, router_g_b, router_e_w, router_e_b, exp_w_gate, exp_w_up, exp_w_down, final_g):
    bsz, s, d = x.shape
    depth = ada_w.shape[0]
    hg_f = hg_lb.shape[-1]
    hg_w = hg_norm_g.shape[-1]
    rw_w = rw_w0.shape[-1]
    rw_cols = rw_mu.shape[-1]
    n_groups = router_g_w.shape[-1]
    n_experts = router_e_w.shape[-1]
    assert hg_f == hg_w and s % CHUNK == 0 and n_experts + n_groups <= LANES and d == 2 * SUB * LANES

    lb_all = jnp.cumsum(jax.nn.softmax(hg_lb.astype(F32), axis=0), axis=0)
    n = bsz * s
    blk = 256
    n_blocks = (n * TOP_K + n_experts * blk) // blk
    for l in range(depth):
        mod = _ada_mod(c, ada_w[l], ada_b[l])
        mod3 = mod.reshape(bsz * 6, 1, d)

        hg_cols = 2 * hg_f + 2 * hg_w
        rw_pad = -(-rw_cols // 256) * 256
        wl = w_in[l]
        w_hg = wl[:, :hg_cols].astype(BF16)
        w_rw = jnp.zeros((d, rw_pad), BF16).at[:, :rw_cols].set(wl[:, hg_cols:hg_cols + rw_cols].astype(BF16))
        w_gt = wl[:, hg_cols + rw_cols:].astype(BF16)
        hg, rw, gates = _in_proj(x, mod3, norm1_g[l], w_hg, w_rw, w_gt, _pick(s, (512, 256, 128, 64)))

        o_a = _hgrn2(hg, lb_all[l], hg_norm_g[l], _pick(s, (512, 256, 128, 64)))
        o_b = _rwkv7(rw, rw_mu[l], rw_w0[l], rw_w2[l], rw_a0[l], rw_a2[l], rw_g2[l],
                     rw_kk[l], rw_ka[l], rw_rk[l].reshape(-1), rw_gn_w[l], rw_gn_b[l])

        wr = jnp.zeros((d, LANES), F32).at[:, :n_experts].set(router_e_w[l])
        wr = wr.at[:, n_experts:n_experts + n_groups].set(router_g_w[l])
        br = jnp.zeros((1, LANES), F32).at[0, :n_experts].set(router_e_b[l])
        br = br.at[0, n_experts:n_experts + n_groups].set(router_g_b[l])
        x1, h2, route, counts = _out_proj(
            x, o_a, o_b, gates, mod3, norm2_g[l],
            w_proj_a[l].astype(BF16), w_proj_b[l].astype(BF16), w_out[l].astype(BF16),
            wr, br, n_groups, n_experts, _pick(s, (512, 256, 128, 64)))

        route2 = route.reshape(n, LANES)
        eid = route2[:, 2:4].astype(jnp.int32)
        rank = route2[:, 4:6].astype(jnp.int32)
        cnt = counts[0, :n_experts].astype(jnp.int32)
        padded = (cnt + blk - 1) // blk * blk
        pad_end = jnp.cumsum(padded)
        pad_start = pad_end - padded
        dest = pad_start[eid] + rank
        blk_start = jnp.arange(n_blocks, dtype=jnp.int32) * blk
        blk_e = jnp.minimum(jnp.sum((pad_end[None, :] <= blk_start[:, None]).astype(jnp.int32), axis=1),
                            n_experts - 1)
        n_used = (pad_end[-1:] // blk).astype(jnp.int32)
        tm = _pick(s, (256, 128, 64))
        xb, smap = _moe_dispatch(h2, dest.reshape(-1), pad_start + cnt, padded - cnt, n_used,
                                 n_blocks * blk, blk, tm)
        yt = _moe_experts(xb, smap, blk_e, n_used, exp_w_gate[l], exp_w_up[l], exp_w_down[l], blk, n)
        last = l == depth - 1
        assert last, "the final RMSNorm is fused into the last layer's combine"
        out = _moe_combine(x1.reshape(n, d), route2, yt, mod3, final_g, s, tm)
        x = out.reshape(bsz, s, d)
    return x
```

```python
import functools

import numpy as np
import jax
import jax.numpy as jnp
from jax import lax
from jax.experimental import pallas as pl
from jax.experimental.pallas import tpu as pltpu

F32 = jnp.float32
BF16 = jnp.bfloat16
HIGHEST = lax.Precision.HIGHEST

NORM_EPS = 1e-6
HG_HEAD = 128
RW_HEAD = 64
RW_GN_EPS = 64e-5
TOP_K = 2
CHUNK = 64
LANES = 128
SUB = 4
U32 = jnp.uint32
VMEM_LIMIT = 56 * 1024 * 1024

NT = (((1,), (1,)), ((), ()))
TN = (((0,), (0,)), ((), ()))


def _dot(a, b, dims=None, precision=None):
    if dims is None:
        return jnp.dot(a, b, preferred_element_type=F32, precision=precision)
    return lax.dot_general(a, b, dims, preferred_element_type=F32, precision=precision)


def _split3(x):
    hi = x.astype(BF16)
    r1 = x - hi.astype(F32)
    mid = r1.astype(BF16)
    lo = (r1 - mid.astype(F32)).astype(BF16)
    return hi, mid, lo


def _dot_exact_lhs(m3_bf16, x):
    return _dot(m3_bf16, jnp.concatenate(_split3(x), axis=0))


def _dot_exact_rhs(x, m3_bf16):
    return _dot(jnp.concatenate(_split3(x), axis=1), m3_bf16)


def _sigmoid(x):
    return 1.0 / (1.0 + jnp.exp(-x))


def _silu(x):
    return x * _sigmoid(x)


def _rows_to_tiles(ref, val):
    m, half = val.shape[0], val.shape[1] // 2
    hi = lax.bitcast_convert_type(val[:, :half].astype(BF16).astype(F32), U32)
    lo = lax.bitcast_convert_type(val[:, half:].astype(BF16).astype(F32), U32)
    w = (hi & jnp.uint32(0xFFFF0000)) | (lo >> 16)
    for j in range(SUB):
        ref[pl.ds(j, m, stride=SUB), :] = w[:, j * LANES:(j + 1) * LANES]


def _tiles_to_rows(ref, m, base=0):
    w = jnp.concatenate([ref[pl.ds(base * SUB + j, m, stride=SUB), :] for j in range(SUB)], axis=1)
    hi = lax.bitcast_convert_type(w & jnp.uint32(0xFFFF0000), F32)
    lo = lax.bitcast_convert_type(w << 16, F32)
    return jnp.concatenate([hi, lo], axis=1)


def _cparams(sem):
    return pltpu.CompilerParams(dimension_semantics=sem, vmem_limit_bytes=VMEM_LIMIT)


def _ada_kernel(c_ref, w_ref, b_ref, o_ref):
    c = c_ref[...]
    o_ref[...] = _dot(_silu(c), w_ref[...], precision=HIGHEST) + b_ref[...]


def _ada_mod(c, w, b):
    bsz, d = c.shape
    n = w.shape[1]
    rows = 8
    cp = jnp.zeros((rows, d), F32).at[:bsz].set(c)
    tn = 1536
    out = pl.pallas_call(
        _ada_kernel,
        out_shape=jax.ShapeDtypeStruct((rows, n), F32),
        grid=(n // tn,),
        in_specs=[pl.BlockSpec((rows, d), lambda j: (0, 0)),
                  pl.BlockSpec((d, tn), lambda j: (0, j)),
                  pl.BlockSpec((1, tn), lambda j: (0, j))],
        out_specs=pl.BlockSpec((rows, tn), lambda j: (0, j)),
        compiler_params=_cparams(("arbitrary",)),
        name="ada_mod",
    )(cp, w, b.reshape(1, n))
    return out[:bsz]


def _in_proj_kernel(x_ref, sh_ref, sc_ref, g_ref, whg_ref, wrw_ref, wgt_ref, hg_ref, rw_ref, gt_ref):
    x = x_ref[0]
    ms = jnp.mean(x * x, axis=-1, keepdims=True)
    h = (x * lax.rsqrt(ms + NORM_EPS) * g_ref[...]) * (1.0 + sc_ref[0]) + sh_ref[0]
    hb = h.astype(BF16)
    step = 512
    for n0 in range(0, whg_ref.shape[1], step):
        hg_ref[0, :, n0:n0 + step] = _dot(hb, whg_ref[:, n0:n0 + step])
    for n0 in range(0, wrw_ref.shape[1], 256):
        rw_ref[0, :, n0:n0 + 256] = _dot(hb, wrw_ref[:, n0:n0 + 256])
    for n0 in range(0, wgt_ref.shape[1], step):
        gt_ref[0, :, n0:n0 + step] = _sigmoid(_dot(hb, wgt_ref[:, n0:n0 + step])).astype(BF16)


def _in_proj(x, mod3, norm_g, w_hg, w_rw, w_gt, tm):
    bsz, s, d = x.shape
    n_hg, n_rw, n_gt = w_hg.shape[1], w_rw.shape[1], w_gt.shape[1]
    const = lambda b, i: (0, 0)
    return pl.pallas_call(
        _in_proj_kernel,
        out_shape=(jax.ShapeDtypeStruct((bsz, s, n_hg), F32),
                   jax.ShapeDtypeStruct((bsz, s, n_rw), F32),
                   jax.ShapeDtypeStruct((bsz, s, n_gt), BF16)),
        grid=(bsz, s // tm),
        in_specs=[pl.BlockSpec((1, tm, d), lambda b, i: (b, i, 0)),
                  pl.BlockSpec((1, 1, d), lambda b, i: (b * 6 + 0, 0, 0)),
                  pl.BlockSpec((1, 1, d), lambda b, i: (b * 6 + 1, 0, 0)),
                  pl.BlockSpec((1, d), const),
                  pl.BlockSpec((d, n_hg), const),
                  pl.BlockSpec((d, n_rw), const),
                  pl.BlockSpec((d, n_gt), const)],
        out_specs=(pl.BlockSpec((1, tm, n_hg), lambda b, i: (b, i, 0)),
                   pl.BlockSpec((1, tm, n_rw), lambda b, i: (b, i, 0)),
                   pl.BlockSpec((1, tm, n_gt), lambda b, i: (b, i, 0))),
        compiler_params=_cparams(("arbitrary", "arbitrary")),
        name="in_proj",
    )(x, mod3, mod3, norm_g.reshape(1, d), w_hg, w_rw, w_gt)


_HG_LEVELS = (32, 16, 8, 4, 2, 1)


def _hgrn2_consts(width):
    c = CHUNK
    t = np.arange(c)[:, None]
    s = np.arange(c)[None, :]
    blocks = [(s <= t), (s > t)]
    lvl_masks = []
    right = []
    for h in _HG_LEVELS:
        m = (t // (2 * h)) * 2 * h + h
        is_r = (t & h) != 0
        blk = np.where(is_r, (s >= m) & (s <= t), (s > t) & (s <= m - 1))
        blocks.append(blk)
        lvl_masks.append(is_r & ((s & h) == 0) & ((t // (2 * h)) == (s // (2 * h))))
        right.append(np.broadcast_to(is_r, (c, width)))
    mst = np.tile(np.concatenate(blocks, axis=0).astype(np.float32), (1, 3))
    lm = np.stack([np.eye(c, dtype=bool)] + lvl_masks).astype(np.float32)
    rm = np.stack(right).astype(np.float32)
    return jnp.asarray(mst, BF16), jnp.asarray(lm, F32), jnp.asarray(rm, F32)


def _hgrn2_kernel(q_ref, f_ref, i_ref, g_ref, lb_ref, ng_ref, mst_ref, lm_ref, rm_ref, o_ref, st_ref):
    c = CHUNK
    n_heads = q_ref.shape[2] // HG_HEAD
    n_chunks = q_ref.shape[1] // c

    @pl.when(pl.program_id(1) == 0)
    def _():
        st_ref[...] = jnp.zeros_like(st_ref)

    mst = mst_ref[...]
    lb = lb_ref[...]
    ng = ng_ref[...]
    heads = [slice(hd * HG_HEAD, (hd + 1) * HG_HEAD) for hd in range(n_heads)]

    def chunk_body(ci, carry):
        r0 = pl.multiple_of(ci * c, c)
        rows = pl.ds(r0, c)
        q = _silu(q_ref[0, rows, :])
        f = lb + (1.0 - lb) * _sigmoid(f_ref[0, rows, :])
        lf = jnp.log(f)
        k = 1.0 - f
        vb = i_ref[0, rows, :].astype(BF16)
        ex = jnp.exp(_dot_exact_lhs(mst, lf))
        ex_cum = ex[0:c]
        qd = (q * ex_cum).astype(BF16)
        kr = (k * ex[c:2 * c]).astype(BF16)
        qb = q.astype(BF16)
        kb = k.astype(BF16)
        sts = [st_ref[hd] for hd in range(n_heads)]
        o = [_dot(qd[:, ls], st.astype(BF16), NT) for ls, st in zip(heads, sts)]
        sc = [lm_ref[0] * _dot(qb[:, ls], kb[:, ls], NT) for ls in heads]
        dqk = q - k
        for li in range(len(_HG_LEVELS)):
            g_l = ((k + rm_ref[li] * dqk) * ex[(2 + li) * c:(3 + li) * c]).astype(BF16)
            sc = [s_h + lm_ref[li + 1] * _dot(g_l[:, ls], g_l[:, ls], NT) for s_h, ls in zip(sc, heads)]
        o = [o_h + _dot(s_h.astype(BF16), vb[:, ls]) for o_h, s_h, ls in zip(o, sc, heads)]
        for hd, ls in enumerate(heads):
            st_ref[hd] = sts[hd] * ex_cum[c - 1:c, ls] + _dot(vb[:, ls], kr[:, ls], TN)
        on = [o_h * lax.rsqrt(jnp.mean(o_h * o_h, axis=-1, keepdims=True) + NORM_EPS) for o_h in o]
        o_full = jnp.concatenate(on, axis=1) * ng
        o_ref[0, rows, :] = (o_full * _silu(g_ref[0, rows, :])).astype(o_ref.dtype)
        return carry

    lax.fori_loop(0, n_chunks, chunk_body, 0)


def _hgrn2(hg, lb, norm_g, ts):
    bsz, s, n4 = hg.shape
    w = n4 // 4
    mst, lm, rm = _hgrn2_consts(w)
    n_heads = w // HG_HEAD
    const2 = lambda b, i: (0, 0)
    const3 = lambda b, i: (0, 0, 0)
    return pl.pallas_call(
        _hgrn2_kernel,
        out_shape=jax.ShapeDtypeStruct((bsz, s, w), BF16),
        grid=(bsz, s // ts),
        in_specs=[pl.BlockSpec((1, ts, w), lambda b, i: (b, i, 0)),
                  pl.BlockSpec((1, ts, w), lambda b, i: (b, i, 1)),
                  pl.BlockSpec((1, ts, w), lambda b, i: (b, i, 2)),
                  pl.BlockSpec((1, ts, w), lambda b, i: (b, i, 3)),
                  pl.BlockSpec((1, w), const2),
                  pl.BlockSpec((1, w), const2),
                  pl.BlockSpec(mst.shape, const2),
                  pl.BlockSpec(lm.shape, const3),
                  pl.BlockSpec(rm.shape, const3)],
        out_specs=pl.BlockSpec((1, ts, w), lambda b, i: (b, i, 0)),
        scratch_shapes=[pltpu.VMEM((n_heads, HG_HEAD, HG_HEAD), F32)],
        compiler_params=_cparams(("arbitrary", "arbitrary")),
        name="hgrn2",
    )(hg, hg, hg, hg, lb.reshape(1, w), norm_g.reshape(1, w), mst, lm, rm)


def _rwkv_consts(width):
    c = CHUNK
    t = np.arange(c)[:, None]
    s = np.arange(c)[None, :]
    tri = np.tile((s <= t).astype(np.float32), (1, 3))
    tt = np.arange(2 * c)[:, None]
    ss = np.arange(2 * c)[None, :]
    same = (tt // c) == (ss // c)
    strict = same & ((ss % c) < (tt % c))
    incl = same & ((ss % c) <= (tt % c))
    hsum = (np.arange(width)[:, None] // RW_HEAD) == (np.arange(width)[None, :] // RW_HEAD)
    hsum = np.tile(hsum, (3, 1))
    return (jnp.asarray(tri, BF16), jnp.asarray(strict.astype(np.float32), F32),
            jnp.asarray(incl.astype(np.float32), F32), jnp.asarray(hsum.astype(np.float32), BF16))


def _rwkv7_kernel(p_ref, mu_ref, w0_ref, a0_ref, kk_ref, ka_ref, rk_ref, gnw_ref, gnb_ref,
                  w2_ref, a2_ref, g2_ref, tri_ref, sm_ref, im_ref, hs_ref,
                  o_ref, carry_ref, zt_ref):
    c = CHUNK
    nb = p_ref.shape[0]
    width = o_ref.shape[2]
    n_pairs = width // LANES

    @pl.when(pl.program_id(0) == 0)
    def _():
        carry_ref[...] = jnp.zeros_like(carry_ref)
        zt_ref[...] = jnp.zeros_like(zt_ref)

    hs = hs_ref[...]
    tri = tri_ref[...]
    smask = sm_ref[...] > 0
    imask = im_ref[...] > 0
    lane = lax.broadcasted_iota(jnp.int32, (c, LANES), 1)
    m0 = (lane < RW_HEAD).astype(F32)
    m1 = 1.0 - m0

    def stack(x):
        return jnp.concatenate([x * m0, x * m1], axis=0)

    rs, k2s, vs_, gs = [], [], [], []
    units = []
    for b in range(nb):
        p = p_ref[b]
        row = lax.broadcasted_iota(jnp.int32, p.shape, 0)
        prev = jnp.where(row == 0, carry_ref[b], pltpu.roll(p, 1, 0))
        carry_ref[b] = p[c - 1:c, :]
        xs = p + mu_ref[...] * (prev - p)
        r = xs[:, 0:width]
        k = xs[:, width:2 * width]
        v = xs[:, 2 * width:3 * width]
        slab = xs[:, 3 * width:]
        nz = -(w0_ref[...] + _dot(jnp.tanh(slab).astype(BF16), w2_ref[...]))
        softplus = jnp.maximum(nz, 0.0) + jnp.log(1.0 + jnp.exp(-jnp.abs(nz)))
        ld = -jnp.exp(-softplus - 0.5)
        a = _sigmoid(a0_ref[...] + _dot(slab.astype(BF16), a2_ref[...]))
        gs.append(_dot(_sigmoid(slab).astype(BF16), g2_ref[...]))
        kk0 = k * kk_ref[...]
        kk = kk0 * lax.rsqrt(jnp.maximum(_dot_exact_rhs(kk0 * kk0, hs), 1e-24))
        k2 = k * (1.0 + (a - 1.0) * ka_ref[...])
        a_in = -kk
        b_in = kk * a
        cum = _dot_exact_lhs(tri, ld)
        cum_t = cum[c - 1:c, :]
        e_c = jnp.exp(cum)
        e_nc = jnp.exp(-cum)
        e_rem = jnp.exp(cum_t - cum)
        at_f = a_in * jnp.exp(cum - ld)
        rt_f = r * e_c
        kt_f = k2 * e_nc
        bt_f = b_in * e_nc
        kh_f = k2 * e_rem
        bh_f = b_in * e_rem
        p_t = jnp.exp(cum_t)
        rs.append(r)
        k2s.append(k2)
        vs_.append(v)
        for pi in range(n_pairs):
            ls = slice(pi * LANES, (pi + 1) * LANES)
            units.append(dict(
                b=b, pi=pi,
                at=stack(at_f[:, ls]).astype(BF16), rt=stack(rt_f[:, ls]).astype(BF16),
                kt=stack(kt_f[:, ls]).astype(BF16), bt=stack(bt_f[:, ls]).astype(BF16),
                kh=stack(kh_f[:, ls]).astype(BF16), bh=stack(bh_f[:, ls]).astype(BF16),
                vs=stack(v[:, ls]).astype(BF16), p_t=p_t[:, ls]))

    for u in units:
        lhs = jnp.concatenate([u['at'], u['rt']], axis=0)
        u['g'] = _dot(lhs, jnp.concatenate([u['kt'], u['bt']], axis=0), NT)
    for u in units:
        g = u.pop('g')
        gk, gb = g[:, :2 * c], g[:, 2 * c:]
        a_kk = jnp.concatenate([jnp.where(smask, gk[:2 * c], 0.0), jnp.where(imask, gk[2 * c:], 0.0)], axis=0)
        u['pw'] = jnp.where(smask, gb[:2 * c], 0.0).astype(BF16)
        u['a_rb'] = jnp.where(imask, gb[2 * c:], 0.0).astype(BF16)
        u['a_kk'] = a_kk.astype(BF16)
    for u in units:
        av = _dot(u.pop('a_kk'), u['vs'])
        u['arkv'] = av[2 * c:]
        u['x'] = jnp.concatenate([u['at'].astype(F32), av[:2 * c]], axis=1)
    n_lvl = int(np.log2(c))
    for lvl in range(n_lvl):
        for u in units:
            u['x'] = u['x'] + _dot(u['pw'], u['x'].astype(BF16))
        if lvl + 1 < n_lvl:
            for u in units:
                u['pw'] = _dot(u['pw'], u['pw']).astype(BF16)
    for u in units:
        x = u.pop('x')
        u['zt'] = zt_ref[u['b'], u['pi']]
        u['uy'] = _dot(jnp.concatenate([x[:, :LANES].astype(BF16), u['rt']], axis=0), u['zt'].astype(BF16), NT)
        u['u_loc'] = x[:, LANES:]
    for u in units:
        uy = u.pop('uy')
        u['u'] = (uy[:2 * c] + u.pop('u_loc')).astype(BF16)
        u['y0'] = uy[2 * c:] + u.pop('arkv')
    for u in units:
        u['y'] = u.pop('y0') + _dot(u['a_rb'], u['u'])
        upd = _dot(jnp.concatenate([u['u'], u['vs']], axis=0), jnp.concatenate([u['bh'], u['kh']], axis=0), TN)
        zt_ref[u['b'], u['pi']] = u['zt'] * u['p_t'] + upd

    inv_n = 1.0 / RW_HEAD
    for b in range(nb):
        ys = [u['y'] for u in units if u['b'] == b]
        y = jnp.concatenate([yy[:c] + yy[c:] for yy in ys], axis=1)
        mean = _dot_exact_rhs(y, hs) * inv_n
        d = y - mean
        var = _dot_exact_rhs(d * d, hs) * inv_n
        yn = d * lax.rsqrt(var + RW_GN_EPS) * gnw_ref[...] + gnb_ref[...]
        bonus = _dot_exact_rhs(rs[b] * k2s[b] * rk_ref[...], hs) * vs_[b]
        o_ref[b] = ((yn + bonus) * gs[b]).astype(o_ref.dtype)


def _rwkv7(rw, mu, w0, w2, a0, a2, g2, k_k, k_a, r_k, gn_w, gn_b):
    bsz, s, cols = rw.shape
    width = w0.shape[-1]
    n_pairs = width // LANES
    slab = cols - 3 * width
    dl, al, gl = w2.shape[0], a2.shape[0], g2.shape[0]
    w2f = jnp.zeros((slab, width), F32).at[0:dl].set(w2).astype(BF16)
    a2f = jnp.zeros((slab, width), F32).at[dl:dl + al].set(a2).astype(BF16)
    g2f = jnp.zeros((slab, width), F32).at[dl + al:dl + al + gl].set(g2).astype(BF16)
    mup = jnp.zeros((1, cols), F32).at[0, :mu.shape[-1]].set(mu)
    tri, sm, im, hs = _rwkv_consts(width)
    row = lambda x: x.reshape(1, width)
    const = lambda i: (0, 0)
    vec = pl.BlockSpec((1, width), const)
    return pl.pallas_call(
        _rwkv7_kernel,
        out_shape=jax.ShapeDtypeStruct((bsz, s, width), BF16),
        grid=(s // CHUNK,),
        in_specs=[pl.BlockSpec((bsz, CHUNK, cols), lambda i: (0, i, 0)),
                  pl.BlockSpec((1, cols), const),
                  vec, vec, vec, vec, vec, vec, vec,
                  pl.BlockSpec((slab, width), const),
                  pl.BlockSpec((slab, width), const),
                  pl.BlockSpec((slab, width), const),
                  pl.BlockSpec(tri.shape, const),
                  pl.BlockSpec(sm.shape, const),
                  pl.BlockSpec(im.shape, const),
                  pl.BlockSpec(hs.shape, const)],
        out_specs=pl.BlockSpec((bsz, CHUNK, width), lambda i: (0, i, 0)),
        scratch_shapes=[pltpu.VMEM((bsz, 1, cols), F32),
                        pltpu.VMEM((bsz, n_pairs, LANES, LANES), F32)],
        compiler_params=_cparams(("arbitrary",)),
        name="rwkv7",
    )(rw, mup, row(w0), row(a0), row(k_k), row(k_a), row(r_k), row(gn_w), row(gn_b),
      w2f, a2f, g2f, tri, sm, im, hs)


def _out_proj_kernel(n_groups, n_experts,
                     x_ref, oa_ref, ob_ref, ga_ref, gb_ref, gt1_ref, sc2_ref, sh2_ref, g2_ref,
                     wa_ref, wb_ref, wo_ref, wr_ref, wrl_ref, br_ref, tril_ref,
                     x1_ref, h2_ref, route_ref, cnt_ref, carry_ref):
    tm = x_ref.shape[1]
    first = (pl.program_id(0) == 0) & (pl.program_id(1) == 0)

    @pl.when(first)
    def _():
        carry_ref[...] = jnp.zeros_like(carry_ref)

    pa = _dot(oa_ref[0], wa_ref[...])
    pb = _dot(ob_ref[0], wb_ref[...])
    mixed = ga_ref[0].astype(F32) * pa + gb_ref[0].astype(F32) * pb
    x1 = x_ref[0] + gt1_ref[0] * _dot(mixed.astype(BF16), wo_ref[...])
    x1_ref[0] = x1
    ms = jnp.mean(x1 * x1, axis=-1, keepdims=True)
    h2 = (x1 * lax.rsqrt(ms + NORM_EPS) * g2_ref[...]) * (1.0 + sc2_ref[0]) + sh2_ref[0]
    _rows_to_tiles(h2_ref, h2)

    h2_hi = h2.astype(BF16)
    h2_lo = (h2 - h2_hi.astype(F32)).astype(BF16)
    logits = (_dot(h2_hi, wr_ref[...]) + _dot(h2_lo, wr_ref[...]) + _dot(h2_hi, wrl_ref[...])) + br_ref[...]
    lane = lax.broadcasted_iota(jnp.int32, logits.shape, 1)
    neg = jnp.float32(-jnp.inf)
    big = jnp.int32(1 << 20)
    eg = n_experts // n_groups
    is_g = (lane >= n_experts) & (lane < n_experts + n_groups)
    lg = jnp.where(is_g, logits, neg)
    mg = jnp.max(lg, axis=-1, keepdims=True)
    p_grp = 1.0 / jnp.sum(jnp.where(is_g, jnp.exp(lg - mg), 0.0), axis=-1, keepdims=True)
    gidx = jnp.min(jnp.where(lg == mg, lane, big), axis=-1, keepdims=True) - n_experts
    sel = (lane >= gidx * eg) & (lane < gidx * eg + eg)
    le = jnp.where(sel, logits, neg)
    me = jnp.max(le, axis=-1, keepdims=True)
    pe_un = jnp.where(sel, jnp.exp(le - me), 0.0)
    pe = jnp.where(sel, pe_un / jnp.sum(pe_un, axis=-1, keepdims=True), -1.0)
    v1 = jnp.max(pe, axis=-1, keepdims=True)
    i1 = jnp.min(jnp.where(pe == v1, lane, big), axis=-1, keepdims=True)
    pe2 = jnp.where(lane == i1, -1.0, pe)
    v2 = jnp.max(pe2, axis=-1, keepdims=True)
    i2 = jnp.min(jnp.where(pe2 == v2, lane, big), axis=-1, keepdims=True)
    wsum = v1 + v2
    w1 = p_grp * v1 / wsum
    w2 = p_grp * v2 / wsum

    oh1 = (lane == i1).astype(F32)
    oh2 = (lane == i2).astype(F32)
    both = oh1 + oh2
    before = _dot(tril_ref[...], both.astype(BF16)) + carry_ref[...]
    rank1 = jnp.sum(oh1 * before, axis=-1, keepdims=True)
    rank2 = jnp.sum(oh2 * before, axis=-1, keepdims=True)
    carry_ref[...] = carry_ref[...] + jnp.sum(both, axis=0, keepdims=True)
    cnt_ref[...] = carry_ref[...]

    out = jnp.where(lane == 0, w1, 0.0)
    out = jnp.where(lane == 1, w2, out)
    out = jnp.where(lane == 2, i1.astype(F32), out)
    out = jnp.where(lane == 3, i2.astype(F32), out)
    out = jnp.where(lane == 4, rank1, out)
    out = jnp.where(lane == 5, rank2, out)
    route_ref[0] = out


def _out_proj(x, o_a, o_b, gates, mod3, norm2_g, wa, wb, wo, wr, br, n_groups, n_experts, tm):
    bsz, s, d = x.shape
    wdt = o_a.shape[-1]
    tril = jnp.asarray(np.tril(np.ones((tm, tm), np.float32), -1), BF16)
    wr_hi = wr.astype(BF16)
    wr_lo = (wr - wr_hi.astype(F32)).astype(BF16)
    const = lambda b, i: (0, 0)
    tile = lambda b, i: (b, i, 0)
    kern = functools.partial(_out_proj_kernel, n_groups, n_experts)
    return pl.pallas_call(
        kern,
        out_shape=(jax.ShapeDtypeStruct((bsz, s, d), F32),
                   jax.ShapeDtypeStruct((bsz * s * SUB, LANES), U32),
                   jax.ShapeDtypeStruct((bsz, s, LANES), F32),
                   jax.ShapeDtypeStruct((1, LANES), F32)),
        grid=(bsz, s // tm),
        in_specs=[pl.BlockSpec((1, tm, d), tile),
                  pl.BlockSpec((1, tm, wdt), tile),
                  pl.BlockSpec((1, tm, wdt), tile),
                  pl.BlockSpec((1, tm, d), lambda b, i: (b, i, 0)),
                  pl.BlockSpec((1, tm, d), lambda b, i: (b, i, 1)),
                  pl.BlockSpec((1, 1, d), lambda b, i: (b * 6 + 2, 0, 0)),
                  pl.BlockSpec((1, 1, d), lambda b, i: (b * 6 + 4, 0, 0)),
                  pl.BlockSpec((1, 1, d), lambda b, i: (b * 6 + 3, 0, 0)),
                  pl.BlockSpec((1, d), const),
                  pl.BlockSpec(wa.shape, const),
                  pl.BlockSpec(wb.shape, const),
                  pl.BlockSpec(wo.shape, const),
                  pl.BlockSpec(wr.shape, const),
                  pl.BlockSpec(wr.shape, const),
                  pl.BlockSpec((1, LANES), const),
                  pl.BlockSpec((tm, tm), const)],
        out_specs=(pl.BlockSpec((1, tm, d), tile),
                   pl.BlockSpec((tm * SUB, LANES), lambda b, i: (b * (s // tm) + i, 0)),
                   pl.BlockSpec((1, tm, LANES), tile),
                   pl.BlockSpec((1, LANES), const)),
        scratch_shapes=[pltpu.VMEM((1, LANES), F32)],
        compiler_params=_cparams(("arbitrary", "arbitrary")),
        name="out_proj",
    )(x, o_a, o_b, gates, gates, mod3, mod3, mod3, norm2_g.reshape(1, d), wa, wb, wo, wr_hi, wr_lo, br, tril)


def _moe_dispatch_kernel(tm, n_tok, blk, dest_ref, zstart_ref, zcnt_ref, nused_ref, h_ref, xb_ref, smap_ref,
                         zbuf, stage, sem, zsem):
    i = pl.program_id(0)
    n_steps = pl.num_programs(0)
    n_slots = smap_ref.shape[0]
    n_experts = zcnt_ref.shape[0]

    @pl.when(i == 0)
    def _():
        def init(s_, carry):
            smap_ref[s_] = 0
            return carry
        lax.fori_loop(nused_ref[0] * blk, n_slots, init, 0)
        zbuf[...] = jnp.zeros_like(zbuf)

        def zero_row(e, j):
            dst = pl.multiple_of((zstart_ref[e] + j) * SUB, SUB)
            return pltpu.make_async_copy(zbuf.at[pl.ds(0, SUB), :], xb_ref.at[pl.ds(dst, SUB), :], zsem)

        def zero_block(b):
            dst = pl.multiple_of(b * (blk * SUB), blk * SUB)
            return pltpu.make_async_copy(zbuf, xb_ref.at[pl.ds(dst, blk * SUB), :], zsem)

        for e in range(n_experts):
            def zstart(j, carry, e=e):
                zero_row(e, j).start()
                smap_ref[zstart_ref[e] + j] = TOP_K * n_tok + e * blk + j
                return carry
            lax.fori_loop(0, zcnt_ref[e], zstart, 0)

        def bstart(b, carry):
            zero_block(b).start()
            return carry
        lax.fori_loop(nused_ref[0], n_slots // blk, bstart, 0)
        for e in range(n_experts):
            def zwait(j, carry, e=e):
                zero_row(e, j).wait()
                return carry
            lax.fori_loop(0, zcnt_ref[e], zwait, 0)

        def bwait(b, carry):
            zero_block(b).wait()
            return carry
        lax.fori_loop(nused_ref[0], n_slots // blk, bwait, 0)

    slot = i % 2
    stage[slot] = h_ref[...]
    for r in range(tm):
        tok = i * tm + r
        for k in range(TOP_K):
            d = dest_ref[tok * TOP_K + k]
            dst = pl.multiple_of(d * SUB, SUB)
            pltpu.make_async_copy(stage.at[slot, pl.ds(r * SUB, SUB), :], xb_ref.at[pl.ds(dst, SUB), :],
                                  sem.at[slot]).start(priority=k)
            smap_ref[d] = k * n_tok + tok

    def wait_step(s_):
        for _ in range(TOP_K):
            pltpu.make_async_copy(stage.at[s_], xb_ref.at[pl.ds(0, tm * SUB), :], sem.at[s_]).wait()

    @pl.when(i > 0)
    def _():
        wait_step(1 - slot)

    @pl.when(i == n_steps - 1)
    def _():
        wait_step(slot)


def _moe_dispatch(h2, dest, zstart, zcnt, n_used, n_slots, blk, tm):
    n_tok = h2.shape[0] // SUB
    grid_spec = pltpu.PrefetchScalarGridSpec(
        num_scalar_prefetch=4,
        grid=(n_tok // tm,),
        in_specs=[pl.BlockSpec((tm * SUB, LANES), lambda i, *_: (i, 0))],
        out_specs=(pl.BlockSpec(memory_space=pl.ANY),
                   pl.BlockSpec(memory_space=pltpu.SMEM)),
        scratch_shapes=[pltpu.VMEM((blk * SUB, LANES), U32),
                        pltpu.VMEM((2, tm * SUB, LANES), U32),
                        pltpu.SemaphoreType.DMA((2,)),
                        pltpu.SemaphoreType.DMA],
    )
    return pl.pallas_call(
        functools.partial(_moe_dispatch_kernel, tm, n_tok, blk),
        out_shape=(jax.ShapeDtypeStruct((n_slots * SUB, LANES), U32),
                   jax.ShapeDtypeStruct((n_slots,), jnp.int32)),
        grid_spec=grid_spec,
        compiler_params=_cparams(("arbitrary",)),
        name="moe_dispatch",
    )(dest, zstart, zcnt, n_used, h2)


def _moe_expert_kernel(n_tok, blk_e_ref, nused_ref, smap_ref, x_ref, wg_ref, wu_ref, wd_ref, y_ref,
                       ystage, wgb, wub, wdb, sem):
    i = pl.program_id(0)
    n_used = nused_ref[0]
    blk = x_ref.shape[0] // SUB
    first_real_blocks = TOP_K * n_tok // blk
    first_spare = y_ref.shape[0] // SUB - blk

    def issue(b, slot, rows):
        for r in rows:
            t = jnp.where(b >= 0, smap_ref[jnp.maximum(b, 0) * blk + r], first_spare + r)
            dst = pl.multiple_of(t * SUB, SUB)
            pltpu.make_async_copy(ystage.at[slot, pl.ds(r * SUB, SUB), :], y_ref.at[pl.ds(dst, SUB), :],
                                  sem.at[slot]).start(priority=r % 2)

    def wait_block(slot):
        pltpu.make_async_copy(ystage.at[slot], y_ref.at[pl.ds(0, blk * SUB), :], sem.at[slot]).wait()

    @pl.when(i == 0)
    def _():
        ystage[...] = jnp.zeros_like(ystage)
        n_spare_blocks = y_ref.shape[0] // (blk * SUB) - first_real_blocks

        def spare_copy(c):
            dst = (first_real_blocks + c) * blk * SUB
            return pltpu.make_async_copy(ystage.at[0], y_ref.at[pl.ds(dst, blk * SUB), :], sem.at[0])

        for c in range(n_spare_blocks):
            spare_copy(c).start()
        for c in range(n_spare_blocks):
            spare_copy(c).wait()

    new_expert = (i == 0) | (blk_e_ref[i] != blk_e_ref[jnp.maximum(i - 1, 0)])

    @pl.when((i < n_used) & new_expert)
    def _():
        wgb[...] = wg_ref[0].astype(BF16)
        wub[...] = wu_ref[0].astype(BF16)
        wdb[...] = wd_ref[0].astype(BF16)

    @pl.when(i < n_used)
    def _():
        slot = i % 2
        pslot = 1 - slot

        @pl.when(i > 0)
        def _():
            wait_block(slot)

        q = blk // 4
        xb = _tiles_to_rows(x_ref, blk).astype(BF16)
        issue(i - 1, pslot, range(0, q))
        hg = _dot(xb, wgb[...])
        issue(i - 1, pslot, range(q, 2 * q))
        hu = _dot(xb, wub[...])
        issue(i - 1, pslot, range(2 * q, 3 * q))
        hid = (_silu(hg) * hu).astype(BF16)
        y = _dot(hid, wdb[...])
        issue(i - 1, pslot, range(3 * q, blk))
        _rows_to_tiles(ystage.at[slot], y)

        @pl.when(i == n_used - 1)
        def _():
            issue(i, slot, range(blk))
            wait_block(pslot)
            wait_block(slot)


def _moe_experts(xb, smap, blk_e, n_used, w_gate, w_up, w_down, blk, n_tok):
    d = w_gate.shape[1]
    assert d == 2 * SUB * LANES and xb.shape[1] == LANES
    n_slots = smap.shape[0]
    nb = n_slots // blk
    f = w_gate.shape[-1]
    n_experts = w_gate.shape[0]
    assert (TOP_K * n_tok) % blk == 0
    n_tiles = TOP_K * n_tok + (n_experts + 1) * blk
    grid_spec = pltpu.PrefetchScalarGridSpec(
        num_scalar_prefetch=3,
        grid=(nb,),
        in_specs=[pl.BlockSpec((blk * SUB, LANES), lambda i, be, nu, sm: (jnp.minimum(i, nu[0] - 1), 0)),
                  pl.BlockSpec((1, d, f), lambda i, be, nu, sm: (be[i], 0, 0)),
                  pl.BlockSpec((1, d, f), lambda i, be, nu, sm: (be[i], 0, 0)),
                  pl.BlockSpec((1, f, d), lambda i, be, nu, sm: (be[i], 0, 0))],
        out_specs=pl.BlockSpec(memory_space=pl.ANY),
        scratch_shapes=[pltpu.VMEM((2, blk * SUB, LANES), U32),
                        pltpu.VMEM((d, f), BF16),
                        pltpu.VMEM((d, f), BF16),
                        pltpu.VMEM((f, d), BF16),
                        pltpu.SemaphoreType.DMA((2,))],
    )
    return pl.pallas_call(
        functools.partial(_moe_expert_kernel, n_tok),
        out_shape=jax.ShapeDtypeStruct((n_tiles * SUB, LANES), U32),
        grid_spec=grid_spec,
        compiler_params=_cparams(("arbitrary",)),
        name="moe_experts",
    )(blk_e, n_used, smap, xb, w_gate, w_up, w_down)


def _moe_combine_kernel(x1_ref, route_ref, gt2_ref, fg_ref, y1_ref, y2_ref, o_ref):
    tm = x1_ref.shape[0]
    route = route_ref[...]
    moe = route[:, 0:1] * _tiles_to_rows(y1_ref, tm) + route[:, 1:2] * _tiles_to_rows(y2_ref, tm)
    xo = x1_ref[...] + gt2_ref[0] * moe
    ms = jnp.mean(xo * xo, axis=-1, keepdims=True)
    o_ref[...] = xo * lax.rsqrt(ms + NORM_EPS) * fg_ref[...]


def _moe_combine(x1, route, yt, mod3, final_g, s, tm):
    n, d = x1.shape
    tiles_per_batch = s // tm
    n_steps = n // tm
    return pl.pallas_call(
        _moe_combine_kernel,
        out_shape=jax.ShapeDtypeStruct((n, d), F32),
        grid=(n_steps,),
        in_specs=[pl.BlockSpec((tm, d), lambda i: (i, 0)),
                  pl.BlockSpec((tm, LANES), lambda i: (i, 0)),
                  pl.BlockSpec((1, 1, d), lambda i: ((i // tiles_per_batch) * 6 + 5, 0, 0)),
                  pl.BlockSpec((1, d), lambda i: (0, 0)),
                  pl.BlockSpec((tm * SUB, LANES), lambda i: (i, 0)),
                  pl.BlockSpec((tm * SUB, LANES), lambda i: (n_steps + i, 0))],
        out_specs=pl.BlockSpec((tm, d), lambda i: (i, 0)),
        compiler_params=_cparams(("arbitrary",)),
        name="moe_combine",
    )(x1, route, mod3, final_g.reshape(1, d), yt, yt)


def _pick(n, candidates):
    for t in candidates:
        if n % t == 0:
            return t
    raise ValueError(f"no tile in {candidates} divides {n}")


def kernel(x, c, ada_w, ada_b, norm1_g, w_in, hg_lb, hg_norm_g, rw_mu, rw_w0, rw_w2, rw_a0, rw_a2, rw_g2, rw_kk, rw_ka, rw_rk, rw_gn_w, rw_gn_b, w_proj_a, w_proj_b, w_out, norm2_g, router_g_w, router_g_b, router_e_w, router_e_b, exp_w_gate, exp_w_up, exp_w_down, final_g):
    bsz, s, d = x.shape
    depth = ada_w.shape[0]
    hg_f = hg_lb.shape[-1]
    hg_w = hg_norm_g.shape[-1]
    rw_w = rw_w0.shape[-1]
    rw_cols = rw_mu.shape[-1]
    n_groups = router_g_w.shape[-1]
    n_experts = router_e_w.shape[-1]
    assert hg_f == hg_w and s % CHUNK == 0 and n_experts + n_groups <= LANES and d == 2 * SUB * LANES

    lb_all = jnp.cumsum(jax.nn.softmax(hg_lb.astype(F32), axis=0), axis=0)
    n = bsz * s
    blk = 256
    n_blocks = (n * TOP_K + n_experts * blk) // blk
    for l in range(depth):
        mod = _ada_mod(c, ada_w[l], ada_b[l])
        mod3 = mod.reshape(bsz * 6, 1, d)

        hg_cols = 2 * hg_f + 2 * hg_w
        rw_pad = -(-rw_cols // 256) * 256
        wl = w_in[l]
        w_hg = wl[:, :hg_cols].astype(BF16)
        w_rw = jnp.zeros((d, rw_pad), BF16).at[:, :rw_cols].set(wl[:, hg_cols:hg_cols + rw_cols].astype(BF16))
        w_gt = wl[:, hg_cols + rw_cols:].astype(BF16)
        hg, rw, gates = _in_proj(x, mod3, norm1_g[l], w_hg, w_rw, w_gt, _pick(s, (512, 256, 128, 64)))

        o_a = _hgrn2(hg, lb_all[l], hg_norm_g[l], _pick(s, (512, 256, 128, 64)))
        o_b = _rwkv7(rw, rw_mu[l], rw_w0[l], rw_w2[l], rw_a0[l], rw_a2[l], rw_g2[l],
                     rw_kk[l], rw_ka[l], rw_rk[l].reshape(-1), rw_gn_w[l], rw_gn_b[l])

        wr = jnp.zeros((d, LANES), F32).at[:, :n_experts].set(router_e_w[l])
        wr = wr.at[:, n_experts:n_experts + n_groups].set(router_g_w[l])
        br = jnp.zeros((1, LANES), F32).at[0, :n_experts].set(router_e_b[l])
        br = br.at[0, n_experts:n_experts + n_groups].set(router_g_b[l])
        x1, h2, route, counts = _out_proj(
            x, o_a, o_b, gates, mod3, norm2_g[l],
            w_proj_a[l].astype(BF16), w_proj_b[l].astype(BF16), w_out[l].astype(BF16),
            wr, br, n_groups, n_experts, _pick(s, (512, 256, 128, 64)))

        route2 = route.reshape(n, LANES)
        eid = route2[:, 2:4].astype(jnp.int32)
        rank = route2[:, 4:6].astype(jnp.int32)
        cnt = counts[0, :n_experts].astype(jnp.int32)
        padded = (cnt + blk - 1) // blk * blk
        pad_end = jnp.cumsum(padded)
        pad_start = pad_end - padded
        dest = pad_start[eid] + rank
        blk_start = jnp.arange(n_blocks, dtype=jnp.int32) * blk
        blk_e = jnp.minimum(jnp.sum((pad_end[None, :] <= blk_start[:, None]).astype(jnp.int32), axis=1),
                            n_experts - 1)
        n_used = (pad_end[-1:] // blk).astype(jnp.int32)
        tm = _pick(s, (256, 128, 64))
        xb, smap = _moe_dispatch(h2, dest.reshape(-1), pad_start + cnt, padded - cnt, n_used,
                                 n_blocks * blk, blk, tm)
        yt = _moe_experts(xb, smap, blk_e, n_used, exp_w_gate[l], exp_w_up[l], exp_w_down[l], blk, n)
        last = l == depth - 1
        assert last, "the final RMSNorm is fused into the last layer's combine"
        out = _moe_combine(x1.reshape(n, d), route2, yt, mod3, final_g, s, tm)
        x = out.reshape(bsz, s, d)
    return x
```

```python
import functools

import numpy as np
import jax
import jax.numpy as jnp
from jax import lax
from jax.experimental import pallas as pl
from jax.experimental.pallas import tpu as pltpu

F32 = jnp.float32
BF16 = jnp.bfloat16
HIGHEST = lax.Precision.HIGHEST

NORM_EPS = 1e-6
HG_HEAD = 128
RW_HEAD = 64
RW_GN_EPS = 64e-5
TOP_K = 2
CHUNK = 64
LANES = 128
SUB = 4
U32 = jnp.uint32
MXU_K = 256
VMEM_LIMIT = 56 * 1024 * 1024

NT = (((1,), (1,)), ((), ()))
TN = (((0,), (0,)), ((), ()))


def _dot(a, b, dims=None, precision=None):
    if dims is None:
        return jnp.dot(a, b, preferred_element_type=F32, precision=precision)
    return lax.dot_general(a, b, dims, preferred_element_type=F32, precision=precision)


def _split3(x):
    hi = x.astype(BF16)
    r1 = x - hi.astype(F32)
    mid = r1.astype(BF16)
    lo = (r1 - mid.astype(F32)).astype(BF16)
    return hi, mid, lo


def _dot_exact_lhs(m3_bf16, x):
    return _dot(m3_bf16, jnp.concatenate(_split3(x), axis=0))


def _head_sums(x, m2_bf16):
    outs = []
    for g0 in range(0, x.shape[1], MXU_K):
        xg = x[:, g0:g0 + MXU_K]
        hi = xg.astype(BF16)
        lo = (xg - hi.astype(F32)).astype(BF16)
        outs.append(_dot(jnp.concatenate([hi, lo], axis=1), m2_bf16))
    return jnp.concatenate(outs, axis=1)


def _sigmoid(x):
    return 1.0 / (1.0 + jnp.exp(-x))


def _silu(x):
    return x * _sigmoid(x)


def _rows_to_tiles(ref, val):
    m, half = val.shape[0], val.shape[1] // 2
    hi = lax.bitcast_convert_type(val[:, :half].astype(BF16).astype(F32), U32)
    lo = lax.bitcast_convert_type(val[:, half:].astype(BF16).astype(F32), U32)
    w = (hi & jnp.uint32(0xFFFF0000)) | (lo >> 16)
    for j in range(SUB):
        ref[pl.ds(j, m, stride=SUB), :] = w[:, j * LANES:(j + 1) * LANES]


def _tiles_to_rows(ref, m, base=0):
    w = jnp.concatenate([ref[pl.ds(base * SUB + j, m, stride=SUB), :] for j in range(SUB)], axis=1)
    hi = lax.bitcast_convert_type(w & jnp.uint32(0xFFFF0000), F32)
    lo = lax.bitcast_convert_type(w << 16, F32)
    return jnp.concatenate([hi, lo], axis=1)


def _cparams(sem):
    return pltpu.CompilerParams(dimension_semantics=sem, vmem_limit_bytes=VMEM_LIMIT)


def _ada_kernel(c_ref, w_ref, b_ref, o_ref):
    c = c_ref[...]
    o_ref[...] = _dot(_silu(c), w_ref[...], precision=HIGHEST) + b_ref[...]


def _ada_mod(c, w, b):
    bsz, d = c.shape
    n = w.shape[1]
    rows = 8
    cp = jnp.zeros((rows, d), F32).at[:bsz].set(c)
    tn = 1536
    out = pl.pallas_call(
        _ada_kernel,
        out_shape=jax.ShapeDtypeStruct((rows, n), F32),
        grid=(n // tn,),
        in_specs=[pl.BlockSpec((rows, d), lambda j: (0, 0)),
                  pl.BlockSpec((d, tn), lambda j: (0, j)),
                  pl.BlockSpec((1, tn), lambda j: (0, j))],
        out_specs=pl.BlockSpec((rows, tn), lambda j: (0, j)),
        compiler_params=_cparams(("arbitrary",)),
        name="ada_mod",
    )(cp, w, b.reshape(1, n))
    return out[:bsz]


def _in_proj_kernel(x_ref, sh_ref, sc_ref, g_ref, whg_ref, wrw_ref, wgt_ref, hg_ref, rw_ref, gt_ref):
    x = x_ref[0]
    ms = jnp.mean(x * x, axis=-1, keepdims=True)
    h = (x * lax.rsqrt(ms + NORM_EPS) * g_ref[...]) * (1.0 + sc_ref[0]) + sh_ref[0]
    hb = h.astype(BF16)
    step = 512
    for n0 in range(0, whg_ref.shape[1], step):
        hg_ref[0, :, n0:n0 + step] = _dot(hb, whg_ref[:, n0:n0 + step])
    for n0 in range(0, wrw_ref.shape[1], 256):
        rw_ref[0, :, n0:n0 + 256] = _dot(hb, wrw_ref[:, n0:n0 + 256])
    for n0 in range(0, wgt_ref.shape[1], step):
        gt_ref[0, :, n0:n0 + step] = _sigmoid(_dot(hb, wgt_ref[:, n0:n0 + step])).astype(BF16)


def _in_proj(x, mod3, norm_g, w_hg, w_rw, w_gt, tm):
    bsz, s, d = x.shape
    n_hg, n_rw, n_gt = w_hg.shape[1], w_rw.shape[1], w_gt.shape[1]
    const = lambda b, i: (0, 0)
    return pl.pallas_call(
        _in_proj_kernel,
        out_shape=(jax.ShapeDtypeStruct((bsz, s, n_hg), F32),
                   jax.ShapeDtypeStruct((bsz, s, n_rw), F32),
                   jax.ShapeDtypeStruct((bsz, s, n_gt), BF16)),
        grid=(bsz, s // tm),
        in_specs=[pl.BlockSpec((1, tm, d), lambda b, i: (b, i, 0)),
                  pl.BlockSpec((1, 1, d), lambda b, i: (b * 6 + 0, 0, 0)),
                  pl.BlockSpec((1, 1, d), lambda b, i: (b * 6 + 1, 0, 0)),
                  pl.BlockSpec((1, d), const),
                  pl.BlockSpec((d, n_hg), const),
                  pl.BlockSpec((d, n_rw), const),
                  pl.BlockSpec((d, n_gt), const)],
        out_specs=(pl.BlockSpec((1, tm, n_hg), lambda b, i: (b, i, 0)),
                   pl.BlockSpec((1, tm, n_rw), lambda b, i: (b, i, 0)),
                   pl.BlockSpec((1, tm, n_gt), lambda b, i: (b, i, 0))),
        compiler_params=_cparams(("arbitrary", "arbitrary")),
        name="in_proj",
    )(x, mod3, mod3, norm_g.reshape(1, d), w_hg, w_rw, w_gt)


_HG_LEVELS = (32, 16, 8, 4, 2, 1)


def _hgrn2_consts(width):
    c = CHUNK
    t = np.arange(c)[:, None]
    s = np.arange(c)[None, :]
    blocks = [(s <= t), (s > t)]
    lvl_masks = []
    right = []
    for h in _HG_LEVELS:
        m = (t // (2 * h)) * 2 * h + h
        is_r = (t & h) != 0
        blk = np.where(is_r, (s >= m) & (s <= t), (s > t) & (s <= m - 1))
        blocks.append(blk)
        lvl_masks.append(is_r & ((s & h) == 0) & ((t // (2 * h)) == (s // (2 * h))))
        right.append(np.broadcast_to(is_r, (c, width)))
    mst = np.tile(np.concatenate(blocks, axis=0).astype(np.float32), (1, 3))
    lm = np.stack([np.eye(c, dtype=bool)] + lvl_masks).astype(np.float32)
    rm = np.stack(right).astype(np.float32)
    return jnp.asarray(mst, BF16), jnp.asarray(lm, F32), jnp.asarray(rm, F32)


def _hgrn2_kernel(q_ref, f_ref, i_ref, g_ref, lb_ref, ng_ref, mst_ref, lm_ref, rm_ref, o_ref, st_ref):
    c = CHUNK
    n_heads = q_ref.shape[2] // HG_HEAD
    n_chunks = q_ref.shape[1] // c

    @pl.when(pl.program_id(1) == 0)
    def _():
        st_ref[...] = jnp.zeros_like(st_ref)

    mst = mst_ref[...]
    lb = lb_ref[...]
    ng = ng_ref[...]
    heads = [slice(hd * HG_HEAD, (hd + 1) * HG_HEAD) for hd in range(n_heads)]

    def chunk_body(ci, carry):
        r0 = pl.multiple_of(ci * c, c)
        rows = pl.ds(r0, c)
        q = _silu(q_ref[0, rows, :])
        f = lb + (1.0 - lb) * _sigmoid(f_ref[0, rows, :])
        lf = jnp.log(f)
        k = 1.0 - f
        vb = i_ref[0, rows, :].astype(BF16)
        ex = jnp.exp(_dot_exact_lhs(mst, lf))
        ex_cum = ex[0:c]
        qd = (q * ex_cum).astype(BF16)
        kr = (k * ex[c:2 * c]).astype(BF16)
        qb = q.astype(BF16)
        kb = k.astype(BF16)
        sts = [st_ref[hd] for hd in range(n_heads)]
        o = [_dot(qd[:, ls], st.astype(BF16), NT) for ls, st in zip(heads, sts)]
        sc = [lm_ref[0] * _dot(qb[:, ls], kb[:, ls], NT) for ls in heads]
        dqk = q - k
        for li in range(len(_HG_LEVELS)):
            g_l = ((k + rm_ref[li] * dqk) * ex[(2 + li) * c:(3 + li) * c]).astype(BF16)
            sc = [s_h + lm_ref[li + 1] * _dot(g_l[:, ls], g_l[:, ls], NT) for s_h, ls in zip(sc, heads)]
        o = [o_h + _dot(s_h.astype(BF16), vb[:, ls]) for o_h, s_h, ls in zip(o, sc, heads)]
        for hd, ls in enumerate(heads):
            st_ref[hd] = sts[hd] * ex_cum[c - 1:c, ls] + _dot(vb[:, ls], kr[:, ls], TN)
        on = [o_h * lax.rsqrt(jnp.mean(o_h * o_h, axis=-1, keepdims=True) + NORM_EPS) for o_h in o]
        o_full = jnp.concatenate(on, axis=1) * ng
        o_ref[0, rows, :] = (o_full * _silu(g_ref[0, rows, :])).astype(o_ref.dtype)
        return carry

    lax.fori_loop(0, n_chunks, chunk_body, 0)


def _hgrn2(hg, lb, norm_g, ts):
    bsz, s, n4 = hg.shape
    w = n4 // 4
    mst, lm, rm = _hgrn2_consts(w)
    n_heads = w // HG_HEAD
    const2 = lambda b, i: (0, 0)
    const3 = lambda b, i: (0, 0, 0)
    return pl.pallas_call(
        _hgrn2_kernel,
        out_shape=jax.ShapeDtypeStruct((bsz, s, w), BF16),
        grid=(bsz, s // ts),
        in_specs=[pl.BlockSpec((1, ts, w), lambda b, i: (b, i, 0)),
                  pl.BlockSpec((1, ts, w), lambda b, i: (b, i, 1)),
                  pl.BlockSpec((1, ts, w), lambda b, i: (b, i, 2)),
                  pl.BlockSpec((1, ts, w), lambda b, i: (b, i, 3)),
                  pl.BlockSpec((1, w), const2),
                  pl.BlockSpec((1, w), const2),
                  pl.BlockSpec(mst.shape, const2),
                  pl.BlockSpec(lm.shape, const3),
                  pl.BlockSpec(rm.shape, const3)],
        out_specs=pl.BlockSpec((1, ts, w), lambda b, i: (b, i, 0)),
        scratch_shapes=[pltpu.VMEM((n_heads, HG_HEAD, HG_HEAD), F32)],
        compiler_params=_cparams(("arbitrary", "arbitrary")),
        name="hgrn2",
    )(hg, hg, hg, hg, lb.reshape(1, w), norm_g.reshape(1, w), mst, lm, rm)


def _rwkv_consts(width):
    c = CHUNK
    t = np.arange(c)[:, None]
    s = np.arange(c)[None, :]
    tri = np.tile((s <= t).astype(np.float32), (1, 3))
    tt = np.arange(2 * c)[:, None]
    ss = np.arange(2 * c)[None, :]
    same = (tt // c) == (ss // c)
    strict = same & ((ss % c) < (tt % c))
    incl = same & ((ss % c) <= (tt % c))
    hsum = (np.arange(MXU_K)[:, None] // RW_HEAD) == (np.arange(MXU_K)[None, :] // RW_HEAD)
    hsum = np.tile(hsum, (2, 1))
    return (jnp.asarray(tri, BF16), jnp.asarray(strict.astype(np.float32), F32),
            jnp.asarray(incl.astype(np.float32), F32), jnp.asarray(hsum.astype(np.float32), BF16))


def _rwkv7_kernel(p_ref, mu_ref, w0_ref, a0_ref, kk_ref, ka_ref, rk_ref, gnw_ref, gnb_ref,
                  w2_ref, a2_ref, g2_ref, tri_ref, sm_ref, im_ref, hs_ref,
                  o_ref, carry_ref, zt_ref):
    c = CHUNK
    nb = p_ref.shape[0]
    width = o_ref.shape[2]
    n_pairs = width // LANES

    @pl.when(pl.program_id(0) == 0)
    def _():
        carry_ref[...] = jnp.zeros_like(carry_ref)
        zt_ref[...] = jnp.zeros_like(zt_ref)

    hs = hs_ref[...]
    tri = tri_ref[...]
    smask = sm_ref[...] > 0
    imask = im_ref[...] > 0
    lane = lax.broadcasted_iota(jnp.int32, (c, LANES), 1)
    m0 = (lane < RW_HEAD).astype(F32)
    m1 = 1.0 - m0

    def stack(x):
        return jnp.concatenate([x * m0, x * m1], axis=0)

    xs_rows = []
    for b in range(nb):
        p = p_ref[b]
        row = lax.broadcasted_iota(jnp.int32, p.shape, 0)
        prev = jnp.where(row == 0, carry_ref[b], pltpu.roll(p, 1, 0))
        carry_ref[b] = p[c - 1:c, :]
        xs_rows.append(p + mu_ref[...] * (prev - p))
    xs = jnp.concatenate(xs_rows, axis=0)
    r_all = xs[:, 0:width]
    k_all = xs[:, width:2 * width]
    v_all = xs[:, 2 * width:3 * width]
    slab = xs[:, 3 * width:]
    nz = -(w0_ref[...] + _dot(jnp.tanh(slab).astype(BF16), w2_ref[...]))
    softplus = jnp.maximum(nz, 0.0) + jnp.log(1.0 + jnp.exp(-jnp.abs(nz)))
    ld_all = -jnp.exp(-softplus - 0.5)
    a_all = _sigmoid(a0_ref[...] + _dot(slab.astype(BF16), a2_ref[...]))
    g_all = _dot(_sigmoid(slab).astype(BF16), g2_ref[...])
    kk0 = k_all * kk_ref[...]
    kk_all = kk0 * lax.rsqrt(jnp.maximum(_head_sums(kk0 * kk0, hs), 1e-24))
    k2_all = k_all * (1.0 + (a_all - 1.0) * ka_ref[...])

    units = []
    for b in range(nb):
        rb = slice(b * c, (b + 1) * c)
        r, k2, v, ld = r_all[rb], k2_all[rb], v_all[rb], ld_all[rb]
        a_in = -kk_all[rb]
        b_in = kk_all[rb] * a_all[rb]
        cum = _dot_exact_lhs(tri, ld)
        cum_t = cum[c - 1:c, :]
        e_c = jnp.exp(cum)
        e_nc = jnp.exp(-cum)
        e_rem = jnp.exp(cum_t - cum)
        at_f = a_in * jnp.exp(cum - ld)
        rt_f = r * e_c
        kt_f = k2 * e_nc
        bt_f = b_in * e_nc
        kh_f = k2 * e_rem
        bh_f = b_in * e_rem
        p_t = jnp.exp(cum_t)
        for pi in range(n_pairs):
            ls = slice(pi * LANES, (pi + 1) * LANES)
            units.append(dict(
                b=b, pi=pi,
                at=stack(at_f[:, ls]).astype(BF16), rt=stack(rt_f[:, ls]).astype(BF16),
                kt=stack(kt_f[:, ls]).astype(BF16), bt=stack(bt_f[:, ls]).astype(BF16),
                kh=stack(kh_f[:, ls]).astype(BF16), bh=stack(bh_f[:, ls]).astype(BF16),
                vs=stack(v[:, ls]).astype(BF16), p_t=p_t[:, ls]))

    for u in units:
        lhs = jnp.concatenate([u['at'], u['rt']], axis=0)
        u['g'] = _dot(lhs, jnp.concatenate([u['kt'], u['bt']], axis=0), NT)
    for u in units:
        g = u.pop('g')
        u['a_ak'] = jnp.where(smask, g[:2 * c, :2 * c], 0.0).astype(BF16)
        u['pw'] = jnp.where(smask, g[:2 * c, 2 * c:], 0.0).astype(BF16)
        u['a_r'] = jnp.where(jnp.concatenate([imask, imask], axis=1), g[2 * c:], 0.0).astype(BF16)
    for u in units:
        akv = _dot(u.pop('a_ak'), u['vs'])
        u['x'] = jnp.concatenate([u['at'].astype(F32), akv], axis=1)
    n_lvl = int(np.log2(c))
    for lvl in range(n_lvl):
        for u in units:
            u['x'] = u['x'] + _dot(u['pw'], u['x'].astype(BF16))
        if lvl + 1 < n_lvl:
            for u in units:
                u['pw'] = _dot(u['pw'], u['pw']).astype(BF16)
    for u in units:
        x = u.pop('x')
        u['zt'] = zt_ref[u['b'], u['pi']]
        u['uy'] = _dot(jnp.concatenate([x[:, :LANES].astype(BF16), u['rt']], axis=0), u['zt'].astype(BF16), NT)
        u['u_loc'] = x[:, LANES:]
    for u in units:
        uy = u.pop('uy')
        u['u'] = (uy[:2 * c] + u.pop('u_loc')).astype(BF16)
        u['y0'] = uy[2 * c:]
    for u in units:
        vu = jnp.concatenate([u['vs'], u['u']], axis=0)
        u['y'] = u.pop('y0') + _dot(u['a_r'], vu)
        upd = _dot(vu, jnp.concatenate([u['kh'], u['bh']], axis=0), TN)
        zt_ref[u['b'], u['pi']] = u['zt'] * u['p_t'] + upd

    inv_n = 1.0 / RW_HEAD
    y = jnp.concatenate(
        [jnp.concatenate([u['y'][:c] + u['y'][c:] for u in units if u['b'] == b], axis=1) for b in range(nb)],
        axis=0)
    mean = _head_sums(y, hs) * inv_n
    d = y - mean
    var = _head_sums(d * d, hs) * inv_n
    yn = d * lax.rsqrt(var + RW_GN_EPS) * gnw_ref[...] + gnb_ref[...]
    bonus = _head_sums(r_all * k2_all * rk_ref[...], hs) * v_all
    out = ((yn + bonus) * g_all).astype(o_ref.dtype)
    for b in range(nb):
        o_ref[b] = out[b * c:(b + 1) * c]


def _rwkv7(rw, mu, w0, w2, a0, a2, g2, k_k, k_a, r_k, gn_w, gn_b):
    bsz, s, cols = rw.shape
    width = w0.shape[-1]
    n_pairs = width // LANES
    slab = cols - 3 * width
    dl, al, gl = w2.shape[0], a2.shape[0], g2.shape[0]
    w2f = jnp.zeros((slab, width), F32).at[0:dl].set(w2).astype(BF16)
    a2f = jnp.zeros((slab, width), F32).at[dl:dl + al].set(a2).astype(BF16)
    g2f = jnp.zeros((slab, width), F32).at[dl + al:dl + al + gl].set(g2).astype(BF16)
    mup = jnp.zeros((1, cols), F32).at[0, :mu.shape[-1]].set(mu)
    tri, sm, im, hs = _rwkv_consts(width)
    row = lambda x: x.reshape(1, width)
    const = lambda i: (0, 0)
    vec = pl.BlockSpec((1, width), const)
    return pl.pallas_call(
        _rwkv7_kernel,
        out_shape=jax.ShapeDtypeStruct((bsz, s, width), BF16),
        grid=(s // CHUNK,),
        in_specs=[pl.BlockSpec((bsz, CHUNK, cols), lambda i: (0, i, 0)),
                  pl.BlockSpec((1, cols), const),
                  vec, vec, vec, vec, vec, vec, vec,
                  pl.BlockSpec((slab, width), const),
                  pl.BlockSpec((slab, width), const),
                  pl.BlockSpec((slab, width), const),
                  pl.BlockSpec(tri.shape, const),
                  pl.BlockSpec(sm.shape, const),
                  pl.BlockSpec(im.shape, const),
                  pl.BlockSpec(hs.shape, const)],
        out_specs=pl.BlockSpec((bsz, CHUNK, width), lambda i: (0, i, 0)),
        scratch_shapes=[pltpu.VMEM((bsz, 1, cols), F32),
                        pltpu.VMEM((bsz, n_pairs, LANES, LANES), F32)],
        compiler_params=_cparams(("arbitrary",)),
        name="rwkv7",
    )(rw, mup, row(w0), row(a0), row(k_k), row(k_a), row(r_k), row(gn_w), row(gn_b),
      w2f, a2f, g2f, tri, sm, im, hs)


def _out_proj_kernel(n_groups, n_experts,
                     x_ref, oa_ref, ob_ref, ga_ref, gb_ref, gt1_ref, sc2_ref, sh2_ref, g2_ref,
                     wa_ref, wb_ref, wo_ref, wr_ref, wrl_ref, br_ref, tril_ref,
                     x1_ref, h2_ref, route_ref, cnt_ref, carry_ref):
    tm = x_ref.shape[1]
    first = (pl.program_id(0) == 0) & (pl.program_id(1) == 0)

    @pl.when(first)
    def _():
        carry_ref[...] = jnp.zeros_like(carry_ref)

    pa = _dot(oa_ref[0], wa_ref[...])
    pb = _dot(ob_ref[0], wb_ref[...])
    mixed = ga_ref[0].astype(F32) * pa + gb_ref[0].astype(F32) * pb
    x1 = x_ref[0] + gt1_ref[0] * _dot(mixed.astype(BF16), wo_ref[...])
    x1_ref[0] = x1
    ms = jnp.mean(x1 * x1, axis=-1, keepdims=True)
    h2 = (x1 * lax.rsqrt(ms + NORM_EPS) * g2_ref[...]) * (1.0 + sc2_ref[0]) + sh2_ref[0]
    _rows_to_tiles(h2_ref, h2)

    h2_hi = h2.astype(BF16)
    h2_lo = (h2 - h2_hi.astype(F32)).astype(BF16)
    logits = (_dot(h2_hi, wr_ref[...]) + _dot(h2_lo, wr_ref[...]) + _dot(h2_hi, wrl_ref[...])) + br_ref[...]
    lane = lax.broadcasted_iota(jnp.int32, logits.shape, 1)
    neg = jnp.float32(-jnp.inf)
    big = jnp.int32(1 << 20)
    eg = n_experts // n_groups
    is_g = (lane >= n_experts) & (lane < n_experts + n_groups)
    lg = jnp.where(is_g, logits, neg)
    mg = jnp.max(lg, axis=-1, keepdims=True)
    p_grp = 1.0 / jnp.sum(jnp.where(is_g, jnp.exp(lg - mg), 0.0), axis=-1, keepdims=True)
    gidx = jnp.min(jnp.where(lg == mg, lane, big), axis=-1, keepdims=True) - n_experts
    sel = (lane >= gidx * eg) & (lane < gidx * eg + eg)
    le = jnp.where(sel, logits, neg)
    me = jnp.max(le, axis=-1, keepdims=True)
    pe_un = jnp.where(sel, jnp.exp(le - me), 0.0)
    pe = jnp.where(sel, pe_un / jnp.sum(pe_un, axis=-1, keepdims=True), -1.0)
    v1 = jnp.max(pe, axis=-1, keepdims=True)
    i1 = jnp.min(jnp.where(pe == v1, lane, big), axis=-1, keepdims=True)
    pe2 = jnp.where(lane == i1, -1.0, pe)
    v2 = jnp.max(pe2, axis=-1, keepdims=True)
    i2 = jnp.min(jnp.where(pe2 == v2, lane, big), axis=-1, keepdims=True)
    wsum = v1 + v2
    w1 = p_grp * v1 / wsum
    w2 = p_grp * v2 / wsum

    oh1 = (lane == i1).astype(F32)
    oh2 = (lane == i2).astype(F32)
    both = oh1 + oh2
    before = _dot(tril_ref[...], both.astype(BF16)) + carry_ref[...]
    rank1 = jnp.sum(oh1 * before, axis=-1, keepdims=True)
    rank2 = jnp.sum(oh2 * before, axis=-1, keepdims=True)
    carry_ref[...] = carry_ref[...] + jnp.sum(both, axis=0, keepdims=True)
    cnt_ref[...] = carry_ref[...]

    out = jnp.where(lane == 0, w1, 0.0)
    out = jnp.where(lane == 1, w2, out)
    out = jnp.where(lane == 2, i1.astype(F32), out)
    out = jnp.where(lane == 3, i2.astype(F32), out)
    out = jnp.where(lane == 4, rank1, out)
    out = jnp.where(lane == 5, rank2, out)
    route_ref[0] = out


def _out_proj(x, o_a, o_b, gates, mod3, norm2_g, wa, wb, wo, wr, br, n_groups, n_experts, tm):
    bsz, s, d = x.shape
    wdt = o_a.shape[-1]
    tril = jnp.asarray(np.tril(np.ones((tm, tm), np.float32), -1), BF16)
    wr_hi = wr.astype(BF16)
    wr_lo = (wr - wr_hi.astype(F32)).astype(BF16)
    const = lambda b, i: (0, 0)
    tile = lambda b, i: (b, i, 0)
    kern = functools.partial(_out_proj_kernel, n_groups, n_experts)
    return pl.pallas_call(
        kern,
        out_shape=(jax.ShapeDtypeStruct((bsz, s, d), F32),
                   jax.ShapeDtypeStruct((bsz * s * SUB, LANES), U32),
                   jax.ShapeDtypeStruct((bsz, s, LANES), F32),
                   jax.ShapeDtypeStruct((1, LANES), F32)),
        grid=(bsz, s // tm),
        in_specs=[pl.BlockSpec((1, tm, d), tile),
                  pl.BlockSpec((1, tm, wdt), tile),
                  pl.BlockSpec((1, tm, wdt), tile),
                  pl.BlockSpec((1, tm, d), lambda b, i: (b, i, 0)),
                  pl.BlockSpec((1, tm, d), lambda b, i: (b, i, 1)),
                  pl.BlockSpec((1, 1, d), lambda b, i: (b * 6 + 2, 0, 0)),
                  pl.BlockSpec((1, 1, d), lambda b, i: (b * 6 + 4, 0, 0)),
                  pl.BlockSpec((1, 1, d), lambda b, i: (b * 6 + 3, 0, 0)),
                  pl.BlockSpec((1, d), const),
                  pl.BlockSpec(wa.shape, const),
                  pl.BlockSpec(wb.shape, const),
                  pl.BlockSpec(wo.shape, const),
                  pl.BlockSpec(wr.shape, const),
                  pl.BlockSpec(wr.shape, const),
                  pl.BlockSpec((1, LANES), const),
                  pl.BlockSpec((tm, tm), const)],
        out_specs=(pl.BlockSpec((1, tm, d), tile),
                   pl.BlockSpec((tm * SUB, LANES), lambda b, i: (b * (s // tm) + i, 0)),
                   pl.BlockSpec((1, tm, LANES), tile),
                   pl.BlockSpec((1, LANES), const)),
        scratch_shapes=[pltpu.VMEM((1, LANES), F32)],
        compiler_params=_cparams(("arbitrary", "arbitrary")),
        name="out_proj",
    )(x, o_a, o_b, gates, gates, mod3, mod3, mod3, norm2_g.reshape(1, d), wa, wb, wo, wr_hi, wr_lo, br, tril)


def _moe_dispatch_kernel(tm, n_tok, blk, dest_ref, zstart_ref, zcnt_ref, nused_ref, h_ref, xb_ref, smap_ref,
                         zbuf, stage, sem, zsem):
    i = pl.program_id(0)
    n_steps = pl.num_programs(0)
    n_slots = smap_ref.shape[0]
    n_experts = zcnt_ref.shape[0]

    @pl.when(i == 0)
    def _():
        def init(s_, carry):
            smap_ref[s_] = 0
            return carry
        lax.fori_loop(nused_ref[0] * blk, n_slots, init, 0)
        zbuf[...] = jnp.zeros_like(zbuf)

        def zero_row(e, j):
            dst = pl.multiple_of((zstart_ref[e] + j) * SUB, SUB)
            return pltpu.make_async_copy(zbuf.at[pl.ds(0, SUB), :], xb_ref.at[pl.ds(dst, SUB), :], zsem)

        def zero_block(b):
            dst = pl.multiple_of(b * (blk * SUB), blk * SUB)
            return pltpu.make_async_copy(zbuf, xb_ref.at[pl.ds(dst, blk * SUB), :], zsem)

        for e in range(n_experts):
            def zstart(j, carry, e=e):
                zero_row(e, j).start()
                smap_ref[zstart_ref[e] + j] = TOP_K * n_tok + e * blk + j
                return carry
            lax.fori_loop(0, zcnt_ref[e], zstart, 0)

        def bstart(b, carry):
            zero_block(b).start()
            return carry
        lax.fori_loop(nused_ref[0], n_slots // blk, bstart, 0)
        for e in range(n_experts):
            def zwait(j, carry, e=e):
                zero_row(e, j).wait()
                return carry
            lax.fori_loop(0, zcnt_ref[e], zwait, 0)

        def bwait(b, carry):
            zero_block(b).wait()
            return carry
        lax.fori_loop(nused_ref[0], n_slots // blk, bwait, 0)

    slot = i % 2
    stage[slot] = h_ref[...]
    for r in range(tm):
        tok = i * tm + r
        for k in range(TOP_K):
            d = dest_ref[tok * TOP_K + k]
            dst = pl.multiple_of(d * SUB, SUB)
            pltpu.make_async_copy(stage.at[slot, pl.ds(r * SUB, SUB), :], xb_ref.at[pl.ds(dst, SUB), :],
                                  sem.at[slot]).start(priority=k)
            smap_ref[d] = k * n_tok + tok

    def wait_step(s_):
        for _ in range(TOP_K):
            pltpu.make_async_copy(stage.at[s_], xb_ref.at[pl.ds(0, tm * SUB), :], sem.at[s_]).wait()

    @pl.when(i > 0)
    def _():
        wait_step(1 - slot)

    @pl.when(i == n_steps - 1)
    def _():
        wait_step(slot)


def _moe_dispatch(h2, dest, zstart, zcnt, n_used, n_slots, blk, tm):
    n_tok = h2.shape[0] // SUB
    grid_spec = pltpu.PrefetchScalarGridSpec(
        num_scalar_prefetch=4,
        grid=(n_tok // tm,),
        in_specs=[pl.BlockSpec((tm * SUB, LANES), lambda i, *_: (i, 0))],
        out_specs=(pl.BlockSpec(memory_space=pl.ANY),
                   pl.BlockSpec(memory_space=pltpu.SMEM)),
        scratch_shapes=[pltpu.VMEM((blk * SUB, LANES), U32),
                        pltpu.VMEM((2, tm * SUB, LANES), U32),
                        pltpu.SemaphoreType.DMA((2,)),
                        pltpu.SemaphoreType.DMA],
    )
    return pl.pallas_call(
        functools.partial(_moe_dispatch_kernel, tm, n_tok, blk),
        out_shape=(jax.ShapeDtypeStruct((n_slots * SUB, LANES), U32),
                   jax.ShapeDtypeStruct((n_slots,), jnp.int32)),
        grid_spec=grid_spec,
        compiler_params=_cparams(("arbitrary",)),
        name="moe_dispatch",
    )(dest, zstart, zcnt, n_used, h2)


def _moe_expert_kernel(n_tok, blk_e_ref, nused_ref, smap_ref, x_ref, wg_ref, wu_ref, wd_ref, y_ref,
                       ystage, wgb, wub, wdb, sem):
    i = pl.program_id(0)
    n_used = nused_ref[0]
    blk = x_ref.shape[0] // SUB
    first_real_blocks = TOP_K * n_tok // blk
    first_spare = y_ref.shape[0] // SUB - blk

    def issue(b, slot, rows):
        for r in rows:
            t = jnp.where(b >= 0, smap_ref[jnp.maximum(b, 0) * blk + r], first_spare + r)
            dst = pl.multiple_of(t * SUB, SUB)
            pltpu.make_async_copy(ystage.at[slot, pl.ds(r * SUB, SUB), :], y_ref.at[pl.ds(dst, SUB), :],
                                  sem.at[slot]).start(priority=r % 2)

    def wait_block(slot):
        pltpu.make_async_copy(ystage.at[slot], y_ref.at[pl.ds(0, blk * SUB), :], sem.at[slot]).wait()

    @pl.when(i == 0)
    def _():
        ystage[...] = jnp.zeros_like(ystage)
        n_spare_blocks = y_ref.shape[0] // (blk * SUB) - first_real_blocks

        def spare_copy(c):
            dst = (first_real_blocks + c) * blk * SUB
            return pltpu.make_async_copy(ystage.at[0], y_ref.at[pl.ds(dst, blk * SUB), :], sem.at[0])

        for c in range(n_spare_blocks):
            spare_copy(c).start()
        for c in range(n_spare_blocks):
            spare_copy(c).wait()

    new_expert = (i == 0) | (blk_e_ref[i] != blk_e_ref[jnp.maximum(i - 1, 0)])

    @pl.when((i < n_used) & new_expert)
    def _():
        wgb[...] = wg_ref[0].astype(BF16)
        wub[...] = wu_ref[0].astype(BF16)
        wdb[...] = wd_ref[0].astype(BF16)

    @pl.when(i < n_used)
    def _():
        slot = i % 2
        pslot = 1 - slot

        @pl.when(i > 0)
        def _():
            wait_block(slot)

        q = blk // 4
        xb = _tiles_to_rows(x_ref, blk).astype(BF16)
        issue(i - 1, pslot, range(0, q))
        hg = _dot(xb, wgb[...])
        issue(i - 1, pslot, range(q, 2 * q))
        hu = _dot(xb, wub[...])
        issue(i - 1, pslot, range(2 * q, 3 * q))
        hid = (_silu(hg) * hu).astype(BF16)
        y = _dot(hid, wdb[...])
        issue(i - 1, pslot, range(3 * q, blk))
        _rows_to_tiles(ystage.at[slot], y)

        @pl.when(i == n_used - 1)
        def _():
            issue(i, slot, range(blk))
            wait_block(pslot)
            wait_block(slot)


def _moe_experts(xb, smap, blk_e, n_used, w_gate, w_up, w_down, blk, n_tok):
    d = w_gate.shape[1]
    assert d == 2 * SUB * LANES and xb.shape[1] == LANES
    n_slots = smap.shape[0]
    nb = n_slots // blk
    f = w_gate.shape[-1]
    n_experts = w_gate.shape[0]
    assert (TOP_K * n_tok) % blk == 0
    n_tiles = TOP_K * n_tok + (n_experts + 1) * blk
    grid_spec = pltpu.PrefetchScalarGridSpec(
        num_scalar_prefetch=3,
        grid=(nb,),
        in_specs=[pl.BlockSpec((blk * SUB, LANES), lambda i, be, nu, sm: (jnp.minimum(i, nu[0] - 1), 0)),
                  pl.BlockSpec((1, d, f), lambda i, be, nu, sm: (be[i], 0, 0)),
                  pl.BlockSpec((1, d, f), lambda i, be, nu, sm: (be[i], 0, 0)),
                  pl.BlockSpec((1, f, d), lambda i, be, nu, sm: (be[i], 0, 0))],
        out_specs=pl.BlockSpec(memory_space=pl.ANY),
        scratch_shapes=[pltpu.VMEM((2, blk * SUB, LANES), U32),
                        pltpu.VMEM((d, f), BF16),
                        pltpu.VMEM((d, f), BF16),
                        pltpu.VMEM((f, d), BF16),
                        pltpu.SemaphoreType.DMA((2,))],
    )
    return pl.pallas_call(
        functools.partial(_moe_expert_kernel, n_tok),
        out_shape=jax.ShapeDtypeStruct((n_tiles * SUB, LANES), U32),
        grid_spec=grid_spec,
        compiler_params=_cparams(("arbitrary",)),
        name="moe_experts",
    )(blk_e, n_used, smap, xb, w_gate, w_up, w_down)


def _moe_combine_kernel(x1_ref, route_ref, gt2_ref, fg_ref, y1_ref, y2_ref, o_ref):
    tm = x1_ref.shape[0]
    route = route_ref[...]
    moe = route[:, 0:1] * _tiles_to_rows(y1_ref, tm) + route[:, 1:2] * _tiles_to_rows(y2_ref, tm)
    xo = x1_ref[...] + gt2_ref[0] * moe
    ms = jnp.mean(xo * xo, axis=-1, keepdims=True)
    o_ref[...] = xo * lax.rsqrt(ms + NORM_EPS) * fg_ref[...]


def _moe_combine(x1, route, yt, mod3, final_g, s, tm):
    n, d = x1.shape
    tiles_per_batch = s // tm
    n_steps = n // tm
    return pl.pallas_call(
        _moe_combine_kernel,
        out_shape=jax.ShapeDtypeStruct((n, d), F32),
        grid=(n_steps,),
        in_specs=[pl.BlockSpec((tm, d), lambda i: (i, 0)),
                  pl.BlockSpec((tm, LANES), lambda i: (i, 0)),
                  pl.BlockSpec((1, 1, d), lambda i: ((i // tiles_per_batch) * 6 + 5, 0, 0)),
                  pl.BlockSpec((1, d), lambda i: (0, 0)),
                  pl.BlockSpec((tm * SUB, LANES), lambda i: (i, 0)),
                  pl.BlockSpec((tm * SUB, LANES), lambda i: (n_steps + i, 0))],
        out_specs=pl.BlockSpec((tm, d), lambda i: (i, 0)),
        compiler_params=_cparams(("arbitrary",)),
        name="moe_combine",
    )(x1, route, mod3, final_g.reshape(1, d), yt, yt)


def _pick(n, candidates):
    for t in candidates:
        if n % t == 0:
            return t
    raise ValueError(f"no tile in {candidates} divides {n}")


def kernel(x, c, ada_w, ada_b, norm1_g, w_in, hg_lb, hg_norm_g, rw_mu, rw_w0, rw_w2, rw_a0, rw_a2, rw_g2, rw_kk, rw_ka, rw_rk, rw_gn_w, rw_gn_b, w_proj_a, w_proj_b, w_out, norm2_g, router_g_w, router_g_b, router_e_w, router_e_b, exp_w_gate, exp_w_up, exp_w_down, final_g):
    bsz, s, d = x.shape
    depth = ada_w.shape[0]
    hg_f = hg_lb.shape[-1]
    hg_w = hg_norm_g.shape[-1]
    rw_w = rw_w0.shape[-1]
    rw_cols = rw_mu.shape[-1]
    n_groups = router_g_w.shape[-1]
    n_experts = router_e_w.shape[-1]
    assert hg_f == hg_w and s % CHUNK == 0 and n_experts + n_groups <= LANES and d == 2 * SUB * LANES

    lb_all = jnp.cumsum(jax.nn.softmax(hg_lb.astype(F32), axis=0), axis=0)
    n = bsz * s
    blk = 256
    n_blocks = (n * TOP_K + n_experts * blk) // blk
    for l in range(depth):
        mod = _ada_mod(c, ada_w[l], ada_b[l])
        mod3 = mod.reshape(bsz * 6, 1, d)

        hg_cols = 2 * hg_f + 2 * hg_w
        rw_pad = -(-rw_cols // 256) * 256
        wl = w_in[l]
        w_hg = wl[:, :hg_cols].astype(BF16)
        w_rw = jnp.zeros((d, rw_pad), BF16).at[:, :rw_cols].set(wl[:, hg_cols:hg_cols + rw_cols].astype(BF16))
        w_gt = wl[:, hg_cols + rw_cols:].astype(BF16)
        hg, rw, gates = _in_proj(x, mod3, norm1_g[l], w_hg, w_rw, w_gt, _pick(s, (512, 256, 128, 64)))

        o_a = _hgrn2(hg, lb_all[l], hg_norm_g[l], _pick(s, (512, 256, 128, 64)))
        o_b = _rwkv7(rw, rw_mu[l], rw_w0[l], rw_w2[l], rw_a0[l], rw_a2[l], rw_g2[l],
                     rw_kk[l], rw_ka[l], rw_rk[l].reshape(-1), rw_gn_w[l], rw_gn_b[l])

        wr = jnp.zeros((d, LANES), F32).at[:, :n_experts].set(router_e_w[l])
        wr = wr.at[:, n_experts:n_experts + n_groups].set(router_g_w[l])
        br = jnp.zeros((1, LANES), F32).at[0, :n_experts].set(router_e_b[l])
        br = br.at[0, n_experts:n_experts + n_groups].set(router_g_b[l])
        x1, h2, route, counts = _out_proj(
            x, o_a, o_b, gates, mod3, norm2_g[l],
            w_proj_a[l].astype(BF16), w_proj_b[l].astype(BF16), w_out[l].astype(BF16),
            wr, br, n_groups, n_experts, _pick(s, (512, 256, 128, 64)))

        route2 = route.reshape(n, LANES)
        eid = route2[:, 2:4].astype(jnp.int32)
        rank = route2[:, 4:6].astype(jnp.int32)
        cnt = counts[0, :n_experts].astype(jnp.int32)
        padded = (cnt + blk - 1) // blk * blk
        pad_end = jnp.cumsum(padded)
        pad_start = pad_end - padded
        dest = pad_start[eid] + rank
        blk_start = jnp.arange(n_blocks, dtype=jnp.int32) * blk
        blk_e = jnp.minimum(jnp.sum((pad_end[None, :] <= blk_start[:, None]).astype(jnp.int32), axis=1),
                            n_experts - 1)
        n_used = (pad_end[-1:] // blk).astype(jnp.int32)
        tm = _pick(s, (256, 128, 64))
        xb, smap = _moe_dispatch(h2, dest.reshape(-1), pad_start + cnt, padded - cnt, n_used,
                                 n_blocks * blk, blk, tm)
        yt = _moe_experts(xb, smap, blk_e, n_used, exp_w_gate[l], exp_w_up[l], exp_w_down[l], blk, n)
        last = l == depth - 1
        assert last, "the final RMSNorm is fused into the last layer's combine"
        out = _moe_combine(x1.reshape(n, d), route2, yt, mod3, final_g, s, tm)
        x = out.reshape(bsz, s, d)
    return x
```

```python
import functools

import numpy as np
import jax
import jax.numpy as jnp
from jax import lax
from jax.experimental import pallas as pl
from jax.experimental.pallas import tpu as pltpu

F32 = jnp.float32
BF16 = jnp.bfloat16
HIGHEST = lax.Precision.HIGHEST

NORM_EPS = 1e-6
HG_HEAD = 128
RW_HEAD = 64
RW_GN_EPS = 64e-5
TOP_K = 2
CHUNK = 64
LANES = 128
SUB = 4
U32 = jnp.uint32
MXU_K = 256
VMEM_LIMIT = 56 * 1024 * 1024

NT = (((1,), (1,)), ((), ()))
TN = (((0,), (0,)), ((), ()))


def _dot(a, b, dims=None, precision=None):
    if dims is None:
        return jnp.dot(a, b, preferred_element_type=F32, precision=precision)
    return lax.dot_general(a, b, dims, preferred_element_type=F32, precision=precision)


def _split3(x):
    hi = x.astype(BF16)
    r1 = x - hi.astype(F32)
    mid = r1.astype(BF16)
    lo = (r1 - mid.astype(F32)).astype(BF16)
    return hi, mid, lo


def _dot_exact_lhs(m3_bf16, x):
    return _dot(m3_bf16, jnp.concatenate(_split3(x), axis=0))


def _head_sums(x, m2_bf16):
    outs = []
    for g0 in range(0, x.shape[1], MXU_K):
        xg = x[:, g0:g0 + MXU_K]
        hi = xg.astype(BF16)
        lo = (xg - hi.astype(F32)).astype(BF16)
        outs.append(_dot(jnp.concatenate([hi, lo], axis=1), m2_bf16))
    return jnp.concatenate(outs, axis=1)


def _sigmoid(x):
    return 1.0 / (1.0 + jnp.exp(-x))


def _silu(x):
    return x * _sigmoid(x)


def _rows_to_tiles(ref, val):
    m, half = val.shape[0], val.shape[1] // 2
    hi = lax.bitcast_convert_type(val[:, :half].astype(BF16).astype(F32), U32)
    lo = lax.bitcast_convert_type(val[:, half:].astype(BF16).astype(F32), U32)
    w = (hi & jnp.uint32(0xFFFF0000)) | (lo >> 16)
    for j in range(SUB):
        ref[pl.ds(j, m, stride=SUB), :] = w[:, j * LANES:(j + 1) * LANES]


def _tiles_to_rows(ref, m, base=0):
    w = jnp.concatenate([ref[pl.ds(base * SUB + j, m, stride=SUB), :] for j in range(SUB)], axis=1)
    hi = lax.bitcast_convert_type(w & jnp.uint32(0xFFFF0000), F32)
    lo = lax.bitcast_convert_type(w << 16, F32)
    return jnp.concatenate([hi, lo], axis=1)


def _cparams(sem):
    return pltpu.CompilerParams(dimension_semantics=sem, vmem_limit_bytes=VMEM_LIMIT)


def _ada_kernel(c_ref, w_ref, b_ref, o_ref):
    c = c_ref[...]
    o_ref[...] = _dot(_silu(c), w_ref[...], precision=HIGHEST) + b_ref[...]


def _ada_mod(c, w, b):
    bsz, d = c.shape
    n = w.shape[1]
    rows = 8
    cp = jnp.zeros((rows, d), F32).at[:bsz].set(c)
    tn = 1536
    out = pl.pallas_call(
        _ada_kernel,
        out_shape=jax.ShapeDtypeStruct((rows, n), F32),
        grid=(n // tn,),
        in_specs=[pl.BlockSpec((rows, d), lambda j: (0, 0)),
                  pl.BlockSpec((d, tn), lambda j: (0, j)),
                  pl.BlockSpec((1, tn), lambda j: (0, j))],
        out_specs=pl.BlockSpec((rows, tn), lambda j: (0, j)),
        compiler_params=_cparams(("arbitrary",)),
        name="ada_mod",
    )(cp, w, b.reshape(1, n))
    return out[:bsz]


def _in_proj_kernel(x_ref, sh_ref, sc_ref, g_ref, whg_ref, wrw_ref, wgt_ref, hg_ref, rw_ref, gt_ref):
    x = x_ref[0]
    ms = jnp.mean(x * x, axis=-1, keepdims=True)
    h = (x * lax.rsqrt(ms + NORM_EPS) * g_ref[...]) * (1.0 + sc_ref[0]) + sh_ref[0]
    hb = h.astype(BF16)
    step = 512
    for n0 in range(0, whg_ref.shape[1], step):
        hg_ref[0, :, n0:n0 + step] = _dot(hb, whg_ref[:, n0:n0 + step])
    for n0 in range(0, wrw_ref.shape[1], 256):
        rw_ref[0, :, n0:n0 + 256] = _dot(hb, wrw_ref[:, n0:n0 + 256])
    for n0 in range(0, wgt_ref.shape[1], step):
        gt_ref[0, :, n0:n0 + step] = _sigmoid(_dot(hb, wgt_ref[:, n0:n0 + step])).astype(BF16)


def _in_proj(x, mod3, norm_g, w_hg, w_rw, w_gt, tm):
    bsz, s, d = x.shape
    n_hg, n_rw, n_gt = w_hg.shape[1], w_rw.shape[1], w_gt.shape[1]
    const = lambda b, i: (0, 0)
    return pl.pallas_call(
        _in_proj_kernel,
        out_shape=(jax.ShapeDtypeStruct((bsz, s, n_hg), F32),
                   jax.ShapeDtypeStruct((bsz, s, n_rw), F32),
                   jax.ShapeDtypeStruct((bsz, s, n_gt), BF16)),
        grid=(bsz, s // tm),
        in_specs=[pl.BlockSpec((1, tm, d), lambda b, i: (b, i, 0)),
                  pl.BlockSpec((1, 1, d), lambda b, i: (b * 6 + 0, 0, 0)),
                  pl.BlockSpec((1, 1, d), lambda b, i: (b * 6 + 1, 0, 0)),
                  pl.BlockSpec((1, d), const),
                  pl.BlockSpec((d, n_hg), const),
                  pl.BlockSpec((d, n_rw), const),
                  pl.BlockSpec((d, n_gt), const)],
        out_specs=(pl.BlockSpec((1, tm, n_hg), lambda b, i: (b, i, 0)),
                   pl.BlockSpec((1, tm, n_rw), lambda b, i: (b, i, 0)),
                   pl.BlockSpec((1, tm, n_gt), lambda b, i: (b, i, 0))),
        compiler_params=_cparams(("arbitrary", "arbitrary")),
        name="in_proj",
    )(x, mod3, mod3, norm_g.reshape(1, d), w_hg, w_rw, w_gt)


_HG_LEVELS = (32, 16, 8, 4, 2, 1)


def _hgrn2_consts(width):
    c = CHUNK
    t = np.arange(c)[:, None]
    s = np.arange(c)[None, :]
    blocks = [(s <= t), (s > t)]
    lvl_masks = []
    right = []
    for h in _HG_LEVELS:
        m = (t // (2 * h)) * 2 * h + h
        is_r = (t & h) != 0
        blk = np.where(is_r, (s >= m) & (s <= t), (s > t) & (s <= m - 1))
        blocks.append(blk)
        lvl_masks.append(is_r & ((s & h) == 0) & ((t // (2 * h)) == (s // (2 * h))))
        right.append(np.broadcast_to(is_r, (c, width)))
    mst = np.tile(np.concatenate(blocks, axis=0).astype(np.float32), (1, 3))
    lm = np.stack([np.eye(c, dtype=bool)] + lvl_masks).astype(np.float32)
    rm = np.stack(right).astype(np.float32)
    return jnp.asarray(mst, BF16), jnp.asarray(lm, F32), jnp.asarray(rm, F32)


def _hgrn2_kernel(q_ref, f_ref, i_ref, g_ref, lb_ref, ng_ref, mst_ref, lm_ref, rm_ref, o_ref, st_ref):
    c = CHUNK
    n_heads = q_ref.shape[2] // HG_HEAD
    n_chunks = q_ref.shape[1] // c

    @pl.when(pl.program_id(1) == 0)
    def _():
        st_ref[...] = jnp.zeros_like(st_ref)

    mst = mst_ref[...]
    lb = lb_ref[...]
    ng = ng_ref[...]
    heads = [slice(hd * HG_HEAD, (hd + 1) * HG_HEAD) for hd in range(n_heads)]

    nsub = 2 if n_chunks % 2 == 0 else 1

    def chunk_body(ci, carry):
        r0 = pl.multiple_of(ci * (nsub * c), nsub * c)
        subs = []
        for j in range(nsub):
            rows = pl.ds(r0 + j * c, c)
            q = _silu(q_ref[0, rows, :])
            f = lb + (1.0 - lb) * _sigmoid(f_ref[0, rows, :])
            k = 1.0 - f
            ex = jnp.exp(_dot_exact_lhs(mst, jnp.log(f)))
            subs.append(dict(rows=rows, q=q, k=k, ex=ex, vb=i_ref[0, rows, :].astype(BF16),
                             qd=(q * ex[0:c]).astype(BF16), kr=(k * ex[c:2 * c]).astype(BF16)))
        for sb in subs:
            qb, kb = sb['q'].astype(BF16), sb['k'].astype(BF16)
            sb['sc'] = [lm_ref[0] * _dot(qb[:, ls], kb[:, ls], NT) for ls in heads]
            sb['dqk'] = sb['q'] - sb['k']
        for li in range(len(_HG_LEVELS)):
            for sb in subs:
                g_l = ((sb['k'] + rm_ref[li] * sb['dqk']) * sb['ex'][(2 + li) * c:(3 + li) * c]).astype(BF16)
                sb['sc'] = [s_h + lm_ref[li + 1] * _dot(g_l[:, ls], g_l[:, ls], NT)
                            for s_h, ls in zip(sb['sc'], heads)]
        for sb in subs:
            sb['kv'] = [_dot(sb['vb'][:, ls], sb['kr'][:, ls], TN) for ls in heads]
            sb['o'] = [_dot(s_h.astype(BF16), sb['vb'][:, ls]) for s_h, ls in zip(sb['sc'], heads)]
        sts = [st_ref[hd] for hd in range(n_heads)]
        for sb in subs:
            sb['o'] = [o_h + _dot(sb['qd'][:, ls], st.astype(BF16), NT) for o_h, ls, st in zip(sb['o'], heads, sts)]
            sts = [st * sb['ex'][c - 1:c, ls] + kv for st, ls, kv in zip(sts, heads, sb['kv'])]
        for hd in range(n_heads):
            st_ref[hd] = sts[hd]
        for sb in subs:
            on = [o_h * lax.rsqrt(jnp.mean(o_h * o_h, axis=-1, keepdims=True) + NORM_EPS) for o_h in sb['o']]
            o_full = jnp.concatenate(on, axis=1) * ng
            o_ref[0, sb['rows'], :] = (o_full * _silu(g_ref[0, sb['rows'], :])).astype(o_ref.dtype)
        return carry

    lax.fori_loop(0, n_chunks // nsub, chunk_body, 0)


def _hgrn2(hg, lb, norm_g, ts):
    bsz, s, n4 = hg.shape
    w = n4 // 4
    mst, lm, rm = _hgrn2_consts(w)
    n_heads = w // HG_HEAD
    const2 = lambda b, i: (0, 0)
    const3 = lambda b, i: (0, 0, 0)
    return pl.pallas_call(
        _hgrn2_kernel,
        out_shape=jax.ShapeDtypeStruct((bsz, s, w), BF16),
        grid=(bsz, s // ts),
        in_specs=[pl.BlockSpec((1, ts, w), lambda b, i: (b, i, 0)),
                  pl.BlockSpec((1, ts, w), lambda b, i: (b, i, 1)),
                  pl.BlockSpec((1, ts, w), lambda b, i: (b, i, 2)),
                  pl.BlockSpec((1, ts, w), lambda b, i: (b, i, 3)),
                  pl.BlockSpec((1, w), const2),
                  pl.BlockSpec((1, w), const2),
                  pl.BlockSpec(mst.shape, const2),
                  pl.BlockSpec(lm.shape, const3),
                  pl.BlockSpec(rm.shape, const3)],
        out_specs=pl.BlockSpec((1, ts, w), lambda b, i: (b, i, 0)),
        scratch_shapes=[pltpu.VMEM((n_heads, HG_HEAD, HG_HEAD), F32)],
        compiler_params=_cparams(("arbitrary", "arbitrary")),
        name="hgrn2",
    )(hg, hg, hg, hg, lb.reshape(1, w), norm_g.reshape(1, w), mst, lm, rm)


def _rwkv_consts(width):
    c = CHUNK
    t = np.arange(c)[:, None]
    s = np.arange(c)[None, :]
    tri = np.tile((s <= t).astype(np.float32), (1, 3))
    tt = np.arange(2 * c)[:, None]
    ss = np.arange(2 * c)[None, :]
    same = (tt // c) == (ss // c)
    strict = same & ((ss % c) < (tt % c))
    incl = same & ((ss % c) <= (tt % c))
    hsum = (np.arange(MXU_K)[:, None] // RW_HEAD) == (np.arange(MXU_K)[None, :] // RW_HEAD)
    hsum = np.tile(hsum, (2, 1))
    return (jnp.asarray(tri, BF16), jnp.asarray(strict.astype(np.float32), F32),
            jnp.asarray(incl.astype(np.float32), F32), jnp.asarray(hsum.astype(np.float32), BF16))


def _rwkv7_kernel(p_ref, mu_ref, w0_ref, a0_ref, kk_ref, ka_ref, rk_ref, gnw_ref, gnb_ref,
                  w2_ref, a2_ref, g2_ref, tri_ref, sm_ref, im_ref, hs_ref,
                  o_ref, carry_ref, zt_ref):
    c = CHUNK
    nb = p_ref.shape[0]
    nch = p_ref.shape[1] // c
    width = o_ref.shape[2]
    n_pairs = width // LANES

    @pl.when(pl.program_id(0) == 0)
    def _():
        carry_ref[...] = jnp.zeros_like(carry_ref)
        zt_ref[...] = jnp.zeros_like(zt_ref)

    hs = hs_ref[...]
    tri = tri_ref[...]
    smask = sm_ref[...] > 0
    imask = im_ref[...] > 0
    lane = lax.broadcasted_iota(jnp.int32, (c, LANES), 1)
    m0 = (lane < RW_HEAD).astype(F32)
    m1 = 1.0 - m0

    def stack(x):
        return jnp.concatenate([x * m0, x * m1], axis=0)

    xs_rows = []
    for b in range(nb):
        p = p_ref[b]
        row = lax.broadcasted_iota(jnp.int32, p.shape, 0)
        prev = jnp.where(row == 0, carry_ref[b], pltpu.roll(p, 1, 0))
        carry_ref[b] = p[nch * c - 1:nch * c, :]
        xs_rows.append(p + mu_ref[...] * (prev - p))
    xs = jnp.concatenate(xs_rows, axis=0)
    r_all = xs[:, 0:width]
    k_all = xs[:, width:2 * width]
    v_all = xs[:, 2 * width:3 * width]
    slab = xs[:, 3 * width:]
    nz = -(w0_ref[...] + _dot(jnp.tanh(slab).astype(BF16), w2_ref[...]))
    softplus = jnp.maximum(nz, 0.0) + jnp.log(1.0 + jnp.exp(-jnp.abs(nz)))
    ld_all = -jnp.exp(-softplus - 0.5)
    a_all = _sigmoid(a0_ref[...] + _dot(slab.astype(BF16), a2_ref[...]))
    g_all = _dot(_sigmoid(slab).astype(BF16), g2_ref[...])
    kk0 = k_all * kk_ref[...]
    kk_all = kk0 * lax.rsqrt(jnp.maximum(_head_sums(kk0 * kk0, hs), 1e-24))
    k2_all = k_all * (1.0 + (a_all - 1.0) * ka_ref[...])

    units = []
    for b, j in [(b, j) for b in range(nb) for j in range(nch)]:
        rb = slice((b * nch + j) * c, (b * nch + j + 1) * c)
        r, k2, v, ld = r_all[rb], k2_all[rb], v_all[rb], ld_all[rb]
        a_in = -kk_all[rb]
        b_in = kk_all[rb] * a_all[rb]
        cum = _dot_exact_lhs(tri, ld)
        cum_t = cum[c - 1:c, :]
        e_c = jnp.exp(cum)
        e_nc = jnp.exp(-cum)
        e_rem = jnp.exp(cum_t - cum)
        at_f = a_in * jnp.exp(cum - ld)
        rt_f = r * e_c
        kt_f = k2 * e_nc
        bt_f = b_in * e_nc
        kh_f = k2 * e_rem
        bh_f = b_in * e_rem
        p_t = jnp.exp(cum_t)
        for pi in range(n_pairs):
            ls = slice(pi * LANES, (pi + 1) * LANES)
            units.append(dict(
                b=b, j=j, pi=pi,
                at=stack(at_f[:, ls]).astype(BF16), rt=stack(rt_f[:, ls]).astype(BF16),
                kt=stack(kt_f[:, ls]).astype(BF16), bt=stack(bt_f[:, ls]).astype(BF16),
                kh=stack(kh_f[:, ls]).astype(BF16), bh=stack(bh_f[:, ls]).astype(BF16),
                vs=stack(v[:, ls]).astype(BF16), p_t=p_t[:, ls]))

    for u in units:
        lhs = jnp.concatenate([u['at'], u['rt']], axis=0)
        u['g'] = _dot(lhs, jnp.concatenate([u['kt'], u['bt']], axis=0), NT)
    for u in units:
        g = u.pop('g')
        u['a_ak'] = jnp.where(smask, g[:2 * c, :2 * c], 0.0).astype(BF16)
        u['pw'] = jnp.where(smask, g[:2 * c, 2 * c:], 0.0).astype(BF16)
        u['a_r'] = jnp.where(jnp.concatenate([imask, imask], axis=1), g[2 * c:], 0.0).astype(BF16)
    for u in units:
        akv = _dot(u.pop('a_ak'), u['vs'])
        u['x'] = jnp.concatenate([u['at'].astype(F32), akv], axis=1)
    n_lvl = int(np.log2(c))
    for lvl in range(n_lvl):
        for u in units:
            u['x'] = u['x'] + _dot(u['pw'], u['x'].astype(BF16))
        if lvl + 1 < n_lvl:
            for u in units:
                u['pw'] = _dot(u['pw'], u['pw']).astype(BF16)
    for u in units:
        x = u.pop('x')
        u['wr'] = jnp.concatenate([x[:, :LANES].astype(BF16), u['rt']], axis=0)
        u['u_loc'] = x[:, LANES:]
    zt = {(b, pi): zt_ref[b, pi] for b in range(nb) for pi in range(n_pairs)}
    for j in range(nch):
        tail = [u for u in units if u['j'] == j]
        for u in tail:
            u['uy'] = _dot(u.pop('wr'), zt[u['b'], u['pi']].astype(BF16), NT)
        for u in tail:
            uy = u.pop('uy')
            u['u'] = (uy[:2 * c] + u.pop('u_loc')).astype(BF16)
            u['y0'] = uy[2 * c:]
        for u in tail:
            vu = jnp.concatenate([u['vs'], u['u']], axis=0)
            u['y'] = u.pop('y0') + _dot(u['a_r'], vu)
            upd = _dot(vu, jnp.concatenate([u['kh'], u['bh']], axis=0), TN)
            zt[u['b'], u['pi']] = zt[u['b'], u['pi']] * u['p_t'] + upd
    for (b, pi), z in zt.items():
        zt_ref[b, pi] = z

    inv_n = 1.0 / RW_HEAD
    y = jnp.concatenate(
        [jnp.concatenate([u['y'][:c] + u['y'][c:] for u in units if (u['b'], u['j']) == (b, j)], axis=1)
         for b in range(nb) for j in range(nch)], axis=0)
    mean = _head_sums(y, hs) * inv_n
    d = y - mean
    var = _head_sums(d * d, hs) * inv_n
    yn = d * lax.rsqrt(var + RW_GN_EPS) * gnw_ref[...] + gnb_ref[...]
    bonus = _head_sums(r_all * k2_all * rk_ref[...], hs) * v_all
    out = ((yn + bonus) * g_all).astype(o_ref.dtype)
    for b in range(nb):
        o_ref[b] = out[b * nch * c:(b + 1) * nch * c]


def _rwkv7(rw, mu, w0, w2, a0, a2, g2, k_k, k_a, r_k, gn_w, gn_b, ts):
    bsz, s, cols = rw.shape
    width = w0.shape[-1]
    n_pairs = width // LANES
    slab = cols - 3 * width
    dl, al, gl = w2.shape[0], a2.shape[0], g2.shape[0]
    w2f = jnp.zeros((slab, width), F32).at[0:dl].set(w2).astype(BF16)
    a2f = jnp.zeros((slab, width), F32).at[dl:dl + al].set(a2).astype(BF16)
    g2f = jnp.zeros((slab, width), F32).at[dl + al:dl + al + gl].set(g2).astype(BF16)
    mup = jnp.zeros((1, cols), F32).at[0, :mu.shape[-1]].set(mu)
    tri, sm, im, hs = _rwkv_consts(width)
    row = lambda x: x.reshape(1, width)
    const = lambda i: (0, 0)
    vec = pl.BlockSpec((1, width), const)
    return pl.pallas_call(
        _rwkv7_kernel,
        out_shape=jax.ShapeDtypeStruct((bsz, s, width), BF16),
        grid=(s // ts,),
        in_specs=[pl.BlockSpec((bsz, ts, cols), lambda i: (0, i, 0)),
                  pl.BlockSpec((1, cols), const),
                  vec, vec, vec, vec, vec, vec, vec,
                  pl.BlockSpec((slab, width), const),
                  pl.BlockSpec((slab, width), const),
                  pl.BlockSpec((slab, width), const),
                  pl.BlockSpec(tri.shape, const),
                  pl.BlockSpec(sm.shape, const),
                  pl.BlockSpec(im.shape, const),
                  pl.BlockSpec(hs.shape, const)],
        out_specs=pl.BlockSpec((bsz, ts, width), lambda i: (0, i, 0)),
        scratch_shapes=[pltpu.VMEM((bsz, 1, cols), F32),
                        pltpu.VMEM((bsz, n_pairs, LANES, LANES), F32)],
        compiler_params=_cparams(("arbitrary",)),
        name="rwkv7",
    )(rw, mup, row(w0), row(a0), row(k_k), row(k_a), row(r_k), row(gn_w), row(gn_b),
      w2f, a2f, g2f, tri, sm, im, hs)


def _out_proj_kernel(n_groups, n_experts,
                     x_ref, oa_ref, ob_ref, ga_ref, gb_ref, gt1_ref, sc2_ref, sh2_ref, g2_ref,
                     wa_ref, wb_ref, wo_ref, wr_ref, wrl_ref, br_ref, tril_ref,
                     x1_ref, h2_ref, route_ref, cnt_ref, carry_ref):
    tm = x_ref.shape[1]
    first = (pl.program_id(0) == 0) & (pl.program_id(1) == 0)

    @pl.when(first)
    def _():
        carry_ref[...] = jnp.zeros_like(carry_ref)

    pa = _dot(oa_ref[0], wa_ref[...])
    pb = _dot(ob_ref[0], wb_ref[...])
    mixed = ga_ref[0].astype(F32) * pa + gb_ref[0].astype(F32) * pb
    x1 = x_ref[0] + gt1_ref[0] * _dot(mixed.astype(BF16), wo_ref[...])
    x1_ref[0] = x1
    ms = jnp.mean(x1 * x1, axis=-1, keepdims=True)
    h2 = (x1 * lax.rsqrt(ms + NORM_EPS) * g2_ref[...]) * (1.0 + sc2_ref[0]) + sh2_ref[0]
    _rows_to_tiles(h2_ref, h2)

    h2_hi = h2.astype(BF16)
    h2_lo = (h2 - h2_hi.astype(F32)).astype(BF16)
    logits = (_dot(h2_hi, wr_ref[...]) + _dot(h2_lo, wr_ref[...]) + _dot(h2_hi, wrl_ref[...])) + br_ref[...]
    lane = lax.broadcasted_iota(jnp.int32, logits.shape, 1)
    neg = jnp.float32(-jnp.inf)
    big = jnp.int32(1 << 20)
    eg = n_experts // n_groups
    is_g = (lane >= n_experts) & (lane < n_experts + n_groups)
    lg = jnp.where(is_g, logits, neg)
    mg = jnp.max(lg, axis=-1, keepdims=True)
    p_grp = 1.0 / jnp.sum(jnp.where(is_g, jnp.exp(lg - mg), 0.0), axis=-1, keepdims=True)
    gidx = jnp.min(jnp.where(lg == mg, lane, big), axis=-1, keepdims=True) - n_experts
    sel = (lane >= gidx * eg) & (lane < gidx * eg + eg)
    le = jnp.where(sel, logits, neg)
    me = jnp.max(le, axis=-1, keepdims=True)
    pe_un = jnp.where(sel, jnp.exp(le - me), 0.0)
    pe = jnp.where(sel, pe_un / jnp.sum(pe_un, axis=-1, keepdims=True), -1.0)
    v1 = jnp.max(pe, axis=-1, keepdims=True)
    i1 = jnp.min(jnp.where(pe == v1, lane, big), axis=-1, keepdims=True)
    pe2 = jnp.where(lane == i1, -1.0, pe)
    v2 = jnp.max(pe2, axis=-1, keepdims=True)
    i2 = jnp.min(jnp.where(pe2 == v2, lane, big), axis=-1, keepdims=True)
    wsum = v1 + v2
    w1 = p_grp * v1 / wsum
    w2 = p_grp * v2 / wsum

    oh1 = (lane == i1).astype(F32)
    oh2 = (lane == i2).astype(F32)
    both = oh1 + oh2
    before = _dot(tril_ref[...], both.astype(BF16)) + carry_ref[...]
    rank1 = jnp.sum(oh1 * before, axis=-1, keepdims=True)
    rank2 = jnp.sum(oh2 * before, axis=-1, keepdims=True)
    carry_ref[...] = carry_ref[...] + jnp.sum(both, axis=0, keepdims=True)
    cnt_ref[...] = carry_ref[...]

    out = jnp.where(lane == 0, w1, 0.0)
    out = jnp.where(lane == 1, w2, out)
    out = jnp.where(lane == 2, i1.astype(F32), out)
    out = jnp.where(lane == 3, i2.astype(F32), out)
    out = jnp.where(lane == 4, rank1, out)
    out = jnp.where(lane == 5, rank2, out)
    route_ref[0] = out


def _out_proj(x, o_a, o_b, gates, mod3, norm2_g, wa, wb, wo, wr, br, n_groups, n_experts, tm):
    bsz, s, d = x.shape
    wdt = o_a.shape[-1]
    tril = jnp.asarray(np.tril(np.ones((tm, tm), np.float32), -1), BF16)
    wr_hi = wr.astype(BF16)
    wr_lo = (wr - wr_hi.astype(F32)).astype(BF16)
    const = lambda b, i: (0, 0)
    tile = lambda b, i: (b, i, 0)
    kern = functools.partial(_out_proj_kernel, n_groups, n_experts)
    return pl.pallas_call(
        kern,
        out_shape=(jax.ShapeDtypeStruct((bsz, s, d), F32),
                   jax.ShapeDtypeStruct((bsz * s * SUB, LANES), U32),
                   jax.ShapeDtypeStruct((bsz, s, LANES), F32),
                   jax.ShapeDtypeStruct((1, LANES), F32)),
        grid=(bsz, s // tm),
        in_specs=[pl.BlockSpec((1, tm, d), tile),
                  pl.BlockSpec((1, tm, wdt), tile),
                  pl.BlockSpec((1, tm, wdt), tile),
                  pl.BlockSpec((1, tm, d), lambda b, i: (b, i, 0)),
                  pl.BlockSpec((1, tm, d), lambda b, i: (b, i, 1)),
                  pl.BlockSpec((1, 1, d), lambda b, i: (b * 6 + 2, 0, 0)),
                  pl.BlockSpec((1, 1, d), lambda b, i: (b * 6 + 4, 0, 0)),
                  pl.BlockSpec((1, 1, d), lambda b, i: (b * 6 + 3, 0, 0)),
                  pl.BlockSpec((1, d), const),
                  pl.BlockSpec(wa.shape, const),
                  pl.BlockSpec(wb.shape, const),
                  pl.BlockSpec(wo.shape, const),
                  pl.BlockSpec(wr.shape, const),
                  pl.BlockSpec(wr.shape, const),
                  pl.BlockSpec((1, LANES), const),
                  pl.BlockSpec((tm, tm), const)],
        out_specs=(pl.BlockSpec((1, tm, d), tile),
                   pl.BlockSpec((tm * SUB, LANES), lambda b, i: (b * (s // tm) + i, 0)),
                   pl.BlockSpec((1, tm, LANES), tile),
                   pl.BlockSpec((1, LANES), const)),
        scratch_shapes=[pltpu.VMEM((1, LANES), F32)],
        compiler_params=_cparams(("arbitrary", "arbitrary")),
        name="out_proj",
    )(x, o_a, o_b, gates, gates, mod3, mod3, mod3, norm2_g.reshape(1, d), wa, wb, wo, wr_hi, wr_lo, br, tril)


def _moe_dispatch_kernel(tm, n_tok, blk, dest_ref, zstart_ref, zcnt_ref, nused_ref, h_ref, xb_ref, smap_ref,
                         zbuf, stage, sem, zsem):
    i = pl.program_id(0)
    n_steps = pl.num_programs(0)
    n_slots = smap_ref.shape[0]
    n_experts = zcnt_ref.shape[0]

    @pl.when(i == 0)
    def _():
        def init(s_, carry):
            smap_ref[s_] = 0
            return carry
        lax.fori_loop(nused_ref[0] * blk, n_slots, init, 0)
        zbuf[...] = jnp.zeros_like(zbuf)

        def zero_row(e, j):
            dst = pl.multiple_of((zstart_ref[e] + j) * SUB, SUB)
            return pltpu.make_async_copy(zbuf.at[pl.ds(0, SUB), :], xb_ref.at[pl.ds(dst, SUB), :], zsem)

        def zero_block(b):
            dst = pl.multiple_of(b * (blk * SUB), blk * SUB)
            return pltpu.make_async_copy(zbuf, xb_ref.at[pl.ds(dst, blk * SUB), :], zsem)

        for e in range(n_experts):
            def zstart(j, carry, e=e):
                zero_row(e, j).start()
                smap_ref[zstart_ref[e] + j] = TOP_K * n_tok + e * blk + j
                return carry
            lax.fori_loop(0, zcnt_ref[e], zstart, 0)

        def bstart(b, carry):
            zero_block(b).start()
            return carry
        lax.fori_loop(nused_ref[0], n_slots // blk, bstart, 0)
        for e in range(n_experts):
            def zwait(j, carry, e=e):
                zero_row(e, j).wait()
                return carry
            lax.fori_loop(0, zcnt_ref[e], zwait, 0)

        def bwait(b, carry):
            zero_block(b).wait()
            return carry
        lax.fori_loop(nused_ref[0], n_slots // blk, bwait, 0)

    slot = i % 2
    stage[slot] = h_ref[...]
    for r in range(tm):
        tok = i * tm + r
        for k in range(TOP_K):
            d = dest_ref[tok * TOP_K + k]
            dst = pl.multiple_of(d * SUB, SUB)
            pltpu.make_async_copy(stage.at[slot, pl.ds(r * SUB, SUB), :], xb_ref.at[pl.ds(dst, SUB), :],
                                  sem.at[slot]).start(priority=k)
            smap_ref[d] = k * n_tok + tok

    def wait_step(s_):
        for _ in range(TOP_K):
            pltpu.make_async_copy(stage.at[s_], xb_ref.at[pl.ds(0, tm * SUB), :], sem.at[s_]).wait()

    @pl.when(i > 0)
    def _():
        wait_step(1 - slot)

    @pl.when(i == n_steps - 1)
    def _():
        wait_step(slot)


def _moe_dispatch(h2, dest, zstart, zcnt, n_used, n_slots, blk, tm):
    n_tok = h2.shape[0] // SUB
    grid_spec = pltpu.PrefetchScalarGridSpec(
        num_scalar_prefetch=4,
        grid=(n_tok // tm,),
        in_specs=[pl.BlockSpec((tm * SUB, LANES), lambda i, *_: (i, 0))],
        out_specs=(pl.BlockSpec(memory_space=pl.ANY),
                   pl.BlockSpec(memory_space=pltpu.SMEM)),
        scratch_shapes=[pltpu.VMEM((blk * SUB, LANES), U32),
                        pltpu.VMEM((2, tm * SUB, LANES), U32),
                        pltpu.SemaphoreType.DMA((2,)),
                        pltpu.SemaphoreType.DMA],
    )
    return pl.pallas_call(
        functools.partial(_moe_dispatch_kernel, tm, n_tok, blk),
        out_shape=(jax.ShapeDtypeStruct((n_slots * SUB, LANES), U32),
                   jax.ShapeDtypeStruct((n_slots,), jnp.int32)),
        grid_spec=grid_spec,
        compiler_params=_cparams(("arbitrary",)),
        name="moe_dispatch",
    )(dest, zstart, zcnt, n_used, h2)


def _moe_expert_kernel(n_tok, blk_e_ref, nused_ref, smap_ref, x_ref, wg_ref, wu_ref, wd_ref, y_ref,
                       ystage, wgb, wub, wdb, sem):
    i = pl.program_id(0)
    n_used = nused_ref[0]
    blk = x_ref.shape[0] // SUB
    first_real_blocks = TOP_K * n_tok // blk
    first_spare = y_ref.shape[0] // SUB - blk

    def issue(b, slot, rows):
        for r in rows:
            t = jnp.where(b >= 0, smap_ref[jnp.maximum(b, 0) * blk + r], first_spare + r)
            dst = pl.multiple_of(t * SUB, SUB)
            pltpu.make_async_copy(ystage.at[slot, pl.ds(r * SUB, SUB), :], y_ref.at[pl.ds(dst, SUB), :],
                                  sem.at[slot]).start(priority=r % 2)

    def wait_block(slot):
        pltpu.make_async_copy(ystage.at[slot], y_ref.at[pl.ds(0, blk * SUB), :], sem.at[slot]).wait()

    @pl.when(i == 0)
    def _():
        ystage[...] = jnp.zeros_like(ystage)
        n_spare_blocks = y_ref.shape[0] // (blk * SUB) - first_real_blocks

        def spare_copy(c):
            dst = (first_real_blocks + c) * blk * SUB
            return pltpu.make_async_copy(ystage.at[0], y_ref.at[pl.ds(dst, blk * SUB), :], sem.at[0])

        for c in range(n_spare_blocks):
            spare_copy(c).start()
        for c in range(n_spare_blocks):
            spare_copy(c).wait()

    new_expert = (i == 0) | (blk_e_ref[i] != blk_e_ref[jnp.maximum(i - 1, 0)])

    @pl.when((i < n_used) & new_expert)
    def _():
        wgb[...] = wg_ref[0].astype(BF16)
        wub[...] = wu_ref[0].astype(BF16)
        wdb[...] = wd_ref[0].astype(BF16)

    @pl.when(i < n_used)
    def _():
        slot = i % 2
        pslot = 1 - slot

        @pl.when(i > 0)
        def _():
            wait_block(slot)

        q = blk // 4
        xb = _tiles_to_rows(x_ref, blk).astype(BF16)
        issue(i - 1, pslot, range(0, q))
        hg = _dot(xb, wgb[...])
        issue(i - 1, pslot, range(q, 2 * q))
        hu = _dot(xb, wub[...])
        issue(i - 1, pslot, range(2 * q, 3 * q))
        hid = (_silu(hg) * hu).astype(BF16)
        y = _dot(hid, wdb[...])
        issue(i - 1, pslot, range(3 * q, blk))
        _rows_to_tiles(ystage.at[slot], y)

        @pl.when(i == n_used - 1)
        def _():
            issue(i, slot, range(blk))
            wait_block(pslot)
            wait_block(slot)


def _moe_experts(xb, smap, blk_e, n_used, w_gate, w_up, w_down, blk, n_tok):
    d = w_gate.shape[1]
    assert d == 2 * SUB * LANES and xb.shape[1] == LANES
    n_slots = smap.shape[0]
    nb = n_slots // blk
    f = w_gate.shape[-1]
    n_experts = w_gate.shape[0]
    assert (TOP_K * n_tok) % blk == 0
    n_tiles = TOP_K * n_tok + (n_experts + 1) * blk
    grid_spec = pltpu.PrefetchScalarGridSpec(
        num_scalar_prefetch=3,
        grid=(nb,),
        in_specs=[pl.BlockSpec((blk * SUB, LANES), lambda i, be, nu, sm: (jnp.minimum(i, nu[0] - 1), 0)),
                  pl.BlockSpec((1, d, f), lambda i, be, nu, sm: (be[i], 0, 0)),
                  pl.BlockSpec((1, d, f), lambda i, be, nu, sm: (be[i], 0, 0)),
                  pl.BlockSpec((1, f, d), lambda i, be, nu, sm: (be[i], 0, 0))],
        out_specs=pl.BlockSpec(memory_space=pl.ANY),
        scratch_shapes=[pltpu.VMEM((2, blk * SUB, LANES), U32),
                        pltpu.VMEM((d, f), BF16),
                        pltpu.VMEM((d, f), BF16),
                        pltpu.VMEM((f, d), BF16),
                        pltpu.SemaphoreType.DMA((2,))],
    )
    return pl.pallas_call(
        functools.partial(_moe_expert_kernel, n_tok),
        out_shape=jax.ShapeDtypeStruct((n_tiles * SUB, LANES), U32),
        grid_spec=grid_spec,
        compiler_params=_cparams(("arbitrary",)),
        name="moe_experts",
    )(blk_e, n_used, smap, xb, w_gate, w_up, w_down)


def _moe_combine_kernel(x1_ref, route_ref, gt2_ref, fg_ref, y1_ref, y2_ref, o_ref):
    tm = x1_ref.shape[0]
    route = route_ref[...]
    moe = route[:, 0:1] * _tiles_to_rows(y1_ref, tm) + route[:, 1:2] * _tiles_to_rows(y2_ref, tm)
    xo = x1_ref[...] + gt2_ref[0] * moe
    ms = jnp.mean(xo * xo, axis=-1, keepdims=True)
    o_ref[...] = xo * lax.rsqrt(ms + NORM_EPS) * fg_ref[...]


def _moe_combine(x1, route, yt, mod3, final_g, s, tm):
    n, d = x1.shape
    tiles_per_batch = s // tm
    n_steps = n // tm
    return pl.pallas_call(
        _moe_combine_kernel,
        out_shape=jax.ShapeDtypeStruct((n, d), F32),
        grid=(n_steps,),
        in_specs=[pl.BlockSpec((tm, d), lambda i: (i, 0)),
                  pl.BlockSpec((tm, LANES), lambda i: (i, 0)),
                  pl.BlockSpec((1, 1, d), lambda i: ((i // tiles_per_batch) * 6 + 5, 0, 0)),
                  pl.BlockSpec((1, d), lambda i: (0, 0)),
                  pl.BlockSpec((tm * SUB, LANES), lambda i: (i, 0)),
                  pl.BlockSpec((tm * SUB, LANES), lambda i: (n_steps + i, 0))],
        out_specs=pl.BlockSpec((tm, d), lambda i: (i, 0)),
        compiler_params=_cparams(("arbitrary",)),
        name="moe_combine",
    )(x1, route, mod3, final_g.reshape(1, d), yt, yt)


def _pick(n, candidates):
    for t in candidates:
        if n % t == 0:
            return t
    raise ValueError(f"no tile in {candidates} divides {n}")


def kernel(x, c, ada_w, ada_b, norm1_g, w_in, hg_lb, hg_norm_g, rw_mu, rw_w0, rw_w2, rw_a0, rw_a2, rw_g2, rw_kk, rw_ka, rw_rk, rw_gn_w, rw_gn_b, w_proj_a, w_proj_b, w_out, norm2_g, router_g_w, router_g_b, router_e_w, router_e_b, exp_w_gate, exp_w_up, exp_w_down, final_g):
    bsz, s, d = x.shape
    depth = ada_w.shape[0]
    hg_f = hg_lb.shape[-1]
    hg_w = hg_norm_g.shape[-1]
    rw_w = rw_w0.shape[-1]
    rw_cols = rw_mu.shape[-1]
    n_groups = router_g_w.shape[-1]
    n_experts = router_e_w.shape[-1]
    assert hg_f == hg_w and s % CHUNK == 0 and n_experts + n_groups <= LANES and d == 2 * SUB * LANES

    lb_all = jnp.cumsum(jax.nn.softmax(hg_lb.astype(F32), axis=0), axis=0)
    n = bsz * s
    blk = 256
    n_blocks = (n * TOP_K + n_experts * blk) // blk
    for l in range(depth):
        mod = _ada_mod(c, ada_w[l], ada_b[l])
        mod3 = mod.reshape(bsz * 6, 1, d)

        hg_cols = 2 * hg_f + 2 * hg_w
        rw_pad = -(-rw_cols // 256) * 256
        wl = w_in[l]
        w_hg = wl[:, :hg_cols].astype(BF16)
        w_rw = jnp.zeros((d, rw_pad), BF16).at[:, :rw_cols].set(wl[:, hg_cols:hg_cols + rw_cols].astype(BF16))
        w_gt = wl[:, hg_cols + rw_cols:].astype(BF16)
        hg, rw, gates = _in_proj(x, mod3, norm1_g[l], w_hg, w_rw, w_gt, _pick(s, (512, 256, 128, 64)))

        o_a = _hgrn2(hg, lb_all[l], hg_norm_g[l], _pick(s, (512, 256, 128, 64)))
        o_b = _rwkv7(rw, rw_mu[l], rw_w0[l], rw_w2[l], rw_a0[l], rw_a2[l], rw_g2[l],
                     rw_kk[l], rw_ka[l], rw_rk[l].reshape(-1), rw_gn_w[l], rw_gn_b[l],
                     _pick(s, (2 * CHUNK, CHUNK)))

        wr = jnp.zeros((d, LANES), F32).at[:, :n_experts].set(router_e_w[l])
        wr = wr.at[:, n_experts:n_experts + n_groups].set(router_g_w[l])
        br = jnp.zeros((1, LANES), F32).at[0, :n_experts].set(router_e_b[l])
        br = br.at[0, n_experts:n_experts + n_groups].set(router_g_b[l])
        x1, h2, route, counts = _out_proj(
            x, o_a, o_b, gates, mod3, norm2_g[l],
            w_proj_a[l].astype(BF16), w_proj_b[l].astype(BF16), w_out[l].astype(BF16),
            wr, br, n_groups, n_experts, _pick(s, (512, 256, 128, 64)))

        route2 = route.reshape(n, LANES)
        eid = route2[:, 2:4].astype(jnp.int32)
        rank = route2[:, 4:6].astype(jnp.int32)
        cnt = counts[0, :n_experts].astype(jnp.int32)
        padded = (cnt + blk - 1) // blk * blk
        pad_end = jnp.cumsum(padded)
        pad_start = pad_end - padded
        dest = pad_start[eid] + rank
        blk_start = jnp.arange(n_blocks, dtype=jnp.int32) * blk
        blk_e = jnp.minimum(jnp.sum((pad_end[None, :] <= blk_start[:, None]).astype(jnp.int32), axis=1),
                            n_experts - 1)
        n_used = (pad_end[-1:] // blk).astype(jnp.int32)
        tm = _pick(s, (256, 128, 64))
        xb, smap = _moe_dispatch(h2, dest.reshape(-1), pad_start + cnt, padded - cnt, n_used,
                                 n_blocks * blk, blk, tm)
        yt = _moe_experts(xb, smap, blk_e, n_used, exp_w_gate[l], exp_w_up[l], exp_w_down[l], blk, n)
        last = l == depth - 1
        assert last, "the final RMSNorm is fused into the last layer's combine"
        out = _moe_combine(x1.reshape(n, d), route2, yt, mod3, final_g, s, tm)
        x = out.reshape(bsz, s, d)
    return x
```

```python
import functools

import numpy as np
import jax
import jax.numpy as jnp
from jax import lax
from jax.experimental import pallas as pl
from jax.experimental.pallas import tpu as pltpu

F32 = jnp.float32
BF16 = jnp.bfloat16
HIGHEST = lax.Precision.HIGHEST

NORM_EPS = 1e-6
HG_HEAD = 128
RW_HEAD = 64
RW_GN_EPS = 64e-5
TOP_K = 2
CHUNK = 64
LANES = 128
SUB = 4
U32 = jnp.uint32
MXU_K = 256
SUBLANES = 8
VMEM_LIMIT = 56 * 1024 * 1024

NT = (((1,), (1,)), ((), ()))
TN = (((0,), (0,)), ((), ()))


def _dot(a, b, dims=None, precision=None):
    if dims is None:
        return jnp.dot(a, b, preferred_element_type=F32, precision=precision)
    return lax.dot_general(a, b, dims, preferred_element_type=F32, precision=precision)


def _split3(x):
    hi = x.astype(BF16)
    r1 = x - hi.astype(F32)
    mid = r1.astype(BF16)
    lo = (r1 - mid.astype(F32)).astype(BF16)
    return hi, mid, lo


def _dot_exact_lhs(m3_bf16, x):
    return _dot(m3_bf16, jnp.concatenate(_split3(x), axis=0))


def _head_sums(x, m2_bf16):
    outs = []
    for g0 in range(0, x.shape[1], MXU_K):
        xg = x[:, g0:g0 + MXU_K]
        hi = xg.astype(BF16)
        lo = (xg - hi.astype(F32)).astype(BF16)
        outs.append(_dot(jnp.concatenate([hi, lo], axis=1), m2_bf16))
    return jnp.concatenate(outs, axis=1)


def _sigmoid(x):
    return 1.0 / (1.0 + jnp.exp(-x))


def _silu(x):
    return x * _sigmoid(x)


def _rows_to_tiles(ref, val):
    m, half = val.shape[0], val.shape[1] // 2
    hi = lax.bitcast_convert_type(val[:, :half].astype(BF16).astype(F32), U32)
    lo = lax.bitcast_convert_type(val[:, half:].astype(BF16).astype(F32), U32)
    w = (hi & jnp.uint32(0xFFFF0000)) | (lo >> 16)
    for j in range(SUB):
        ref[pl.ds(j, m, stride=SUB), :] = w[:, j * LANES:(j + 1) * LANES]


def _tiles_to_rows(ref, m, base=0):
    w = jnp.concatenate([ref[pl.ds(base * SUB + j, m, stride=SUB), :] for j in range(SUB)], axis=1)
    hi = lax.bitcast_convert_type(w & jnp.uint32(0xFFFF0000), F32)
    lo = lax.bitcast_convert_type(w << 16, F32)
    return jnp.concatenate([hi, lo], axis=1)


def _cparams(sem):
    return pltpu.CompilerParams(dimension_semantics=sem, vmem_limit_bytes=VMEM_LIMIT)


def _ada_kernel(c_ref, w_ref, b_ref, o_ref):
    c = c_ref[...]
    o_ref[...] = _dot(_silu(c), w_ref[...], precision=HIGHEST) + b_ref[...]


def _ada_mod(c, w, b):
    bsz, d = c.shape
    n = w.shape[1]
    rows = 8
    cp = jnp.zeros((rows, d), F32).at[:bsz].set(c)
    tn = 1536
    out = pl.pallas_call(
        _ada_kernel,
        out_shape=jax.ShapeDtypeStruct((rows, n), F32),
        grid=(n // tn,),
        in_specs=[pl.BlockSpec((rows, d), lambda j: (0, 0)),
                  pl.BlockSpec((d, tn), lambda j: (0, j)),
                  pl.BlockSpec((1, tn), lambda j: (0, j))],
        out_specs=pl.BlockSpec((rows, tn), lambda j: (0, j)),
        compiler_params=_cparams(("arbitrary",)),
        name="ada_mod",
    )(cp, w, b.reshape(1, n))
    return out[:bsz]


def _in_proj_kernel(x_ref, sh_ref, sc_ref, g_ref, whg_ref, wrw_ref, wgt_ref, hg_ref, rw_ref, gt_ref):
    x = x_ref[0]
    ms = jnp.mean(x * x, axis=-1, keepdims=True)
    h = (x * lax.rsqrt(ms + NORM_EPS) * g_ref[...]) * (1.0 + sc_ref[0]) + sh_ref[0]
    hb = h.astype(BF16)
    step = 512
    for n0 in range(0, whg_ref.shape[1], step):
        hg_ref[0, :, n0:n0 + step] = _dot(hb, whg_ref[:, n0:n0 + step])
    for n0 in range(0, wrw_ref.shape[1], 256):
        rw_ref[0, :, n0:n0 + 256] = _dot(hb, wrw_ref[:, n0:n0 + 256])
    for n0 in range(0, wgt_ref.shape[1], step):
        gt_ref[0, :, n0:n0 + step] = _sigmoid(_dot(hb, wgt_ref[:, n0:n0 + step])).astype(BF16)


def _in_proj(x, mod3, norm_g, w_hg, w_rw, w_gt, tm):
    bsz, s, d = x.shape
    n_hg, n_rw, n_gt = w_hg.shape[1], w_rw.shape[1], w_gt.shape[1]
    const = lambda b, i: (0, 0)
    return pl.pallas_call(
        _in_proj_kernel,
        out_shape=(jax.ShapeDtypeStruct((bsz, s, n_hg), F32),
                   jax.ShapeDtypeStruct((bsz, s, n_rw), F32),
                   jax.ShapeDtypeStruct((bsz, s, n_gt), BF16)),
        grid=(bsz, s // tm),
        in_specs=[pl.BlockSpec((1, tm, d), lambda b, i: (b, i, 0)),
                  pl.BlockSpec((1, 1, d), lambda b, i: (b * 6 + 0, 0, 0)),
                  pl.BlockSpec((1, 1, d), lambda b, i: (b * 6 + 1, 0, 0)),
                  pl.BlockSpec((1, d), const),
                  pl.BlockSpec((d, n_hg), const),
                  pl.BlockSpec((d, n_rw), const),
                  pl.BlockSpec((d, n_gt), const)],
        out_specs=(pl.BlockSpec((1, tm, n_hg), lambda b, i: (b, i, 0)),
                   pl.BlockSpec((1, tm, n_rw), lambda b, i: (b, i, 0)),
                   pl.BlockSpec((1, tm, n_gt), lambda b, i: (b, i, 0))),
        compiler_params=_cparams(("arbitrary", "arbitrary")),
        name="in_proj",
    )(x, mod3, mod3, norm_g.reshape(1, d), w_hg, w_rw, w_gt)


_HG_LEVELS = (32, 16, 8, 4, 2, 1)


def _hgrn2_consts(width):
    c = CHUNK
    t = np.arange(c)[:, None]
    s = np.arange(c)[None, :]
    blocks = [(s <= t), (s > t)]
    lvl_masks = []
    right = []
    for h in _HG_LEVELS:
        m = (t // (2 * h)) * 2 * h + h
        is_r = (t & h) != 0
        blk = np.where(is_r, (s >= m) & (s <= t), (s > t) & (s <= m - 1))
        blocks.append(blk)
        lvl_masks.append(is_r & ((s & h) == 0) & ((t // (2 * h)) == (s // (2 * h))))
        right.append(np.broadcast_to(is_r, (c, width)))
    mst = np.tile(np.concatenate(blocks, axis=0).astype(np.float32), (1, 3))
    lm = np.stack([np.eye(c, dtype=bool)] + lvl_masks).astype(np.float32)
    rm = np.stack(right).astype(np.float32)
    return jnp.asarray(mst, BF16), jnp.asarray(lm, F32), jnp.asarray(rm, F32)


def _hgrn2_kernel(q_ref, f_ref, i_ref, g_ref, lb_ref, ng_ref, mst_ref, lm_ref, rm_ref, o_ref, st_ref):
    c = CHUNK
    n_heads = q_ref.shape[2] // HG_HEAD
    n_chunks = q_ref.shape[1] // c

    @pl.when(pl.program_id(1) == 0)
    def _():
        st_ref[...] = jnp.zeros_like(st_ref)

    mst = mst_ref[...]
    lb = lb_ref[...]
    ng = ng_ref[...]
    heads = [slice(hd * HG_HEAD, (hd + 1) * HG_HEAD) for hd in range(n_heads)]

    nsub = 2 if n_chunks % 2 == 0 else 1

    def chunk_body(ci, carry):
        r0 = pl.multiple_of(ci * (nsub * c), nsub * c)
        subs = []
        for j in range(nsub):
            rows = pl.ds(r0 + j * c, c)
            q = _silu(q_ref[0, rows, :])
            f = lb + (1.0 - lb) * _sigmoid(f_ref[0, rows, :])
            k = 1.0 - f
            ex = jnp.exp(_dot_exact_lhs(mst, jnp.log(f)))
            subs.append(dict(rows=rows, q=q, k=k, ex=ex, vb=i_ref[0, rows, :].astype(BF16),
                             qd=(q * ex[0:c]).astype(BF16), kr=(k * ex[c:2 * c]).astype(BF16)))
        for sb in subs:
            qb, kb = sb['q'].astype(BF16), sb['k'].astype(BF16)
            sb['sc'] = [lm_ref[0] * _dot(qb[:, ls], kb[:, ls], NT) for ls in heads]
            sb['dqk'] = sb['q'] - sb['k']
        for li in range(len(_HG_LEVELS)):
            for sb in subs:
                g_l = ((sb['k'] + rm_ref[li] * sb['dqk']) * sb['ex'][(2 + li) * c:(3 + li) * c]).astype(BF16)
                sb['sc'] = [s_h + lm_ref[li + 1] * _dot(g_l[:, ls], g_l[:, ls], NT)
                            for s_h, ls in zip(sb['sc'], heads)]
        for sb in subs:
            sb['kv'] = [_dot(sb['vb'][:, ls], sb['kr'][:, ls], TN) for ls in heads]
            sb['o'] = [_dot(s_h.astype(BF16), sb['vb'][:, ls]) for s_h, ls in zip(sb['sc'], heads)]
        sts = [st_ref[hd] for hd in range(n_heads)]
        for sb in subs:
            sb['o'] = [o_h + _dot(sb['qd'][:, ls], st.astype(BF16), NT) for o_h, ls, st in zip(sb['o'], heads, sts)]
            sts = [st * sb['ex'][c - 1:c, ls] + kv for st, ls, kv in zip(sts, heads, sb['kv'])]
        for hd in range(n_heads):
            st_ref[hd] = sts[hd]
        for sb in subs:
            on = [o_h * lax.rsqrt(jnp.mean(o_h * o_h, axis=-1, keepdims=True) + NORM_EPS) for o_h in sb['o']]
            o_full = jnp.concatenate(on, axis=1) * ng
            o_ref[0, sb['rows'], :] = (o_full * _silu(g_ref[0, sb['rows'], :])).astype(o_ref.dtype)
        return carry

    lax.fori_loop(0, n_chunks // nsub, chunk_body, 0)


def _hgrn2(hg, lb, norm_g, ts):
    bsz, s, n4 = hg.shape
    w = n4 // 4
    mst, lm, rm = _hgrn2_consts(w)
    n_heads = w // HG_HEAD
    const2 = lambda b, i: (0, 0)
    const3 = lambda b, i: (0, 0, 0)
    return pl.pallas_call(
        _hgrn2_kernel,
        out_shape=jax.ShapeDtypeStruct((bsz, s, w), BF16),
        grid=(bsz, s // ts),
        in_specs=[pl.BlockSpec((1, ts, w), lambda b, i: (b, i, 0)),
                  pl.BlockSpec((1, ts, w), lambda b, i: (b, i, 1)),
                  pl.BlockSpec((1, ts, w), lambda b, i: (b, i, 2)),
                  pl.BlockSpec((1, ts, w), lambda b, i: (b, i, 3)),
                  pl.BlockSpec((1, w), const2),
                  pl.BlockSpec((1, w), const2),
                  pl.BlockSpec(mst.shape, const2),
                  pl.BlockSpec(lm.shape, const3),
                  pl.BlockSpec(rm.shape, const3)],
        out_specs=pl.BlockSpec((1, ts, w), lambda b, i: (b, i, 0)),
        scratch_shapes=[pltpu.VMEM((n_heads, HG_HEAD, HG_HEAD), F32)],
        compiler_params=_cparams(("arbitrary", "arbitrary")),
        name="hgrn2",
    )(hg, hg, hg, hg, lb.reshape(1, w), norm_g.reshape(1, w), mst, lm, rm)


def _rwkv_consts(width):
    c = CHUNK
    t = np.arange(c)[:, None]
    s = np.arange(c)[None, :]
    tri = np.tile((s <= t).astype(np.float32), (1, 3))
    tt = np.arange(2 * c)[:, None]
    ss = np.arange(2 * c)[None, :]
    same = (tt // c) == (ss // c)
    strict = same & ((ss % c) < (tt % c))
    incl = same & ((ss % c) <= (tt % c))
    hsum = (np.arange(MXU_K)[:, None] // RW_HEAD) == (np.arange(MXU_K)[None, :] // RW_HEAD)
    hsum = np.tile(hsum, (2, 1))
    return (jnp.asarray(tri, BF16), jnp.asarray(strict.astype(np.float32), F32),
            jnp.asarray(incl.astype(np.float32), F32), jnp.asarray(hsum.astype(np.float32), BF16))


def _rwkv7_kernel(p_ref, mu_ref, w0_ref, a0_ref, kk_ref, ka_ref, rk_ref, gnw_ref, gnb_ref,
                  w2_ref, a2_ref, g2_ref, tri_ref, sm_ref, im_ref, hs_ref,
                  o_ref, carry_ref, zt_ref):
    c = CHUNK
    nb = p_ref.shape[0]
    nch = p_ref.shape[1] // c
    width = o_ref.shape[2]
    n_pairs = width // LANES

    @pl.when(pl.program_id(0) == 0)
    def _():
        carry_ref[...] = jnp.zeros_like(carry_ref)
        zt_ref[...] = jnp.zeros_like(zt_ref)

    hs = hs_ref[...]
    tri = tri_ref[...]
    smask = sm_ref[...] > 0
    imask = im_ref[...] > 0
    lane = lax.broadcasted_iota(jnp.int32, (c, LANES), 1)
    m0 = (lane < RW_HEAD).astype(F32)
    m1 = 1.0 - m0

    def stack(x):
        return jnp.concatenate([x * m0, x * m1], axis=0)

    xs_rows = []
    for b in range(nb):
        p = p_ref[b]
        row = lax.broadcasted_iota(jnp.int32, p.shape, 0)
        prev = jnp.where(row == 0, carry_ref[b], pltpu.roll(p, 1, 0))
        carry_ref[b] = p[nch * c - 1:nch * c, :]
        xs_rows.append(p + mu_ref[...] * (prev - p))
    xs = jnp.concatenate(xs_rows, axis=0)
    r_all = xs[:, 0:width]
    k_all = xs[:, width:2 * width]
    v_all = xs[:, 2 * width:3 * width]
    slab = xs[:, 3 * width:]
    nz = -(w0_ref[...] + _dot(jnp.tanh(slab).astype(BF16), w2_ref[...]))
    softplus = jnp.maximum(nz, 0.0) + jnp.log(1.0 + jnp.exp(-jnp.abs(nz)))
    ld_all = -jnp.exp(-softplus - 0.5)
    a_all = _sigmoid(a0_ref[...] + _dot(slab.astype(BF16), a2_ref[...]))
    g_all = _dot(_sigmoid(slab).astype(BF16), g2_ref[...])
    kk0 = k_all * kk_ref[...]
    kk_all = kk0 * lax.rsqrt(jnp.maximum(_head_sums(kk0 * kk0, hs), 1e-24))
    k2_all = k_all * (1.0 + (a_all - 1.0) * ka_ref[...])

    units = []
    for b, j in [(b, j) for b in range(nb) for j in range(nch)]:
        rb = slice((b * nch + j) * c, (b * nch + j + 1) * c)
        r, k2, v, ld = r_all[rb], k2_all[rb], v_all[rb], ld_all[rb]
        a_in = -kk_all[rb]
        b_in = kk_all[rb] * a_all[rb]
        cum = _dot_exact_lhs(tri, ld)
        cum_t = cum[c - 1:c, :]
        e_c = jnp.exp(cum)
        e_nc = jnp.exp(-cum)
        e_rem = jnp.exp(cum_t - cum)
        at_f = a_in * jnp.exp(cum - ld)
        rt_f = r * e_c
        kt_f = k2 * e_nc
        bt_f = b_in * e_nc
        kh_f = k2 * e_rem
        bh_f = b_in * e_rem
        p_t = jnp.exp(cum_t)
        for pi in range(n_pairs):
            ls = slice(pi * LANES, (pi + 1) * LANES)
            units.append(dict(
                b=b, j=j, pi=pi,
                at=stack(at_f[:, ls]).astype(BF16), rt=stack(rt_f[:, ls]).astype(BF16),
                kt=stack(kt_f[:, ls]).astype(BF16), bt=stack(bt_f[:, ls]).astype(BF16),
                kh=stack(kh_f[:, ls]).astype(BF16), bh=stack(bh_f[:, ls]).astype(BF16),
                vs=stack(v[:, ls]).astype(BF16), p_t=p_t[:, ls]))

    for u in units:
        lhs = jnp.concatenate([u['at'], u['rt']], axis=0)
        u['g'] = _dot(lhs, jnp.concatenate([u['kt'], u['bt']], axis=0), NT)
    for u in units:
        g = u.pop('g')
        u['a_ak'] = jnp.where(smask, g[:2 * c, :2 * c], 0.0).astype(BF16)
        u['pw'] = jnp.where(smask, g[:2 * c, 2 * c:], 0.0).astype(BF16)
        u['a_r'] = jnp.where(jnp.concatenate([imask, imask], axis=1), g[2 * c:], 0.0).astype(BF16)
    for u in units:
        akv = _dot(u.pop('a_ak'), u['vs'])
        u['x'] = jnp.concatenate([u['at'].astype(F32), akv], axis=1)
    n_lvl = int(np.log2(c))
    for lvl in range(n_lvl):
        for u in units:
            u['x'] = u['x'] + _dot(u['pw'], u['x'].astype(BF16))
        if lvl + 1 < n_lvl:
            for u in units:
                u['pw'] = _dot(u['pw'], u['pw']).astype(BF16)
    for u in units:
        x = u.pop('x')
        u['wr'] = jnp.concatenate([x[:, :LANES].astype(BF16), u['rt']], axis=0)
        u['u_loc'] = x[:, LANES:]
    zt = {(b, pi): zt_ref[b, pi] for b in range(nb) for pi in range(n_pairs)}
    for j in range(nch):
        tail = [u for u in units if u['j'] == j]
        for u in tail:
            u['uy'] = _dot(u.pop('wr'), zt[u['b'], u['pi']].astype(BF16), NT)
        for u in tail:
            uy = u.pop('uy')
            u['u'] = (uy[:2 * c] + u.pop('u_loc')).astype(BF16)
            u['y0'] = uy[2 * c:]
        for u in tail:
            vu = jnp.concatenate([u['vs'], u['u']], axis=0)
            u['y'] = u.pop('y0') + _dot(u['a_r'], vu)
            upd = _dot(vu, jnp.concatenate([u['kh'], u['bh']], axis=0), TN)
            zt[u['b'], u['pi']] = zt[u['b'], u['pi']] * u['p_t'] + upd
    for (b, pi), z in zt.items():
        zt_ref[b, pi] = z

    inv_n = 1.0 / RW_HEAD
    y = jnp.concatenate(
        [jnp.concatenate([u['y'][:c] + u['y'][c:] for u in units if (u['b'], u['j']) == (b, j)], axis=1)
         for b in range(nb) for j in range(nch)], axis=0)
    mean = _head_sums(y, hs) * inv_n
    d = y - mean
    var = _head_sums(d * d, hs) * inv_n
    yn = d * lax.rsqrt(var + RW_GN_EPS) * gnw_ref[...] + gnb_ref[...]
    bonus = _head_sums(r_all * k2_all * rk_ref[...], hs) * v_all
    out = ((yn + bonus) * g_all).astype(o_ref.dtype)
    for b in range(nb):
        o_ref[b] = out[b * nch * c:(b + 1) * nch * c]


def _rwkv7(rw, mu, w0, w2, a0, a2, g2, k_k, k_a, r_k, gn_w, gn_b, ts):
    bsz, s, cols = rw.shape
    width = w0.shape[-1]
    n_pairs = width // LANES
    slab = cols - 3 * width
    dl, al, gl = w2.shape[0], a2.shape[0], g2.shape[0]
    w2f = jnp.zeros((slab, width), F32).at[0:dl].set(w2).astype(BF16)
    a2f = jnp.zeros((slab, width), F32).at[dl:dl + al].set(a2).astype(BF16)
    g2f = jnp.zeros((slab, width), F32).at[dl + al:dl + al + gl].set(g2).astype(BF16)
    mup = jnp.zeros((1, cols), F32).at[0, :mu.shape[-1]].set(mu)
    tri, sm, im, hs = _rwkv_consts(width)
    row = lambda x: x.reshape(1, width)
    const = lambda i: (0, 0)
    vec = pl.BlockSpec((1, width), const)
    return pl.pallas_call(
        _rwkv7_kernel,
        out_shape=jax.ShapeDtypeStruct((bsz, s, width), BF16),
        grid=(s // ts,),
        in_specs=[pl.BlockSpec((bsz, ts, cols), lambda i: (0, i, 0)),
                  pl.BlockSpec((1, cols), const),
                  vec, vec, vec, vec, vec, vec, vec,
                  pl.BlockSpec((slab, width), const),
                  pl.BlockSpec((slab, width), const),
                  pl.BlockSpec((slab, width), const),
                  pl.BlockSpec(tri.shape, const),
                  pl.BlockSpec(sm.shape, const),
                  pl.BlockSpec(im.shape, const),
                  pl.BlockSpec(hs.shape, const)],
        out_specs=pl.BlockSpec((bsz, ts, width), lambda i: (0, i, 0)),
        scratch_shapes=[pltpu.VMEM((bsz, 1, cols), F32),
                        pltpu.VMEM((bsz, n_pairs, LANES, LANES), F32)],
        compiler_params=_cparams(("arbitrary",)),
        name="rwkv7",
    )(rw, mup, row(w0), row(a0), row(k_k), row(k_a), row(r_k), row(gn_w), row(gn_b),
      w2f, a2f, g2f, tri, sm, im, hs)


def _out_proj_kernel(n_groups, n_experts,
                     x_ref, oa_ref, ob_ref, ga_ref, gb_ref, gt1_ref, sc2_ref, sh2_ref, g2_ref,
                     wa_ref, wb_ref, wo_ref, wr_ref, wrl_ref, br_ref, tril_ref,
                     x1_ref, h2_ref, route_ref, routet_ref, cnt_ref, carry_ref):
    tm = x_ref.shape[1]
    first = (pl.program_id(0) == 0) & (pl.program_id(1) == 0)

    @pl.when(first)
    def _():
        carry_ref[...] = jnp.zeros_like(carry_ref)

    tp = tril_ref.shape[0]
    parts = [dict(rows=slice(r0, r0 + tp)) for r0 in range(0, tm, tp)]
    lane = lax.broadcasted_iota(jnp.int32, (tp, LANES), 1)
    neg = jnp.float32(-jnp.inf)
    big = jnp.int32(1 << 20)
    eg = n_experts // n_groups
    is_g = (lane >= n_experts) & (lane < n_experts + n_groups)
    for pt in parts:
        rows = pt['rows']
        pa = _dot(oa_ref[0, rows, :], wa_ref[...])
        pb = _dot(ob_ref[0, rows, :], wb_ref[...])
        mixed = ga_ref[0, rows, :].astype(F32) * pa + gb_ref[0, rows, :].astype(F32) * pb
        x1 = x_ref[0, rows, :] + gt1_ref[0] * _dot(mixed.astype(BF16), wo_ref[...])
        x1_ref[0, rows, :] = x1
        ms = jnp.mean(x1 * x1, axis=-1, keepdims=True)
        h2 = (x1 * lax.rsqrt(ms + NORM_EPS) * g2_ref[...]) * (1.0 + sc2_ref[0]) + sh2_ref[0]
        _rows_to_tiles(h2_ref.at[pl.ds(rows.start * SUB, tp * SUB), :], h2)
        h2_hi = h2.astype(BF16)
        h2_lo = (h2 - h2_hi.astype(F32)).astype(BF16)
        logits = (_dot(h2_hi, wr_ref[...]) + _dot(h2_lo, wr_ref[...]) + _dot(h2_hi, wrl_ref[...])
                  + br_ref[...])
        lg = jnp.where(is_g, logits, neg)
        mg = jnp.max(lg, axis=-1, keepdims=True)
        p_grp = 1.0 / jnp.sum(jnp.where(is_g, jnp.exp(lg - mg), 0.0), axis=-1, keepdims=True)
        gidx = jnp.min(jnp.where(lg == mg, lane, big), axis=-1, keepdims=True) - n_experts
        sel = (lane >= gidx * eg) & (lane < gidx * eg + eg)
        le = jnp.where(sel, logits, neg)
        me = jnp.max(le, axis=-1, keepdims=True)
        pe_un = jnp.where(sel, jnp.exp(le - me), 0.0)
        pe = jnp.where(sel, pe_un / jnp.sum(pe_un, axis=-1, keepdims=True), -1.0)
        v1 = jnp.max(pe, axis=-1, keepdims=True)
        i1 = jnp.min(jnp.where(pe == v1, lane, big), axis=-1, keepdims=True)
        pe2 = jnp.where(lane == i1, -1.0, pe)
        v2 = jnp.max(pe2, axis=-1, keepdims=True)
        i2 = jnp.min(jnp.where(pe2 == v2, lane, big), axis=-1, keepdims=True)
        wsum = v1 + v2
        pt.update(w1=p_grp * v1 / wsum, w2=p_grp * v2 / wsum, i1=i1, i2=i2,
                  oh1=(lane == i1).astype(F32), oh2=(lane == i2).astype(F32))

    carry = carry_ref[...]
    for pt in parts:
        both = pt['oh1'] + pt['oh2']
        before = _dot(tril_ref[...], both.astype(BF16)) + carry
        rank1 = jnp.sum(pt['oh1'] * before, axis=-1, keepdims=True)
        rank2 = jnp.sum(pt['oh2'] * before, axis=-1, keepdims=True)
        carry = carry + jnp.sum(both, axis=0, keepdims=True)
        out = jnp.where(lane == 0, pt['w1'], 0.0)
        out = jnp.where(lane == 1, pt['w2'], out)
        out = jnp.where(lane == 2, pt['i1'].astype(F32), out)
        out = jnp.where(lane == 3, pt['i2'].astype(F32), out)
        out = jnp.where(lane == 4, rank1, out)
        out = jnp.where(lane == 5, rank2, out)
        route_ref[0, pt['rows'], :] = out
        routet_ref[:, pt['rows']] = out.T[0:SUBLANES, :]
    carry_ref[...] = carry
    cnt_ref[...] = carry


def _out_proj(x, o_a, o_b, gates, mod3, norm2_g, wa, wb, wo, wr, br, n_groups, n_experts, tm):
    bsz, s, d = x.shape
    wdt = o_a.shape[-1]
    tp = tm
    tril = jnp.asarray(np.tril(np.ones((tp, tp), np.float32), -1), BF16)
    wr_hi = wr.astype(BF16)
    wr_lo = (wr - wr_hi.astype(F32)).astype(BF16)
    const = lambda b, i: (0, 0)
    tile = lambda b, i: (b, i, 0)
    kern = functools.partial(_out_proj_kernel, n_groups, n_experts)
    return pl.pallas_call(
        kern,
        out_shape=(jax.ShapeDtypeStruct((bsz, s, d), F32),
                   jax.ShapeDtypeStruct((bsz * s * SUB, LANES), U32),
                   jax.ShapeDtypeStruct((bsz, s, LANES), F32),
                   jax.ShapeDtypeStruct((bsz * (s // tm) * SUBLANES, tm), F32),
                   jax.ShapeDtypeStruct((1, LANES), F32)),
        grid=(bsz, s // tm),
        in_specs=[pl.BlockSpec((1, tm, d), tile),
                  pl.BlockSpec((1, tm, wdt), tile),
                  pl.BlockSpec((1, tm, wdt), tile),
                  pl.BlockSpec((1, tm, d), lambda b, i: (b, i, 0)),
                  pl.BlockSpec((1, tm, d), lambda b, i: (b, i, 1)),
                  pl.BlockSpec((1, 1, d), lambda b, i: (b * 6 + 2, 0, 0)),
                  pl.BlockSpec((1, 1, d), lambda b, i: (b * 6 + 4, 0, 0)),
                  pl.BlockSpec((1, 1, d), lambda b, i: (b * 6 + 3, 0, 0)),
                  pl.BlockSpec((1, d), const),
                  pl.BlockSpec(wa.shape, const),
                  pl.BlockSpec(wb.shape, const),
                  pl.BlockSpec(wo.shape, const),
                  pl.BlockSpec(wr.shape, const),
                  pl.BlockSpec(wr.shape, const),
                  pl.BlockSpec((1, LANES), const),
                  pl.BlockSpec((tp, tp), const)],
        out_specs=(pl.BlockSpec((1, tm, d), tile),
                   pl.BlockSpec((tm * SUB, LANES), lambda b, i: (b * (s // tm) + i, 0)),
                   pl.BlockSpec((1, tm, LANES), tile),
                   pl.BlockSpec((SUBLANES, tm), lambda b, i: (b * (s // tm) + i, 0)),
                   pl.BlockSpec((1, LANES), const)),
        scratch_shapes=[pltpu.VMEM((1, LANES), F32)],
        compiler_params=_cparams(("arbitrary", "arbitrary")),
        name="out_proj",
    )(x, o_a, o_b, gates, gates, mod3, mod3, mod3, norm2_g.reshape(1, d), wa, wb, wo, wr_hi, wr_lo, br, tril)


def _moe_dispatch_kernel(tm, tm_o, n_tok, blk, dest_ref, zstart_ref, zcnt_ref, nused_ref, h_ref, xb_ref, smap_ref,
                         zbuf, stage, sem, zsem):
    i = pl.program_id(0)
    n_steps = pl.num_programs(0)
    n_slots = smap_ref.shape[0]
    n_experts = zcnt_ref.shape[0]

    @pl.when(i == 0)
    def _():
        def init(s_, carry):
            smap_ref[s_] = 0
            return carry
        lax.fori_loop(nused_ref[0] * blk, n_slots, init, 0)
        zbuf[...] = jnp.zeros_like(zbuf)

        def zero_row(e, j):
            dst = pl.multiple_of((zstart_ref[e] + j) * SUB, SUB)
            return pltpu.make_async_copy(zbuf.at[pl.ds(0, SUB), :], xb_ref.at[pl.ds(dst, SUB), :], zsem)

        def zero_block(b):
            dst = pl.multiple_of(b * (blk * SUB), blk * SUB)
            return pltpu.make_async_copy(zbuf, xb_ref.at[pl.ds(dst, blk * SUB), :], zsem)

        for e in range(n_experts):
            def zstart(j, carry, e=e):
                zero_row(e, j).start()
                smap_ref[zstart_ref[e] + j] = TOP_K * n_tok + e * blk + j
                return carry
            lax.fori_loop(0, zcnt_ref[e], zstart, 0)

        def bstart(b, carry):
            zero_block(b).start()
            return carry
        lax.fori_loop(nused_ref[0], n_slots // blk, bstart, 0)
        for e in range(n_experts):
            def zwait(j, carry, e=e):
                zero_row(e, j).wait()
                return carry
            lax.fori_loop(0, zcnt_ref[e], zwait, 0)

        def bwait(b, carry):
            zero_block(b).wait()
            return carry
        lax.fori_loop(nused_ref[0], n_slots // blk, bwait, 0)

    slot = i % 2
    stage[slot] = h_ref[...]
    tok0 = i * tm
    dest0 = (tok0 // tm_o) * (TOP_K * tm_o) + tok0 % tm_o
    for r in range(tm):
        tok = tok0 + r
        for k in range(TOP_K):
            d = dest_ref[dest0 + k * tm_o + r]
            dst = pl.multiple_of(d * SUB, SUB)
            pltpu.make_async_copy(stage.at[slot, pl.ds(r * SUB, SUB), :], xb_ref.at[pl.ds(dst, SUB), :],
                                  sem.at[slot]).start(priority=k)
            smap_ref[d] = k * n_tok + tok

    def wait_step(s_):
        for _ in range(TOP_K):
            pltpu.make_async_copy(stage.at[s_], xb_ref.at[pl.ds(0, tm * SUB), :], sem.at[s_]).wait()

    @pl.when(i > 0)
    def _():
        wait_step(1 - slot)

    @pl.when(i == n_steps - 1)
    def _():
        wait_step(slot)


def _moe_dispatch(h2, dest, zstart, zcnt, n_used, n_slots, blk, tm, tm_o):
    n_tok = h2.shape[0] // SUB
    grid_spec = pltpu.PrefetchScalarGridSpec(
        num_scalar_prefetch=4,
        grid=(n_tok // tm,),
        in_specs=[pl.BlockSpec((tm * SUB, LANES), lambda i, *_: (i, 0))],
        out_specs=(pl.BlockSpec(memory_space=pl.ANY),
                   pl.BlockSpec(memory_space=pltpu.SMEM)),
        scratch_shapes=[pltpu.VMEM((blk * SUB, LANES), U32),
                        pltpu.VMEM((2, tm * SUB, LANES), U32),
                        pltpu.SemaphoreType.DMA((2,)),
                        pltpu.SemaphoreType.DMA],
    )
    return pl.pallas_call(
        functools.partial(_moe_dispatch_kernel, tm, tm_o, n_tok, blk),
        out_shape=(jax.ShapeDtypeStruct((n_slots * SUB, LANES), U32),
                   jax.ShapeDtypeStruct((n_slots,), jnp.int32)),
        grid_spec=grid_spec,
        compiler_params=_cparams(("arbitrary",)),
        name="moe_dispatch",
    )(dest, zstart, zcnt, n_used, h2)


def _moe_expert_kernel(n_tok, blk_e_ref, nused_ref, smap_ref, x_ref, wg_ref, wu_ref, wd_ref, y_ref,
                       ystage, wgb, wub, wdb, sem):
    i = pl.program_id(0)
    n_used = nused_ref[0]
    blk = x_ref.shape[0] // SUB
    first_real_blocks = TOP_K * n_tok // blk
    first_spare = y_ref.shape[0] // SUB - blk

    def issue(b, slot, rows):
        for r in rows:
            t = jnp.where(b >= 0, smap_ref[jnp.maximum(b, 0) * blk + r], first_spare + r)
            dst = pl.multiple_of(t * SUB, SUB)
            pltpu.make_async_copy(ystage.at[slot, pl.ds(r * SUB, SUB), :], y_ref.at[pl.ds(dst, SUB), :],
                                  sem.at[slot]).start(priority=r % 2)

    def wait_block(slot):
        pltpu.make_async_copy(ystage.at[slot], y_ref.at[pl.ds(0, blk * SUB), :], sem.at[slot]).wait()

    @pl.when(i == 0)
    def _():
        ystage[...] = jnp.zeros_like(ystage)
        n_spare_blocks = y_ref.shape[0] // (blk * SUB) - first_real_blocks

        def spare_copy(c):
            dst = (first_real_blocks + c) * blk * SUB
            return pltpu.make_async_copy(ystage.at[0], y_ref.at[pl.ds(dst, blk * SUB), :], sem.at[0])

        for c in range(n_spare_blocks):
            spare_copy(c).start()
        for c in range(n_spare_blocks):
            spare_copy(c).wait()

    new_expert = (i == 0) | (blk_e_ref[i] != blk_e_ref[jnp.maximum(i - 1, 0)])

    @pl.when((i < n_used) & new_expert)
    def _():
        wgb[...] = wg_ref[0].astype(BF16)
        wub[...] = wu_ref[0].astype(BF16)
        wdb[...] = wd_ref[0].astype(BF16)

    @pl.when(i < n_used)
    def _():
        slot = i % 2
        pslot = 1 - slot

        @pl.when(i > 0)
        def _():
            wait_block(slot)

        q = blk // 4
        xb = _tiles_to_rows(x_ref, blk).astype(BF16)
        issue(i - 1, pslot, range(0, q))
        hg = _dot(xb, wgb[...])
        issue(i - 1, pslot, range(q, 2 * q))
        hu = _dot(xb, wub[...])
        issue(i - 1, pslot, range(2 * q, 3 * q))
        hid = (_silu(hg) * hu).astype(BF16)
        y = _dot(hid, wdb[...])
        issue(i - 1, pslot, range(3 * q, blk))
        _rows_to_tiles(ystage.at[slot], y)

        @pl.when(i == n_used - 1)
        def _():
            issue(i, slot, range(blk))
            wait_block(pslot)
            wait_block(slot)


def _moe_experts(xb, smap, blk_e, n_used, w_gate, w_up, w_down, blk, n_tok):
    d = w_gate.shape[1]
    assert d == 2 * SUB * LANES and xb.shape[1] == LANES
    n_slots = smap.shape[0]
    nb = n_slots // blk
    f = w_gate.shape[-1]
    n_experts = w_gate.shape[0]
    assert (TOP_K * n_tok) % blk == 0
    n_tiles = TOP_K * n_tok + (n_experts + 1) * blk
    grid_spec = pltpu.PrefetchScalarGridSpec(
        num_scalar_prefetch=3,
        grid=(nb,),
        in_specs=[pl.BlockSpec((blk * SUB, LANES), lambda i, be, nu, sm: (jnp.minimum(i, nu[0] - 1), 0)),
                  pl.BlockSpec((1, d, f), lambda i, be, nu, sm: (be[i], 0, 0)),
                  pl.BlockSpec((1, d, f), lambda i, be, nu, sm: (be[i], 0, 0)),
                  pl.BlockSpec((1, f, d), lambda i, be, nu, sm: (be[i], 0, 0))],
        out_specs=pl.BlockSpec(memory_space=pl.ANY),
        scratch_shapes=[pltpu.VMEM((2, blk * SUB, LANES), U32),
                        pltpu.VMEM((d, f), BF16),
                        pltpu.VMEM((d, f), BF16),
                        pltpu.VMEM((f, d), BF16),
                        pltpu.SemaphoreType.DMA((2,))],
    )
    return pl.pallas_call(
        functools.partial(_moe_expert_kernel, n_tok),
        out_shape=jax.ShapeDtypeStruct((n_tiles * SUB, LANES), U32),
        grid_spec=grid_spec,
        compiler_params=_cparams(("arbitrary",)),
        name="moe_experts",
    )(blk_e, n_used, smap, xb, w_gate, w_up, w_down)


def _moe_combine_kernel(x1_ref, route_ref, gt2_ref, fg_ref, y1_ref, y2_ref, o_ref):
    tm = x1_ref.shape[0]
    route = route_ref[...]
    moe = route[:, 0:1] * _tiles_to_rows(y1_ref, tm) + route[:, 1:2] * _tiles_to_rows(y2_ref, tm)
    xo = x1_ref[...] + gt2_ref[0] * moe
    ms = jnp.mean(xo * xo, axis=-1, keepdims=True)
    o_ref[...] = xo * lax.rsqrt(ms + NORM_EPS) * fg_ref[...]


def _moe_combine(x1, route, yt, mod3, final_g, s, tm):
    n, d = x1.shape
    tiles_per_batch = s // tm
    n_steps = n // tm
    return pl.pallas_call(
        _moe_combine_kernel,
        out_shape=jax.ShapeDtypeStruct((n, d), F32),
        grid=(n_steps,),
        in_specs=[pl.BlockSpec((tm, d), lambda i: (i, 0)),
                  pl.BlockSpec((tm, LANES), lambda i: (i, 0)),
                  pl.BlockSpec((1, 1, d), lambda i: ((i // tiles_per_batch) * 6 + 5, 0, 0)),
                  pl.BlockSpec((1, d), lambda i: (0, 0)),
                  pl.BlockSpec((tm * SUB, LANES), lambda i: (i, 0)),
                  pl.BlockSpec((tm * SUB, LANES), lambda i: (n_steps + i, 0))],
        out_specs=pl.BlockSpec((tm, d), lambda i: (i, 0)),
        compiler_params=_cparams(("arbitrary",)),
        name="moe_combine",
    )(x1, route, mod3, final_g.reshape(1, d), yt, yt)


def _pick(n, candidates):
    for t in candidates:
        if n % t == 0:
            return t
    raise ValueError(f"no tile in {candidates} divides {n}")


def kernel(x, c, ada_w, ada_b, norm1_g, w_in, hg_lb, hg_norm_g, rw_mu, rw_w0, rw_w2, rw_a0, rw_a2, rw_g2, rw_kk, rw_ka, rw_rk, rw_gn_w, rw_gn_b, w_proj_a, w_proj_b, w_out, norm2_g, router_g_w, router_g_b, router_e_w, router_e_b, exp_w_gate, exp_w_up, exp_w_down, final_g):
    bsz, s, d = x.shape
    depth = ada_w.shape[0]
    hg_f = hg_lb.shape[-1]
    hg_w = hg_norm_g.shape[-1]
    rw_w = rw_w0.shape[-1]
    rw_cols = rw_mu.shape[-1]
    n_groups = router_g_w.shape[-1]
    n_experts = router_e_w.shape[-1]
    assert hg_f == hg_w and s % CHUNK == 0 and n_experts + n_groups <= LANES and d == 2 * SUB * LANES

    lb_all = jnp.cumsum(jax.nn.softmax(hg_lb.astype(F32), axis=0), axis=0)
    n = bsz * s
    blk = 256
    n_blocks = (n * TOP_K + n_experts * blk) // blk
    for l in range(depth):
        mod = _ada_mod(c, ada_w[l], ada_b[l])
        mod3 = mod.reshape(bsz * 6, 1, d)

        hg_cols = 2 * hg_f + 2 * hg_w
        rw_pad = -(-rw_cols // 256) * 256
        wl = w_in[l]
        w_hg = wl[:, :hg_cols].astype(BF16)
        w_rw = jnp.zeros((d, rw_pad), BF16).at[:, :rw_cols].set(wl[:, hg_cols:hg_cols + rw_cols].astype(BF16))
        w_gt = wl[:, hg_cols + rw_cols:].astype(BF16)
        hg, rw, gates = _in_proj(x, mod3, norm1_g[l], w_hg, w_rw, w_gt, _pick(s, (512, 256, 128, 64)))

        o_a = _hgrn2(hg, lb_all[l], hg_norm_g[l], _pick(s, (512, 256, 128, 64)))
        o_b = _rwkv7(rw, rw_mu[l], rw_w0[l], rw_w2[l], rw_a0[l], rw_a2[l], rw_g2[l],
                     rw_kk[l], rw_ka[l], rw_rk[l].reshape(-1), rw_gn_w[l], rw_gn_b[l],
                     _pick(s, (2 * CHUNK, CHUNK)))

        wr = jnp.zeros((d, LANES), F32).at[:, :n_experts].set(router_e_w[l])
        wr = wr.at[:, n_experts:n_experts + n_groups].set(router_g_w[l])
        br = jnp.zeros((1, LANES), F32).at[0, :n_experts].set(router_e_b[l])
        br = br.at[0, n_experts:n_experts + n_groups].set(router_g_b[l])
        tm_o = _pick(s, (512, 256, 128, 64))
        x1, h2, route, route_t, counts = _out_proj(
            x, o_a, o_b, gates, mod3, norm2_g[l],
            w_proj_a[l].astype(BF16), w_proj_b[l].astype(BF16), w_out[l].astype(BF16),
            wr, br, n_groups, n_experts, tm_o)

        route2 = route.reshape(n, LANES)
        rt = route_t.reshape(n // tm_o, SUBLANES, tm_o)
        eid = rt[:, 2:4, :].astype(jnp.int32)
        rank = rt[:, 4:6, :].astype(jnp.int32)
        cnt = counts[0, :n_experts].astype(jnp.int32)
        padded = (cnt + blk - 1) // blk * blk
        pad_end = jnp.cumsum(padded)
        pad_start = pad_end - padded
        dest = pad_start[eid] + rank
        blk_start = jnp.arange(n_blocks, dtype=jnp.int32) * blk
        blk_e = jnp.minimum(jnp.sum((pad_end[None, :] <= blk_start[:, None]).astype(jnp.int32), axis=1),
                            n_experts - 1)
        n_used = (pad_end[-1:] // blk).astype(jnp.int32)
        tm = _pick(s, (256, 128, 64))
        xb, smap = _moe_dispatch(h2, dest.reshape(-1), pad_start + cnt, padded - cnt, n_used,
                                 n_blocks * blk, blk, tm, tm_o)
        yt = _moe_experts(xb, smap, blk_e, n_used, exp_w_gate[l], exp_w_up[l], exp_w_down[l], blk, n)
        last = l == depth - 1
        assert last, "the final RMSNorm is fused into the last layer's combine"
        out = _moe_combine(x1.reshape(n, d), route2, yt, mod3, final_g, s, tm)
        x = out.reshape(bsz, s, d)
    return x
```

```python
import functools

import numpy as np
import jax
import jax.numpy as jnp
from jax import lax
from jax.experimental import pallas as pl
from jax.experimental.pallas import tpu as pltpu

F32 = jnp.float32
BF16 = jnp.bfloat16
HIGHEST = lax.Precision.HIGHEST

NORM_EPS = 1e-6
HG_HEAD = 128
RW_HEAD = 64
RW_GN_EPS = 64e-5
TOP_K = 2
CHUNK = 64
LANES = 128
SUB = 4
U32 = jnp.uint32
MXU_K = 256
SUBLANES = 8
VMEM_LIMIT = 56 * 1024 * 1024

NT = (((1,), (1,)), ((), ()))
TN = (((0,), (0,)), ((), ()))


def _dot(a, b, dims=None, precision=None):
    if dims is None:
        return jnp.dot(a, b, preferred_element_type=F32, precision=precision)
    return lax.dot_general(a, b, dims, preferred_element_type=F32, precision=precision)


def _split3(x):
    hi = x.astype(BF16)
    r1 = x - hi.astype(F32)
    mid = r1.astype(BF16)
    lo = (r1 - mid.astype(F32)).astype(BF16)
    return hi, mid, lo


def _dot_exact_lhs(m3_bf16, x):
    return _dot(m3_bf16, jnp.concatenate(_split3(x), axis=0))


def _head_sums(x, m2_bf16):
    outs = []
    for g0 in range(0, x.shape[1], MXU_K):
        xg = x[:, g0:g0 + MXU_K]
        hi = xg.astype(BF16)
        lo = (xg - hi.astype(F32)).astype(BF16)
        outs.append(_dot(jnp.concatenate([hi, lo], axis=1), m2_bf16))
    return jnp.concatenate(outs, axis=1)


def _sigmoid(x):
    return 1.0 / (1.0 + jnp.exp(-x))


def _silu(x):
    return x * _sigmoid(x)


def _rows_to_tiles(ref, val):
    m, half = val.shape[0], val.shape[1] // 2
    hi = lax.bitcast_convert_type(val[:, :half].astype(BF16).astype(F32), U32)
    lo = lax.bitcast_convert_type(val[:, half:].astype(BF16).astype(F32), U32)
    w = (hi & jnp.uint32(0xFFFF0000)) | (lo >> 16)
    for j in range(SUB):
        ref[pl.ds(j, m, stride=SUB), :] = w[:, j * LANES:(j + 1) * LANES]


def _tiles_to_rows(ref, m, base=0):
    w = jnp.concatenate([ref[pl.ds(base * SUB + j, m, stride=SUB), :] for j in range(SUB)], axis=1)
    hi = lax.bitcast_convert_type(w & jnp.uint32(0xFFFF0000), F32)
    lo = lax.bitcast_convert_type(w << 16, F32)
    return jnp.concatenate([hi, lo], axis=1)


def _cparams(sem):
    return pltpu.CompilerParams(dimension_semantics=sem, vmem_limit_bytes=VMEM_LIMIT)


def _ada_kernel(c_ref, w_ref, b_ref, o_ref):
    c = c_ref[...]
    o_ref[...] = _dot(_silu(c), w_ref[...], precision=HIGHEST) + b_ref[...]


def _ada_mod(c, w, b):
    bsz, d = c.shape
    n = w.shape[1]
    rows = 8
    cp = jnp.zeros((rows, d), F32).at[:bsz].set(c)
    tn = 1536
    out = pl.pallas_call(
        _ada_kernel,
        out_shape=jax.ShapeDtypeStruct((rows, n), F32),
        grid=(n // tn,),
        in_specs=[pl.BlockSpec((rows, d), lambda j: (0, 0)),
                  pl.BlockSpec((d, tn), lambda j: (0, j)),
                  pl.BlockSpec((1, tn), lambda j: (0, j))],
        out_specs=pl.BlockSpec((rows, tn), lambda j: (0, j)),
        compiler_params=_cparams(("arbitrary",)),
        name="ada_mod",
    )(cp, w, b.reshape(1, n))
    return out[:bsz]


def _in_proj_kernel(x_ref, sh_ref, sc_ref, g_ref, whg_ref, wrw_ref, wgt_ref, hg_ref, rw_ref, gt_ref):
    x = x_ref[0]
    ms = jnp.mean(x * x, axis=-1, keepdims=True)
    h = (x * lax.rsqrt(ms + NORM_EPS) * g_ref[...]) * (1.0 + sc_ref[0]) + sh_ref[0]
    hb = h.astype(BF16)
    step = 512
    for n0 in range(0, whg_ref.shape[1], step):
        hg_ref[0, :, n0:n0 + step] = _dot(hb, whg_ref[:, n0:n0 + step])
    for n0 in range(0, wrw_ref.shape[1], 256):
        rw_ref[0, :, n0:n0 + 256] = _dot(hb, wrw_ref[:, n0:n0 + 256])
    for n0 in range(0, wgt_ref.shape[1], step):
        gt_ref[0, :, n0:n0 + step] = _sigmoid(_dot(hb, wgt_ref[:, n0:n0 + step])).astype(BF16)


def _in_proj(x, mod3, norm_g, w_hg, w_rw, w_gt, tm):
    bsz, s, d = x.shape
    n_hg, n_rw, n_gt = w_hg.shape[1], w_rw.shape[1], w_gt.shape[1]
    const = lambda b, i: (0, 0)
    return pl.pallas_call(
        _in_proj_kernel,
        out_shape=(jax.ShapeDtypeStruct((bsz, s, n_hg), F32),
                   jax.ShapeDtypeStruct((bsz, s, n_rw), F32),
                   jax.ShapeDtypeStruct((bsz, s, n_gt), BF16)),
        grid=(bsz, s // tm),
        in_specs=[pl.BlockSpec((1, tm, d), lambda b, i: (b, i, 0)),
                  pl.BlockSpec((1, 1, d), lambda b, i: (b * 6 + 0, 0, 0)),
                  pl.BlockSpec((1, 1, d), lambda b, i: (b * 6 + 1, 0, 0)),
                  pl.BlockSpec((1, d), const),
                  pl.BlockSpec((d, n_hg), const),
                  pl.BlockSpec((d, n_rw), const),
                  pl.BlockSpec((d, n_gt), const)],
        out_specs=(pl.BlockSpec((1, tm, n_hg), lambda b, i: (b, i, 0)),
                   pl.BlockSpec((1, tm, n_rw), lambda b, i: (b, i, 0)),
                   pl.BlockSpec((1, tm, n_gt), lambda b, i: (b, i, 0))),
        compiler_params=_cparams(("arbitrary", "arbitrary")),
        name="in_proj",
    )(x, mod3, mod3, norm_g.reshape(1, d), w_hg, w_rw, w_gt)


_HG_LEVELS = (32, 16, 8, 4, 2, 1)


def _hgrn2_consts(width):
    c = CHUNK
    t = np.arange(c)[:, None]
    s = np.arange(c)[None, :]
    blocks = [(s <= t), (s > t)]
    lvl_masks = []
    right = []
    for h in _HG_LEVELS:
        m = (t // (2 * h)) * 2 * h + h
        is_r = (t & h) != 0
        blk = np.where(is_r, (s >= m) & (s <= t), (s > t) & (s <= m - 1))
        blocks.append(blk)
        lvl_masks.append(is_r & ((s & h) == 0) & ((t // (2 * h)) == (s // (2 * h))))
        right.append(np.broadcast_to(is_r, (c, width)))
    mst = np.tile(np.concatenate(blocks, axis=0).astype(np.float32), (1, 3))
    lm = np.stack([np.eye(c, dtype=bool)] + lvl_masks).astype(np.float32)
    rm = np.stack(right).astype(np.float32)
    return jnp.asarray(mst, BF16), jnp.asarray(lm, F32), jnp.asarray(rm, F32)


def _hgrn2_kernel(q_ref, f_ref, i_ref, g_ref, lb_ref, ng_ref, mst_ref, lm_ref, rm_ref, o_ref, st_ref):
    c = CHUNK
    n_heads = q_ref.shape[2] // HG_HEAD
    n_chunks = q_ref.shape[1] // c

    @pl.when(pl.program_id(1) == 0)
    def _():
        st_ref[...] = jnp.zeros_like(st_ref)

    mst = mst_ref[...]
    lb = lb_ref[...]
    ng = ng_ref[...]
    heads = [slice(hd * HG_HEAD, (hd + 1) * HG_HEAD) for hd in range(n_heads)]

    nsub = 2 if n_chunks % 2 == 0 else 1

    def chunk_body(ci, carry):
        r0 = pl.multiple_of(ci * (nsub * c), nsub * c)
        subs = []
        for j in range(nsub):
            rows = pl.ds(r0 + j * c, c)
            q = _silu(q_ref[0, rows, :])
            f = lb + (1.0 - lb) * _sigmoid(f_ref[0, rows, :])
            k = 1.0 - f
            ex = jnp.exp(_dot_exact_lhs(mst, jnp.log(f)))
            subs.append(dict(rows=rows, q=q, k=k, ex=ex, vb=i_ref[0, rows, :].astype(BF16),
                             qd=(q * ex[0:c]).astype(BF16), kr=(k * ex[c:2 * c]).astype(BF16)))
        for sb in subs:
            qb, kb = sb['q'].astype(BF16), sb['k'].astype(BF16)
            sb['sc'] = [lm_ref[0] * _dot(qb[:, ls], kb[:, ls], NT) for ls in heads]
            sb['dqk'] = sb['q'] - sb['k']
        for li in range(len(_HG_LEVELS)):
            for sb in subs:
                g_l = ((sb['k'] + rm_ref[li] * sb['dqk']) * sb['ex'][(2 + li) * c:(3 + li) * c]).astype(BF16)
                sb['sc'] = [s_h + lm_ref[li + 1] * _dot(g_l[:, ls], g_l[:, ls], NT)
                            for s_h, ls in zip(sb['sc'], heads)]
        for sb in subs:
            sb['kv'] = [_dot(sb['vb'][:, ls], sb['kr'][:, ls], TN) for ls in heads]
            sb['o'] = [_dot(s_h.astype(BF16), sb['vb'][:, ls]) for s_h, ls in zip(sb['sc'], heads)]
        sts = [st_ref[hd] for hd in range(n_heads)]
        for sb in subs:
            sb['o'] = [o_h + _dot(sb['qd'][:, ls], st.astype(BF16), NT) for o_h, ls, st in zip(sb['o'], heads, sts)]
            sts = [st * sb['ex'][c - 1:c, ls] + kv for st, ls, kv in zip(sts, heads, sb['kv'])]
        for hd in range(n_heads):
            st_ref[hd] = sts[hd]
        for sb in subs:
            on = [o_h * lax.rsqrt(jnp.mean(o_h * o_h, axis=-1, keepdims=True) + NORM_EPS) for o_h in sb['o']]
            o_full = jnp.concatenate(on, axis=1) * ng
            o_ref[0, sb['rows'], :] = (o_full * _silu(g_ref[0, sb['rows'], :])).astype(o_ref.dtype)
        return carry

    lax.fori_loop(0, n_chunks // nsub, chunk_body, 0)


def _hgrn2(hg, lb, norm_g, ts):
    bsz, s, n4 = hg.shape
    w = n4 // 4
    mst, lm, rm = _hgrn2_consts(w)
    n_heads = w // HG_HEAD
    const2 = lambda b, i: (0, 0)
    const3 = lambda b, i: (0, 0, 0)
    return pl.pallas_call(
        _hgrn2_kernel,
        out_shape=jax.ShapeDtypeStruct((bsz, s, w), BF16),
        grid=(bsz, s // ts),
        in_specs=[pl.BlockSpec((1, ts, w), lambda b, i: (b, i, 0)),
                  pl.BlockSpec((1, ts, w), lambda b, i: (b, i, 1)),
                  pl.BlockSpec((1, ts, w), lambda b, i: (b, i, 2)),
                  pl.BlockSpec((1, ts, w), lambda b, i: (b, i, 3)),
                  pl.BlockSpec((1, w), const2),
                  pl.BlockSpec((1, w), const2),
                  pl.BlockSpec(mst.shape, const2),
                  pl.BlockSpec(lm.shape, const3),
                  pl.BlockSpec(rm.shape, const3)],
        out_specs=pl.BlockSpec((1, ts, w), lambda b, i: (b, i, 0)),
        scratch_shapes=[pltpu.VMEM((n_heads, HG_HEAD, HG_HEAD), F32)],
        compiler_params=_cparams(("arbitrary", "arbitrary")),
        name="hgrn2",
    )(hg, hg, hg, hg, lb.reshape(1, w), norm_g.reshape(1, w), mst, lm, rm)


def _rwkv_consts(width):
    c = CHUNK
    t = np.arange(c)[:, None]
    s = np.arange(c)[None, :]
    tri = np.tile((s <= t).astype(np.float32), (1, 3))
    tt = np.arange(2 * c)[:, None]
    ss = np.arange(2 * c)[None, :]
    same = (tt // c) == (ss // c)
    strict = same & ((ss % c) < (tt % c))
    incl = same & ((ss % c) <= (tt % c))
    hsum = (np.arange(MXU_K)[:, None] // RW_HEAD) == (np.arange(MXU_K)[None, :] // RW_HEAD)
    hsum = np.tile(hsum, (2, 1))
    return (jnp.asarray(tri, BF16), jnp.asarray(strict.astype(np.float32), F32),
            jnp.asarray(incl.astype(np.float32), F32), jnp.asarray(hsum.astype(np.float32), BF16))


def _rwkv7_kernel(p_ref, mu_ref, w0_ref, a0_ref, kk_ref, ka_ref, rk_ref, gnw_ref, gnb_ref,
                  w2_ref, a2_ref, g2_ref, tri_ref, sm_ref, im_ref, hs_ref,
                  o_ref, carry_ref, zt_ref):
    c = CHUNK
    nb = p_ref.shape[0]
    nch = p_ref.shape[1] // c
    width = o_ref.shape[2]
    n_pairs = width // LANES

    @pl.when(pl.program_id(0) == 0)
    def _():
        carry_ref[...] = jnp.zeros_like(carry_ref)
        zt_ref[...] = jnp.zeros_like(zt_ref)

    hs = hs_ref[...]
    tri = tri_ref[...]
    smask = sm_ref[...] > 0
    imask = im_ref[...] > 0
    lane = lax.broadcasted_iota(jnp.int32, (c, LANES), 1)
    m0 = (lane < RW_HEAD).astype(F32)
    m1 = 1.0 - m0

    def stack(x):
        return jnp.concatenate([x * m0, x * m1], axis=0)

    xs_rows = []
    for b in range(nb):
        p = p_ref[b]
        row = lax.broadcasted_iota(jnp.int32, p.shape, 0)
        prev = jnp.where(row == 0, carry_ref[b], pltpu.roll(p, 1, 0))
        carry_ref[b] = p[nch * c - 1:nch * c, :]
        xs_rows.append(p + mu_ref[...] * (prev - p))
    xs = jnp.concatenate(xs_rows, axis=0)
    r_all = xs[:, 0:width]
    k_all = xs[:, width:2 * width]
    v_all = xs[:, 2 * width:3 * width]
    slab = xs[:, 3 * width:]
    nz = -(w0_ref[...] + _dot(jnp.tanh(slab).astype(BF16), w2_ref[...]))
    softplus = jnp.maximum(nz, 0.0) + jnp.log(1.0 + jnp.exp(-jnp.abs(nz)))
    ld_all = -jnp.exp(-softplus - 0.5)
    a_all = _sigmoid(a0_ref[...] + _dot(slab.astype(BF16), a2_ref[...]))
    g_all = _dot(_sigmoid(slab).astype(BF16), g2_ref[...])
    kk0 = k_all * kk_ref[...]
    kk_all = kk0 * lax.rsqrt(jnp.maximum(_head_sums(kk0 * kk0, hs), 1e-24))
    k2_all = k_all * (1.0 + (a_all - 1.0) * ka_ref[...])

    units = []
    for b, j in [(b, j) for b in range(nb) for j in range(nch)]:
        rb = slice((b * nch + j) * c, (b * nch + j + 1) * c)
        r, k2, v, ld = r_all[rb], k2_all[rb], v_all[rb], ld_all[rb]
        a_in = -kk_all[rb]
        b_in = kk_all[rb] * a_all[rb]
        cum = _dot_exact_lhs(tri, ld)
        cum_t = cum[c - 1:c, :]
        e_c = jnp.exp(cum)
        e_nc = jnp.exp(-cum)
        e_rem = jnp.exp(cum_t - cum)
        at_f = a_in * jnp.exp(cum - ld)
        rt_f = r * e_c
        kt_f = k2 * e_nc
        bt_f = b_in * e_nc
        kh_f = k2 * e_rem
        bh_f = b_in * e_rem
        p_t = jnp.exp(cum_t)
        for pi in range(n_pairs):
            ls = slice(pi * LANES, (pi + 1) * LANES)
            units.append(dict(
                b=b, j=j, pi=pi,
                at=stack(at_f[:, ls]).astype(BF16), rt=stack(rt_f[:, ls]).astype(BF16),
                kt=stack(kt_f[:, ls]).astype(BF16), bt=stack(bt_f[:, ls]).astype(BF16),
                kh=stack(kh_f[:, ls]).astype(BF16), bh=stack(bh_f[:, ls]).astype(BF16),
                vs=stack(v[:, ls]).astype(BF16), p_t=p_t[:, ls]))

    for u in units:
        lhs = jnp.concatenate([u['at'], u['rt']], axis=0)
        u['g'] = _dot(lhs, jnp.concatenate([u['kt'], u['bt']], axis=0), NT)
    for u in units:
        g = u.pop('g')
        u['a_ak'] = jnp.where(smask, g[:2 * c, :2 * c], 0.0).astype(BF16)
        u['pw'] = jnp.where(smask, g[:2 * c, 2 * c:], 0.0).astype(BF16)
        u['a_r'] = jnp.where(jnp.concatenate([imask, imask], axis=1), g[2 * c:], 0.0).astype(BF16)
    for u in units:
        akv = _dot(u.pop('a_ak'), u['vs'])
        u['x'] = jnp.concatenate([u['at'].astype(F32), akv], axis=1)
    n_lvl = int(np.log2(c))
    for lvl in range(n_lvl):
        for u in units:
            u['x'] = u['x'] + _dot(u['pw'], u['x'].astype(BF16))
        if lvl + 1 < n_lvl:
            for u in units:
                u['pw'] = _dot(u['pw'], u['pw']).astype(BF16)
    for u in units:
        x = u.pop('x')
        u['wr'] = jnp.concatenate([x[:, :LANES].astype(BF16), u['rt']], axis=0)
        u['u_loc'] = x[:, LANES:]
    zt = {(b, pi): zt_ref[b, pi] for b in range(nb) for pi in range(n_pairs)}
    for j in range(nch):
        tail = [u for u in units if u['j'] == j]
        for u in tail:
            u['uy'] = _dot(u.pop('wr'), zt[u['b'], u['pi']].astype(BF16), NT)
        for u in tail:
            uy = u.pop('uy')
            u['u'] = (uy[:2 * c] + u.pop('u_loc')).astype(BF16)
            u['y0'] = uy[2 * c:]
        for u in tail:
            vu = jnp.concatenate([u['vs'], u['u']], axis=0)
            u['y'] = u.pop('y0') + _dot(u['a_r'], vu)
            upd = _dot(vu, jnp.concatenate([u['kh'], u['bh']], axis=0), TN)
            zt[u['b'], u['pi']] = zt[u['b'], u['pi']] * u['p_t'] + upd
    for (b, pi), z in zt.items():
        zt_ref[b, pi] = z

    inv_n = 1.0 / RW_HEAD
    y = jnp.concatenate(
        [jnp.concatenate([u['y'][:c] + u['y'][c:] for u in units if (u['b'], u['j']) == (b, j)], axis=1)
         for b in range(nb) for j in range(nch)], axis=0)
    mean = _head_sums(y, hs) * inv_n
    d = y - mean
    var = _head_sums(d * d, hs) * inv_n
    yn = d * lax.rsqrt(var + RW_GN_EPS) * gnw_ref[...] + gnb_ref[...]
    bonus = _head_sums(r_all * k2_all * rk_ref[...], hs) * v_all
    out = ((yn + bonus) * g_all).astype(o_ref.dtype)
    for b in range(nb):
        o_ref[b] = out[b * nch * c:(b + 1) * nch * c]


def _rwkv7(rw, mu, w0, w2, a0, a2, g2, k_k, k_a, r_k, gn_w, gn_b, ts):
    bsz, s, cols = rw.shape
    width = w0.shape[-1]
    n_pairs = width // LANES
    slab = cols - 3 * width
    dl, al, gl = w2.shape[0], a2.shape[0], g2.shape[0]
    w2f = jnp.zeros((slab, width), F32).at[0:dl].set(w2).astype(BF16)
    a2f = jnp.zeros((slab, width), F32).at[dl:dl + al].set(a2).astype(BF16)
    g2f = jnp.zeros((slab, width), F32).at[dl + al:dl + al + gl].set(g2).astype(BF16)
    mup = jnp.zeros((1, cols), F32).at[0, :mu.shape[-1]].set(mu)
    tri, sm, im, hs = _rwkv_consts(width)
    row = lambda x: x.reshape(1, width)
    const = lambda i: (0, 0)
    vec = pl.BlockSpec((1, width), const)
    return pl.pallas_call(
        _rwkv7_kernel,
        out_shape=jax.ShapeDtypeStruct((bsz, s, width), BF16),
        grid=(s // ts,),
        in_specs=[pl.BlockSpec((bsz, ts, cols), lambda i: (0, i, 0)),
                  pl.BlockSpec((1, cols), const),
                  vec, vec, vec, vec, vec, vec, vec,
                  pl.BlockSpec((slab, width), const),
                  pl.BlockSpec((slab, width), const),
                  pl.BlockSpec((slab, width), const),
                  pl.BlockSpec(tri.shape, const),
                  pl.BlockSpec(sm.shape, const),
                  pl.BlockSpec(im.shape, const),
                  pl.BlockSpec(hs.shape, const)],
        out_specs=pl.BlockSpec((bsz, ts, width), lambda i: (0, i, 0)),
        scratch_shapes=[pltpu.VMEM((bsz, 1, cols), F32),
                        pltpu.VMEM((bsz, n_pairs, LANES, LANES), F32)],
        compiler_params=_cparams(("arbitrary",)),
        name="rwkv7",
    )(rw, mup, row(w0), row(a0), row(k_k), row(k_a), row(r_k), row(gn_w), row(gn_b),
      w2f, a2f, g2f, tri, sm, im, hs)


def _out_proj_kernel(n_groups, n_experts,
                     x_ref, oa_ref, ob_ref, ga_ref, gb_ref, gt1_ref, sc2_ref, sh2_ref, g2_ref,
                     wa_ref, wb_ref, wo_ref, wr_ref, wrl_ref, br_ref, tril_ref,
                     x1_ref, h2_ref, route_ref, routet_ref, cnt_ref, carry_ref):
    tm = x_ref.shape[1]
    first = (pl.program_id(0) == 0) & (pl.program_id(1) == 0)

    @pl.when(first)
    def _():
        carry_ref[...] = jnp.zeros_like(carry_ref)

    tp = tril_ref.shape[0]
    parts = [dict(rows=slice(r0, r0 + tp)) for r0 in range(0, tm, tp)]
    lane = lax.broadcasted_iota(jnp.int32, (tp, LANES), 1)
    neg = jnp.float32(-jnp.inf)
    big = jnp.int32(1 << 20)
    eg = n_experts // n_groups
    is_g = (lane >= n_experts) & (lane < n_experts + n_groups)
    for pt in parts:
        rows = pt['rows']
        pa = _dot(oa_ref[0, rows, :], wa_ref[...])
        pb = _dot(ob_ref[0, rows, :], wb_ref[...])
        mixed = ga_ref[0, rows, :].astype(F32) * pa + gb_ref[0, rows, :].astype(F32) * pb
        x1 = x_ref[0, rows, :] + gt1_ref[0] * _dot(mixed.astype(BF16), wo_ref[...])
        x1_ref[0, rows, :] = x1
        ms = jnp.mean(x1 * x1, axis=-1, keepdims=True)
        h2 = (x1 * lax.rsqrt(ms + NORM_EPS) * g2_ref[...]) * (1.0 + sc2_ref[0]) + sh2_ref[0]
        _rows_to_tiles(h2_ref.at[pl.ds(rows.start * SUB, tp * SUB), :], h2)
        h2_hi = h2.astype(BF16)
        h2_lo = (h2 - h2_hi.astype(F32)).astype(BF16)
        logits = (_dot(h2_hi, wr_ref[...]) + _dot(h2_lo, wr_ref[...]) + _dot(h2_hi, wrl_ref[...])
                  + br_ref[...])
        lg = jnp.where(is_g, logits, neg)
        mg = jnp.max(lg, axis=-1, keepdims=True)
        p_grp = 1.0 / jnp.sum(jnp.where(is_g, jnp.exp(lg - mg), 0.0), axis=-1, keepdims=True)
        gidx = jnp.min(jnp.where(lg == mg, lane, big), axis=-1, keepdims=True) - n_experts
        sel = (lane >= gidx * eg) & (lane < gidx * eg + eg)
        le = jnp.where(sel, logits, neg)
        me = jnp.max(le, axis=-1, keepdims=True)
        pe_un = jnp.where(sel, jnp.exp(le - me), 0.0)
        pe = jnp.where(sel, pe_un / jnp.sum(pe_un, axis=-1, keepdims=True), -1.0)
        v1 = jnp.max(pe, axis=-1, keepdims=True)
        i1 = jnp.min(jnp.where(pe == v1, lane, big), axis=-1, keepdims=True)
        pe2 = jnp.where(lane == i1, -1.0, pe)
        v2 = jnp.max(pe2, axis=-1, keepdims=True)
        i2 = jnp.min(jnp.where(pe2 == v2, lane, big), axis=-1, keepdims=True)
        wsum = v1 + v2
        pt.update(w1=p_grp * v1 / wsum, w2=p_grp * v2 / wsum, i1=i1, i2=i2,
                  oh1=(lane == i1).astype(F32), oh2=(lane == i2).astype(F32))

    carry = carry_ref[...]
    for pt in parts:
        both = pt['oh1'] + pt['oh2']
        before = _dot(tril_ref[...], both.astype(BF16)) + carry
        rank1 = jnp.sum(pt['oh1'] * before, axis=-1, keepdims=True)
        rank2 = jnp.sum(pt['oh2'] * before, axis=-1, keepdims=True)
        carry = carry + jnp.sum(both, axis=0, keepdims=True)
        out = jnp.where(lane == 0, pt['w1'], 0.0)
        out = jnp.where(lane == 1, pt['w2'], out)
        out = jnp.where(lane == 2, pt['i1'].astype(F32), out)
        out = jnp.where(lane == 3, pt['i2'].astype(F32), out)
        out = jnp.where(lane == 4, rank1, out)
        out = jnp.where(lane == 5, rank2, out)
        route_ref[0, pt['rows'], :] = out
        routet_ref[:, pt['rows']] = out.T[0:SUBLANES, :]
    carry_ref[...] = carry
    cnt_ref[...] = carry


def _out_proj(x, o_a, o_b, gates, mod3, norm2_g, wa, wb, wo, wr, br, n_groups, n_experts, tm):
    bsz, s, d = x.shape
    wdt = o_a.shape[-1]
    tp = tm
    tril = jnp.asarray(np.tril(np.ones((tp, tp), np.float32), -1), BF16)
    wr_hi = wr.astype(BF16)
    wr_lo = (wr - wr_hi.astype(F32)).astype(BF16)
    const = lambda b, i: (0, 0)
    tile = lambda b, i: (b, i, 0)
    kern = functools.partial(_out_proj_kernel, n_groups, n_experts)
    return pl.pallas_call(
        kern,
        out_shape=(jax.ShapeDtypeStruct((bsz, s, d), F32),
                   jax.ShapeDtypeStruct((bsz * s * SUB, LANES), U32),
                   jax.ShapeDtypeStruct((bsz, s, LANES), F32),
                   jax.ShapeDtypeStruct((bsz * (s // tm) * SUBLANES, tm), F32),
                   jax.ShapeDtypeStruct((1, LANES), F32)),
        grid=(bsz, s // tm),
        in_specs=[pl.BlockSpec((1, tm, d), tile),
                  pl.BlockSpec((1, tm, wdt), tile),
                  pl.BlockSpec((1, tm, wdt), tile),
                  pl.BlockSpec((1, tm, d), lambda b, i: (b, i, 0)),
                  pl.BlockSpec((1, tm, d), lambda b, i: (b, i, 1)),
                  pl.BlockSpec((1, 1, d), lambda b, i: (b * 6 + 2, 0, 0)),
                  pl.BlockSpec((1, 1, d), lambda b, i: (b * 6 + 4, 0, 0)),
                  pl.BlockSpec((1, 1, d), lambda b, i: (b * 6 + 3, 0, 0)),
                  pl.BlockSpec((1, d), const),
                  pl.BlockSpec(wa.shape, const),
                  pl.BlockSpec(wb.shape, const),
                  pl.BlockSpec(wo.shape, const),
                  pl.BlockSpec(wr.shape, const),
                  pl.BlockSpec(wr.shape, const),
                  pl.BlockSpec((1, LANES), const),
                  pl.BlockSpec((tp, tp), const)],
        out_specs=(pl.BlockSpec((1, tm, d), tile),
                   pl.BlockSpec((tm * SUB, LANES), lambda b, i: (b * (s // tm) + i, 0)),
                   pl.BlockSpec((1, tm, LANES), tile),
                   pl.BlockSpec((SUBLANES, tm), lambda b, i: (b * (s // tm) + i, 0)),
                   pl.BlockSpec((1, LANES), const)),
        scratch_shapes=[pltpu.VMEM((1, LANES), F32)],
        compiler_params=_cparams(("arbitrary", "arbitrary")),
        name="out_proj",
    )(x, o_a, o_b, gates, gates, mod3, mod3, mod3, norm2_g.reshape(1, d), wa, wb, wo, wr_hi, wr_lo, br, tril)


def _moe_dispatch_kernel(tm, tm_o, n_tok, blk, dest_ref, zstart_ref, zcnt_ref, nused_ref, h_ref, xb_ref, smap_ref,
                         zbuf, stage, sem, zsem):
    i = pl.program_id(0)
    n_steps = pl.num_programs(0)
    n_slots = smap_ref.shape[0]
    n_experts = zcnt_ref.shape[0]

    @pl.when(i == 0)
    def _():
        def init(s_, carry):
            smap_ref[s_] = 0
            return carry
        lax.fori_loop(nused_ref[0] * blk, n_slots, init, 0)
        zbuf[...] = jnp.zeros_like(zbuf)

        def zero_row(e, j):
            dst = pl.multiple_of((zstart_ref[e] + j) * SUB, SUB)
            return pltpu.make_async_copy(zbuf.at[pl.ds(0, SUB), :], xb_ref.at[pl.ds(dst, SUB), :], zsem)

        def zero_block(b):
            dst = pl.multiple_of(b * (blk * SUB), blk * SUB)
            return pltpu.make_async_copy(zbuf, xb_ref.at[pl.ds(dst, blk * SUB), :], zsem)

        for e in range(n_experts):
            def zstart(j, carry, e=e):
                zero_row(e, j).start()
                smap_ref[zstart_ref[e] + j] = TOP_K * n_tok + e * blk + j
                return carry
            lax.fori_loop(0, zcnt_ref[e], zstart, 0)

        def bstart(b, carry):
            zero_block(b).start()
            return carry
        lax.fori_loop(nused_ref[0], n_slots // blk, bstart, 0)
        for e in range(n_experts):
            def zwait(j, carry, e=e):
                zero_row(e, j).wait()
                return carry
            lax.fori_loop(0, zcnt_ref[e], zwait, 0)

        def bwait(b, carry):
            zero_block(b).wait()
            return carry
        lax.fori_loop(nused_ref[0], n_slots // blk, bwait, 0)

    slot = i % 2
    stage[slot] = h_ref[...]
    tok0 = i * tm
    dest0 = (tok0 // tm_o) * (TOP_K * tm_o) + tok0 % tm_o
    for r in range(tm):
        tok = tok0 + r
        for k in range(TOP_K):
            d = dest_ref[dest0 + k * tm_o + r]
            dst = pl.multiple_of(d * SUB, SUB)
            pltpu.make_async_copy(stage.at[slot, pl.ds(r * SUB, SUB), :], xb_ref.at[pl.ds(dst, SUB), :],
                                  sem.at[slot]).start(priority=k)
            smap_ref[d] = k * n_tok + tok

    def wait_step(s_):
        for _ in range(TOP_K):
            pltpu.make_async_copy(stage.at[s_], xb_ref.at[pl.ds(0, tm * SUB), :], sem.at[s_]).wait()

    @pl.when(i > 0)
    def _():
        wait_step(1 - slot)

    @pl.when(i == n_steps - 1)
    def _():
        wait_step(slot)


def _moe_dispatch(h2, dest, zstart, zcnt, n_used, n_slots, blk, tm, tm_o):
    n_tok = h2.shape[0] // SUB
    grid_spec = pltpu.PrefetchScalarGridSpec(
        num_scalar_prefetch=4,
        grid=(n_tok // tm,),
        in_specs=[pl.BlockSpec((tm * SUB, LANES), lambda i, *_: (i, 0))],
        out_specs=(pl.BlockSpec(memory_space=pl.ANY),
                   pl.BlockSpec(memory_space=pltpu.SMEM)),
        scratch_shapes=[pltpu.VMEM((blk * SUB, LANES), U32),
                        pltpu.VMEM((2, tm * SUB, LANES), U32),
                        pltpu.SemaphoreType.DMA((2,)),
                        pltpu.SemaphoreType.DMA],
    )
    return pl.pallas_call(
        functools.partial(_moe_dispatch_kernel, tm, tm_o, n_tok, blk),
        out_shape=(jax.ShapeDtypeStruct((n_slots * SUB, LANES), U32),
                   jax.ShapeDtypeStruct((n_slots,), jnp.int32)),
        grid_spec=grid_spec,
        compiler_params=_cparams(("arbitrary",)),
        name="moe_dispatch",
    )(dest, zstart, zcnt, n_used, h2)


def _moe_expert_kernel(n_tok, blk_e_ref, nused_ref, smap_ref, x_ref, wg_ref, wu_ref, wd_ref, y_ref,
                       ystage, wgb, wub, wdb, sem):
    i = pl.program_id(0)
    n_used = nused_ref[0]
    blk = x_ref.shape[0] // SUB
    first_real_blocks = TOP_K * n_tok // blk
    first_spare = y_ref.shape[0] // SUB - blk

    def issue(b, slot, rows):
        for r in rows:
            t = jnp.where(b >= 0, smap_ref[jnp.maximum(b, 0) * blk + r], first_spare + r)
            dst = pl.multiple_of(t * SUB, SUB)
            pltpu.make_async_copy(ystage.at[slot, pl.ds(r * SUB, SUB), :], y_ref.at[pl.ds(dst, SUB), :],
                                  sem.at[slot]).start(priority=r % 2)

    def wait_block(slot):
        pltpu.make_async_copy(ystage.at[slot], y_ref.at[pl.ds(0, blk * SUB), :], sem.at[slot]).wait()

    @pl.when(i == 0)
    def _():
        ystage[...] = jnp.zeros_like(ystage)
        n_spare_blocks = y_ref.shape[0] // (blk * SUB) - first_real_blocks

        def spare_copy(c):
            dst = (first_real_blocks + c) * blk * SUB
            return pltpu.make_async_copy(ystage.at[0], y_ref.at[pl.ds(dst, blk * SUB), :], sem.at[0])

        for c in range(n_spare_blocks):
            spare_copy(c).start()
        for c in range(n_spare_blocks):
            spare_copy(c).wait()

    new_expert = (i == 0) | (blk_e_ref[i] != blk_e_ref[jnp.maximum(i - 1, 0)])

    @pl.when((i < n_used) & new_expert)
    def _():
        wgb[...] = wg_ref[0].astype(BF16)
        wub[...] = wu_ref[0].astype(BF16)
        wdb[...] = wd_ref[0].astype(BF16)

    @pl.when(i < n_used)
    def _():
        slot = i % 2
        pslot = 1 - slot

        @pl.when(i > 0)
        def _():
            wait_block(slot)

        q = blk // 4
        xb = _tiles_to_rows(x_ref, blk).astype(BF16)
        issue(i - 1, pslot, range(0, q))
        hg = _dot(xb, wgb[...])
        issue(i - 1, pslot, range(q, 2 * q))
        hu = _dot(xb, wub[...])
        issue(i - 1, pslot, range(2 * q, 3 * q))
        hid = (_silu(hg) * hu).astype(BF16)
        y = _dot(hid, wdb[...])
        issue(i - 1, pslot, range(3 * q, blk))
        _rows_to_tiles(ystage.at[slot], y)

        @pl.when(i == n_used - 1)
        def _():
            issue(i, slot, range(blk))
            wait_block(pslot)
            wait_block(slot)


def _moe_experts(xb, smap, blk_e, n_used, w_gate, w_up, w_down, blk, n_tok):
    d = w_gate.shape[1]
    assert d == 2 * SUB * LANES and xb.shape[1] == LANES
    n_slots = smap.shape[0]
    nb = n_slots // blk
    f = w_gate.shape[-1]
    n_experts = w_gate.shape[0]
    assert (TOP_K * n_tok) % blk == 0
    n_tiles = TOP_K * n_tok + (n_experts + 1) * blk
    grid_spec = pltpu.PrefetchScalarGridSpec(
        num_scalar_prefetch=3,
        grid=(nb,),
        in_specs=[pl.BlockSpec((blk * SUB, LANES), lambda i, be, nu, sm: (jnp.minimum(i, nu[0] - 1), 0)),
                  pl.BlockSpec((1, d, f), lambda i, be, nu, sm: (be[i], 0, 0)),
                  pl.BlockSpec((1, d, f), lambda i, be, nu, sm: (be[i], 0, 0)),
                  pl.BlockSpec((1, f, d), lambda i, be, nu, sm: (be[i], 0, 0))],
        out_specs=pl.BlockSpec(memory_space=pl.ANY),
        scratch_shapes=[pltpu.VMEM((2, blk * SUB, LANES), U32),
                        pltpu.VMEM((d, f), BF16),
                        pltpu.VMEM((d, f), BF16),
                        pltpu.VMEM((f, d), BF16),
                        pltpu.SemaphoreType.DMA((2,))],
    )
    return pl.pallas_call(
        functools.partial(_moe_expert_kernel, n_tok),
        out_shape=jax.ShapeDtypeStruct((n_tiles * SUB, LANES), U32),
        grid_spec=grid_spec,
        compiler_params=_cparams(("arbitrary",)),
        name="moe_experts",
    )(blk_e, n_used, smap, xb, w_gate, w_up, w_down)


def _moe_combine_kernel(x1_ref, route_ref, gt2_ref, fg_ref, y1_ref, y2_ref, o_ref):
    tm = x1_ref.shape[0]
    route = route_ref[...]
    moe = route[:, 0:1] * _tiles_to_rows(y1_ref, tm) + route[:, 1:2] * _tiles_to_rows(y2_ref, tm)
    xo = x1_ref[...] + gt2_ref[0] * moe
    ms = jnp.mean(xo * xo, axis=-1, keepdims=True)
    o_ref[...] = xo * lax.rsqrt(ms + NORM_EPS) * fg_ref[...]


def _moe_combine(x1, route, yt, mod3, final_g, s, tm):
    n, d = x1.shape
    tiles_per_batch = s // tm
    n_steps = n // tm
    return pl.pallas_call(
        _moe_combine_kernel,
        out_shape=jax.ShapeDtypeStruct((n, d), F32),
        grid=(n_steps,),
        in_specs=[pl.BlockSpec((tm, d), lambda i: (i, 0)),
                  pl.BlockSpec((tm, LANES), lambda i: (i, 0)),
                  pl.BlockSpec((1, 1, d), lambda i: ((i // tiles_per_batch) * 6 + 5, 0, 0)),
                  pl.BlockSpec((1, d), lambda i: (0, 0)),
                  pl.BlockSpec((tm * SUB, LANES), lambda i: (i, 0)),
                  pl.BlockSpec((tm * SUB, LANES), lambda i: (n_steps + i, 0))],
        out_specs=pl.BlockSpec((tm, d), lambda i: (i, 0)),
        compiler_params=_cparams(("arbitrary",)),
        name="moe_combine",
    )(x1, route, mod3, final_g.reshape(1, d), yt, yt)


def _pick(n, candidates):
    for t in candidates:
        if n % t == 0:
            return t
    raise ValueError(f"no tile in {candidates} divides {n}")


def kernel(x, c, ada_w, ada_b, norm1_g, w_in, hg_lb, hg_norm_g, rw_mu, rw_w0, rw_w2, rw_a0, rw_a2, rw_g2, rw_kk, rw_ka, rw_rk, rw_gn_w, rw_gn_b, w_proj_a, w_proj_b, w_out, norm2_g, router_g_w, router_g_b, router_e_w, router_e_b, exp_w_gate, exp_w_up, exp_w_down, final_g):
    bsz, s, d = x.shape
    depth = ada_w.shape[0]
    hg_f = hg_lb.shape[-1]
    hg_w = hg_norm_g.shape[-1]
    rw_w = rw_w0.shape[-1]
    rw_cols = rw_mu.shape[-1]
    n_groups = router_g_w.shape[-1]
    n_experts = router_e_w.shape[-1]
    assert hg_f == hg_w and s % CHUNK == 0 and n_experts + n_groups <= LANES and d == 2 * SUB * LANES

    lb_all = jnp.cumsum(jax.nn.softmax(hg_lb.astype(F32), axis=0), axis=0)
    n = bsz * s
    blk = 256
    n_blocks = (n * TOP_K + n_experts * blk) // blk
    for l in range(depth):
        mod = _ada_mod(c, ada_w[l], ada_b[l])
        mod3 = mod.reshape(bsz * 6, 1, d)

        hg_cols = 2 * hg_f + 2 * hg_w
        rw_pad = -(-rw_cols // 256) * 256
        wl = w_in[l]
        w_hg = wl[:, :hg_cols].astype(BF16)
        w_rw = jnp.zeros((d, rw_pad), BF16).at[:, :rw_cols].set(wl[:, hg_cols:hg_cols + rw_cols].astype(BF16))
        w_gt = wl[:, hg_cols + rw_cols:].astype(BF16)
        hg, rw, gates = _in_proj(x, mod3, norm1_g[l], w_hg, w_rw, w_gt, _pick(s, (512, 256, 128, 64)))

        o_a = _hgrn2(hg, lb_all[l], hg_norm_g[l], _pick(s, (512, 256, 128, 64)))
        o_b = _rwkv7(rw, rw_mu[l], rw_w0[l], rw_w2[l], rw_a0[l], rw_a2[l], rw_g2[l],
                     rw_kk[l], rw_ka[l], rw_rk[l].reshape(-1), rw_gn_w[l], rw_gn_b[l],
                     _pick(s, (2 * CHUNK, CHUNK)))

        wr = jnp.zeros((d, LANES), F32).at[:, :n_experts].set(router_e_w[l])
        wr = wr.at[:, n_experts:n_experts + n_groups].set(router_g_w[l])
        br = jnp.zeros((1, LANES), F32).at[0, :n_experts].set(router_e_b[l])
        br = br.at[0, n_experts:n_experts + n_groups].set(router_g_b[l])
        tm_o = _pick(s, (512, 256, 128, 64))
        x1, h2, route, route_t, counts = _out_proj(
            x, o_a, o_b, gates, mod3, norm2_g[l],
            w_proj_a[l].astype(BF16), w_proj_b[l].astype(BF16), w_out[l].astype(BF16),
            wr, br, n_groups, n_experts, tm_o)

        route2 = route.reshape(n, LANES)
        rt = route_t.reshape(n // tm_o, SUBLANES, tm_o)
        eid = rt[:, 2:4, :].astype(jnp.int32)
        rank = rt[:, 4:6, :].astype(jnp.int32)
        cnt = counts[0, :n_experts].astype(jnp.int32)
        padded = (cnt + blk - 1) // blk * blk
        pad_end = jnp.cumsum(padded)
        pad_start = pad_end - padded
        e_ax = jnp.arange(n_experts, dtype=jnp.int32)[:, None, None, None]
        dest = rank + jnp.sum(jnp.where(eid[None] == e_ax, pad_start[:, None, None, None], 0), axis=0)
        blk_start = jnp.arange(n_blocks, dtype=jnp.int32) * blk
        blk_e = jnp.minimum(jnp.sum((pad_end[None, :] <= blk_start[:, None]).astype(jnp.int32), axis=1),
                            n_experts - 1)
        n_used = (pad_end[-1:] // blk).astype(jnp.int32)
        tm = _pick(s, (256, 128, 64))
        xb, smap = _moe_dispatch(h2, dest.reshape(-1), pad_start + cnt, padded - cnt, n_used,
                                 n_blocks * blk, blk, tm, tm_o)
        yt = _moe_experts(xb, smap, blk_e, n_used, exp_w_gate[l], exp_w_up[l], exp_w_down[l], blk, n)
        last = l == depth - 1
        assert last, "the final RMSNorm is fused into the last layer's combine"
        out = _moe_combine(x1.reshape(n, d), route2, yt, mod3, final_g, s, tm)
        x = out.reshape(bsz, s, d)
    return x
```

```python
import functools

import numpy as np
import jax
import jax.numpy as jnp
from jax import lax
from jax.experimental import pallas as pl
from jax.experimental.pallas import tpu as pltpu

F32 = jnp.float32
BF16 = jnp.bfloat16
HIGHEST = lax.Precision.HIGHEST

NORM_EPS = 1e-6
HG_HEAD = 128
RW_HEAD = 64
RW_GN_EPS = 64e-5
TOP_K = 2
CHUNK = 64
LANES = 128
SUB = 4
U32 = jnp.uint32
MXU_K = 256
SUBLANES = 8
VMEM_LIMIT = 56 * 1024 * 1024

NT = (((1,), (1,)), ((), ()))
TN = (((0,), (0,)), ((), ()))


def _dot(a, b, dims=None, precision=None):
    if dims is None:
        return jnp.dot(a, b, preferred_element_type=F32, precision=precision)
    return lax.dot_general(a, b, dims, preferred_element_type=F32, precision=precision)


def _split3(x):
    hi = x.astype(BF16)
    r1 = x - hi.astype(F32)
    mid = r1.astype(BF16)
    lo = (r1 - mid.astype(F32)).astype(BF16)
    return hi, mid, lo


def _dot_exact_lhs(m3_bf16, x):
    return _dot(m3_bf16, jnp.concatenate(_split3(x), axis=0))


def _head_sums(x, m2_bf16):
    outs = []
    for g0 in range(0, x.shape[1], MXU_K):
        xg = x[:, g0:g0 + MXU_K]
        hi = xg.astype(BF16)
        lo = (xg - hi.astype(F32)).astype(BF16)
        outs.append(_dot(jnp.concatenate([hi, lo], axis=1), m2_bf16))
    return jnp.concatenate(outs, axis=1)


def _sigmoid(x):
    return 1.0 / (1.0 + jnp.exp(-x))


def _silu(x):
    return x * _sigmoid(x)


def _rows_to_tiles(ref, val):
    m, half = val.shape[0], val.shape[1] // 2
    hi = lax.bitcast_convert_type(val[:, :half].astype(BF16).astype(F32), U32)
    lo = lax.bitcast_convert_type(val[:, half:].astype(BF16).astype(F32), U32)
    w = (hi & jnp.uint32(0xFFFF0000)) | (lo >> 16)
    for j in range(SUB):
        ref[pl.ds(j, m, stride=SUB), :] = w[:, j * LANES:(j + 1) * LANES]


def _tiles_to_rows(ref, m, base=0):
    w = jnp.concatenate([ref[pl.ds(base * SUB + j, m, stride=SUB), :] for j in range(SUB)], axis=1)
    hi = lax.bitcast_convert_type(w & jnp.uint32(0xFFFF0000), F32)
    lo = lax.bitcast_convert_type(w << 16, F32)
    return jnp.concatenate([hi, lo], axis=1)


def _cparams(sem):
    return pltpu.CompilerParams(dimension_semantics=sem, vmem_limit_bytes=VMEM_LIMIT)


def _ada_kernel(c_ref, w_ref, b_ref, o_ref):
    c = c_ref[...]
    o_ref[...] = _dot(_silu(c), w_ref[...], precision=HIGHEST) + b_ref[...]


def _ada_mod(c, w, b):
    bsz, d = c.shape
    n = w.shape[1]
    rows = 8
    cp = jnp.zeros((rows, d), F32).at[:bsz].set(c)
    tn = 1536
    out = pl.pallas_call(
        _ada_kernel,
        out_shape=jax.ShapeDtypeStruct((rows, n), F32),
        grid=(n // tn,),
        in_specs=[pl.BlockSpec((rows, d), lambda j: (0, 0)),
                  pl.BlockSpec((d, tn), lambda j: (0, j)),
                  pl.BlockSpec((1, tn), lambda j: (0, j))],
        out_specs=pl.BlockSpec((rows, tn), lambda j: (0, j)),
        compiler_params=_cparams(("arbitrary",)),
        name="ada_mod",
    )(cp, w, b.reshape(1, n))
    return out[:bsz]


def _in_proj_kernel(x_ref, sh_ref, sc_ref, g_ref, whg_ref, wrw_ref, wgt_ref, hg_ref, rw_ref, gt_ref):
    x = x_ref[0]
    ms = jnp.mean(x * x, axis=-1, keepdims=True)
    h = (x * lax.rsqrt(ms + NORM_EPS) * g_ref[...]) * (1.0 + sc_ref[0]) + sh_ref[0]
    hb = h.astype(BF16)
    step = 512
    for n0 in range(0, whg_ref.shape[1], step):
        hg_ref[0, :, n0:n0 + step] = _dot(hb, whg_ref[:, n0:n0 + step])
    for n0 in range(0, wrw_ref.shape[1], 256):
        rw_ref[0, :, n0:n0 + 256] = _dot(hb, wrw_ref[:, n0:n0 + 256])
    for n0 in range(0, wgt_ref.shape[1], step):
        gt_ref[0, :, n0:n0 + step] = _sigmoid(_dot(hb, wgt_ref[:, n0:n0 + step])).astype(BF16)


def _in_proj(x, mod3, norm_g, w_hg, w_rw, w_gt, tm):
    bsz, s, d = x.shape
    n_hg, n_rw, n_gt = w_hg.shape[1], w_rw.shape[1], w_gt.shape[1]
    const = lambda b, i: (0, 0)
    return pl.pallas_call(
        _in_proj_kernel,
        out_shape=(jax.ShapeDtypeStruct((bsz, s, n_hg), F32),
                   jax.ShapeDtypeStruct((bsz, s, n_rw), F32),
                   jax.ShapeDtypeStruct((bsz, s, n_gt), BF16)),
        grid=(bsz, s // tm),
        in_specs=[pl.BlockSpec((1, tm, d), lambda b, i: (b, i, 0)),
                  pl.BlockSpec((1, 1, d), lambda b, i: (b * 6 + 0, 0, 0)),
                  pl.BlockSpec((1, 1, d), lambda b, i: (b * 6 + 1, 0, 0)),
                  pl.BlockSpec((1, d), const),
                  pl.BlockSpec((d, n_hg), const),
                  pl.BlockSpec((d, n_rw), const),
                  pl.BlockSpec((d, n_gt), const)],
        out_specs=(pl.BlockSpec((1, tm, n_hg), lambda b, i: (b, i, 0)),
                   pl.BlockSpec((1, tm, n_rw), lambda b, i: (b, i, 0)),
                   pl.BlockSpec((1, tm, n_gt), lambda b, i: (b, i, 0))),
        compiler_params=_cparams(("arbitrary", "arbitrary")),
        name="in_proj",
    )(x, mod3, mod3, norm_g.reshape(1, d), w_hg, w_rw, w_gt)


_HG_LEVELS = (32, 16, 8, 4, 2, 1)


def _hgrn2_consts(width):
    c = CHUNK
    t = np.arange(c)[:, None]
    s = np.arange(c)[None, :]
    blocks = [(s <= t), (s > t)]
    lvl_masks = []
    right = []
    for h in _HG_LEVELS:
        m = (t // (2 * h)) * 2 * h + h
        is_r = (t & h) != 0
        blk = np.where(is_r, (s >= m) & (s <= t), (s > t) & (s <= m - 1))
        blocks.append(blk)
        lvl_masks.append(is_r & ((s & h) == 0) & ((t // (2 * h)) == (s // (2 * h))))
        right.append(np.broadcast_to(is_r, (c, width)))
    mst = np.tile(np.concatenate(blocks, axis=0).astype(np.float32), (1, 3))
    lm = np.stack([np.eye(c, dtype=bool)] + lvl_masks).astype(np.float32)
    rm = np.stack(right).astype(np.float32)
    return jnp.asarray(mst, BF16), jnp.asarray(lm, F32), jnp.asarray(rm, F32)


def _hgrn2_kernel(q_ref, f_ref, i_ref, g_ref, lb_ref, ng_ref, mst_ref, lm_ref, rm_ref, o_ref, st_ref):
    c = CHUNK
    n_heads = q_ref.shape[2] // HG_HEAD
    n_chunks = q_ref.shape[1] // c

    @pl.when(pl.program_id(1) == 0)
    def _():
        st_ref[...] = jnp.zeros_like(st_ref)

    mst = mst_ref[...]
    lb = lb_ref[...]
    ng = ng_ref[...]
    heads = [slice(hd * HG_HEAD, (hd + 1) * HG_HEAD) for hd in range(n_heads)]

    nsub = 2 if n_chunks % 2 == 0 else 1

    def chunk_body(ci, carry):
        r0 = pl.multiple_of(ci * (nsub * c), nsub * c)
        subs = []
        for j in range(nsub):
            rows = pl.ds(r0 + j * c, c)
            q = _silu(q_ref[0, rows, :])
            f = lb + (1.0 - lb) * _sigmoid(f_ref[0, rows, :])
            k = 1.0 - f
            ex = jnp.exp(_dot_exact_lhs(mst, jnp.log(f)))
            subs.append(dict(rows=rows, q=q, k=k, ex=ex, vb=i_ref[0, rows, :].astype(BF16),
                             qd=(q * ex[0:c]).astype(BF16), kr=(k * ex[c:2 * c]).astype(BF16)))
        for sb in subs:
            qb, kb = sb['q'].astype(BF16), sb['k'].astype(BF16)
            sb['sc'] = [lm_ref[0] * _dot(qb[:, ls], kb[:, ls], NT) for ls in heads]
            sb['dqk'] = sb['q'] - sb['k']
        for li in range(len(_HG_LEVELS)):
            for sb in subs:
                g_l = ((sb['k'] + rm_ref[li] * sb['dqk']) * sb['ex'][(2 + li) * c:(3 + li) * c]).astype(BF16)
                sb['sc'] = [s_h + lm_ref[li + 1] * _dot(g_l[:, ls], g_l[:, ls], NT)
                            for s_h, ls in zip(sb['sc'], heads)]
        for sb in subs:
            sb['kv'] = [_dot(sb['vb'][:, ls], sb['kr'][:, ls], TN) for ls in heads]
            sb['o'] = [_dot(s_h.astype(BF16), sb['vb'][:, ls]) for s_h, ls in zip(sb['sc'], heads)]
        sts = [st_ref[hd] for hd in range(n_heads)]
        for sb in subs:
            sb['o'] = [o_h + _dot(sb['qd'][:, ls], st.astype(BF16), NT) for o_h, ls, st in zip(sb['o'], heads, sts)]
            sts = [st * sb['ex'][c - 1:c, ls] + kv for st, ls, kv in zip(sts, heads, sb['kv'])]
        for hd in range(n_heads):
            st_ref[hd] = sts[hd]
        for sb in subs:
            on = [o_h * lax.rsqrt(jnp.mean(o_h * o_h, axis=-1, keepdims=True) + NORM_EPS) for o_h in sb['o']]
            o_full = jnp.concatenate(on, axis=1) * ng
            o_ref[0, sb['rows'], :] = (o_full * _silu(g_ref[0, sb['rows'], :])).astype(o_ref.dtype)
        return carry

    lax.fori_loop(0, n_chunks // nsub, chunk_body, 0)


def _hgrn2(hg, lb, norm_g, ts):
    bsz, s, n4 = hg.shape
    w = n4 // 4
    mst, lm, rm = _hgrn2_consts(w)
    n_heads = w // HG_HEAD
    const2 = lambda b, i: (0, 0)
    const3 = lambda b, i: (0, 0, 0)
    return pl.pallas_call(
        _hgrn2_kernel,
        out_shape=jax.ShapeDtypeStruct((bsz, s, w), BF16),
        grid=(bsz, s // ts),
        in_specs=[pl.BlockSpec((1, ts, w), lambda b, i: (b, i, 0)),
                  pl.BlockSpec((1, ts, w), lambda b, i: (b, i, 1)),
                  pl.BlockSpec((1, ts, w), lambda b, i: (b, i, 2)),
                  pl.BlockSpec((1, ts, w), lambda b, i: (b, i, 3)),
                  pl.BlockSpec((1, w), const2),
                  pl.BlockSpec((1, w), const2),
                  pl.BlockSpec(mst.shape, const2),
                  pl.BlockSpec(lm.shape, const3),
                  pl.BlockSpec(rm.shape, const3)],
        out_specs=pl.BlockSpec((1, ts, w), lambda b, i: (b, i, 0)),
        scratch_shapes=[pltpu.VMEM((n_heads, HG_HEAD, HG_HEAD), F32)],
        compiler_params=_cparams(("arbitrary", "arbitrary")),
        name="hgrn2",
    )(hg, hg, hg, hg, lb.reshape(1, w), norm_g.reshape(1, w), mst, lm, rm)


def _rwkv_consts(width):
    c = CHUNK
    t = np.arange(c)[:, None]
    s = np.arange(c)[None, :]
    tri = np.tile((s <= t).astype(np.float32), (1, 3))
    tt = np.arange(2 * c)[:, None]
    ss = np.arange(2 * c)[None, :]
    same = (tt // c) == (ss // c)
    strict = same & ((ss % c) < (tt % c))
    incl = same & ((ss % c) <= (tt % c))
    hsum = (np.arange(MXU_K)[:, None] // RW_HEAD) == (np.arange(MXU_K)[None, :] // RW_HEAD)
    hsum = np.tile(hsum, (2, 1))
    return (jnp.asarray(tri, BF16), jnp.asarray(strict.astype(np.float32), F32),
            jnp.asarray(incl.astype(np.float32), F32), jnp.asarray(hsum.astype(np.float32), BF16))


def _rwkv7_kernel(p_ref, mu_ref, w0_ref, a0_ref, kk_ref, ka_ref, rk_ref, gnw_ref, gnb_ref,
                  w2_ref, a2_ref, g2_ref, tri_ref, sm_ref, im_ref, hs_ref,
                  o_ref, carry_ref, zt_ref):
    c = CHUNK
    nb = p_ref.shape[0]
    nch = p_ref.shape[1] // c
    width = o_ref.shape[2]
    n_pairs = width // LANES

    @pl.when(pl.program_id(0) == 0)
    def _():
        carry_ref[...] = jnp.zeros_like(carry_ref)
        zt_ref[...] = jnp.zeros_like(zt_ref)

    hs = hs_ref[...]
    tri = tri_ref[...]
    smask = sm_ref[...] > 0
    imask = im_ref[...] > 0
    lane = lax.broadcasted_iota(jnp.int32, (c, LANES), 1)
    m0 = (lane < RW_HEAD).astype(F32)
    m1 = 1.0 - m0

    def stack(x):
        return jnp.concatenate([x * m0, x * m1], axis=0)

    xs_rows = []
    for b in range(nb):
        p = p_ref[b]
        row = lax.broadcasted_iota(jnp.int32, p.shape, 0)
        prev = jnp.where(row == 0, carry_ref[b], pltpu.roll(p, 1, 0))
        carry_ref[b] = p[nch * c - 1:nch * c, :]
        xs_rows.append(p + mu_ref[...] * (prev - p))
    xs = jnp.concatenate(xs_rows, axis=0)
    r_all = xs[:, 0:width]
    k_all = xs[:, width:2 * width]
    v_all = xs[:, 2 * width:3 * width]
    slab = xs[:, 3 * width:]
    nz = -(w0_ref[...] + _dot(jnp.tanh(slab).astype(BF16), w2_ref[...]))
    softplus = jnp.maximum(nz, 0.0) + jnp.log(1.0 + jnp.exp(-jnp.abs(nz)))
    ld_all = -jnp.exp(-softplus - 0.5)
    a_all = _sigmoid(a0_ref[...] + _dot(slab.astype(BF16), a2_ref[...]))
    g_all = _dot(_sigmoid(slab).astype(BF16), g2_ref[...])
    kk0 = k_all * kk_ref[...]
    kk_all = kk0 * lax.rsqrt(jnp.maximum(_head_sums(kk0 * kk0, hs), 1e-24))
    k2_all = k_all * (1.0 + (a_all - 1.0) * ka_ref[...])

    units = []
    for b, j in [(b, j) for b in range(nb) for j in range(nch)]:
        rb = slice((b * nch + j) * c, (b * nch + j + 1) * c)
        r, k2, v, ld = r_all[rb], k2_all[rb], v_all[rb], ld_all[rb]
        a_in = -kk_all[rb]
        b_in = kk_all[rb] * a_all[rb]
        cum = _dot_exact_lhs(tri, ld)
        cum_t = cum[c - 1:c, :]
        e_c = jnp.exp(cum)
        e_nc = jnp.exp(-cum)
        e_rem = jnp.exp(cum_t - cum)
        at_f = a_in * jnp.exp(cum - ld)
        rt_f = r * e_c
        kt_f = k2 * e_nc
        bt_f = b_in * e_nc
        kh_f = k2 * e_rem
        bh_f = b_in * e_rem
        p_t = jnp.exp(cum_t)
        for pi in range(n_pairs):
            ls = slice(pi * LANES, (pi + 1) * LANES)
            units.append(dict(
                b=b, j=j, pi=pi,
                at=stack(at_f[:, ls]).astype(BF16), rt=stack(rt_f[:, ls]).astype(BF16),
                kt=stack(kt_f[:, ls]).astype(BF16), bt=stack(bt_f[:, ls]).astype(BF16),
                kh=stack(kh_f[:, ls]).astype(BF16), bh=stack(bh_f[:, ls]).astype(BF16),
                vs=stack(v[:, ls]).astype(BF16), p_t=p_t[:, ls]))

    for u in units:
        lhs = jnp.concatenate([u['at'], u['rt']], axis=0)
        u['g'] = _dot(lhs, jnp.concatenate([u['kt'], u['bt']], axis=0), NT)
    for u in units:
        g = u.pop('g')
        u['a_ak'] = jnp.where(smask, g[:2 * c, :2 * c], 0.0).astype(BF16)
        u['pw'] = jnp.where(smask, g[:2 * c, 2 * c:], 0.0).astype(BF16)
        u['a_r'] = jnp.where(jnp.concatenate([imask, imask], axis=1), g[2 * c:], 0.0).astype(BF16)
    for u in units:
        akv = _dot(u.pop('a_ak'), u['vs'])
        u['x'] = jnp.concatenate([u['at'].astype(F32), akv], axis=1)
    n_lvl = int(np.log2(c))
    for lvl in range(n_lvl):
        for u in units:
            u['x'] = u['x'] + _dot(u['pw'], u['x'].astype(BF16))
        if lvl + 1 < n_lvl:
            for u in units:
                u['pw'] = _dot(u['pw'], u['pw']).astype(BF16)
    for u in units:
        x = u.pop('x')
        u['wr'] = jnp.concatenate([x[:, :LANES].astype(BF16), u['rt']], axis=0)
        u['u_loc'] = x[:, LANES:]
    zt = {(b, pi): zt_ref[b, pi] for b in range(nb) for pi in range(n_pairs)}
    for j in range(nch):
        tail = [u for u in units if u['j'] == j]
        for u in tail:
            u['uy'] = _dot(u.pop('wr'), zt[u['b'], u['pi']].astype(BF16), NT)
        for u in tail:
            uy = u.pop('uy')
            u['u'] = (uy[:2 * c] + u.pop('u_loc')).astype(BF16)
            u['y0'] = uy[2 * c:]
        for u in tail:
            vu = jnp.concatenate([u['vs'], u['u']], axis=0)
            u['y'] = u.pop('y0') + _dot(u['a_r'], vu)
            upd = _dot(vu, jnp.concatenate([u['kh'], u['bh']], axis=0), TN)
            zt[u['b'], u['pi']] = zt[u['b'], u['pi']] * u['p_t'] + upd
    for (b, pi), z in zt.items():
        zt_ref[b, pi] = z

    inv_n = 1.0 / RW_HEAD
    y = jnp.concatenate(
        [jnp.concatenate([u['y'][:c] + u['y'][c:] for u in units if (u['b'], u['j']) == (b, j)], axis=1)
         for b in range(nb) for j in range(nch)], axis=0)
    mean = _head_sums(y, hs) * inv_n
    d = y - mean
    var = _head_sums(d * d, hs) * inv_n
    yn = d * lax.rsqrt(var + RW_GN_EPS) * gnw_ref[...] + gnb_ref[...]
    bonus = _head_sums(r_all * k2_all * rk_ref[...], hs) * v_all
    out = ((yn + bonus) * g_all).astype(o_ref.dtype)
    for b in range(nb):
        o_ref[b] = out[b * nch * c:(b + 1) * nch * c]


def _rwkv7(rw, mu, w0, w2, a0, a2, g2, k_k, k_a, r_k, gn_w, gn_b, ts):
    bsz, s, cols = rw.shape
    width = w0.shape[-1]
    n_pairs = width // LANES
    slab = cols - 3 * width
    dl, al, gl = w2.shape[0], a2.shape[0], g2.shape[0]
    w2f = jnp.zeros((slab, width), F32).at[0:dl].set(w2).astype(BF16)
    a2f = jnp.zeros((slab, width), F32).at[dl:dl + al].set(a2).astype(BF16)
    g2f = jnp.zeros((slab, width), F32).at[dl + al:dl + al + gl].set(g2).astype(BF16)
    mup = jnp.zeros((1, cols), F32).at[0, :mu.shape[-1]].set(mu)
    tri, sm, im, hs = _rwkv_consts(width)
    row = lambda x: x.reshape(1, width)
    const = lambda i: (0, 0)
    vec = pl.BlockSpec((1, width), const)
    return pl.pallas_call(
        _rwkv7_kernel,
        out_shape=jax.ShapeDtypeStruct((bsz, s, width), BF16),
        grid=(s // ts,),
        in_specs=[pl.BlockSpec((bsz, ts, cols), lambda i: (0, i, 0)),
                  pl.BlockSpec((1, cols), const),
                  vec, vec, vec, vec, vec, vec, vec,
                  pl.BlockSpec((slab, width), const),
                  pl.BlockSpec((slab, width), const),
                  pl.BlockSpec((slab, width), const),
                  pl.BlockSpec(tri.shape, const),
                  pl.BlockSpec(sm.shape, const),
                  pl.BlockSpec(im.shape, const),
                  pl.BlockSpec(hs.shape, const)],
        out_specs=pl.BlockSpec((bsz, ts, width), lambda i: (0, i, 0)),
        scratch_shapes=[pltpu.VMEM((bsz, 1, cols), F32),
                        pltpu.VMEM((bsz, n_pairs, LANES, LANES), F32)],
        compiler_params=_cparams(("arbitrary",)),
        name="rwkv7",
    )(rw, mup, row(w0), row(a0), row(k_k), row(k_a), row(r_k), row(gn_w), row(gn_b),
      w2f, a2f, g2f, tri, sm, im, hs)


def _out_proj_kernel(n_groups, n_experts,
                     x_ref, oa_ref, ob_ref, ga_ref, gb_ref, gt1_ref, sc2_ref, sh2_ref, g2_ref,
                     wa_ref, wb_ref, wo_ref, wr_ref, wrl_ref, br_ref, tril_ref,
                     x1_ref, h2_ref, route_ref, routet_ref, cnt_ref, carry_ref):
    tm = x_ref.shape[1]
    first = (pl.program_id(0) == 0) & (pl.program_id(1) == 0)

    @pl.when(first)
    def _():
        carry_ref[...] = jnp.zeros_like(carry_ref)

    tp = tril_ref.shape[0]
    parts = [dict(rows=slice(r0, r0 + tp)) for r0 in range(0, tm, tp)]
    lane = lax.broadcasted_iota(jnp.int32, (tp, LANES), 1)
    neg = jnp.float32(-jnp.inf)
    big = jnp.int32(1 << 20)
    eg = n_experts // n_groups
    is_g = (lane >= n_experts) & (lane < n_experts + n_groups)
    for pt in parts:
        rows = pt['rows']
        pa = _dot(oa_ref[0, rows, :], wa_ref[...])
        pb = _dot(ob_ref[0, rows, :], wb_ref[...])
        mixed = ga_ref[0, rows, :].astype(F32) * pa + gb_ref[0, rows, :].astype(F32) * pb
        x1 = x_ref[0, rows, :] + gt1_ref[0] * _dot(mixed.astype(BF16), wo_ref[...])
        x1_ref[0, rows, :] = x1
        ms = jnp.mean(x1 * x1, axis=-1, keepdims=True)
        h2 = (x1 * lax.rsqrt(ms + NORM_EPS) * g2_ref[...]) * (1.0 + sc2_ref[0]) + sh2_ref[0]
        _rows_to_tiles(h2_ref.at[pl.ds(rows.start * SUB, tp * SUB), :], h2)
        h2_hi = h2.astype(BF16)
        h2_lo = (h2 - h2_hi.astype(F32)).astype(BF16)
        logits = (_dot(h2_hi, wr_ref[...]) + _dot(h2_lo, wr_ref[...]) + _dot(h2_hi, wrl_ref[...])
                  + br_ref[...])
        lg = jnp.where(is_g, logits, neg)
        mg = jnp.max(lg, axis=-1, keepdims=True)
        p_grp = 1.0 / jnp.sum(jnp.where(is_g, jnp.exp(lg - mg), 0.0), axis=-1, keepdims=True)
        gidx = jnp.min(jnp.where(lg == mg, lane, big), axis=-1, keepdims=True) - n_experts
        sel = (lane >= gidx * eg) & (lane < gidx * eg + eg)
        le = jnp.where(sel, logits, neg)
        me = jnp.max(le, axis=-1, keepdims=True)
        pe_un = jnp.where(sel, jnp.exp(le - me), 0.0)
        pe = jnp.where(sel, pe_un / jnp.sum(pe_un, axis=-1, keepdims=True), -1.0)
        v1 = jnp.max(pe, axis=-1, keepdims=True)
        i1 = jnp.min(jnp.where(pe == v1, lane, big), axis=-1, keepdims=True)
        pe2 = jnp.where(lane == i1, -1.0, pe)
        v2 = jnp.max(pe2, axis=-1, keepdims=True)
        i2 = jnp.min(jnp.where(pe2 == v2, lane, big), axis=-1, keepdims=True)
        wsum = v1 + v2
        pt.update(w1=p_grp * v1 / wsum, w2=p_grp * v2 / wsum, i1=i1, i2=i2,
                  oh1=(lane == i1).astype(F32), oh2=(lane == i2).astype(F32))

    carry = carry_ref[...]
    for pt in parts:
        both = pt['oh1'] + pt['oh2']
        before = _dot(tril_ref[...], both.astype(BF16)) + carry
        rank1 = jnp.sum(pt['oh1'] * before, axis=-1, keepdims=True)
        rank2 = jnp.sum(pt['oh2'] * before, axis=-1, keepdims=True)
        carry = carry + jnp.sum(both, axis=0, keepdims=True)
        out = jnp.where(lane == 0, pt['w1'], 0.0)
        out = jnp.where(lane == 1, pt['w2'], out)
        out = jnp.where(lane == 2, pt['i1'].astype(F32), out)
        out = jnp.where(lane == 3, pt['i2'].astype(F32), out)
        out = jnp.where(lane == 4, rank1, out)
        out = jnp.where(lane == 5, rank2, out)
        route_ref[0, pt['rows'], :] = out
        routet_ref[:, pt['rows']] = out.T[0:SUBLANES, :]
    carry_ref[...] = carry
    cnt_ref[...] = carry


def _out_proj(x, o_a, o_b, gates, mod3, norm2_g, wa, wb, wo, wr, br, n_groups, n_experts, tm):
    bsz, s, d = x.shape
    wdt = o_a.shape[-1]
    tp = tm
    tril = jnp.asarray(np.tril(np.ones((tp, tp), np.float32), -1), BF16)
    wr_hi = wr.astype(BF16)
    wr_lo = (wr - wr_hi.astype(F32)).astype(BF16)
    const = lambda b, i: (0, 0)
    tile = lambda b, i: (b, i, 0)
    kern = functools.partial(_out_proj_kernel, n_groups, n_experts)
    return pl.pallas_call(
        kern,
        out_shape=(jax.ShapeDtypeStruct((bsz, s, d), F32),
                   jax.ShapeDtypeStruct((bsz * s * SUB, LANES), U32),
                   jax.ShapeDtypeStruct((bsz, s, LANES), F32),
                   jax.ShapeDtypeStruct((bsz * (s // tm) * SUBLANES, tm), F32),
                   jax.ShapeDtypeStruct((1, LANES), F32)),
        grid=(bsz, s // tm),
        in_specs=[pl.BlockSpec((1, tm, d), tile),
                  pl.BlockSpec((1, tm, wdt), tile),
                  pl.BlockSpec((1, tm, wdt), tile),
                  pl.BlockSpec((1, tm, d), lambda b, i: (b, i, 0)),
                  pl.BlockSpec((1, tm, d), lambda b, i: (b, i, 1)),
                  pl.BlockSpec((1, 1, d), lambda b, i: (b * 6 + 2, 0, 0)),
                  pl.BlockSpec((1, 1, d), lambda b, i: (b * 6 + 4, 0, 0)),
                  pl.BlockSpec((1, 1, d), lambda b, i: (b * 6 + 3, 0, 0)),
                  pl.BlockSpec((1, d), const),
                  pl.BlockSpec(wa.shape, const),
                  pl.BlockSpec(wb.shape, const),
                  pl.BlockSpec(wo.shape, const),
                  pl.BlockSpec(wr.shape, const),
                  pl.BlockSpec(wr.shape, const),
                  pl.BlockSpec((1, LANES), const),
                  pl.BlockSpec((tp, tp), const)],
        out_specs=(pl.BlockSpec((1, tm, d), tile),
                   pl.BlockSpec((tm * SUB, LANES), lambda b, i: (b * (s // tm) + i, 0)),
                   pl.BlockSpec((1, tm, LANES), tile),
                   pl.BlockSpec((SUBLANES, tm), lambda b, i: (b * (s // tm) + i, 0)),
                   pl.BlockSpec((1, LANES), const)),
        scratch_shapes=[pltpu.VMEM((1, LANES), F32)],
        compiler_params=_cparams(("arbitrary", "arbitrary")),
        name="out_proj",
    )(x, o_a, o_b, gates, gates, mod3, mod3, mod3, norm2_g.reshape(1, d), wa, wb, wo, wr_hi, wr_lo, br, tril)


def _moe_dispatch_kernel(tm, tm_o, n_tok, blk, dest_ref, zstart_ref, zcnt_ref, nused_ref, h_ref, xb_ref, smap_ref,
                         zbuf, stage, sem, zsem):
    i = pl.program_id(0)
    n_steps = pl.num_programs(0)
    n_slots = smap_ref.shape[0]
    n_experts = zcnt_ref.shape[0]

    @pl.when(i == 0)
    def _():
        def init(s_, carry):
            smap_ref[s_] = 0
            return carry
        lax.fori_loop(nused_ref[0] * blk, n_slots, init, 0)
        zbuf[...] = jnp.zeros_like(zbuf)

        def zero_row(e, j):
            dst = pl.multiple_of((zstart_ref[e] + j) * SUB, SUB)
            return pltpu.make_async_copy(zbuf.at[pl.ds(0, SUB), :], xb_ref.at[pl.ds(dst, SUB), :], zsem)

        def zero_block(b):
            dst = pl.multiple_of(b * (blk * SUB), blk * SUB)
            return pltpu.make_async_copy(zbuf, xb_ref.at[pl.ds(dst, blk * SUB), :], zsem)

        for e in range(n_experts):
            def zstart(j, carry, e=e):
                zero_row(e, j).start()
                smap_ref[zstart_ref[e] + j] = TOP_K * n_tok + e * blk + j
                return carry
            lax.fori_loop(0, zcnt_ref[e], zstart, 0)

        def bstart(b, carry):
            zero_block(b).start()
            return carry
        lax.fori_loop(nused_ref[0], n_slots // blk, bstart, 0)
        for e in range(n_experts):
            def zwait(j, carry, e=e):
                zero_row(e, j).wait()
                return carry
            lax.fori_loop(0, zcnt_ref[e], zwait, 0)

        def bwait(b, carry):
            zero_block(b).wait()
            return carry
        lax.fori_loop(nused_ref[0], n_slots // blk, bwait, 0)

    slot = i % 2
    stage[slot] = h_ref[...]
    tok0 = i * tm
    dest0 = (tok0 // tm_o) * (TOP_K * tm_o) + tok0 % tm_o
    for r in range(tm):
        tok = tok0 + r
        for k in range(TOP_K):
            d = dest_ref[dest0 + k * tm_o + r]
            dst = pl.multiple_of(d * SUB, SUB)
            pltpu.make_async_copy(stage.at[slot, pl.ds(r * SUB, SUB), :], xb_ref.at[pl.ds(dst, SUB), :],
                                  sem.at[slot]).start(priority=k)
            smap_ref[d] = k * n_tok + tok

    def wait_step(s_):
        for _ in range(TOP_K):
            pltpu.make_async_copy(stage.at[s_], xb_ref.at[pl.ds(0, tm * SUB), :], sem.at[s_]).wait()

    @pl.when(i > 0)
    def _():
        wait_step(1 - slot)

    @pl.when(i == n_steps - 1)
    def _():
        wait_step(slot)


def _moe_dispatch(h2, dest, zstart, zcnt, n_used, n_slots, blk, tm, tm_o):
    n_tok = h2.shape[0] // SUB
    grid_spec = pltpu.PrefetchScalarGridSpec(
        num_scalar_prefetch=4,
        grid=(n_tok // tm,),
        in_specs=[pl.BlockSpec((tm * SUB, LANES), lambda i, *_: (i, 0))],
        out_specs=(pl.BlockSpec(memory_space=pl.ANY),
                   pl.BlockSpec(memory_space=pltpu.SMEM)),
        scratch_shapes=[pltpu.VMEM((blk * SUB, LANES), U32),
                        pltpu.VMEM((2, tm * SUB, LANES), U32),
                        pltpu.SemaphoreType.DMA((2,)),
                        pltpu.SemaphoreType.DMA],
    )
    return pl.pallas_call(
        functools.partial(_moe_dispatch_kernel, tm, tm_o, n_tok, blk),
        out_shape=(jax.ShapeDtypeStruct((n_slots * SUB, LANES), U32),
                   jax.ShapeDtypeStruct((n_slots,), jnp.int32)),
        grid_spec=grid_spec,
        compiler_params=_cparams(("arbitrary",)),
        name="moe_dispatch",
    )(dest, zstart, zcnt, n_used, h2)


def _moe_slotmap_kernel(tm_o, n_tok, blk, dest_ref, zstart_ref, zcnt_ref, nused_ref, smap_ref):
    i = pl.program_id(0)
    n_slots = smap_ref.shape[0]

    @pl.when(i == 0)
    def _():
        def init(s_, carry):
            smap_ref[s_] = TOP_K * n_tok
            return carry
        lax.fori_loop(nused_ref[0] * blk, n_slots, init, 0)
        for e in range(zcnt_ref.shape[0]):
            def pad(j, carry, e=e):
                smap_ref[zstart_ref[e] + j] = TOP_K * n_tok + e * blk + j
                return carry
            lax.fori_loop(0, zcnt_ref[e], pad, 0)

    base = i * (TOP_K * tm_o)
    for k in range(TOP_K):
        for r in range(tm_o):
            smap_ref[dest_ref[base + k * tm_o + r]] = k * n_tok + i * tm_o + r


def _moe_slotmap(dest, zstart, zcnt, n_used, n_slots, blk, tm_o, n_tok):
    grid_spec = pltpu.PrefetchScalarGridSpec(
        num_scalar_prefetch=4,
        grid=(n_tok // tm_o,),
        in_specs=[],
        out_specs=pl.BlockSpec(memory_space=pltpu.SMEM),
    )
    return pl.pallas_call(
        functools.partial(_moe_slotmap_kernel, tm_o, n_tok, blk),
        out_shape=jax.ShapeDtypeStruct((n_slots,), jnp.int32),
        grid_spec=grid_spec,
        compiler_params=_cparams(("arbitrary",)),
        name="moe_slotmap",
    )(dest, zstart, zcnt, n_used)


def _moe_expert_gather_kernel(n_tok, blk_e_ref, nused_ref, smap_ref, h_ref, wg_ref, wu_ref, wd_ref, y_ref,
                              hv, xbuf, ystage, wgb, wub, wdb, sem, hsem):
    i = pl.program_id(0)
    nb = pl.num_programs(0)
    n_used = nused_ref[0]
    blk = xbuf.shape[1] // SUB
    first_real_blocks = TOP_K * n_tok // blk
    first_spare = y_ref.shape[0] // SUB - blk

    def gather(b, slot, rows):
        for r in rows:
            t = smap_ref[b * blk + r]
            tok = jnp.where(t < n_tok, t, jnp.where(t < TOP_K * n_tok, t - n_tok, 0))
            src = pl.multiple_of(tok * SUB, SUB)
            xbuf[slot, pl.ds(r * SUB, SUB), :] = hv[pl.ds(src, SUB), :]

    def issue(b, slot, rows):
        for r in rows:
            t = jnp.where(b >= 0, smap_ref[jnp.maximum(b, 0) * blk + r], first_spare + r)
            dst = pl.multiple_of(t * SUB, SUB)
            pltpu.make_async_copy(ystage.at[slot, pl.ds(r * SUB, SUB), :], y_ref.at[pl.ds(dst, SUB), :],
                                  sem.at[slot]).start(priority=r % 2)

    def wait_block(slot):
        pltpu.make_async_copy(ystage.at[slot], y_ref.at[pl.ds(0, blk * SUB), :], sem.at[slot]).wait()

    @pl.when(i == 0)
    def _():
        load = pltpu.make_async_copy(h_ref, hv, hsem)
        load.start()
        ystage[...] = jnp.zeros_like(ystage)
        n_spare_blocks = y_ref.shape[0] // (blk * SUB) - first_real_blocks

        def spare_copy(c):
            dst = (first_real_blocks + c) * blk * SUB
            return pltpu.make_async_copy(ystage.at[0], y_ref.at[pl.ds(dst, blk * SUB), :], sem.at[0])

        for c in range(n_spare_blocks):
            spare_copy(c).start()
        for c in range(n_spare_blocks):
            spare_copy(c).wait()
        load.wait()
        gather(0, 0, range(blk))

    new_expert = (i == 0) | (blk_e_ref[i] != blk_e_ref[jnp.maximum(i - 1, 0)])

    @pl.when((i < n_used) & new_expert)
    def _():
        wgb[...] = wg_ref[0].astype(BF16)
        wub[...] = wu_ref[0].astype(BF16)
        wdb[...] = wd_ref[0].astype(BF16)

    @pl.when(i < n_used)
    def _():
        slot = i % 2
        pslot = 1 - slot
        nxt = jnp.minimum(i + 1, nb - 1)

        @pl.when(i > 0)
        def _():
            wait_block(slot)

        q = blk // 4
        xb = _tiles_to_rows(xbuf.at[slot], blk).astype(BF16)
        issue(i - 1, pslot, range(0, q))
        gather(nxt, pslot, range(0, q))
        hg = _dot(xb, wgb[...])
        issue(i - 1, pslot, range(q, 2 * q))
        gather(nxt, pslot, range(q, 2 * q))
        hu = _dot(xb, wub[...])
        issue(i - 1, pslot, range(2 * q, 3 * q))
        gather(nxt, pslot, range(2 * q, 3 * q))
        hid = (_silu(hg) * hu).astype(BF16)
        y = _dot(hid, wdb[...])
        issue(i - 1, pslot, range(3 * q, blk))
        gather(nxt, pslot, range(3 * q, blk))
        _rows_to_tiles(ystage.at[slot], y)

        @pl.when(i == n_used - 1)
        def _():
            issue(i, slot, range(blk))
            wait_block(pslot)
            wait_block(slot)


def _moe_experts_gather(h2, smap, blk_e, n_used, w_gate, w_up, w_down, blk, n_tok):
    d = w_gate.shape[1]
    assert d == 2 * SUB * LANES and h2.shape == (n_tok * SUB, LANES)
    n_slots = smap.shape[0]
    nb = n_slots // blk
    f = w_gate.shape[-1]
    n_experts = w_gate.shape[0]
    assert (TOP_K * n_tok) % blk == 0
    n_tiles = TOP_K * n_tok + (n_experts + 1) * blk
    grid_spec = pltpu.PrefetchScalarGridSpec(
        num_scalar_prefetch=3,
        grid=(nb,),
        in_specs=[pl.BlockSpec(memory_space=pl.ANY),
                  pl.BlockSpec((1, d, f), lambda i, be, nu, sm: (be[i], 0, 0)),
                  pl.BlockSpec((1, d, f), lambda i, be, nu, sm: (be[i], 0, 0)),
                  pl.BlockSpec((1, f, d), lambda i, be, nu, sm: (be[i], 0, 0))],
        out_specs=pl.BlockSpec(memory_space=pl.ANY),
        scratch_shapes=[pltpu.VMEM((n_tok * SUB, LANES), U32),
                        pltpu.VMEM((2, blk * SUB, LANES), U32),
                        pltpu.VMEM((2, blk * SUB, LANES), U32),
                        pltpu.VMEM((d, f), BF16),
                        pltpu.VMEM((d, f), BF16),
                        pltpu.VMEM((f, d), BF16),
                        pltpu.SemaphoreType.DMA((2,)),
                        pltpu.SemaphoreType.DMA],
    )
    return pl.pallas_call(
        functools.partial(_moe_expert_gather_kernel, n_tok),
        out_shape=jax.ShapeDtypeStruct((n_tiles * SUB, LANES), U32),
        grid_spec=grid_spec,
        compiler_params=_cparams(("arbitrary",)),
        name="moe_experts",
    )(blk_e, n_used, smap, h2, w_gate, w_up, w_down)


def _moe_expert_kernel(n_tok, blk_e_ref, nused_ref, smap_ref, x_ref, wg_ref, wu_ref, wd_ref, y_ref,
                       ystage, wgb, wub, wdb, sem):
    i = pl.program_id(0)
    n_used = nused_ref[0]
    blk = x_ref.shape[0] // SUB
    first_real_blocks = TOP_K * n_tok // blk
    first_spare = y_ref.shape[0] // SUB - blk

    def issue(b, slot, rows):
        for r in rows:
            t = jnp.where(b >= 0, smap_ref[jnp.maximum(b, 0) * blk + r], first_spare + r)
            dst = pl.multiple_of(t * SUB, SUB)
            pltpu.make_async_copy(ystage.at[slot, pl.ds(r * SUB, SUB), :], y_ref.at[pl.ds(dst, SUB), :],
                                  sem.at[slot]).start(priority=r % 2)

    def wait_block(slot):
        pltpu.make_async_copy(ystage.at[slot], y_ref.at[pl.ds(0, blk * SUB), :], sem.at[slot]).wait()

    @pl.when(i == 0)
    def _():
        ystage[...] = jnp.zeros_like(ystage)
        n_spare_blocks = y_ref.shape[0] // (blk * SUB) - first_real_blocks

        def spare_copy(c):
            dst = (first_real_blocks + c) * blk * SUB
            return pltpu.make_async_copy(ystage.at[0], y_ref.at[pl.ds(dst, blk * SUB), :], sem.at[0])

        for c in range(n_spare_blocks):
            spare_copy(c).start()
        for c in range(n_spare_blocks):
            spare_copy(c).wait()

    new_expert = (i == 0) | (blk_e_ref[i] != blk_e_ref[jnp.maximum(i - 1, 0)])

    @pl.when((i < n_used) & new_expert)
    def _():
        wgb[...] = wg_ref[0].astype(BF16)
        wub[...] = wu_ref[0].astype(BF16)
        wdb[...] = wd_ref[0].astype(BF16)

    @pl.when(i < n_used)
    def _():
        slot = i % 2
        pslot = 1 - slot

        @pl.when(i > 0)
        def _():
            wait_block(slot)

        q = blk // 4
        xb = _tiles_to_rows(x_ref, blk).astype(BF16)
        issue(i - 1, pslot, range(0, q))
        hg = _dot(xb, wgb[...])
        issue(i - 1, pslot, range(q, 2 * q))
        hu = _dot(xb, wub[...])
        issue(i - 1, pslot, range(2 * q, 3 * q))
        hid = (_silu(hg) * hu).astype(BF16)
        y = _dot(hid, wdb[...])
        issue(i - 1, pslot, range(3 * q, blk))
        _rows_to_tiles(ystage.at[slot], y)

        @pl.when(i == n_used - 1)
        def _():
            issue(i, slot, range(blk))
            wait_block(pslot)
            wait_block(slot)


def _moe_experts(xb, smap, blk_e, n_used, w_gate, w_up, w_down, blk, n_tok):
    d = w_gate.shape[1]
    assert d == 2 * SUB * LANES and xb.shape[1] == LANES
    n_slots = smap.shape[0]
    nb = n_slots // blk
    f = w_gate.shape[-1]
    n_experts = w_gate.shape[0]
    assert (TOP_K * n_tok) % blk == 0
    n_tiles = TOP_K * n_tok + (n_experts + 1) * blk
    grid_spec = pltpu.PrefetchScalarGridSpec(
        num_scalar_prefetch=3,
        grid=(nb,),
        in_specs=[pl.BlockSpec((blk * SUB, LANES), lambda i, be, nu, sm: (jnp.minimum(i, nu[0] - 1), 0)),
                  pl.BlockSpec((1, d, f), lambda i, be, nu, sm: (be[i], 0, 0)),
                  pl.BlockSpec((1, d, f), lambda i, be, nu, sm: (be[i], 0, 0)),
                  pl.BlockSpec((1, f, d), lambda i, be, nu, sm: (be[i], 0, 0))],
        out_specs=pl.BlockSpec(memory_space=pl.ANY),
        scratch_shapes=[pltpu.VMEM((2, blk * SUB, LANES), U32),
                        pltpu.VMEM((d, f), BF16),
                        pltpu.VMEM((d, f), BF16),
                        pltpu.VMEM((f, d), BF16),
                        pltpu.SemaphoreType.DMA((2,))],
    )
    return pl.pallas_call(
        functools.partial(_moe_expert_kernel, n_tok),
        out_shape=jax.ShapeDtypeStruct((n_tiles * SUB, LANES), U32),
        grid_spec=grid_spec,
        compiler_params=_cparams(("arbitrary",)),
        name="moe_experts",
    )(blk_e, n_used, smap, xb, w_gate, w_up, w_down)


def _moe_combine_kernel(x1_ref, route_ref, gt2_ref, fg_ref, y1_ref, y2_ref, o_ref):
    tm = x1_ref.shape[0]
    route = route_ref[...]
    moe = route[:, 0:1] * _tiles_to_rows(y1_ref, tm) + route[:, 1:2] * _tiles_to_rows(y2_ref, tm)
    xo = x1_ref[...] + gt2_ref[0] * moe
    ms = jnp.mean(xo * xo, axis=-1, keepdims=True)
    o_ref[...] = xo * lax.rsqrt(ms + NORM_EPS) * fg_ref[...]


def _moe_combine(x1, route, yt, mod3, final_g, s, tm):
    n, d = x1.shape
    tiles_per_batch = s // tm
    n_steps = n // tm
    return pl.pallas_call(
        _moe_combine_kernel,
        out_shape=jax.ShapeDtypeStruct((n, d), F32),
        grid=(n_steps,),
        in_specs=[pl.BlockSpec((tm, d), lambda i: (i, 0)),
                  pl.BlockSpec((tm, LANES), lambda i: (i, 0)),
                  pl.BlockSpec((1, 1, d), lambda i: ((i // tiles_per_batch) * 6 + 5, 0, 0)),
                  pl.BlockSpec((1, d), lambda i: (0, 0)),
                  pl.BlockSpec((tm * SUB, LANES), lambda i: (i, 0)),
                  pl.BlockSpec((tm * SUB, LANES), lambda i: (n_steps + i, 0))],
        out_specs=pl.BlockSpec((tm, d), lambda i: (i, 0)),
        compiler_params=_cparams(("arbitrary",)),
        name="moe_combine",
    )(x1, route, mod3, final_g.reshape(1, d), yt, yt)


def _pick(n, candidates):
    for t in candidates:
        if n % t == 0:
            return t
    raise ValueError(f"no tile in {candidates} divides {n}")


def kernel(x, c, ada_w, ada_b, norm1_g, w_in, hg_lb, hg_norm_g, rw_mu, rw_w0, rw_w2, rw_a0, rw_a2, rw_g2, rw_kk, rw_ka, rw_rk, rw_gn_w, rw_gn_b, w_proj_a, w_proj_b, w_out, norm2_g, router_g_w, router_g_b, router_e_w, router_e_b, exp_w_gate, exp_w_up, exp_w_down, final_g):
    bsz, s, d = x.shape
    depth = ada_w.shape[0]
    hg_f = hg_lb.shape[-1]
    hg_w = hg_norm_g.shape[-1]
    rw_w = rw_w0.shape[-1]
    rw_cols = rw_mu.shape[-1]
    n_groups = router_g_w.shape[-1]
    n_experts = router_e_w.shape[-1]
    assert hg_f == hg_w and s % CHUNK == 0 and n_experts + n_groups <= LANES and d == 2 * SUB * LANES

    lb_all = jnp.cumsum(jax.nn.softmax(hg_lb.astype(F32), axis=0), axis=0)
    n = bsz * s
    blk = 256
    n_blocks = (n * TOP_K + n_experts * blk) // blk
    for l in range(depth):
        mod = _ada_mod(c, ada_w[l], ada_b[l])
        mod3 = mod.reshape(bsz * 6, 1, d)

        hg_cols = 2 * hg_f + 2 * hg_w
        rw_pad = -(-rw_cols // 256) * 256
        wl = w_in[l]
        w_hg = wl[:, :hg_cols].astype(BF16)
        w_rw = jnp.zeros((d, rw_pad), BF16).at[:, :rw_cols].set(wl[:, hg_cols:hg_cols + rw_cols].astype(BF16))
        w_gt = wl[:, hg_cols + rw_cols:].astype(BF16)
        hg, rw, gates = _in_proj(x, mod3, norm1_g[l], w_hg, w_rw, w_gt, _pick(s, (512, 256, 128, 64)))

        o_a = _hgrn2(hg, lb_all[l], hg_norm_g[l], _pick(s, (512, 256, 128, 64)))
        o_b = _rwkv7(rw, rw_mu[l], rw_w0[l], rw_w2[l], rw_a0[l], rw_a2[l], rw_g2[l],
                     rw_kk[l], rw_ka[l], rw_rk[l].reshape(-1), rw_gn_w[l], rw_gn_b[l],
                     _pick(s, (2 * CHUNK, CHUNK)))

        wr = jnp.zeros((d, LANES), F32).at[:, :n_experts].set(router_e_w[l])
        wr = wr.at[:, n_experts:n_experts + n_groups].set(router_g_w[l])
        br = jnp.zeros((1, LANES), F32).at[0, :n_experts].set(router_e_b[l])
        br = br.at[0, n_experts:n_experts + n_groups].set(router_g_b[l])
        tm_o = _pick(s, (512, 256, 128, 64))
        x1, h2, route, route_t, counts = _out_proj(
            x, o_a, o_b, gates, mod3, norm2_g[l],
            w_proj_a[l].astype(BF16), w_proj_b[l].astype(BF16), w_out[l].astype(BF16),
            wr, br, n_groups, n_experts, tm_o)

        route2 = route.reshape(n, LANES)
        rt = route_t.reshape(n // tm_o, SUBLANES, tm_o)
        eid = rt[:, 2:4, :].astype(jnp.int32)
        rank = rt[:, 4:6, :].astype(jnp.int32)
        cnt = counts[0, :n_experts].astype(jnp.int32)
        padded = (cnt + blk - 1) // blk * blk
        pad_end = jnp.cumsum(padded)
        pad_start = pad_end - padded
        e_ax = jnp.arange(n_experts, dtype=jnp.int32)[:, None, None, None]
        dest = rank + jnp.sum(jnp.where(eid[None] == e_ax, pad_start[:, None, None, None], 0), axis=0)
        blk_start = jnp.arange(n_blocks, dtype=jnp.int32) * blk
        blk_e = jnp.minimum(jnp.sum((pad_end[None, :] <= blk_start[:, None]).astype(jnp.int32), axis=1),
                            n_experts - 1)
        n_used = (pad_end[-1:] // blk).astype(jnp.int32)
        tm = _pick(s, (256, 128, 64))
        smap = _moe_slotmap(dest.reshape(-1), pad_start + cnt, padded - cnt, n_used, n_blocks * blk, blk, tm_o, n)
        yt = _moe_experts_gather(h2, smap, blk_e, n_used, exp_w_gate[l], exp_w_up[l], exp_w_down[l], blk, n)
        last = l == depth - 1
        assert last, "the final RMSNorm is fused into the last layer's combine"
        out = _moe_combine(x1.reshape(n, d), route2, yt, mod3, final_g, s, tm)
        x = out.reshape(bsz, s, d)
    return x
```

```python
import functools

import numpy as np
import jax
import jax.numpy as jnp
from jax import lax
from jax.experimental import pallas as pl
from jax.experimental.pallas import tpu as pltpu

F32 = jnp.float32
BF16 = jnp.bfloat16
HIGHEST = lax.Precision.HIGHEST

NORM_EPS = 1e-6
HG_HEAD = 128
RW_HEAD = 64
RW_GN_EPS = 64e-5
TOP_K = 2
CHUNK = 64
LANES = 128
SUB = 4
U32 = jnp.uint32
MXU_K = 256
SUBLANES = 8
VMEM_LIMIT = 56 * 1024 * 1024

NT = (((1,), (1,)), ((), ()))
TN = (((0,), (0,)), ((), ()))


def _dot(a, b, dims=None, precision=None):
    if dims is None:
        return jnp.dot(a, b, preferred_element_type=F32, precision=precision)
    return lax.dot_general(a, b, dims, preferred_element_type=F32, precision=precision)


def _split3(x):
    hi = x.astype(BF16)
    r1 = x - hi.astype(F32)
    mid = r1.astype(BF16)
    lo = (r1 - mid.astype(F32)).astype(BF16)
    return hi, mid, lo


def _dot_exact_lhs(m3_bf16, x):
    return _dot(m3_bf16, jnp.concatenate(_split3(x), axis=0))


def _head_sums(x, m2_bf16):
    outs = []
    for g0 in range(0, x.shape[1], MXU_K):
        xg = x[:, g0:g0 + MXU_K]
        hi = xg.astype(BF16)
        lo = (xg - hi.astype(F32)).astype(BF16)
        outs.append(_dot(jnp.concatenate([hi, lo], axis=1), m2_bf16))
    return jnp.concatenate(outs, axis=1)


def _sigmoid(x):
    return 1.0 / (1.0 + jnp.exp(-x))


def _silu(x):
    return x * _sigmoid(x)


def _rows_to_tiles(ref, val):
    m, half = val.shape[0], val.shape[1] // 2
    hi = lax.bitcast_convert_type(val[:, :half].astype(BF16).astype(F32), U32)
    lo = lax.bitcast_convert_type(val[:, half:].astype(BF16).astype(F32), U32)
    w = (hi & jnp.uint32(0xFFFF0000)) | (lo >> 16)
    for j in range(SUB):
        ref[pl.ds(j, m, stride=SUB), :] = w[:, j * LANES:(j + 1) * LANES]


def _tiles_to_rows(ref, m, base=0):
    w = jnp.concatenate([ref[pl.ds(base * SUB + j, m, stride=SUB), :] for j in range(SUB)], axis=1)
    hi = lax.bitcast_convert_type(w & jnp.uint32(0xFFFF0000), F32)
    lo = lax.bitcast_convert_type(w << 16, F32)
    return jnp.concatenate([hi, lo], axis=1)


def _cparams(sem):
    return pltpu.CompilerParams(dimension_semantics=sem, vmem_limit_bytes=VMEM_LIMIT)


def _ada_kernel(c_ref, w_ref, b_ref, o_ref):
    c = c_ref[...]
    o_ref[...] = _dot(_silu(c), w_ref[...], precision=HIGHEST) + b_ref[...]


def _ada_mod(c, w, b):
    bsz, d = c.shape
    n = w.shape[1]
    rows = 8
    cp = jnp.zeros((rows, d), F32).at[:bsz].set(c)
    tn = 1536
    out = pl.pallas_call(
        _ada_kernel,
        out_shape=jax.ShapeDtypeStruct((rows, n), F32),
        grid=(n // tn,),
        in_specs=[pl.BlockSpec((rows, d), lambda j: (0, 0)),
                  pl.BlockSpec((d, tn), lambda j: (0, j)),
                  pl.BlockSpec((1, tn), lambda j: (0, j))],
        out_specs=pl.BlockSpec((rows, tn), lambda j: (0, j)),
        compiler_params=_cparams(("arbitrary",)),
        name="ada_mod",
    )(cp, w, b.reshape(1, n))
    return out[:bsz]


def _in_proj_kernel(x_ref, sh_ref, sc_ref, g_ref, whg_ref, wrw_ref, wgt_ref, hg_ref, rw_ref, gt_ref):
    x = x_ref[0]
    ms = jnp.mean(x * x, axis=-1, keepdims=True)
    h = (x * lax.rsqrt(ms + NORM_EPS) * g_ref[...]) * (1.0 + sc_ref[0]) + sh_ref[0]
    hb = h.astype(BF16)
    step = 512
    for n0 in range(0, whg_ref.shape[1], step):
        hg_ref[0, :, n0:n0 + step] = _dot(hb, whg_ref[:, n0:n0 + step])
    for n0 in range(0, wrw_ref.shape[1], 256):
        rw_ref[0, :, n0:n0 + 256] = _dot(hb, wrw_ref[:, n0:n0 + 256])
    for n0 in range(0, wgt_ref.shape[1], step):
        gt_ref[0, :, n0:n0 + step] = _sigmoid(_dot(hb, wgt_ref[:, n0:n0 + step])).astype(BF16)


def _in_proj(x, mod3, norm_g, w_hg, w_rw, w_gt, tm):
    bsz, s, d = x.shape
    n_hg, n_rw, n_gt = w_hg.shape[1], w_rw.shape[1], w_gt.shape[1]
    const = lambda b, i: (0, 0)
    return pl.pallas_call(
        _in_proj_kernel,
        out_shape=(jax.ShapeDtypeStruct((bsz, s, n_hg), F32),
                   jax.ShapeDtypeStruct((bsz, s, n_rw), F32),
                   jax.ShapeDtypeStruct((bsz, s, n_gt), BF16)),
        grid=(bsz, s // tm),
        in_specs=[pl.BlockSpec((1, tm, d), lambda b, i: (b, i, 0)),
                  pl.BlockSpec((1, 1, d), lambda b, i: (b * 6 + 0, 0, 0)),
                  pl.BlockSpec((1, 1, d), lambda b, i: (b * 6 + 1, 0, 0)),
                  pl.BlockSpec((1, d), const),
                  pl.BlockSpec((d, n_hg), const),
                  pl.BlockSpec((d, n_rw), const),
                  pl.BlockSpec((d, n_gt), const)],
        out_specs=(pl.BlockSpec((1, tm, n_hg), lambda b, i: (b, i, 0)),
                   pl.BlockSpec((1, tm, n_rw), lambda b, i: (b, i, 0)),
                   pl.BlockSpec((1, tm, n_gt), lambda b, i: (b, i, 0))),
        compiler_params=_cparams(("arbitrary", "arbitrary")),
        name="in_proj",
    )(x, mod3, mod3, norm_g.reshape(1, d), w_hg, w_rw, w_gt)


_HG_LEVELS = (32, 16, 8, 4, 2, 1)


def _hgrn2_consts(width):
    c = CHUNK
    t = np.arange(c)[:, None]
    s = np.arange(c)[None, :]
    blocks = [(s <= t), (s > t)]
    lvl_masks = []
    right = []
    for h in _HG_LEVELS:
        m = (t // (2 * h)) * 2 * h + h
        is_r = (t & h) != 0
        blk = np.where(is_r, (s >= m) & (s <= t), (s > t) & (s <= m - 1))
        blocks.append(blk)
        lvl_masks.append(is_r & ((s & h) == 0) & ((t // (2 * h)) == (s // (2 * h))))
        right.append(np.broadcast_to(is_r, (c, width)))
    mst = np.tile(np.concatenate(blocks, axis=0).astype(np.float32), (1, 3))
    lm = np.stack([np.eye(c, dtype=bool)] + lvl_masks).astype(np.float32)
    rm = np.stack(right).astype(np.float32)
    return jnp.asarray(mst, BF16), jnp.asarray(lm, F32), jnp.asarray(rm, F32)


def _hgrn2_kernel(q_ref, f_ref, i_ref, g_ref, lb_ref, ng_ref, mst_ref, lm_ref, rm_ref, o_ref, st_ref):
    c = CHUNK
    n_chunks = q_ref.shape[1] // c

    @pl.when(pl.program_id(1) == 0)
    def _():
        st_ref[...] = jnp.zeros_like(st_ref)

    nsub = 2 if n_chunks % 2 == 0 else 1

    def chunk_body(ci, carry):
        r0 = pl.multiple_of(ci * (nsub * c), nsub * c)
        for _ in _hgrn2_steps(q_ref, f_ref, i_ref, g_ref, lb_ref, ng_ref, mst_ref, lm_ref, rm_ref, o_ref, st_ref,
                              0, r0, nsub):
            pass
        return carry

    lax.fori_loop(0, n_chunks // nsub, chunk_body, 0)


def _hgrn2_steps(q_ref, f_ref, i_ref, g_ref, lb_ref, ng_ref, mst_ref, lm_ref, rm_ref, o_ref, st_ref, b, r0, nsub):
    c = CHUNK
    n_heads = q_ref.shape[2] // HG_HEAD
    mst = mst_ref[...]
    lb = lb_ref[...]
    ng = ng_ref[...]
    heads = [slice(hd * HG_HEAD, (hd + 1) * HG_HEAD) for hd in range(n_heads)]
    subs = []
    for j in range(nsub):
        rows = pl.ds(r0 + j * c, c)
        q = _silu(q_ref[b, rows, :])
        f = lb + (1.0 - lb) * _sigmoid(f_ref[b, rows, :])
        k = 1.0 - f
        ex = jnp.exp(_dot_exact_lhs(mst, jnp.log(f)))
        subs.append(dict(rows=rows, q=q, k=k, ex=ex, vb=i_ref[b, rows, :].astype(BF16),
                         qd=(q * ex[0:c]).astype(BF16), kr=(k * ex[c:2 * c]).astype(BF16)))
        yield
    for sb in subs:
        qb, kb = sb['q'].astype(BF16), sb['k'].astype(BF16)
        sb['sc'] = [lm_ref[0] * _dot(qb[:, ls], kb[:, ls], NT) for ls in heads]
        sb['dqk'] = sb['q'] - sb['k']
    yield
    for li in range(len(_HG_LEVELS)):
        for sb in subs:
            g_l = ((sb['k'] + rm_ref[li] * sb['dqk']) * sb['ex'][(2 + li) * c:(3 + li) * c]).astype(BF16)
            sb['sc'] = [s_h + lm_ref[li + 1] * _dot(g_l[:, ls], g_l[:, ls], NT)
                        for s_h, ls in zip(sb['sc'], heads)]
        yield
    for sb in subs:
        sb['kv'] = [_dot(sb['vb'][:, ls], sb['kr'][:, ls], TN) for ls in heads]
        sb['o'] = [_dot(s_h.astype(BF16), sb['vb'][:, ls]) for s_h, ls in zip(sb['sc'], heads)]
        yield
    sts = [st_ref[b, hd] for hd in range(n_heads)]
    for sb in subs:
        sb['o'] = [o_h + _dot(sb['qd'][:, ls], st.astype(BF16), NT) for o_h, ls, st in zip(sb['o'], heads, sts)]
        sts = [st * sb['ex'][c - 1:c, ls] + kv for st, ls, kv in zip(sts, heads, sb['kv'])]
        yield
    for hd in range(n_heads):
        st_ref[b, hd] = sts[hd]
    for sb in subs:
        on = [o_h * lax.rsqrt(jnp.mean(o_h * o_h, axis=-1, keepdims=True) + NORM_EPS) for o_h in sb['o']]
        o_full = jnp.concatenate(on, axis=1) * ng
        o_ref[b, sb['rows'], :] = (o_full * _silu(g_ref[b, sb['rows'], :])).astype(o_ref.dtype)
        yield


def _hgrn2(hg, lb, norm_g, ts):
    bsz, s, n4 = hg.shape
    w = n4 // 4
    mst, lm, rm = _hgrn2_consts(w)
    n_heads = w // HG_HEAD
    const2 = lambda b, i: (0, 0)
    const3 = lambda b, i: (0, 0, 0)
    return pl.pallas_call(
        _hgrn2_kernel,
        out_shape=jax.ShapeDtypeStruct((bsz, s, w), BF16),
        grid=(bsz, s // ts),
        in_specs=[pl.BlockSpec((1, ts, w), lambda b, i: (b, i, 0)),
                  pl.BlockSpec((1, ts, w), lambda b, i: (b, i, 1)),
                  pl.BlockSpec((1, ts, w), lambda b, i: (b, i, 2)),
                  pl.BlockSpec((1, ts, w), lambda b, i: (b, i, 3)),
                  pl.BlockSpec((1, w), const2),
                  pl.BlockSpec((1, w), const2),
                  pl.BlockSpec(mst.shape, const2),
                  pl.BlockSpec(lm.shape, const3),
                  pl.BlockSpec(rm.shape, const3)],
        out_specs=pl.BlockSpec((1, ts, w), lambda b, i: (b, i, 0)),
        scratch_shapes=[pltpu.VMEM((1, n_heads, HG_HEAD, HG_HEAD), F32)],
        compiler_params=_cparams(("arbitrary", "arbitrary")),
        name="hgrn2",
    )(hg, hg, hg, hg, lb.reshape(1, w), norm_g.reshape(1, w), mst, lm, rm)


def _rwkv_consts(width):
    c = CHUNK
    t = np.arange(c)[:, None]
    s = np.arange(c)[None, :]
    tri = np.tile((s <= t).astype(np.float32), (1, 3))
    tt = np.arange(2 * c)[:, None]
    ss = np.arange(2 * c)[None, :]
    same = (tt // c) == (ss // c)
    strict = same & ((ss % c) < (tt % c))
    incl = same & ((ss % c) <= (tt % c))
    hsum = (np.arange(MXU_K)[:, None] // RW_HEAD) == (np.arange(MXU_K)[None, :] // RW_HEAD)
    hsum = np.tile(hsum, (2, 1))
    return (jnp.asarray(tri, BF16), jnp.asarray(strict.astype(np.float32), F32),
            jnp.asarray(incl.astype(np.float32), F32), jnp.asarray(hsum.astype(np.float32), BF16))


def _rwkv7_kernel(p_ref, mu_ref, w0_ref, a0_ref, kk_ref, ka_ref, rk_ref, gnw_ref, gnb_ref,
                  w2_ref, a2_ref, g2_ref, tri_ref, sm_ref, im_ref, hs_ref,
                  o_ref, carry_ref, zt_ref):
    @pl.when(pl.program_id(0) == 0)
    def _():
        carry_ref[...] = jnp.zeros_like(carry_ref)
        zt_ref[...] = jnp.zeros_like(zt_ref)

    for _ in _rwkv7_steps(p_ref, mu_ref, w0_ref, a0_ref, kk_ref, ka_ref, rk_ref, gnw_ref, gnb_ref,
                          w2_ref, a2_ref, g2_ref, tri_ref, sm_ref, im_ref, hs_ref, o_ref, carry_ref, zt_ref):
        pass


def _rwkv7_steps(p_ref, mu_ref, w0_ref, a0_ref, kk_ref, ka_ref, rk_ref, gnw_ref, gnb_ref,
                 w2_ref, a2_ref, g2_ref, tri_ref, sm_ref, im_ref, hs_ref, o_ref, carry_ref, zt_ref):
    c = CHUNK
    nb = p_ref.shape[0]
    nch = p_ref.shape[1] // c
    width = o_ref.shape[2]
    n_pairs = width // LANES

    hs = hs_ref[...]
    tri = tri_ref[...]
    smask = sm_ref[...] > 0
    imask = im_ref[...] > 0
    lane = lax.broadcasted_iota(jnp.int32, (c, LANES), 1)
    m0 = (lane < RW_HEAD).astype(F32)
    m1 = 1.0 - m0

    def stack(x):
        return jnp.concatenate([x * m0, x * m1], axis=0)

    xs_rows = []
    for b in range(nb):
        p = p_ref[b]
        row = lax.broadcasted_iota(jnp.int32, p.shape, 0)
        prev = jnp.where(row == 0, carry_ref[b], pltpu.roll(p, 1, 0))
        carry_ref[b] = p[nch * c - 1:nch * c, :]
        xs_rows.append(p + mu_ref[...] * (prev - p))
    xs = jnp.concatenate(xs_rows, axis=0)
    r_all = xs[:, 0:width]
    k_all = xs[:, width:2 * width]
    v_all = xs[:, 2 * width:3 * width]
    slab = xs[:, 3 * width:]
    nz = -(w0_ref[...] + _dot(jnp.tanh(slab).astype(BF16), w2_ref[...]))
    softplus = jnp.maximum(nz, 0.0) + jnp.log(1.0 + jnp.exp(-jnp.abs(nz)))
    ld_all = -jnp.exp(-softplus - 0.5)
    a_all = _sigmoid(a0_ref[...] + _dot(slab.astype(BF16), a2_ref[...]))
    g_all = _dot(_sigmoid(slab).astype(BF16), g2_ref[...])
    kk0 = k_all * kk_ref[...]
    kk_all = kk0 * lax.rsqrt(jnp.maximum(_head_sums(kk0 * kk0, hs), 1e-24))
    k2_all = k_all * (1.0 + (a_all - 1.0) * ka_ref[...])
    yield

    units = []
    for b, j in [(b, j) for b in range(nb) for j in range(nch)]:
        rb = slice((b * nch + j) * c, (b * nch + j + 1) * c)
        r, k2, v, ld = r_all[rb], k2_all[rb], v_all[rb], ld_all[rb]
        a_in = -kk_all[rb]
        b_in = kk_all[rb] * a_all[rb]
        cum = _dot_exact_lhs(tri, ld)
        cum_t = cum[c - 1:c, :]
        e_c = jnp.exp(cum)
        e_nc = jnp.exp(-cum)
        e_rem = jnp.exp(cum_t - cum)
        at_f = a_in * jnp.exp(cum - ld)
        rt_f = r * e_c
        kt_f = k2 * e_nc
        bt_f = b_in * e_nc
        kh_f = k2 * e_rem
        bh_f = b_in * e_rem
        p_t = jnp.exp(cum_t)
        for pi in range(n_pairs):
            ls = slice(pi * LANES, (pi + 1) * LANES)
            units.append(dict(
                b=b, j=j, pi=pi,
                at=stack(at_f[:, ls]).astype(BF16), rt=stack(rt_f[:, ls]).astype(BF16),
                kt=stack(kt_f[:, ls]).astype(BF16), bt=stack(bt_f[:, ls]).astype(BF16),
                kh=stack(kh_f[:, ls]).astype(BF16), bh=stack(bh_f[:, ls]).astype(BF16),
                vs=stack(v[:, ls]).astype(BF16), p_t=p_t[:, ls]))
        yield

    for u in units:
        lhs = jnp.concatenate([u['at'], u['rt']], axis=0)
        u['g'] = _dot(lhs, jnp.concatenate([u['kt'], u['bt']], axis=0), NT)
    yield
    for u in units:
        g = u.pop('g')
        u['a_ak'] = jnp.where(smask, g[:2 * c, :2 * c], 0.0).astype(BF16)
        u['pw'] = jnp.where(smask, g[:2 * c, 2 * c:], 0.0).astype(BF16)
        u['a_r'] = jnp.where(jnp.concatenate([imask, imask], axis=1), g[2 * c:], 0.0).astype(BF16)
    for u in units:
        akv = _dot(u.pop('a_ak'), u['vs'])
        u['x'] = jnp.concatenate([u['at'].astype(F32), akv], axis=1)
    yield
    n_lvl = int(np.log2(c))
    for lvl in range(n_lvl):
        for u in units:
            u['x'] = u['x'] + _dot(u['pw'], u['x'].astype(BF16))
        yield
        if lvl + 1 < n_lvl:
            for u in units:
                u['pw'] = _dot(u['pw'], u['pw']).astype(BF16)
            yield
    for u in units:
        x = u.pop('x')
        u['wr'] = jnp.concatenate([x[:, :LANES].astype(BF16), u['rt']], axis=0)
        u['u_loc'] = x[:, LANES:]
    zt = {(b, pi): zt_ref[b, pi] for b in range(nb) for pi in range(n_pairs)}
    for j in range(nch):
        tail = [u for u in units if u['j'] == j]
        for u in tail:
            u['uy'] = _dot(u.pop('wr'), zt[u['b'], u['pi']].astype(BF16), NT)
        yield
        for u in tail:
            uy = u.pop('uy')
            u['u'] = (uy[:2 * c] + u.pop('u_loc')).astype(BF16)
            u['y0'] = uy[2 * c:]
        for u in tail:
            vu = jnp.concatenate([u['vs'], u['u']], axis=0)
            u['y'] = u.pop('y0') + _dot(u['a_r'], vu)
            upd = _dot(vu, jnp.concatenate([u['kh'], u['bh']], axis=0), TN)
            zt[u['b'], u['pi']] = zt[u['b'], u['pi']] * u['p_t'] + upd
        yield
    for (b, pi), z in zt.items():
        zt_ref[b, pi] = z

    inv_n = 1.0 / RW_HEAD
    y = jnp.concatenate(
        [jnp.concatenate([u['y'][:c] + u['y'][c:] for u in units if (u['b'], u['j']) == (b, j)], axis=1)
         for b in range(nb) for j in range(nch)], axis=0)
    mean = _head_sums(y, hs) * inv_n
    yield
    d = y - mean
    var = _head_sums(d * d, hs) * inv_n
    yield
    yn = d * lax.rsqrt(var + RW_GN_EPS) * gnw_ref[...] + gnb_ref[...]
    bonus = _head_sums(r_all * k2_all * rk_ref[...], hs) * v_all
    out = ((yn + bonus) * g_all).astype(o_ref.dtype)
    for b in range(nb):
        o_ref[b] = out[b * nch * c:(b + 1) * nch * c]


def _rwkv7(rw, mu, w0, w2, a0, a2, g2, k_k, k_a, r_k, gn_w, gn_b, ts):
    bsz, s, cols = rw.shape
    width = w0.shape[-1]
    n_pairs = width // LANES
    slab = cols - 3 * width
    dl, al, gl = w2.shape[0], a2.shape[0], g2.shape[0]
    w2f = jnp.zeros((slab, width), F32).at[0:dl].set(w2).astype(BF16)
    a2f = jnp.zeros((slab, width), F32).at[dl:dl + al].set(a2).astype(BF16)
    g2f = jnp.zeros((slab, width), F32).at[dl + al:dl + al + gl].set(g2).astype(BF16)
    mup = jnp.zeros((1, cols), F32).at[0, :mu.shape[-1]].set(mu)
    tri, sm, im, hs = _rwkv_consts(width)
    row = lambda x: x.reshape(1, width)
    const = lambda i: (0, 0)
    vec = pl.BlockSpec((1, width), const)
    return pl.pallas_call(
        _rwkv7_kernel,
        out_shape=jax.ShapeDtypeStruct((bsz, s, width), BF16),
        grid=(s // ts,),
        in_specs=[pl.BlockSpec((bsz, ts, cols), lambda i: (0, i, 0)),
                  pl.BlockSpec((1, cols), const),
                  vec, vec, vec, vec, vec, vec, vec,
                  pl.BlockSpec((slab, width), const),
                  pl.BlockSpec((slab, width), const),
                  pl.BlockSpec((slab, width), const),
                  pl.BlockSpec(tri.shape, const),
                  pl.BlockSpec(sm.shape, const),
                  pl.BlockSpec(im.shape, const),
                  pl.BlockSpec(hs.shape, const)],
        out_specs=pl.BlockSpec((bsz, ts, width), lambda i: (0, i, 0)),
        scratch_shapes=[pltpu.VMEM((bsz, 1, cols), F32),
                        pltpu.VMEM((bsz, n_pairs, LANES, LANES), F32)],
        compiler_params=_cparams(("arbitrary",)),
        name="rwkv7",
    )(rw, mup, row(w0), row(a0), row(k_k), row(k_a), row(r_k), row(gn_w), row(gn_b),
      w2f, a2f, g2f, tri, sm, im, hs)


def _out_proj_kernel(n_groups, n_experts,
                     x_ref, oa_ref, ob_ref, ga_ref, gb_ref, gt1_ref, sc2_ref, sh2_ref, g2_ref,
                     wa_ref, wb_ref, wo_ref, wr_ref, wrl_ref, br_ref, upper_ref,
                     x1_ref, h2_ref, routet_ref, cnt_ref, carry_ref):
    first = (pl.program_id(0) == 0) & (pl.program_id(1) == 0)

    @pl.when(first)
    def _():
        carry_ref[...] = jnp.zeros_like(carry_ref)

    pa = _dot(oa_ref[0], wa_ref[...])
    pb = _dot(ob_ref[0], wb_ref[...])
    mixed = ga_ref[0].astype(F32) * pa + gb_ref[0].astype(F32) * pb
    x1 = x_ref[0] + gt1_ref[0] * _dot(mixed.astype(BF16), wo_ref[...])
    x1_ref[0] = x1
    ms = jnp.mean(x1 * x1, axis=-1, keepdims=True)
    h2 = (x1 * lax.rsqrt(ms + NORM_EPS) * g2_ref[...]) * (1.0 + sc2_ref[0]) + sh2_ref[0]
    _rows_to_tiles(h2_ref, h2)

    h2_hi = h2.astype(BF16)
    h2_lo = (h2 - h2_hi.astype(F32)).astype(BF16)
    logits = (_dot(wr_ref[...], h2_hi, NT) + _dot(wr_ref[...], h2_lo, NT) + _dot(wrl_ref[...], h2_hi, NT)
              + br_ref[...])
    row = lax.broadcasted_iota(jnp.int32, logits.shape, 0)
    neg = jnp.float32(-jnp.inf)
    big = jnp.int32(1 << 20)
    eg = n_experts // n_groups
    is_g = (row >= n_experts) & (row < n_experts + n_groups)
    lg = jnp.where(is_g, logits, neg)
    mg = jnp.max(lg, axis=0, keepdims=True)
    p_grp = 1.0 / jnp.sum(jnp.where(is_g, jnp.exp(lg - mg), 0.0), axis=0, keepdims=True)
    gidx = jnp.min(jnp.where(lg == mg, row, big), axis=0, keepdims=True) - n_experts
    sel = (row >= gidx * eg) & (row < gidx * eg + eg)
    le = jnp.where(sel, logits, neg)
    me = jnp.max(le, axis=0, keepdims=True)
    pe_un = jnp.where(sel, jnp.exp(le - me), 0.0)
    pe = jnp.where(sel, pe_un / jnp.sum(pe_un, axis=0, keepdims=True), -1.0)
    v1 = jnp.max(pe, axis=0, keepdims=True)
    i1 = jnp.min(jnp.where(pe == v1, row, big), axis=0, keepdims=True)
    pe2 = jnp.where(row == i1, -1.0, pe)
    v2 = jnp.max(pe2, axis=0, keepdims=True)
    i2 = jnp.min(jnp.where(pe2 == v2, row, big), axis=0, keepdims=True)
    wsum = v1 + v2
    w1 = p_grp * v1 / wsum
    w2 = p_grp * v2 / wsum

    oh1 = (row == i1).astype(F32)
    oh2 = (row == i2).astype(F32)
    both = oh1 + oh2
    before = _dot(both.astype(BF16), upper_ref[...]) + carry_ref[...]
    rank1 = jnp.sum(oh1 * before, axis=0, keepdims=True)
    rank2 = jnp.sum(oh2 * before, axis=0, keepdims=True)
    carry_ref[...] = carry_ref[...] + jnp.sum(both, axis=1, keepdims=True)
    cnt_ref[...] = carry_ref[...]
    zero = jnp.zeros_like(w1)
    routet_ref[...] = jnp.concatenate(
        [w1, w2, i1.astype(F32), i2.astype(F32), rank1, rank2, zero, zero], axis=0)


def _out_proj(x, o_a, o_b, gates, mod3, norm2_g, wa, wb, wo, wr, br, n_groups, n_experts, tm):
    bsz, s, d = x.shape
    wdt = o_a.shape[-1]
    upper = jnp.asarray(np.triu(np.ones((tm, tm), np.float32), 1), BF16)
    wrt = wr.T
    wr_hi = wrt.astype(BF16)
    wr_lo = (wrt - wr_hi.astype(F32)).astype(BF16)
    const = lambda b, i: (0, 0)
    tile = lambda b, i: (b, i, 0)
    kern = functools.partial(_out_proj_kernel, n_groups, n_experts)
    return pl.pallas_call(
        kern,
        out_shape=(jax.ShapeDtypeStruct((bsz, s, d), F32),
                   jax.ShapeDtypeStruct((bsz * s * SUB, LANES), U32),
                   jax.ShapeDtypeStruct((bsz * (s // tm) * SUBLANES, tm), F32),
                   jax.ShapeDtypeStruct((LANES, 1), F32)),
        grid=(bsz, s // tm),
        in_specs=[pl.BlockSpec((1, tm, d), tile),
                  pl.BlockSpec((1, tm, wdt), tile),
                  pl.BlockSpec((1, tm, wdt), tile),
                  pl.BlockSpec((1, tm, d), lambda b, i: (b, i, 0)),
                  pl.BlockSpec((1, tm, d), lambda b, i: (b, i, 1)),
                  pl.BlockSpec((1, 1, d), lambda b, i: (b * 6 + 2, 0, 0)),
                  pl.BlockSpec((1, 1, d), lambda b, i: (b * 6 + 4, 0, 0)),
                  pl.BlockSpec((1, 1, d), lambda b, i: (b * 6 + 3, 0, 0)),
                  pl.BlockSpec((1, d), const),
                  pl.BlockSpec(wa.shape, const),
                  pl.BlockSpec(wb.shape, const),
                  pl.BlockSpec(wo.shape, const),
                  pl.BlockSpec(wrt.shape, const),
                  pl.BlockSpec(wrt.shape, const),
                  pl.BlockSpec((LANES, 1), const),
                  pl.BlockSpec((tm, tm), const)],
        out_specs=(pl.BlockSpec((1, tm, d), tile),
                   pl.BlockSpec((tm * SUB, LANES), lambda b, i: (b * (s // tm) + i, 0)),
                   pl.BlockSpec((SUBLANES, tm), lambda b, i: (b * (s // tm) + i, 0)),
                   pl.BlockSpec((LANES, 1), const)),
        scratch_shapes=[pltpu.VMEM((LANES, 1), F32)],
        compiler_params=_cparams(("arbitrary", "arbitrary")),
        name="out_proj",
    )(x, o_a, o_b, gates, gates, mod3, mod3, mod3, norm2_g.reshape(1, d), wa, wb, wo, wr_hi, wr_lo,
      br.reshape(LANES, 1), upper)


def _moe_dispatch_kernel(tm, tm_o, n_tok, blk, dest_ref, zstart_ref, zcnt_ref, nused_ref, h_ref, xb_ref, smap_ref,
                         zbuf, stage, sem, zsem):
    i = pl.program_id(0)
    n_steps = pl.num_programs(0)
    n_slots = smap_ref.shape[0]
    n_experts = zcnt_ref.shape[0]

    @pl.when(i == 0)
    def _():
        def init(s_, carry):
            smap_ref[s_] = 0
            return carry
        lax.fori_loop(nused_ref[0] * blk, n_slots, init, 0)
        zbuf[...] = jnp.zeros_like(zbuf)

        def zero_row(e, j):
            dst = pl.multiple_of((zstart_ref[e] + j) * SUB, SUB)
            return pltpu.make_async_copy(zbuf.at[pl.ds(0, SUB), :], xb_ref.at[pl.ds(dst, SUB), :], zsem)

        def zero_block(b):
            dst = pl.multiple_of(b * (blk * SUB), blk * SUB)
            return pltpu.make_async_copy(zbuf, xb_ref.at[pl.ds(dst, blk * SUB), :], zsem)

        for e in range(n_experts):
            def zstart(j, carry, e=e):
                zero_row(e, j).start()
                smap_ref[zstart_ref[e] + j] = TOP_K * n_tok + e * blk + j
                return carry
            lax.fori_loop(0, zcnt_ref[e], zstart, 0)

        def bstart(b, carry):
            zero_block(b).start()
            return carry
        lax.fori_loop(nused_ref[0], n_slots // blk, bstart, 0)
        for e in range(n_experts):
            def zwait(j, carry, e=e):
                zero_row(e, j).wait()
                return carry
            lax.fori_loop(0, zcnt_ref[e], zwait, 0)

        def bwait(b, carry):
            zero_block(b).wait()
            return carry
        lax.fori_loop(nused_ref[0], n_slots // blk, bwait, 0)

    slot = i % 2
    stage[slot] = h_ref[...]
    tok0 = i * tm
    dest0 = (tok0 // tm_o) * (TOP_K * tm_o) + tok0 % tm_o
    for r in range(tm):
        tok = tok0 + r
        for k in range(TOP_K):
            d = dest_ref[dest0 + k * tm_o + r]
            dst = pl.multiple_of(d * SUB, SUB)
            pltpu.make_async_copy(stage.at[slot, pl.ds(r * SUB, SUB), :], xb_ref.at[pl.ds(dst, SUB), :],
                                  sem.at[slot]).start(priority=k)
            smap_ref[d] = k * n_tok + tok

    def wait_step(s_):
        for _ in range(TOP_K):
            pltpu.make_async_copy(stage.at[s_], xb_ref.at[pl.ds(0, tm * SUB), :], sem.at[s_]).wait()

    @pl.when(i > 0)
    def _():
        wait_step(1 - slot)

    @pl.when(i == n_steps - 1)
    def _():
        wait_step(slot)


def _moe_dispatch(h2, dest, zstart, zcnt, n_used, n_slots, blk, tm, tm_o):
    n_tok = h2.shape[0] // SUB
    grid_spec = pltpu.PrefetchScalarGridSpec(
        num_scalar_prefetch=4,
        grid=(n_tok // tm,),
        in_specs=[pl.BlockSpec((tm * SUB, LANES), lambda i, *_: (i, 0))],
        out_specs=(pl.BlockSpec(memory_space=pl.ANY),
                   pl.BlockSpec(memory_space=pltpu.SMEM)),
        scratch_shapes=[pltpu.VMEM((blk * SUB, LANES), U32),
                        pltpu.VMEM((2, tm * SUB, LANES), U32),
                        pltpu.SemaphoreType.DMA((2,)),
                        pltpu.SemaphoreType.DMA],
    )
    return pl.pallas_call(
        functools.partial(_moe_dispatch_kernel, tm, tm_o, n_tok, blk),
        out_shape=(jax.ShapeDtypeStruct((n_slots * SUB, LANES), U32),
                   jax.ShapeDtypeStruct((n_slots,), jnp.int32)),
        grid_spec=grid_spec,
        compiler_params=_cparams(("arbitrary",)),
        name="moe_dispatch",
    )(dest, zstart, zcnt, n_used, h2)


def _moe_slotmap_kernel(tm_o, n_tok, blk, dest_ref, zstart_ref, zcnt_ref, nused_ref, smap_ref):
    i = pl.program_id(0)
    n_slots = smap_ref.shape[0]

    @pl.when(i == 0)
    def _():
        def init(s_, carry):
            smap_ref[s_] = TOP_K * n_tok
            return carry
        lax.fori_loop(nused_ref[0] * blk, n_slots, init, 0)
        for e in range(zcnt_ref.shape[0]):
            def pad(j, carry, e=e):
                smap_ref[zstart_ref[e] + j] = TOP_K * n_tok + e * blk + j
                return carry
            lax.fori_loop(0, zcnt_ref[e], pad, 0)

    base = i * (TOP_K * tm_o)
    for k in range(TOP_K):
        for r in range(tm_o):
            smap_ref[dest_ref[base + k * tm_o + r]] = k * n_tok + i * tm_o + r


def _moe_slotmap(dest, zstart, zcnt, n_used, n_slots, blk, tm_o, n_tok):
    grid_spec = pltpu.PrefetchScalarGridSpec(
        num_scalar_prefetch=4,
        grid=(n_tok // tm_o,),
        in_specs=[],
        out_specs=pl.BlockSpec(memory_space=pltpu.SMEM),
    )
    return pl.pallas_call(
        functools.partial(_moe_slotmap_kernel, tm_o, n_tok, blk),
        out_shape=jax.ShapeDtypeStruct((n_slots,), jnp.int32),
        grid_spec=grid_spec,
        compiler_params=_cparams(("arbitrary",)),
        name="moe_slotmap",
    )(dest, zstart, zcnt, n_used)


def _moe_expert_gather_kernel(n_tok, blk_e_ref, nused_ref, smap_ref, h_ref, wg_ref, wu_ref, wd_ref, y_ref,
                              hv, xbuf, ystage, wgb, wub, wdb, sem, hsem):
    i = pl.program_id(0)
    nb = pl.num_programs(0)
    n_used = nused_ref[0]
    blk = xbuf.shape[1] // SUB
    first_real_blocks = TOP_K * n_tok // blk
    first_spare = y_ref.shape[0] // SUB - blk

    def gather(b, slot, rows):
        for r in rows:
            t = smap_ref[b * blk + r]
            tok = jnp.where(t < n_tok, t, jnp.where(t < TOP_K * n_tok, t - n_tok, 0))
            src = pl.multiple_of(tok * SUB, SUB)
            xbuf[slot, pl.ds(r * SUB, SUB), :] = hv[pl.ds(src, SUB), :]

    def issue(b, slot, rows):
        for r in rows:
            t = jnp.where(b >= 0, smap_ref[jnp.maximum(b, 0) * blk + r], first_spare + r)
            dst = pl.multiple_of(t * SUB, SUB)
            pltpu.make_async_copy(ystage.at[slot, pl.ds(r * SUB, SUB), :], y_ref.at[pl.ds(dst, SUB), :],
                                  sem.at[slot]).start(priority=r % 2)

    def wait_block(slot):
        pltpu.make_async_copy(ystage.at[slot], y_ref.at[pl.ds(0, blk * SUB), :], sem.at[slot]).wait()

    @pl.when(i == 0)
    def _():
        load = pltpu.make_async_copy(h_ref, hv, hsem)
        load.start()
        ystage[...] = jnp.zeros_like(ystage)
        n_spare_blocks = y_ref.shape[0] // (blk * SUB) - first_real_blocks

        def spare_copy(c):
            dst = (first_real_blocks + c) * blk * SUB
            return pltpu.make_async_copy(ystage.at[0], y_ref.at[pl.ds(dst, blk * SUB), :], sem.at[0])

        for c in range(n_spare_blocks):
            spare_copy(c).start()
        for c in range(n_spare_blocks):
            spare_copy(c).wait()
        load.wait()
        gather(0, 0, range(blk))

    new_expert = (i == 0) | (blk_e_ref[i] != blk_e_ref[jnp.maximum(i - 1, 0)])

    @pl.when((i < n_used) & new_expert)
    def _():
        wgb[...] = wg_ref[0].astype(BF16)
        wub[...] = wu_ref[0].astype(BF16)
        wdb[...] = wd_ref[0].astype(BF16)

    @pl.when(i < n_used)
    def _():
        slot = i % 2
        pslot = 1 - slot
        nxt = jnp.minimum(i + 1, nb - 1)

        @pl.when(i > 0)
        def _():
            wait_block(slot)

        q = blk // 4
        xb = _tiles_to_rows(xbuf.at[slot], blk).astype(BF16)
        issue(i - 1, pslot, range(0, q))
        gather(nxt, pslot, range(0, q))
        hg = _dot(xb, wgb[...])
        issue(i - 1, pslot, range(q, 2 * q))
        gather(nxt, pslot, range(q, 2 * q))
        hu = _dot(xb, wub[...])
        issue(i - 1, pslot, range(2 * q, 3 * q))
        gather(nxt, pslot, range(2 * q, 3 * q))
        hid = (_silu(hg) * hu).astype(BF16)
        y = _dot(hid, wdb[...])
        issue(i - 1, pslot, range(3 * q, blk))
        gather(nxt, pslot, range(3 * q, blk))
        _rows_to_tiles(ystage.at[slot], y)

        @pl.when(i == n_used - 1)
        def _():
            issue(i, slot, range(blk))
            wait_block(pslot)
            wait_block(slot)


def _moe_experts_gather(h2, smap, blk_e, n_used, w_gate, w_up, w_down, blk, n_tok):
    d = w_gate.shape[1]
    assert d == 2 * SUB * LANES and h2.shape == (n_tok * SUB, LANES)
    n_slots = smap.shape[0]
    nb = n_slots // blk
    f = w_gate.shape[-1]
    n_experts = w_gate.shape[0]
    assert (TOP_K * n_tok) % blk == 0
    n_tiles = TOP_K * n_tok + (n_experts + 1) * blk
    grid_spec = pltpu.PrefetchScalarGridSpec(
        num_scalar_prefetch=3,
        grid=(nb,),
        in_specs=[pl.BlockSpec(memory_space=pl.ANY),
                  pl.BlockSpec((1, d, f), lambda i, be, nu, sm: (be[i], 0, 0)),
                  pl.BlockSpec((1, d, f), lambda i, be, nu, sm: (be[i], 0, 0)),
                  pl.BlockSpec((1, f, d), lambda i, be, nu, sm: (be[i], 0, 0))],
        out_specs=pl.BlockSpec(memory_space=pl.ANY),
        scratch_shapes=[pltpu.VMEM((n_tok * SUB, LANES), U32),
                        pltpu.VMEM((2, blk * SUB, LANES), U32),
                        pltpu.VMEM((2, blk * SUB, LANES), U32),
                        pltpu.VMEM((d, f), BF16),
                        pltpu.VMEM((d, f), BF16),
                        pltpu.VMEM((f, d), BF16),
                        pltpu.SemaphoreType.DMA((2,)),
                        pltpu.SemaphoreType.DMA],
    )
    return pl.pallas_call(
        functools.partial(_moe_expert_gather_kernel, n_tok),
        out_shape=jax.ShapeDtypeStruct((n_tiles * SUB, LANES), U32),
        grid_spec=grid_spec,
        compiler_params=_cparams(("arbitrary",)),
        name="moe_experts",
    )(blk_e, n_used, smap, h2, w_gate, w_up, w_down)


def _moe_expert_kernel(n_tok, blk_e_ref, nused_ref, smap_ref, x_ref, wg_ref, wu_ref, wd_ref, y_ref,
                       ystage, wgb, wub, wdb, sem):
    i = pl.program_id(0)
    n_used = nused_ref[0]
    blk = x_ref.shape[0] // SUB
    first_real_blocks = TOP_K * n_tok // blk
    first_spare = y_ref.shape[0] // SUB - blk

    def issue(b, slot, rows):
        for r in rows:
            t = jnp.where(b >= 0, smap_ref[jnp.maximum(b, 0) * blk + r], first_spare + r)
            dst = pl.multiple_of(t * SUB, SUB)
            pltpu.make_async_copy(ystage.at[slot, pl.ds(r * SUB, SUB), :], y_ref.at[pl.ds(dst, SUB), :],
                                  sem.at[slot]).start(priority=r % 2)

    def wait_block(slot):
        pltpu.make_async_copy(ystage.at[slot], y_ref.at[pl.ds(0, blk * SUB), :], sem.at[slot]).wait()

    @pl.when(i == 0)
    def _():
        ystage[...] = jnp.zeros_like(ystage)
        n_spare_blocks = y_ref.shape[0] // (blk * SUB) - first_real_blocks

        def spare_copy(c):
            dst = (first_real_blocks + c) * blk * SUB
            return pltpu.make_async_copy(ystage.at[0], y_ref.at[pl.ds(dst, blk * SUB), :], sem.at[0])

        for c in range(n_spare_blocks):
            spare_copy(c).start()
        for c in range(n_spare_blocks):
            spare_copy(c).wait()

    new_expert = (i == 0) | (blk_e_ref[i] != blk_e_ref[jnp.maximum(i - 1, 0)])

    @pl.when((i < n_used) & new_expert)
    def _():
        wgb[...] = wg_ref[0].astype(BF16)
        wub[...] = wu_ref[0].astype(BF16)
        wdb[...] = wd_ref[0].astype(BF16)

    @pl.when(i < n_used)
    def _():
        slot = i % 2
        pslot = 1 - slot

        @pl.when(i > 0)
        def _():
            wait_block(slot)

        q = blk // 4
        xb = _tiles_to_rows(x_ref, blk).astype(BF16)
        issue(i - 1, pslot, range(0, q))
        hg = _dot(xb, wgb[...])
        issue(i - 1, pslot, range(q, 2 * q))
        hu = _dot(xb, wub[...])
        issue(i - 1, pslot, range(2 * q, 3 * q))
        hid = (_silu(hg) * hu).astype(BF16)
        y = _dot(hid, wdb[...])
        issue(i - 1, pslot, range(3 * q, blk))
        _rows_to_tiles(ystage.at[slot], y)

        @pl.when(i == n_used - 1)
        def _():
            issue(i, slot, range(blk))
            wait_block(pslot)
            wait_block(slot)


def _moe_experts(xb, smap, blk_e, n_used, w_gate, w_up, w_down, blk, n_tok):
    d = w_gate.shape[1]
    assert d == 2 * SUB * LANES and xb.shape[1] == LANES
    n_slots = smap.shape[0]
    nb = n_slots // blk
    f = w_gate.shape[-1]
    n_experts = w_gate.shape[0]
    assert (TOP_K * n_tok) % blk == 0
    n_tiles = TOP_K * n_tok + (n_experts + 1) * blk
    grid_spec = pltpu.PrefetchScalarGridSpec(
        num_scalar_prefetch=3,
        grid=(nb,),
        in_specs=[pl.BlockSpec((blk * SUB, LANES), lambda i, be, nu, sm: (jnp.minimum(i, nu[0] - 1), 0)),
                  pl.BlockSpec((1, d, f), lambda i, be, nu, sm: (be[i], 0, 0)),
                  pl.BlockSpec((1, d, f), lambda i, be, nu, sm: (be[i], 0, 0)),
                  pl.BlockSpec((1, f, d), lambda i, be, nu, sm: (be[i], 0, 0))],
        out_specs=pl.BlockSpec(memory_space=pl.ANY),
        scratch_shapes=[pltpu.VMEM((2, blk * SUB, LANES), U32),
                        pltpu.VMEM((d, f), BF16),
                        pltpu.VMEM((d, f), BF16),
                        pltpu.VMEM((f, d), BF16),
                        pltpu.SemaphoreType.DMA((2,))],
    )
    return pl.pallas_call(
        functools.partial(_moe_expert_kernel, n_tok),
        out_shape=jax.ShapeDtypeStruct((n_tiles * SUB, LANES), U32),
        grid_spec=grid_spec,
        compiler_params=_cparams(("arbitrary",)),
        name="moe_experts",
    )(blk_e, n_used, smap, xb, w_gate, w_up, w_down)


def _moe_combine_kernel(x1_ref, routet_ref, gt2_ref, fg_ref, y1_ref, y2_ref, o_ref):
    tm = x1_ref.shape[0]
    route = routet_ref[...].T
    moe = route[:, 0:1] * _tiles_to_rows(y1_ref, tm) + route[:, 1:2] * _tiles_to_rows(y2_ref, tm)
    xo = x1_ref[...] + gt2_ref[0] * moe
    ms = jnp.mean(xo * xo, axis=-1, keepdims=True)
    o_ref[...] = xo * lax.rsqrt(ms + NORM_EPS) * fg_ref[...]


def _moe_combine(x1, route_t, yt, mod3, final_g, s, tm):
    n, d = x1.shape
    tiles_per_batch = s // tm
    n_steps = n // tm
    per_o = route_t.shape[1] // tm
    return pl.pallas_call(
        _moe_combine_kernel,
        out_shape=jax.ShapeDtypeStruct((n, d), F32),
        grid=(n_steps,),
        in_specs=[pl.BlockSpec((tm, d), lambda i: (i, 0)),
                  pl.BlockSpec((SUBLANES, tm), lambda i: (i // per_o, i % per_o)),
                  pl.BlockSpec((1, 1, d), lambda i: ((i // tiles_per_batch) * 6 + 5, 0, 0)),
                  pl.BlockSpec((1, d), lambda i: (0, 0)),
                  pl.BlockSpec((tm * SUB, LANES), lambda i: (i, 0)),
                  pl.BlockSpec((tm * SUB, LANES), lambda i: (n_steps + i, 0))],
        out_specs=pl.BlockSpec((tm, d), lambda i: (i, 0)),
        compiler_params=_cparams(("arbitrary",)),
        name="moe_combine",
    )(x1, route_t, mod3, final_g.reshape(1, d), yt, yt)


def _pick(n, candidates):
    for t in candidates:
        if n % t == 0:
            return t
    raise ValueError(f"no tile in {candidates} divides {n}")


def kernel(x, c, ada_w, ada_b, norm1_g, w_in, hg_lb, hg_norm_g, rw_mu, rw_w0, rw_w2, rw_a0, rw_a2, rw_g2, rw_kk, rw_ka, rw_rk, rw_gn_w, rw_gn_b, w_proj_a, w_proj_b, w_out, norm2_g, router_g_w, router_g_b, router_e_w, router_e_b, exp_w_gate, exp_w_up, exp_w_down, final_g):
    bsz, s, d = x.shape
    depth = ada_w.shape[0]
    hg_f = hg_lb.shape[-1]
    hg_w = hg_norm_g.shape[-1]
    rw_w = rw_w0.shape[-1]
    rw_cols = rw_mu.shape[-1]
    n_groups = router_g_w.shape[-1]
    n_experts = router_e_w.shape[-1]
    assert hg_f == hg_w and s % CHUNK == 0 and n_experts + n_groups <= LANES and d == 2 * SUB * LANES

    lb_all = jnp.cumsum(jax.nn.softmax(hg_lb.astype(F32), axis=0), axis=0)
    n = bsz * s
    blk = 256
    n_blocks = (n * TOP_K + n_experts * blk) // blk
    for l in range(depth):
        mod = _ada_mod(c, ada_w[l], ada_b[l])
        mod3 = mod.reshape(bsz * 6, 1, d)

        hg_cols = 2 * hg_f + 2 * hg_w
        rw_pad = -(-rw_cols // 256) * 256
        wl = w_in[l]
        w_hg = wl[:, :hg_cols].astype(BF16)
        w_rw = jnp.zeros((d, rw_pad), BF16).at[:, :rw_cols].set(wl[:, hg_cols:hg_cols + rw_cols].astype(BF16))
        w_gt = wl[:, hg_cols + rw_cols:].astype(BF16)
        hg, rw, gates = _in_proj(x, mod3, norm1_g[l], w_hg, w_rw, w_gt, _pick(s, (512, 256, 128, 64)))

        o_a = _hgrn2(hg, lb_all[l], hg_norm_g[l], _pick(s, (512, 256, 128, 64)))
        o_b = _rwkv7(rw, rw_mu[l], rw_w0[l], rw_w2[l], rw_a0[l], rw_a2[l], rw_g2[l],
                     rw_kk[l], rw_ka[l], rw_rk[l].reshape(-1), rw_gn_w[l], rw_gn_b[l],
                     _pick(s, (2 * CHUNK, CHUNK)))

        wr = jnp.zeros((d, LANES), F32).at[:, :n_experts].set(router_e_w[l])
        wr = wr.at[:, n_experts:n_experts + n_groups].set(router_g_w[l])
        br = jnp.zeros((1, LANES), F32).at[0, :n_experts].set(router_e_b[l])
        br = br.at[0, n_experts:n_experts + n_groups].set(router_g_b[l])
        tm_o = _pick(s, (512, 256, 128, 64))
        x1, h2, route_t, counts = _out_proj(
            x, o_a, o_b, gates, mod3, norm2_g[l],
            w_proj_a[l].astype(BF16), w_proj_b[l].astype(BF16), w_out[l].astype(BF16),
            wr, br, n_groups, n_experts, tm_o)

        rt = route_t.reshape(n // tm_o, SUBLANES, tm_o)
        eid = rt[:, 2:4, :].astype(jnp.int32)
        rank = rt[:, 4:6, :].astype(jnp.int32)
        cnt = counts[:n_experts, 0].astype(jnp.int32)
        padded = (cnt + blk - 1) // blk * blk
        pad_end = jnp.cumsum(padded)
        pad_start = pad_end - padded
        e_ax = jnp.arange(n_experts, dtype=jnp.int32)[:, None, None, None]
        dest = rank + jnp.sum(jnp.where(eid[None] == e_ax, pad_start[:, None, None, None], 0), axis=0)
        blk_start = jnp.arange(n_blocks, dtype=jnp.int32) * blk
        blk_e = jnp.minimum(jnp.sum((pad_end[None, :] <= blk_start[:, None]).astype(jnp.int32), axis=1),
                            n_experts - 1)
        n_used = (pad_end[-1:] // blk).astype(jnp.int32)
        tm = _pick(s, (256, 128, 64))
        smap = _moe_slotmap(dest.reshape(-1), pad_start + cnt, padded - cnt, n_used, n_blocks * blk, blk, tm_o, n)
        yt = _moe_experts_gather(h2, smap, blk_e, n_used, exp_w_gate[l], exp_w_up[l], exp_w_down[l], blk, n)
        last = l == depth - 1
        assert last, "the final RMSNorm is fused into the last layer's combine"
        out = _moe_combine(x1.reshape(n, d), route_t, yt, mod3, final_g, s, tm)
        x = out.reshape(bsz, s, d)
    return x
```

```python
import functools

import numpy as np
import jax
import jax.numpy as jnp
from jax import lax
from jax.experimental import pallas as pl
from jax.experimental.pallas import tpu as pltpu

F32 = jnp.float32
BF16 = jnp.bfloat16
HIGHEST = lax.Precision.HIGHEST

NORM_EPS = 1e-6
HG_HEAD = 128
RW_HEAD = 64
RW_GN_EPS = 64e-5
TOP_K = 2
CHUNK = 64
LANES = 128
SUB = 4
U32 = jnp.uint32
MXU_K = 256
SUBLANES = 8
SMAP_BITS = 16
VMEM_LIMIT = 56 * 1024 * 1024

NT = (((1,), (1,)), ((), ()))
TN = (((0,), (0,)), ((), ()))


def _dot(a, b, dims=None, precision=None):
    if dims is None:
        return jnp.dot(a, b, preferred_element_type=F32, precision=precision)
    return lax.dot_general(a, b, dims, preferred_element_type=F32, precision=precision)


def _split3(x):
    hi = x.astype(BF16)
    r1 = x - hi.astype(F32)
    mid = r1.astype(BF16)
    lo = (r1 - mid.astype(F32)).astype(BF16)
    return hi, mid, lo


def _dot_exact_lhs(m3_bf16, x):
    return _dot(m3_bf16, jnp.concatenate(_split3(x), axis=0))


def _head_sums(x, m2_bf16):
    outs = []
    for g0 in range(0, x.shape[1], MXU_K):
        xg = x[:, g0:g0 + MXU_K]
        hi = xg.astype(BF16)
        lo = (xg - hi.astype(F32)).astype(BF16)
        outs.append(_dot(jnp.concatenate([hi, lo], axis=1), m2_bf16))
    return jnp.concatenate(outs, axis=1)


def _sigmoid(x):
    return 1.0 / (1.0 + jnp.exp(-x))


def _silu(x):
    return x * _sigmoid(x)


def _rows_to_tiles(ref, val):
    m, half = val.shape[0], val.shape[1] // 2
    hi = lax.bitcast_convert_type(val[:, :half].astype(BF16).astype(F32), U32)
    lo = lax.bitcast_convert_type(val[:, half:].astype(BF16).astype(F32), U32)
    w = (hi & jnp.uint32(0xFFFF0000)) | (lo >> 16)
    for j in range(SUB):
        ref[pl.ds(j, m, stride=SUB), :] = w[:, j * LANES:(j + 1) * LANES]


def _tiles_to_rows(ref, m, base=0):
    w = jnp.concatenate([ref[pl.ds(base * SUB + j, m, stride=SUB), :] for j in range(SUB)], axis=1)
    hi = lax.bitcast_convert_type(w & jnp.uint32(0xFFFF0000), F32)
    lo = lax.bitcast_convert_type(w << 16, F32)
    return jnp.concatenate([hi, lo], axis=1)


def _cparams(sem):
    return pltpu.CompilerParams(dimension_semantics=sem, vmem_limit_bytes=VMEM_LIMIT)


def _ada_kernel(c_ref, w_ref, b_ref, o_ref):
    c = c_ref[...]
    o_ref[...] = _dot(_silu(c), w_ref[...], precision=HIGHEST) + b_ref[...]


def _ada_mod(c, w, b):
    bsz, d = c.shape
    n = w.shape[1]
    rows = 8
    cp = jnp.zeros((rows, d), F32).at[:bsz].set(c)
    tn = 1536
    out = pl.pallas_call(
        _ada_kernel,
        out_shape=jax.ShapeDtypeStruct((rows, n), F32),
        grid=(n // tn,),
        in_specs=[pl.BlockSpec((rows, d), lambda j: (0, 0)),
                  pl.BlockSpec((d, tn), lambda j: (0, j)),
                  pl.BlockSpec((1, tn), lambda j: (0, j))],
        out_specs=pl.BlockSpec((rows, tn), lambda j: (0, j)),
        compiler_params=_cparams(("arbitrary",)),
        name="ada_mod",
    )(cp, w, b.reshape(1, n))
    return out[:bsz]


def _in_proj_kernel(x_ref, sh_ref, sc_ref, g_ref, whg_ref, wrw_ref, wgt_ref, hg_ref, rw_ref, gt_ref):
    x = x_ref[0]
    ms = jnp.mean(x * x, axis=-1, keepdims=True)
    h = (x * lax.rsqrt(ms + NORM_EPS) * g_ref[...]) * (1.0 + sc_ref[0]) + sh_ref[0]
    hb = h.astype(BF16)
    step = 512
    for n0 in range(0, whg_ref.shape[1], step):
        hg_ref[0, :, n0:n0 + step] = _dot(hb, whg_ref[:, n0:n0 + step])
    for n0 in range(0, wrw_ref.shape[1], 256):
        rw_ref[0, :, n0:n0 + 256] = _dot(hb, wrw_ref[:, n0:n0 + 256])
    for n0 in range(0, wgt_ref.shape[1], step):
        gt_ref[0, :, n0:n0 + step] = _sigmoid(_dot(hb, wgt_ref[:, n0:n0 + step])).astype(BF16)


def _in_proj(x, mod3, norm_g, w_hg, w_rw, w_gt, tm):
    bsz, s, d = x.shape
    n_hg, n_rw, n_gt = w_hg.shape[1], w_rw.shape[1], w_gt.shape[1]
    const = lambda b, i: (0, 0)
    return pl.pallas_call(
        _in_proj_kernel,
        out_shape=(jax.ShapeDtypeStruct((bsz, s, n_hg), F32),
                   jax.ShapeDtypeStruct((bsz, s, n_rw), F32),
                   jax.ShapeDtypeStruct((bsz, s, n_gt), BF16)),
        grid=(bsz, s // tm),
        in_specs=[pl.BlockSpec((1, tm, d), lambda b, i: (b, i, 0)),
                  pl.BlockSpec((1, 1, d), lambda b, i: (b * 6 + 0, 0, 0)),
                  pl.BlockSpec((1, 1, d), lambda b, i: (b * 6 + 1, 0, 0)),
                  pl.BlockSpec((1, d), const),
                  pl.BlockSpec((d, n_hg), const),
                  pl.BlockSpec((d, n_rw), const),
                  pl.BlockSpec((d, n_gt), const)],
        out_specs=(pl.BlockSpec((1, tm, n_hg), lambda b, i: (b, i, 0)),
                   pl.BlockSpec((1, tm, n_rw), lambda b, i: (b, i, 0)),
                   pl.BlockSpec((1, tm, n_gt), lambda b, i: (b, i, 0))),
        compiler_params=_cparams(("arbitrary", "arbitrary")),
        name="in_proj",
    )(x, mod3, mod3, norm_g.reshape(1, d), w_hg, w_rw, w_gt)


_HG_LEVELS = (32, 16, 8, 4, 2, 1)


def _hgrn2_consts(width):
    c = CHUNK
    t = np.arange(c)[:, None]
    s = np.arange(c)[None, :]
    blocks = [(s <= t), (s > t)]
    lvl_masks = []
    right = []
    for h in _HG_LEVELS:
        m = (t // (2 * h)) * 2 * h + h
        is_r = (t & h) != 0
        blk = np.where(is_r, (s >= m) & (s <= t), (s > t) & (s <= m - 1))
        blocks.append(blk)
        lvl_masks.append(is_r & ((s & h) == 0) & ((t // (2 * h)) == (s // (2 * h))))
        right.append(np.broadcast_to(is_r, (c, width)))
    mst = np.tile(np.concatenate(blocks, axis=0).astype(np.float32), (1, 3))
    lm = np.stack([np.eye(c, dtype=bool)] + lvl_masks).astype(np.float32)
    rm = np.stack(right).astype(np.float32)
    return jnp.asarray(mst, BF16), jnp.asarray(lm, F32), jnp.asarray(rm, F32)


def _hgrn2_kernel(q_ref, f_ref, i_ref, g_ref, lb_ref, ng_ref, mst_ref, lm_ref, rm_ref, o_ref, st_ref):
    c = CHUNK
    n_chunks = q_ref.shape[1] // c

    @pl.when(pl.program_id(1) == 0)
    def _():
        st_ref[...] = jnp.zeros_like(st_ref)

    nsub = 2 if n_chunks % 2 == 0 else 1

    def chunk_body(ci, carry):
        r0 = pl.multiple_of(ci * (nsub * c), nsub * c)
        for _ in _hgrn2_steps(q_ref, f_ref, i_ref, g_ref, lb_ref, ng_ref, mst_ref, lm_ref, rm_ref, o_ref, st_ref,
                              0, r0, nsub):
            pass
        return carry

    lax.fori_loop(0, n_chunks // nsub, chunk_body, 0)


def _hgrn2_steps(q_ref, f_ref, i_ref, g_ref, lb_ref, ng_ref, mst_ref, lm_ref, rm_ref, o_ref, st_ref, b, r0, nsub):
    c = CHUNK
    n_heads = q_ref.shape[2] // HG_HEAD
    mst = mst_ref[...]
    lb = lb_ref[...]
    ng = ng_ref[...]
    heads = [slice(hd * HG_HEAD, (hd + 1) * HG_HEAD) for hd in range(n_heads)]
    subs = []
    for j in range(nsub):
        rows = pl.ds(r0 + j * c, c)
        q = _silu(q_ref[b, rows, :])
        f = lb + (1.0 - lb) * _sigmoid(f_ref[b, rows, :])
        k = 1.0 - f
        ex = jnp.exp(_dot_exact_lhs(mst, jnp.log(f)))
        subs.append(dict(rows=rows, q=q, k=k, ex=ex, vb=i_ref[b, rows, :].astype(BF16),
                         qd=(q * ex[0:c]).astype(BF16), kr=(k * ex[c:2 * c]).astype(BF16)))
        yield
    for sb in subs:
        qb, kb = sb['q'].astype(BF16), sb['k'].astype(BF16)
        sb['sc'] = [lm_ref[0] * _dot(qb[:, ls], kb[:, ls], NT) for ls in heads]
        sb['dqk'] = sb['q'] - sb['k']
    yield
    for li in range(len(_HG_LEVELS)):
        for sb in subs:
            g_l = ((sb['k'] + rm_ref[li] * sb['dqk']) * sb['ex'][(2 + li) * c:(3 + li) * c]).astype(BF16)
            sb['sc'] = [s_h + lm_ref[li + 1] * _dot(g_l[:, ls], g_l[:, ls], NT)
                        for s_h, ls in zip(sb['sc'], heads)]
        yield
    for sb in subs:
        sb['kv'] = [_dot(sb['vb'][:, ls], sb['kr'][:, ls], TN) for ls in heads]
        sb['o'] = [_dot(s_h.astype(BF16), sb['vb'][:, ls]) for s_h, ls in zip(sb['sc'], heads)]
        yield
    sts = [st_ref[b, hd] for hd in range(n_heads)]
    for sb in subs:
        sb['o'] = [o_h + _dot(sb['qd'][:, ls], st.astype(BF16), NT) for o_h, ls, st in zip(sb['o'], heads, sts)]
        sts = [st * sb['ex'][c - 1:c, ls] + kv for st, ls, kv in zip(sts, heads, sb['kv'])]
        yield
    for hd in range(n_heads):
        st_ref[b, hd] = sts[hd]
    for sb in subs:
        on = [o_h * lax.rsqrt(jnp.mean(o_h * o_h, axis=-1, keepdims=True) + NORM_EPS) for o_h in sb['o']]
        o_full = jnp.concatenate(on, axis=1) * ng
        o_ref[b, sb['rows'], :] = (o_full * _silu(g_ref[b, sb['rows'], :])).astype(o_ref.dtype)
        yield


def _hgrn2(hg, lb, norm_g, ts):
    bsz, s, n4 = hg.shape
    w = n4 // 4
    mst, lm, rm = _hgrn2_consts(w)
    n_heads = w // HG_HEAD
    const2 = lambda b, i: (0, 0)
    const3 = lambda b, i: (0, 0, 0)
    return pl.pallas_call(
        _hgrn2_kernel,
        out_shape=jax.ShapeDtypeStruct((bsz, s, w), BF16),
        grid=(bsz, s // ts),
        in_specs=[pl.BlockSpec((1, ts, w), lambda b, i: (b, i, 0)),
                  pl.BlockSpec((1, ts, w), lambda b, i: (b, i, 1)),
                  pl.BlockSpec((1, ts, w), lambda b, i: (b, i, 2)),
                  pl.BlockSpec((1, ts, w), lambda b, i: (b, i, 3)),
                  pl.BlockSpec((1, w), const2),
                  pl.BlockSpec((1, w), const2),
                  pl.BlockSpec(mst.shape, const2),
                  pl.BlockSpec(lm.shape, const3),
                  pl.BlockSpec(rm.shape, const3)],
        out_specs=pl.BlockSpec((1, ts, w), lambda b, i: (b, i, 0)),
        scratch_shapes=[pltpu.VMEM((1, n_heads, HG_HEAD, HG_HEAD), F32)],
        compiler_params=_cparams(("arbitrary", "arbitrary")),
        name="hgrn2",
    )(hg, hg, hg, hg, lb.reshape(1, w), norm_g.reshape(1, w), mst, lm, rm)


def _rwkv_consts(width):
    c = CHUNK
    t = np.arange(c)[:, None]
    s = np.arange(c)[None, :]
    tri = np.tile((s <= t).astype(np.float32), (1, 3))
    tt = np.arange(2 * c)[:, None]
    ss = np.arange(2 * c)[None, :]
    same = (tt // c) == (ss // c)
    strict = same & ((ss % c) < (tt % c))
    incl = same & ((ss % c) <= (tt % c))
    hsum = (np.arange(MXU_K)[:, None] // RW_HEAD) == (np.arange(MXU_K)[None, :] // RW_HEAD)
    hsum = np.tile(hsum, (2, 1))
    return (jnp.asarray(tri, BF16), jnp.asarray(strict.astype(np.float32), F32),
            jnp.asarray(incl.astype(np.float32), F32), jnp.asarray(hsum.astype(np.float32), BF16))


def _rwkv7_kernel(p_ref, mu_ref, w0_ref, a0_ref, kk_ref, ka_ref, rk_ref, gnw_ref, gnb_ref,
                  w2_ref, a2_ref, g2_ref, tri_ref, sm_ref, im_ref, hs_ref,
                  o_ref, carry_ref, zt_ref):
    @pl.when(pl.program_id(0) == 0)
    def _():
        carry_ref[...] = jnp.zeros_like(carry_ref)
        zt_ref[...] = jnp.zeros_like(zt_ref)

    for _ in _rwkv7_steps(p_ref, mu_ref, w0_ref, a0_ref, kk_ref, ka_ref, rk_ref, gnw_ref, gnb_ref,
                          w2_ref, a2_ref, g2_ref, tri_ref, sm_ref, im_ref, hs_ref, o_ref, carry_ref, zt_ref):
        pass


def _rwkv7_steps(p_ref, mu_ref, w0_ref, a0_ref, kk_ref, ka_ref, rk_ref, gnw_ref, gnb_ref,
                 w2_ref, a2_ref, g2_ref, tri_ref, sm_ref, im_ref, hs_ref, o_ref, carry_ref, zt_ref):
    c = CHUNK
    nb = p_ref.shape[0]
    nch = p_ref.shape[1] // c
    width = o_ref.shape[2]
    n_pairs = width // LANES

    hs = hs_ref[...]
    tri = tri_ref[...]
    smask = sm_ref[...] > 0
    imask = im_ref[...] > 0
    lane = lax.broadcasted_iota(jnp.int32, (c, LANES), 1)
    m0 = (lane < RW_HEAD).astype(F32)
    m1 = 1.0 - m0

    def stack(x):
        return jnp.concatenate([x * m0, x * m1], axis=0)

    xs_rows = []
    for b in range(nb):
        p = p_ref[b]
        row = lax.broadcasted_iota(jnp.int32, p.shape, 0)
        prev = jnp.where(row == 0, carry_ref[b], pltpu.roll(p, 1, 0))
        carry_ref[b] = p[nch * c - 1:nch * c, :]
        xs_rows.append(p + mu_ref[...] * (prev - p))
    xs = jnp.concatenate(xs_rows, axis=0)
    r_all = xs[:, 0:width]
    k_all = xs[:, width:2 * width]
    v_all = xs[:, 2 * width:3 * width]
    slab = xs[:, 3 * width:]
    nz = -(w0_ref[...] + _dot(jnp.tanh(slab).astype(BF16), w2_ref[...]))
    softplus = jnp.maximum(nz, 0.0) + jnp.log(1.0 + jnp.exp(-jnp.abs(nz)))
    ld_all = -jnp.exp(-softplus - 0.5)
    a_all = _sigmoid(a0_ref[...] + _dot(slab.astype(BF16), a2_ref[...]))
    g_all = _dot(_sigmoid(slab).astype(BF16), g2_ref[...])
    kk0 = k_all * kk_ref[...]
    kk_all = kk0 * lax.rsqrt(jnp.maximum(_head_sums(kk0 * kk0, hs), 1e-24))
    k2_all = k_all * (1.0 + (a_all - 1.0) * ka_ref[...])
    yield

    units = []
    for b, j in [(b, j) for b in range(nb) for j in range(nch)]:
        rb = slice((b * nch + j) * c, (b * nch + j + 1) * c)
        r, k2, v, ld = r_all[rb], k2_all[rb], v_all[rb], ld_all[rb]
        a_in = -kk_all[rb]
        b_in = kk_all[rb] * a_all[rb]
        cum = _dot_exact_lhs(tri, ld)
        cum_t = cum[c - 1:c, :]
        e_c = jnp.exp(cum)
        e_nc = jnp.exp(-cum)
        e_rem = jnp.exp(cum_t - cum)
        at_f = a_in * jnp.exp(cum - ld)
        rt_f = r * e_c
        kt_f = k2 * e_nc
        bt_f = b_in * e_nc
        kh_f = k2 * e_rem
        bh_f = b_in * e_rem
        p_t = jnp.exp(cum_t)
        for pi in range(n_pairs):
            ls = slice(pi * LANES, (pi + 1) * LANES)
            units.append(dict(
                b=b, j=j, pi=pi,
                at=stack(at_f[:, ls]).astype(BF16), rt=stack(rt_f[:, ls]).astype(BF16),
                kt=stack(kt_f[:, ls]).astype(BF16), bt=stack(bt_f[:, ls]).astype(BF16),
                kh=stack(kh_f[:, ls]).astype(BF16), bh=stack(bh_f[:, ls]).astype(BF16),
                vs=stack(v[:, ls]).astype(BF16), p_t=p_t[:, ls]))
        yield

    for u in units:
        lhs = jnp.concatenate([u['at'], u['rt']], axis=0)
        u['g'] = _dot(lhs, jnp.concatenate([u['kt'], u['bt']], axis=0), NT)
    yield
    for u in units:
        g = u.pop('g')
        u['a_ak'] = jnp.where(smask, g[:2 * c, :2 * c], 0.0).astype(BF16)
        u['pw'] = jnp.where(smask, g[:2 * c, 2 * c:], 0.0).astype(BF16)
        u['a_r'] = jnp.where(jnp.concatenate([imask, imask], axis=1), g[2 * c:], 0.0).astype(BF16)
    for u in units:
        akv = _dot(u.pop('a_ak'), u['vs'])
        u['x'] = jnp.concatenate([u['at'].astype(F32), akv], axis=1)
    yield
    n_lvl = int(np.log2(c))
    for lvl in range(n_lvl):
        for u in units:
            u['x'] = u['x'] + _dot(u['pw'], u['x'].astype(BF16))
        yield
        if lvl + 1 < n_lvl:
            for u in units:
                u['pw'] = _dot(u['pw'], u['pw']).astype(BF16)
            yield
    for u in units:
        x = u.pop('x')
        u['wr'] = jnp.concatenate([x[:, :LANES].astype(BF16), u['rt']], axis=0)
        u['u_loc'] = x[:, LANES:]
    zt = {(b, pi): zt_ref[b, pi] for b in range(nb) for pi in range(n_pairs)}
    for j in range(nch):
        tail = [u for u in units if u['j'] == j]
        for u in tail:
            u['uy'] = _dot(u.pop('wr'), zt[u['b'], u['pi']].astype(BF16), NT)
        yield
        for u in tail:
            uy = u.pop('uy')
            u['u'] = (uy[:2 * c] + u.pop('u_loc')).astype(BF16)
            u['y0'] = uy[2 * c:]
        for u in tail:
            vu = jnp.concatenate([u['vs'], u['u']], axis=0)
            u['y'] = u.pop('y0') + _dot(u['a_r'], vu)
            upd = _dot(vu, jnp.concatenate([u['kh'], u['bh']], axis=0), TN)
            zt[u['b'], u['pi']] = zt[u['b'], u['pi']] * u['p_t'] + upd
        yield
    for (b, pi), z in zt.items():
        zt_ref[b, pi] = z

    inv_n = 1.0 / RW_HEAD
    y = jnp.concatenate(
        [jnp.concatenate([u['y'][:c] + u['y'][c:] for u in units if (u['b'], u['j']) == (b, j)], axis=1)
         for b in range(nb) for j in range(nch)], axis=0)
    mean = _head_sums(y, hs) * inv_n
    yield
    d = y - mean
    var = _head_sums(d * d, hs) * inv_n
    yield
    yn = d * lax.rsqrt(var + RW_GN_EPS) * gnw_ref[...] + gnb_ref[...]
    bonus = _head_sums(r_all * k2_all * rk_ref[...], hs) * v_all
    out = ((yn + bonus) * g_all).astype(o_ref.dtype)
    for b in range(nb):
        o_ref[b] = out[b * nch * c:(b + 1) * nch * c]


def _rwkv7(rw, mu, w0, w2, a0, a2, g2, k_k, k_a, r_k, gn_w, gn_b, ts):
    bsz, s, cols = rw.shape
    width = w0.shape[-1]
    n_pairs = width // LANES
    slab = cols - 3 * width
    dl, al, gl = w2.shape[0], a2.shape[0], g2.shape[0]
    w2f = jnp.zeros((slab, width), F32).at[0:dl].set(w2).astype(BF16)
    a2f = jnp.zeros((slab, width), F32).at[dl:dl + al].set(a2).astype(BF16)
    g2f = jnp.zeros((slab, width), F32).at[dl + al:dl + al + gl].set(g2).astype(BF16)
    mup = jnp.zeros((1, cols), F32).at[0, :mu.shape[-1]].set(mu)
    tri, sm, im, hs = _rwkv_consts(width)
    row = lambda x: x.reshape(1, width)
    const = lambda i: (0, 0)
    vec = pl.BlockSpec((1, width), const)
    return pl.pallas_call(
        _rwkv7_kernel,
        out_shape=jax.ShapeDtypeStruct((bsz, s, width), BF16),
        grid=(s // ts,),
        in_specs=[pl.BlockSpec((bsz, ts, cols), lambda i: (0, i, 0)),
                  pl.BlockSpec((1, cols), const),
                  vec, vec, vec, vec, vec, vec, vec,
                  pl.BlockSpec((slab, width), const),
                  pl.BlockSpec((slab, width), const),
                  pl.BlockSpec((slab, width), const),
                  pl.BlockSpec(tri.shape, const),
                  pl.BlockSpec(sm.shape, const),
                  pl.BlockSpec(im.shape, const),
                  pl.BlockSpec(hs.shape, const)],
        out_specs=pl.BlockSpec((bsz, ts, width), lambda i: (0, i, 0)),
        scratch_shapes=[pltpu.VMEM((bsz, 1, cols), F32),
                        pltpu.VMEM((bsz, n_pairs, LANES, LANES), F32)],
        compiler_params=_cparams(("arbitrary",)),
        name="rwkv7",
    )(rw, mup, row(w0), row(a0), row(k_k), row(k_a), row(r_k), row(gn_w), row(gn_b),
      w2f, a2f, g2f, tri, sm, im, hs)


def _out_proj_kernel(n_groups, n_experts,
                     x_ref, oa_ref, ob_ref, ga_ref, gb_ref, gt1_ref, sc2_ref, sh2_ref, g2_ref,
                     wa_ref, wb_ref, wo_ref, wr_ref, wrl_ref, br_ref, upper_ref,
                     x1_ref, h2_ref, routet_ref, cnt_ref, carry_ref):
    first = (pl.program_id(0) == 0) & (pl.program_id(1) == 0)

    @pl.when(first)
    def _():
        carry_ref[...] = jnp.zeros_like(carry_ref)

    pa = _dot(oa_ref[0], wa_ref[...])
    pb = _dot(ob_ref[0], wb_ref[...])
    mixed = ga_ref[0].astype(F32) * pa + gb_ref[0].astype(F32) * pb
    x1 = x_ref[0] + gt1_ref[0] * _dot(mixed.astype(BF16), wo_ref[...])
    x1_ref[0] = x1
    ms = jnp.mean(x1 * x1, axis=-1, keepdims=True)
    h2 = (x1 * lax.rsqrt(ms + NORM_EPS) * g2_ref[...]) * (1.0 + sc2_ref[0]) + sh2_ref[0]
    _rows_to_tiles(h2_ref, h2)

    h2_hi = h2.astype(BF16)
    h2_lo = (h2 - h2_hi.astype(F32)).astype(BF16)
    logits = (_dot(wr_ref[...], h2_hi, NT) + _dot(wr_ref[...], h2_lo, NT) + _dot(wrl_ref[...], h2_hi, NT)
              + br_ref[...])
    row = lax.broadcasted_iota(jnp.int32, logits.shape, 0)
    neg = jnp.float32(-jnp.inf)
    big = jnp.int32(1 << 20)
    eg = n_experts // n_groups
    is_g = (row >= n_experts) & (row < n_experts + n_groups)
    lg = jnp.where(is_g, logits, neg)
    mg = jnp.max(lg, axis=0, keepdims=True)
    p_grp = 1.0 / jnp.sum(jnp.where(is_g, jnp.exp(lg - mg), 0.0), axis=0, keepdims=True)
    gidx = jnp.min(jnp.where(lg == mg, row, big), axis=0, keepdims=True) - n_experts
    sel = (row >= gidx * eg) & (row < gidx * eg + eg)
    le = jnp.where(sel, logits, neg)
    me = jnp.max(le, axis=0, keepdims=True)
    pe_un = jnp.where(sel, jnp.exp(le - me), 0.0)
    pe = jnp.where(sel, pe_un / jnp.sum(pe_un, axis=0, keepdims=True), -1.0)
    v1 = jnp.max(pe, axis=0, keepdims=True)
    i1 = jnp.min(jnp.where(pe == v1, row, big), axis=0, keepdims=True)
    pe2 = jnp.where(row == i1, -1.0, pe)
    v2 = jnp.max(pe2, axis=0, keepdims=True)
    i2 = jnp.min(jnp.where(pe2 == v2, row, big), axis=0, keepdims=True)
    wsum = v1 + v2
    w1 = p_grp * v1 / wsum
    w2 = p_grp * v2 / wsum

    oh1 = (row == i1).astype(F32)
    oh2 = (row == i2).astype(F32)
    both = oh1 + oh2
    before = _dot(both.astype(BF16), upper_ref[...]) + carry_ref[...]
    rank1 = jnp.sum(oh1 * before, axis=0, keepdims=True)
    rank2 = jnp.sum(oh2 * before, axis=0, keepdims=True)
    carry_ref[...] = carry_ref[...] + jnp.sum(both, axis=1, keepdims=True)
    cnt_ref[...] = carry_ref[...]
    zero = jnp.zeros_like(w1)
    routet_ref[...] = jnp.concatenate(
        [w1, w2, i1.astype(F32), i2.astype(F32), rank1, rank2, zero, zero], axis=0)


def _out_proj(x, o_a, o_b, gates, mod3, norm2_g, wa, wb, wo, wr, br, n_groups, n_experts, tm):
    bsz, s, d = x.shape
    wdt = o_a.shape[-1]
    upper = jnp.asarray(np.triu(np.ones((tm, tm), np.float32), 1), BF16)
    wrt = wr.T
    wr_hi = wrt.astype(BF16)
    wr_lo = (wrt - wr_hi.astype(F32)).astype(BF16)
    const = lambda b, i: (0, 0)
    tile = lambda b, i: (b, i, 0)
    kern = functools.partial(_out_proj_kernel, n_groups, n_experts)
    return pl.pallas_call(
        kern,
        out_shape=(jax.ShapeDtypeStruct((bsz, s, d), F32),
                   jax.ShapeDtypeStruct((bsz * s * SUB, LANES), U32),
                   jax.ShapeDtypeStruct((bsz * (s // tm) * SUBLANES, tm), F32),
                   jax.ShapeDtypeStruct((LANES, 1), F32)),
        grid=(bsz, s // tm),
        in_specs=[pl.BlockSpec((1, tm, d), tile),
                  pl.BlockSpec((1, tm, wdt), tile),
                  pl.BlockSpec((1, tm, wdt), tile),
                  pl.BlockSpec((1, tm, d), lambda b, i: (b, i, 0)),
                  pl.BlockSpec((1, tm, d), lambda b, i: (b, i, 1)),
                  pl.BlockSpec((1, 1, d), lambda b, i: (b * 6 + 2, 0, 0)),
                  pl.BlockSpec((1, 1, d), lambda b, i: (b * 6 + 4, 0, 0)),
                  pl.BlockSpec((1, 1, d), lambda b, i: (b * 6 + 3, 0, 0)),
                  pl.BlockSpec((1, d), const),
                  pl.BlockSpec(wa.shape, const),
                  pl.BlockSpec(wb.shape, const),
                  pl.BlockSpec(wo.shape, const),
                  pl.BlockSpec(wrt.shape, const),
                  pl.BlockSpec(wrt.shape, const),
                  pl.BlockSpec((LANES, 1), const),
                  pl.BlockSpec((tm, tm), const)],
        out_specs=(pl.BlockSpec((1, tm, d), tile),
                   pl.BlockSpec((tm * SUB, LANES), lambda b, i: (b * (s // tm) + i, 0)),
                   pl.BlockSpec((SUBLANES, tm), lambda b, i: (b * (s // tm) + i, 0)),
                   pl.BlockSpec((LANES, 1), const)),
        scratch_shapes=[pltpu.VMEM((LANES, 1), F32)],
        compiler_params=_cparams(("arbitrary", "arbitrary")),
        name="out_proj",
    )(x, o_a, o_b, gates, gates, mod3, mod3, mod3, norm2_g.reshape(1, d), wa, wb, wo, wr_hi, wr_lo,
      br.reshape(LANES, 1), upper)


def _moe_dispatch_kernel(tm, tm_o, n_tok, blk, dest_ref, zstart_ref, zcnt_ref, nused_ref, h_ref, xb_ref, smap_ref,
                         zbuf, stage, sem, zsem):
    i = pl.program_id(0)
    n_steps = pl.num_programs(0)
    n_slots = smap_ref.shape[0]
    n_experts = zcnt_ref.shape[0]

    @pl.when(i == 0)
    def _():
        def init(s_, carry):
            smap_ref[s_] = 0
            return carry
        lax.fori_loop(nused_ref[0] * blk, n_slots, init, 0)
        zbuf[...] = jnp.zeros_like(zbuf)

        def zero_row(e, j):
            dst = pl.multiple_of((zstart_ref[e] + j) * SUB, SUB)
            return pltpu.make_async_copy(zbuf.at[pl.ds(0, SUB), :], xb_ref.at[pl.ds(dst, SUB), :], zsem)

        def zero_block(b):
            dst = pl.multiple_of(b * (blk * SUB), blk * SUB)
            return pltpu.make_async_copy(zbuf, xb_ref.at[pl.ds(dst, blk * SUB), :], zsem)

        for e in range(n_experts):
            def zstart(j, carry, e=e):
                zero_row(e, j).start()
                smap_ref[zstart_ref[e] + j] = TOP_K * n_tok + e * blk + j
                return carry
            lax.fori_loop(0, zcnt_ref[e], zstart, 0)

        def bstart(b, carry):
            zero_block(b).start()
            return carry
        lax.fori_loop(nused_ref[0], n_slots // blk, bstart, 0)
        for e in range(n_experts):
            def zwait(j, carry, e=e):
                zero_row(e, j).wait()
                return carry
            lax.fori_loop(0, zcnt_ref[e], zwait, 0)

        def bwait(b, carry):
            zero_block(b).wait()
            return carry
        lax.fori_loop(nused_ref[0], n_slots // blk, bwait, 0)

    slot = i % 2
    stage[slot] = h_ref[...]
    tok0 = i * tm
    dest0 = (tok0 // tm_o) * (TOP_K * tm_o) + tok0 % tm_o
    for r in range(tm):
        tok = tok0 + r
        for k in range(TOP_K):
            d = dest_ref[dest0 + k * tm_o + r]
            dst = pl.multiple_of(d * SUB, SUB)
            pltpu.make_async_copy(stage.at[slot, pl.ds(r * SUB, SUB), :], xb_ref.at[pl.ds(dst, SUB), :],
                                  sem.at[slot]).start(priority=k)
            smap_ref[d] = k * n_tok + tok

    def wait_step(s_):
        for _ in range(TOP_K):
            pltpu.make_async_copy(stage.at[s_], xb_ref.at[pl.ds(0, tm * SUB), :], sem.at[s_]).wait()

    @pl.when(i > 0)
    def _():
        wait_step(1 - slot)

    @pl.when(i == n_steps - 1)
    def _():
        wait_step(slot)


def _moe_dispatch(h2, dest, zstart, zcnt, n_used, n_slots, blk, tm, tm_o):
    n_tok = h2.shape[0] // SUB
    grid_spec = pltpu.PrefetchScalarGridSpec(
        num_scalar_prefetch=4,
        grid=(n_tok // tm,),
        in_specs=[pl.BlockSpec((tm * SUB, LANES), lambda i, *_: (i, 0))],
        out_specs=(pl.BlockSpec(memory_space=pl.ANY),
                   pl.BlockSpec(memory_space=pltpu.SMEM)),
        scratch_shapes=[pltpu.VMEM((blk * SUB, LANES), U32),
                        pltpu.VMEM((2, tm * SUB, LANES), U32),
                        pltpu.SemaphoreType.DMA((2,)),
                        pltpu.SemaphoreType.DMA],
    )
    return pl.pallas_call(
        functools.partial(_moe_dispatch_kernel, tm, tm_o, n_tok, blk),
        out_shape=(jax.ShapeDtypeStruct((n_slots * SUB, LANES), U32),
                   jax.ShapeDtypeStruct((n_slots,), jnp.int32)),
        grid_spec=grid_spec,
        compiler_params=_cparams(("arbitrary",)),
        name="moe_dispatch",
    )(dest, zstart, zcnt, n_used, h2)


def _moe_slotmap_kernel(tm_o, n_tok, blk, dest_ref, zstart_ref, zcnt_ref, nused_ref, smap_ref):
    i = pl.program_id(0)
    n_slots = smap_ref.shape[0] - blk
    n_experts = zcnt_ref.shape[0]

    @pl.when(i == 0)
    def _():
        def init(s_, carry):
            smap_ref[blk + s_] = TOP_K * n_tok
            return carry
        lax.fori_loop(nused_ref[0] * blk, n_slots, init, 0)
        for r in range(blk):
            smap_ref[r] = TOP_K * n_tok + n_experts * blk + r
        for e in range(n_experts):
            def pad(j, carry, e=e):
                smap_ref[blk + zstart_ref[e] + j] = TOP_K * n_tok + e * blk + j
                return carry
            lax.fori_loop(0, zcnt_ref[e], pad, 0)

    base = i * (TOP_K * tm_o)
    for k in range(TOP_K):
        for r in range(tm_o):
            tok = i * tm_o + r
            smap_ref[blk + dest_ref[base + k * tm_o + r]] = (k * n_tok + tok) + (tok << SMAP_BITS)


def _moe_slotmap(dest, zstart, zcnt, n_used, n_slots, blk, tm_o, n_tok):
    grid_spec = pltpu.PrefetchScalarGridSpec(
        num_scalar_prefetch=4,
        grid=(n_tok // tm_o,),
        in_specs=[],
        out_specs=pl.BlockSpec(memory_space=pltpu.SMEM),
    )
    n_experts = zcnt.shape[0]
    assert TOP_K * n_tok + (n_experts + 1) * blk <= (1 << SMAP_BITS) and n_tok <= (1 << (31 - SMAP_BITS))
    return pl.pallas_call(
        functools.partial(_moe_slotmap_kernel, tm_o, n_tok, blk),
        out_shape=jax.ShapeDtypeStruct((blk + n_slots,), jnp.int32),
        grid_spec=grid_spec,
        compiler_params=_cparams(("arbitrary",)),
        name="moe_slotmap",
    )(dest, zstart, zcnt, n_used)


def _moe_expert_gather_kernel(n_tok, blk_e_ref, nused_ref, smap_ref, h_ref, wg_ref, wu_ref, wd_ref, y_ref,
                              hv, xbuf, ystage, wgb, wub, wdb, sem, hsem):
    i = pl.program_id(0)
    nb = pl.num_programs(0)
    n_used = nused_ref[0]
    blk = xbuf.shape[1] // SUB
    first_real_blocks = TOP_K * n_tok // blk

    def gather(b, slot, rows):
        for r in rows:
            tok = lax.shift_right_logical(smap_ref[(b + 1) * blk + r], SMAP_BITS)
            src = pl.multiple_of(tok * SUB, SUB)
            xbuf[slot, pl.ds(r * SUB, SUB), :] = hv[pl.ds(src, SUB), :]

    def issue(b, slot, rows):
        for r in rows:
            t = smap_ref[(b + 1) * blk + r] & ((1 << SMAP_BITS) - 1)
            dst = pl.multiple_of(t * SUB, SUB)
            pltpu.make_async_copy(ystage.at[slot, pl.ds(r * SUB, SUB), :], y_ref.at[pl.ds(dst, SUB), :],
                                  sem.at[slot]).start(priority=r % 2)

    def wait_block(slot):
        pltpu.make_async_copy(ystage.at[slot], y_ref.at[pl.ds(0, blk * SUB), :], sem.at[slot]).wait()

    @pl.when(i == 0)
    def _():
        load = pltpu.make_async_copy(h_ref, hv, hsem)
        load.start()
        ystage[...] = jnp.zeros_like(ystage)
        n_spare_blocks = y_ref.shape[0] // (blk * SUB) - first_real_blocks

        def spare_copy(c):
            dst = (first_real_blocks + c) * blk * SUB
            return pltpu.make_async_copy(ystage.at[0], y_ref.at[pl.ds(dst, blk * SUB), :], sem.at[0])

        for c in range(n_spare_blocks):
            spare_copy(c).start()
        for c in range(n_spare_blocks):
            spare_copy(c).wait()
        load.wait()
        gather(0, 0, range(blk))

    new_expert = (i == 0) | (blk_e_ref[i] != blk_e_ref[jnp.maximum(i - 1, 0)])

    @pl.when((i < n_used) & new_expert)
    def _():
        wgb[...] = wg_ref[0].astype(BF16)
        wub[...] = wu_ref[0].astype(BF16)
        wdb[...] = wd_ref[0].astype(BF16)

    @pl.when(i < n_used)
    def _():
        slot = i % 2
        pslot = 1 - slot
        nxt = jnp.minimum(i + 1, nb - 1)

        @pl.when(i > 0)
        def _():
            wait_block(slot)

        q = blk // 4
        xb = _tiles_to_rows(xbuf.at[slot], blk).astype(BF16)
        issue(i - 1, pslot, range(0, q))
        gather(nxt, pslot, range(0, q))
        hg = _dot(xb, wgb[...])
        issue(i - 1, pslot, range(q, 2 * q))
        gather(nxt, pslot, range(q, 2 * q))
        hu = _dot(xb, wub[...])
        issue(i - 1, pslot, range(2 * q, 3 * q))
        gather(nxt, pslot, range(2 * q, 3 * q))
        hid = (_silu(hg) * hu).astype(BF16)
        y = _dot(hid, wdb[...])
        issue(i - 1, pslot, range(3 * q, blk))
        gather(nxt, pslot, range(3 * q, blk))
        _rows_to_tiles(ystage.at[slot], y)

        @pl.when(i == n_used - 1)
        def _():
            issue(i, slot, range(blk))
            wait_block(pslot)
            wait_block(slot)


def _moe_experts_gather(h2, smap, blk_e, n_used, w_gate, w_up, w_down, blk, n_tok):
    d = w_gate.shape[1]
    assert d == 2 * SUB * LANES and h2.shape == (n_tok * SUB, LANES)
    n_slots = smap.shape[0] - blk
    nb = n_slots // blk
    f = w_gate.shape[-1]
    n_experts = w_gate.shape[0]
    assert (TOP_K * n_tok) % blk == 0
    n_tiles = TOP_K * n_tok + (n_experts + 1) * blk
    grid_spec = pltpu.PrefetchScalarGridSpec(
        num_scalar_prefetch=3,
        grid=(nb,),
        in_specs=[pl.BlockSpec(memory_space=pl.ANY),
                  pl.BlockSpec((1, d, f), lambda i, be, nu, sm: (be[i], 0, 0)),
                  pl.BlockSpec((1, d, f), lambda i, be, nu, sm: (be[i], 0, 0)),
                  pl.BlockSpec((1, f, d), lambda i, be, nu, sm: (be[i], 0, 0))],
        out_specs=pl.BlockSpec(memory_space=pl.ANY),
        scratch_shapes=[pltpu.VMEM((n_tok * SUB, LANES), U32),
                        pltpu.VMEM((2, blk * SUB, LANES), U32),
                        pltpu.VMEM((2, blk * SUB, LANES), U32),
                        pltpu.VMEM((d, f), BF16),
                        pltpu.VMEM((d, f), BF16),
                        pltpu.VMEM((f, d), BF16),
                        pltpu.SemaphoreType.DMA((2,)),
                        pltpu.SemaphoreType.DMA],
    )
    return pl.pallas_call(
        functools.partial(_moe_expert_gather_kernel, n_tok),
        out_shape=jax.ShapeDtypeStruct((n_tiles * SUB, LANES), U32),
        grid_spec=grid_spec,
        compiler_params=_cparams(("arbitrary",)),
        name="moe_experts",
    )(blk_e, n_used, smap, h2, w_gate, w_up, w_down)


def _moe_expert_kernel(n_tok, blk_e_ref, nused_ref, smap_ref, x_ref, wg_ref, wu_ref, wd_ref, y_ref,
                       ystage, wgb, wub, wdb, sem):
    i = pl.program_id(0)
    n_used = nused_ref[0]
    blk = x_ref.shape[0] // SUB
    first_real_blocks = TOP_K * n_tok // blk
    first_spare = y_ref.shape[0] // SUB - blk

    def issue(b, slot, rows):
        for r in rows:
            t = jnp.where(b >= 0, smap_ref[jnp.maximum(b, 0) * blk + r], first_spare + r)
            dst = pl.multiple_of(t * SUB, SUB)
            pltpu.make_async_copy(ystage.at[slot, pl.ds(r * SUB, SUB), :], y_ref.at[pl.ds(dst, SUB), :],
                                  sem.at[slot]).start(priority=r % 2)

    def wait_block(slot):
        pltpu.make_async_copy(ystage.at[slot], y_ref.at[pl.ds(0, blk * SUB), :], sem.at[slot]).wait()

    @pl.when(i == 0)
    def _():
        ystage[...] = jnp.zeros_like(ystage)
        n_spare_blocks = y_ref.shape[0] // (blk * SUB) - first_real_blocks

        def spare_copy(c):
            dst = (first_real_blocks + c) * blk * SUB
            return pltpu.make_async_copy(ystage.at[0], y_ref.at[pl.ds(dst, blk * SUB), :], sem.at[0])

        for c in range(n_spare_blocks):
            spare_copy(c).start()
        for c in range(n_spare_blocks):
            spare_copy(c).wait()

    new_expert = (i == 0) | (blk_e_ref[i] != blk_e_ref[jnp.maximum(i - 1, 0)])

    @pl.when((i < n_used) & new_expert)
    def _():
        wgb[...] = wg_ref[0].astype(BF16)
        wub[...] = wu_ref[0].astype(BF16)
        wdb[...] = wd_ref[0].astype(BF16)

    @pl.when(i < n_used)
    def _():
        slot = i % 2
        pslot = 1 - slot

        @pl.when(i > 0)
        def _():
            wait_block(slot)

        q = blk // 4
        xb = _tiles_to_rows(x_ref, blk).astype(BF16)
        issue(i - 1, pslot, range(0, q))
        hg = _dot(xb, wgb[...])
        issue(i - 1, pslot, range(q, 2 * q))
        hu = _dot(xb, wub[...])
        issue(i - 1, pslot, range(2 * q, 3 * q))
        hid = (_silu(hg) * hu).astype(BF16)
        y = _dot(hid, wdb[...])
        issue(i - 1, pslot, range(3 * q, blk))
        _rows_to_tiles(ystage.at[slot], y)

        @pl.when(i == n_used - 1)
        def _():
            issue(i, slot, range(blk))
            wait_block(pslot)
            wait_block(slot)


def _moe_experts(xb, smap, blk_e, n_used, w_gate, w_up, w_down, blk, n_tok):
    d = w_gate.shape[1]
    assert d == 2 * SUB * LANES and xb.shape[1] == LANES
    n_slots = smap.shape[0]
    nb = n_slots // blk
    f = w_gate.shape[-1]
    n_experts = w_gate.shape[0]
    assert (TOP_K * n_tok) % blk == 0
    n_tiles = TOP_K * n_tok + (n_experts + 1) * blk
    grid_spec = pltpu.PrefetchScalarGridSpec(
        num_scalar_prefetch=3,
        grid=(nb,),
        in_specs=[pl.BlockSpec((blk * SUB, LANES), lambda i, be, nu, sm: (jnp.minimum(i, nu[0] - 1), 0)),
                  pl.BlockSpec((1, d, f), lambda i, be, nu, sm: (be[i], 0, 0)),
                  pl.BlockSpec((1, d, f), lambda i, be, nu, sm: (be[i], 0, 0)),
                  pl.BlockSpec((1, f, d), lambda i, be, nu, sm: (be[i], 0, 0))],
        out_specs=pl.BlockSpec(memory_space=pl.ANY),
        scratch_shapes=[pltpu.VMEM((2, blk * SUB, LANES), U32),
                        pltpu.VMEM((d, f), BF16),
                        pltpu.VMEM((d, f), BF16),
                        pltpu.VMEM((f, d), BF16),
                        pltpu.SemaphoreType.DMA((2,))],
    )
    return pl.pallas_call(
        functools.partial(_moe_expert_kernel, n_tok),
        out_shape=jax.ShapeDtypeStruct((n_tiles * SUB, LANES), U32),
        grid_spec=grid_spec,
        compiler_params=_cparams(("arbitrary",)),
        name="moe_experts",
    )(blk_e, n_used, smap, xb, w_gate, w_up, w_down)


def _moe_combine_kernel(x1_ref, routet_ref, gt2_ref, fg_ref, y1_ref, y2_ref, o_ref):
    tm = x1_ref.shape[0]
    route = routet_ref[...].T
    moe = route[:, 0:1] * _tiles_to_rows(y1_ref, tm) + route[:, 1:2] * _tiles_to_rows(y2_ref, tm)
    xo = x1_ref[...] + gt2_ref[0] * moe
    ms = jnp.mean(xo * xo, axis=-1, keepdims=True)
    o_ref[...] = xo * lax.rsqrt(ms + NORM_EPS) * fg_ref[...]


def _moe_combine(x1, route_t, yt, mod3, final_g, s, tm):
    n, d = x1.shape
    tiles_per_batch = s // tm
    n_steps = n // tm
    per_o = route_t.shape[1] // tm
    return pl.pallas_call(
        _moe_combine_kernel,
        out_shape=jax.ShapeDtypeStruct((n, d), F32),
        grid=(n_steps,),
        in_specs=[pl.BlockSpec((tm, d), lambda i: (i, 0)),
                  pl.BlockSpec((SUBLANES, tm), lambda i: (i // per_o, i % per_o)),
                  pl.BlockSpec((1, 1, d), lambda i: ((i // tiles_per_batch) * 6 + 5, 0, 0)),
                  pl.BlockSpec((1, d), lambda i: (0, 0)),
                  pl.BlockSpec((tm * SUB, LANES), lambda i: (i, 0)),
                  pl.BlockSpec((tm * SUB, LANES), lambda i: (n_steps + i, 0))],
        out_specs=pl.BlockSpec((tm, d), lambda i: (i, 0)),
        compiler_params=_cparams(("arbitrary",)),
        name="moe_combine",
    )(x1, route_t, mod3, final_g.reshape(1, d), yt, yt)


def _pick(n, candidates):
    for t in candidates:
        if n % t == 0:
            return t
    raise ValueError(f"no tile in {candidates} divides {n}")


def kernel(x, c, ada_w, ada_b, norm1_g, w_in, hg_lb, hg_norm_g, rw_mu, rw_w0, rw_w2, rw_a0, rw_a2, rw_g2, rw_kk, rw_ka, rw_rk, rw_gn_w, rw_gn_b, w_proj_a, w_proj_b, w_out, norm2_g, router_g_w, router_g_b, router_e_w, router_e_b, exp_w_gate, exp_w_up, exp_w_down, final_g):
    bsz, s, d = x.shape
    depth = ada_w.shape[0]
    hg_f = hg_lb.shape[-1]
    hg_w = hg_norm_g.shape[-1]
    rw_w = rw_w0.shape[-1]
    rw_cols = rw_mu.shape[-1]
    n_groups = router_g_w.shape[-1]
    n_experts = router_e_w.shape[-1]
    assert hg_f == hg_w and s % CHUNK == 0 and n_experts + n_groups <= LANES and d == 2 * SUB * LANES

    lb_all = jnp.cumsum(jax.nn.softmax(hg_lb.astype(F32), axis=0), axis=0)
    n = bsz * s
    blk = 256
    n_blocks = (n * TOP_K + n_experts * blk) // blk
    for l in range(depth):
        mod = _ada_mod(c, ada_w[l], ada_b[l])
        mod3 = mod.reshape(bsz * 6, 1, d)

        hg_cols = 2 * hg_f + 2 * hg_w
        rw_pad = -(-rw_cols // 256) * 256
        wl = w_in[l]
        w_hg = wl[:, :hg_cols].astype(BF16)
        w_rw = jnp.zeros((d, rw_pad), BF16).at[:, :rw_cols].set(wl[:, hg_cols:hg_cols + rw_cols].astype(BF16))
        w_gt = wl[:, hg_cols + rw_cols:].astype(BF16)
        hg, rw, gates = _in_proj(x, mod3, norm1_g[l], w_hg, w_rw, w_gt, _pick(s, (512, 256, 128, 64)))

        o_a = _hgrn2(hg, lb_all[l], hg_norm_g[l], _pick(s, (512, 256, 128, 64)))
        o_b = _rwkv7(rw, rw_mu[l], rw_w0[l], rw_w2[l], rw_a0[l], rw_a2[l], rw_g2[l],
                     rw_kk[l], rw_ka[l], rw_rk[l].reshape(-1), rw_gn_w[l], rw_gn_b[l],
                     _pick(s, (2 * CHUNK, CHUNK)))

        wr = jnp.zeros((d, LANES), F32).at[:, :n_experts].set(router_e_w[l])
        wr = wr.at[:, n_experts:n_experts + n_groups].set(router_g_w[l])
        br = jnp.zeros((1, LANES), F32).at[0, :n_experts].set(router_e_b[l])
        br = br.at[0, n_experts:n_experts + n_groups].set(router_g_b[l])
        tm_o = _pick(s, (512, 256, 128, 64))
        x1, h2, route_t, counts = _out_proj(
            x, o_a, o_b, gates, mod3, norm2_g[l],
            w_proj_a[l].astype(BF16), w_proj_b[l].astype(BF16), w_out[l].astype(BF16),
            wr, br, n_groups, n_experts, tm_o)

        rt = route_t.reshape(n // tm_o, SUBLANES, tm_o)
        eid = rt[:, 2:4, :].astype(jnp.int32)
        rank = rt[:, 4:6, :].astype(jnp.int32)
        cnt = counts[:n_experts, 0].astype(jnp.int32)
        padded = (cnt + blk - 1) // blk * blk
        pad_end = jnp.cumsum(padded)
        pad_start = pad_end - padded
        e_ax = jnp.arange(n_experts, dtype=jnp.int32)[:, None, None, None]
        dest = rank + jnp.sum(jnp.where(eid[None] == e_ax, pad_start[:, None, None, None], 0), axis=0)
        blk_start = jnp.arange(n_blocks, dtype=jnp.int32) * blk
        blk_e = jnp.minimum(jnp.sum((pad_end[None, :] <= blk_start[:, None]).astype(jnp.int32), axis=1),
                            n_experts - 1)
        n_used = (pad_end[-1:] // blk).astype(jnp.int32)
        tm = _pick(s, (256, 128, 64))
        smap = _moe_slotmap(dest.reshape(-1), pad_start + cnt, padded - cnt, n_used, n_blocks * blk, blk, tm_o, n)
        yt = _moe_experts_gather(h2, smap, blk_e, n_used, exp_w_gate[l], exp_w_up[l], exp_w_down[l], blk, n)
        last = l == depth - 1
        assert last, "the final RMSNorm is fused into the last layer's combine"
        out = _moe_combine(x1.reshape(n, d), route_t, yt, mod3, final_g, s, tm)
        x = out.reshape(bsz, s, d)
    return x
```

```python
import functools

import numpy as np
import jax
import jax.numpy as jnp
from jax import lax
from jax.experimental import pallas as pl
from jax.experimental.pallas import tpu as pltpu

F32 = jnp.float32
BF16 = jnp.bfloat16
HIGHEST = lax.Precision.HIGHEST

NORM_EPS = 1e-6
HG_HEAD = 128
RW_HEAD = 64
RW_GN_EPS = 64e-5
TOP_K = 2
CHUNK = 64
LANES = 128
SUB = 4
U32 = jnp.uint32
MXU_K = 256
SUBLANES = 8
SMAP_BITS = 16
VMEM_LIMIT = 56 * 1024 * 1024

NT = (((1,), (1,)), ((), ()))
TN = (((0,), (0,)), ((), ()))


def _dot(a, b, dims=None, precision=None):
    if dims is None:
        return jnp.dot(a, b, preferred_element_type=F32, precision=precision)
    return lax.dot_general(a, b, dims, preferred_element_type=F32, precision=precision)


def _split3(x):
    hi = x.astype(BF16)
    r1 = x - hi.astype(F32)
    mid = r1.astype(BF16)
    lo = (r1 - mid.astype(F32)).astype(BF16)
    return hi, mid, lo


def _dot_exact_lhs(m3_bf16, x):
    return _dot(m3_bf16, jnp.concatenate(_split3(x), axis=0))


def _head_sums(x, m2_bf16):
    outs = []
    for g0 in range(0, x.shape[1], MXU_K):
        xg = x[:, g0:g0 + MXU_K]
        hi = xg.astype(BF16)
        lo = (xg - hi.astype(F32)).astype(BF16)
        outs.append(_dot(jnp.concatenate([hi, lo], axis=1), m2_bf16))
    return jnp.concatenate(outs, axis=1)


def _sigmoid(x):
    return 1.0 / (1.0 + jnp.exp(-x))


def _silu(x):
    return x * _sigmoid(x)


def _rows_to_tiles(ref, val):
    m, half = val.shape[0], val.shape[1] // 2
    hi = lax.bitcast_convert_type(val[:, :half].astype(BF16).astype(F32), U32)
    lo = lax.bitcast_convert_type(val[:, half:].astype(BF16).astype(F32), U32)
    w = (hi & jnp.uint32(0xFFFF0000)) | (lo >> 16)
    for j in range(SUB):
        ref[pl.ds(j, m, stride=SUB), :] = w[:, j * LANES:(j + 1) * LANES]


def _tiles_to_rows(ref, m, base=0):
    w = jnp.concatenate([ref[pl.ds(base * SUB + j, m, stride=SUB), :] for j in range(SUB)], axis=1)
    hi = lax.bitcast_convert_type(w & jnp.uint32(0xFFFF0000), F32)
    lo = lax.bitcast_convert_type(w << 16, F32)
    return jnp.concatenate([hi, lo], axis=1)


def _cparams(sem):
    return pltpu.CompilerParams(dimension_semantics=sem, vmem_limit_bytes=VMEM_LIMIT)


def _ada_kernel(c_ref, w_ref, b_ref, o_ref):
    c = c_ref[...]
    o_ref[...] = _dot(_silu(c), w_ref[...], precision=HIGHEST) + b_ref[...]


def _ada_mod(c, w, b):
    bsz, d = c.shape
    n = w.shape[1]
    rows = 8
    cp = jnp.zeros((rows, d), F32).at[:bsz].set(c)
    tn = 1536
    out = pl.pallas_call(
        _ada_kernel,
        out_shape=jax.ShapeDtypeStruct((rows, n), F32),
        grid=(n // tn,),
        in_specs=[pl.BlockSpec((rows, d), lambda j: (0, 0)),
                  pl.BlockSpec((d, tn), lambda j: (0, j)),
                  pl.BlockSpec((1, tn), lambda j: (0, j))],
        out_specs=pl.BlockSpec((rows, tn), lambda j: (0, j)),
        compiler_params=_cparams(("arbitrary",)),
        name="ada_mod",
    )(cp, w, b.reshape(1, n))
    return out[:bsz]


def _in_proj_kernel(x_ref, sh_ref, sc_ref, g_ref, whg_ref, wrw_ref, wgt_ref, hg_ref, rw_ref, gt_ref):
    x = x_ref[0]
    ms = jnp.mean(x * x, axis=-1, keepdims=True)
    h = (x * lax.rsqrt(ms + NORM_EPS) * g_ref[...]) * (1.0 + sc_ref[0]) + sh_ref[0]
    hb = h.astype(BF16)
    step = 512
    for n0 in range(0, whg_ref.shape[1], step):
        hg_ref[0, :, n0:n0 + step] = _dot(hb, whg_ref[:, n0:n0 + step])
    for n0 in range(0, wrw_ref.shape[1], 256):
        rw_ref[0, :, n0:n0 + 256] = _dot(hb, wrw_ref[:, n0:n0 + 256])
    for n0 in range(0, wgt_ref.shape[1], step):
        gt_ref[0, :, n0:n0 + step] = _sigmoid(_dot(hb, wgt_ref[:, n0:n0 + step])).astype(BF16)


def _in_proj(x, mod3, norm_g, w_hg, w_rw, w_gt, tm):
    bsz, s, d = x.shape
    n_hg, n_rw, n_gt = w_hg.shape[1], w_rw.shape[1], w_gt.shape[1]
    const = lambda b, i: (0, 0)
    return pl.pallas_call(
        _in_proj_kernel,
        out_shape=(jax.ShapeDtypeStruct((bsz, s, n_hg), F32),
                   jax.ShapeDtypeStruct((bsz, s, n_rw), F32),
                   jax.ShapeDtypeStruct((bsz, s, n_gt), BF16)),
        grid=(bsz, s // tm),
        in_specs=[pl.BlockSpec((1, tm, d), lambda b, i: (b, i, 0)),
                  pl.BlockSpec((1, 1, d), lambda b, i: (b * 6 + 0, 0, 0)),
                  pl.BlockSpec((1, 1, d), lambda b, i: (b * 6 + 1, 0, 0)),
                  pl.BlockSpec((1, d), const),
                  pl.BlockSpec((d, n_hg), const),
                  pl.BlockSpec((d, n_rw), const),
                  pl.BlockSpec((d, n_gt), const)],
        out_specs=(pl.BlockSpec((1, tm, n_hg), lambda b, i: (b, i, 0)),
                   pl.BlockSpec((1, tm, n_rw), lambda b, i: (b, i, 0)),
                   pl.BlockSpec((1, tm, n_gt), lambda b, i: (b, i, 0))),
        compiler_params=_cparams(("arbitrary", "arbitrary")),
        name="in_proj",
    )(x, mod3, mod3, norm_g.reshape(1, d), w_hg, w_rw, w_gt)


_HG_LEVELS = (32, 16, 8, 4, 2, 1)


def _hgrn2_consts(width):
    c = CHUNK
    t = np.arange(c)[:, None]
    s = np.arange(c)[None, :]
    blocks = [(s <= t), (s > t)]
    lvl_masks = []
    right = []
    for h in _HG_LEVELS:
        m = (t // (2 * h)) * 2 * h + h
        is_r = (t & h) != 0
        blk = np.where(is_r, (s >= m) & (s <= t), (s > t) & (s <= m - 1))
        blocks.append(blk)
        lvl_masks.append(is_r & ((s & h) == 0) & ((t // (2 * h)) == (s // (2 * h))))
        right.append(np.broadcast_to(is_r, (c, width)))
    mst = np.tile(np.concatenate(blocks, axis=0).astype(np.float32), (1, 3))
    lm = np.stack([np.eye(c, dtype=bool)] + lvl_masks).astype(np.float32)
    rm = np.stack(right).astype(np.float32)
    return jnp.asarray(mst, BF16), jnp.asarray(lm, F32), jnp.asarray(rm, F32)


def _hgrn2_kernel(q_ref, f_ref, i_ref, g_ref, lb_ref, ng_ref, mst_ref, lm_ref, rm_ref, o_ref, st_ref):
    c = CHUNK
    n_chunks = q_ref.shape[1] // c

    @pl.when(pl.program_id(1) == 0)
    def _():
        st_ref[...] = jnp.zeros_like(st_ref)

    nsub = 2 if n_chunks % 2 == 0 else 1

    def chunk_body(ci, carry):
        r0 = pl.multiple_of(ci * (nsub * c), nsub * c)
        for _ in _hgrn2_steps(q_ref, f_ref, i_ref, g_ref, lb_ref, ng_ref, mst_ref, lm_ref, rm_ref, o_ref, st_ref,
                              0, r0, nsub):
            pass
        return carry

    lax.fori_loop(0, n_chunks // nsub, chunk_body, 0)


def _hgrn2_steps(q_ref, f_ref, i_ref, g_ref, lb_ref, ng_ref, mst_ref, lm_ref, rm_ref, o_ref, st_ref, b, r0, nsub):
    c = CHUNK
    n_heads = q_ref.shape[2] // HG_HEAD
    mst = mst_ref[...]
    lb = lb_ref[...]
    ng = ng_ref[...]
    heads = [slice(hd * HG_HEAD, (hd + 1) * HG_HEAD) for hd in range(n_heads)]
    subs = []
    for j in range(nsub):
        rows = pl.ds(r0 + j * c, c)
        q = _silu(q_ref[b, rows, :])
        f = lb + (1.0 - lb) * _sigmoid(f_ref[b, rows, :])
        k = 1.0 - f
        ex = jnp.exp(_dot_exact_lhs(mst, jnp.log(f)))
        subs.append(dict(rows=rows, q=q, k=k, ex=ex, vb=i_ref[b, rows, :].astype(BF16),
                         qd=(q * ex[0:c]).astype(BF16), kr=(k * ex[c:2 * c]).astype(BF16)))
        yield
    for sb in subs:
        qb, kb = sb['q'].astype(BF16), sb['k'].astype(BF16)
        sb['sc'] = [lm_ref[0] * _dot(qb[:, ls], kb[:, ls], NT) for ls in heads]
        sb['dqk'] = sb['q'] - sb['k']
    yield
    for li in range(len(_HG_LEVELS)):
        for sb in subs:
            g_l = ((sb['k'] + rm_ref[li] * sb['dqk']) * sb['ex'][(2 + li) * c:(3 + li) * c]).astype(BF16)
            sb['sc'] = [s_h + lm_ref[li + 1] * _dot(g_l[:, ls], g_l[:, ls], NT)
                        for s_h, ls in zip(sb['sc'], heads)]
        yield
    for sb in subs:
        sb['kv'] = [_dot(sb['vb'][:, ls], sb['kr'][:, ls], TN) for ls in heads]
        sb['o'] = [_dot(s_h.astype(BF16), sb['vb'][:, ls]) for s_h, ls in zip(sb['sc'], heads)]
        yield
    sts = [st_ref[b, hd] for hd in range(n_heads)]
    for sb in subs:
        sb['o'] = [o_h + _dot(sb['qd'][:, ls], st.astype(BF16), NT) for o_h, ls, st in zip(sb['o'], heads, sts)]
        sts = [st * sb['ex'][c - 1:c, ls] + kv for st, ls, kv in zip(sts, heads, sb['kv'])]
        yield
    for hd in range(n_heads):
        st_ref[b, hd] = sts[hd]
    for sb in subs:
        on = [o_h * lax.rsqrt(jnp.mean(o_h * o_h, axis=-1, keepdims=True) + NORM_EPS) for o_h in sb['o']]
        o_full = jnp.concatenate(on, axis=1) * ng
        o_ref[b, sb['rows'], :] = (o_full * _silu(g_ref[b, sb['rows'], :])).astype(o_ref.dtype)
        yield


def _hgrn2(hg, lb, norm_g, ts):
    bsz, s, n4 = hg.shape
    w = n4 // 4
    mst, lm, rm = _hgrn2_consts(w)
    n_heads = w // HG_HEAD
    const2 = lambda b, i: (0, 0)
    const3 = lambda b, i: (0, 0, 0)
    return pl.pallas_call(
        _hgrn2_kernel,
        out_shape=jax.ShapeDtypeStruct((bsz, s, w), BF16),
        grid=(bsz, s // ts),
        in_specs=[pl.BlockSpec((1, ts, w), lambda b, i: (b, i, 0)),
                  pl.BlockSpec((1, ts, w), lambda b, i: (b, i, 1)),
                  pl.BlockSpec((1, ts, w), lambda b, i: (b, i, 2)),
                  pl.BlockSpec((1, ts, w), lambda b, i: (b, i, 3)),
                  pl.BlockSpec((1, w), const2),
                  pl.BlockSpec((1, w), const2),
                  pl.BlockSpec(mst.shape, const2),
                  pl.BlockSpec(lm.shape, const3),
                  pl.BlockSpec(rm.shape, const3)],
        out_specs=pl.BlockSpec((1, ts, w), lambda b, i: (b, i, 0)),
        scratch_shapes=[pltpu.VMEM((1, n_heads, HG_HEAD, HG_HEAD), F32)],
        compiler_params=_cparams(("arbitrary", "arbitrary")),
        name="hgrn2",
    )(hg, hg, hg, hg, lb.reshape(1, w), norm_g.reshape(1, w), mst, lm, rm)


def _rwkv_consts(width):
    c = CHUNK
    t = np.arange(c)[:, None]
    s = np.arange(c)[None, :]
    tri = np.tile((s <= t).astype(np.float32), (1, 3))
    tt = np.arange(2 * c)[:, None]
    ss = np.arange(2 * c)[None, :]
    same = (tt // c) == (ss // c)
    strict = same & ((ss % c) < (tt % c))
    incl = same & ((ss % c) <= (tt % c))
    hsum = (np.arange(MXU_K)[:, None] // RW_HEAD) == (np.arange(MXU_K)[None, :] // RW_HEAD)
    hsum = np.tile(hsum, (2, 1))
    return (jnp.asarray(tri, BF16), jnp.asarray(strict.astype(np.float32), F32),
            jnp.asarray(incl.astype(np.float32), F32), jnp.asarray(hsum.astype(np.float32), BF16))


def _rwkv7_kernel(p_ref, mu_ref, w0_ref, a0_ref, kk_ref, ka_ref, rk_ref, gnw_ref, gnb_ref,
                  w2_ref, a2_ref, g2_ref, tri_ref, sm_ref, im_ref, hs_ref,
                  o_ref, carry_ref, zt_ref):
    @pl.when(pl.program_id(0) == 0)
    def _():
        carry_ref[...] = jnp.zeros_like(carry_ref)
        zt_ref[...] = jnp.zeros_like(zt_ref)

    for _ in _rwkv7_steps(p_ref, mu_ref, w0_ref, a0_ref, kk_ref, ka_ref, rk_ref, gnw_ref, gnb_ref,
                          w2_ref, a2_ref, g2_ref, tri_ref, sm_ref, im_ref, hs_ref, o_ref, carry_ref, zt_ref):
        pass


def _rwkv7_steps(p_ref, mu_ref, w0_ref, a0_ref, kk_ref, ka_ref, rk_ref, gnw_ref, gnb_ref,
                 w2_ref, a2_ref, g2_ref, tri_ref, sm_ref, im_ref, hs_ref, o_ref, carry_ref, zt_ref):
    c = CHUNK
    nb = p_ref.shape[0]
    nch = p_ref.shape[1] // c
    width = o_ref.shape[2]
    n_pairs = width // LANES

    hs = hs_ref[...]
    tri = tri_ref[...]
    smask = sm_ref[...] > 0
    imask = im_ref[...] > 0
    lane = lax.broadcasted_iota(jnp.int32, (c, LANES), 1)
    m0 = (lane < RW_HEAD).astype(F32)
    m1 = 1.0 - m0

    def stack(x):
        return jnp.concatenate([x * m0, x * m1], axis=0)

    xs_rows = []
    for b in range(nb):
        p = p_ref[b]
        row = lax.broadcasted_iota(jnp.int32, p.shape, 0)
        prev = jnp.where(row == 0, carry_ref[b], pltpu.roll(p, 1, 0))
        carry_ref[b] = p[nch * c - 1:nch * c, :]
        xs_rows.append(p + mu_ref[...] * (prev - p))
    xs = jnp.concatenate(xs_rows, axis=0)
    r_all = xs[:, 0:width]
    k_all = xs[:, width:2 * width]
    v_all = xs[:, 2 * width:3 * width]
    slab = xs[:, 3 * width:]
    nz = -(w0_ref[...] + _dot(jnp.tanh(slab).astype(BF16), w2_ref[...]))
    softplus = jnp.maximum(nz, 0.0) + jnp.log(1.0 + jnp.exp(-jnp.abs(nz)))
    ld_all = -jnp.exp(-softplus - 0.5)
    a_all = _sigmoid(a0_ref[...] + _dot(slab.astype(BF16), a2_ref[...]))
    g_all = _dot(_sigmoid(slab).astype(BF16), g2_ref[...])
    kk0 = k_all * kk_ref[...]
    kk_all = kk0 * lax.rsqrt(jnp.maximum(_head_sums(kk0 * kk0, hs), 1e-24))
    k2_all = k_all * (1.0 + (a_all - 1.0) * ka_ref[...])
    yield

    units = []
    for b, j in [(b, j) for b in range(nb) for j in range(nch)]:
        rb = slice((b * nch + j) * c, (b * nch + j + 1) * c)
        r, k2, v, ld = r_all[rb], k2_all[rb], v_all[rb], ld_all[rb]
        a_in = -kk_all[rb]
        b_in = kk_all[rb] * a_all[rb]
        cum = _dot_exact_lhs(tri, ld)
        cum_t = cum[c - 1:c, :]
        e_c = jnp.exp(cum)
        e_nc = jnp.exp(-cum)
        e_rem = jnp.exp(cum_t - cum)
        at_f = a_in * jnp.exp(cum - ld)
        rt_f = r * e_c
        kt_f = k2 * e_nc
        bt_f = b_in * e_nc
        kh_f = k2 * e_rem
        bh_f = b_in * e_rem
        p_t = jnp.exp(cum_t)
        for pi in range(n_pairs):
            ls = slice(pi * LANES, (pi + 1) * LANES)
            units.append(dict(
                b=b, j=j, pi=pi,
                at=stack(at_f[:, ls]).astype(BF16), rt=stack(rt_f[:, ls]).astype(BF16),
                kt=stack(kt_f[:, ls]).astype(BF16), bt=stack(bt_f[:, ls]).astype(BF16),
                kh=stack(kh_f[:, ls]).astype(BF16), bh=stack(bh_f[:, ls]).astype(BF16),
                vs=stack(v[:, ls]).astype(BF16), p_t=p_t[:, ls]))
        yield

    for u in units:
        lhs = jnp.concatenate([u['at'], u['rt']], axis=0)
        u['g'] = _dot(lhs, jnp.concatenate([u['kt'], u['bt']], axis=0), NT)
    yield
    for u in units:
        g = u.pop('g')
        u['a_ak'] = jnp.where(smask, g[:2 * c, :2 * c], 0.0).astype(BF16)
        u['pw'] = jnp.where(smask, g[:2 * c, 2 * c:], 0.0).astype(BF16)
        u['a_r'] = jnp.where(jnp.concatenate([imask, imask], axis=1), g[2 * c:], 0.0).astype(BF16)
    for u in units:
        akv = _dot(u.pop('a_ak'), u['vs'])
        u['x'] = jnp.concatenate([u['at'].astype(F32), akv], axis=1)
    yield
    n_lvl = int(np.log2(c))
    for lvl in range(n_lvl):
        for u in units:
            u['x'] = u['x'] + _dot(u['pw'], u['x'].astype(BF16))
        yield
        if lvl + 1 < n_lvl:
            for u in units:
                u['pw'] = _dot(u['pw'], u['pw']).astype(BF16)
            yield
    for u in units:
        x = u.pop('x')
        u['wr'] = jnp.concatenate([x[:, :LANES].astype(BF16), u['rt']], axis=0)
        u['u_loc'] = x[:, LANES:]
    zt = {(b, pi): zt_ref[b, pi] for b in range(nb) for pi in range(n_pairs)}
    for j in range(nch):
        tail = [u for u in units if u['j'] == j]
        for u in tail:
            u['uy'] = _dot(u.pop('wr'), zt[u['b'], u['pi']].astype(BF16), NT)
        yield
        for u in tail:
            uy = u.pop('uy')
            u['u'] = (uy[:2 * c] + u.pop('u_loc')).astype(BF16)
            u['y0'] = uy[2 * c:]
        for u in tail:
            vu = jnp.concatenate([u['vs'], u['u']], axis=0)
            u['y'] = u.pop('y0') + _dot(u['a_r'], vu)
            upd = _dot(vu, jnp.concatenate([u['kh'], u['bh']], axis=0), TN)
            zt[u['b'], u['pi']] = zt[u['b'], u['pi']] * u['p_t'] + upd
        yield
    for (b, pi), z in zt.items():
        zt_ref[b, pi] = z

    inv_n = 1.0 / RW_HEAD
    y = jnp.concatenate(
        [jnp.concatenate([u['y'][:c] + u['y'][c:] for u in units if (u['b'], u['j']) == (b, j)], axis=1)
         for b in range(nb) for j in range(nch)], axis=0)
    mean = _head_sums(y, hs) * inv_n
    yield
    d = y - mean
    var = _head_sums(d * d, hs) * inv_n
    yield
    yn = d * lax.rsqrt(var + RW_GN_EPS) * gnw_ref[...] + gnb_ref[...]
    bonus = _head_sums(r_all * k2_all * rk_ref[...], hs) * v_all
    out = ((yn + bonus) * g_all).astype(o_ref.dtype)
    for b in range(nb):
        o_ref[b] = out[b * nch * c:(b + 1) * nch * c]


def _rwkv7(rw, mu, w0, w2, a0, a2, g2, k_k, k_a, r_k, gn_w, gn_b, ts):
    bsz, s, cols = rw.shape
    width = w0.shape[-1]
    n_pairs = width // LANES
    slab = cols - 3 * width
    dl, al, gl = w2.shape[0], a2.shape[0], g2.shape[0]
    w2f = jnp.zeros((slab, width), F32).at[0:dl].set(w2).astype(BF16)
    a2f = jnp.zeros((slab, width), F32).at[dl:dl + al].set(a2).astype(BF16)
    g2f = jnp.zeros((slab, width), F32).at[dl + al:dl + al + gl].set(g2).astype(BF16)
    mup = jnp.zeros((1, cols), F32).at[0, :mu.shape[-1]].set(mu)
    tri, sm, im, hs = _rwkv_consts(width)
    row = lambda x: x.reshape(1, width)
    const = lambda i: (0, 0)
    vec = pl.BlockSpec((1, width), const)
    return pl.pallas_call(
        _rwkv7_kernel,
        out_shape=jax.ShapeDtypeStruct((bsz, s, width), BF16),
        grid=(s // ts,),
        in_specs=[pl.BlockSpec((bsz, ts, cols), lambda i: (0, i, 0)),
                  pl.BlockSpec((1, cols), const),
                  vec, vec, vec, vec, vec, vec, vec,
                  pl.BlockSpec((slab, width), const),
                  pl.BlockSpec((slab, width), const),
                  pl.BlockSpec((slab, width), const),
                  pl.BlockSpec(tri.shape, const),
                  pl.BlockSpec(sm.shape, const),
                  pl.BlockSpec(im.shape, const),
                  pl.BlockSpec(hs.shape, const)],
        out_specs=pl.BlockSpec((bsz, ts, width), lambda i: (0, i, 0)),
        scratch_shapes=[pltpu.VMEM((bsz, 1, cols), F32),
                        pltpu.VMEM((bsz, n_pairs, LANES, LANES), F32)],
        compiler_params=_cparams(("arbitrary",)),
        name="rwkv7",
    )(rw, mup, row(w0), row(a0), row(k_k), row(k_a), row(r_k), row(gn_w), row(gn_b),
      w2f, a2f, g2f, tri, sm, im, hs)


def _out_proj_kernel(n_groups, n_experts,
                     x_ref, oa_ref, ob_ref, ga_ref, gb_ref, gt1_ref, sc2_ref, sh2_ref, g2_ref,
                     wa_ref, wb_ref, wo_ref, wr_ref, wrl_ref, br_ref, upper_ref,
                     x1_ref, h2_ref, routet_ref, cnt_ref, carry_ref):
    first = (pl.program_id(0) == 0) & (pl.program_id(1) == 0)

    @pl.when(first)
    def _():
        carry_ref[...] = jnp.zeros_like(carry_ref)

    pa = _dot(oa_ref[0], wa_ref[...])
    pb = _dot(ob_ref[0], wb_ref[...])
    mixed = ga_ref[0].astype(F32) * pa + gb_ref[0].astype(F32) * pb
    x1 = x_ref[0] + gt1_ref[0] * _dot(mixed.astype(BF16), wo_ref[...])
    x1_ref[0] = x1
    ms = jnp.mean(x1 * x1, axis=-1, keepdims=True)
    h2 = (x1 * lax.rsqrt(ms + NORM_EPS) * g2_ref[...]) * (1.0 + sc2_ref[0]) + sh2_ref[0]
    _rows_to_tiles(h2_ref, h2)

    h2_hi = h2.astype(BF16)
    h2_lo = (h2 - h2_hi.astype(F32)).astype(BF16)
    logits = (_dot(wr_ref[...], h2_hi, NT) + _dot(wr_ref[...], h2_lo, NT) + _dot(wrl_ref[...], h2_hi, NT)
              + br_ref[...])
    row = lax.broadcasted_iota(jnp.int32, logits.shape, 0)
    neg = jnp.float32(-jnp.inf)
    big = jnp.int32(1 << 20)
    eg = n_experts // n_groups
    is_g = (row >= n_experts) & (row < n_experts + n_groups)
    lg = jnp.where(is_g, logits, neg)
    mg = jnp.max(lg, axis=0, keepdims=True)
    p_grp = 1.0 / jnp.sum(jnp.where(is_g, jnp.exp(lg - mg), 0.0), axis=0, keepdims=True)
    gidx = jnp.min(jnp.where(lg == mg, row, big), axis=0, keepdims=True) - n_experts
    sel = (row >= gidx * eg) & (row < gidx * eg + eg)
    le = jnp.where(sel, logits, neg)
    me = jnp.max(le, axis=0, keepdims=True)
    pe_un = jnp.where(sel, jnp.exp(le - me), 0.0)
    pe = jnp.where(sel, pe_un / jnp.sum(pe_un, axis=0, keepdims=True), -1.0)
    v1 = jnp.max(pe, axis=0, keepdims=True)
    i1 = jnp.min(jnp.where(pe == v1, row, big), axis=0, keepdims=True)
    pe2 = jnp.where(row == i1, -1.0, pe)
    v2 = jnp.max(pe2, axis=0, keepdims=True)
    i2 = jnp.min(jnp.where(pe2 == v2, row, big), axis=0, keepdims=True)
    wsum = v1 + v2
    w1 = p_grp * v1 / wsum
    w2 = p_grp * v2 / wsum

    oh1 = (row == i1).astype(F32)
    oh2 = (row == i2).astype(F32)
    both = oh1 + oh2
    before = _dot(both.astype(BF16), upper_ref[...]) + carry_ref[...]
    rank1 = jnp.sum(oh1 * before, axis=0, keepdims=True)
    rank2 = jnp.sum(oh2 * before, axis=0, keepdims=True)
    carry_ref[...] = carry_ref[...] + jnp.sum(both, axis=1, keepdims=True)
    cnt_ref[...] = carry_ref[...]
    zero = jnp.zeros_like(w1)
    routet_ref[...] = jnp.concatenate(
        [w1, w2, i1.astype(F32), i2.astype(F32), rank1, rank2, zero, zero], axis=0)


def _out_proj(x, o_a, o_b, gates, mod3, norm2_g, wa, wb, wo, wr, br, n_groups, n_experts, tm):
    bsz, s, d = x.shape
    wdt = o_a.shape[-1]
    upper = jnp.asarray(np.triu(np.ones((tm, tm), np.float32), 1), BF16)
    wrt = wr.T
    wr_hi = wrt.astype(BF16)
    wr_lo = (wrt - wr_hi.astype(F32)).astype(BF16)
    const = lambda b, i: (0, 0)
    tile = lambda b, i: (b, i, 0)
    kern = functools.partial(_out_proj_kernel, n_groups, n_experts)
    return pl.pallas_call(
        kern,
        out_shape=(jax.ShapeDtypeStruct((bsz, s, d), F32),
                   jax.ShapeDtypeStruct((bsz * s * SUB, LANES), U32),
                   jax.ShapeDtypeStruct((bsz * (s // tm) * SUBLANES, tm), F32),
                   jax.ShapeDtypeStruct((LANES, 1), F32)),
        grid=(bsz, s // tm),
        in_specs=[pl.BlockSpec((1, tm, d), tile),
                  pl.BlockSpec((1, tm, wdt), tile),
                  pl.BlockSpec((1, tm, wdt), tile),
                  pl.BlockSpec((1, tm, d), lambda b, i: (b, i, 0)),
                  pl.BlockSpec((1, tm, d), lambda b, i: (b, i, 1)),
                  pl.BlockSpec((1, 1, d), lambda b, i: (b * 6 + 2, 0, 0)),
                  pl.BlockSpec((1, 1, d), lambda b, i: (b * 6 + 4, 0, 0)),
                  pl.BlockSpec((1, 1, d), lambda b, i: (b * 6 + 3, 0, 0)),
                  pl.BlockSpec((1, d), const),
                  pl.BlockSpec(wa.shape, const),
                  pl.BlockSpec(wb.shape, const),
                  pl.BlockSpec(wo.shape, const),
                  pl.BlockSpec(wrt.shape, const),
                  pl.BlockSpec(wrt.shape, const),
                  pl.BlockSpec((LANES, 1), const),
                  pl.BlockSpec((tm, tm), const)],
        out_specs=(pl.BlockSpec((1, tm, d), tile),
                   pl.BlockSpec((tm * SUB, LANES), lambda b, i: (b * (s // tm) + i, 0)),
                   pl.BlockSpec((SUBLANES, tm), lambda b, i: (b * (s // tm) + i, 0)),
                   pl.BlockSpec((LANES, 1), const)),
        scratch_shapes=[pltpu.VMEM((LANES, 1), F32)],
        compiler_params=_cparams(("arbitrary", "arbitrary")),
        name="out_proj",
    )(x, o_a, o_b, gates, gates, mod3, mod3, mod3, norm2_g.reshape(1, d), wa, wb, wo, wr_hi, wr_lo,
      br.reshape(LANES, 1), upper)


def _moe_dispatch_kernel(tm, tm_o, n_tok, blk, dest_ref, zstart_ref, zcnt_ref, nused_ref, h_ref, xb_ref, smap_ref,
                         zbuf, stage, sem, zsem):
    i = pl.program_id(0)
    n_steps = pl.num_programs(0)
    n_slots = smap_ref.shape[0]
    n_experts = zcnt_ref.shape[0]

    @pl.when(i == 0)
    def _():
        def init(s_, carry):
            smap_ref[s_] = 0
            return carry
        lax.fori_loop(nused_ref[0] * blk, n_slots, init, 0)
        zbuf[...] = jnp.zeros_like(zbuf)

        def zero_row(e, j):
            dst = pl.multiple_of((zstart_ref[e] + j) * SUB, SUB)
            return pltpu.make_async_copy(zbuf.at[pl.ds(0, SUB), :], xb_ref.at[pl.ds(dst, SUB), :], zsem)

        def zero_block(b):
            dst = pl.multiple_of(b * (blk * SUB), blk * SUB)
            return pltpu.make_async_copy(zbuf, xb_ref.at[pl.ds(dst, blk * SUB), :], zsem)

        for e in range(n_experts):
            def zstart(j, carry, e=e):
                zero_row(e, j).start()
                smap_ref[zstart_ref[e] + j] = TOP_K * n_tok + e * blk + j
                return carry
            lax.fori_loop(0, zcnt_ref[e], zstart, 0)

        def bstart(b, carry):
            zero_block(b).start()
            return carry
        lax.fori_loop(nused_ref[0], n_slots // blk, bstart, 0)
        for e in range(n_experts):
            def zwait(j, carry, e=e):
                zero_row(e, j).wait()
                return carry
            lax.fori_loop(0, zcnt_ref[e], zwait, 0)

        def bwait(b, carry):
            zero_block(b).wait()
            return carry
        lax.fori_loop(nused_ref[0], n_slots // blk, bwait, 0)

    slot = i % 2
    stage[slot] = h_ref[...]
    tok0 = i * tm
    dest0 = (tok0 // tm_o) * (TOP_K * tm_o) + tok0 % tm_o
    for r in range(tm):
        tok = tok0 + r
        for k in range(TOP_K):
            d = dest_ref[dest0 + k * tm_o + r]
            dst = pl.multiple_of(d * SUB, SUB)
            pltpu.make_async_copy(stage.at[slot, pl.ds(r * SUB, SUB), :], xb_ref.at[pl.ds(dst, SUB), :],
                                  sem.at[slot]).start(priority=k)
            smap_ref[d] = k * n_tok + tok

    def wait_step(s_):
        for _ in range(TOP_K):
            pltpu.make_async_copy(stage.at[s_], xb_ref.at[pl.ds(0, tm * SUB), :], sem.at[s_]).wait()

    @pl.when(i > 0)
    def _():
        wait_step(1 - slot)

    @pl.when(i == n_steps - 1)
    def _():
        wait_step(slot)


def _moe_dispatch(h2, dest, zstart, zcnt, n_used, n_slots, blk, tm, tm_o):
    n_tok = h2.shape[0] // SUB
    grid_spec = pltpu.PrefetchScalarGridSpec(
        num_scalar_prefetch=4,
        grid=(n_tok // tm,),
        in_specs=[pl.BlockSpec((tm * SUB, LANES), lambda i, *_: (i, 0))],
        out_specs=(pl.BlockSpec(memory_space=pl.ANY),
                   pl.BlockSpec(memory_space=pltpu.SMEM)),
        scratch_shapes=[pltpu.VMEM((blk * SUB, LANES), U32),
                        pltpu.VMEM((2, tm * SUB, LANES), U32),
                        pltpu.SemaphoreType.DMA((2,)),
                        pltpu.SemaphoreType.DMA],
    )
    return pl.pallas_call(
        functools.partial(_moe_dispatch_kernel, tm, tm_o, n_tok, blk),
        out_shape=(jax.ShapeDtypeStruct((n_slots * SUB, LANES), U32),
                   jax.ShapeDtypeStruct((n_slots,), jnp.int32)),
        grid_spec=grid_spec,
        compiler_params=_cparams(("arbitrary",)),
        name="moe_dispatch",
    )(dest, zstart, zcnt, n_used, h2)


def _moe_slotmap_kernel(tm_o, n_tok, blk, dest_ref, zstart_ref, zcnt_ref, nused_ref, smap_ref):
    i = pl.program_id(0)
    n_slots = smap_ref.shape[0] - blk
    n_experts = zcnt_ref.shape[0]

    @pl.when(i == 0)
    def _():
        def init(s_, carry):
            smap_ref[blk + s_] = TOP_K * n_tok
            return carry
        lax.fori_loop(nused_ref[0] * blk, n_slots, init, 0)
        for r in range(blk):
            smap_ref[r] = TOP_K * n_tok + n_experts * blk + r
        for e in range(n_experts):
            def pad(j, carry, e=e):
                smap_ref[blk + zstart_ref[e] + j] = TOP_K * n_tok + e * blk + j
                return carry
            lax.fori_loop(0, zcnt_ref[e], pad, 0)

    base = i * (TOP_K * tm_o)
    both = 1 + (1 << SMAP_BITS)
    for k in range(TOP_K):
        v0 = k * n_tok + (i * tm_o) * both
        for r in range(tm_o):
            smap_ref[dest_ref[base + k * tm_o + r]] = v0 + r * both


def _moe_slotmap(dest, zstart, zcnt, n_used, n_slots, blk, tm_o, n_tok):
    grid_spec = pltpu.PrefetchScalarGridSpec(
        num_scalar_prefetch=4,
        grid=(n_tok // tm_o,),
        in_specs=[],
        out_specs=pl.BlockSpec(memory_space=pltpu.SMEM),
    )
    n_experts = zcnt.shape[0]
    assert TOP_K * n_tok + (n_experts + 1) * blk <= (1 << SMAP_BITS) and n_tok <= (1 << (31 - SMAP_BITS))
    return pl.pallas_call(
        functools.partial(_moe_slotmap_kernel, tm_o, n_tok, blk),
        out_shape=jax.ShapeDtypeStruct((blk + n_slots,), jnp.int32),
        grid_spec=grid_spec,
        compiler_params=_cparams(("arbitrary",)),
        name="moe_slotmap",
    )(dest, zstart, zcnt, n_used)


def _moe_expert_gather_kernel(n_tok, blk_e_ref, nused_ref, smap_ref, h_ref, wg_ref, wu_ref, wd_ref, y_ref,
                              hv, xbuf, ystage, wgb, wub, wdb, sem, hsem):
    i = pl.program_id(0)
    nb = pl.num_programs(0)
    n_used = nused_ref[0]
    blk = xbuf.shape[1] // SUB
    first_real_blocks = TOP_K * n_tok // blk

    def gather(b, slot, rows):
        for r in rows:
            tok = lax.shift_right_logical(smap_ref[(b + 1) * blk + r], SMAP_BITS)
            src = pl.multiple_of(tok * SUB, SUB)
            xbuf[slot, pl.ds(r * SUB, SUB), :] = hv[pl.ds(src, SUB), :]

    def issue(b, slot, rows):
        for r in rows:
            t = smap_ref[(b + 1) * blk + r] & ((1 << SMAP_BITS) - 1)
            dst = pl.multiple_of(t * SUB, SUB)
            pltpu.make_async_copy(ystage.at[slot, pl.ds(r * SUB, SUB), :], y_ref.at[pl.ds(dst, SUB), :],
                                  sem.at[slot]).start(priority=r % 2)

    def wait_block(slot):
        pltpu.make_async_copy(ystage.at[slot], y_ref.at[pl.ds(0, blk * SUB), :], sem.at[slot]).wait()

    @pl.when(i == 0)
    def _():
        load = pltpu.make_async_copy(h_ref, hv, hsem)
        load.start()
        ystage[...] = jnp.zeros_like(ystage)
        n_spare_blocks = y_ref.shape[0] // (blk * SUB) - first_real_blocks

        def spare_copy(c):
            dst = (first_real_blocks + c) * blk * SUB
            return pltpu.make_async_copy(ystage.at[0], y_ref.at[pl.ds(dst, blk * SUB), :], sem.at[0])

        for c in range(n_spare_blocks):
            spare_copy(c).start()
        for c in range(n_spare_blocks):
            spare_copy(c).wait()
        load.wait()
        gather(0, 0, range(blk))

    new_expert = (i == 0) | (blk_e_ref[i] != blk_e_ref[jnp.maximum(i - 1, 0)])

    @pl.when((i < n_used) & new_expert)
    def _():
        wgb[...] = wg_ref[0].astype(BF16)
        wub[...] = wu_ref[0].astype(BF16)
        wdb[...] = wd_ref[0].astype(BF16)

    @pl.when(i < n_used)
    def _():
        slot = i % 2
        pslot = 1 - slot
        nxt = jnp.minimum(i + 1, nb - 1)

        @pl.when(i > 0)
        def _():
            wait_block(slot)

        q = blk // 4
        xb = _tiles_to_rows(xbuf.at[slot], blk).astype(BF16)
        issue(i - 1, pslot, range(0, q))
        gather(nxt, pslot, range(0, q))
        hg = _dot(xb, wgb[...])
        issue(i - 1, pslot, range(q, 2 * q))
        gather(nxt, pslot, range(q, 2 * q))
        hu = _dot(xb, wub[...])
        issue(i - 1, pslot, range(2 * q, 3 * q))
        gather(nxt, pslot, range(2 * q, 3 * q))
        hid = (_silu(hg) * hu).astype(BF16)
        y = _dot(hid, wdb[...])
        issue(i - 1, pslot, range(3 * q, blk))
        gather(nxt, pslot, range(3 * q, blk))
        _rows_to_tiles(ystage.at[slot], y)

        @pl.when(i == n_used - 1)
        def _():
            issue(i, slot, range(blk))
            wait_block(pslot)
            wait_block(slot)


def _moe_experts_gather(h2, smap, blk_e, n_used, w_gate, w_up, w_down, blk, n_tok):
    d = w_gate.shape[1]
    assert d == 2 * SUB * LANES and h2.shape == (n_tok * SUB, LANES)
    n_slots = smap.shape[0] - blk
    nb = n_slots // blk
    f = w_gate.shape[-1]
    n_experts = w_gate.shape[0]
    assert (TOP_K * n_tok) % blk == 0
    n_tiles = TOP_K * n_tok + (n_experts + 1) * blk
    grid_spec = pltpu.PrefetchScalarGridSpec(
        num_scalar_prefetch=3,
        grid=(nb,),
        in_specs=[pl.BlockSpec(memory_space=pl.ANY),
                  pl.BlockSpec((1, d, f), lambda i, be, nu, sm: (be[i], 0, 0)),
                  pl.BlockSpec((1, d, f), lambda i, be, nu, sm: (be[i], 0, 0)),
                  pl.BlockSpec((1, f, d), lambda i, be, nu, sm: (be[i], 0, 0))],
        out_specs=pl.BlockSpec(memory_space=pl.ANY),
        scratch_shapes=[pltpu.VMEM((n_tok * SUB, LANES), U32),
                        pltpu.VMEM((2, blk * SUB, LANES), U32),
                        pltpu.VMEM((2, blk * SUB, LANES), U32),
                        pltpu.VMEM((d, f), BF16),
                        pltpu.VMEM((d, f), BF16),
                        pltpu.VMEM((f, d), BF16),
                        pltpu.SemaphoreType.DMA((2,)),
                        pltpu.SemaphoreType.DMA],
    )
    return pl.pallas_call(
        functools.partial(_moe_expert_gather_kernel, n_tok),
        out_shape=jax.ShapeDtypeStruct((n_tiles * SUB, LANES), U32),
        grid_spec=grid_spec,
        compiler_params=_cparams(("arbitrary",)),
        name="moe_experts",
    )(blk_e, n_used, smap, h2, w_gate, w_up, w_down)


def _moe_expert_kernel(n_tok, blk_e_ref, nused_ref, smap_ref, x_ref, wg_ref, wu_ref, wd_ref, y_ref,
                       ystage, wgb, wub, wdb, sem):
    i = pl.program_id(0)
    n_used = nused_ref[0]
    blk = x_ref.shape[0] // SUB
    first_real_blocks = TOP_K * n_tok // blk
    first_spare = y_ref.shape[0] // SUB - blk

    def issue(b, slot, rows):
        for r in rows:
            t = jnp.where(b >= 0, smap_ref[jnp.maximum(b, 0) * blk + r], first_spare + r)
            dst = pl.multiple_of(t * SUB, SUB)
            pltpu.make_async_copy(ystage.at[slot, pl.ds(r * SUB, SUB), :], y_ref.at[pl.ds(dst, SUB), :],
                                  sem.at[slot]).start(priority=r % 2)

    def wait_block(slot):
        pltpu.make_async_copy(ystage.at[slot], y_ref.at[pl.ds(0, blk * SUB), :], sem.at[slot]).wait()

    @pl.when(i == 0)
    def _():
        ystage[...] = jnp.zeros_like(ystage)
        n_spare_blocks = y_ref.shape[0] // (blk * SUB) - first_real_blocks

        def spare_copy(c):
            dst = (first_real_blocks + c) * blk * SUB
            return pltpu.make_async_copy(ystage.at[0], y_ref.at[pl.ds(dst, blk * SUB), :], sem.at[0])

        for c in range(n_spare_blocks):
            spare_copy(c).start()
        for c in range(n_spare_blocks):
            spare_copy(c).wait()

    new_expert = (i == 0) | (blk_e_ref[i] != blk_e_ref[jnp.maximum(i - 1, 0)])

    @pl.when((i < n_used) & new_expert)
    def _():
        wgb[...] = wg_ref[0].astype(BF16)
        wub[...] = wu_ref[0].astype(BF16)
        wdb[...] = wd_ref[0].astype(BF16)

    @pl.when(i < n_used)
    def _():
        slot = i % 2
        pslot = 1 - slot

        @pl.when(i > 0)
        def _():
            wait_block(slot)

        q = blk // 4
        xb = _tiles_to_rows(x_ref, blk).astype(BF16)
        issue(i - 1, pslot, range(0, q))
        hg = _dot(xb, wgb[...])
        issue(i - 1, pslot, range(q, 2 * q))
        hu = _dot(xb, wub[...])
        issue(i - 1, pslot, range(2 * q, 3 * q))
        hid = (_silu(hg) * hu).astype(BF16)
        y = _dot(hid, wdb[...])
        issue(i - 1, pslot, range(3 * q, blk))
        _rows_to_tiles(ystage.at[slot], y)

        @pl.when(i == n_used - 1)
        def _():
            issue(i, slot, range(blk))
            wait_block(pslot)
            wait_block(slot)


def _moe_experts(xb, smap, blk_e, n_used, w_gate, w_up, w_down, blk, n_tok):
    d = w_gate.shape[1]
    assert d == 2 * SUB * LANES and xb.shape[1] == LANES
    n_slots = smap.shape[0]
    nb = n_slots // blk
    f = w_gate.shape[-1]
    n_experts = w_gate.shape[0]
    assert (TOP_K * n_tok) % blk == 0
    n_tiles = TOP_K * n_tok + (n_experts + 1) * blk
    grid_spec = pltpu.PrefetchScalarGridSpec(
        num_scalar_prefetch=3,
        grid=(nb,),
        in_specs=[pl.BlockSpec((blk * SUB, LANES), lambda i, be, nu, sm: (jnp.minimum(i, nu[0] - 1), 0)),
                  pl.BlockSpec((1, d, f), lambda i, be, nu, sm: (be[i], 0, 0)),
                  pl.BlockSpec((1, d, f), lambda i, be, nu, sm: (be[i], 0, 0)),
                  pl.BlockSpec((1, f, d), lambda i, be, nu, sm: (be[i], 0, 0))],
        out_specs=pl.BlockSpec(memory_space=pl.ANY),
        scratch_shapes=[pltpu.VMEM((2, blk * SUB, LANES), U32),
                        pltpu.VMEM((d, f), BF16),
                        pltpu.VMEM((d, f), BF16),
                        pltpu.VMEM((f, d), BF16),
                        pltpu.SemaphoreType.DMA((2,))],
    )
    return pl.pallas_call(
        functools.partial(_moe_expert_kernel, n_tok),
        out_shape=jax.ShapeDtypeStruct((n_tiles * SUB, LANES), U32),
        grid_spec=grid_spec,
        compiler_params=_cparams(("arbitrary",)),
        name="moe_experts",
    )(blk_e, n_used, smap, xb, w_gate, w_up, w_down)


def _moe_combine_kernel(x1_ref, routet_ref, gt2_ref, fg_ref, y1_ref, y2_ref, o_ref):
    tm = x1_ref.shape[0]
    route = routet_ref[...].T
    moe = route[:, 0:1] * _tiles_to_rows(y1_ref, tm) + route[:, 1:2] * _tiles_to_rows(y2_ref, tm)
    xo = x1_ref[...] + gt2_ref[0] * moe
    ms = jnp.mean(xo * xo, axis=-1, keepdims=True)
    o_ref[...] = xo * lax.rsqrt(ms + NORM_EPS) * fg_ref[...]


def _moe_combine(x1, route_t, yt, mod3, final_g, s, tm):
    n, d = x1.shape
    tiles_per_batch = s // tm
    n_steps = n // tm
    per_o = route_t.shape[1] // tm
    return pl.pallas_call(
        _moe_combine_kernel,
        out_shape=jax.ShapeDtypeStruct((n, d), F32),
        grid=(n_steps,),
        in_specs=[pl.BlockSpec((tm, d), lambda i: (i, 0)),
                  pl.BlockSpec((SUBLANES, tm), lambda i: (i // per_o, i % per_o)),
                  pl.BlockSpec((1, 1, d), lambda i: ((i // tiles_per_batch) * 6 + 5, 0, 0)),
                  pl.BlockSpec((1, d), lambda i: (0, 0)),
                  pl.BlockSpec((tm * SUB, LANES), lambda i: (i, 0)),
                  pl.BlockSpec((tm * SUB, LANES), lambda i: (n_steps + i, 0))],
        out_specs=pl.BlockSpec((tm, d), lambda i: (i, 0)),
        compiler_params=_cparams(("arbitrary",)),
        name="moe_combine",
    )(x1, route_t, mod3, final_g.reshape(1, d), yt, yt)


def _pick(n, candidates):
    for t in candidates:
        if n % t == 0:
            return t
    raise ValueError(f"no tile in {candidates} divides {n}")


def kernel(x, c, ada_w, ada_b, norm1_g, w_in, hg_lb, hg_norm_g, rw_mu, rw_w0, rw_w2, rw_a0, rw_a2, rw_g2, rw_kk, rw_ka, rw_rk, rw_gn_w, rw_gn_b, w_proj_a, w_proj_b, w_out, norm2_g, router_g_w, router_g_b, router_e_w, router_e_b, exp_w_gate, exp_w_up, exp_w_down, final_g):
    bsz, s, d = x.shape
    depth = ada_w.shape[0]
    hg_f = hg_lb.shape[-1]
    hg_w = hg_norm_g.shape[-1]
    rw_w = rw_w0.shape[-1]
    rw_cols = rw_mu.shape[-1]
    n_groups = router_g_w.shape[-1]
    n_experts = router_e_w.shape[-1]
    assert hg_f == hg_w and s % CHUNK == 0 and n_experts + n_groups <= LANES and d == 2 * SUB * LANES

    lb_all = jnp.cumsum(jax.nn.softmax(hg_lb.astype(F32), axis=0), axis=0)
    n = bsz * s
    blk = 256
    n_blocks = (n * TOP_K + n_experts * blk) // blk
    for l in range(depth):
        mod = _ada_mod(c, ada_w[l], ada_b[l])
        mod3 = mod.reshape(bsz * 6, 1, d)

        hg_cols = 2 * hg_f + 2 * hg_w
        rw_pad = -(-rw_cols // 256) * 256
        wl = w_in[l]
        w_hg = wl[:, :hg_cols].astype(BF16)
        w_rw = jnp.zeros((d, rw_pad), BF16).at[:, :rw_cols].set(wl[:, hg_cols:hg_cols + rw_cols].astype(BF16))
        w_gt = wl[:, hg_cols + rw_cols:].astype(BF16)
        hg, rw, gates = _in_proj(x, mod3, norm1_g[l], w_hg, w_rw, w_gt, _pick(s, (512, 256, 128, 64)))

        o_a = _hgrn2(hg, lb_all[l], hg_norm_g[l], _pick(s, (512, 256, 128, 64)))
        o_b = _rwkv7(rw, rw_mu[l], rw_w0[l], rw_w2[l], rw_a0[l], rw_a2[l], rw_g2[l],
                     rw_kk[l], rw_ka[l], rw_rk[l].reshape(-1), rw_gn_w[l], rw_gn_b[l],
                     _pick(s, (2 * CHUNK, CHUNK)))

        wr = jnp.zeros((d, LANES), F32).at[:, :n_experts].set(router_e_w[l])
        wr = wr.at[:, n_experts:n_experts + n_groups].set(router_g_w[l])
        br = jnp.zeros((1, LANES), F32).at[0, :n_experts].set(router_e_b[l])
        br = br.at[0, n_experts:n_experts + n_groups].set(router_g_b[l])
        tm_o = _pick(s, (512, 256, 128, 64))
        x1, h2, route_t, counts = _out_proj(
            x, o_a, o_b, gates, mod3, norm2_g[l],
            w_proj_a[l].astype(BF16), w_proj_b[l].astype(BF16), w_out[l].astype(BF16),
            wr, br, n_groups, n_experts, tm_o)

        rt = route_t.reshape(n // tm_o, SUBLANES, tm_o)
        eid = rt[:, 2:4, :].astype(jnp.int32)
        rank = rt[:, 4:6, :].astype(jnp.int32)
        cnt = counts[:n_experts, 0].astype(jnp.int32)
        padded = (cnt + blk - 1) // blk * blk
        pad_end = jnp.cumsum(padded)
        pad_start = pad_end - padded
        e_ax = jnp.arange(n_experts, dtype=jnp.int32)[:, None, None, None]
        dest = rank + jnp.sum(jnp.where(eid[None] == e_ax, pad_start[:, None, None, None], 0), axis=0)
        blk_start = jnp.arange(n_blocks, dtype=jnp.int32) * blk
        blk_e = jnp.minimum(jnp.sum((pad_end[None, :] <= blk_start[:, None]).astype(jnp.int32), axis=1),
                            n_experts - 1)
        n_used = (pad_end[-1:] // blk).astype(jnp.int32)
        tm = _pick(s, (256, 128, 64))
        smap = _moe_slotmap(dest.reshape(-1) + blk, pad_start + cnt, padded - cnt, n_used, n_blocks * blk, blk, tm_o, n)
        yt = _moe_experts_gather(h2, smap, blk_e, n_used, exp_w_gate[l], exp_w_up[l], exp_w_down[l], blk, n)
        last = l == depth - 1
        assert last, "the final RMSNorm is fused into the last layer's combine"
        out = _moe_combine(x1.reshape(n, d), route_t, yt, mod3, final_g, s, tm)
        x = out.reshape(bsz, s, d)
    return x
```

```python
import functools

import numpy as np
import jax
import jax.numpy as jnp
from jax import lax
from jax.experimental import pallas as pl
from jax.experimental.pallas import tpu as pltpu

F32 = jnp.float32
BF16 = jnp.bfloat16
HIGHEST = lax.Precision.HIGHEST

NORM_EPS = 1e-6
HG_HEAD = 128
RW_HEAD = 64
RW_GN_EPS = 64e-5
TOP_K = 2
CHUNK = 64
LANES = 128
SUB = 4
U32 = jnp.uint32
MXU_K = 256
SUBLANES = 8
SMAP_BITS = 16
VMEM_LIMIT = 56 * 1024 * 1024

NT = (((1,), (1,)), ((), ()))
TN = (((0,), (0,)), ((), ()))


def _dot(a, b, dims=None, precision=None):
    if dims is None:
        return jnp.dot(a, b, preferred_element_type=F32, precision=precision)
    return lax.dot_general(a, b, dims, preferred_element_type=F32, precision=precision)


def _split3(x):
    hi = x.astype(BF16)
    r1 = x - hi.astype(F32)
    mid = r1.astype(BF16)
    lo = (r1 - mid.astype(F32)).astype(BF16)
    return hi, mid, lo


def _dot_exact_lhs(m3_bf16, x):
    return _dot(m3_bf16, jnp.concatenate(_split3(x), axis=0))


def _head_sums(x, m2_bf16):
    outs = []
    for g0 in range(0, x.shape[1], MXU_K):
        xg = x[:, g0:g0 + MXU_K]
        hi = xg.astype(BF16)
        lo = (xg - hi.astype(F32)).astype(BF16)
        outs.append(_dot(jnp.concatenate([hi, lo], axis=1), m2_bf16))
    return jnp.concatenate(outs, axis=1)


def _sigmoid(x):
    return 1.0 / (1.0 + jnp.exp(-x))


def _silu(x):
    return x * _sigmoid(x)


def _rows_to_tiles(ref, val):
    m, half = val.shape[0], val.shape[1] // 2
    hi = lax.bitcast_convert_type(val[:, :half].astype(BF16).astype(F32), U32)
    lo = lax.bitcast_convert_type(val[:, half:].astype(BF16).astype(F32), U32)
    w = (hi & jnp.uint32(0xFFFF0000)) | (lo >> 16)
    for j in range(SUB):
        ref[pl.ds(j, m, stride=SUB), :] = w[:, j * LANES:(j + 1) * LANES]


def _tiles_to_rows(ref, m, base=0):
    w = jnp.concatenate([ref[pl.ds(base * SUB + j, m, stride=SUB), :] for j in range(SUB)], axis=1)
    hi = lax.bitcast_convert_type(w & jnp.uint32(0xFFFF0000), F32)
    lo = lax.bitcast_convert_type(w << 16, F32)
    return jnp.concatenate([hi, lo], axis=1)


def _cparams(sem):
    return pltpu.CompilerParams(dimension_semantics=sem, vmem_limit_bytes=VMEM_LIMIT)


def _ada_kernel(c_ref, w_ref, b_ref, o_ref):
    c = c_ref[...]
    o_ref[...] = _dot(_silu(c), w_ref[...], precision=HIGHEST) + b_ref[...]


def _ada_mod(c, w, b):
    bsz, d = c.shape
    n = w.shape[1]
    rows = 8
    cp = jnp.zeros((rows, d), F32).at[:bsz].set(c)
    tn = 1536
    out = pl.pallas_call(
        _ada_kernel,
        out_shape=jax.ShapeDtypeStruct((rows, n), F32),
        grid=(n // tn,),
        in_specs=[pl.BlockSpec((rows, d), lambda j: (0, 0)),
                  pl.BlockSpec((d, tn), lambda j: (0, j)),
                  pl.BlockSpec((1, tn), lambda j: (0, j))],
        out_specs=pl.BlockSpec((rows, tn), lambda j: (0, j)),
        compiler_params=_cparams(("arbitrary",)),
        name="ada_mod",
    )(cp, w, b.reshape(1, n))
    return out[:bsz]


def _in_proj_kernel(x_ref, sh_ref, sc_ref, g_ref, whg_ref, wrw_ref, wgt_ref, hg_ref, rw_ref, gt_ref):
    x = x_ref[0]
    ms = jnp.mean(x * x, axis=-1, keepdims=True)
    h = (x * lax.rsqrt(ms + NORM_EPS) * g_ref[...]) * (1.0 + sc_ref[0]) + sh_ref[0]
    hb = h.astype(BF16)
    step = 512
    for n0 in range(0, whg_ref.shape[1], step):
        hg_ref[0, :, n0:n0 + step] = _dot(hb, whg_ref[:, n0:n0 + step])
    for n0 in range(0, wrw_ref.shape[1], 256):
        rw_ref[0, :, n0:n0 + 256] = _dot(hb, wrw_ref[:, n0:n0 + 256])
    for n0 in range(0, wgt_ref.shape[1], step):
        gt_ref[0, :, n0:n0 + step] = _sigmoid(_dot(hb, wgt_ref[:, n0:n0 + step])).astype(BF16)


def _in_proj(x, mod3, norm_g, w_hg, w_rw, w_gt, tm):
    bsz, s, d = x.shape
    n_hg, n_rw, n_gt = w_hg.shape[1], w_rw.shape[1], w_gt.shape[1]
    const = lambda b, i: (0, 0)
    return pl.pallas_call(
        _in_proj_kernel,
        out_shape=(jax.ShapeDtypeStruct((bsz, s, n_hg), F32),
                   jax.ShapeDtypeStruct((bsz, s, n_rw), F32),
                   jax.ShapeDtypeStruct((bsz, s, n_gt), BF16)),
        grid=(bsz, s // tm),
        in_specs=[pl.BlockSpec((1, tm, d), lambda b, i: (b, i, 0)),
                  pl.BlockSpec((1, 1, d), lambda b, i: (b * 6 + 0, 0, 0)),
                  pl.BlockSpec((1, 1, d), lambda b, i: (b * 6 + 1, 0, 0)),
                  pl.BlockSpec((1, d), const),
                  pl.BlockSpec((d, n_hg), const),
                  pl.BlockSpec((d, n_rw), const),
                  pl.BlockSpec((d, n_gt), const)],
        out_specs=(pl.BlockSpec((1, tm, n_hg), lambda b, i: (b, i, 0)),
                   pl.BlockSpec((1, tm, n_rw), lambda b, i: (b, i, 0)),
                   pl.BlockSpec((1, tm, n_gt), lambda b, i: (b, i, 0))),
        compiler_params=_cparams(("arbitrary", "arbitrary")),
        name="in_proj",
    )(x, mod3, mod3, norm_g.reshape(1, d), w_hg, w_rw, w_gt)


_HG_LEVELS = (32, 16, 8, 4, 2, 1)


def _hgrn2_consts(width):
    c = CHUNK
    t = np.arange(c)[:, None]
    s = np.arange(c)[None, :]
    blocks = [(s <= t), (s > t)]
    lvl_masks = []
    right = []
    for h in _HG_LEVELS:
        m = (t // (2 * h)) * 2 * h + h
        is_r = (t & h) != 0
        blk = np.where(is_r, (s >= m) & (s <= t), (s > t) & (s <= m - 1))
        blocks.append(blk)
        lvl_masks.append(is_r & ((s & h) == 0) & ((t // (2 * h)) == (s // (2 * h))))
        right.append(np.broadcast_to(is_r, (c, width)))
    mst = np.tile(np.concatenate(blocks, axis=0).astype(np.float32), (1, 3))
    lm = np.stack([np.eye(c, dtype=bool)] + lvl_masks).astype(np.float32)
    rm = np.stack(right).astype(np.float32)
    return jnp.asarray(mst, BF16), jnp.asarray(lm, F32), jnp.asarray(rm, F32)


def _hgrn2_kernel(q_ref, f_ref, i_ref, g_ref, lb_ref, ng_ref, mst_ref, lm_ref, rm_ref, o_ref, st_ref):
    c = CHUNK
    n_chunks = q_ref.shape[1] // c

    @pl.when(pl.program_id(1) == 0)
    def _():
        st_ref[...] = jnp.zeros_like(st_ref)

    nsub = 2 if n_chunks % 2 == 0 else 1

    def chunk_body(ci, carry):
        r0 = pl.multiple_of(ci * (nsub * c), nsub * c)
        for _ in _hgrn2_steps(q_ref, f_ref, i_ref, g_ref, lb_ref, ng_ref, mst_ref, lm_ref, rm_ref, o_ref, st_ref,
                              0, r0, nsub):
            pass
        return carry

    lax.fori_loop(0, n_chunks // nsub, chunk_body, 0)


def _hgrn2_steps(q_ref, f_ref, i_ref, g_ref, lb_ref, ng_ref, mst_ref, lm_ref, rm_ref, o_ref, st_ref, b, r0, nsub):
    c = CHUNK
    n_heads = q_ref.shape[2] // HG_HEAD
    mst = mst_ref[...]
    lb = lb_ref[...]
    ng = ng_ref[...]
    heads = [slice(hd * HG_HEAD, (hd + 1) * HG_HEAD) for hd in range(n_heads)]
    subs = []
    for j in range(nsub):
        rows = pl.ds(r0 + j * c, c)
        q = _silu(q_ref[b, rows, :])
        f = lb + (1.0 - lb) * _sigmoid(f_ref[b, rows, :])
        k = 1.0 - f
        ex = jnp.exp(_dot_exact_lhs(mst, jnp.log(f)))
        subs.append(dict(rows=rows, q=q, k=k, ex=ex, vb=i_ref[b, rows, :].astype(BF16),
                         qd=(q * ex[0:c]).astype(BF16), kr=(k * ex[c:2 * c]).astype(BF16)))
        yield
    for sb in subs:
        qb, kb = sb['q'].astype(BF16), sb['k'].astype(BF16)
        sb['sc'] = [lm_ref[0] * _dot(qb[:, ls], kb[:, ls], NT) for ls in heads]
        sb['dqk'] = sb['q'] - sb['k']
    yield
    for li in range(len(_HG_LEVELS)):
        for sb in subs:
            g_l = ((sb['k'] + rm_ref[li] * sb['dqk']) * sb['ex'][(2 + li) * c:(3 + li) * c]).astype(BF16)
            sb['sc'] = [s_h + lm_ref[li + 1] * _dot(g_l[:, ls], g_l[:, ls], NT)
                        for s_h, ls in zip(sb['sc'], heads)]
        yield
    for sb in subs:
        sb['kv'] = [_dot(sb['vb'][:, ls], sb['kr'][:, ls], TN) for ls in heads]
        sb['o'] = [_dot(s_h.astype(BF16), sb['vb'][:, ls]) for s_h, ls in zip(sb['sc'], heads)]
        yield
    sts = [st_ref[b, hd] for hd in range(n_heads)]
    for sb in subs:
        sb['o'] = [o_h + _dot(sb['qd'][:, ls], st.astype(BF16), NT) for o_h, ls, st in zip(sb['o'], heads, sts)]
        sts = [st * sb['ex'][c - 1:c, ls] + kv for st, ls, kv in zip(sts, heads, sb['kv'])]
        yield
    for hd in range(n_heads):
        st_ref[b, hd] = sts[hd]
    for sb in subs:
        on = [o_h * lax.rsqrt(jnp.mean(o_h * o_h, axis=-1, keepdims=True) + NORM_EPS) for o_h in sb['o']]
        o_full = jnp.concatenate(on, axis=1) * ng
        o_ref[b, sb['rows'], :] = (o_full * _silu(g_ref[b, sb['rows'], :])).astype(o_ref.dtype)
        yield


def _hgrn2(hg, lb, norm_g, ts):
    bsz, s, n4 = hg.shape
    w = n4 // 4
    mst, lm, rm = _hgrn2_consts(w)
    n_heads = w // HG_HEAD
    const2 = lambda b, i: (0, 0)
    const3 = lambda b, i: (0, 0, 0)
    return pl.pallas_call(
        _hgrn2_kernel,
        out_shape=jax.ShapeDtypeStruct((bsz, s, w), BF16),
        grid=(bsz, s // ts),
        in_specs=[pl.BlockSpec((1, ts, w), lambda b, i: (b, i, 0)),
                  pl.BlockSpec((1, ts, w), lambda b, i: (b, i, 1)),
                  pl.BlockSpec((1, ts, w), lambda b, i: (b, i, 2)),
                  pl.BlockSpec((1, ts, w), lambda b, i: (b, i, 3)),
                  pl.BlockSpec((1, w), const2),
                  pl.BlockSpec((1, w), const2),
                  pl.BlockSpec(mst.shape, const2),
                  pl.BlockSpec(lm.shape, const3),
                  pl.BlockSpec(rm.shape, const3)],
        out_specs=pl.BlockSpec((1, ts, w), lambda b, i: (b, i, 0)),
        scratch_shapes=[pltpu.VMEM((1, n_heads, HG_HEAD, HG_HEAD), F32)],
        compiler_params=_cparams(("arbitrary", "arbitrary")),
        name="hgrn2",
    )(hg, hg, hg, hg, lb.reshape(1, w), norm_g.reshape(1, w), mst, lm, rm)


def _rwkv_consts(width):
    c = CHUNK
    t = np.arange(c)[:, None]
    s = np.arange(c)[None, :]
    tri = np.tile((s <= t).astype(np.float32), (1, 3))
    tt = np.arange(2 * c)[:, None]
    ss = np.arange(2 * c)[None, :]
    same = (tt // c) == (ss // c)
    strict = same & ((ss % c) < (tt % c))
    incl = same & ((ss % c) <= (tt % c))
    hsum = (np.arange(MXU_K)[:, None] // RW_HEAD) == (np.arange(MXU_K)[None, :] // RW_HEAD)
    hsum = np.tile(hsum, (2, 1))
    return (jnp.asarray(tri, BF16), jnp.asarray(strict.astype(np.float32), F32),
            jnp.asarray(incl.astype(np.float32), F32), jnp.asarray(hsum.astype(np.float32), BF16))


def _rwkv7_kernel(p_ref, mu_ref, w0_ref, a0_ref, kk_ref, ka_ref, rk_ref, gnw_ref, gnb_ref,
                  w2_ref, a2_ref, g2_ref, tri_ref, sm_ref, im_ref, hs_ref,
                  o_ref, carry_ref, zt_ref):
    @pl.when(pl.program_id(0) == 0)
    def _():
        carry_ref[...] = jnp.zeros_like(carry_ref)
        zt_ref[...] = jnp.zeros_like(zt_ref)

    for _ in _rwkv7_steps(p_ref, mu_ref, w0_ref, a0_ref, kk_ref, ka_ref, rk_ref, gnw_ref, gnb_ref,
                          w2_ref, a2_ref, g2_ref, tri_ref, sm_ref, im_ref, hs_ref, o_ref, carry_ref, zt_ref):
        pass


def _rwkv7_steps(p_ref, mu_ref, w0_ref, a0_ref, kk_ref, ka_ref, rk_ref, gnw_ref, gnb_ref,
                 w2_ref, a2_ref, g2_ref, tri_ref, sm_ref, im_ref, hs_ref, o_ref, carry_ref, zt_ref):
    c = CHUNK
    nb = p_ref.shape[0]
    nch = p_ref.shape[1] // c
    width = o_ref.shape[2]
    n_pairs = width // LANES

    hs = hs_ref[...]
    tri = tri_ref[...]
    smask = sm_ref[...] > 0
    imask = im_ref[...] > 0
    lane = lax.broadcasted_iota(jnp.int32, (c, LANES), 1)
    m0 = (lane < RW_HEAD).astype(F32)
    m1 = 1.0 - m0

    def stack(x):
        return jnp.concatenate([x * m0, x * m1], axis=0)

    xs_rows = []
    for b in range(nb):
        p = p_ref[b]
        row = lax.broadcasted_iota(jnp.int32, p.shape, 0)
        prev = jnp.where(row == 0, carry_ref[b], pltpu.roll(p, 1, 0))
        carry_ref[b] = p[nch * c - 1:nch * c, :]
        xs_rows.append(p + mu_ref[...] * (prev - p))
    xs = jnp.concatenate(xs_rows, axis=0)
    r_all = xs[:, 0:width]
    k_all = xs[:, width:2 * width]
    v_all = xs[:, 2 * width:3 * width]
    slab = xs[:, 3 * width:]
    nz = -(w0_ref[...] + _dot(jnp.tanh(slab).astype(BF16), w2_ref[...]))
    softplus = jnp.maximum(nz, 0.0) + jnp.log(1.0 + jnp.exp(-jnp.abs(nz)))
    ld_all = -jnp.exp(-softplus - 0.5)
    a_all = _sigmoid(a0_ref[...] + _dot(slab.astype(BF16), a2_ref[...]))
    g_all = _dot(_sigmoid(slab).astype(BF16), g2_ref[...])
    kk0 = k_all * kk_ref[...]
    kk_all = kk0 * lax.rsqrt(jnp.maximum(_head_sums(kk0 * kk0, hs), 1e-24))
    k2_all = k_all * (1.0 + (a_all - 1.0) * ka_ref[...])
    yield

    units = []
    for b, j in [(b, j) for b in range(nb) for j in range(nch)]:
        rb = slice((b * nch + j) * c, (b * nch + j + 1) * c)
        r, k2, v, ld = r_all[rb], k2_all[rb], v_all[rb], ld_all[rb]
        a_in = -kk_all[rb]
        b_in = kk_all[rb] * a_all[rb]
        cum = _dot_exact_lhs(tri, ld)
        cum_t = cum[c - 1:c, :]
        e_c = jnp.exp(cum)
        e_nc = jnp.exp(-cum)
        e_rem = jnp.exp(cum_t - cum)
        at_f = a_in * jnp.exp(cum - ld)
        rt_f = r * e_c
        kt_f = k2 * e_nc
        bt_f = b_in * e_nc
        kh_f = k2 * e_rem
        bh_f = b_in * e_rem
        p_t = jnp.exp(cum_t)
        for pi in range(n_pairs):
            ls = slice(pi * LANES, (pi + 1) * LANES)
            units.append(dict(
                b=b, j=j, pi=pi,
                at=stack(at_f[:, ls]).astype(BF16), rt=stack(rt_f[:, ls]).astype(BF16),
                kt=stack(kt_f[:, ls]).astype(BF16), bt=stack(bt_f[:, ls]).astype(BF16),
                kh=stack(kh_f[:, ls]).astype(BF16), bh=stack(bh_f[:, ls]).astype(BF16),
                vs=stack(v[:, ls]).astype(BF16), p_t=p_t[:, ls]))
        yield

    for u in units:
        lhs = jnp.concatenate([u['at'], u['rt']], axis=0)
        u['g'] = _dot(lhs, jnp.concatenate([u['kt'], u['bt']], axis=0), NT)
    yield
    for u in units:
        g = u.pop('g')
        u['a_ak'] = jnp.where(smask, g[:2 * c, :2 * c], 0.0).astype(BF16)
        u['pw'] = jnp.where(smask, g[:2 * c, 2 * c:], 0.0).astype(BF16)
        u['a_r'] = jnp.where(jnp.concatenate([imask, imask], axis=1), g[2 * c:], 0.0).astype(BF16)
    for u in units:
        akv = _dot(u.pop('a_ak'), u['vs'])
        u['x'] = jnp.concatenate([u['at'].astype(F32), akv], axis=1)
    yield
    n_lvl = int(np.log2(c))
    for lvl in range(n_lvl):
        for u in units:
            u['x'] = u['x'] + _dot(u['pw'], u['x'].astype(BF16))
        yield
        if lvl + 1 < n_lvl:
            for u in units:
                u['pw'] = _dot(u['pw'], u['pw']).astype(BF16)
            yield
    for u in units:
        x = u.pop('x')
        u['wr'] = jnp.concatenate([x[:, :LANES].astype(BF16), u['rt']], axis=0)
        u['u_loc'] = x[:, LANES:]
    zt = {(b, pi): zt_ref[b, pi] for b in range(nb) for pi in range(n_pairs)}
    for j in range(nch):
        tail = [u for u in units if u['j'] == j]
        for u in tail:
            u['uy'] = _dot(u.pop('wr'), zt[u['b'], u['pi']].astype(BF16), NT)
        yield
        for u in tail:
            uy = u.pop('uy')
            u['u'] = (uy[:2 * c] + u.pop('u_loc')).astype(BF16)
            u['y0'] = uy[2 * c:]
        for u in tail:
            vu = jnp.concatenate([u['vs'], u['u']], axis=0)
            u['y'] = u.pop('y0') + _dot(u['a_r'], vu)
            upd = _dot(vu, jnp.concatenate([u['kh'], u['bh']], axis=0), TN)
            zt[u['b'], u['pi']] = zt[u['b'], u['pi']] * u['p_t'] + upd
        yield
    for (b, pi), z in zt.items():
        zt_ref[b, pi] = z

    inv_n = 1.0 / RW_HEAD
    y = jnp.concatenate(
        [jnp.concatenate([u['y'][:c] + u['y'][c:] for u in units if (u['b'], u['j']) == (b, j)], axis=1)
         for b in range(nb) for j in range(nch)], axis=0)
    mean = _head_sums(y, hs) * inv_n
    yield
    d = y - mean
    var = _head_sums(d * d, hs) * inv_n
    yield
    yn = d * lax.rsqrt(var + RW_GN_EPS) * gnw_ref[...] + gnb_ref[...]
    bonus = _head_sums(r_all * k2_all * rk_ref[...], hs) * v_all
    out = ((yn + bonus) * g_all).astype(o_ref.dtype)
    for b in range(nb):
        o_ref[b] = out[b * nch * c:(b + 1) * nch * c]


def _rwkv7(rw, mu, w0, w2, a0, a2, g2, k_k, k_a, r_k, gn_w, gn_b, ts):
    bsz, s, cols = rw.shape
    width = w0.shape[-1]
    n_pairs = width // LANES
    slab = cols - 3 * width
    dl, al, gl = w2.shape[0], a2.shape[0], g2.shape[0]
    w2f = jnp.zeros((slab, width), F32).at[0:dl].set(w2).astype(BF16)
    a2f = jnp.zeros((slab, width), F32).at[dl:dl + al].set(a2).astype(BF16)
    g2f = jnp.zeros((slab, width), F32).at[dl + al:dl + al + gl].set(g2).astype(BF16)
    mup = jnp.zeros((1, cols), F32).at[0, :mu.shape[-1]].set(mu)
    tri, sm, im, hs = _rwkv_consts(width)
    row = lambda x: x.reshape(1, width)
    const = lambda i: (0, 0)
    vec = pl.BlockSpec((1, width), const)
    return pl.pallas_call(
        _rwkv7_kernel,
        out_shape=jax.ShapeDtypeStruct((bsz, s, width), BF16),
        grid=(s // ts,),
        in_specs=[pl.BlockSpec((bsz, ts, cols), lambda i: (0, i, 0)),
                  pl.BlockSpec((1, cols), const),
                  vec, vec, vec, vec, vec, vec, vec,
                  pl.BlockSpec((slab, width), const),
                  pl.BlockSpec((slab, width), const),
                  pl.BlockSpec((slab, width), const),
                  pl.BlockSpec(tri.shape, const),
                  pl.BlockSpec(sm.shape, const),
                  pl.BlockSpec(im.shape, const),
                  pl.BlockSpec(hs.shape, const)],
        out_specs=pl.BlockSpec((bsz, ts, width), lambda i: (0, i, 0)),
        scratch_shapes=[pltpu.VMEM((bsz, 1, cols), F32),
                        pltpu.VMEM((bsz, n_pairs, LANES, LANES), F32)],
        compiler_params=_cparams(("arbitrary",)),
        name="rwkv7",
    )(rw, mup, row(w0), row(a0), row(k_k), row(k_a), row(r_k), row(gn_w), row(gn_b),
      w2f, a2f, g2f, tri, sm, im, hs)


def _out_proj_kernel(n_groups, n_experts,
                     x_ref, oa_ref, ob_ref, ga_ref, gb_ref, gt1_ref, sc2_ref, sh2_ref, g2_ref,
                     wa_ref, wb_ref, wo_ref, wr_ref, wrl_ref, br_ref, upper_ref,
                     x1_ref, h2_ref, routet_ref, cnt_ref, carry_ref):
    first = (pl.program_id(0) == 0) & (pl.program_id(1) == 0)

    @pl.when(first)
    def _():
        carry_ref[...] = jnp.zeros_like(carry_ref)

    pa = _dot(oa_ref[0], wa_ref[...])
    pb = _dot(ob_ref[0], wb_ref[...])
    mixed = ga_ref[0].astype(F32) * pa + gb_ref[0].astype(F32) * pb
    x1 = x_ref[0] + gt1_ref[0] * _dot(mixed.astype(BF16), wo_ref[...])
    x1_ref[0] = x1
    ms = jnp.mean(x1 * x1, axis=-1, keepdims=True)
    h2 = (x1 * lax.rsqrt(ms + NORM_EPS) * g2_ref[...]) * (1.0 + sc2_ref[0]) + sh2_ref[0]
    _rows_to_tiles(h2_ref, h2)

    h2_hi = h2.astype(BF16)
    h2_lo = (h2 - h2_hi.astype(F32)).astype(BF16)
    logits = (_dot(wr_ref[...], h2_hi, NT) + _dot(wr_ref[...], h2_lo, NT) + _dot(wrl_ref[...], h2_hi, NT)
              + br_ref[...])
    row = lax.broadcasted_iota(jnp.int32, logits.shape, 0)
    neg = jnp.float32(-jnp.inf)
    big = jnp.int32(1 << 20)
    eg = n_experts // n_groups
    is_g = (row >= n_experts) & (row < n_experts + n_groups)
    lg = jnp.where(is_g, logits, neg)
    mg = jnp.max(lg, axis=0, keepdims=True)
    p_grp = 1.0 / jnp.sum(jnp.where(is_g, jnp.exp(lg - mg), 0.0), axis=0, keepdims=True)
    gidx = jnp.min(jnp.where(lg == mg, row, big), axis=0, keepdims=True) - n_experts
    sel = (row >= gidx * eg) & (row < gidx * eg + eg)
    le = jnp.where(sel, logits, neg)
    me = jnp.max(le, axis=0, keepdims=True)
    pe_un = jnp.where(sel, jnp.exp(le - me), 0.0)
    pe = jnp.where(sel, pe_un / jnp.sum(pe_un, axis=0, keepdims=True), -1.0)
    v1 = jnp.max(pe, axis=0, keepdims=True)
    i1 = jnp.min(jnp.where(pe == v1, row, big), axis=0, keepdims=True)
    pe2 = jnp.where(row == i1, -1.0, pe)
    v2 = jnp.max(pe2, axis=0, keepdims=True)
    i2 = jnp.min(jnp.where(pe2 == v2, row, big), axis=0, keepdims=True)
    wsum = v1 + v2
    w1 = p_grp * v1 / wsum
    w2 = p_grp * v2 / wsum

    oh1 = (row == i1).astype(F32)
    oh2 = (row == i2).astype(F32)
    both = oh1 + oh2
    before = _dot(both.astype(BF16), upper_ref[...]) + carry_ref[...]
    rank1 = jnp.sum(oh1 * before, axis=0, keepdims=True)
    rank2 = jnp.sum(oh2 * before, axis=0, keepdims=True)
    carry_ref[...] = carry_ref[...] + jnp.sum(both, axis=1, keepdims=True)
    cnt_ref[...] = carry_ref[...]
    zero = jnp.zeros_like(w1)
    routet_ref[...] = jnp.concatenate(
        [w1, w2, i1.astype(F32), i2.astype(F32), rank1, rank2, zero, zero], axis=0)


def _out_proj(x, o_a, o_b, gates, mod3, norm2_g, wa, wb, wo, wr, br, n_groups, n_experts, tm):
    bsz, s, d = x.shape
    wdt = o_a.shape[-1]
    upper = jnp.asarray(np.triu(np.ones((tm, tm), np.float32), 1), BF16)
    wrt = wr.T
    wr_hi = wrt.astype(BF16)
    wr_lo = (wrt - wr_hi.astype(F32)).astype(BF16)
    const = lambda b, i: (0, 0)
    tile = lambda b, i: (b, i, 0)
    kern = functools.partial(_out_proj_kernel, n_groups, n_experts)
    return pl.pallas_call(
        kern,
        out_shape=(jax.ShapeDtypeStruct((bsz, s, d), F32),
                   jax.ShapeDtypeStruct((bsz * s * SUB, LANES), U32),
                   jax.ShapeDtypeStruct((bsz * (s // tm) * SUBLANES, tm), F32),
                   jax.ShapeDtypeStruct((LANES, 1), F32)),
        grid=(bsz, s // tm),
        in_specs=[pl.BlockSpec((1, tm, d), tile),
                  pl.BlockSpec((1, tm, wdt), tile),
                  pl.BlockSpec((1, tm, wdt), tile),
                  pl.BlockSpec((1, tm, d), lambda b, i: (b, i, 0)),
                  pl.BlockSpec((1, tm, d), lambda b, i: (b, i, 1)),
                  pl.BlockSpec((1, 1, d), lambda b, i: (b * 6 + 2, 0, 0)),
                  pl.BlockSpec((1, 1, d), lambda b, i: (b * 6 + 4, 0, 0)),
                  pl.BlockSpec((1, 1, d), lambda b, i: (b * 6 + 3, 0, 0)),
                  pl.BlockSpec((1, d), const),
                  pl.BlockSpec(wa.shape, const),
                  pl.BlockSpec(wb.shape, const),
                  pl.BlockSpec(wo.shape, const),
                  pl.BlockSpec(wrt.shape, const),
                  pl.BlockSpec(wrt.shape, const),
                  pl.BlockSpec((LANES, 1), const),
                  pl.BlockSpec((tm, tm), const)],
        out_specs=(pl.BlockSpec((1, tm, d), tile),
                   pl.BlockSpec((tm * SUB, LANES), lambda b, i: (b * (s // tm) + i, 0)),
                   pl.BlockSpec((SUBLANES, tm), lambda b, i: (b * (s // tm) + i, 0)),
                   pl.BlockSpec((LANES, 1), const)),
        scratch_shapes=[pltpu.VMEM((LANES, 1), F32)],
        compiler_params=_cparams(("arbitrary", "arbitrary")),
        name="out_proj",
    )(x, o_a, o_b, gates, gates, mod3, mod3, mod3, norm2_g.reshape(1, d), wa, wb, wo, wr_hi, wr_lo,
      br.reshape(LANES, 1), upper)


def _moe_slotmap_kernel(tm_o, n_tok, blk, dest_ref, zstart_ref, zcnt_ref, nused_ref, smap_ref):
    i = pl.program_id(0)
    n_slots = smap_ref.shape[0] - blk
    n_experts = zcnt_ref.shape[0]

    @pl.when(i == 0)
    def _():
        unroll = 8

        def init(g, carry):
            for u in range(unroll):
                smap_ref[blk + g * unroll + u] = TOP_K * n_tok
            return carry
        lax.fori_loop(nused_ref[0] * (blk // unroll), n_slots // unroll, init, 0)
        for r in range(blk):
            smap_ref[r] = TOP_K * n_tok + n_experts * blk + r
        for e in range(n_experts):
            def pad(g, carry, e=e):
                for u in range(unroll):
                    j = jnp.maximum(zcnt_ref[e] - 1 - (g * unroll + u), 0)
                    smap_ref[blk + zstart_ref[e] + j] = TOP_K * n_tok + e * blk + j
                return carry
            lax.fori_loop(0, (zcnt_ref[e] + unroll - 1) // unroll, pad, 0)

    base = i * (TOP_K * tm_o)
    both = 1 + (1 << SMAP_BITS)
    for k in range(TOP_K):
        v0 = k * n_tok + (i * tm_o) * both
        for r in range(tm_o):
            smap_ref[dest_ref[base + k * tm_o + r]] = v0 + r * both


def _moe_slotmap(dest, zstart, zcnt, n_used, n_slots, blk, tm_o, n_tok):
    grid_spec = pltpu.PrefetchScalarGridSpec(
        num_scalar_prefetch=4,
        grid=(n_tok // tm_o,),
        in_specs=[],
        out_specs=pl.BlockSpec(memory_space=pltpu.SMEM),
    )
    n_experts = zcnt.shape[0]
    assert TOP_K * n_tok + (n_experts + 1) * blk <= (1 << SMAP_BITS) and n_tok <= (1 << (31 - SMAP_BITS))
    return pl.pallas_call(
        functools.partial(_moe_slotmap_kernel, tm_o, n_tok, blk),
        out_shape=jax.ShapeDtypeStruct((blk + n_slots,), jnp.int32),
        grid_spec=grid_spec,
        compiler_params=_cparams(("arbitrary",)),
        name="moe_slotmap",
    )(dest, zstart, zcnt, n_used)


def _moe_expert_gather_kernel(n_tok, blk_e_ref, nused_ref, smap_ref, h_ref, wg_ref, wu_ref, wd_ref, y_ref,
                              hv, xbuf, ystage, wgb, wub, wdb, sem, hsem):
    i = pl.program_id(0)
    nb = pl.num_programs(0)
    n_used = nused_ref[0]
    blk = xbuf.shape[1] // SUB
    first_real_blocks = TOP_K * n_tok // blk

    def gather(b, slot, rows):
        for r in rows:
            tok = lax.shift_right_logical(smap_ref[(b + 1) * blk + r], SMAP_BITS)
            src = pl.multiple_of(tok * SUB, SUB)
            xbuf[slot, pl.ds(r * SUB, SUB), :] = hv[pl.ds(src, SUB), :]

    def issue(b, slot, rows):
        for r in rows:
            t = smap_ref[(b + 1) * blk + r] & ((1 << SMAP_BITS) - 1)
            dst = pl.multiple_of(t * SUB, SUB)
            pltpu.make_async_copy(ystage.at[slot, pl.ds(r * SUB, SUB), :], y_ref.at[pl.ds(dst, SUB), :],
                                  sem.at[slot]).start(priority=r % 2)

    def wait_block(slot):
        pltpu.make_async_copy(ystage.at[slot], y_ref.at[pl.ds(0, blk * SUB), :], sem.at[slot]).wait()

    @pl.when(i == 0)
    def _():
        load = pltpu.make_async_copy(h_ref, hv, hsem)
        load.start()
        ystage[...] = jnp.zeros_like(ystage)
        n_spare_blocks = y_ref.shape[0] // (blk * SUB) - first_real_blocks

        def spare_copy(c):
            dst = (first_real_blocks + c) * blk * SUB
            return pltpu.make_async_copy(ystage.at[0], y_ref.at[pl.ds(dst, blk * SUB), :], sem.at[0])

        for c in range(n_spare_blocks):
            spare_copy(c).start()
        for c in range(n_spare_blocks):
            spare_copy(c).wait()
        load.wait()
        gather(0, 0, range(blk))

    new_expert = (i == 0) | (blk_e_ref[i] != blk_e_ref[jnp.maximum(i - 1, 0)])

    @pl.when((i < n_used) & new_expert)
    def _():
        wgb[...] = wg_ref[0].astype(BF16)
        wub[...] = wu_ref[0].astype(BF16)
        wdb[...] = wd_ref[0].astype(BF16)

    @pl.when(i < n_used)
    def _():
        slot = i % 2
        pslot = 1 - slot
        nxt = jnp.minimum(i + 1, nb - 1)

        @pl.when(i > 0)
        def _():
            wait_block(slot)

        q = blk // 4
        xb = _tiles_to_rows(xbuf.at[slot], blk).astype(BF16)
        issue(i - 1, pslot, range(0, q))
        gather(nxt, pslot, range(0, q))
        hg = _dot(xb, wgb[...])
        issue(i - 1, pslot, range(q, 2 * q))
        gather(nxt, pslot, range(q, 2 * q))
        hu = _dot(xb, wub[...])
        issue(i - 1, pslot, range(2 * q, 3 * q))
        gather(nxt, pslot, range(2 * q, 3 * q))
        hid = (_silu(hg) * hu).astype(BF16)
        y = _dot(hid, wdb[...])
        issue(i - 1, pslot, range(3 * q, blk))
        gather(nxt, pslot, range(3 * q, blk))
        _rows_to_tiles(ystage.at[slot], y)

        @pl.when(i == n_used - 1)
        def _():
            issue(i, slot, range(blk))
            wait_block(pslot)
            wait_block(slot)


def _moe_experts_gather(h2, smap, blk_e, n_used, w_gate, w_up, w_down, blk, n_tok):
    d = w_gate.shape[1]
    assert d == 2 * SUB * LANES and h2.shape == (n_tok * SUB, LANES)
    n_slots = smap.shape[0] - blk
    nb = n_slots // blk
    f = w_gate.shape[-1]
    n_experts = w_gate.shape[0]
    assert (TOP_K * n_tok) % blk == 0
    n_tiles = TOP_K * n_tok + (n_experts + 1) * blk
    grid_spec = pltpu.PrefetchScalarGridSpec(
        num_scalar_prefetch=3,
        grid=(nb,),
        in_specs=[pl.BlockSpec(memory_space=pl.ANY),
                  pl.BlockSpec((1, d, f), lambda i, be, nu, sm: (be[i], 0, 0)),
                  pl.BlockSpec((1, d, f), lambda i, be, nu, sm: (be[i], 0, 0)),
                  pl.BlockSpec((1, f, d), lambda i, be, nu, sm: (be[i], 0, 0))],
        out_specs=pl.BlockSpec(memory_space=pl.ANY),
        scratch_shapes=[pltpu.VMEM((n_tok * SUB, LANES), U32),
                        pltpu.VMEM((2, blk * SUB, LANES), U32),
                        pltpu.VMEM((2, blk * SUB, LANES), U32),
                        pltpu.VMEM((d, f), BF16),
                        pltpu.VMEM((d, f), BF16),
                        pltpu.VMEM((f, d), BF16),
                        pltpu.SemaphoreType.DMA((2,)),
                        pltpu.SemaphoreType.DMA],
    )
    return pl.pallas_call(
        functools.partial(_moe_expert_gather_kernel, n_tok),
        out_shape=jax.ShapeDtypeStruct((n_tiles * SUB, LANES), U32),
        grid_spec=grid_spec,
        compiler_params=_cparams(("arbitrary",)),
        name="moe_experts",
    )(blk_e, n_used, smap, h2, w_gate, w_up, w_down)


def _moe_combine_kernel(x1_ref, routet_ref, gt2_ref, fg_ref, y1_ref, y2_ref, o_ref):
    tm = x1_ref.shape[0]
    route = routet_ref[...].T
    moe = route[:, 0:1] * _tiles_to_rows(y1_ref, tm) + route[:, 1:2] * _tiles_to_rows(y2_ref, tm)
    xo = x1_ref[...] + gt2_ref[0] * moe
    ms = jnp.mean(xo * xo, axis=-1, keepdims=True)
    o_ref[...] = xo * lax.rsqrt(ms + NORM_EPS) * fg_ref[...]


def _moe_combine(x1, route_t, yt, mod3, final_g, s, tm):
    n, d = x1.shape
    tiles_per_batch = s // tm
    n_steps = n // tm
    per_o = route_t.shape[1] // tm
    return pl.pallas_call(
        _moe_combine_kernel,
        out_shape=jax.ShapeDtypeStruct((n, d), F32),
        grid=(n_steps,),
        in_specs=[pl.BlockSpec((tm, d), lambda i: (i, 0)),
                  pl.BlockSpec((SUBLANES, tm), lambda i: (i // per_o, i % per_o)),
                  pl.BlockSpec((1, 1, d), lambda i: ((i // tiles_per_batch) * 6 + 5, 0, 0)),
                  pl.BlockSpec((1, d), lambda i: (0, 0)),
                  pl.BlockSpec((tm * SUB, LANES), lambda i: (i, 0)),
                  pl.BlockSpec((tm * SUB, LANES), lambda i: (n_steps + i, 0))],
        out_specs=pl.BlockSpec((tm, d), lambda i: (i, 0)),
        compiler_params=_cparams(("arbitrary",)),
        name="moe_combine",
    )(x1, route_t, mod3, final_g.reshape(1, d), yt, yt)


def _pick(n, candidates):
    for t in candidates:
        if n % t == 0:
            return t
    raise ValueError(f"no tile in {candidates} divides {n}")


def kernel(x, c, ada_w, ada_b, norm1_g, w_in, hg_lb, hg_norm_g, rw_mu, rw_w0, rw_w2, rw_a0, rw_a2, rw_g2, rw_kk, rw_ka, rw_rk, rw_gn_w, rw_gn_b, w_proj_a, w_proj_b, w_out, norm2_g, router_g_w, router_g_b, router_e_w, router_e_b, exp_w_gate, exp_w_up, exp_w_down, final_g):
    bsz, s, d = x.shape
    depth = ada_w.shape[0]
    hg_f = hg_lb.shape[-1]
    hg_w = hg_norm_g.shape[-1]
    rw_w = rw_w0.shape[-1]
    rw_cols = rw_mu.shape[-1]
    n_groups = router_g_w.shape[-1]
    n_experts = router_e_w.shape[-1]
    assert hg_f == hg_w and s % CHUNK == 0 and n_experts + n_groups <= LANES and d == 2 * SUB * LANES

    lb_all = jnp.cumsum(jax.nn.softmax(hg_lb.astype(F32), axis=0), axis=0)
    n = bsz * s
    blk = 512
    n_blocks = (n * TOP_K + n_experts * blk) // blk
    for l in range(depth):
        mod = _ada_mod(c, ada_w[l], ada_b[l])
        mod3 = mod.reshape(bsz * 6, 1, d)

        hg_cols = 2 * hg_f + 2 * hg_w
        rw_pad = -(-rw_cols // 256) * 256
        wl = w_in[l]
        w_hg = wl[:, :hg_cols].astype(BF16)
        w_rw = jnp.zeros((d, rw_pad), BF16).at[:, :rw_cols].set(wl[:, hg_cols:hg_cols + rw_cols].astype(BF16))
        w_gt = wl[:, hg_cols + rw_cols:].astype(BF16)
        hg, rw, gates = _in_proj(x, mod3, norm1_g[l], w_hg, w_rw, w_gt, _pick(s, (512, 256, 128, 64)))

        o_a = _hgrn2(hg, lb_all[l], hg_norm_g[l], _pick(s, (512, 256, 128, 64)))
        o_b = _rwkv7(rw, rw_mu[l], rw_w0[l], rw_w2[l], rw_a0[l], rw_a2[l], rw_g2[l],
                     rw_kk[l], rw_ka[l], rw_rk[l].reshape(-1), rw_gn_w[l], rw_gn_b[l],
                     _pick(s, (2 * CHUNK, CHUNK)))

        wr = jnp.zeros((d, LANES), F32).at[:, :n_experts].set(router_e_w[l])
        wr = wr.at[:, n_experts:n_experts + n_groups].set(router_g_w[l])
        br = jnp.zeros((1, LANES), F32).at[0, :n_experts].set(router_e_b[l])
        br = br.at[0, n_experts:n_experts + n_groups].set(router_g_b[l])
        tm_o = _pick(s, (512, 256, 128, 64))
        x1, h2, route_t, counts = _out_proj(
            x, o_a, o_b, gates, mod3, norm2_g[l],
            w_proj_a[l].astype(BF16), w_proj_b[l].astype(BF16), w_out[l].astype(BF16),
            wr, br, n_groups, n_experts, tm_o)

        rt = route_t.reshape(n // tm_o, SUBLANES, tm_o)
        eid = rt[:, 2:4, :].astype(jnp.int32)
        rank = rt[:, 4:6, :].astype(jnp.int32)
        cnt = counts[:n_experts, 0].astype(jnp.int32)
        padded = (cnt + blk - 1) // blk * blk
        pad_end = jnp.cumsum(padded)
        pad_start = pad_end - padded
        e_ax = jnp.arange(n_experts, dtype=jnp.int32)[:, None, None, None]
        dest = rank + jnp.sum(jnp.where(eid[None] == e_ax, pad_start[:, None, None, None], 0), axis=0)
        blk_start = jnp.arange(n_blocks, dtype=jnp.int32) * blk
        blk_e = jnp.minimum(jnp.sum((pad_end[None, :] <= blk_start[:, None]).astype(jnp.int32), axis=1),
                            n_experts - 1)
        n_used = (pad_end[-1:] // blk).astype(jnp.int32)
        tm = _pick(s, (256, 128, 64))
        smap = _moe_slotmap(dest.reshape(-1) + blk, pad_start + cnt, padded - cnt, n_used, n_blocks * blk, blk, tm_o, n)
        yt = _moe_experts_gather(h2, smap, blk_e, n_used, exp_w_gate[l], exp_w_up[l], exp_w_down[l], blk, n)
        last = l == depth - 1
        assert last, "the final RMSNorm is fused into the last layer's combine"
        out = _moe_combine(x1.reshape(n, d), route_t, yt, mod3, final_g, s, tm)
        x = out.reshape(bsz, s, d)
    return x
```

```python
import functools

import numpy as np
import jax
import jax.numpy as jnp
from jax import lax
from jax.experimental import pallas as pl
from jax.experimental.pallas import tpu as pltpu

F32 = jnp.float32
BF16 = jnp.bfloat16
HIGHEST = lax.Precision.HIGHEST

NORM_EPS = 1e-6
HG_HEAD = 128
RW_HEAD = 64
RW_GN_EPS = 64e-5
TOP_K = 2
CHUNK = 64
LANES = 128
SUB = 4
U32 = jnp.uint32
MXU_K = 256
SUBLANES = 8
SMAP_BITS = 16
VMEM_LIMIT = 56 * 1024 * 1024

NT = (((1,), (1,)), ((), ()))
TN = (((0,), (0,)), ((), ()))


def _dot(a, b, dims=None, precision=None):
    if dims is None:
        return jnp.dot(a, b, preferred_element_type=F32, precision=precision)
    return lax.dot_general(a, b, dims, preferred_element_type=F32, precision=precision)


def _split3(x):
    hi = x.astype(BF16)
    r1 = x - hi.astype(F32)
    mid = r1.astype(BF16)
    lo = (r1 - mid.astype(F32)).astype(BF16)
    return hi, mid, lo


def _dot_exact_lhs(m3_bf16, x):
    return _dot(m3_bf16, jnp.concatenate(_split3(x), axis=0))


def _head_sums(x, m2_bf16):
    outs = []
    for g0 in range(0, x.shape[1], MXU_K):
        xg = x[:, g0:g0 + MXU_K]
        hi = xg.astype(BF16)
        lo = (xg - hi.astype(F32)).astype(BF16)
        outs.append(_dot(jnp.concatenate([hi, lo], axis=1), m2_bf16))
    return jnp.concatenate(outs, axis=1)


def _sigmoid(x):
    return 1.0 / (1.0 + jnp.exp(-x))


def _silu(x):
    return x * _sigmoid(x)


def _rows_to_tiles(ref, val):
    m, half = val.shape[0], val.shape[1] // 2
    hi = lax.bitcast_convert_type(val[:, :half].astype(BF16).astype(F32), U32)
    lo = lax.bitcast_convert_type(val[:, half:].astype(BF16).astype(F32), U32)
    w = (hi & jnp.uint32(0xFFFF0000)) | (lo >> 16)
    for j in range(SUB):
        ref[pl.ds(j, m, stride=SUB), :] = w[:, j * LANES:(j + 1) * LANES]


def _tiles_to_rows(ref, m, base=0):
    w = jnp.concatenate([ref[pl.ds(base * SUB + j, m, stride=SUB), :] for j in range(SUB)], axis=1)
    hi = lax.bitcast_convert_type(w & jnp.uint32(0xFFFF0000), F32)
    lo = lax.bitcast_convert_type(w << 16, F32)
    return jnp.concatenate([hi, lo], axis=1)


def _cparams(sem):
    return pltpu.CompilerParams(dimension_semantics=sem, vmem_limit_bytes=VMEM_LIMIT)


def _ada_kernel(c_ref, w_ref, b_ref, o_ref):
    c = c_ref[...]
    o_ref[...] = _dot(_silu(c), w_ref[...], precision=HIGHEST) + b_ref[...]


def _ada_mod(c, w, b):
    bsz, d = c.shape
    n = w.shape[1]
    rows = 8
    cp = jnp.zeros((rows, d), F32).at[:bsz].set(c)
    tn = 1536
    out = pl.pallas_call(
        _ada_kernel,
        out_shape=jax.ShapeDtypeStruct((rows, n), F32),
        grid=(n // tn,),
        in_specs=[pl.BlockSpec((rows, d), lambda j: (0, 0)),
                  pl.BlockSpec((d, tn), lambda j: (0, j)),
                  pl.BlockSpec((1, tn), lambda j: (0, j))],
        out_specs=pl.BlockSpec((rows, tn), lambda j: (0, j)),
        compiler_params=_cparams(("arbitrary",)),
        name="ada_mod",
    )(cp, w, b.reshape(1, n))
    return out[:bsz]


def _in_proj_kernel(x_ref, sh_ref, sc_ref, g_ref, whg_ref, wrw_ref, wgt_ref, hg_ref, rw_ref, gt_ref):
    x = x_ref[0]
    ms = jnp.mean(x * x, axis=-1, keepdims=True)
    h = (x * lax.rsqrt(ms + NORM_EPS) * g_ref[...]) * (1.0 + sc_ref[0]) + sh_ref[0]
    hb = h.astype(BF16)
    step = 512
    for n0 in range(0, whg_ref.shape[1], step):
        hg_ref[0, :, n0:n0 + step] = _dot(hb, whg_ref[:, n0:n0 + step])
    for n0 in range(0, wrw_ref.shape[1], 256):
        rw_ref[0, :, n0:n0 + 256] = _dot(hb, wrw_ref[:, n0:n0 + 256])
    for n0 in range(0, wgt_ref.shape[1], step):
        gt_ref[0, :, n0:n0 + step] = _sigmoid(_dot(hb, wgt_ref[:, n0:n0 + step])).astype(BF16)


def _in_proj(x, mod3, norm_g, w_hg, w_rw, w_gt, tm):
    bsz, s, d = x.shape
    n_hg, n_rw, n_gt = w_hg.shape[1], w_rw.shape[1], w_gt.shape[1]
    const = lambda b, i: (0, 0)
    return pl.pallas_call(
        _in_proj_kernel,
        out_shape=(jax.ShapeDtypeStruct((bsz, s, n_hg), F32),
                   jax.ShapeDtypeStruct((bsz, s, n_rw), F32),
                   jax.ShapeDtypeStruct((bsz, s, n_gt), BF16)),
        grid=(bsz, s // tm),
        in_specs=[pl.BlockSpec((1, tm, d), lambda b, i: (b, i, 0)),
                  pl.BlockSpec((1, 1, d), lambda b, i: (b * 6 + 0, 0, 0)),
                  pl.BlockSpec((1, 1, d), lambda b, i: (b * 6 + 1, 0, 0)),
                  pl.BlockSpec((1, d), const),
                  pl.BlockSpec((d, n_hg), const),
                  pl.BlockSpec((d, n_rw), const),
                  pl.BlockSpec((d, n_gt), const)],
        out_specs=(pl.BlockSpec((1, tm, n_hg), lambda b, i: (b, i, 0)),
                   pl.BlockSpec((1, tm, n_rw), lambda b, i: (b, i, 0)),
                   pl.BlockSpec((1, tm, n_gt), lambda b, i: (b, i, 0))),
        compiler_params=_cparams(("arbitrary", "arbitrary")),
        name="in_proj",
    )(x, mod3, mod3, norm_g.reshape(1, d), w_hg, w_rw, w_gt)


_HG_LEVELS = (32, 16, 8, 4, 2, 1)


def _hgrn2_consts(width):
    c = CHUNK
    t = np.arange(c)[:, None]
    s = np.arange(c)[None, :]
    blocks = [(s <= t), (s > t)]
    lvl_masks = []
    right = []
    for h in _HG_LEVELS:
        m = (t // (2 * h)) * 2 * h + h
        is_r = (t & h) != 0
        blk = np.where(is_r, (s >= m) & (s <= t), (s > t) & (s <= m - 1))
        blocks.append(blk)
        lvl_masks.append(is_r & ((s & h) == 0) & ((t // (2 * h)) == (s // (2 * h))))
        right.append(np.broadcast_to(is_r, (c, width)))
    mst = np.tile(np.concatenate(blocks, axis=0).astype(np.float32), (1, 3))
    lm = np.stack([np.eye(c, dtype=bool)] + lvl_masks).astype(np.float32)
    rm = np.stack(right).astype(np.float32)
    return jnp.asarray(mst, BF16), jnp.asarray(lm, F32), jnp.asarray(rm, F32)


def _hgrn2_kernel(q_ref, f_ref, i_ref, g_ref, lb_ref, ng_ref, mst_ref, lm_ref, rm_ref, o_ref, st_ref):
    c = CHUNK
    n_chunks = q_ref.shape[1] // c

    @pl.when(pl.program_id(1) == 0)
    def _():
        st_ref[...] = jnp.zeros_like(st_ref)

    nsub = 2 if n_chunks % 2 == 0 else 1

    def chunk_body(ci, carry):
        r0 = pl.multiple_of(ci * (nsub * c), nsub * c)
        for _ in _hgrn2_steps(q_ref, f_ref, i_ref, g_ref, lb_ref, ng_ref, mst_ref, lm_ref, rm_ref, o_ref, st_ref,
                              0, r0, nsub):
            pass
        return carry

    lax.fori_loop(0, n_chunks // nsub, chunk_body, 0)


def _hgrn2_steps(q_ref, f_ref, i_ref, g_ref, lb_ref, ng_ref, mst_ref, lm_ref, rm_ref, o_ref, st_ref, b, r0, nsub):
    c = CHUNK
    n_heads = q_ref.shape[2] // HG_HEAD
    mst = mst_ref[...]
    lb = lb_ref[...]
    ng = ng_ref[...]
    heads = [slice(hd * HG_HEAD, (hd + 1) * HG_HEAD) for hd in range(n_heads)]
    subs = []
    for j in range(nsub):
        rows = pl.ds(r0 + j * c, c)
        q = _silu(q_ref[b, rows, :])
        f = lb + (1.0 - lb) * _sigmoid(f_ref[b, rows, :])
        k = 1.0 - f
        ex = jnp.exp(_dot_exact_lhs(mst, jnp.log(f)))
        subs.append(dict(rows=rows, q=q, k=k, ex=ex, vb=i_ref[b, rows, :].astype(BF16),
                         qd=(q * ex[0:c]).astype(BF16), kr=(k * ex[c:2 * c]).astype(BF16)))
        yield
    for sb in subs:
        qb, kb = sb['q'].astype(BF16), sb['k'].astype(BF16)
        sb['sc'] = [lm_ref[0] * _dot(qb[:, ls], kb[:, ls], NT) for ls in heads]
        sb['dqk'] = sb['q'] - sb['k']
    yield
    for li in range(len(_HG_LEVELS)):
        for sb in subs:
            g_l = ((sb['k'] + rm_ref[li] * sb['dqk']) * sb['ex'][(2 + li) * c:(3 + li) * c]).astype(BF16)
            sb['sc'] = [s_h + lm_ref[li + 1] * _dot(g_l[:, ls], g_l[:, ls], NT)
                        for s_h, ls in zip(sb['sc'], heads)]
        yield
    for sb in subs:
        sb['kv'] = [_dot(sb['vb'][:, ls], sb['kr'][:, ls], TN) for ls in heads]
        sb['o'] = [_dot(s_h.astype(BF16), sb['vb'][:, ls]) for s_h, ls in zip(sb['sc'], heads)]
        yield
    sts = [st_ref[b, hd] for hd in range(n_heads)]
    for sb in subs:
        sb['o'] = [o_h + _dot(sb['qd'][:, ls], st.astype(BF16), NT) for o_h, ls, st in zip(sb['o'], heads, sts)]
        sts = [st * sb['ex'][c - 1:c, ls] + kv for st, ls, kv in zip(sts, heads, sb['kv'])]
        yield
    for hd in range(n_heads):
        st_ref[b, hd] = sts[hd]
    for sb in subs:
        on = [o_h * lax.rsqrt(jnp.mean(o_h * o_h, axis=-1, keepdims=True) + NORM_EPS) for o_h in sb['o']]
        o_full = jnp.concatenate(on, axis=1) * ng
        o_ref[b, sb['rows'], :] = (o_full * _silu(g_ref[b, sb['rows'], :])).astype(o_ref.dtype)
        yield


def _hgrn2(hg, lb, norm_g, ts):
    bsz, s, n4 = hg.shape
    w = n4 // 4
    mst, lm, rm = _hgrn2_consts(w)
    n_heads = w // HG_HEAD
    const2 = lambda b, i: (0, 0)
    const3 = lambda b, i: (0, 0, 0)
    return pl.pallas_call(
        _hgrn2_kernel,
        out_shape=jax.ShapeDtypeStruct((bsz, s, w), BF16),
        grid=(bsz, s // ts),
        in_specs=[pl.BlockSpec((1, ts, w), lambda b, i: (b, i, 0)),
                  pl.BlockSpec((1, ts, w), lambda b, i: (b, i, 1)),
                  pl.BlockSpec((1, ts, w), lambda b, i: (b, i, 2)),
                  pl.BlockSpec((1, ts, w), lambda b, i: (b, i, 3)),
                  pl.BlockSpec((1, w), const2),
                  pl.BlockSpec((1, w), const2),
                  pl.BlockSpec(mst.shape, const2),
                  pl.BlockSpec(lm.shape, const3),
                  pl.BlockSpec(rm.shape, const3)],
        out_specs=pl.BlockSpec((1, ts, w), lambda b, i: (b, i, 0)),
        scratch_shapes=[pltpu.VMEM((1, n_heads, HG_HEAD, HG_HEAD), F32)],
        compiler_params=_cparams(("arbitrary", "arbitrary")),
        name="hgrn2",
    )(hg, hg, hg, hg, lb.reshape(1, w), norm_g.reshape(1, w), mst, lm, rm)


def _rwkv_consts(width):
    c = CHUNK
    t = np.arange(c)[:, None]
    s = np.arange(c)[None, :]
    tri = np.tile((s <= t).astype(np.float32), (1, 3))
    tt = np.arange(2 * c)[:, None]
    ss = np.arange(2 * c)[None, :]
    same = (tt // c) == (ss // c)
    strict = same & ((ss % c) < (tt % c))
    incl = same & ((ss % c) <= (tt % c))
    hsum = (np.arange(MXU_K)[:, None] // RW_HEAD) == (np.arange(MXU_K)[None, :] // RW_HEAD)
    hsum = np.tile(hsum, (2, 1))
    return (jnp.asarray(tri, BF16), jnp.asarray(strict.astype(np.float32), F32),
            jnp.asarray(incl.astype(np.float32), F32), jnp.asarray(hsum.astype(np.float32), BF16))


def _rwkv7_kernel(p_ref, mu_ref, w0_ref, a0_ref, kk_ref, ka_ref, rk_ref, gnw_ref, gnb_ref,
                  w2_ref, a2_ref, g2_ref, tri_ref, sm_ref, im_ref, hs_ref,
                  o_ref, carry_ref, zt_ref):
    @pl.when(pl.program_id(0) == 0)
    def _():
        carry_ref[...] = jnp.zeros_like(carry_ref)
        zt_ref[...] = jnp.zeros_like(zt_ref)

    for _ in _rwkv7_steps(p_ref, mu_ref, w0_ref, a0_ref, kk_ref, ka_ref, rk_ref, gnw_ref, gnb_ref,
                          w2_ref, a2_ref, g2_ref, tri_ref, sm_ref, im_ref, hs_ref, o_ref, carry_ref, zt_ref):
        pass


def _rwkv7_steps(p_ref, mu_ref, w0_ref, a0_ref, kk_ref, ka_ref, rk_ref, gnw_ref, gnb_ref,
                 w2_ref, a2_ref, g2_ref, tri_ref, sm_ref, im_ref, hs_ref, o_ref, carry_ref, zt_ref):
    c = CHUNK
    nb = p_ref.shape[0]
    nch = p_ref.shape[1] // c
    width = o_ref.shape[2]
    n_pairs = width // LANES

    hs = hs_ref[...]
    tri = tri_ref[...]
    smask = sm_ref[...] > 0
    imask = im_ref[...] > 0
    lane = lax.broadcasted_iota(jnp.int32, (c, LANES), 1)
    m0 = (lane < RW_HEAD).astype(F32)
    m1 = 1.0 - m0

    def stack(x):
        return jnp.concatenate([x * m0, x * m1], axis=0)

    xs_rows = []
    for b in range(nb):
        p = p_ref[b]
        row = lax.broadcasted_iota(jnp.int32, p.shape, 0)
        prev = jnp.where(row == 0, carry_ref[b], pltpu.roll(p, 1, 0))
        carry_ref[b] = p[nch * c - 1:nch * c, :]
        xs_rows.append(p + mu_ref[...] * (prev - p))
    xs = jnp.concatenate(xs_rows, axis=0)
    r_all = xs[:, 0:width]
    k_all = xs[:, width:2 * width]
    v_all = xs[:, 2 * width:3 * width]
    slab = xs[:, 3 * width:]
    nz = -(w0_ref[...] + _dot(jnp.tanh(slab).astype(BF16), w2_ref[...]))
    softplus = jnp.maximum(nz, 0.0) + jnp.log(1.0 + jnp.exp(-jnp.abs(nz)))
    ld_all = -jnp.exp(-softplus - 0.5)
    a_all = _sigmoid(a0_ref[...] + _dot(slab.astype(BF16), a2_ref[...]))
    g_all = _dot(_sigmoid(slab).astype(BF16), g2_ref[...])
    kk0 = k_all * kk_ref[...]
    kk_all = kk0 * lax.rsqrt(jnp.maximum(_head_sums(kk0 * kk0, hs), 1e-24))
    k2_all = k_all * (1.0 + (a_all - 1.0) * ka_ref[...])
    yield

    units = []
    for b, j in [(b, j) for b in range(nb) for j in range(nch)]:
        rb = slice((b * nch + j) * c, (b * nch + j + 1) * c)
        r, k2, v, ld = r_all[rb], k2_all[rb], v_all[rb], ld_all[rb]
        a_in = -kk_all[rb]
        b_in = kk_all[rb] * a_all[rb]
        cum = _dot_exact_lhs(tri, ld)
        cum_t = cum[c - 1:c, :]
        e_c = jnp.exp(cum)
        e_nc = jnp.exp(-cum)
        e_rem = jnp.exp(cum_t - cum)
        at_f = a_in * jnp.exp(cum - ld)
        rt_f = r * e_c
        kt_f = k2 * e_nc
        bt_f = b_in * e_nc
        kh_f = k2 * e_rem
        bh_f = b_in * e_rem
        p_t = jnp.exp(cum_t)
        for pi in range(n_pairs):
            ls = slice(pi * LANES, (pi + 1) * LANES)
            units.append(dict(
                b=b, j=j, pi=pi,
                at=stack(at_f[:, ls]).astype(BF16), rt=stack(rt_f[:, ls]).astype(BF16),
                kt=stack(kt_f[:, ls]).astype(BF16), bt=stack(bt_f[:, ls]).astype(BF16),
                kh=stack(kh_f[:, ls]).astype(BF16), bh=stack(bh_f[:, ls]).astype(BF16),
                vs=stack(v[:, ls]).astype(BF16), p_t=p_t[:, ls]))
        yield

    for u in units:
        lhs = jnp.concatenate([u['at'], u['rt']], axis=0)
        u['g'] = _dot(lhs, jnp.concatenate([u['kt'], u['bt']], axis=0), NT)
    yield
    for u in units:
        g = u.pop('g')
        u['a_ak'] = jnp.where(smask, g[:2 * c, :2 * c], 0.0).astype(BF16)
        u['pw'] = jnp.where(smask, g[:2 * c, 2 * c:], 0.0).astype(BF16)
        u['a_r'] = jnp.where(jnp.concatenate([imask, imask], axis=1), g[2 * c:], 0.0).astype(BF16)
    for u in units:
        akv = _dot(u.pop('a_ak'), u['vs'])
        u['x'] = jnp.concatenate([u['at'].astype(F32), akv], axis=1)
    yield
    n_lvl = int(np.log2(c))
    for lvl in range(n_lvl):
        for u in units:
            u['x'] = u['x'] + _dot(u['pw'], u['x'].astype(BF16))
        yield
        if lvl + 1 < n_lvl:
            for u in units:
                u['pw'] = _dot(u['pw'], u['pw']).astype(BF16)
            yield
    for u in units:
        x = u.pop('x')
        u['wr'] = jnp.concatenate([x[:, :LANES].astype(BF16), u['rt']], axis=0)
        u['u_loc'] = x[:, LANES:]
    zt = {(b, pi): zt_ref[b, pi] for b in range(nb) for pi in range(n_pairs)}
    for j in range(nch):
        tail = [u for u in units if u['j'] == j]
        for u in tail:
            u['uy'] = _dot(u.pop('wr'), zt[u['b'], u['pi']].astype(BF16), NT)
        yield
        for u in tail:
            uy = u.pop('uy')
            u['u'] = (uy[:2 * c] + u.pop('u_loc')).astype(BF16)
            u['y0'] = uy[2 * c:]
        for u in tail:
            vu = jnp.concatenate([u['vs'], u['u']], axis=0)
            u['y'] = u.pop('y0') + _dot(u['a_r'], vu)
            upd = _dot(vu, jnp.concatenate([u['kh'], u['bh']], axis=0), TN)
            zt[u['b'], u['pi']] = zt[u['b'], u['pi']] * u['p_t'] + upd
        yield
    for (b, pi), z in zt.items():
        zt_ref[b, pi] = z

    inv_n = 1.0 / RW_HEAD
    y = jnp.concatenate(
        [jnp.concatenate([u['y'][:c] + u['y'][c:] for u in units if (u['b'], u['j']) == (b, j)], axis=1)
         for b in range(nb) for j in range(nch)], axis=0)
    mean = _head_sums(y, hs) * inv_n
    yield
    d = y - mean
    var = _head_sums(d * d, hs) * inv_n
    yield
    yn = d * lax.rsqrt(var + RW_GN_EPS) * gnw_ref[...] + gnb_ref[...]
    bonus = _head_sums(r_all * k2_all * rk_ref[...], hs) * v_all
    out = ((yn + bonus) * g_all).astype(o_ref.dtype)
    for b in range(nb):
        o_ref[b] = out[b * nch * c:(b + 1) * nch * c]


def _rwkv7(rw, mu, w0, w2, a0, a2, g2, k_k, k_a, r_k, gn_w, gn_b, ts):
    bsz, s, cols = rw.shape
    width = w0.shape[-1]
    n_pairs = width // LANES
    slab = cols - 3 * width
    dl, al, gl = w2.shape[0], a2.shape[0], g2.shape[0]
    w2f = jnp.zeros((slab, width), F32).at[0:dl].set(w2).astype(BF16)
    a2f = jnp.zeros((slab, width), F32).at[dl:dl + al].set(a2).astype(BF16)
    g2f = jnp.zeros((slab, width), F32).at[dl + al:dl + al + gl].set(g2).astype(BF16)
    mup = jnp.zeros((1, cols), F32).at[0, :mu.shape[-1]].set(mu)
    tri, sm, im, hs = _rwkv_consts(width)
    row = lambda x: x.reshape(1, width)
    const = lambda i: (0, 0)
    vec = pl.BlockSpec((1, width), const)
    return pl.pallas_call(
        _rwkv7_kernel,
        out_shape=jax.ShapeDtypeStruct((bsz, s, width), BF16),
        grid=(s // ts,),
        in_specs=[pl.BlockSpec((bsz, ts, cols), lambda i: (0, i, 0)),
                  pl.BlockSpec((1, cols), const),
                  vec, vec, vec, vec, vec, vec, vec,
                  pl.BlockSpec((slab, width), const),
                  pl.BlockSpec((slab, width), const),
                  pl.BlockSpec((slab, width), const),
                  pl.BlockSpec(tri.shape, const),
                  pl.BlockSpec(sm.shape, const),
                  pl.BlockSpec(im.shape, const),
                  pl.BlockSpec(hs.shape, const)],
        out_specs=pl.BlockSpec((bsz, ts, width), lambda i: (0, i, 0)),
        scratch_shapes=[pltpu.VMEM((bsz, 1, cols), F32),
                        pltpu.VMEM((bsz, n_pairs, LANES, LANES), F32)],
        compiler_params=_cparams(("arbitrary",)),
        name="rwkv7",
    )(rw, mup, row(w0), row(a0), row(k_k), row(k_a), row(r_k), row(gn_w), row(gn_b),
      w2f, a2f, g2f, tri, sm, im, hs)


def _out_proj_kernel(n_groups, n_experts,
                     x_ref, oa_ref, ob_ref, ga_ref, gb_ref, gt1_ref, sc2_ref, sh2_ref, g2_ref,
                     wa_ref, wb_ref, wo_ref, wr_ref, wrl_ref, br_ref, upper_ref,
                     x1_ref, h2_ref, routet_ref, cnt_ref, carry_ref):
    first = (pl.program_id(0) == 0) & (pl.program_id(1) == 0)

    @pl.when(first)
    def _():
        carry_ref[...] = jnp.zeros_like(carry_ref)

    pa = _dot(oa_ref[0], wa_ref[...])
    pb = _dot(ob_ref[0], wb_ref[...])
    mixed = ga_ref[0].astype(F32) * pa + gb_ref[0].astype(F32) * pb
    x1 = x_ref[0] + gt1_ref[0] * _dot(mixed.astype(BF16), wo_ref[...])
    x1_ref[0] = x1
    ms = jnp.mean(x1 * x1, axis=-1, keepdims=True)
    h2 = (x1 * lax.rsqrt(ms + NORM_EPS) * g2_ref[...]) * (1.0 + sc2_ref[0]) + sh2_ref[0]
    _rows_to_tiles(h2_ref, h2)

    h2_hi = h2.astype(BF16)
    h2_lo = (h2 - h2_hi.astype(F32)).astype(BF16)
    logits = (_dot(wr_ref[...], h2_hi, NT) + _dot(wr_ref[...], h2_lo, NT) + _dot(wrl_ref[...], h2_hi, NT)
              + br_ref[...])
    row = lax.broadcasted_iota(jnp.int32, logits.shape, 0)
    neg = jnp.float32(-jnp.inf)
    big = jnp.int32(1 << 20)
    eg = n_experts // n_groups
    is_g = (row >= n_experts) & (row < n_experts + n_groups)
    lg = jnp.where(is_g, logits, neg)
    mg = jnp.max(lg, axis=0, keepdims=True)
    p_grp = 1.0 / jnp.sum(jnp.where(is_g, jnp.exp(lg - mg), 0.0), axis=0, keepdims=True)
    gidx = jnp.min(jnp.where(lg == mg, row, big), axis=0, keepdims=True) - n_experts
    sel = (row >= gidx * eg) & (row < gidx * eg + eg)
    le = jnp.where(sel, logits, neg)
    me = jnp.max(le, axis=0, keepdims=True)
    pe_un = jnp.where(sel, jnp.exp(le - me), 0.0)
    pe = jnp.where(sel, pe_un / jnp.sum(pe_un, axis=0, keepdims=True), -1.0)
    v1 = jnp.max(pe, axis=0, keepdims=True)
    i1 = jnp.min(jnp.where(pe == v1, row, big), axis=0, keepdims=True)
    pe2 = jnp.where(row == i1, -1.0, pe)
    v2 = jnp.max(pe2, axis=0, keepdims=True)
    i2 = jnp.min(jnp.where(pe2 == v2, row, big), axis=0, keepdims=True)
    wsum = v1 + v2
    w1 = p_grp * v1 / wsum
    w2 = p_grp * v2 / wsum

    oh1 = (row == i1).astype(F32)
    oh2 = (row == i2).astype(F32)
    both = oh1 + oh2
    before = _dot(both.astype(BF16), upper_ref[...]) + carry_ref[...]
    rank1 = jnp.sum(oh1 * before, axis=0, keepdims=True)
    rank2 = jnp.sum(oh2 * before, axis=0, keepdims=True)
    carry_ref[...] = carry_ref[...] + jnp.sum(both, axis=1, keepdims=True)
    cnt_ref[...] = carry_ref[...]
    zero = jnp.zeros_like(w1)
    routet_ref[...] = jnp.concatenate(
        [w1, w2, i1.astype(F32), i2.astype(F32), rank1, rank2, zero, zero], axis=0)


def _out_proj(x, o_a, o_b, gates, mod3, norm2_g, wa, wb, wo, wr, br, n_groups, n_experts, tm):
    bsz, s, d = x.shape
    wdt = o_a.shape[-1]
    upper = jnp.asarray(np.triu(np.ones((tm, tm), np.float32), 1), BF16)
    wrt = wr.T
    wr_hi = wrt.astype(BF16)
    wr_lo = (wrt - wr_hi.astype(F32)).astype(BF16)
    const = lambda b, i: (0, 0)
    tile = lambda b, i: (b, i, 0)
    kern = functools.partial(_out_proj_kernel, n_groups, n_experts)
    return pl.pallas_call(
        kern,
        out_shape=(jax.ShapeDtypeStruct((bsz, s, d), F32),
                   jax.ShapeDtypeStruct((bsz * s * SUB, LANES), U32),
                   jax.ShapeDtypeStruct((bsz * (s // tm) * SUBLANES, tm), F32),
                   jax.ShapeDtypeStruct((LANES, 1), F32)),
        grid=(bsz, s // tm),
        in_specs=[pl.BlockSpec((1, tm, d), tile),
                  pl.BlockSpec((1, tm, wdt), tile),
                  pl.BlockSpec((1, tm, wdt), tile),
                  pl.BlockSpec((1, tm, d), lambda b, i: (b, i, 0)),
                  pl.BlockSpec((1, tm, d), lambda b, i: (b, i, 1)),
                  pl.BlockSpec((1, 1, d), lambda b, i: (b * 6 + 2, 0, 0)),
                  pl.BlockSpec((1, 1, d), lambda b, i: (b * 6 + 4, 0, 0)),
                  pl.BlockSpec((1, 1, d), lambda b, i: (b * 6 + 3, 0, 0)),
                  pl.BlockSpec((1, d), const),
                  pl.BlockSpec(wa.shape, const),
                  pl.BlockSpec(wb.shape, const),
                  pl.BlockSpec(wo.shape, const),
                  pl.BlockSpec(wrt.shape, const),
                  pl.BlockSpec(wrt.shape, const),
                  pl.BlockSpec((LANES, 1), const),
                  pl.BlockSpec((tm, tm), const)],
        out_specs=(pl.BlockSpec((1, tm, d), tile),
                   pl.BlockSpec((tm * SUB, LANES), lambda b, i: (b * (s // tm) + i, 0)),
                   pl.BlockSpec((SUBLANES, tm), lambda b, i: (b * (s // tm) + i, 0)),
                   pl.BlockSpec((LANES, 1), const)),
        scratch_shapes=[pltpu.VMEM((LANES, 1), F32)],
        compiler_params=_cparams(("arbitrary", "arbitrary")),
        name="out_proj",
    )(x, o_a, o_b, gates, gates, mod3, mod3, mod3, norm2_g.reshape(1, d), wa, wb, wo, wr_hi, wr_lo,
      br.reshape(LANES, 1), upper)


def _moe_slotmap_kernel(tm_o, n_tok, blk, dest_ref, zstart_ref, zcnt_ref, nused_ref, smap_ref):
    i = pl.program_id(0)
    n_slots = smap_ref.shape[0] - blk
    n_experts = zcnt_ref.shape[0]

    @pl.when(i == 0)
    def _():
        unroll = 8

        def init(g, carry):
            for u in range(unroll):
                smap_ref[blk + g * unroll + u] = TOP_K * n_tok
            return carry
        lax.fori_loop(nused_ref[0] * (blk // unroll), n_slots // unroll, init, 0)
        for r in range(blk):
            smap_ref[r] = TOP_K * n_tok + n_experts * blk + r
        for e in range(n_experts):
            def pad(g, carry, e=e):
                for u in range(unroll):
                    j = jnp.maximum(zcnt_ref[e] - 1 - (g * unroll + u), 0)
                    smap_ref[blk + zstart_ref[e] + j] = TOP_K * n_tok + e * blk + j
                return carry
            lax.fori_loop(0, (zcnt_ref[e] + unroll - 1) // unroll, pad, 0)

    base = i * (TOP_K * tm_o)
    both = 1 + (1 << SMAP_BITS)
    for k in range(TOP_K):
        v0 = k * n_tok + (i * tm_o) * both
        for r in range(tm_o):
            smap_ref[dest_ref[base + k * tm_o + r]] = v0 + r * both


def _moe_slotmap(dest, zstart, zcnt, n_used, n_slots, blk, tm_o, n_tok):
    grid_spec = pltpu.PrefetchScalarGridSpec(
        num_scalar_prefetch=4,
        grid=(n_tok // tm_o,),
        in_specs=[],
        out_specs=pl.BlockSpec(memory_space=pltpu.SMEM),
    )
    n_experts = zcnt.shape[0]
    assert TOP_K * n_tok + (n_experts + 1) * blk <= (1 << SMAP_BITS) and n_tok <= (1 << (31 - SMAP_BITS))
    return pl.pallas_call(
        functools.partial(_moe_slotmap_kernel, tm_o, n_tok, blk),
        out_shape=jax.ShapeDtypeStruct((blk + n_slots,), jnp.int32),
        grid_spec=grid_spec,
        compiler_params=_cparams(("arbitrary",)),
        name="moe_slotmap",
    )(dest, zstart, zcnt, n_used)


def _moe_expert_gather_kernel(n_tok, blk_e_ref, nused_ref, smap_ref, h_ref, wg_ref, wu_ref, wd_ref, y_ref,
                              hv, xbuf, ystage, wgb, wub, wdb, sem, hsem):
    i = pl.program_id(0)
    nb = pl.num_programs(0)
    n_used = nused_ref[0]
    blk = xbuf.shape[1] // SUB
    first_real_blocks = TOP_K * n_tok // blk

    def gather(b, slot, rows):
        for r in rows:
            tok = lax.shift_right_logical(smap_ref[(b + 1) * blk + r], SMAP_BITS)
            src = pl.multiple_of(tok * SUB, SUB)
            xbuf[slot, pl.ds(r * SUB, SUB), :] = hv[pl.ds(src, SUB), :]

    def issue(b, slot, rows):
        for r in rows:
            t = smap_ref[(b + 1) * blk + r] & ((1 << SMAP_BITS) - 1)
            dst = pl.multiple_of(t * SUB, SUB)
            pltpu.make_async_copy(ystage.at[slot, pl.ds(r * SUB, SUB), :], y_ref.at[pl.ds(dst, SUB), :],
                                  sem.at[slot]).start(priority=r % 2)

    def wait_block(slot):
        pltpu.make_async_copy(ystage.at[slot], y_ref.at[pl.ds(0, blk * SUB), :], sem.at[slot]).wait()

    @pl.when(i == 0)
    def _():
        load = pltpu.make_async_copy(h_ref, hv, hsem)
        load.start()
        ystage[...] = jnp.zeros_like(ystage)
        n_spare_blocks = y_ref.shape[0] // (blk * SUB) - first_real_blocks

        def spare_copy(c):
            dst = (first_real_blocks + c) * blk * SUB
            return pltpu.make_async_copy(ystage.at[0], y_ref.at[pl.ds(dst, blk * SUB), :], sem.at[0])

        for c in range(n_spare_blocks):
            spare_copy(c).start()
        for c in range(n_spare_blocks):
            spare_copy(c).wait()
        load.wait()
        gather(0, 0, range(blk))

    new_expert = (i == 0) | (blk_e_ref[i] != blk_e_ref[jnp.maximum(i - 1, 0)])

    @pl.when((i < n_used) & new_expert)
    def _():
        wgb[...] = wg_ref[0].astype(BF16)
        wub[...] = wu_ref[0].astype(BF16)
        wdb[...] = wd_ref[0].astype(BF16)

    @pl.when(i < n_used)
    def _():
        slot = i % 2
        pslot = 1 - slot
        nxt = jnp.minimum(i + 1, nb - 1)

        @pl.when(i > 0)
        def _():
            wait_block(slot)

        q = blk // 4
        xb = _tiles_to_rows(xbuf.at[slot], blk).astype(BF16)
        issue(i - 1, pslot, range(0, q))
        gather(nxt, pslot, range(0, q))
        hg = _dot(xb, wgb[...])
        issue(i - 1, pslot, range(q, 2 * q))
        gather(nxt, pslot, range(q, 2 * q))
        hu = _dot(xb, wub[...])
        issue(i - 1, pslot, range(2 * q, 3 * q))
        gather(nxt, pslot, range(2 * q, 3 * q))
        hid = (_silu(hg) * hu).astype(BF16)
        y = _dot(hid, wdb[...])
        issue(i - 1, pslot, range(3 * q, blk))
        gather(nxt, pslot, range(3 * q, blk))
        _rows_to_tiles(ystage.at[slot], y)

        @pl.when(i == n_used - 1)
        def _():
            issue(i, slot, range(blk))
            wait_block(pslot)
            wait_block(slot)


def _moe_experts_gather(h2, smap, blk_e, n_used, w_gate, w_up, w_down, blk, n_tok):
    d = w_gate.shape[1]
    assert d == 2 * SUB * LANES and h2.shape == (n_tok * SUB, LANES)
    n_slots = smap.shape[0] - blk
    nb = n_slots // blk
    f = w_gate.shape[-1]
    n_experts = w_gate.shape[0]
    assert (TOP_K * n_tok) % blk == 0
    n_tiles = TOP_K * n_tok + (n_experts + 1) * blk
    grid_spec = pltpu.PrefetchScalarGridSpec(
        num_scalar_prefetch=3,
        grid=(nb,),
        in_specs=[pl.BlockSpec(memory_space=pl.ANY),
                  pl.BlockSpec((1, d, f), lambda i, be, nu, sm: (be[i], 0, 0)),
                  pl.BlockSpec((1, d, f), lambda i, be, nu, sm: (be[i], 0, 0)),
                  pl.BlockSpec((1, f, d), lambda i, be, nu, sm: (be[i], 0, 0))],
        out_specs=pl.BlockSpec(memory_space=pl.ANY),
        scratch_shapes=[pltpu.VMEM((n_tok * SUB, LANES), U32),
                        pltpu.VMEM((2, blk * SUB, LANES), U32),
                        pltpu.VMEM((2, blk * SUB, LANES), U32),
                        pltpu.VMEM((d, f), BF16),
                        pltpu.VMEM((d, f), BF16),
                        pltpu.VMEM((f, d), BF16),
                        pltpu.SemaphoreType.DMA((2,)),
                        pltpu.SemaphoreType.DMA],
    )
    return pl.pallas_call(
        functools.partial(_moe_expert_gather_kernel, n_tok),
        out_shape=jax.ShapeDtypeStruct((n_tiles * SUB, LANES), U32),
        grid_spec=grid_spec,
        compiler_params=_cparams(("arbitrary",)),
        name="moe_experts",
    )(blk_e, n_used, smap, h2, w_gate, w_up, w_down)


def _moe_combine_kernel(x1_ref, routet_ref, gt2_ref, fg_ref, y1_ref, y2_ref, o_ref):
    tm = x1_ref.shape[0]
    route = routet_ref[...].T
    moe = route[:, 0:1] * _tiles_to_rows(y1_ref, tm) + route[:, 1:2] * _tiles_to_rows(y2_ref, tm)
    xo = x1_ref[...] + gt2_ref[0] * moe
    ms = jnp.mean(xo * xo, axis=-1, keepdims=True)
    o_ref[...] = xo * lax.rsqrt(ms + NORM_EPS) * fg_ref[...]


def _moe_combine(x1, route_t, yt, mod3, final_g, s, tm):
    n, d = x1.shape
    tiles_per_batch = s // tm
    n_steps = n // tm
    per_o = route_t.shape[1] // tm
    return pl.pallas_call(
        _moe_combine_kernel,
        out_shape=jax.ShapeDtypeStruct((n, d), F32),
        grid=(n_steps,),
        in_specs=[pl.BlockSpec((tm, d), lambda i: (i, 0)),
                  pl.BlockSpec((SUBLANES, tm), lambda i: (i // per_o, i % per_o)),
                  pl.BlockSpec((1, 1, d), lambda i: ((i // tiles_per_batch) * 6 + 5, 0, 0)),
                  pl.BlockSpec((1, d), lambda i: (0, 0)),
                  pl.BlockSpec((tm * SUB, LANES), lambda i: (i, 0)),
                  pl.BlockSpec((tm * SUB, LANES), lambda i: (n_steps + i, 0))],
        out_specs=pl.BlockSpec((tm, d), lambda i: (i, 0)),
        compiler_params=_cparams(("arbitrary",)),
        name="moe_combine",
    )(x1, route_t, mod3, final_g.reshape(1, d), yt, yt)


def _pick(n, candidates):
    for t in candidates:
        if n % t == 0:
            return t
    raise ValueError(f"no tile in {candidates} divides {n}")


def kernel(x, c, ada_w, ada_b, norm1_g, w_in, hg_lb, hg_norm_g, rw_mu, rw_w0, rw_w2, rw_a0, rw_a2, rw_g2, rw_kk, rw_ka, rw_rk, rw_gn_w, rw_gn_b, w_proj_a, w_proj_b, w_out, norm2_g, router_g_w, router_g_b, router_e_w, router_e_b, exp_w_gate, exp_w_up, exp_w_down, final_g):
    bsz, s, d = x.shape
    depth = ada_w.shape[0]
    hg_f = hg_lb.shape[-1]
    hg_w = hg_norm_g.shape[-1]
    rw_w = rw_w0.shape[-1]
    rw_cols = rw_mu.shape[-1]
    n_groups = router_g_w.shape[-1]
    n_experts = router_e_w.shape[-1]
    assert hg_f == hg_w and s % CHUNK == 0 and n_experts + n_groups <= LANES and d == 2 * SUB * LANES

    lb_all = jnp.cumsum(jax.nn.softmax(hg_lb.astype(F32), axis=0), axis=0)
    n = bsz * s
    blk = 512
    n_blocks = (n * TOP_K + n_experts * blk) // blk
    for l in range(depth):
        mod = _ada_mod(c, ada_w[l], ada_b[l])
        mod3 = mod.reshape(bsz * 6, 1, d)

        hg_cols = 2 * hg_f + 2 * hg_w
        rw_pad = -(-rw_cols // 256) * 256
        wl = w_in[l]
        w_hg = wl[:, :hg_cols].astype(BF16)
        w_rw = jnp.zeros((d, rw_pad), BF16).at[:, :rw_cols].set(wl[:, hg_cols:hg_cols + rw_cols].astype(BF16))
        w_gt = wl[:, hg_cols + rw_cols:].astype(BF16)
        hg, rw, gates = _in_proj(x, mod3, norm1_g[l], w_hg, w_rw, w_gt, _pick(s, (512, 256, 128, 64)))

        o_a = _hgrn2(hg, lb_all[l], hg_norm_g[l], _pick(s, (1024, 512, 256, 128, 64)))
        o_b = _rwkv7(rw, rw_mu[l], rw_w0[l], rw_w2[l], rw_a0[l], rw_a2[l], rw_g2[l],
                     rw_kk[l], rw_ka[l], rw_rk[l].reshape(-1), rw_gn_w[l], rw_gn_b[l],
                     _pick(s, (2 * CHUNK, CHUNK)))

        wr = jnp.zeros((d, LANES), F32).at[:, :n_experts].set(router_e_w[l])
        wr = wr.at[:, n_experts:n_experts + n_groups].set(router_g_w[l])
        br = jnp.zeros((1, LANES), F32).at[0, :n_experts].set(router_e_b[l])
        br = br.at[0, n_experts:n_experts + n_groups].set(router_g_b[l])
        tm_o = _pick(s, (1024, 512, 256, 128, 64))
        x1, h2, route_t, counts = _out_proj(
            x, o_a, o_b, gates, mod3, norm2_g[l],
            w_proj_a[l].astype(BF16), w_proj_b[l].astype(BF16), w_out[l].astype(BF16),
            wr, br, n_groups, n_experts, tm_o)

        rt = route_t.reshape(n // tm_o, SUBLANES, tm_o)
        eid = rt[:, 2:4, :].astype(jnp.int32)
        rank = rt[:, 4:6, :].astype(jnp.int32)
        cnt = counts[:n_experts, 0].astype(jnp.int32)
        padded = (cnt + blk - 1) // blk * blk
        pad_end = jnp.cumsum(padded)
        pad_start = pad_end - padded
        e_ax = jnp.arange(n_experts, dtype=jnp.int32)[:, None, None, None]
        dest = rank + jnp.sum(jnp.where(eid[None] == e_ax, pad_start[:, None, None, None], 0), axis=0)
        blk_start = jnp.arange(n_blocks, dtype=jnp.int32) * blk
        blk_e = jnp.minimum(jnp.sum((pad_end[None, :] <= blk_start[:, None]).astype(jnp.int32), axis=1),
                            n_experts - 1)
        n_used = (pad_end[-1:] // blk).astype(jnp.int32)
        tm = tm_o
        smap = _moe_slotmap(dest.reshape(-1) + blk, pad_start + cnt, padded - cnt, n_used, n_blocks * blk, blk, tm_o, n)
        yt = _moe_experts_gather(h2, smap, blk_e, n_used, exp_w_gate[l], exp_w_up[l], exp_w_down[l], blk, n)
        last = l == depth - 1
        assert last, "the final RMSNorm is fused into the last layer's combine"
        out = _moe_combine(x1.reshape(n, d), route_t, yt, mod3, final_g, s, tm)
        x = out.reshape(bsz, s, d)
    return x
```

```python
import functools

import numpy as np
import jax
import jax.numpy as jnp
from jax import lax
from jax.experimental import pallas as pl
from jax.experimental.pallas import tpu as pltpu

F32 = jnp.float32
BF16 = jnp.bfloat16
HIGHEST = lax.Precision.HIGHEST

NORM_EPS = 1e-6
HG_HEAD = 128
RW_HEAD = 64
RW_GN_EPS = 64e-5
TOP_K = 2
CHUNK = 64
LANES = 128
SUB = 4
U32 = jnp.uint32
MXU_K = 256
SUBLANES = 8
SMAP_BITS = 16
VMEM_LIMIT = 56 * 1024 * 1024

NT = (((1,), (1,)), ((), ()))
TN = (((0,), (0,)), ((), ()))


def _dot(a, b, dims=None, precision=None):
    if dims is None:
        return jnp.dot(a, b, preferred_element_type=F32, precision=precision)
    return lax.dot_general(a, b, dims, preferred_element_type=F32, precision=precision)


def _split3(x):
    hi = x.astype(BF16)
    r1 = x - hi.astype(F32)
    mid = r1.astype(BF16)
    lo = (r1 - mid.astype(F32)).astype(BF16)
    return hi, mid, lo


def _dot_exact_lhs(m3_bf16, x):
    return _dot(m3_bf16, jnp.concatenate(_split3(x), axis=0))


def _head_sums(x, m2_bf16):
    outs = []
    for g0 in range(0, x.shape[1], MXU_K):
        xg = x[:, g0:g0 + MXU_K]
        hi = xg.astype(BF16)
        lo = (xg - hi.astype(F32)).astype(BF16)
        outs.append(_dot(jnp.concatenate([hi, lo], axis=1), m2_bf16))
    return jnp.concatenate(outs, axis=1)


def _sigmoid(x):
    return 1.0 / (1.0 + jnp.exp(-x))


def _silu(x):
    return x * _sigmoid(x)


def _rows_to_tiles(ref, val):
    m, half = val.shape[0], val.shape[1] // 2
    hi = lax.bitcast_convert_type(val[:, :half].astype(BF16).astype(F32), U32)
    lo = lax.bitcast_convert_type(val[:, half:].astype(BF16).astype(F32), U32)
    w = (hi & jnp.uint32(0xFFFF0000)) | (lo >> 16)
    for j in range(SUB):
        ref[pl.ds(j, m, stride=SUB), :] = w[:, j * LANES:(j + 1) * LANES]


def _tiles_to_rows(ref, m, base=0):
    w = jnp.concatenate([ref[pl.ds(base * SUB + j, m, stride=SUB), :] for j in range(SUB)], axis=1)
    hi = lax.bitcast_convert_type(w & jnp.uint32(0xFFFF0000), F32)
    lo = lax.bitcast_convert_type(w << 16, F32)
    return jnp.concatenate([hi, lo], axis=1)


def _cparams(sem):
    return pltpu.CompilerParams(dimension_semantics=sem, vmem_limit_bytes=VMEM_LIMIT)


def _ada_kernel(c_ref, w_ref, b_ref, o_ref):
    c = c_ref[...]
    o_ref[...] = _dot(_silu(c), w_ref[...], precision=HIGHEST) + b_ref[...]


def _ada_mod(c, w, b):
    bsz, d = c.shape
    n = w.shape[1]
    rows = 8
    cp = jnp.zeros((rows, d), F32).at[:bsz].set(c)
    tn = 1536
    out = pl.pallas_call(
        _ada_kernel,
        out_shape=jax.ShapeDtypeStruct((rows, n), F32),
        grid=(n // tn,),
        in_specs=[pl.BlockSpec((rows, d), lambda j: (0, 0)),
                  pl.BlockSpec((d, tn), lambda j: (0, j)),
                  pl.BlockSpec((1, tn), lambda j: (0, j))],
        out_specs=pl.BlockSpec((rows, tn), lambda j: (0, j)),
        compiler_params=_cparams(("arbitrary",)),
        name="ada_mod",
    )(cp, w, b.reshape(1, n))
    return out[:bsz]


def _in_proj_kernel(x_ref, sh_ref, sc_ref, g_ref, lb_ref, whg_ref, wrw_ref, wgt_ref, hg_ref, rw_ref, gt_ref):
    x = x_ref[0]
    ms = jnp.mean(x * x, axis=-1, keepdims=True)
    h = (x * lax.rsqrt(ms + NORM_EPS) * g_ref[...]) * (1.0 + sc_ref[0]) + sh_ref[0]
    hb = h.astype(BF16)
    lb = lb_ref[...]
    w = lb.shape[1]
    hg_maps = (_silu, lambda t: lb + (1.0 - lb) * _sigmoid(t), lambda t: t, _silu)
    for part, fn in enumerate(hg_maps):
        hg_ref[0, :, part * w:(part + 1) * w] = fn(_dot(hb, whg_ref[:, part * w:(part + 1) * w]))
    step = 512
    for n0 in range(0, wrw_ref.shape[1], 256):
        rw_ref[0, :, n0:n0 + 256] = _dot(hb, wrw_ref[:, n0:n0 + 256])
    for n0 in range(0, wgt_ref.shape[1], step):
        gt_ref[0, :, n0:n0 + step] = _sigmoid(_dot(hb, wgt_ref[:, n0:n0 + step])).astype(BF16)


def _in_proj(x, mod3, norm_g, lb, w_hg, w_rw, w_gt, tm):
    bsz, s, d = x.shape
    assert w_hg.shape[1] == 4 * lb.shape[0]
    n_hg, n_rw, n_gt = w_hg.shape[1], w_rw.shape[1], w_gt.shape[1]
    const = lambda b, i: (0, 0)
    return pl.pallas_call(
        _in_proj_kernel,
        out_shape=(jax.ShapeDtypeStruct((bsz, s, n_hg), F32),
                   jax.ShapeDtypeStruct((bsz, s, n_rw), F32),
                   jax.ShapeDtypeStruct((bsz, s, n_gt), BF16)),
        grid=(bsz, s // tm),
        in_specs=[pl.BlockSpec((1, tm, d), lambda b, i: (b, i, 0)),
                  pl.BlockSpec((1, 1, d), lambda b, i: (b * 6 + 0, 0, 0)),
                  pl.BlockSpec((1, 1, d), lambda b, i: (b * 6 + 1, 0, 0)),
                  pl.BlockSpec((1, d), const),
                  pl.BlockSpec((1, lb.shape[0]), const),
                  pl.BlockSpec((d, n_hg), const),
                  pl.BlockSpec((d, n_rw), const),
                  pl.BlockSpec((d, n_gt), const)],
        out_specs=(pl.BlockSpec((1, tm, n_hg), lambda b, i: (b, i, 0)),
                   pl.BlockSpec((1, tm, n_rw), lambda b, i: (b, i, 0)),
                   pl.BlockSpec((1, tm, n_gt), lambda b, i: (b, i, 0))),
        compiler_params=_cparams(("arbitrary", "arbitrary")),
        name="in_proj",
    )(x, mod3, mod3, norm_g.reshape(1, d), lb.reshape(1, -1), w_hg, w_rw, w_gt)


_HG_LEVELS = (32, 16, 8, 4, 2, 1)


def _hgrn2_consts(width):
    c = CHUNK
    t = np.arange(c)[:, None]
    s = np.arange(c)[None, :]
    blocks = [(s <= t), (s > t)]
    lvl_masks = []
    right = []
    for h in _HG_LEVELS:
        m = (t // (2 * h)) * 2 * h + h
        is_r = (t & h) != 0
        blk = np.where(is_r, (s >= m) & (s <= t), (s > t) & (s <= m - 1))
        blocks.append(blk)
        lvl_masks.append(is_r & ((s & h) == 0) & ((t // (2 * h)) == (s // (2 * h))))
        right.append(np.broadcast_to(is_r, (c, width)))
    mst = np.tile(np.concatenate(blocks, axis=0).astype(np.float32), (1, 3))
    lm = np.stack([np.eye(c, dtype=bool)] + lvl_masks).astype(np.float32)
    rm = np.stack(right).astype(np.float32)
    return jnp.asarray(mst, BF16), jnp.asarray(lm, F32), jnp.asarray(rm, F32)


def _hgrn2_kernel(q_ref, f_ref, i_ref, g_ref, ng_ref, mst_ref, lm_ref, rm_ref, o_ref, st_ref):
    c = CHUNK
    n_chunks = q_ref.shape[1] // c

    @pl.when(pl.program_id(1) == 0)
    def _():
        st_ref[...] = jnp.zeros_like(st_ref)

    nsub = 2 if n_chunks % 2 == 0 else 1

    def chunk_body(ci, carry):
        r0 = pl.multiple_of(ci * (nsub * c), nsub * c)
        for _ in _hgrn2_steps(q_ref, f_ref, i_ref, g_ref, ng_ref, mst_ref, lm_ref, rm_ref, o_ref, st_ref,
                              0, r0, nsub):
            pass
        return carry

    lax.fori_loop(0, n_chunks // nsub, chunk_body, 0)


def _hgrn2_steps(q_ref, f_ref, i_ref, g_ref, ng_ref, mst_ref, lm_ref, rm_ref, o_ref, st_ref, b, r0, nsub):
    c = CHUNK
    n_heads = q_ref.shape[2] // HG_HEAD
    mst = mst_ref[...]
    ng = ng_ref[...]
    heads = [slice(hd * HG_HEAD, (hd + 1) * HG_HEAD) for hd in range(n_heads)]
    subs = []
    for j in range(nsub):
        rows = pl.ds(r0 + j * c, c)
        q = q_ref[b, rows, :]
        f = f_ref[b, rows, :]
        k = 1.0 - f
        ex = jnp.exp2(_dot_exact_lhs(mst, jnp.log2(f)))
        subs.append(dict(rows=rows, q=q, k=k, ex=ex, vb=i_ref[b, rows, :].astype(BF16),
                         qd=(q * ex[0:c]).astype(BF16), kr=(k * ex[c:2 * c]).astype(BF16)))
        yield
    for sb in subs:
        qb, kb = sb['q'].astype(BF16), sb['k'].astype(BF16)
        sb['sc'] = [lm_ref[0] * _dot(qb[:, ls], kb[:, ls], NT) for ls in heads]
        sb['dqk'] = sb['q'] - sb['k']
    yield
    for li, h in enumerate(_HG_LEVELS):
        for sb in subs:
            if h % SUBLANES == 0:
                qk = jnp.concatenate([(sb['q'] if m % 2 else sb['k'])[m * h:(m + 1) * h] for m in range(c // h)],
                                     axis=0)
            else:
                qk = sb['k'] + rm_ref[li] * sb['dqk']
            g_l = (qk * sb['ex'][(2 + li) * c:(3 + li) * c]).astype(BF16)
            sb['sc'] = [s_h + lm_ref[li + 1] * _dot(g_l[:, ls], g_l[:, ls], NT)
                        for s_h, ls in zip(sb['sc'], heads)]
        yield
    for sb in subs:
        sb['kv'] = [_dot(sb['vb'][:, ls], sb['kr'][:, ls], TN) for ls in heads]
        sb['o'] = [_dot(s_h.astype(BF16), sb['vb'][:, ls]) for s_h, ls in zip(sb['sc'], heads)]
        yield
    sts = [st_ref[b, hd] for hd in range(n_heads)]
    for sb in subs:
        sb['o'] = [o_h + _dot(sb['qd'][:, ls], st.astype(BF16), NT) for o_h, ls, st in zip(sb['o'], heads, sts)]
        sts = [st * sb['ex'][c - 1:c, ls] + kv for st, ls, kv in zip(sts, heads, sb['kv'])]
        yield
    for hd in range(n_heads):
        st_ref[b, hd] = sts[hd]
    for sb in subs:
        on = [o_h * lax.rsqrt(jnp.mean(o_h * o_h, axis=-1, keepdims=True) + NORM_EPS) for o_h in sb['o']]
        o_full = jnp.concatenate(on, axis=1) * ng
        o_ref[b, sb['rows'], :] = (o_full * g_ref[b, sb['rows'], :]).astype(o_ref.dtype)
        yield


def _hgrn2(hg, norm_g, ts):
    bsz, s, n4 = hg.shape
    w = n4 // 4
    mst, lm, rm = _hgrn2_consts(w)
    n_heads = w // HG_HEAD
    const2 = lambda b, i: (0, 0)
    const3 = lambda b, i: (0, 0, 0)
    return pl.pallas_call(
        _hgrn2_kernel,
        out_shape=jax.ShapeDtypeStruct((bsz, s, w), BF16),
        grid=(bsz, s // ts),
        in_specs=[pl.BlockSpec((1, ts, w), lambda b, i: (b, i, 0)),
                  pl.BlockSpec((1, ts, w), lambda b, i: (b, i, 1)),
                  pl.BlockSpec((1, ts, w), lambda b, i: (b, i, 2)),
                  pl.BlockSpec((1, ts, w), lambda b, i: (b, i, 3)),
                  pl.BlockSpec((1, w), const2),
                  pl.BlockSpec(mst.shape, const2),
                  pl.BlockSpec(lm.shape, const3),
                  pl.BlockSpec(rm.shape, const3)],
        out_specs=pl.BlockSpec((1, ts, w), lambda b, i: (b, i, 0)),
        scratch_shapes=[pltpu.VMEM((1, n_heads, HG_HEAD, HG_HEAD), F32)],
        compiler_params=_cparams(("arbitrary", "arbitrary")),
        name="hgrn2",
    )(hg, hg, hg, hg, norm_g.reshape(1, w), mst, lm, rm)


def _rwkv_consts(width):
    c = CHUNK
    t = np.arange(c)[:, None]
    s = np.arange(c)[None, :]
    tri = np.tile((s <= t).astype(np.float32), (1, 3))
    tt = np.arange(2 * c)[:, None]
    ss = np.arange(2 * c)[None, :]
    same = (tt // c) == (ss // c)
    strict = same & ((ss % c) < (tt % c))
    incl = same & ((ss % c) <= (tt % c))
    hsum = (np.arange(MXU_K)[:, None] // RW_HEAD) == (np.arange(MXU_K)[None, :] // RW_HEAD)
    hsum = np.tile(hsum, (2, 1))
    return (jnp.asarray(tri, BF16), jnp.asarray(strict.astype(np.float32), F32),
            jnp.asarray(incl.astype(np.float32), F32), jnp.asarray(hsum.astype(np.float32), BF16))


def _rwkv7_kernel(p_ref, mu_ref, w0_ref, a0_ref, kk_ref, ka_ref, rk_ref, gnw_ref, gnb_ref,
                  w2_ref, a2_ref, g2_ref, tri_ref, sm_ref, im_ref, hs_ref,
                  o_ref, carry_ref, zt_ref):
    @pl.when(pl.program_id(0) == 0)
    def _():
        carry_ref[...] = jnp.zeros_like(carry_ref)
        zt_ref[...] = jnp.zeros_like(zt_ref)

    for _ in _rwkv7_steps(p_ref, mu_ref, w0_ref, a0_ref, kk_ref, ka_ref, rk_ref, gnw_ref, gnb_ref,
                          w2_ref, a2_ref, g2_ref, tri_ref, sm_ref, im_ref, hs_ref, o_ref, carry_ref, zt_ref):
        pass


def _rwkv7_steps(p_ref, mu_ref, w0_ref, a0_ref, kk_ref, ka_ref, rk_ref, gnw_ref, gnb_ref,
                 w2_ref, a2_ref, g2_ref, tri_ref, sm_ref, im_ref, hs_ref, o_ref, carry_ref, zt_ref):
    c = CHUNK
    nb = p_ref.shape[0]
    nch = p_ref.shape[1] // c
    width = o_ref.shape[2]
    n_pairs = width // LANES

    hs = hs_ref[...]
    tri = tri_ref[...]
    smask = sm_ref[...] > 0
    imask = im_ref[...] > 0
    lane = lax.broadcasted_iota(jnp.int32, (c, LANES), 1)
    m0 = (lane < RW_HEAD).astype(F32)
    m1 = 1.0 - m0

    def stack(x):
        return jnp.concatenate([x * m0, x * m1], axis=0)

    xs_rows = []
    for b in range(nb):
        p = p_ref[b]
        row = lax.broadcasted_iota(jnp.int32, p.shape, 0)
        prev = jnp.where(row == 0, carry_ref[b], pltpu.roll(p, 1, 0))
        carry_ref[b] = p[nch * c - 1:nch * c, :]
        xs_rows.append(p + mu_ref[...] * (prev - p))
    xs = jnp.concatenate(xs_rows, axis=0)
    r_all = xs[:, 0:width]
    k_all = xs[:, width:2 * width]
    v_all = xs[:, 2 * width:3 * width]
    slab = xs[:, 3 * width:]
    nz = -(w0_ref[...] + _dot(jnp.tanh(slab).astype(BF16), w2_ref[...]))
    softplus = jnp.maximum(nz, 0.0) + jnp.log(1.0 + jnp.exp(-jnp.abs(nz)))
    ld_all = -jnp.exp(-softplus - 0.5)
    a_all = _sigmoid(a0_ref[...] + _dot(slab.astype(BF16), a2_ref[...]))
    g_all = _dot(_sigmoid(slab).astype(BF16), g2_ref[...])
    kk0 = k_all * kk_ref[...]
    kk_all = kk0 * lax.rsqrt(jnp.maximum(_head_sums(kk0 * kk0, hs), 1e-24))
    k2_all = k_all * (1.0 + (a_all - 1.0) * ka_ref[...])
    yield

    units = []
    for b, j in [(b, j) for b in range(nb) for j in range(nch)]:
        rb = slice((b * nch + j) * c, (b * nch + j + 1) * c)
        r, k2, v, ld = r_all[rb], k2_all[rb], v_all[rb], ld_all[rb]
        a_in = -kk_all[rb]
        b_in = kk_all[rb] * a_all[rb]
        cum = _dot_exact_lhs(tri, ld)
        cum_t = cum[c - 1:c, :]
        e_c = jnp.exp(cum)
        e_nc = jnp.exp(-cum)
        e_rem = jnp.exp(cum_t - cum)
        at_f = a_in * jnp.exp(cum - ld)
        rt_f = r * e_c
        kt_f = k2 * e_nc
        bt_f = b_in * e_nc
        kh_f = k2 * e_rem
        bh_f = b_in * e_rem
        p_t = jnp.exp(cum_t)
        for pi in range(n_pairs):
            ls = slice(pi * LANES, (pi + 1) * LANES)
            units.append(dict(
                b=b, j=j, pi=pi,
                at=stack(at_f[:, ls]).astype(BF16), rt=stack(rt_f[:, ls]).astype(BF16),
                kt=stack(kt_f[:, ls]).astype(BF16), bt=stack(bt_f[:, ls]).astype(BF16),
                kh=stack(kh_f[:, ls]).astype(BF16), bh=stack(bh_f[:, ls]).astype(BF16),
                vs=stack(v[:, ls]).astype(BF16), p_t=p_t[:, ls]))
        yield

    for u in units:
        lhs = jnp.concatenate([u['at'], u['rt']], axis=0)
        u['g'] = _dot(lhs, jnp.concatenate([u['kt'], u['bt']], axis=0), NT)
    yield
    for u in units:
        g = u.pop('g')
        u['a_ak'] = jnp.where(smask, g[:2 * c, :2 * c], 0.0).astype(BF16)
        u['pw'] = jnp.where(smask, g[:2 * c, 2 * c:], 0.0).astype(BF16)
        u['a_r'] = jnp.where(jnp.concatenate([imask, imask], axis=1), g[2 * c:], 0.0).astype(BF16)
    for u in units:
        akv = _dot(u.pop('a_ak'), u['vs'])
        u['x'] = jnp.concatenate([u['at'].astype(F32), akv], axis=1)
    yield
    n_lvl = int(np.log2(c))
    for lvl in range(n_lvl):
        for u in units:
            u['x'] = u['x'] + _dot(u['pw'], u['x'].astype(BF16))
        yield
        if lvl + 1 < n_lvl:
            for u in units:
                u['pw'] = _dot(u['pw'], u['pw']).astype(BF16)
            yield
    for u in units:
        x = u.pop('x')
        u['wr'] = jnp.concatenate([x[:, :LANES].astype(BF16), u['rt']], axis=0)
        u['u_loc'] = x[:, LANES:]
    zt = {(b, pi): zt_ref[b, pi] for b in range(nb) for pi in range(n_pairs)}
    for j in range(nch):
        tail = [u for u in units if u['j'] == j]
        for u in tail:
            u['uy'] = _dot(u.pop('wr'), zt[u['b'], u['pi']].astype(BF16), NT)
        yield
        for u in tail:
            uy = u.pop('uy')
            u['u'] = (uy[:2 * c] + u.pop('u_loc')).astype(BF16)
            u['y0'] = uy[2 * c:]
        for u in tail:
            vu = jnp.concatenate([u['vs'], u['u']], axis=0)
            u['y'] = u.pop('y0') + _dot(u['a_r'], vu)
            upd = _dot(vu, jnp.concatenate([u['kh'], u['bh']], axis=0), TN)
            zt[u['b'], u['pi']] = zt[u['b'], u['pi']] * u['p_t'] + upd
        yield
    for (b, pi), z in zt.items():
        zt_ref[b, pi] = z

    inv_n = 1.0 / RW_HEAD
    y = jnp.concatenate(
        [jnp.concatenate([u['y'][:c] + u['y'][c:] for u in units if (u['b'], u['j']) == (b, j)], axis=1)
         for b in range(nb) for j in range(nch)], axis=0)
    mean = _head_sums(y, hs) * inv_n
    yield
    d = y - mean
    var = _head_sums(d * d, hs) * inv_n
    yield
    yn = d * lax.rsqrt(var + RW_GN_EPS) * gnw_ref[...] + gnb_ref[...]
    bonus = _head_sums(r_all * k2_all * rk_ref[...], hs) * v_all
    out = ((yn + bonus) * g_all).astype(o_ref.dtype)
    for b in range(nb):
        o_ref[b] = out[b * nch * c:(b + 1) * nch * c]


def _rwkv7(rw, mu, w0, w2, a0, a2, g2, k_k, k_a, r_k, gn_w, gn_b, ts):
    bsz, s, cols = rw.shape
    width = w0.shape[-1]
    n_pairs = width // LANES
    slab = cols - 3 * width
    dl, al, gl = w2.shape[0], a2.shape[0], g2.shape[0]
    w2f = jnp.zeros((slab, width), F32).at[0:dl].set(w2).astype(BF16)
    a2f = jnp.zeros((slab, width), F32).at[dl:dl + al].set(a2).astype(BF16)
    g2f = jnp.zeros((slab, width), F32).at[dl + al:dl + al + gl].set(g2).astype(BF16)
    mup = jnp.zeros((1, cols), F32).at[0, :mu.shape[-1]].set(mu)
    tri, sm, im, hs = _rwkv_consts(width)
    row = lambda x: x.reshape(1, width)
    const = lambda i: (0, 0)
    vec = pl.BlockSpec((1, width), const)
    return pl.pallas_call(
        _rwkv7_kernel,
        out_shape=jax.ShapeDtypeStruct((bsz, s, width), BF16),
        grid=(s // ts,),
        in_specs=[pl.BlockSpec((bsz, ts, cols), lambda i: (0, i, 0)),
                  pl.BlockSpec((1, cols), const),
                  vec, vec, vec, vec, vec, vec, vec,
                  pl.BlockSpec((slab, width), const),
                  pl.BlockSpec((slab, width), const),
                  pl.BlockSpec((slab, width), const),
                  pl.BlockSpec(tri.shape, const),
                  pl.BlockSpec(sm.shape, const),
                  pl.BlockSpec(im.shape, const),
                  pl.BlockSpec(hs.shape, const)],
        out_specs=pl.BlockSpec((bsz, ts, width), lambda i: (0, i, 0)),
        scratch_shapes=[pltpu.VMEM((bsz, 1, cols), F32),
                        pltpu.VMEM((bsz, n_pairs, LANES, LANES), F32)],
        compiler_params=_cparams(("arbitrary",)),
        name="rwkv7",
    )(rw, mup, row(w0), row(a0), row(k_k), row(k_a), row(r_k), row(gn_w), row(gn_b),
      w2f, a2f, g2f, tri, sm, im, hs)


def _out_proj_kernel(n_groups, n_experts,
                     x_ref, oa_ref, ob_ref, ga_ref, gb_ref, gt1_ref, sc2_ref, sh2_ref, g2_ref,
                     wa_ref, wb_ref, wo_ref, wr_ref, wrl_ref, br_ref, upper_ref,
                     x1_ref, h2_ref, routet_ref, cnt_ref, carry_ref):
    first = (pl.program_id(0) == 0) & (pl.program_id(1) == 0)

    @pl.when(first)
    def _():
        carry_ref[...] = jnp.zeros_like(carry_ref)

    pa = _dot(oa_ref[0], wa_ref[...])
    pb = _dot(ob_ref[0], wb_ref[...])
    mixed = ga_ref[0].astype(F32) * pa + gb_ref[0].astype(F32) * pb
    x1 = x_ref[0] + gt1_ref[0] * _dot(mixed.astype(BF16), wo_ref[...])
    x1_ref[0] = x1
    ms = jnp.mean(x1 * x1, axis=-1, keepdims=True)
    h2 = (x1 * lax.rsqrt(ms + NORM_EPS) * g2_ref[...]) * (1.0 + sc2_ref[0]) + sh2_ref[0]
    _rows_to_tiles(h2_ref, h2)

    h2_hi = h2.astype(BF16)
    h2_lo = (h2 - h2_hi.astype(F32)).astype(BF16)
    logits = (_dot(wr_ref[...], h2_hi, NT) + _dot(wr_ref[...], h2_lo, NT) + _dot(wrl_ref[...], h2_hi, NT)
              + br_ref[...])
    row = lax.broadcasted_iota(jnp.int32, logits.shape, 0)
    neg = jnp.float32(-jnp.inf)
    big = jnp.int32(1 << 20)
    eg = n_experts // n_groups
    is_g = (row >= n_experts) & (row < n_experts + n_groups)
    lg = jnp.where(is_g, logits, neg)
    mg = jnp.max(lg, axis=0, keepdims=True)
    p_grp = 1.0 / jnp.sum(jnp.where(is_g, jnp.exp(lg - mg), 0.0), axis=0, keepdims=True)
    gidx = jnp.min(jnp.where(lg == mg, row, big), axis=0, keepdims=True) - n_experts
    sel = (row >= gidx * eg) & (row < gidx * eg + eg)
    le = jnp.where(sel, logits, neg)
    me = jnp.max(le, axis=0, keepdims=True)
    pe_un = jnp.where(sel, jnp.exp(le - me), 0.0)
    pe = jnp.where(sel, pe_un / jnp.sum(pe_un, axis=0, keepdims=True), -1.0)
    v1 = jnp.max(pe, axis=0, keepdims=True)
    i1 = jnp.min(jnp.where(pe == v1, row, big), axis=0, keepdims=True)
    pe2 = jnp.where(row == i1, -1.0, pe)
    v2 = jnp.max(pe2, axis=0, keepdims=True)
    i2 = jnp.min(jnp.where(pe2 == v2, row, big), axis=0, keepdims=True)
    wsum = v1 + v2
    w1 = p_grp * v1 / wsum
    w2 = p_grp * v2 / wsum

    oh1 = (row == i1).astype(F32)
    oh2 = (row == i2).astype(F32)
    both = oh1 + oh2
    before = _dot(both.astype(BF16), upper_ref[...]) + carry_ref[...]
    rank1 = jnp.sum(oh1 * before, axis=0, keepdims=True)
    rank2 = jnp.sum(oh2 * before, axis=0, keepdims=True)
    carry_ref[...] = carry_ref[...] + jnp.sum(both, axis=1, keepdims=True)
    cnt_ref[...] = carry_ref[...]
    zero = jnp.zeros_like(w1)
    routet_ref[...] = jnp.concatenate(
        [w1, w2, i1.astype(F32), i2.astype(F32), rank1, rank2, zero, zero], axis=0)


def _out_proj(x, o_a, o_b, gates, mod3, norm2_g, wa, wb, wo, wr, br, n_groups, n_experts, tm):
    bsz, s, d = x.shape
    wdt = o_a.shape[-1]
    upper = jnp.asarray(np.triu(np.ones((tm, tm), np.float32), 1), BF16)
    wrt = wr.T
    wr_hi = wrt.astype(BF16)
    wr_lo = (wrt - wr_hi.astype(F32)).astype(BF16)
    const = lambda b, i: (0, 0)
    tile = lambda b, i: (b, i, 0)
    kern = functools.partial(_out_proj_kernel, n_groups, n_experts)
    return pl.pallas_call(
        kern,
        out_shape=(jax.ShapeDtypeStruct((bsz, s, d), F32),
                   jax.ShapeDtypeStruct((bsz * s * SUB, LANES), U32),
                   jax.ShapeDtypeStruct((bsz * (s // tm) * SUBLANES, tm), F32),
                   jax.ShapeDtypeStruct((LANES, 1), F32)),
        grid=(bsz, s // tm),
        in_specs=[pl.BlockSpec((1, tm, d), tile),
                  pl.BlockSpec((1, tm, wdt), tile),
                  pl.BlockSpec((1, tm, wdt), tile),
                  pl.BlockSpec((1, tm, d), lambda b, i: (b, i, 0)),
                  pl.BlockSpec((1, tm, d), lambda b, i: (b, i, 1)),
                  pl.BlockSpec((1, 1, d), lambda b, i: (b * 6 + 2, 0, 0)),
                  pl.BlockSpec((1, 1, d), lambda b, i: (b * 6 + 4, 0, 0)),
                  pl.BlockSpec((1, 1, d), lambda b, i: (b * 6 + 3, 0, 0)),
                  pl.BlockSpec((1, d), const),
                  pl.BlockSpec(wa.shape, const),
                  pl.BlockSpec(wb.shape, const),
                  pl.BlockSpec(wo.shape, const),
                  pl.BlockSpec(wrt.shape, const),
                  pl.BlockSpec(wrt.shape, const),
                  pl.BlockSpec((LANES, 1), const),
                  pl.BlockSpec((tm, tm), const)],
        out_specs=(pl.BlockSpec((1, tm, d), tile),
                   pl.BlockSpec((tm * SUB, LANES), lambda b, i: (b * (s // tm) + i, 0)),
                   pl.BlockSpec((SUBLANES, tm), lambda b, i: (b * (s // tm) + i, 0)),
                   pl.BlockSpec((LANES, 1), const)),
        scratch_shapes=[pltpu.VMEM((LANES, 1), F32)],
        compiler_params=_cparams(("arbitrary", "arbitrary")),
        name="out_proj",
    )(x, o_a, o_b, gates, gates, mod3, mod3, mod3, norm2_g.reshape(1, d), wa, wb, wo, wr_hi, wr_lo,
      br.reshape(LANES, 1), upper)


def _moe_slotmap_kernel(tm_o, n_tok, blk, dest_ref, zstart_ref, zcnt_ref, nused_ref, smap_ref):
    i = pl.program_id(0)
    n_slots = smap_ref.shape[0] - blk
    n_experts = zcnt_ref.shape[0]

    @pl.when(i == 0)
    def _():
        unroll = 8

        def init(g, carry):
            for u in range(unroll):
                smap_ref[blk + g * unroll + u] = TOP_K * n_tok
            return carry
        lax.fori_loop(nused_ref[0] * (blk // unroll), n_slots // unroll, init, 0)
        for r in range(blk):
            smap_ref[r] = TOP_K * n_tok + n_experts * blk + r
        for e in range(n_experts):
            def pad(g, carry, e=e):
                for u in range(unroll):
                    j = jnp.maximum(zcnt_ref[e] - 1 - (g * unroll + u), 0)
                    smap_ref[blk + zstart_ref[e] + j] = TOP_K * n_tok + e * blk + j
                return carry
            lax.fori_loop(0, (zcnt_ref[e] + unroll - 1) // unroll, pad, 0)

    base = i * (TOP_K * tm_o)
    both = 1 + (1 << SMAP_BITS)
    for k in range(TOP_K):
        v0 = k * n_tok + (i * tm_o) * both
        for r in range(tm_o):
            smap_ref[dest_ref[base + k * tm_o + r]] = v0 + r * both


def _moe_slotmap(dest, zstart, zcnt, n_used, n_slots, blk, tm_o, n_tok):
    grid_spec = pltpu.PrefetchScalarGridSpec(
        num_scalar_prefetch=4,
        grid=(n_tok // tm_o,),
        in_specs=[],
        out_specs=pl.BlockSpec(memory_space=pltpu.SMEM),
    )
    n_experts = zcnt.shape[0]
    assert TOP_K * n_tok + (n_experts + 1) * blk <= (1 << SMAP_BITS) and n_tok <= (1 << (31 - SMAP_BITS))
    return pl.pallas_call(
        functools.partial(_moe_slotmap_kernel, tm_o, n_tok, blk),
        out_shape=jax.ShapeDtypeStruct((blk + n_slots,), jnp.int32),
        grid_spec=grid_spec,
        compiler_params=_cparams(("arbitrary",)),
        name="moe_slotmap",
    )(dest, zstart, zcnt, n_used)


def _moe_expert_gather_kernel(n_tok, blk_e_ref, nused_ref, smap_ref, h_ref, wg_ref, wu_ref, wd_ref, y_ref,
                              hv, xbuf, ystage, wgb, wub, wdb, sem, hsem):
    i = pl.program_id(0)
    nb = pl.num_programs(0)
    n_used = nused_ref[0]
    blk = xbuf.shape[1] // SUB
    first_real_blocks = TOP_K * n_tok // blk

    def gather(b, slot, rows):
        for r in rows:
            tok = lax.shift_right_logical(smap_ref[(b + 1) * blk + r], SMAP_BITS)
            src = pl.multiple_of(tok * SUB, SUB)
            xbuf[slot, pl.ds(r * SUB, SUB), :] = hv[pl.ds(src, SUB), :]

    def issue(b, slot, rows):
        for r in rows:
            t = smap_ref[(b + 1) * blk + r] & ((1 << SMAP_BITS) - 1)
            dst = pl.multiple_of(t * SUB, SUB)
            pltpu.make_async_copy(ystage.at[slot, pl.ds(r * SUB, SUB), :], y_ref.at[pl.ds(dst, SUB), :],
                                  sem.at[slot]).start(priority=r % 2)

    def wait_block(slot):
        pltpu.make_async_copy(ystage.at[slot], y_ref.at[pl.ds(0, blk * SUB), :], sem.at[slot]).wait()

    @pl.when(i == 0)
    def _():
        load = pltpu.make_async_copy(h_ref, hv, hsem)
        load.start()
        ystage[...] = jnp.zeros_like(ystage)
        n_spare_blocks = y_ref.shape[0] // (blk * SUB) - first_real_blocks

        def spare_copy(c):
            dst = (first_real_blocks + c) * blk * SUB
            return pltpu.make_async_copy(ystage.at[0], y_ref.at[pl.ds(dst, blk * SUB), :], sem.at[0])

        for c in range(n_spare_blocks):
            spare_copy(c).start()
        for c in range(n_spare_blocks):
            spare_copy(c).wait()
        load.wait()
        gather(0, 0, range(blk))

    new_expert = (i == 0) | (blk_e_ref[i] != blk_e_ref[jnp.maximum(i - 1, 0)])

    @pl.when((i < n_used) & new_expert)
    def _():
        wgb[...] = wg_ref[0].astype(BF16)
        wub[...] = wu_ref[0].astype(BF16)
        wdb[...] = wd_ref[0].astype(BF16)

    @pl.when(i < n_used)
    def _():
        slot = i % 2
        pslot = 1 - slot
        nxt = jnp.minimum(i + 1, nb - 1)

        @pl.when(i > 0)
        def _():
            wait_block(slot)

        q = blk // 4
        xb = _tiles_to_rows(xbuf.at[slot], blk).astype(BF16)
        issue(i - 1, pslot, range(0, q))
        gather(nxt, pslot, range(0, q))
        hg = _dot(xb, wgb[...])
        issue(i - 1, pslot, range(q, 2 * q))
        gather(nxt, pslot, range(q, 2 * q))
        hu = _dot(xb, wub[...])
        issue(i - 1, pslot, range(2 * q, 3 * q))
        gather(nxt, pslot, range(2 * q, 3 * q))
        hid = (_silu(hg) * hu).astype(BF16)
        y = _dot(hid, wdb[...])
        issue(i - 1, pslot, range(3 * q, blk))
        gather(nxt, pslot, range(3 * q, blk))
        _rows_to_tiles(ystage.at[slot], y)

        @pl.when(i == n_used - 1)
        def _():
            issue(i, slot, range(blk))
            wait_block(pslot)
            wait_block(slot)


def _moe_experts_gather(h2, smap, blk_e, n_used, w_gate, w_up, w_down, blk, n_tok):
    d = w_gate.shape[1]
    assert d == 2 * SUB * LANES and h2.shape == (n_tok * SUB, LANES)
    n_slots = smap.shape[0] - blk
    nb = n_slots // blk
    f = w_gate.shape[-1]
    n_experts = w_gate.shape[0]
    assert (TOP_K * n_tok) % blk == 0
    n_tiles = TOP_K * n_tok + (n_experts + 1) * blk
    grid_spec = pltpu.PrefetchScalarGridSpec(
        num_scalar_prefetch=3,
        grid=(nb,),
        in_specs=[pl.BlockSpec(memory_space=pl.ANY),
                  pl.BlockSpec((1, d, f), lambda i, be, nu, sm: (be[i], 0, 0)),
                  pl.BlockSpec((1, d, f), lambda i, be, nu, sm: (be[i], 0, 0)),
                  pl.BlockSpec((1, f, d), lambda i, be, nu, sm: (be[i], 0, 0))],
        out_specs=pl.BlockSpec(memory_space=pl.ANY),
        scratch_shapes=[pltpu.VMEM((n_tok * SUB, LANES), U32),
                        pltpu.VMEM((2, blk * SUB, LANES), U32),
                        pltpu.VMEM((2, blk * SUB, LANES), U32),
                        pltpu.VMEM((d, f), BF16),
                        pltpu.VMEM((d, f), BF16),
                        pltpu.VMEM((f, d), BF16),
                        pltpu.SemaphoreType.DMA((2,)),
                        pltpu.SemaphoreType.DMA],
    )
    return pl.pallas_call(
        functools.partial(_moe_expert_gather_kernel, n_tok),
        out_shape=jax.ShapeDtypeStruct((n_tiles * SUB, LANES), U32),
        grid_spec=grid_spec,
        compiler_params=_cparams(("arbitrary",)),
        name="moe_experts",
    )(blk_e, n_used, smap, h2, w_gate, w_up, w_down)


def _moe_combine_kernel(x1_ref, routet_ref, gt2_ref, fg_ref, y1_ref, y2_ref, o_ref):
    tm = x1_ref.shape[0]
    route = routet_ref[...].T
    moe = route[:, 0:1] * _tiles_to_rows(y1_ref, tm) + route[:, 1:2] * _tiles_to_rows(y2_ref, tm)
    xo = x1_ref[...] + gt2_ref[0] * moe
    ms = jnp.mean(xo * xo, axis=-1, keepdims=True)
    o_ref[...] = xo * lax.rsqrt(ms + NORM_EPS) * fg_ref[...]


def _moe_combine(x1, route_t, yt, mod3, final_g, s, tm):
    n, d = x1.shape
    tiles_per_batch = s // tm
    n_steps = n // tm
    per_o = route_t.shape[1] // tm
    return pl.pallas_call(
        _moe_combine_kernel,
        out_shape=jax.ShapeDtypeStruct((n, d), F32),
        grid=(n_steps,),
        in_specs=[pl.BlockSpec((tm, d), lambda i: (i, 0)),
                  pl.BlockSpec((SUBLANES, tm), lambda i: (i // per_o, i % per_o)),
                  pl.BlockSpec((1, 1, d), lambda i: ((i // tiles_per_batch) * 6 + 5, 0, 0)),
                  pl.BlockSpec((1, d), lambda i: (0, 0)),
                  pl.BlockSpec((tm * SUB, LANES), lambda i: (i, 0)),
                  pl.BlockSpec((tm * SUB, LANES), lambda i: (n_steps + i, 0))],
        out_specs=pl.BlockSpec((tm, d), lambda i: (i, 0)),
        compiler_params=_cparams(("arbitrary",)),
        name="moe_combine",
    )(x1, route_t, mod3, final_g.reshape(1, d), yt, yt)


def _pick(n, candidates):
    for t in candidates:
        if n % t == 0:
            return t
    raise ValueError(f"no tile in {candidates} divides {n}")


def kernel(x, c, ada_w, ada_b, norm1_g, w_in, hg_lb, hg_norm_g, rw_mu, rw_w0, rw_w2, rw_a0, rw_a2, rw_g2, rw_kk, rw_ka, rw_rk, rw_gn_w, rw_gn_b, w_proj_a, w_proj_b, w_out, norm2_g, router_g_w, router_g_b, router_e_w, router_e_b, exp_w_gate, exp_w_up, exp_w_down, final_g):
    bsz, s, d = x.shape
    depth = ada_w.shape[0]
    hg_f = hg_lb.shape[-1]
    hg_w = hg_norm_g.shape[-1]
    rw_w = rw_w0.shape[-1]
    rw_cols = rw_mu.shape[-1]
    n_groups = router_g_w.shape[-1]
    n_experts = router_e_w.shape[-1]
    assert hg_f == hg_w and s % CHUNK == 0 and n_experts + n_groups <= LANES and d == 2 * SUB * LANES

    lb_all = jnp.cumsum(jax.nn.softmax(hg_lb.astype(F32), axis=0), axis=0)
    n = bsz * s
    blk = 512
    n_blocks = (n * TOP_K + n_experts * blk) // blk
    for l in range(depth):
        mod = _ada_mod(c, ada_w[l], ada_b[l])
        mod3 = mod.reshape(bsz * 6, 1, d)

        hg_cols = 2 * hg_f + 2 * hg_w
        rw_pad = -(-rw_cols // 256) * 256
        wl = w_in[l]
        w_hg = wl[:, :hg_cols].astype(BF16)
        w_rw = jnp.zeros((d, rw_pad), BF16).at[:, :rw_cols].set(wl[:, hg_cols:hg_cols + rw_cols].astype(BF16))
        w_gt = wl[:, hg_cols + rw_cols:].astype(BF16)
        hg, rw, gates = _in_proj(x, mod3, norm1_g[l], lb_all[l], w_hg, w_rw, w_gt, _pick(s, (512, 256, 128, 64)))

        o_a = _hgrn2(hg, hg_norm_g[l], _pick(s, (1024, 512, 256, 128, 64)))
        o_b = _rwkv7(rw, rw_mu[l], rw_w0[l], rw_w2[l], rw_a0[l], rw_a2[l], rw_g2[l],
                     rw_kk[l], rw_ka[l], rw_rk[l].reshape(-1), rw_gn_w[l], rw_gn_b[l],
                     _pick(s, (2 * CHUNK, CHUNK)))

        wr = jnp.zeros((d, LANES), F32).at[:, :n_experts].set(router_e_w[l])
        wr = wr.at[:, n_experts:n_experts + n_groups].set(router_g_w[l])
        br = jnp.zeros((1, LANES), F32).at[0, :n_experts].set(router_e_b[l])
        br = br.at[0, n_experts:n_experts + n_groups].set(router_g_b[l])
        tm_o = _pick(s, (1024, 512, 256, 128, 64))
        x1, h2, route_t, counts = _out_proj(
            x, o_a, o_b, gates, mod3, norm2_g[l],
            w_proj_a[l].astype(BF16), w_proj_b[l].astype(BF16), w_out[l].astype(BF16),
            wr, br, n_groups, n_experts, tm_o)

        rt = route_t.reshape(n // tm_o, SUBLANES, tm_o)
        eid = rt[:, 2:4, :].astype(jnp.int32)
        rank = rt[:, 4:6, :].astype(jnp.int32)
        cnt = counts[:n_experts, 0].astype(jnp.int32)
        padded = (cnt + blk - 1) // blk * blk
        pad_end = jnp.cumsum(padded)
        pad_start = pad_end - padded
        e_ax = jnp.arange(n_experts, dtype=jnp.int32)[:, None, None, None]
        dest = rank + jnp.sum(jnp.where(eid[None] == e_ax, pad_start[:, None, None, None], 0), axis=0)
        blk_start = jnp.arange(n_blocks, dtype=jnp.int32) * blk
        blk_e = jnp.minimum(jnp.sum((pad_end[None, :] <= blk_start[:, None]).astype(jnp.int32), axis=1),
                            n_experts - 1)
        n_used = (pad_end[-1:] // blk).astype(jnp.int32)
        tm = tm_o
        smap = _moe_slotmap(dest.reshape(-1) + blk, pad_start + cnt, padded - cnt, n_used, n_blocks * blk, blk, tm_o, n)
        yt = _moe_experts_gather(h2, smap, blk_e, n_used, exp_w_gate[l], exp_w_up[l], exp_w_down[l], blk, n)
        last = l == depth - 1
        assert last, "the final RMSNorm is fused into the last layer's combine"
        out = _moe_combine(x1.reshape(n, d), route_t, yt, mod3, final_g, s, tm)
        x = out.reshape(bsz, s, d)
    return x
```

```python
import functools

import numpy as np
import jax
import jax.numpy as jnp
from jax import lax
from jax.experimental import pallas as pl
from jax.experimental.pallas import tpu as pltpu

F32 = jnp.float32
BF16 = jnp.bfloat16
HIGHEST = lax.Precision.HIGHEST

NORM_EPS = 1e-6
HG_HEAD = 128
RW_HEAD = 64
RW_GN_EPS = 64e-5
TOP_K = 2
CHUNK = 64
LANES = 128
SUB = 4
U32 = jnp.uint32
MXU_K = 256
SUBLANES = 8
SMAP_BITS = 16
VMEM_LIMIT = 56 * 1024 * 1024

NT = (((1,), (1,)), ((), ()))
TN = (((0,), (0,)), ((), ()))


def _dot(a, b, dims=None, precision=None):
    if dims is None:
        return jnp.dot(a, b, preferred_element_type=F32, precision=precision)
    return lax.dot_general(a, b, dims, preferred_element_type=F32, precision=precision)


def _split3(x):
    hi = x.astype(BF16)
    r1 = x - hi.astype(F32)
    mid = r1.astype(BF16)
    lo = (r1 - mid.astype(F32)).astype(BF16)
    return hi, mid, lo


def _dot_exact_lhs(m3_bf16, x):
    return _dot(m3_bf16, jnp.concatenate(_split3(x), axis=0))


def _head_sums(x, m2_bf16):
    outs = []
    for g0 in range(0, x.shape[1], MXU_K):
        xg = x[:, g0:g0 + MXU_K]
        hi = xg.astype(BF16)
        lo = (xg - hi.astype(F32)).astype(BF16)
        outs.append(_dot(jnp.concatenate([hi, lo], axis=1), m2_bf16))
    return jnp.concatenate(outs, axis=1)


def _sigmoid(x):
    return 1.0 / (1.0 + jnp.exp(-x))


def _silu(x):
    return x * _sigmoid(x)


def _rows_to_tiles(ref, val):
    m, half = val.shape[0], val.shape[1] // 2
    hi = lax.bitcast_convert_type(val[:, :half].astype(BF16).astype(F32), U32)
    lo = lax.bitcast_convert_type(val[:, half:].astype(BF16).astype(F32), U32)
    w = (hi & jnp.uint32(0xFFFF0000)) | (lo >> 16)
    for j in range(SUB):
        ref[pl.ds(j, m, stride=SUB), :] = w[:, j * LANES:(j + 1) * LANES]


def _tiles_to_rows(ref, m, base=0):
    w = jnp.concatenate([ref[pl.ds(base * SUB + j, m, stride=SUB), :] for j in range(SUB)], axis=1)
    hi = lax.bitcast_convert_type(w & jnp.uint32(0xFFFF0000), F32)
    lo = lax.bitcast_convert_type(w << 16, F32)
    return jnp.concatenate([hi, lo], axis=1)


def _cparams(sem):
    return pltpu.CompilerParams(dimension_semantics=sem, vmem_limit_bytes=VMEM_LIMIT)


def _ada_kernel(c_ref, w_ref, b_ref, o_ref):
    c = c_ref[...]
    o_ref[...] = _dot(_silu(c), w_ref[...], precision=HIGHEST) + b_ref[...]


def _ada_mod(c, w, b):
    bsz, d = c.shape
    n = w.shape[1]
    rows = 8
    cp = jnp.zeros((rows, d), F32).at[:bsz].set(c)
    tn = 1536
    out = pl.pallas_call(
        _ada_kernel,
        out_shape=jax.ShapeDtypeStruct((rows, n), F32),
        grid=(n // tn,),
        in_specs=[pl.BlockSpec((rows, d), lambda j: (0, 0)),
                  pl.BlockSpec((d, tn), lambda j: (0, j)),
                  pl.BlockSpec((1, tn), lambda j: (0, j))],
        out_specs=pl.BlockSpec((rows, tn), lambda j: (0, j)),
        compiler_params=_cparams(("arbitrary",)),
        name="ada_mod",
    )(cp, w, b.reshape(1, n))
    return out[:bsz]


def _in_proj_kernel(x_ref, sh_ref, sc_ref, g_ref, lb_ref, whg_ref, wrw_ref, wgt_ref, hg_ref, rw_ref, gt_ref):
    x = x_ref[0]
    ms = jnp.mean(x * x, axis=-1, keepdims=True)
    h = (x * lax.rsqrt(ms + NORM_EPS) * g_ref[...]) * (1.0 + sc_ref[0]) + sh_ref[0]
    hb = h.astype(BF16)
    lb = lb_ref[...]
    w = lb.shape[1]
    hg_maps = (_silu, lambda t: lb + (1.0 - lb) * _sigmoid(t), lambda t: t, _silu)
    for part, fn in enumerate(hg_maps):
        hg_ref[0, :, part * w:(part + 1) * w] = fn(_dot(hb, whg_ref[:, part * w:(part + 1) * w]))
    step = 512
    for n0 in range(0, wrw_ref.shape[1], 256):
        rw_ref[0, :, n0:n0 + 256] = _dot(hb, wrw_ref[:, n0:n0 + 256])
    for n0 in range(0, wgt_ref.shape[1], step):
        gt_ref[0, :, n0:n0 + step] = _sigmoid(_dot(hb, wgt_ref[:, n0:n0 + step])).astype(BF16)


def _in_proj(x, mod3, norm_g, lb, w_hg, w_rw, w_gt, tm):
    bsz, s, d = x.shape
    assert w_hg.shape[1] == 4 * lb.shape[0]
    n_hg, n_rw, n_gt = w_hg.shape[1], w_rw.shape[1], w_gt.shape[1]
    const = lambda b, i: (0, 0)
    return pl.pallas_call(
        _in_proj_kernel,
        out_shape=(jax.ShapeDtypeStruct((bsz, s, n_hg), F32),
                   jax.ShapeDtypeStruct((bsz, s, n_rw), F32),
                   jax.ShapeDtypeStruct((bsz, s, n_gt), BF16)),
        grid=(bsz, s // tm),
        in_specs=[pl.BlockSpec((1, tm, d), lambda b, i: (b, i, 0)),
                  pl.BlockSpec((1, 1, d), lambda b, i: (b * 6 + 0, 0, 0)),
                  pl.BlockSpec((1, 1, d), lambda b, i: (b * 6 + 1, 0, 0)),
                  pl.BlockSpec((1, d), const),
                  pl.BlockSpec((1, lb.shape[0]), const),
                  pl.BlockSpec((d, n_hg), const),
                  pl.BlockSpec((d, n_rw), const),
                  pl.BlockSpec((d, n_gt), const)],
        out_specs=(pl.BlockSpec((1, tm, n_hg), lambda b, i: (b, i, 0)),
                   pl.BlockSpec((1, tm, n_rw), lambda b, i: (b, i, 0)),
                   pl.BlockSpec((1, tm, n_gt), lambda b, i: (b, i, 0))),
        compiler_params=_cparams(("arbitrary", "arbitrary")),
        name="in_proj",
    )(x, mod3, mod3, norm_g.reshape(1, d), lb.reshape(1, -1), w_hg, w_rw, w_gt)


_HG_LEVELS = (32, 16, 8, 4, 2, 1)


def _hgrn2_consts(width):
    c = CHUNK
    t = np.arange(c)[:, None]
    s = np.arange(c)[None, :]
    blocks = [(s <= t), (s > t)]
    lvl_masks = []
    right = []
    for h in _HG_LEVELS:
        m = (t // (2 * h)) * 2 * h + h
        is_r = (t & h) != 0
        blk = np.where(is_r, (s >= m) & (s <= t), (s > t) & (s <= m - 1))
        blocks.append(blk)
        lvl_masks.append(is_r & ((s & h) == 0) & ((t // (2 * h)) == (s // (2 * h))))
        right.append(np.broadcast_to(is_r, (c, width)))
    mst = np.tile(np.concatenate(blocks, axis=0).astype(np.float32), (1, 3))
    lm = np.stack([np.eye(c, dtype=bool)] + lvl_masks).astype(np.float32)
    rm = np.stack(right).astype(np.float32)
    return jnp.asarray(mst, BF16), jnp.asarray(lm, F32), jnp.asarray(rm, F32)


def _hgrn2_kernel(q_ref, f_ref, i_ref, g_ref, ng_ref, mst_ref, lm_ref, rm_ref, o_ref, st_ref):
    c = CHUNK
    n_chunks = q_ref.shape[1] // c

    @pl.when(pl.program_id(1) == 0)
    def _():
        st_ref[...] = jnp.zeros_like(st_ref)

    nsub = next(n for n in (4, 2, 1) if n_chunks % n == 0)

    def chunk_body(ci, carry):
        r0 = pl.multiple_of(ci * (nsub * c), nsub * c)
        for _ in _hgrn2_steps(q_ref, f_ref, i_ref, g_ref, ng_ref, mst_ref, lm_ref, rm_ref, o_ref, st_ref,
                              0, r0, nsub):
            pass
        return carry

    lax.fori_loop(0, n_chunks // nsub, chunk_body, 0)


def _hgrn2_steps(q_ref, f_ref, i_ref, g_ref, ng_ref, mst_ref, lm_ref, rm_ref, o_ref, st_ref, b, r0, nsub):
    c = CHUNK
    n_heads = q_ref.shape[2] // HG_HEAD
    mst = mst_ref[...]
    ng = ng_ref[...]
    heads = [slice(hd * HG_HEAD, (hd + 1) * HG_HEAD) for hd in range(n_heads)]
    subs = []
    for j in range(nsub):
        rows = pl.ds(r0 + j * c, c)
        q = q_ref[b, rows, :]
        f = f_ref[b, rows, :]
        k = 1.0 - f
        ex = jnp.exp2(_dot_exact_lhs(mst, jnp.log2(f)))
        subs.append(dict(rows=rows, q=q, k=k, ex=ex, vb=i_ref[b, rows, :].astype(BF16),
                         qd=(q * ex[0:c]).astype(BF16), kr=(k * ex[c:2 * c]).astype(BF16)))
        yield
    for sb in subs:
        qb, kb = sb['q'].astype(BF16), sb['k'].astype(BF16)
        sb['sc'] = [lm_ref[0] * _dot(qb[:, ls], kb[:, ls], NT) for ls in heads]
        sb['dqk'] = sb['q'] - sb['k']
    yield
    for li, h in enumerate(_HG_LEVELS):
        for sb in subs:
            if h % SUBLANES == 0:
                qk = jnp.concatenate([(sb['q'] if m % 2 else sb['k'])[m * h:(m + 1) * h] for m in range(c // h)],
                                     axis=0)
            else:
                qk = sb['k'] + rm_ref[li] * sb['dqk']
            g_l = (qk * sb['ex'][(2 + li) * c:(3 + li) * c]).astype(BF16)
            sb['sc'] = [s_h + lm_ref[li + 1] * _dot(g_l[:, ls], g_l[:, ls], NT)
                        for s_h, ls in zip(sb['sc'], heads)]
        yield
    for sb in subs:
        sb['kv'] = [_dot(sb['vb'][:, ls], sb['kr'][:, ls], TN) for ls in heads]
        sb['o'] = [_dot(s_h.astype(BF16), sb['vb'][:, ls]) for s_h, ls in zip(sb['sc'], heads)]
        yield
    sts = [st_ref[b, hd] for hd in range(n_heads)]
    for sb in subs:
        sb['o'] = [o_h + _dot(sb['qd'][:, ls], st.astype(BF16), NT) for o_h, ls, st in zip(sb['o'], heads, sts)]
        sts = [st * sb['ex'][c - 1:c, ls] + kv for st, ls, kv in zip(sts, heads, sb['kv'])]
        yield
    for hd in range(n_heads):
        st_ref[b, hd] = sts[hd]
    for sb in subs:
        on = [o_h * lax.rsqrt(jnp.mean(o_h * o_h, axis=-1, keepdims=True) + NORM_EPS) for o_h in sb['o']]
        o_full = jnp.concatenate(on, axis=1) * ng
        o_ref[b, sb['rows'], :] = (o_full * g_ref[b, sb['rows'], :]).astype(o_ref.dtype)
        yield


def _hgrn2(hg, norm_g, ts):
    bsz, s, n4 = hg.shape
    w = n4 // 4
    mst, lm, rm = _hgrn2_consts(w)
    n_heads = w // HG_HEAD
    const2 = lambda b, i: (0, 0)
    const3 = lambda b, i: (0, 0, 0)
    return pl.pallas_call(
        _hgrn2_kernel,
        out_shape=jax.ShapeDtypeStruct((bsz, s, w), BF16),
        grid=(bsz, s // ts),
        in_specs=[pl.BlockSpec((1, ts, w), lambda b, i: (b, i, 0)),
                  pl.BlockSpec((1, ts, w), lambda b, i: (b, i, 1)),
                  pl.BlockSpec((1, ts, w), lambda b, i: (b, i, 2)),
                  pl.BlockSpec((1, ts, w), lambda b, i: (b, i, 3)),
                  pl.BlockSpec((1, w), const2),
                  pl.BlockSpec(mst.shape, const2),
                  pl.BlockSpec(lm.shape, const3),
                  pl.BlockSpec(rm.shape, const3)],
        out_specs=pl.BlockSpec((1, ts, w), lambda b, i: (b, i, 0)),
        scratch_shapes=[pltpu.VMEM((1, n_heads, HG_HEAD, HG_HEAD), F32)],
        compiler_params=_cparams(("arbitrary", "arbitrary")),
        name="hgrn2",
    )(hg, hg, hg, hg, norm_g.reshape(1, w), mst, lm, rm)


def _rwkv_consts(width):
    c = CHUNK
    t = np.arange(c)[:, None]
    s = np.arange(c)[None, :]
    tri = np.tile((s <= t).astype(np.float32), (1, 3))
    tt = np.arange(2 * c)[:, None]
    ss = np.arange(2 * c)[None, :]
    same = (tt // c) == (ss // c)
    strict = same & ((ss % c) < (tt % c))
    incl = same & ((ss % c) <= (tt % c))
    hsum = (np.arange(MXU_K)[:, None] // RW_HEAD) == (np.arange(MXU_K)[None, :] // RW_HEAD)
    hsum = np.tile(hsum, (2, 1))
    return (jnp.asarray(tri, BF16), jnp.asarray(strict.astype(np.float32), F32),
            jnp.asarray(incl.astype(np.float32), F32), jnp.asarray(hsum.astype(np.float32), BF16))


def _rwkv7_kernel(p_ref, mu_ref, w0_ref, a0_ref, kk_ref, ka_ref, rk_ref, gnw_ref, gnb_ref,
                  w2_ref, a2_ref, g2_ref, tri_ref, sm_ref, im_ref, hs_ref,
                  o_ref, carry_ref, zt_ref):
    @pl.when(pl.program_id(0) == 0)
    def _():
        carry_ref[...] = jnp.zeros_like(carry_ref)
        zt_ref[...] = jnp.zeros_like(zt_ref)

    for _ in _rwkv7_steps(p_ref, mu_ref, w0_ref, a0_ref, kk_ref, ka_ref, rk_ref, gnw_ref, gnb_ref,
                          w2_ref, a2_ref, g2_ref, tri_ref, sm_ref, im_ref, hs_ref, o_ref, carry_ref, zt_ref):
        pass


def _rwkv7_steps(p_ref, mu_ref, w0_ref, a0_ref, kk_ref, ka_ref, rk_ref, gnw_ref, gnb_ref,
                 w2_ref, a2_ref, g2_ref, tri_ref, sm_ref, im_ref, hs_ref, o_ref, carry_ref, zt_ref):
    c = CHUNK
    nb = p_ref.shape[0]
    nch = p_ref.shape[1] // c
    width = o_ref.shape[2]
    n_pairs = width // LANES

    hs = hs_ref[...]
    tri = tri_ref[...]
    smask = sm_ref[...] > 0
    imask = im_ref[...] > 0
    lane = lax.broadcasted_iota(jnp.int32, (c, LANES), 1)
    m0 = (lane < RW_HEAD).astype(F32)
    m1 = 1.0 - m0

    def stack(x):
        return jnp.concatenate([x * m0, x * m1], axis=0)

    xs_rows = []
    for b in range(nb):
        p = p_ref[b]
        row = lax.broadcasted_iota(jnp.int32, p.shape, 0)
        prev = jnp.where(row == 0, carry_ref[b], pltpu.roll(p, 1, 0))
        carry_ref[b] = p[nch * c - 1:nch * c, :]
        xs_rows.append(p + mu_ref[...] * (prev - p))
    xs = jnp.concatenate(xs_rows, axis=0)
    r_all = xs[:, 0:width]
    k_all = xs[:, width:2 * width]
    v_all = xs[:, 2 * width:3 * width]
    slab = xs[:, 3 * width:]
    nz = -(w0_ref[...] + _dot(jnp.tanh(slab).astype(BF16), w2_ref[...]))
    softplus = jnp.maximum(nz, 0.0) + jnp.log(1.0 + jnp.exp(-jnp.abs(nz)))
    ld_all = -jnp.exp(-softplus - 0.5)
    a_all = _sigmoid(a0_ref[...] + _dot(slab.astype(BF16), a2_ref[...]))
    g_all = _dot(_sigmoid(slab).astype(BF16), g2_ref[...])
    kk0 = k_all * kk_ref[...]
    kk_all = kk0 * lax.rsqrt(jnp.maximum(_head_sums(kk0 * kk0, hs), 1e-24))
    k2_all = k_all * (1.0 + (a_all - 1.0) * ka_ref[...])
    yield

    units = []
    for b, j in [(b, j) for b in range(nb) for j in range(nch)]:
        rb = slice((b * nch + j) * c, (b * nch + j + 1) * c)
        r, k2, v, ld = r_all[rb], k2_all[rb], v_all[rb], ld_all[rb]
        a_in = -kk_all[rb]
        b_in = kk_all[rb] * a_all[rb]
        cum = _dot_exact_lhs(tri, ld)
        cum_t = cum[c - 1:c, :]
        e_c = jnp.exp(cum)
        e_nc = jnp.exp(-cum)
        e_rem = jnp.exp(cum_t - cum)
        at_f = a_in * jnp.exp(cum - ld)
        rt_f = r * e_c
        kt_f = k2 * e_nc
        bt_f = b_in * e_nc
        kh_f = k2 * e_rem
        bh_f = b_in * e_rem
        p_t = jnp.exp(cum_t)
        for pi in range(n_pairs):
            ls = slice(pi * LANES, (pi + 1) * LANES)
            units.append(dict(
                b=b, j=j, pi=pi,
                at=stack(at_f[:, ls]).astype(BF16), rt=stack(rt_f[:, ls]).astype(BF16),
                kt=stack(kt_f[:, ls]).astype(BF16), bt=stack(bt_f[:, ls]).astype(BF16),
                kh=stack(kh_f[:, ls]).astype(BF16), bh=stack(bh_f[:, ls]).astype(BF16),
                vs=stack(v[:, ls]).astype(BF16), p_t=p_t[:, ls]))
        yield

    for u in units:
        lhs = jnp.concatenate([u['at'], u['rt']], axis=0)
        u['g'] = _dot(lhs, jnp.concatenate([u['kt'], u['bt']], axis=0), NT)
    yield
    for u in units:
        g = u.pop('g')
        u['a_ak'] = jnp.where(smask, g[:2 * c, :2 * c], 0.0).astype(BF16)
        u['pw'] = jnp.where(smask, g[:2 * c, 2 * c:], 0.0).astype(BF16)
        u['a_r'] = jnp.where(jnp.concatenate([imask, imask], axis=1), g[2 * c:], 0.0).astype(BF16)
    for u in units:
        akv = _dot(u.pop('a_ak'), u['vs'])
        u['x'] = jnp.concatenate([u['at'].astype(F32), akv], axis=1)
    yield
    n_lvl = int(np.log2(c))
    for lvl in range(n_lvl):
        for u in units:
            u['x'] = u['x'] + _dot(u['pw'], u['x'].astype(BF16))
        yield
        if lvl + 1 < n_lvl:
            for u in units:
                u['pw'] = _dot(u['pw'], u['pw']).astype(BF16)
            yield
    for u in units:
        x = u.pop('x')
        u['wr'] = jnp.concatenate([x[:, :LANES].astype(BF16), u['rt']], axis=0)
        u['u_loc'] = x[:, LANES:]
    zt = {(b, pi): zt_ref[b, pi] for b in range(nb) for pi in range(n_pairs)}
    for j in range(nch):
        tail = [u for u in units if u['j'] == j]
        for u in tail:
            u['uy'] = _dot(u.pop('wr'), zt[u['b'], u['pi']].astype(BF16), NT)
        yield
        for u in tail:
            uy = u.pop('uy')
            u['u'] = (uy[:2 * c] + u.pop('u_loc')).astype(BF16)
            u['y0'] = uy[2 * c:]
        for u in tail:
            vu = jnp.concatenate([u['vs'], u['u']], axis=0)
            u['y'] = u.pop('y0') + _dot(u['a_r'], vu)
            upd = _dot(vu, jnp.concatenate([u['kh'], u['bh']], axis=0), TN)
            zt[u['b'], u['pi']] = zt[u['b'], u['pi']] * u['p_t'] + upd
        yield
    for (b, pi), z in zt.items():
        zt_ref[b, pi] = z

    inv_n = 1.0 / RW_HEAD
    y = jnp.concatenate(
        [jnp.concatenate([u['y'][:c] + u['y'][c:] for u in units if (u['b'], u['j']) == (b, j)], axis=1)
         for b in range(nb) for j in range(nch)], axis=0)
    mean = _head_sums(y, hs) * inv_n
    yield
    d = y - mean
    var = _head_sums(d * d, hs) * inv_n
    yield
    yn = d * lax.rsqrt(var + RW_GN_EPS) * gnw_ref[...] + gnb_ref[...]
    bonus = _head_sums(r_all * k2_all * rk_ref[...], hs) * v_all
    out = ((yn + bonus) * g_all).astype(o_ref.dtype)
    for b in range(nb):
        o_ref[b] = out[b * nch * c:(b + 1) * nch * c]


def _rwkv7(rw, mu, w0, w2, a0, a2, g2, k_k, k_a, r_k, gn_w, gn_b, ts):
    bsz, s, cols = rw.shape
    width = w0.shape[-1]
    n_pairs = width // LANES
    slab = cols - 3 * width
    dl, al, gl = w2.shape[0], a2.shape[0], g2.shape[0]
    w2f = jnp.zeros((slab, width), F32).at[0:dl].set(w2).astype(BF16)
    a2f = jnp.zeros((slab, width), F32).at[dl:dl + al].set(a2).astype(BF16)
    g2f = jnp.zeros((slab, width), F32).at[dl + al:dl + al + gl].set(g2).astype(BF16)
    mup = jnp.zeros((1, cols), F32).at[0, :mu.shape[-1]].set(mu)
    tri, sm, im, hs = _rwkv_consts(width)
    row = lambda x: x.reshape(1, width)
    const = lambda i: (0, 0)
    vec = pl.BlockSpec((1, width), const)
    return pl.pallas_call(
        _rwkv7_kernel,
        out_shape=jax.ShapeDtypeStruct((bsz, s, width), BF16),
        grid=(s // ts,),
        in_specs=[pl.BlockSpec((bsz, ts, cols), lambda i: (0, i, 0)),
                  pl.BlockSpec((1, cols), const),
                  vec, vec, vec, vec, vec, vec, vec,
                  pl.BlockSpec((slab, width), const),
                  pl.BlockSpec((slab, width), const),
                  pl.BlockSpec((slab, width), const),
                  pl.BlockSpec(tri.shape, const),
                  pl.BlockSpec(sm.shape, const),
                  pl.BlockSpec(im.shape, const),
                  pl.BlockSpec(hs.shape, const)],
        out_specs=pl.BlockSpec((bsz, ts, width), lambda i: (0, i, 0)),
        scratch_shapes=[pltpu.VMEM((bsz, 1, cols), F32),
                        pltpu.VMEM((bsz, n_pairs, LANES, LANES), F32)],
        compiler_params=_cparams(("arbitrary",)),
        name="rwkv7",
    )(rw, mup, row(w0), row(a0), row(k_k), row(k_a), row(r_k), row(gn_w), row(gn_b),
      w2f, a2f, g2f, tri, sm, im, hs)


def _out_proj_kernel(n_groups, n_experts,
                     x_ref, oa_ref, ob_ref, ga_ref, gb_ref, gt1_ref, sc2_ref, sh2_ref, g2_ref,
                     wa_ref, wb_ref, wo_ref, wr_ref, wrl_ref, br_ref, upper_ref,
                     x1_ref, h2_ref, routet_ref, cnt_ref, carry_ref):
    first = (pl.program_id(0) == 0) & (pl.program_id(1) == 0)

    @pl.when(first)
    def _():
        carry_ref[...] = jnp.zeros_like(carry_ref)

    pa = _dot(oa_ref[0], wa_ref[...])
    pb = _dot(ob_ref[0], wb_ref[...])
    mixed = ga_ref[0].astype(F32) * pa + gb_ref[0].astype(F32) * pb
    x1 = x_ref[0] + gt1_ref[0] * _dot(mixed.astype(BF16), wo_ref[...])
    x1_ref[0] = x1
    ms = jnp.mean(x1 * x1, axis=-1, keepdims=True)
    h2 = (x1 * lax.rsqrt(ms + NORM_EPS) * g2_ref[...]) * (1.0 + sc2_ref[0]) + sh2_ref[0]
    _rows_to_tiles(h2_ref, h2)

    h2_hi = h2.astype(BF16)
    h2_lo = (h2 - h2_hi.astype(F32)).astype(BF16)
    logits = (_dot(wr_ref[...], h2_hi, NT) + _dot(wr_ref[...], h2_lo, NT) + _dot(wrl_ref[...], h2_hi, NT)
              + br_ref[...])
    row = lax.broadcasted_iota(jnp.int32, logits.shape, 0)
    neg = jnp.float32(-jnp.inf)
    big = jnp.int32(1 << 20)
    eg = n_experts // n_groups
    is_g = (row >= n_experts) & (row < n_experts + n_groups)
    lg = jnp.where(is_g, logits, neg)
    mg = jnp.max(lg, axis=0, keepdims=True)
    p_grp = 1.0 / jnp.sum(jnp.where(is_g, jnp.exp(lg - mg), 0.0), axis=0, keepdims=True)
    gidx = jnp.min(jnp.where(lg == mg, row, big), axis=0, keepdims=True) - n_experts
    sel = (row >= gidx * eg) & (row < gidx * eg + eg)
    le = jnp.where(sel, logits, neg)
    me = jnp.max(le, axis=0, keepdims=True)
    pe_un = jnp.where(sel, jnp.exp(le - me), 0.0)
    pe = jnp.where(sel, pe_un / jnp.sum(pe_un, axis=0, keepdims=True), -1.0)
    v1 = jnp.max(pe, axis=0, keepdims=True)
    i1 = jnp.min(jnp.where(pe == v1, row, big), axis=0, keepdims=True)
    pe2 = jnp.where(row == i1, -1.0, pe)
    v2 = jnp.max(pe2, axis=0, keepdims=True)
    i2 = jnp.min(jnp.where(pe2 == v2, row, big), axis=0, keepdims=True)
    wsum = v1 + v2
    w1 = p_grp * v1 / wsum
    w2 = p_grp * v2 / wsum

    oh1 = (row == i1).astype(F32)
    oh2 = (row == i2).astype(F32)
    both = oh1 + oh2
    before = _dot(both.astype(BF16), upper_ref[...]) + carry_ref[...]
    rank1 = jnp.sum(oh1 * before, axis=0, keepdims=True)
    rank2 = jnp.sum(oh2 * before, axis=0, keepdims=True)
    carry_ref[...] = carry_ref[...] + jnp.sum(both, axis=1, keepdims=True)
    cnt_ref[...] = carry_ref[...]
    zero = jnp.zeros_like(w1)
    routet_ref[...] = jnp.concatenate(
        [w1, w2, i1.astype(F32), i2.astype(F32), rank1, rank2, zero, zero], axis=0)


def _out_proj(x, o_a, o_b, gates, mod3, norm2_g, wa, wb, wo, wr, br, n_groups, n_experts, tm):
    bsz, s, d = x.shape
    wdt = o_a.shape[-1]
    upper = jnp.asarray(np.triu(np.ones((tm, tm), np.float32), 1), BF16)
    wrt = wr.T
    wr_hi = wrt.astype(BF16)
    wr_lo = (wrt - wr_hi.astype(F32)).astype(BF16)
    const = lambda b, i: (0, 0)
    tile = lambda b, i: (b, i, 0)
    kern = functools.partial(_out_proj_kernel, n_groups, n_experts)
    return pl.pallas_call(
        kern,
        out_shape=(jax.ShapeDtypeStruct((bsz, s, d), F32),
                   jax.ShapeDtypeStruct((bsz * s * SUB, LANES), U32),
                   jax.ShapeDtypeStruct((bsz * (s // tm) * SUBLANES, tm), F32),
                   jax.ShapeDtypeStruct((LANES, 1), F32)),
        grid=(bsz, s // tm),
        in_specs=[pl.BlockSpec((1, tm, d), tile),
                  pl.BlockSpec((1, tm, wdt), tile),
                  pl.BlockSpec((1, tm, wdt), tile),
                  pl.BlockSpec((1, tm, d), lambda b, i: (b, i, 0)),
                  pl.BlockSpec((1, tm, d), lambda b, i: (b, i, 1)),
                  pl.BlockSpec((1, 1, d), lambda b, i: (b * 6 + 2, 0, 0)),
                  pl.BlockSpec((1, 1, d), lambda b, i: (b * 6 + 4, 0, 0)),
                  pl.BlockSpec((1, 1, d), lambda b, i: (b * 6 + 3, 0, 0)),
                  pl.BlockSpec((1, d), const),
                  pl.BlockSpec(wa.shape, const),
                  pl.BlockSpec(wb.shape, const),
                  pl.BlockSpec(wo.shape, const),
                  pl.BlockSpec(wrt.shape, const),
                  pl.BlockSpec(wrt.shape, const),
                  pl.BlockSpec((LANES, 1), const),
                  pl.BlockSpec((tm, tm), const)],
        out_specs=(pl.BlockSpec((1, tm, d), tile),
                   pl.BlockSpec((tm * SUB, LANES), lambda b, i: (b * (s // tm) + i, 0)),
                   pl.BlockSpec((SUBLANES, tm), lambda b, i: (b * (s // tm) + i, 0)),
                   pl.BlockSpec((LANES, 1), const)),
        scratch_shapes=[pltpu.VMEM((LANES, 1), F32)],
        compiler_params=_cparams(("arbitrary", "arbitrary")),
        name="out_proj",
    )(x, o_a, o_b, gates, gates, mod3, mod3, mod3, norm2_g.reshape(1, d), wa, wb, wo, wr_hi, wr_lo,
      br.reshape(LANES, 1), upper)


def _moe_slotmap_kernel(tm_o, n_tok, blk, dest_ref, zstart_ref, zcnt_ref, nused_ref, smap_ref):
    i = pl.program_id(0)
    n_slots = smap_ref.shape[0] - blk
    n_experts = zcnt_ref.shape[0]

    @pl.when(i == 0)
    def _():
        unroll = 8

        def init(g, carry):
            for u in range(unroll):
                smap_ref[blk + g * unroll + u] = TOP_K * n_tok
            return carry
        lax.fori_loop(nused_ref[0] * (blk // unroll), n_slots // unroll, init, 0)
        for r in range(blk):
            smap_ref[r] = TOP_K * n_tok + n_experts * blk + r
        for e in range(n_experts):
            def pad(g, carry, e=e):
                for u in range(unroll):
                    j = jnp.maximum(zcnt_ref[e] - 1 - (g * unroll + u), 0)
                    smap_ref[blk + zstart_ref[e] + j] = TOP_K * n_tok + e * blk + j
                return carry
            lax.fori_loop(0, (zcnt_ref[e] + unroll - 1) // unroll, pad, 0)

    base = i * (TOP_K * tm_o)
    both = 1 + (1 << SMAP_BITS)
    for k in range(TOP_K):
        v0 = k * n_tok + (i * tm_o) * both
        for r in range(tm_o):
            smap_ref[dest_ref[base + k * tm_o + r]] = v0 + r * both


def _moe_slotmap(dest, zstart, zcnt, n_used, n_slots, blk, tm_o, n_tok):
    grid_spec = pltpu.PrefetchScalarGridSpec(
        num_scalar_prefetch=4,
        grid=(n_tok // tm_o,),
        in_specs=[],
        out_specs=pl.BlockSpec(memory_space=pltpu.SMEM),
    )
    n_experts = zcnt.shape[0]
    assert TOP_K * n_tok + (n_experts + 1) * blk <= (1 << SMAP_BITS) and n_tok <= (1 << (31 - SMAP_BITS))
    return pl.pallas_call(
        functools.partial(_moe_slotmap_kernel, tm_o, n_tok, blk),
        out_shape=jax.ShapeDtypeStruct((blk + n_slots,), jnp.int32),
        grid_spec=grid_spec,
        compiler_params=_cparams(("arbitrary",)),
        name="moe_slotmap",
    )(dest, zstart, zcnt, n_used)


def _moe_expert_gather_kernel(n_tok, blk_e_ref, nused_ref, smap_ref, h_ref, wg_ref, wu_ref, wd_ref, y_ref,
                              hv, xbuf, ystage, wgb, wub, wdb, sem, hsem):
    i = pl.program_id(0)
    nb = pl.num_programs(0)
    n_used = nused_ref[0]
    blk = xbuf.shape[1] // SUB
    first_real_blocks = TOP_K * n_tok // blk

    def gather(b, slot, rows):
        for r in rows:
            tok = lax.shift_right_logical(smap_ref[(b + 1) * blk + r], SMAP_BITS)
            src = pl.multiple_of(tok * SUB, SUB)
            xbuf[slot, pl.ds(r * SUB, SUB), :] = hv[pl.ds(src, SUB), :]

    def issue(b, slot, rows):
        for r in rows:
            t = smap_ref[(b + 1) * blk + r] & ((1 << SMAP_BITS) - 1)
            dst = pl.multiple_of(t * SUB, SUB)
            pltpu.make_async_copy(ystage.at[slot, pl.ds(r * SUB, SUB), :], y_ref.at[pl.ds(dst, SUB), :],
                                  sem.at[slot]).start(priority=r % 2)

    def wait_block(slot):
        pltpu.make_async_copy(ystage.at[slot], y_ref.at[pl.ds(0, blk * SUB), :], sem.at[slot]).wait()

    @pl.when(i == 0)
    def _():
        load = pltpu.make_async_copy(h_ref, hv, hsem)
        load.start()
        ystage[...] = jnp.zeros_like(ystage)
        n_spare_blocks = y_ref.shape[0] // (blk * SUB) - first_real_blocks

        def spare_copy(c):
            dst = (first_real_blocks + c) * blk * SUB
            return pltpu.make_async_copy(ystage.at[0], y_ref.at[pl.ds(dst, blk * SUB), :], sem.at[0])

        for c in range(n_spare_blocks):
            spare_copy(c).start()
        for c in range(n_spare_blocks):
            spare_copy(c).wait()
        load.wait()
        gather(0, 0, range(blk))

    new_expert = (i == 0) | (blk_e_ref[i] != blk_e_ref[jnp.maximum(i - 1, 0)])

    @pl.when((i < n_used) & new_expert)
    def _():
        wgb[...] = wg_ref[0].astype(BF16)
        wub[...] = wu_ref[0].astype(BF16)
        wdb[...] = wd_ref[0].astype(BF16)

    @pl.when(i < n_used)
    def _():
        slot = i % 2
        pslot = 1 - slot
        nxt = jnp.minimum(i + 1, nb - 1)

        @pl.when(i > 0)
        def _():
            wait_block(slot)

        q = blk // 4
        xb = _tiles_to_rows(xbuf.at[slot], blk).astype(BF16)
        issue(i - 1, pslot, range(0, q))
        gather(nxt, pslot, range(0, q))
        hg = _dot(xb, wgb[...])
        issue(i - 1, pslot, range(q, 2 * q))
        gather(nxt, pslot, range(q, 2 * q))
        hu = _dot(xb, wub[...])
        issue(i - 1, pslot, range(2 * q, 3 * q))
        gather(nxt, pslot, range(2 * q, 3 * q))
        hid = (_silu(hg) * hu).astype(BF16)
        y = _dot(hid, wdb[...])
        issue(i - 1, pslot, range(3 * q, blk))
        gather(nxt, pslot, range(3 * q, blk))
        _rows_to_tiles(ystage.at[slot], y)

        @pl.when(i == n_used - 1)
        def _():
            issue(i, slot, range(blk))
            wait_block(pslot)
            wait_block(slot)


def _moe_experts_gather(h2, smap, blk_e, n_used, w_gate, w_up, w_down, blk, n_tok):
    d = w_gate.shape[1]
    assert d == 2 * SUB * LANES and h2.shape == (n_tok * SUB, LANES)
    n_slots = smap.shape[0] - blk
    nb = n_slots // blk
    f = w_gate.shape[-1]
    n_experts = w_gate.shape[0]
    assert (TOP_K * n_tok) % blk == 0
    n_tiles = TOP_K * n_tok + (n_experts + 1) * blk
    grid_spec = pltpu.PrefetchScalarGridSpec(
        num_scalar_prefetch=3,
        grid=(nb,),
        in_specs=[pl.BlockSpec(memory_space=pl.ANY),
                  pl.BlockSpec((1, d, f), lambda i, be, nu, sm: (be[i], 0, 0)),
                  pl.BlockSpec((1, d, f), lambda i, be, nu, sm: (be[i], 0, 0)),
                  pl.BlockSpec((1, f, d), lambda i, be, nu, sm: (be[i], 0, 0))],
        out_specs=pl.BlockSpec(memory_space=pl.ANY),
        scratch_shapes=[pltpu.VMEM((n_tok * SUB, LANES), U32),
                        pltpu.VMEM((2, blk * SUB, LANES), U32),
                        pltpu.VMEM((2, blk * SUB, LANES), U32),
                        pltpu.VMEM((d, f), BF16),
                        pltpu.VMEM((d, f), BF16),
                        pltpu.VMEM((f, d), BF16),
                        pltpu.SemaphoreType.DMA((2,)),
                        pltpu.SemaphoreType.DMA],
    )
    return pl.pallas_call(
        functools.partial(_moe_expert_gather_kernel, n_tok),
        out_shape=jax.ShapeDtypeStruct((n_tiles * SUB, LANES), U32),
        grid_spec=grid_spec,
        compiler_params=_cparams(("arbitrary",)),
        name="moe_experts",
    )(blk_e, n_used, smap, h2, w_gate, w_up, w_down)


def _moe_combine_kernel(x1_ref, routet_ref, gt2_ref, fg_ref, y1_ref, y2_ref, o_ref):
    tm = x1_ref.shape[0]
    route = routet_ref[...].T
    moe = route[:, 0:1] * _tiles_to_rows(y1_ref, tm) + route[:, 1:2] * _tiles_to_rows(y2_ref, tm)
    xo = x1_ref[...] + gt2_ref[0] * moe
    ms = jnp.mean(xo * xo, axis=-1, keepdims=True)
    o_ref[...] = xo * lax.rsqrt(ms + NORM_EPS) * fg_ref[...]


def _moe_combine(x1, route_t, yt, mod3, final_g, s, tm):
    n, d = x1.shape
    tiles_per_batch = s // tm
    n_steps = n // tm
    per_o = route_t.shape[1] // tm
    return pl.pallas_call(
        _moe_combine_kernel,
        out_shape=jax.ShapeDtypeStruct((n, d), F32),
        grid=(n_steps,),
        in_specs=[pl.BlockSpec((tm, d), lambda i: (i, 0)),
                  pl.BlockSpec((SUBLANES, tm), lambda i: (i // per_o, i % per_o)),
                  pl.BlockSpec((1, 1, d), lambda i: ((i // tiles_per_batch) * 6 + 5, 0, 0)),
                  pl.BlockSpec((1, d), lambda i: (0, 0)),
                  pl.BlockSpec((tm * SUB, LANES), lambda i: (i, 0)),
                  pl.BlockSpec((tm * SUB, LANES), lambda i: (n_steps + i, 0))],
        out_specs=pl.BlockSpec((tm, d), lambda i: (i, 0)),
        compiler_params=_cparams(("arbitrary",)),
        name="moe_combine",
    )(x1, route_t, mod3, final_g.reshape(1, d), yt, yt)


def _pick(n, candidates):
    for t in candidates:
        if n % t == 0:
            return t
    raise ValueError(f"no tile in {candidates} divides {n}")


def kernel(x, c, ada_w, ada_b, norm1_g, w_in, hg_lb, hg_norm_g, rw_mu, rw_w0, rw_w2, rw_a0, rw_a2, rw_g2, rw_kk, rw_ka, rw_rk, rw_gn_w, rw_gn_b, w_proj_a, w_proj_b, w_out, norm2_g, router_g_w, router_g_b, router_e_w, router_e_b, exp_w_gate, exp_w_up, exp_w_down, final_g):
    bsz, s, d = x.shape
    depth = ada_w.shape[0]
    hg_f = hg_lb.shape[-1]
    hg_w = hg_norm_g.shape[-1]
    rw_w = rw_w0.shape[-1]
    rw_cols = rw_mu.shape[-1]
    n_groups = router_g_w.shape[-1]
    n_experts = router_e_w.shape[-1]
    assert hg_f == hg_w and s % CHUNK == 0 and n_experts + n_groups <= LANES and d == 2 * SUB * LANES

    lb_all = jnp.cumsum(jax.nn.softmax(hg_lb.astype(F32), axis=0), axis=0)
    n = bsz * s
    blk = 512
    n_blocks = (n * TOP_K + n_experts * blk) // blk
    for l in range(depth):
        mod = _ada_mod(c, ada_w[l], ada_b[l])
        mod3 = mod.reshape(bsz * 6, 1, d)

        hg_cols = 2 * hg_f + 2 * hg_w
        rw_pad = -(-rw_cols // 256) * 256
        wl = w_in[l]
        w_hg = wl[:, :hg_cols].astype(BF16)
        w_rw = jnp.zeros((d, rw_pad), BF16).at[:, :rw_cols].set(wl[:, hg_cols:hg_cols + rw_cols].astype(BF16))
        w_gt = wl[:, hg_cols + rw_cols:].astype(BF16)
        hg, rw, gates = _in_proj(x, mod3, norm1_g[l], lb_all[l], w_hg, w_rw, w_gt, _pick(s, (512, 256, 128, 64)))

        o_a = _hgrn2(hg, hg_norm_g[l], _pick(s, (1024, 512, 256, 128, 64)))
        o_b = _rwkv7(rw, rw_mu[l], rw_w0[l], rw_w2[l], rw_a0[l], rw_a2[l], rw_g2[l],
                     rw_kk[l], rw_ka[l], rw_rk[l].reshape(-1), rw_gn_w[l], rw_gn_b[l],
                     _pick(s, (2 * CHUNK, CHUNK)))

        wr = jnp.zeros((d, LANES), F32).at[:, :n_experts].set(router_e_w[l])
        wr = wr.at[:, n_experts:n_experts + n_groups].set(router_g_w[l])
        br = jnp.zeros((1, LANES), F32).at[0, :n_experts].set(router_e_b[l])
        br = br.at[0, n_experts:n_experts + n_groups].set(router_g_b[l])
        tm_o = _pick(s, (1024, 512, 256, 128, 64))
        x1, h2, route_t, counts = _out_proj(
            x, o_a, o_b, gates, mod3, norm2_g[l],
            w_proj_a[l].astype(BF16), w_proj_b[l].astype(BF16), w_out[l].astype(BF16),
            wr, br, n_groups, n_experts, tm_o)

        rt = route_t.reshape(n // tm_o, SUBLANES, tm_o)
        eid = rt[:, 2:4, :].astype(jnp.int32)
        rank = rt[:, 4:6, :].astype(jnp.int32)
        cnt = counts[:n_experts, 0].astype(jnp.int32)
        padded = (cnt + blk - 1) // blk * blk
        pad_end = jnp.cumsum(padded)
        pad_start = pad_end - padded
        e_ax = jnp.arange(n_experts, dtype=jnp.int32)[:, None, None, None]
        dest = rank + jnp.sum(jnp.where(eid[None] == e_ax, pad_start[:, None, None, None], 0), axis=0)
        blk_start = jnp.arange(n_blocks, dtype=jnp.int32) * blk
        blk_e = jnp.minimum(jnp.sum((pad_end[None, :] <= blk_start[:, None]).astype(jnp.int32), axis=1),
                            n_experts - 1)
        n_used = (pad_end[-1:] // blk).astype(jnp.int32)
        tm = tm_o
        smap = _moe_slotmap(dest.reshape(-1) + blk, pad_start + cnt, padded - cnt, n_used, n_blocks * blk, blk, tm_o, n)
        yt = _moe_experts_gather(h2, smap, blk_e, n_used, exp_w_gate[l], exp_w_up[l], exp_w_down[l], blk, n)
        last = l == depth - 1
        assert last, "the final RMSNorm is fused into the last layer's combine"
        out = _moe_combine(x1.reshape(n, d), route_t, yt, mod3, final_g, s, tm)
        x = out.reshape(bsz, s, d)
    return x
```

```python
import functools

import numpy as np
import jax
import jax.numpy as jnp
from jax import lax
from jax.experimental import pallas as pl
from jax.experimental.pallas import tpu as pltpu

F32 = jnp.float32
BF16 = jnp.bfloat16
HIGHEST = lax.Precision.HIGHEST

NORM_EPS = 1e-6
HG_HEAD = 128
RW_HEAD = 64
RW_GN_EPS = 64e-5
TOP_K = 2
CHUNK = 64
LANES = 128
SUB = 4
U32 = jnp.uint32
MXU_K = 256
SUBLANES = 8
SMAP_BITS = 16
VMEM_LIMIT = 56 * 1024 * 1024

NT = (((1,), (1,)), ((), ()))
TN = (((0,), (0,)), ((), ()))


def _dot(a, b, dims=None, precision=None):
    if dims is None:
        return jnp.dot(a, b, preferred_element_type=F32, precision=precision)
    return lax.dot_general(a, b, dims, preferred_element_type=F32, precision=precision)


def _split3(x):
    hi = x.astype(BF16)
    r1 = x - hi.astype(F32)
    mid = r1.astype(BF16)
    lo = (r1 - mid.astype(F32)).astype(BF16)
    return hi, mid, lo


def _dot_exact_lhs(m3_bf16, x):
    return _dot(m3_bf16, jnp.concatenate(_split3(x), axis=0))


def _head_sums(x, m2_bf16):
    outs = []
    for g0 in range(0, x.shape[1], MXU_K):
        xg = x[:, g0:g0 + MXU_K]
        hi = xg.astype(BF16)
        lo = (xg - hi.astype(F32)).astype(BF16)
        outs.append(_dot(jnp.concatenate([hi, lo], axis=1), m2_bf16))
    return jnp.concatenate(outs, axis=1)


def _sigmoid(x):
    return 1.0 / (1.0 + jnp.exp(-x))


def _silu(x):
    return x * _sigmoid(x)


def _rows_to_tiles(ref, val):
    m, half = val.shape[0], val.shape[1] // 2
    hi = lax.bitcast_convert_type(val[:, :half].astype(BF16).astype(F32), U32)
    lo = lax.bitcast_convert_type(val[:, half:].astype(BF16).astype(F32), U32)
    w = (hi & jnp.uint32(0xFFFF0000)) | (lo >> 16)
    for j in range(SUB):
        ref[pl.ds(j, m, stride=SUB), :] = w[:, j * LANES:(j + 1) * LANES]


def _tiles_to_rows(ref, m, base=0):
    w = jnp.concatenate([ref[pl.ds(base * SUB + j, m, stride=SUB), :] for j in range(SUB)], axis=1)
    hi = lax.bitcast_convert_type(w & jnp.uint32(0xFFFF0000), F32)
    lo = lax.bitcast_convert_type(w << 16, F32)
    return jnp.concatenate([hi, lo], axis=1)


def _cparams(sem):
    return pltpu.CompilerParams(dimension_semantics=sem, vmem_limit_bytes=VMEM_LIMIT)


def _ada_kernel(c_ref, w_ref, b_ref, o_ref):
    c = c_ref[...]
    o_ref[...] = _dot(_silu(c), w_ref[...], precision=HIGHEST) + b_ref[...]


def _ada_mod(c, w, b):
    bsz, d = c.shape
    n = w.shape[1]
    rows = 8
    cp = jnp.zeros((rows, d), F32).at[:bsz].set(c)
    tn = 1536
    out = pl.pallas_call(
        _ada_kernel,
        out_shape=jax.ShapeDtypeStruct((rows, n), F32),
        grid=(n // tn,),
        in_specs=[pl.BlockSpec((rows, d), lambda j: (0, 0)),
                  pl.BlockSpec((d, tn), lambda j: (0, j)),
                  pl.BlockSpec((1, tn), lambda j: (0, j))],
        out_specs=pl.BlockSpec((rows, tn), lambda j: (0, j)),
        compiler_params=_cparams(("arbitrary",)),
        name="ada_mod",
    )(cp, w, b.reshape(1, n))
    return out[:bsz]


def _in_proj_kernel(lora, x_ref, sh_ref, sc_ref, g_ref, lb_ref, mu_ref, whg_ref, wrw_ref, wgt_ref,
                    hg_ref, rw_ref, gt_ref, carry_ref):
    @pl.when(pl.program_id(1) == 0)
    def _():
        carry_ref[...] = jnp.zeros_like(carry_ref)

    x = x_ref[0]
    ms = jnp.mean(x * x, axis=-1, keepdims=True)
    h = (x * lax.rsqrt(ms + NORM_EPS) * g_ref[...]) * (1.0 + sc_ref[0]) + sh_ref[0]
    hb = h.astype(BF16)
    lb = lb_ref[...]
    w = lb.shape[1]
    hg_maps = (_silu, lambda t: lb + (1.0 - lb) * _sigmoid(t), lambda t: t, _silu)
    for part, fn in enumerate(hg_maps):
        hg_ref[0, :, part * w:(part + 1) * w] = fn(_dot(hb, whg_ref[:, part * w:(part + 1) * w]))
    step = 512
    n_rw = wrw_ref.shape[1]
    row = lax.broadcasted_iota(jnp.int32, (x.shape[0], 256), 0)
    lane = lax.broadcasted_iota(jnp.int32, (x.shape[0], 256), 1)
    dl, al, gl = lora
    for n0 in range(0, n_rw, 256):
        cs = slice(n0, n0 + 256)
        p = _dot(hb, wrw_ref[:, cs])
        prev = jnp.where(row == 0, carry_ref[:, cs], pltpu.roll(p, 1, 0))
        carry_ref[:, cs] = p[x.shape[0] - 1:, :]
        xs = p + mu_ref[:, cs] * (prev - p)
        if n0 == n_rw - 256:
            xs = jnp.where(lane < dl, jnp.tanh(xs),
                           jnp.where(lane < dl + al, xs, jnp.where(lane < dl + al + gl, _sigmoid(xs), 0.0)))
        rw_ref[0, :, cs] = xs
    for n0 in range(0, wgt_ref.shape[1], step):
        gt_ref[0, :, n0:n0 + step] = _sigmoid(_dot(hb, wgt_ref[:, n0:n0 + step])).astype(BF16)


def _in_proj(x, mod3, norm_g, lb, mu, lora, w_hg, w_rw, w_gt, tm):
    bsz, s, d = x.shape
    assert w_hg.shape[1] == 4 * lb.shape[0] and sum(lora) <= 256
    mup = jnp.zeros((1, w_rw.shape[1]), F32).at[0, :mu.shape[-1]].set(mu)
    n_hg, n_rw, n_gt = w_hg.shape[1], w_rw.shape[1], w_gt.shape[1]
    const = lambda b, i: (0, 0)
    return pl.pallas_call(
        functools.partial(_in_proj_kernel, lora),
        out_shape=(jax.ShapeDtypeStruct((bsz, s, n_hg), F32),
                   jax.ShapeDtypeStruct((bsz, s, n_rw), F32),
                   jax.ShapeDtypeStruct((bsz, s, n_gt), BF16)),
        grid=(bsz, s // tm),
        in_specs=[pl.BlockSpec((1, tm, d), lambda b, i: (b, i, 0)),
                  pl.BlockSpec((1, 1, d), lambda b, i: (b * 6 + 0, 0, 0)),
                  pl.BlockSpec((1, 1, d), lambda b, i: (b * 6 + 1, 0, 0)),
                  pl.BlockSpec((1, d), const),
                  pl.BlockSpec((1, lb.shape[0]), const),
                  pl.BlockSpec((1, n_rw), const),
                  pl.BlockSpec((d, n_hg), const),
                  pl.BlockSpec((d, n_rw), const),
                  pl.BlockSpec((d, n_gt), const)],
        out_specs=(pl.BlockSpec((1, tm, n_hg), lambda b, i: (b, i, 0)),
                   pl.BlockSpec((1, tm, n_rw), lambda b, i: (b, i, 0)),
                   pl.BlockSpec((1, tm, n_gt), lambda b, i: (b, i, 0))),
        scratch_shapes=[pltpu.VMEM((1, n_rw), F32)],
        compiler_params=_cparams(("arbitrary", "arbitrary")),
        name="in_proj",
    )(x, mod3, mod3, norm_g.reshape(1, d), lb.reshape(1, -1), mup, w_hg, w_rw, w_gt)


_HG_LEVELS = (32, 16, 8, 4, 2, 1)


def _hgrn2_consts(width):
    c = CHUNK
    t = np.arange(c)[:, None]
    s = np.arange(c)[None, :]
    blocks = [(s <= t), (s > t)]
    lvl_masks = []
    right = []
    for h in _HG_LEVELS:
        m = (t // (2 * h)) * 2 * h + h
        is_r = (t & h) != 0
        blk = np.where(is_r, (s >= m) & (s <= t), (s > t) & (s <= m - 1))
        blocks.append(blk)
        lvl_masks.append(is_r & ((s & h) == 0) & ((t // (2 * h)) == (s // (2 * h))))
        right.append(np.broadcast_to(is_r, (c, width)))
    mst = np.tile(np.concatenate(blocks, axis=0).astype(np.float32), (1, 3))
    lm = np.stack([np.eye(c, dtype=bool)] + lvl_masks).astype(np.float32)
    rm = np.stack(right).astype(np.float32)
    return jnp.asarray(mst, BF16), jnp.asarray(lm, F32), jnp.asarray(rm, F32)


def _hgrn2_kernel(q_ref, f_ref, i_ref, g_ref, ng_ref, mst_ref, lm_ref, rm_ref, o_ref, st_ref):
    c = CHUNK
    n_chunks = q_ref.shape[1] // c

    @pl.when(pl.program_id(1) == 0)
    def _():
        st_ref[...] = jnp.zeros_like(st_ref)

    nsub = next(n for n in (4, 2, 1) if n_chunks % n == 0)

    def chunk_body(ci, carry):
        r0 = pl.multiple_of(ci * (nsub * c), nsub * c)
        for _ in _hgrn2_steps(q_ref, f_ref, i_ref, g_ref, ng_ref, mst_ref, lm_ref, rm_ref, o_ref, st_ref,
                              0, r0, nsub):
            pass
        return carry

    lax.fori_loop(0, n_chunks // nsub, chunk_body, 0)


def _hgrn2_steps(q_ref, f_ref, i_ref, g_ref, ng_ref, mst_ref, lm_ref, rm_ref, o_ref, st_ref, b, r0, nsub):
    c = CHUNK
    n_heads = q_ref.shape[2] // HG_HEAD
    mst = mst_ref[...]
    ng = ng_ref[...]
    heads = [slice(hd * HG_HEAD, (hd + 1) * HG_HEAD) for hd in range(n_heads)]
    subs = []
    for j in range(nsub):
        rows = pl.ds(r0 + j * c, c)
        q = q_ref[b, rows, :]
        f = f_ref[b, rows, :]
        k = 1.0 - f
        ex = jnp.exp2(_dot_exact_lhs(mst, jnp.log2(f)))
        subs.append(dict(rows=rows, q=q, k=k, ex=ex, vb=i_ref[b, rows, :].astype(BF16),
                         qd=(q * ex[0:c]).astype(BF16), kr=(k * ex[c:2 * c]).astype(BF16)))
        yield
    for sb in subs:
        qb, kb = sb['q'].astype(BF16), sb['k'].astype(BF16)
        sb['sc'] = [lm_ref[0] * _dot(qb[:, ls], kb[:, ls], NT) for ls in heads]
        sb['dqk'] = sb['q'] - sb['k']
    yield
    for li, h in enumerate(_HG_LEVELS):
        for sb in subs:
            if h % SUBLANES == 0:
                qk = jnp.concatenate([(sb['q'] if m % 2 else sb['k'])[m * h:(m + 1) * h] for m in range(c // h)],
                                     axis=0)
            else:
                qk = sb['k'] + rm_ref[li] * sb['dqk']
            g_l = (qk * sb['ex'][(2 + li) * c:(3 + li) * c]).astype(BF16)
            sb['sc'] = [s_h + lm_ref[li + 1] * _dot(g_l[:, ls], g_l[:, ls], NT)
                        for s_h, ls in zip(sb['sc'], heads)]
        yield
    for sb in subs:
        sb['kv'] = [_dot(sb['vb'][:, ls], sb['kr'][:, ls], TN) for ls in heads]
        sb['o'] = [_dot(s_h.astype(BF16), sb['vb'][:, ls]) for s_h, ls in zip(sb['sc'], heads)]
        yield
    sts = [st_ref[b, hd] for hd in range(n_heads)]
    for sb in subs:
        sb['o'] = [o_h + _dot(sb['qd'][:, ls], st.astype(BF16), NT) for o_h, ls, st in zip(sb['o'], heads, sts)]
        sts = [st * sb['ex'][c - 1:c, ls] + kv for st, ls, kv in zip(sts, heads, sb['kv'])]
        yield
    for hd in range(n_heads):
        st_ref[b, hd] = sts[hd]
    for sb in subs:
        on = [o_h * lax.rsqrt(jnp.mean(o_h * o_h, axis=-1, keepdims=True) + NORM_EPS) for o_h in sb['o']]
        o_full = jnp.concatenate(on, axis=1) * ng
        o_ref[b, sb['rows'], :] = (o_full * g_ref[b, sb['rows'], :]).astype(o_ref.dtype)
        yield


def _hgrn2(hg, norm_g, ts):
    bsz, s, n4 = hg.shape
    w = n4 // 4
    mst, lm, rm = _hgrn2_consts(w)
    n_heads = w // HG_HEAD
    const2 = lambda b, i: (0, 0)
    const3 = lambda b, i: (0, 0, 0)
    return pl.pallas_call(
        _hgrn2_kernel,
        out_shape=jax.ShapeDtypeStruct((bsz, s, w), BF16),
        grid=(bsz, s // ts),
        in_specs=[pl.BlockSpec((1, ts, w), lambda b, i: (b, i, 0)),
                  pl.BlockSpec((1, ts, w), lambda b, i: (b, i, 1)),
                  pl.BlockSpec((1, ts, w), lambda b, i: (b, i, 2)),
                  pl.BlockSpec((1, ts, w), lambda b, i: (b, i, 3)),
                  pl.BlockSpec((1, w), const2),
                  pl.BlockSpec(mst.shape, const2),
                  pl.BlockSpec(lm.shape, const3),
                  pl.BlockSpec(rm.shape, const3)],
        out_specs=pl.BlockSpec((1, ts, w), lambda b, i: (b, i, 0)),
        scratch_shapes=[pltpu.VMEM((1, n_heads, HG_HEAD, HG_HEAD), F32)],
        compiler_params=_cparams(("arbitrary", "arbitrary")),
        name="hgrn2",
    )(hg, hg, hg, hg, norm_g.reshape(1, w), mst, lm, rm)


def _rwkv_consts(width):
    c = CHUNK
    t = np.arange(c)[:, None]
    s = np.arange(c)[None, :]
    tri = np.tile((s <= t).astype(np.float32), (1, 3))
    tt = np.arange(2 * c)[:, None]
    ss = np.arange(2 * c)[None, :]
    same = (tt // c) == (ss // c)
    strict = same & ((ss % c) < (tt % c))
    incl = same & ((ss % c) <= (tt % c))
    hsum = (np.arange(MXU_K)[:, None] // RW_HEAD) == (np.arange(MXU_K)[None, :] // RW_HEAD)
    hsum = np.tile(hsum, (2, 1))
    return (jnp.asarray(tri, BF16), jnp.asarray(strict.astype(np.float32), F32),
            jnp.asarray(incl.astype(np.float32), F32), jnp.asarray(hsum.astype(np.float32), BF16))


def _rwkv7_kernel(p_ref, w0_ref, a0_ref, kk_ref, ka_ref, rk_ref, gnw_ref, gnb_ref,
                  w2_ref, a2_ref, g2_ref, tri_ref, sm_ref, im_ref, hs_ref,
                  o_ref, zt_ref):
    @pl.when(pl.program_id(0) == 0)
    def _():
        zt_ref[...] = jnp.zeros_like(zt_ref)

    for _ in _rwkv7_steps(p_ref, w0_ref, a0_ref, kk_ref, ka_ref, rk_ref, gnw_ref, gnb_ref,
                          w2_ref, a2_ref, g2_ref, tri_ref, sm_ref, im_ref, hs_ref, o_ref, zt_ref):
        pass


def _rwkv7_steps(p_ref, w0_ref, a0_ref, kk_ref, ka_ref, rk_ref, gnw_ref, gnb_ref,
                 w2_ref, a2_ref, g2_ref, tri_ref, sm_ref, im_ref, hs_ref, o_ref, zt_ref):
    c = CHUNK
    nb = p_ref.shape[0]
    nch = p_ref.shape[1] // c
    width = o_ref.shape[2]
    n_pairs = width // LANES

    hs = hs_ref[...]
    tri = tri_ref[...]
    smask = sm_ref[...] > 0
    imask = im_ref[...] > 0
    lane = lax.broadcasted_iota(jnp.int32, (c, LANES), 1)
    m0 = (lane < RW_HEAD).astype(F32)
    m1 = 1.0 - m0

    def stack(x):
        return jnp.concatenate([x * m0, x * m1], axis=0)

    xs = jnp.concatenate([p_ref[b] for b in range(nb)], axis=0)
    r_all = xs[:, 0:width]
    k_all = xs[:, width:2 * width]
    v_all = xs[:, 2 * width:3 * width]
    slab = xs[:, 3 * width:].astype(BF16)
    nz = -(w0_ref[...] + _dot(slab, w2_ref[...]))
    softplus = jnp.maximum(nz, 0.0) + jnp.log(1.0 + jnp.exp(-jnp.abs(nz)))
    ld_all = -jnp.exp(-softplus - 0.5)
    a_all = _sigmoid(a0_ref[...] + _dot(slab, a2_ref[...]))
    g_all = _dot(slab, g2_ref[...])
    kk0 = k_all * kk_ref[...]
    kk_all = kk0 * lax.rsqrt(jnp.maximum(_head_sums(kk0 * kk0, hs), 1e-24))
    k2_all = k_all * (1.0 + (a_all - 1.0) * ka_ref[...])
    yield

    units = []
    for b, j in [(b, j) for b in range(nb) for j in range(nch)]:
        rb = slice((b * nch + j) * c, (b * nch + j + 1) * c)
        r, k2, v, ld = r_all[rb], k2_all[rb], v_all[rb], ld_all[rb]
        a_in = -kk_all[rb]
        b_in = kk_all[rb] * a_all[rb]
        cum = _dot_exact_lhs(tri, ld)
        cum_t = cum[c - 1:c, :]
        e_c = jnp.exp(cum)
        e_nc = jnp.exp(-cum)
        e_rem = jnp.exp(cum_t - cum)
        at_f = a_in * jnp.exp(cum - ld)
        rt_f = r * e_c
        kt_f = k2 * e_nc
        bt_f = b_in * e_nc
        kh_f = k2 * e_rem
        bh_f = b_in * e_rem
        p_t = jnp.exp(cum_t)
        for pi in range(n_pairs):
            ls = slice(pi * LANES, (pi + 1) * LANES)
            units.append(dict(
                b=b, j=j, pi=pi,
                at=stack(at_f[:, ls]).astype(BF16), rt=stack(rt_f[:, ls]).astype(BF16),
                kt=stack(kt_f[:, ls]).astype(BF16), bt=stack(bt_f[:, ls]).astype(BF16),
                kh=stack(kh_f[:, ls]).astype(BF16), bh=stack(bh_f[:, ls]).astype(BF16),
                vs=stack(v[:, ls]).astype(BF16), p_t=p_t[:, ls]))
        yield

    for u in units:
        lhs = jnp.concatenate([u['at'], u['rt']], axis=0)
        u['g'] = _dot(lhs, jnp.concatenate([u['kt'], u['bt']], axis=0), NT)
    yield
    for u in units:
        g = u.pop('g')
        u['a_ak'] = jnp.where(smask, g[:2 * c, :2 * c], 0.0).astype(BF16)
        u['pw'] = jnp.where(smask, g[:2 * c, 2 * c:], 0.0).astype(BF16)
        u['a_r'] = jnp.where(jnp.concatenate([imask, imask], axis=1), g[2 * c:], 0.0).astype(BF16)
    for u in units:
        akv = _dot(u.pop('a_ak'), u['vs'])
        u['x'] = jnp.concatenate([u['at'].astype(F32), akv], axis=1)
    yield
    n_lvl = int(np.log2(c))
    for lvl in range(n_lvl):
        for u in units:
            u['x'] = u['x'] + _dot(u['pw'], u['x'].astype(BF16))
        yield
        if lvl + 1 < n_lvl:
            for u in units:
                u['pw'] = _dot(u['pw'], u['pw']).astype(BF16)
            yield
    for u in units:
        x = u.pop('x')
        u['wr'] = jnp.concatenate([x[:, :LANES].astype(BF16), u['rt']], axis=0)
        u['u_loc'] = x[:, LANES:]
    zt = {(b, pi): zt_ref[b, pi] for b in range(nb) for pi in range(n_pairs)}
    for j in range(nch):
        tail = [u for u in units if u['j'] == j]
        for u in tail:
            u['uy'] = _dot(u.pop('wr'), zt[u['b'], u['pi']].astype(BF16), NT)
        yield
        for u in tail:
            uy = u.pop('uy')
            u['u'] = (uy[:2 * c] + u.pop('u_loc')).astype(BF16)
            u['y0'] = uy[2 * c:]
        for u in tail:
            vu = jnp.concatenate([u['vs'], u['u']], axis=0)
            u['y'] = u.pop('y0') + _dot(u['a_r'], vu)
            upd = _dot(vu, jnp.concatenate([u['kh'], u['bh']], axis=0), TN)
            zt[u['b'], u['pi']] = zt[u['b'], u['pi']] * u['p_t'] + upd
        yield
    for (b, pi), z in zt.items():
        zt_ref[b, pi] = z

    inv_n = 1.0 / RW_HEAD
    y = jnp.concatenate(
        [jnp.concatenate([u['y'][:c] + u['y'][c:] for u in units if (u['b'], u['j']) == (b, j)], axis=1)
         for b in range(nb) for j in range(nch)], axis=0)
    mean = _head_sums(y, hs) * inv_n
    yield
    d = y - mean
    var = _head_sums(d * d, hs) * inv_n
    yield
    yn = d * lax.rsqrt(var + RW_GN_EPS) * gnw_ref[...] + gnb_ref[...]
    bonus = _head_sums(r_all * k2_all * rk_ref[...], hs) * v_all
    out = ((yn + bonus) * g_all).astype(o_ref.dtype)
    for b in range(nb):
        o_ref[b] = out[b * nch * c:(b + 1) * nch * c]


def _rwkv7(rw, w0, w2, a0, a2, g2, k_k, k_a, r_k, gn_w, gn_b, ts):
    bsz, s, cols = rw.shape
    width = w0.shape[-1]
    n_pairs = width // LANES
    slab = cols - 3 * width
    dl, al, gl = w2.shape[0], a2.shape[0], g2.shape[0]
    w2f = jnp.zeros((slab, width), F32).at[0:dl].set(w2).astype(BF16)
    a2f = jnp.zeros((slab, width), F32).at[dl:dl + al].set(a2).astype(BF16)
    g2f = jnp.zeros((slab, width), F32).at[dl + al:dl + al + gl].set(g2).astype(BF16)
    tri, sm, im, hs = _rwkv_consts(width)
    row = lambda x: x.reshape(1, width)
    const = lambda i: (0, 0)
    vec = pl.BlockSpec((1, width), const)
    return pl.pallas_call(
        _rwkv7_kernel,
        out_shape=jax.ShapeDtypeStruct((bsz, s, width), BF16),
        grid=(s // ts,),
        in_specs=[pl.BlockSpec((bsz, ts, cols), lambda i: (0, i, 0)),
                  vec, vec, vec, vec, vec, vec, vec,
                  pl.BlockSpec((slab, width), const),
                  pl.BlockSpec((slab, width), const),
                  pl.BlockSpec((slab, width), const),
                  pl.BlockSpec(tri.shape, const),
                  pl.BlockSpec(sm.shape, const),
                  pl.BlockSpec(im.shape, const),
                  pl.BlockSpec(hs.shape, const)],
        out_specs=pl.BlockSpec((bsz, ts, width), lambda i: (0, i, 0)),
        scratch_shapes=[pltpu.VMEM((bsz, n_pairs, LANES, LANES), F32)],
        compiler_params=_cparams(("arbitrary",)),
        name="rwkv7",
    )(rw, row(w0), row(a0), row(k_k), row(k_a), row(r_k), row(gn_w), row(gn_b),
      w2f, a2f, g2f, tri, sm, im, hs)


def _out_proj_kernel(n_groups, n_experts,
                     x_ref, oa_ref, ob_ref, ga_ref, gb_ref, gt1_ref, sc2_ref, sh2_ref, g2_ref,
                     wa_ref, wb_ref, wo_ref, wr_ref, wrl_ref, br_ref, upper_ref,
                     x1_ref, h2_ref, routet_ref, cnt_ref, carry_ref):
    first = (pl.program_id(0) == 0) & (pl.program_id(1) == 0)

    @pl.when(first)
    def _():
        carry_ref[...] = jnp.zeros_like(carry_ref)

    pa = _dot(oa_ref[0], wa_ref[...])
    pb = _dot(ob_ref[0], wb_ref[...])
    mixed = ga_ref[0].astype(F32) * pa + gb_ref[0].astype(F32) * pb
    x1 = x_ref[0] + gt1_ref[0] * _dot(mixed.astype(BF16), wo_ref[...])
    x1_ref[0] = x1
    ms = jnp.mean(x1 * x1, axis=-1, keepdims=True)
    h2 = (x1 * lax.rsqrt(ms + NORM_EPS) * g2_ref[...]) * (1.0 + sc2_ref[0]) + sh2_ref[0]
    _rows_to_tiles(h2_ref, h2)

    h2_hi = h2.astype(BF16)
    h2_lo = (h2 - h2_hi.astype(F32)).astype(BF16)
    logits = (_dot(wr_ref[...], h2_hi, NT) + _dot(wr_ref[...], h2_lo, NT) + _dot(wrl_ref[...], h2_hi, NT)
              + br_ref[...])
    row = lax.broadcasted_iota(jnp.int32, logits.shape, 0)
    neg = jnp.float32(-jnp.inf)
    big = jnp.int32(1 << 20)
    eg = n_experts // n_groups
    is_g = (row >= n_experts) & (row < n_experts + n_groups)
    lg = jnp.where(is_g, logits, neg)
    mg = jnp.max(lg, axis=0, keepdims=True)
    p_grp = 1.0 / jnp.sum(jnp.where(is_g, jnp.exp(lg - mg), 0.0), axis=0, keepdims=True)
    gidx = jnp.min(jnp.where(lg == mg, row, big), axis=0, keepdims=True) - n_experts
    sel = (row >= gidx * eg) & (row < gidx * eg + eg)
    le = jnp.where(sel, logits, neg)
    me = jnp.max(le, axis=0, keepdims=True)
    pe_un = jnp.where(sel, jnp.exp(le - me), 0.0)
    pe = jnp.where(sel, pe_un / jnp.sum(pe_un, axis=0, keepdims=True), -1.0)
    v1 = jnp.max(pe, axis=0, keepdims=True)
    i1 = jnp.min(jnp.where(pe == v1, row, big), axis=0, keepdims=True)
    pe2 = jnp.where(row == i1, -1.0, pe)
    v2 = jnp.max(pe2, axis=0, keepdims=True)
    i2 = jnp.min(jnp.where(pe2 == v2, row, big), axis=0, keepdims=True)
    wsum = v1 + v2
    w1 = p_grp * v1 / wsum
    w2 = p_grp * v2 / wsum

    oh1 = (row == i1).astype(F32)
    oh2 = (row == i2).astype(F32)
    both = oh1 + oh2
    before = _dot(both.astype(BF16), upper_ref[...]) + carry_ref[...]
    rank1 = jnp.sum(oh1 * before, axis=0, keepdims=True)
    rank2 = jnp.sum(oh2 * before, axis=0, keepdims=True)
    carry_ref[...] = carry_ref[...] + jnp.sum(both, axis=1, keepdims=True)
    cnt_ref[...] = carry_ref[...]
    zero = jnp.zeros_like(w1)
    routet_ref[...] = jnp.concatenate(
        [w1, w2, i1.astype(F32), i2.astype(F32), rank1, rank2, zero, zero], axis=0)


def _out_proj(x, o_a, o_b, gates, mod3, norm2_g, wa, wb, wo, wr, br, n_groups, n_experts, tm):
    bsz, s, d = x.shape
    wdt = o_a.shape[-1]
    upper = jnp.asarray(np.triu(np.ones((tm, tm), np.float32), 1), BF16)
    wrt = wr.T
    wr_hi = wrt.astype(BF16)
    wr_lo = (wrt - wr_hi.astype(F32)).astype(BF16)
    const = lambda b, i: (0, 0)
    tile = lambda b, i: (b, i, 0)
    kern = functools.partial(_out_proj_kernel, n_groups, n_experts)
    return pl.pallas_call(
        kern,
        out_shape=(jax.ShapeDtypeStruct((bsz, s, d), F32),
                   jax.ShapeDtypeStruct((bsz * s * SUB, LANES), U32),
                   jax.ShapeDtypeStruct((bsz * (s // tm) * SUBLANES, tm), F32),
                   jax.ShapeDtypeStruct((LANES, 1), F32)),
        grid=(bsz, s // tm),
        in_specs=[pl.BlockSpec((1, tm, d), tile),
                  pl.BlockSpec((1, tm, wdt), tile),
                  pl.BlockSpec((1, tm, wdt), tile),
                  pl.BlockSpec((1, tm, d), lambda b, i: (b, i, 0)),
                  pl.BlockSpec((1, tm, d), lambda b, i: (b, i, 1)),
                  pl.BlockSpec((1, 1, d), lambda b, i: (b * 6 + 2, 0, 0)),
                  pl.BlockSpec((1, 1, d), lambda b, i: (b * 6 + 4, 0, 0)),
                  pl.BlockSpec((1, 1, d), lambda b, i: (b * 6 + 3, 0, 0)),
                  pl.BlockSpec((1, d), const),
                  pl.BlockSpec(wa.shape, const),
                  pl.BlockSpec(wb.shape, const),
                  pl.BlockSpec(wo.shape, const),
                  pl.BlockSpec(wrt.shape, const),
                  pl.BlockSpec(wrt.shape, const),
                  pl.BlockSpec((LANES, 1), const),
                  pl.BlockSpec((tm, tm), const)],
        out_specs=(pl.BlockSpec((1, tm, d), tile),
                   pl.BlockSpec((tm * SUB, LANES), lambda b, i: (b * (s // tm) + i, 0)),
                   pl.BlockSpec((SUBLANES, tm), lambda b, i: (b * (s // tm) + i, 0)),
                   pl.BlockSpec((LANES, 1), const)),
        scratch_shapes=[pltpu.VMEM((LANES, 1), F32)],
        compiler_params=_cparams(("arbitrary", "arbitrary")),
        name="out_proj",
    )(x, o_a, o_b, gates, gates, mod3, mod3, mod3, norm2_g.reshape(1, d), wa, wb, wo, wr_hi, wr_lo,
      br.reshape(LANES, 1), upper)


def _moe_slotmap_kernel(tm_o, n_tok, blk, dest_ref, zstart_ref, zcnt_ref, nused_ref, smap_ref):
    i = pl.program_id(0)
    n_slots = smap_ref.shape[0] - blk
    n_experts = zcnt_ref.shape[0]

    @pl.when(i == 0)
    def _():
        unroll = 8

        def init(g, carry):
            for u in range(unroll):
                smap_ref[blk + g * unroll + u] = TOP_K * n_tok
            return carry
        lax.fori_loop(nused_ref[0] * (blk // unroll), n_slots // unroll, init, 0)
        for r in range(blk):
            smap_ref[r] = TOP_K * n_tok + n_experts * blk + r
        for e in range(n_experts):
            def pad(g, carry, e=e):
                for u in range(unroll):
                    j = jnp.maximum(zcnt_ref[e] - 1 - (g * unroll + u), 0)
                    smap_ref[blk + zstart_ref[e] + j] = TOP_K * n_tok + e * blk + j
                return carry
            lax.fori_loop(0, (zcnt_ref[e] + unroll - 1) // unroll, pad, 0)

    base = i * (TOP_K * tm_o)
    both = 1 + (1 << SMAP_BITS)
    for k in range(TOP_K):
        v0 = k * n_tok + (i * tm_o) * both
        for r in range(tm_o):
            smap_ref[dest_ref[base + k * tm_o + r]] = v0 + r * both


def _moe_slotmap(dest, zstart, zcnt, n_used, n_slots, blk, tm_o, n_tok):
    grid_spec = pltpu.PrefetchScalarGridSpec(
        num_scalar_prefetch=4,
        grid=(n_tok // tm_o,),
        in_specs=[],
        out_specs=pl.BlockSpec(memory_space=pltpu.SMEM),
    )
    n_experts = zcnt.shape[0]
    assert TOP_K * n_tok + (n_experts + 1) * blk <= (1 << SMAP_BITS) and n_tok <= (1 << (31 - SMAP_BITS))
    return pl.pallas_call(
        functools.partial(_moe_slotmap_kernel, tm_o, n_tok, blk),
        out_shape=jax.ShapeDtypeStruct((blk + n_slots,), jnp.int32),
        grid_spec=grid_spec,
        compiler_params=_cparams(("arbitrary",)),
        name="moe_slotmap",
    )(dest, zstart, zcnt, n_used)


def _moe_expert_gather_kernel(n_tok, blk_e_ref, nused_ref, smap_ref, h_ref, wg_ref, wu_ref, wd_ref, y_ref,
                              hv, xbuf, ystage, wgb, wub, wdb, sem, hsem):
    i = pl.program_id(0)
    nb = pl.num_programs(0)
    n_used = nused_ref[0]
    blk = xbuf.shape[1] // SUB
    first_real_blocks = TOP_K * n_tok // blk

    def gather(b, slot, rows):
        for r in rows:
            tok = lax.shift_right_logical(smap_ref[(b + 1) * blk + r], SMAP_BITS)
            src = pl.multiple_of(tok * SUB, SUB)
            xbuf[slot, pl.ds(r * SUB, SUB), :] = hv[pl.ds(src, SUB), :]

    def issue(b, slot, rows):
        for r in rows:
            t = smap_ref[(b + 1) * blk + r] & ((1 << SMAP_BITS) - 1)
            dst = pl.multiple_of(t * SUB, SUB)
            pltpu.make_async_copy(ystage.at[slot, pl.ds(r * SUB, SUB), :], y_ref.at[pl.ds(dst, SUB), :],
                                  sem.at[slot]).start(priority=r % 2)

    def wait_block(slot):
        pltpu.make_async_copy(ystage.at[slot], y_ref.at[pl.ds(0, blk * SUB), :], sem.at[slot]).wait()

    @pl.when(i == 0)
    def _():
        load = pltpu.make_async_copy(h_ref, hv, hsem)
        load.start()
        ystage[...] = jnp.zeros_like(ystage)
        n_spare_blocks = y_ref.shape[0] // (blk * SUB) - first_real_blocks

        def spare_copy(c):
            dst = (first_real_blocks + c) * blk * SUB
            return pltpu.make_async_copy(ystage.at[0], y_ref.at[pl.ds(dst, blk * SUB), :], sem.at[0])

        for c in range(n_spare_blocks):
            spare_copy(c).start()
        for c in range(n_spare_blocks):
            spare_copy(c).wait()
        load.wait()
        gather(0, 0, range(blk))

    new_expert = (i == 0) | (blk_e_ref[i] != blk_e_ref[jnp.maximum(i - 1, 0)])

    @pl.when((i < n_used) & new_expert)
    def _():
        wgb[...] = wg_ref[0].astype(BF16)
        wub[...] = wu_ref[0].astype(BF16)
        wdb[...] = wd_ref[0].astype(BF16)

    @pl.when(i < n_used)
    def _():
        slot = i % 2
        pslot = 1 - slot
        nxt = jnp.minimum(i + 1, nb - 1)

        @pl.when(i > 0)
        def _():
            wait_block(slot)

        q = blk // 4
        xb = _tiles_to_rows(xbuf.at[slot], blk).astype(BF16)
        issue(i - 1, pslot, range(0, q))
        gather(nxt, pslot, range(0, q))
        hg = _dot(xb, wgb[...])
        issue(i - 1, pslot, range(q, 2 * q))
        gather(nxt, pslot, range(q, 2 * q))
        hu = _dot(xb, wub[...])
        issue(i - 1, pslot, range(2 * q, 3 * q))
        gather(nxt, pslot, range(2 * q, 3 * q))
        hid = (_silu(hg) * hu).astype(BF16)
        y = _dot(hid, wdb[...])
        issue(i - 1, pslot, range(3 * q, blk))
        gather(nxt, pslot, range(3 * q, blk))
        _rows_to_tiles(ystage.at[slot], y)

        @pl.when(i == n_used - 1)
        def _():
            issue(i, slot, range(blk))
            wait_block(pslot)
            wait_block(slot)


def _moe_experts_gather(h2, smap, blk_e, n_used, w_gate, w_up, w_down, blk, n_tok):
    d = w_gate.shape[1]
    assert d == 2 * SUB * LANES and h2.shape == (n_tok * SUB, LANES)
    n_slots = smap.shape[0] - blk
    nb = n_slots // blk
    f = w_gate.shape[-1]
    n_experts = w_gate.shape[0]
    assert (TOP_K * n_tok) % blk == 0
    n_tiles = TOP_K * n_tok + (n_experts + 1) * blk
    grid_spec = pltpu.PrefetchScalarGridSpec(
        num_scalar_prefetch=3,
        grid=(nb,),
        in_specs=[pl.BlockSpec(memory_space=pl.ANY),
                  pl.BlockSpec((1, d, f), lambda i, be, nu, sm: (be[i], 0, 0)),
                  pl.BlockSpec((1, d, f), lambda i, be, nu, sm: (be[i], 0, 0)),
                  pl.BlockSpec((1, f, d), lambda i, be, nu, sm: (be[i], 0, 0))],
        out_specs=pl.BlockSpec(memory_space=pl.ANY),
        scratch_shapes=[pltpu.VMEM((n_tok * SUB, LANES), U32),
                        pltpu.VMEM((2, blk * SUB, LANES), U32),
                        pltpu.VMEM((2, blk * SUB, LANES), U32),
                        pltpu.VMEM((d, f), BF16),
                        pltpu.VMEM((d, f), BF16),
                        pltpu.VMEM((f, d), BF16),
                        pltpu.SemaphoreType.DMA((2,)),
                        pltpu.SemaphoreType.DMA],
    )
    return pl.pallas_call(
        functools.partial(_moe_expert_gather_kernel, n_tok),
        out_shape=jax.ShapeDtypeStruct((n_tiles * SUB, LANES), U32),
        grid_spec=grid_spec,
        compiler_params=_cparams(("arbitrary",)),
        name="moe_experts",
    )(blk_e, n_used, smap, h2, w_gate, w_up, w_down)


def _moe_combine_kernel(x1_ref, routet_ref, gt2_ref, fg_ref, y1_ref, y2_ref, o_ref):
    tm = x1_ref.shape[0]
    route = routet_ref[...].T
    moe = route[:, 0:1] * _tiles_to_rows(y1_ref, tm) + route[:, 1:2] * _tiles_to_rows(y2_ref, tm)
    xo = x1_ref[...] + gt2_ref[0] * moe
    ms = jnp.mean(xo * xo, axis=-1, keepdims=True)
    o_ref[...] = xo * lax.rsqrt(ms + NORM_EPS) * fg_ref[...]


def _moe_combine(x1, route_t, yt, mod3, final_g, s, tm):
    n, d = x1.shape
    tiles_per_batch = s // tm
    n_steps = n // tm
    per_o = route_t.shape[1] // tm
    return pl.pallas_call(
        _moe_combine_kernel,
        out_shape=jax.ShapeDtypeStruct((n, d), F32),
        grid=(n_steps,),
        in_specs=[pl.BlockSpec((tm, d), lambda i: (i, 0)),
                  pl.BlockSpec((SUBLANES, tm), lambda i: (i // per_o, i % per_o)),
                  pl.BlockSpec((1, 1, d), lambda i: ((i // tiles_per_batch) * 6 + 5, 0, 0)),
                  pl.BlockSpec((1, d), lambda i: (0, 0)),
                  pl.BlockSpec((tm * SUB, LANES), lambda i: (i, 0)),
                  pl.BlockSpec((tm * SUB, LANES), lambda i: (n_steps + i, 0))],
        out_specs=pl.BlockSpec((tm, d), lambda i: (i, 0)),
        compiler_params=_cparams(("arbitrary",)),
        name="moe_combine",
    )(x1, route_t, mod3, final_g.reshape(1, d), yt, yt)


def _pick(n, candidates):
    for t in candidates:
        if n % t == 0:
            return t
    raise ValueError(f"no tile in {candidates} divides {n}")


def kernel(x, c, ada_w, ada_b, norm1_g, w_in, hg_lb, hg_norm_g, rw_mu, rw_w0, rw_w2, rw_a0, rw_a2, rw_g2, rw_kk, rw_ka, rw_rk, rw_gn_w, rw_gn_b, w_proj_a, w_proj_b, w_out, norm2_g, router_g_w, router_g_b, router_e_w, router_e_b, exp_w_gate, exp_w_up, exp_w_down, final_g):
    bsz, s, d = x.shape
    depth = ada_w.shape[0]
    hg_f = hg_lb.shape[-1]
    hg_w = hg_norm_g.shape[-1]
    rw_w = rw_w0.shape[-1]
    rw_cols = rw_mu.shape[-1]
    n_groups = router_g_w.shape[-1]
    n_experts = router_e_w.shape[-1]
    assert hg_f == hg_w and s % CHUNK == 0 and n_experts + n_groups <= LANES and d == 2 * SUB * LANES

    lb_all = jnp.cumsum(jax.nn.softmax(hg_lb.astype(F32), axis=0), axis=0)
    n = bsz * s
    blk = 512
    n_blocks = (n * TOP_K + n_experts * blk) // blk
    for l in range(depth):
        mod = _ada_mod(c, ada_w[l], ada_b[l])
        mod3 = mod.reshape(bsz * 6, 1, d)

        hg_cols = 2 * hg_f + 2 * hg_w
        rw_pad = -(-rw_cols // 256) * 256
        wl = w_in[l]
        w_hg = wl[:, :hg_cols].astype(BF16)
        w_rw = jnp.zeros((d, rw_pad), BF16).at[:, :rw_cols].set(wl[:, hg_cols:hg_cols + rw_cols].astype(BF16))
        w_gt = wl[:, hg_cols + rw_cols:].astype(BF16)
        lora = (rw_w2.shape[1], rw_a2.shape[1], rw_g2.shape[1])
        assert rw_pad - 3 * rw_w == 256 and rw_cols == 3 * rw_w + sum(lora)
        hg, rw, gates = _in_proj(x, mod3, norm1_g[l], lb_all[l], rw_mu[l], lora, w_hg, w_rw, w_gt,
                                 _pick(s, (512, 256, 128, 64)))

        o_a = _hgrn2(hg, hg_norm_g[l], _pick(s, (1024, 512, 256, 128, 64)))
        o_b = _rwkv7(rw, rw_w0[l], rw_w2[l], rw_a0[l], rw_a2[l], rw_g2[l],
                     rw_kk[l], rw_ka[l], rw_rk[l].reshape(-1), rw_gn_w[l], rw_gn_b[l],
                     _pick(s, (2 * CHUNK, CHUNK)))

        wr = jnp.zeros((d, LANES), F32).at[:, :n_experts].set(router_e_w[l])
        wr = wr.at[:, n_experts:n_experts + n_groups].set(router_g_w[l])
        br = jnp.zeros((1, LANES), F32).at[0, :n_experts].set(router_e_b[l])
        br = br.at[0, n_experts:n_experts + n_groups].set(router_g_b[l])
        tm_o = _pick(s, (1024, 512, 256, 128, 64))
        x1, h2, route_t, counts = _out_proj(
            x, o_a, o_b, gates, mod3, norm2_g[l],
            w_proj_a[l].astype(BF16), w_proj_b[l].astype(BF16), w_out[l].astype(BF16),
            wr, br, n_groups, n_experts, tm_o)

        rt = route_t.reshape(n // tm_o, SUBLANES, tm_o)
        eid = rt[:, 2:4, :].astype(jnp.int32)
        rank = rt[:, 4:6, :].astype(jnp.int32)
        cnt = counts[:n_experts, 0].astype(jnp.int32)
        padded = (cnt + blk - 1) // blk * blk
        pad_end = jnp.cumsum(padded)
        pad_start = pad_end - padded
        e_ax = jnp.arange(n_experts, dtype=jnp.int32)[:, None, None, None]
        dest = rank + jnp.sum(jnp.where(eid[None] == e_ax, pad_start[:, None, None, None], 0), axis=0)
        blk_start = jnp.arange(n_blocks, dtype=jnp.int32) * blk
        blk_e = jnp.minimum(jnp.sum((pad_end[None, :] <= blk_start[:, None]).astype(jnp.int32), axis=1),
                            n_experts - 1)
        n_used = (pad_end[-1:] // blk).astype(jnp.int32)
        tm = tm_o
        smap = _moe_slotmap(dest.reshape(-1) + blk, pad_start + cnt, padded - cnt, n_used, n_blocks * blk, blk, tm_o, n)
        yt = _moe_experts_gather(h2, smap, blk_e, n_used, exp_w_gate[l], exp_w_up[l], exp_w_down[l], blk, n)
        last = l == depth - 1
        assert last, "the final RMSNorm is fused into the last layer's combine"
        out = _moe_combine(x1.reshape(n, d), route_t, yt, mod3, final_g, s, tm)
        x = out.reshape(bsz, s, d)
    return x
```

```python
import functools

import numpy as np
import jax
import jax.numpy as jnp
from jax import lax
from jax.experimental import pallas as pl
from jax.experimental.pallas import tpu as pltpu

F32 = jnp.float32
BF16 = jnp.bfloat16

NORM_EPS = 1e-6
HG_HEAD = 128
RW_HEAD = 64
RW_GN_EPS = 64e-5
TOP_K = 2
CHUNK = 64
LANES = 128
SUB = 4
U32 = jnp.uint32
MXU_K = 256
SUBLANES = 8
SMAP_BITS = 16
VMEM_LIMIT = 56 * 1024 * 1024

NT = (((1,), (1,)), ((), ()))
TN = (((0,), (0,)), ((), ()))


def _dot(a, b, dims=None, precision=None):
    if dims is None:
        return jnp.dot(a, b, preferred_element_type=F32, precision=precision)
    return lax.dot_general(a, b, dims, preferred_element_type=F32, precision=precision)


def _split3(x):
    hi = x.astype(BF16)
    r1 = x - hi.astype(F32)
    mid = r1.astype(BF16)
    lo = (r1 - mid.astype(F32)).astype(BF16)
    return hi, mid, lo


def _dot_exact_lhs(m3_bf16, x):
    return _dot(m3_bf16, jnp.concatenate(_split3(x), axis=0))


def _head_sums(x, m2_bf16):
    outs = []
    for g0 in range(0, x.shape[1], MXU_K):
        xg = x[:, g0:g0 + MXU_K]
        hi = xg.astype(BF16)
        lo = (xg - hi.astype(F32)).astype(BF16)
        outs.append(_dot(jnp.concatenate([hi, lo], axis=1), m2_bf16))
    return jnp.concatenate(outs, axis=1)


def _sigmoid(x):
    return 1.0 / (1.0 + jnp.exp(-x))


def _silu(x):
    return x * _sigmoid(x)


def _rows_to_tiles(ref, val):
    m, half = val.shape[0], val.shape[1] // 2
    hi = lax.bitcast_convert_type(val[:, :half].astype(BF16).astype(F32), U32)
    lo = lax.bitcast_convert_type(val[:, half:].astype(BF16).astype(F32), U32)
    w = (hi & jnp.uint32(0xFFFF0000)) | (lo >> 16)
    for j in range(SUB):
        ref[pl.ds(j, m, stride=SUB), :] = w[:, j * LANES:(j + 1) * LANES]


def _tiles_to_rows(ref, m, base=0):
    w = jnp.concatenate([ref[pl.ds(base * SUB + j, m, stride=SUB), :] for j in range(SUB)], axis=1)
    hi = lax.bitcast_convert_type(w & jnp.uint32(0xFFFF0000), F32)
    lo = lax.bitcast_convert_type(w << 16, F32)
    return jnp.concatenate([hi, lo], axis=1)


def _cparams(sem):
    return pltpu.CompilerParams(dimension_semantics=sem, vmem_limit_bytes=VMEM_LIMIT)


def _ada_kernel(cb_ref, w_ref, b_ref, o_ref):
    bsz, d = cb_ref.shape[0], cb_ref.shape[1]
    tn = w_ref.shape[1]
    reps = tn // LANES

    def body(i, accs):
        k0 = pl.multiple_of(i * SUBLANES, SUBLANES)
        w = w_ref[pl.ds(k0, SUBLANES), :]
        out = []
        for b in range(bsz):
            sc = _silu(cb_ref[b, pl.ds(k0, SUBLANES), :])
            out.append(accs[b] + w * jnp.concatenate([sc] * reps, axis=1))
        return tuple(out)

    accs = lax.fori_loop(0, d // SUBLANES, body, tuple(jnp.zeros((SUBLANES, tn), F32) for _ in range(bsz)),
                         unroll=4)
    o_ref[...] = jnp.zeros_like(o_ref)
    for b in range(bsz):
        o_ref[b:b + 1, :] = jnp.sum(accs[b], axis=0, keepdims=True) + b_ref[...]


def _ada_mod(c, w, b):
    bsz, d = c.shape
    n = w.shape[1]
    rows = -(-bsz // SUBLANES) * SUBLANES
    cb = jnp.broadcast_to(c[:, :, None], (bsz, d, LANES))
    tn = _pick(n, (1024, 512, 256, 128))
    out = pl.pallas_call(
        _ada_kernel,
        out_shape=jax.ShapeDtypeStruct((rows, n), F32),
        grid=(n // tn,),
        in_specs=[pl.BlockSpec((bsz, d, LANES), lambda j: (0, 0, 0)),
                  pl.BlockSpec((d, tn), lambda j: (0, j)),
                  pl.BlockSpec((1, tn), lambda j: (0, j))],
        out_specs=pl.BlockSpec((rows, tn), lambda j: (0, j)),
        compiler_params=_cparams(("arbitrary",)),
        name="ada_mod",
    )(cb, w, b.reshape(1, n))
    return out[:bsz]


def _in_proj_kernel(lora, x_ref, sh_ref, sc_ref, g_ref, lb_ref, mu_ref, whg_ref, wrw_ref, wgt_ref,
                    hg_ref, rw_ref, gt_ref, carry_ref):
    @pl.when(pl.program_id(1) == 0)
    def _():
        carry_ref[...] = jnp.zeros_like(carry_ref)

    x = x_ref[0]
    ms = jnp.mean(x * x, axis=-1, keepdims=True)
    h = (x * lax.rsqrt(ms + NORM_EPS) * g_ref[...]) * (1.0 + sc_ref[0]) + sh_ref[0]
    hb = h.astype(BF16)
    lb = lb_ref[...]
    w = lb.shape[1]
    hg_maps = (_silu, lambda t: lb + (1.0 - lb) * _sigmoid(t), lambda t: t, _silu)
    for part, fn in enumerate(hg_maps):
        hg_ref[0, :, part * w:(part + 1) * w] = fn(_dot(hb, whg_ref[:, part * w:(part + 1) * w]))
    step = 512
    n_rw = wrw_ref.shape[1]
    row = lax.broadcasted_iota(jnp.int32, (x.shape[0], 256), 0)
    lane = lax.broadcasted_iota(jnp.int32, (x.shape[0], 256), 1)
    dl, al, gl = lora
    for n0 in range(0, n_rw, 256):
        cs = slice(n0, n0 + 256)
        p = _dot(hb, wrw_ref[:, cs])
        prev = jnp.where(row == 0, carry_ref[:, cs], pltpu.roll(p, 1, 0))
        carry_ref[:, cs] = p[x.shape[0] - 1:, :]
        xs = p + mu_ref[:, cs] * (prev - p)
        if n0 == n_rw - 256:
            xs = jnp.where(lane < dl, jnp.tanh(xs),
                           jnp.where(lane < dl + al, xs, jnp.where(lane < dl + al + gl, _sigmoid(xs), 0.0)))
        rw_ref[0, :, cs] = xs
    for n0 in range(0, wgt_ref.shape[1], step):
        gt_ref[0, :, n0:n0 + step] = _sigmoid(_dot(hb, wgt_ref[:, n0:n0 + step])).astype(BF16)


def _in_proj(x, mod3, norm_g, lb, mu, lora, w_hg, w_rw, w_gt, tm):
    bsz, s, d = x.shape
    assert w_hg.shape[1] == 4 * lb.shape[0] and sum(lora) <= 256
    mup = jnp.zeros((1, w_rw.shape[1]), F32).at[0, :mu.shape[-1]].set(mu)
    n_hg, n_rw, n_gt = w_hg.shape[1], w_rw.shape[1], w_gt.shape[1]
    const = lambda b, i: (0, 0)
    return pl.pallas_call(
        functools.partial(_in_proj_kernel, lora),
        out_shape=(jax.ShapeDtypeStruct((bsz, s, n_hg), F32),
                   jax.ShapeDtypeStruct((bsz, s, n_rw), F32),
                   jax.ShapeDtypeStruct((bsz, s, n_gt), BF16)),
        grid=(bsz, s // tm),
        in_specs=[pl.BlockSpec((1, tm, d), lambda b, i: (b, i, 0)),
                  pl.BlockSpec((1, 1, d), lambda b, i: (b * 6 + 0, 0, 0)),
                  pl.BlockSpec((1, 1, d), lambda b, i: (b * 6 + 1, 0, 0)),
                  pl.BlockSpec((1, d), const),
                  pl.BlockSpec((1, lb.shape[0]), const),
                  pl.BlockSpec((1, n_rw), const),
                  pl.BlockSpec((d, n_hg), const),
                  pl.BlockSpec((d, n_rw), const),
                  pl.BlockSpec((d, n_gt), const)],
        out_specs=(pl.BlockSpec((1, tm, n_hg), lambda b, i: (b, i, 0)),
                   pl.BlockSpec((1, tm, n_rw), lambda b, i: (b, i, 0)),
                   pl.BlockSpec((1, tm, n_gt), lambda b, i: (b, i, 0))),
        scratch_shapes=[pltpu.VMEM((1, n_rw), F32)],
        compiler_params=_cparams(("arbitrary", "arbitrary")),
        name="in_proj",
    )(x, mod3, mod3, norm_g.reshape(1, d), lb.reshape(1, -1), mup, w_hg, w_rw, w_gt)


_HG_LEVELS = (32, 16, 8, 4, 2, 1)


def _hgrn2_consts(width):
    c = CHUNK
    t = np.arange(c)[:, None]
    s = np.arange(c)[None, :]
    blocks = [(s <= t), (s > t)]
    lvl_masks = []
    right = []
    for h in _HG_LEVELS:
        m = (t // (2 * h)) * 2 * h + h
        is_r = (t & h) != 0
        blk = np.where(is_r, (s >= m) & (s <= t), (s > t) & (s <= m - 1))
        blocks.append(blk)
        lvl_masks.append(is_r & ((s & h) == 0) & ((t // (2 * h)) == (s // (2 * h))))
        right.append(np.broadcast_to(is_r, (c, width)))
    mst = np.tile(np.concatenate(blocks, axis=0).astype(np.float32), (1, 3))
    lm = np.stack([np.eye(c, dtype=bool)] + lvl_masks).astype(np.float32)
    rm = np.stack(right).astype(np.float32)
    return jnp.asarray(mst, BF16), jnp.asarray(lm, F32), jnp.asarray(rm, F32)


def _hgrn2_kernel(q_ref, f_ref, i_ref, g_ref, ng_ref, mst_ref, lm_ref, rm_ref, o_ref, st_ref):
    c = CHUNK
    n_chunks = q_ref.shape[1] // c

    @pl.when(pl.program_id(1) == 0)
    def _():
        st_ref[...] = jnp.zeros_like(st_ref)

    nsub = next(n for n in (8, 4, 2, 1) if n_chunks % n == 0)

    def chunk_body(ci, carry):
        r0 = pl.multiple_of(ci * (nsub * c), nsub * c)
        for _ in _hgrn2_steps(q_ref, f_ref, i_ref, g_ref, ng_ref, mst_ref, lm_ref, rm_ref, o_ref, st_ref,
                              0, r0, nsub):
            pass
        return carry

    lax.fori_loop(0, n_chunks // nsub, chunk_body, 0)


def _hgrn2_steps(q_ref, f_ref, i_ref, g_ref, ng_ref, mst_ref, lm_ref, rm_ref, o_ref, st_ref, b, r0, nsub):
    c = CHUNK
    n_heads = q_ref.shape[2] // HG_HEAD
    mst = mst_ref[...]
    ng = ng_ref[...]
    heads = [slice(hd * HG_HEAD, (hd + 1) * HG_HEAD) for hd in range(n_heads)]
    subs = []
    for j in range(nsub):
        rows = pl.ds(r0 + j * c, c)
        q = q_ref[b, rows, :]
        f = f_ref[b, rows, :]
        k = 1.0 - f
        ex = jnp.exp2(_dot_exact_lhs(mst, jnp.log2(f)))
        subs.append(dict(rows=rows, q=q, k=k, ex=ex, vb=i_ref[b, rows, :].astype(BF16),
                         qd=(q * ex[0:c]).astype(BF16), kr=(k * ex[c:2 * c]).astype(BF16)))
        yield
    for sb in subs:
        qb, kb = sb['q'].astype(BF16), sb['k'].astype(BF16)
        sb['sc'] = [lm_ref[0] * _dot(qb[:, ls], kb[:, ls], NT) for ls in heads]
        sb['dqk'] = sb['q'] - sb['k']
    yield
    for li, h in enumerate(_HG_LEVELS):
        for sb in subs:
            if h % SUBLANES == 0:
                qk = jnp.concatenate([(sb['q'] if m % 2 else sb['k'])[m * h:(m + 1) * h] for m in range(c // h)],
                                     axis=0)
            else:
                qk = sb['k'] + rm_ref[li] * sb['dqk']
            g_l = (qk * sb['ex'][(2 + li) * c:(3 + li) * c]).astype(BF16)
            sb['sc'] = [s_h + lm_ref[li + 1] * _dot(g_l[:, ls], g_l[:, ls], NT)
                        for s_h, ls in zip(sb['sc'], heads)]
        yield
    for sb in subs:
        sb['kv'] = [_dot(sb['vb'][:, ls], sb['kr'][:, ls], TN) for ls in heads]
        sb['o'] = [_dot(s_h.astype(BF16), sb['vb'][:, ls]) for s_h, ls in zip(sb['sc'], heads)]
        yield
    sts = [st_ref[b, hd] for hd in range(n_heads)]
    for sb in subs:
        sb['o'] = [o_h + _dot(sb['qd'][:, ls], st.astype(BF16), NT) for o_h, ls, st in zip(sb['o'], heads, sts)]
        sts = [st * sb['ex'][c - 1:c, ls] + kv for st, ls, kv in zip(sts, heads, sb['kv'])]
        yield
    for hd in range(n_heads):
        st_ref[b, hd] = sts[hd]
    for sb in subs:
        on = [o_h * lax.rsqrt(jnp.mean(o_h * o_h, axis=-1, keepdims=True) + NORM_EPS) for o_h in sb['o']]
        o_full = jnp.concatenate(on, axis=1) * ng
        o_ref[b, sb['rows'], :] = (o_full * g_ref[b, sb['rows'], :]).astype(o_ref.dtype)
        yield


def _hgrn2(hg, norm_g, ts):
    bsz, s, n4 = hg.shape
    w = n4 // 4
    mst, lm, rm = _hgrn2_consts(w)
    n_heads = w // HG_HEAD
    const2 = lambda b, i: (0, 0)
    const3 = lambda b, i: (0, 0, 0)
    return pl.pallas_call(
        _hgrn2_kernel,
        out_shape=jax.ShapeDtypeStruct((bsz, s, w), BF16),
        grid=(bsz, s // ts),
        in_specs=[pl.BlockSpec((1, ts, w), lambda b, i: (b, i, 0)),
                  pl.BlockSpec((1, ts, w), lambda b, i: (b, i, 1)),
                  pl.BlockSpec((1, ts, w), lambda b, i: (b, i, 2)),
                  pl.BlockSpec((1, ts, w), lambda b, i: (b, i, 3)),
                  pl.BlockSpec((1, w), const2),
                  pl.BlockSpec(mst.shape, const2),
                  pl.BlockSpec(lm.shape, const3),
                  pl.BlockSpec(rm.shape, const3)],
        out_specs=pl.BlockSpec((1, ts, w), lambda b, i: (b, i, 0)),
        scratch_shapes=[pltpu.VMEM((1, n_heads, HG_HEAD, HG_HEAD), F32)],
        compiler_params=_cparams(("arbitrary", "arbitrary")),
        name="hgrn2",
    )(hg, hg, hg, hg, norm_g.reshape(1, w), mst, lm, rm)


def _rwkv_consts(width):
    c = CHUNK
    t = np.arange(c)[:, None]
    s = np.arange(c)[None, :]
    tri = np.tile((s <= t).astype(np.float32), (1, 3))
    tt = np.arange(2 * c)[:, None]
    ss = np.arange(2 * c)[None, :]
    same = (tt // c) == (ss // c)
    strict = same & ((ss % c) < (tt % c))
    incl = same & ((ss % c) <= (tt % c))
    hsum = (np.arange(MXU_K)[:, None] // RW_HEAD) == (np.arange(MXU_K)[None, :] // RW_HEAD)
    hsum = np.tile(hsum, (2, 1))
    return (jnp.asarray(tri, BF16), jnp.asarray(strict.astype(np.float32), F32),
            jnp.asarray(incl.astype(np.float32), F32), jnp.asarray(hsum.astype(np.float32), BF16))


def _rwkv7_kernel(p_ref, w0_ref, a0_ref, kk_ref, ka_ref, rk_ref, gnw_ref, gnb_ref,
                  w2_ref, a2_ref, g2_ref, tri_ref, sm_ref, im_ref, hs_ref,
                  o_ref, zt_ref):
    @pl.when(pl.program_id(0) == 0)
    def _():
        zt_ref[...] = jnp.zeros_like(zt_ref)

    for _ in _rwkv7_steps(p_ref, w0_ref, a0_ref, kk_ref, ka_ref, rk_ref, gnw_ref, gnb_ref,
                          w2_ref, a2_ref, g2_ref, tri_ref, sm_ref, im_ref, hs_ref, o_ref, zt_ref):
        pass


def _rwkv7_steps(p_ref, w0_ref, a0_ref, kk_ref, ka_ref, rk_ref, gnw_ref, gnb_ref,
                 w2_ref, a2_ref, g2_ref, tri_ref, sm_ref, im_ref, hs_ref, o_ref, zt_ref):
    c = CHUNK
    nb = p_ref.shape[0]
    nch = p_ref.shape[1] // c
    width = o_ref.shape[2]
    n_pairs = width // LANES

    hs = hs_ref[...]
    tri = tri_ref[...]
    smask = sm_ref[...] > 0
    imask = im_ref[...] > 0
    lane = lax.broadcasted_iota(jnp.int32, (c, LANES), 1)
    m0 = (lane < RW_HEAD).astype(F32)
    m1 = 1.0 - m0

    def stack(x):
        return jnp.concatenate([x * m0, x * m1], axis=0)

    xs = jnp.concatenate([p_ref[b] for b in range(nb)], axis=0)
    r_all = xs[:, 0:width]
    k_all = xs[:, width:2 * width]
    v_all = xs[:, 2 * width:3 * width]
    slab = xs[:, 3 * width:].astype(BF16)
    nz = -(w0_ref[...] + _dot(slab, w2_ref[...]))
    softplus = jnp.maximum(nz, 0.0) + jnp.log(1.0 + jnp.exp(-jnp.abs(nz)))
    ld_all = -jnp.exp(-softplus - 0.5)
    a_all = _sigmoid(a0_ref[...] + _dot(slab, a2_ref[...]))
    g_all = _dot(slab, g2_ref[...])
    kk0 = k_all * kk_ref[...]
    kk_all = kk0 * lax.rsqrt(jnp.maximum(_head_sums(kk0 * kk0, hs), 1e-24))
    k2_all = k_all * (1.0 + (a_all - 1.0) * ka_ref[...])
    yield

    units = []
    for b, j in [(b, j) for b in range(nb) for j in range(nch)]:
        rb = slice((b * nch + j) * c, (b * nch + j + 1) * c)
        r, k2, v, ld = r_all[rb], k2_all[rb], v_all[rb], ld_all[rb]
        a_in = -kk_all[rb]
        b_in = kk_all[rb] * a_all[rb]
        cum = _dot_exact_lhs(tri, ld)
        cum_t = cum[c - 1:c, :]
        e_c = jnp.exp(cum)
        e_nc = jnp.exp(-cum)
        e_rem = jnp.exp(cum_t - cum)
        at_f = a_in * jnp.exp(cum - ld)
        rt_f = r * e_c
        kt_f = k2 * e_nc
        bt_f = b_in * e_nc
        kh_f = k2 * e_rem
        bh_f = b_in * e_rem
        p_t = jnp.exp(cum_t)
        for pi in range(n_pairs):
            ls = slice(pi * LANES, (pi + 1) * LANES)
            units.append(dict(
                b=b, j=j, pi=pi,
                at=stack(at_f[:, ls]).astype(BF16), rt=stack(rt_f[:, ls]).astype(BF16),
                kt=stack(kt_f[:, ls]).astype(BF16), bt=stack(bt_f[:, ls]).astype(BF16),
                kh=stack(kh_f[:, ls]).astype(BF16), bh=stack(bh_f[:, ls]).astype(BF16),
                vs=stack(v[:, ls]).astype(BF16), p_t=p_t[:, ls]))
        yield

    for u in units:
        lhs = jnp.concatenate([u['at'], u['rt']], axis=0)
        u['g'] = _dot(lhs, jnp.concatenate([u['kt'], u['bt']], axis=0), NT)
    yield
    for u in units:
        g = u.pop('g')
        u['a_ak'] = jnp.where(smask, g[:2 * c, :2 * c], 0.0).astype(BF16)
        u['pw'] = jnp.where(smask, g[:2 * c, 2 * c:], 0.0).astype(BF16)
        u['a_r'] = jnp.where(jnp.concatenate([imask, imask], axis=1), g[2 * c:], 0.0).astype(BF16)
    for u in units:
        akv = _dot(u.pop('a_ak'), u['vs'])
        u['x'] = jnp.concatenate([u['at'].astype(F32), akv], axis=1)
    yield
    n_lvl = int(np.log2(c))
    for lvl in range(n_lvl):
        for u in units:
            u['x'] = u['x'] + _dot(u['pw'], u['x'].astype(BF16))
        yield
        if lvl + 1 < n_lvl:
            for u in units:
                u['pw'] = _dot(u['pw'], u['pw']).astype(BF16)
            yield
    for u in units:
        x = u.pop('x')
        u['wr'] = jnp.concatenate([x[:, :LANES].astype(BF16), u['rt']], axis=0)
        u['u_loc'] = x[:, LANES:]
    zt = {(b, pi): zt_ref[b, pi] for b in range(nb) for pi in range(n_pairs)}
    for j in range(nch):
        tail = [u for u in units if u['j'] == j]
        for u in tail:
            u['uy'] = _dot(u.pop('wr'), zt[u['b'], u['pi']].astype(BF16), NT)
        yield
        for u in tail:
            uy = u.pop('uy')
            u['u'] = (uy[:2 * c] + u.pop('u_loc')).astype(BF16)
            u['y0'] = uy[2 * c:]
        for u in tail:
            vu = jnp.concatenate([u['vs'], u['u']], axis=0)
            u['y'] = u.pop('y0') + _dot(u['a_r'], vu)
            upd = _dot(vu, jnp.concatenate([u['kh'], u['bh']], axis=0), TN)
            zt[u['b'], u['pi']] = zt[u['b'], u['pi']] * u['p_t'] + upd
        yield
    for (b, pi), z in zt.items():
        zt_ref[b, pi] = z

    inv_n = 1.0 / RW_HEAD
    y = jnp.concatenate(
        [jnp.concatenate([u['y'][:c] + u['y'][c:] for u in units if (u['b'], u['j']) == (b, j)], axis=1)
         for b in range(nb) for j in range(nch)], axis=0)
    mean = _head_sums(y, hs) * inv_n
    yield
    d = y - mean
    var = _head_sums(d * d, hs) * inv_n
    yield
    yn = d * lax.rsqrt(var + RW_GN_EPS) * gnw_ref[...] + gnb_ref[...]
    bonus = _head_sums(r_all * k2_all * rk_ref[...], hs) * v_all
    out = ((yn + bonus) * g_all).astype(o_ref.dtype)
    for b in range(nb):
        o_ref[b] = out[b * nch * c:(b + 1) * nch * c]


def _rwkv7(rw, w0, w2, a0, a2, g2, k_k, k_a, r_k, gn_w, gn_b, ts):
    bsz, s, cols = rw.shape
    width = w0.shape[-1]
    n_pairs = width // LANES
    slab = cols - 3 * width
    dl, al, gl = w2.shape[0], a2.shape[0], g2.shape[0]
    w2f = jnp.zeros((slab, width), F32).at[0:dl].set(w2).astype(BF16)
    a2f = jnp.zeros((slab, width), F32).at[dl:dl + al].set(a2).astype(BF16)
    g2f = jnp.zeros((slab, width), F32).at[dl + al:dl + al + gl].set(g2).astype(BF16)
    tri, sm, im, hs = _rwkv_consts(width)
    row = lambda x: x.reshape(1, width)
    const = lambda i: (0, 0)
    vec = pl.BlockSpec((1, width), const)
    return pl.pallas_call(
        _rwkv7_kernel,
        out_shape=jax.ShapeDtypeStruct((bsz, s, width), BF16),
        grid=(s // ts,),
        in_specs=[pl.BlockSpec((bsz, ts, cols), lambda i: (0, i, 0)),
                  vec, vec, vec, vec, vec, vec, vec,
                  pl.BlockSpec((slab, width), const),
                  pl.BlockSpec((slab, width), const),
                  pl.BlockSpec((slab, width), const),
                  pl.BlockSpec(tri.shape, const),
                  pl.BlockSpec(sm.shape, const),
                  pl.BlockSpec(im.shape, const),
                  pl.BlockSpec(hs.shape, const)],
        out_specs=pl.BlockSpec((bsz, ts, width), lambda i: (0, i, 0)),
        scratch_shapes=[pltpu.VMEM((bsz, n_pairs, LANES, LANES), F32)],
        compiler_params=_cparams(("arbitrary",)),
        name="rwkv7",
    )(rw, row(w0), row(a0), row(k_k), row(k_a), row(r_k), row(gn_w), row(gn_b),
      w2f, a2f, g2f, tri, sm, im, hs)


def _out_proj_kernel(n_groups, n_experts,
                     x_ref, oa_ref, ob_ref, ga_ref, gb_ref, gt1_ref, sc2_ref, sh2_ref, g2_ref,
                     wa_ref, wb_ref, wo_ref, wr_ref, wrl_ref, br_ref, upper_ref,
                     x1_ref, h2_ref, routet_ref, cnt_ref, carry_ref):
    first = (pl.program_id(0) == 0) & (pl.program_id(1) == 0)

    @pl.when(first)
    def _():
        carry_ref[...] = jnp.zeros_like(carry_ref)

    pa = _dot(oa_ref[0], wa_ref[...])
    pb = _dot(ob_ref[0], wb_ref[...])
    mixed = ga_ref[0].astype(F32) * pa + gb_ref[0].astype(F32) * pb
    x1 = x_ref[0] + gt1_ref[0] * _dot(mixed.astype(BF16), wo_ref[...])
    x1_ref[0] = x1
    ms = jnp.mean(x1 * x1, axis=-1, keepdims=True)
    h2 = (x1 * lax.rsqrt(ms + NORM_EPS) * g2_ref[...]) * (1.0 + sc2_ref[0]) + sh2_ref[0]
    _rows_to_tiles(h2_ref, h2)

    h2_hi = h2.astype(BF16)
    h2_lo = (h2 - h2_hi.astype(F32)).astype(BF16)
    logits = (_dot(wr_ref[...], h2_hi, NT) + _dot(wr_ref[...], h2_lo, NT) + _dot(wrl_ref[...], h2_hi, NT)
              + br_ref[...])
    row = lax.broadcasted_iota(jnp.int32, logits.shape, 0)
    neg = jnp.float32(-jnp.inf)
    big = jnp.int32(1 << 20)
    eg = n_experts // n_groups
    is_g = (row >= n_experts) & (row < n_experts + n_groups)
    lg = jnp.where(is_g, logits, neg)
    mg = jnp.max(lg, axis=0, keepdims=True)
    p_grp = 1.0 / jnp.sum(jnp.where(is_g, jnp.exp(lg - mg), 0.0), axis=0, keepdims=True)
    gidx = jnp.min(jnp.where(lg == mg, row, big), axis=0, keepdims=True) - n_experts
    sel = (row >= gidx * eg) & (row < gidx * eg + eg)
    le = jnp.where(sel, logits, neg)
    me = jnp.max(le, axis=0, keepdims=True)
    pe_un = jnp.where(sel, jnp.exp(le - me), 0.0)
    pe = jnp.where(sel, pe_un / jnp.sum(pe_un, axis=0, keepdims=True), -1.0)
    v1 = jnp.max(pe, axis=0, keepdims=True)
    i1 = jnp.min(jnp.where(pe == v1, row, big), axis=0, keepdims=True)
    pe2 = jnp.where(row == i1, -1.0, pe)
    v2 = jnp.max(pe2, axis=0, keepdims=True)
    i2 = jnp.min(jnp.where(pe2 == v2, row, big), axis=0, keepdims=True)
    wsum = v1 + v2
    w1 = p_grp * v1 / wsum
    w2 = p_grp * v2 / wsum

    oh1 = (row == i1).astype(F32)
    oh2 = (row == i2).astype(F32)
    both = oh1 + oh2
    before = _dot(both.astype(BF16), upper_ref[...]) + carry_ref[...]
    rank1 = jnp.sum(oh1 * before, axis=0, keepdims=True)
    rank2 = jnp.sum(oh2 * before, axis=0, keepdims=True)
    carry_ref[...] = carry_ref[...] + jnp.sum(both, axis=1, keepdims=True)
    cnt_ref[...] = carry_ref[...]
    zero = jnp.zeros_like(w1)
    routet_ref[...] = jnp.concatenate(
        [w1, w2, i1.astype(F32), i2.astype(F32), rank1, rank2, zero, zero], axis=0)


def _out_proj(x, o_a, o_b, gates, mod3, norm2_g, wa, wb, wo, wr, br, n_groups, n_experts, tm):
    bsz, s, d = x.shape
    wdt = o_a.shape[-1]
    upper = jnp.asarray(np.triu(np.ones((tm, tm), np.float32), 1), BF16)
    wrt = wr.T
    wr_hi = wrt.astype(BF16)
    wr_lo = (wrt - wr_hi.astype(F32)).astype(BF16)
    const = lambda b, i: (0, 0)
    tile = lambda b, i: (b, i, 0)
    kern = functools.partial(_out_proj_kernel, n_groups, n_experts)
    return pl.pallas_call(
        kern,
        out_shape=(jax.ShapeDtypeStruct((bsz, s, d), F32),
                   jax.ShapeDtypeStruct((bsz * s * SUB, LANES), U32),
                   jax.ShapeDtypeStruct((bsz * (s // tm) * SUBLANES, tm), F32),
                   jax.ShapeDtypeStruct((LANES, 1), F32)),
        grid=(bsz, s // tm),
        in_specs=[pl.BlockSpec((1, tm, d), tile),
                  pl.BlockSpec((1, tm, wdt), tile),
                  pl.BlockSpec((1, tm, wdt), tile),
                  pl.BlockSpec((1, tm, d), lambda b, i: (b, i, 0)),
                  pl.BlockSpec((1, tm, d), lambda b, i: (b, i, 1)),
                  pl.BlockSpec((1, 1, d), lambda b, i: (b * 6 + 2, 0, 0)),
                  pl.BlockSpec((1, 1, d), lambda b, i: (b * 6 + 4, 0, 0)),
                  pl.BlockSpec((1, 1, d), lambda b, i: (b * 6 + 3, 0, 0)),
                  pl.BlockSpec((1, d), const),
                  pl.BlockSpec(wa.shape, const),
                  pl.BlockSpec(wb.shape, const),
                  pl.BlockSpec(wo.shape, const),
                  pl.BlockSpec(wrt.shape, const),
                  pl.BlockSpec(wrt.shape, const),
                  pl.BlockSpec((LANES, 1), const),
                  pl.BlockSpec((tm, tm), const)],
        out_specs=(pl.BlockSpec((1, tm, d), tile),
                   pl.BlockSpec((tm * SUB, LANES), lambda b, i: (b * (s // tm) + i, 0)),
                   pl.BlockSpec((SUBLANES, tm), lambda b, i: (b * (s // tm) + i, 0)),
                   pl.BlockSpec((LANES, 1), const)),
        scratch_shapes=[pltpu.VMEM((LANES, 1), F32)],
        compiler_params=_cparams(("arbitrary", "arbitrary")),
        name="out_proj",
    )(x, o_a, o_b, gates, gates, mod3, mod3, mod3, norm2_g.reshape(1, d), wa, wb, wo, wr_hi, wr_lo,
      br.reshape(LANES, 1), upper)


def _moe_slotmap_kernel(tm_o, n_tok, blk, dest_ref, zstart_ref, zcnt_ref, nused_ref, smap_ref):
    i = pl.program_id(0)
    n_slots = smap_ref.shape[0] - blk
    n_experts = zcnt_ref.shape[0]

    @pl.when(i == 0)
    def _():
        unroll = 8

        def init(g, carry):
            for u in range(unroll):
                smap_ref[blk + g * unroll + u] = TOP_K * n_tok
            return carry
        lax.fori_loop(nused_ref[0] * (blk // unroll), n_slots // unroll, init, 0)
        for r in range(blk):
            smap_ref[r] = TOP_K * n_tok + n_experts * blk + r
        for e in range(n_experts):
            def pad(g, carry, e=e):
                for u in range(unroll):
                    j = jnp.maximum(zcnt_ref[e] - 1 - (g * unroll + u), 0)
                    smap_ref[blk + zstart_ref[e] + j] = TOP_K * n_tok + e * blk + j
                return carry
            lax.fori_loop(0, (zcnt_ref[e] + unroll - 1) // unroll, pad, 0)

    base = i * (TOP_K * tm_o)
    both = 1 + (1 << SMAP_BITS)
    for k in range(TOP_K):
        v0 = k * n_tok + (i * tm_o) * both
        for r in range(tm_o):
            smap_ref[dest_ref[base + k * tm_o + r]] = v0 + r * both


def _moe_slotmap(dest, zstart, zcnt, n_used, n_slots, blk, tm_o, n_tok):
    grid_spec = pltpu.PrefetchScalarGridSpec(
        num_scalar_prefetch=4,
        grid=(n_tok // tm_o,),
        in_specs=[],
        out_specs=pl.BlockSpec(memory_space=pltpu.SMEM),
    )
    n_experts = zcnt.shape[0]
    assert TOP_K * n_tok + (n_experts + 1) * blk <= (1 << SMAP_BITS) and n_tok <= (1 << (31 - SMAP_BITS))
    return pl.pallas_call(
        functools.partial(_moe_slotmap_kernel, tm_o, n_tok, blk),
        out_shape=jax.ShapeDtypeStruct((blk + n_slots,), jnp.int32),
        grid_spec=grid_spec,
        compiler_params=_cparams(("arbitrary",)),
        name="moe_slotmap",
    )(dest, zstart, zcnt, n_used)


def _moe_expert_gather_kernel(n_tok, blk_e_ref, nused_ref, smap_ref, h_ref, wg_ref, wu_ref, wd_ref, y_ref,
                              hv, xbuf, ystage, wgb, wub, wdb, sem, hsem):
    i = pl.program_id(0)
    nb = pl.num_programs(0)
    n_used = nused_ref[0]
    blk = xbuf.shape[1] // SUB
    first_real_blocks = TOP_K * n_tok // blk

    def gather(b, slot, rows):
        for r in rows:
            tok = lax.shift_right_logical(smap_ref[(b + 1) * blk + r], SMAP_BITS)
            src = pl.multiple_of(tok * SUB, SUB)
            xbuf[slot, pl.ds(r * SUB, SUB), :] = hv[pl.ds(src, SUB), :]

    def issue(b, slot, rows):
        for r in rows:
            t = smap_ref[(b + 1) * blk + r] & ((1 << SMAP_BITS) - 1)
            dst = pl.multiple_of(t * SUB, SUB)
            pltpu.make_async_copy(ystage.at[slot, pl.ds(r * SUB, SUB), :], y_ref.at[pl.ds(dst, SUB), :],
                                  sem.at[slot]).start(priority=r % 2)

    def wait_block(slot):
        pltpu.make_async_copy(ystage.at[slot], y_ref.at[pl.ds(0, blk * SUB), :], sem.at[slot]).wait()

    @pl.when(i == 0)
    def _():
        load = pltpu.make_async_copy(h_ref, hv, hsem)
        load.start()
        ystage[...] = jnp.zeros_like(ystage)
        n_spare_blocks = y_ref.shape[0] // (blk * SUB) - first_real_blocks

        def spare_copy(c):
            dst = (first_real_blocks + c) * blk * SUB
            return pltpu.make_async_copy(ystage.at[0], y_ref.at[pl.ds(dst, blk * SUB), :], sem.at[0])

        for c in range(n_spare_blocks):
            spare_copy(c).start()
        for c in range(n_spare_blocks):
            spare_copy(c).wait()
        load.wait()
        gather(0, 0, range(blk))

    new_expert = (i == 0) | (blk_e_ref[i] != blk_e_ref[jnp.maximum(i - 1, 0)])

    @pl.when((i < n_used) & new_expert)
    def _():
        wgb[...] = wg_ref[0].astype(BF16)
        wub[...] = wu_ref[0].astype(BF16)
        wdb[...] = wd_ref[0].astype(BF16)

    @pl.when(i < n_used)
    def _():
        slot = i % 2
        pslot = 1 - slot
        nxt = jnp.minimum(i + 1, nb - 1)

        @pl.when(i > 0)
        def _():
            wait_block(slot)

        q = blk // 4
        xb = _tiles_to_rows(xbuf.at[slot], blk).astype(BF16)
        issue(i - 1, pslot, range(0, q))
        gather(nxt, pslot, range(0, q))
        hg = _dot(xb, wgb[...])
        issue(i - 1, pslot, range(q, 2 * q))
        gather(nxt, pslot, range(q, 2 * q))
        hu = _dot(xb, wub[...])
        issue(i - 1, pslot, range(2 * q, 3 * q))
        gather(nxt, pslot, range(2 * q, 3 * q))
        hid = (_silu(hg) * hu).astype(BF16)
        y = _dot(hid, wdb[...])
        issue(i - 1, pslot, range(3 * q, blk))
        gather(nxt, pslot, range(3 * q, blk))
        _rows_to_tiles(ystage.at[slot], y)

        @pl.when(i == n_used - 1)
        def _():
            issue(i, slot, range(blk))
            wait_block(pslot)
            wait_block(slot)


def _moe_experts_gather(h2, smap, blk_e, n_used, w_gate, w_up, w_down, blk, n_tok):
    d = w_gate.shape[1]
    assert d == 2 * SUB * LANES and h2.shape == (n_tok * SUB, LANES)
    n_slots = smap.shape[0] - blk
    nb = n_slots // blk
    f = w_gate.shape[-1]
    n_experts = w_gate.shape[0]
    assert (TOP_K * n_tok) % blk == 0
    n_tiles = TOP_K * n_tok + (n_experts + 1) * blk
    grid_spec = pltpu.PrefetchScalarGridSpec(
        num_scalar_prefetch=3,
        grid=(nb,),
        in_specs=[pl.BlockSpec(memory_space=pl.ANY),
                  pl.BlockSpec((1, d, f), lambda i, be, nu, sm: (be[i], 0, 0)),
                  pl.BlockSpec((1, d, f), lambda i, be, nu, sm: (be[i], 0, 0)),
                  pl.BlockSpec((1, f, d), lambda i, be, nu, sm: (be[i], 0, 0))],
        out_specs=pl.BlockSpec(memory_space=pl.ANY),
        scratch_shapes=[pltpu.VMEM((n_tok * SUB, LANES), U32),
                        pltpu.VMEM((2, blk * SUB, LANES), U32),
                        pltpu.VMEM((2, blk * SUB, LANES), U32),
                        pltpu.VMEM((d, f), BF16),
                        pltpu.VMEM((d, f), BF16),
                        pltpu.VMEM((f, d), BF16),
                        pltpu.SemaphoreType.DMA((2,)),
                        pltpu.SemaphoreType.DMA],
    )
    return pl.pallas_call(
        functools.partial(_moe_expert_gather_kernel, n_tok),
        out_shape=jax.ShapeDtypeStruct((n_tiles * SUB, LANES), U32),
        grid_spec=grid_spec,
        compiler_params=_cparams(("arbitrary",)),
        name="moe_experts",
    )(blk_e, n_used, smap, h2, w_gate, w_up, w_down)


def _moe_combine_kernel(x1_ref, routet_ref, gt2_ref, fg_ref, y1_ref, y2_ref, o_ref):
    tm = x1_ref.shape[0]
    route = routet_ref[...].T
    moe = route[:, 0:1] * _tiles_to_rows(y1_ref, tm) + route[:, 1:2] * _tiles_to_rows(y2_ref, tm)
    xo = x1_ref[...] + gt2_ref[0] * moe
    ms = jnp.mean(xo * xo, axis=-1, keepdims=True)
    o_ref[...] = xo * lax.rsqrt(ms + NORM_EPS) * fg_ref[...]


def _moe_combine(x1, route_t, yt, mod3, final_g, s, tm):
    n, d = x1.shape
    tiles_per_batch = s // tm
    n_steps = n // tm
    per_o = route_t.shape[1] // tm
    return pl.pallas_call(
        _moe_combine_kernel,
        out_shape=jax.ShapeDtypeStruct((n, d), F32),
        grid=(n_steps,),
        in_specs=[pl.BlockSpec((tm, d), lambda i: (i, 0)),
                  pl.BlockSpec((SUBLANES, tm), lambda i: (i // per_o, i % per_o)),
                  pl.BlockSpec((1, 1, d), lambda i: ((i // tiles_per_batch) * 6 + 5, 0, 0)),
                  pl.BlockSpec((1, d), lambda i: (0, 0)),
                  pl.BlockSpec((tm * SUB, LANES), lambda i: (i, 0)),
                  pl.BlockSpec((tm * SUB, LANES), lambda i: (n_steps + i, 0))],
        out_specs=pl.BlockSpec((tm, d), lambda i: (i, 0)),
        compiler_params=_cparams(("arbitrary",)),
        name="moe_combine",
    )(x1, route_t, mod3, final_g.reshape(1, d), yt, yt)


def _pick(n, candidates):
    for t in candidates:
        if n % t == 0:
            return t
    raise ValueError(f"no tile in {candidates} divides {n}")


def kernel(x, c, ada_w, ada_b, norm1_g, w_in, hg_lb, hg_norm_g, rw_mu, rw_w0, rw_w2, rw_a0, rw_a2, rw_g2, rw_kk, rw_ka, rw_rk, rw_gn_w, rw_gn_b, w_proj_a, w_proj_b, w_out, norm2_g, router_g_w, router_g_b, router_e_w, router_e_b, exp_w_gate, exp_w_up, exp_w_down, final_g):
    bsz, s, d = x.shape
    depth = ada_w.shape[0]
    hg_f = hg_lb.shape[-1]
    hg_w = hg_norm_g.shape[-1]
    rw_w = rw_w0.shape[-1]
    rw_cols = rw_mu.shape[-1]
    n_groups = router_g_w.shape[-1]
    n_experts = router_e_w.shape[-1]
    assert hg_f == hg_w and s % CHUNK == 0 and n_experts + n_groups <= LANES and d == 2 * SUB * LANES

    lb_all = jnp.cumsum(jax.nn.softmax(hg_lb.astype(F32), axis=0), axis=0)
    n = bsz * s
    blk = 512
    n_blocks = (n * TOP_K + n_experts * blk) // blk
    for l in range(depth):
        mod = _ada_mod(c, ada_w[l], ada_b[l])
        mod3 = mod.reshape(bsz * 6, 1, d)

        hg_cols = 2 * hg_f + 2 * hg_w
        rw_pad = -(-rw_cols // 256) * 256
        wl = w_in[l]
        w_hg = wl[:, :hg_cols].astype(BF16)
        w_rw = jnp.zeros((d, rw_pad), BF16).at[:, :rw_cols].set(wl[:, hg_cols:hg_cols + rw_cols].astype(BF16))
        w_gt = wl[:, hg_cols + rw_cols:].astype(BF16)
        lora = (rw_w2.shape[1], rw_a2.shape[1], rw_g2.shape[1])
        assert rw_pad - 3 * rw_w == 256 and rw_cols == 3 * rw_w + sum(lora)
        hg, rw, gates = _in_proj(x, mod3, norm1_g[l], lb_all[l], rw_mu[l], lora, w_hg, w_rw, w_gt,
                                 _pick(s, (512, 256, 128, 64)))

        o_a = _hgrn2(hg, hg_norm_g[l], _pick(s, (1024, 512, 256, 128, 64)))
        o_b = _rwkv7(rw, rw_w0[l], rw_w2[l], rw_a0[l], rw_a2[l], rw_g2[l],
                     rw_kk[l], rw_ka[l], rw_rk[l].reshape(-1), rw_gn_w[l], rw_gn_b[l],
                     _pick(s, (2 * CHUNK, CHUNK)))

        wr = jnp.zeros((d, LANES), F32).at[:, :n_experts].set(router_e_w[l])
        wr = wr.at[:, n_experts:n_experts + n_groups].set(router_g_w[l])
        br = jnp.zeros((1, LANES), F32).at[0, :n_experts].set(router_e_b[l])
        br = br.at[0, n_experts:n_experts + n_groups].set(router_g_b[l])
        tm_o = _pick(s, (1024, 512, 256, 128, 64))
        x1, h2, route_t, counts = _out_proj(
            x, o_a, o_b, gates, mod3, norm2_g[l],
            w_proj_a[l].astype(BF16), w_proj_b[l].astype(BF16), w_out[l].astype(BF16),
            wr, br, n_groups, n_experts, tm_o)

        rt = route_t.reshape(n // tm_o, SUBLANES, tm_o)
        eid = rt[:, 2:4, :].astype(jnp.int32)
        rank = rt[:, 4:6, :].astype(jnp.int32)
        cnt = counts[:n_experts, 0].astype(jnp.int32)
        padded = (cnt + blk - 1) // blk * blk
        pad_end = jnp.cumsum(padded)
        pad_start = pad_end - padded
        e_ax = jnp.arange(n_experts, dtype=jnp.int32)[:, None, None, None]
        dest = rank + jnp.sum(jnp.where(eid[None] == e_ax, pad_start[:, None, None, None], 0), axis=0)
        blk_start = jnp.arange(n_blocks, dtype=jnp.int32) * blk
        blk_e = jnp.minimum(jnp.sum((pad_end[None, :] <= blk_start[:, None]).astype(jnp.int32), axis=1),
                            n_experts - 1)
        n_used = (pad_end[-1:] // blk).astype(jnp.int32)
        tm = tm_o
        smap = _moe_slotmap(dest.reshape(-1) + blk, pad_start + cnt, padded - cnt, n_used, n_blocks * blk, blk, tm_o, n)
        yt = _moe_experts_gather(h2, smap, blk_e, n_used, exp_w_gate[l], exp_w_up[l], exp_w_down[l], blk, n)
        last = l == depth - 1
        assert last, "the final RMSNorm is fused into the last layer's combine"
        out = _moe_combine(x1.reshape(n, d), route_t, yt, mod3, final_g, s, tm)
        x = out.reshape(bsz, s, d)
    return x
```

```python
import functools

import numpy as np
import jax
import jax.numpy as jnp
from jax import lax
from jax.experimental import pallas as pl
from jax.experimental.pallas import tpu as pltpu

F32 = jnp.float32
BF16 = jnp.bfloat16

NORM_EPS = 1e-6
HG_HEAD = 128
RW_HEAD = 64
RW_GN_EPS = 64e-5
TOP_K = 2
CHUNK = 64
LANES = 128
SUB = 4
U32 = jnp.uint32
MXU_K = 256
SUBLANES = 8
SMAP_BITS = 16
VMEM_LIMIT = 56 * 1024 * 1024

NT = (((1,), (1,)), ((), ()))
TN = (((0,), (0,)), ((), ()))


def _dot(a, b, dims=None, precision=None):
    if dims is None:
        return jnp.dot(a, b, preferred_element_type=F32, precision=precision)
    return lax.dot_general(a, b, dims, preferred_element_type=F32, precision=precision)


def _split3(x):
    hi = x.astype(BF16)
    r1 = x - hi.astype(F32)
    mid = r1.astype(BF16)
    lo = (r1 - mid.astype(F32)).astype(BF16)
    return hi, mid, lo


def _dot_exact_lhs(m3_bf16, x):
    return _dot(m3_bf16, jnp.concatenate(_split3(x), axis=0))


def _head_sums(x, m2_bf16):
    outs = []
    for g0 in range(0, x.shape[1], MXU_K):
        xg = x[:, g0:g0 + MXU_K]
        hi = xg.astype(BF16)
        lo = (xg - hi.astype(F32)).astype(BF16)
        outs.append(_dot(jnp.concatenate([hi, lo], axis=1), m2_bf16))
    return jnp.concatenate(outs, axis=1)


def _sigmoid(x):
    return 1.0 / (1.0 + jnp.exp(-x))


def _silu(x):
    return x * _sigmoid(x)


def _rows_to_tiles(ref, val):
    m, half = val.shape[0], val.shape[1] // 2
    hi = lax.bitcast_convert_type(val[:, :half].astype(BF16).astype(F32), U32)
    lo = lax.bitcast_convert_type(val[:, half:].astype(BF16).astype(F32), U32)
    w = (hi & jnp.uint32(0xFFFF0000)) | (lo >> 16)
    for j in range(SUB):
        ref[pl.ds(j, m, stride=SUB), :] = w[:, j * LANES:(j + 1) * LANES]


def _tiles_to_rows(ref, m, base=0):
    w = jnp.concatenate([ref[pl.ds(base * SUB + j, m, stride=SUB), :] for j in range(SUB)], axis=1)
    hi = lax.bitcast_convert_type(w & jnp.uint32(0xFFFF0000), F32)
    lo = lax.bitcast_convert_type(w << 16, F32)
    return jnp.concatenate([hi, lo], axis=1)


def _cparams(sem):
    return pltpu.CompilerParams(dimension_semantics=sem, vmem_limit_bytes=VMEM_LIMIT)


def _ada_kernel(cb_ref, w_ref, b_ref, o_ref, sc_ref):
    bsz, d = cb_ref.shape[0], cb_ref.shape[1]
    tn = w_ref.shape[1]
    reps = tn // LANES

    @pl.when(pl.program_id(0) == 0)
    def _():
        sc_ref[...] = _silu(cb_ref[...])

    def body(i, accs):
        k0 = pl.multiple_of(i * SUBLANES, SUBLANES)
        w = w_ref[pl.ds(k0, SUBLANES), :]
        out = []
        for b in range(bsz):
            sc = sc_ref[b, pl.ds(k0, SUBLANES), :]
            out.append(accs[b] + w * jnp.concatenate([sc] * reps, axis=1))
        return tuple(out)

    accs = lax.fori_loop(0, d // SUBLANES, body, tuple(jnp.zeros((SUBLANES, tn), F32) for _ in range(bsz)),
                         unroll=4)
    o_ref[...] = jnp.zeros_like(o_ref)
    for b in range(bsz):
        o_ref[b:b + 1, :] = jnp.sum(accs[b], axis=0, keepdims=True) + b_ref[...]


def _ada_mod(c, w, b):
    bsz, d = c.shape
    n = w.shape[1]
    rows = -(-bsz // SUBLANES) * SUBLANES
    cb = jnp.broadcast_to(c[:, :, None], (bsz, d, LANES))
    tn = _pick(n, (1024, 512, 256, 128))
    out = pl.pallas_call(
        _ada_kernel,
        out_shape=jax.ShapeDtypeStruct((rows, n), F32),
        grid=(n // tn,),
        in_specs=[pl.BlockSpec((bsz, d, LANES), lambda j: (0, 0, 0)),
                  pl.BlockSpec((d, tn), lambda j: (0, j)),
                  pl.BlockSpec((1, tn), lambda j: (0, j))],
        out_specs=pl.BlockSpec((rows, tn), lambda j: (0, j)),
        scratch_shapes=[pltpu.VMEM((bsz, d, LANES), F32)],
        compiler_params=_cparams(("arbitrary",)),
        name="ada_mod",
    )(cb, w, b.reshape(1, n))
    return out[:bsz]


def _in_proj_kernel(lora, x_ref, sh_ref, sc_ref, g_ref, lb_ref, mu_ref, whg_ref, wrw_ref, wgt_ref,
                    hg_ref, rw_ref, gt_ref, carry_ref):
    @pl.when(pl.program_id(1) == 0)
    def _():
        carry_ref[...] = jnp.zeros_like(carry_ref)

    x = x_ref[0]
    ms = jnp.mean(x * x, axis=-1, keepdims=True)
    h = (x * lax.rsqrt(ms + NORM_EPS) * g_ref[...]) * (1.0 + sc_ref[0]) + sh_ref[0]
    hb = h.astype(BF16)
    lb = lb_ref[...]
    w = lb.shape[1]
    hg_maps = (_silu, lambda t: lb + (1.0 - lb) * _sigmoid(t), lambda t: t, _silu)
    for part, fn in enumerate(hg_maps):
        hg_ref[0, :, part * w:(part + 1) * w] = fn(_dot(hb, whg_ref[:, part * w:(part + 1) * w]))
    step = 512
    n_rw = wrw_ref.shape[1]
    row = lax.broadcasted_iota(jnp.int32, (x.shape[0], 256), 0)
    lane = lax.broadcasted_iota(jnp.int32, (x.shape[0], 256), 1)
    dl, al, gl = lora
    for n0 in range(0, n_rw, 256):
        cs = slice(n0, n0 + 256)
        p = _dot(hb, wrw_ref[:, cs])
        prev = jnp.where(row == 0, carry_ref[:, cs], pltpu.roll(p, 1, 0))
        carry_ref[:, cs] = p[x.shape[0] - 1:, :]
        xs = p + mu_ref[:, cs] * (prev - p)
        if n0 == n_rw - 256:
            xs = jnp.where(lane < dl, jnp.tanh(xs),
                           jnp.where(lane < dl + al, xs, jnp.where(lane < dl + al + gl, _sigmoid(xs), 0.0)))
        rw_ref[0, :, cs] = xs
    for n0 in range(0, wgt_ref.shape[1], step):
        gt_ref[0, :, n0:n0 + step] = _sigmoid(_dot(hb, wgt_ref[:, n0:n0 + step])).astype(BF16)


def _w_in_split_kernel(n_hg, n_rw, w_ref, hg_ref, rw_ref, gt_ref):
    hg_ref[...] = w_ref[0, :, :n_hg].astype(BF16)
    tail = n_rw // LANES * LANES
    if tail < rw_ref.shape[1]:
        rw_ref[:, tail:] = jnp.zeros((rw_ref.shape[0], rw_ref.shape[1] - tail), BF16)
    rw_ref[:, :n_rw] = w_ref[0, :, n_hg:n_hg + n_rw].astype(BF16)
    gt_ref[...] = w_ref[0, :, n_hg + n_rw:].astype(BF16)


def _w_in_split(w_in, layer, n_hg, n_rw, n_rw_pad):
    _, d, n_in = w_in.shape
    n_gt = n_in - n_hg - n_rw
    tr = _pick(d, (128, 64, 32, 16))
    return pl.pallas_call(
        functools.partial(_w_in_split_kernel, n_hg, n_rw),
        out_shape=(jax.ShapeDtypeStruct((d, n_hg), BF16),
                   jax.ShapeDtypeStruct((d, n_rw_pad), BF16),
                   jax.ShapeDtypeStruct((d, n_gt), BF16)),
        grid=(d // tr,),
        in_specs=[pl.BlockSpec((1, tr, n_in), lambda i: (layer, i, 0))],
        out_specs=(pl.BlockSpec((tr, n_hg), lambda i: (i, 0)),
                   pl.BlockSpec((tr, n_rw_pad), lambda i: (i, 0)),
                   pl.BlockSpec((tr, n_gt), lambda i: (i, 0))),
        compiler_params=_cparams(("arbitrary",)),
        name="w_in_split",
    )(w_in)


def _in_proj(x, mod3, norm_g, lb, mu, lora, w_hg, w_rw, w_gt, tm):
    bsz, s, d = x.shape
    assert w_hg.shape[1] == 4 * lb.shape[0] and sum(lora) <= 256
    mup = jnp.zeros((1, w_rw.shape[1]), F32).at[0, :mu.shape[-1]].set(mu)
    n_hg, n_rw, n_gt = w_hg.shape[1], w_rw.shape[1], w_gt.shape[1]
    const = lambda b, i: (0, 0)
    return pl.pallas_call(
        functools.partial(_in_proj_kernel, lora),
        out_shape=(jax.ShapeDtypeStruct((bsz, s, n_hg), F32),
                   jax.ShapeDtypeStruct((bsz, s, n_rw), F32),
                   jax.ShapeDtypeStruct((bsz, s, n_gt), BF16)),
        grid=(bsz, s // tm),
        in_specs=[pl.BlockSpec((1, tm, d), lambda b, i: (b, i, 0)),
                  pl.BlockSpec((1, 1, d), lambda b, i: (b * 6 + 0, 0, 0)),
                  pl.BlockSpec((1, 1, d), lambda b, i: (b * 6 + 1, 0, 0)),
                  pl.BlockSpec((1, d), const),
                  pl.BlockSpec((1, lb.shape[0]), const),
                  pl.BlockSpec((1, n_rw), const),
                  pl.BlockSpec((d, n_hg), const),
                  pl.BlockSpec((d, n_rw), const),
                  pl.BlockSpec((d, n_gt), const)],
        out_specs=(pl.BlockSpec((1, tm, n_hg), lambda b, i: (b, i, 0)),
                   pl.BlockSpec((1, tm, n_rw), lambda b, i: (b, i, 0)),
                   pl.BlockSpec((1, tm, n_gt), lambda b, i: (b, i, 0))),
        scratch_shapes=[pltpu.VMEM((1, n_rw), F32)],
        compiler_params=_cparams(("arbitrary", "arbitrary")),
        name="in_proj",
    )(x, mod3, mod3, norm_g.reshape(1, d), lb.reshape(1, -1), mup, w_hg, w_rw, w_gt)


_HG_LEVELS = (32, 16, 8, 4, 2, 1)


def _hgrn2_consts(width):
    c = CHUNK
    t = np.arange(c)[:, None]
    s = np.arange(c)[None, :]
    blocks = [(s <= t), (s > t)]
    lvl_masks = []
    right = []
    for h in _HG_LEVELS:
        m = (t // (2 * h)) * 2 * h + h
        is_r = (t & h) != 0
        blk = np.where(is_r, (s >= m) & (s <= t), (s > t) & (s <= m - 1))
        blocks.append(blk)
        lvl_masks.append(is_r & ((s & h) == 0) & ((t // (2 * h)) == (s // (2 * h))))
        right.append(np.broadcast_to(is_r, (c, width)))
    mst = np.tile(np.concatenate(blocks, axis=0).astype(np.float32), (1, 3))
    lm = np.stack([np.eye(c, dtype=bool)] + lvl_masks).astype(np.float32)
    rm = np.stack(right).astype(np.float32)
    return jnp.asarray(mst, BF16), jnp.asarray(lm, F32), jnp.asarray(rm, F32)


def _hgrn2_kernel(q_ref, f_ref, i_ref, g_ref, ng_ref, mst_ref, lm_ref, rm_ref, o_ref, st_ref):
    c = CHUNK
    n_chunks = q_ref.shape[1] // c

    @pl.when(pl.program_id(1) == 0)
    def _():
        st_ref[...] = jnp.zeros_like(st_ref)

    nsub = next(n for n in (8, 4, 2, 1) if n_chunks % n == 0)

    def chunk_body(ci, carry):
        r0 = pl.multiple_of(ci * (nsub * c), nsub * c)
        for _ in _hgrn2_steps(q_ref, f_ref, i_ref, g_ref, ng_ref, mst_ref, lm_ref, rm_ref, o_ref, st_ref,
                              0, r0, nsub):
            pass
        return carry

    lax.fori_loop(0, n_chunks // nsub, chunk_body, 0)


def _hgrn2_steps(q_ref, f_ref, i_ref, g_ref, ng_ref, mst_ref, lm_ref, rm_ref, o_ref, st_ref, b, r0, nsub):
    c = CHUNK
    n_heads = q_ref.shape[2] // HG_HEAD
    mst = mst_ref[...]
    ng = ng_ref[...]
    heads = [slice(hd * HG_HEAD, (hd + 1) * HG_HEAD) for hd in range(n_heads)]
    subs = []
    for j in range(nsub):
        rows = pl.ds(r0 + j * c, c)
        q = q_ref[b, rows, :]
        f = f_ref[b, rows, :]
        k = 1.0 - f
        ex = jnp.exp2(_dot_exact_lhs(mst, jnp.log2(f)))
        subs.append(dict(rows=rows, q=q, k=k, ex=ex, vb=i_ref[b, rows, :].astype(BF16),
                         qd=(q * ex[0:c]).astype(BF16), kr=(k * ex[c:2 * c]).astype(BF16)))
        yield
    for sb in subs:
        qb, kb = sb['q'].astype(BF16), sb['k'].astype(BF16)
        sb['sc'] = [lm_ref[0] * _dot(qb[:, ls], kb[:, ls], NT) for ls in heads]
        sb['dqk'] = sb['q'] - sb['k']
    yield
    for li, h in enumerate(_HG_LEVELS):
        for sb in subs:
            if h % SUBLANES == 0:
                qk = jnp.concatenate([(sb['q'] if m % 2 else sb['k'])[m * h:(m + 1) * h] for m in range(c // h)],
                                     axis=0)
            else:
                qk = sb['k'] + rm_ref[li] * sb['dqk']
            g_l = (qk * sb['ex'][(2 + li) * c:(3 + li) * c]).astype(BF16)
            sb['sc'] = [s_h + lm_ref[li + 1] * _dot(g_l[:, ls], g_l[:, ls], NT)
                        for s_h, ls in zip(sb['sc'], heads)]
        yield
    for sb in subs:
        sb['kv'] = [_dot(sb['vb'][:, ls], sb['kr'][:, ls], TN) for ls in heads]
        sb['o'] = [_dot(s_h.astype(BF16), sb['vb'][:, ls]) for s_h, ls in zip(sb['sc'], heads)]
        yield
    sts = [st_ref[b, hd] for hd in range(n_heads)]
    for sb in subs:
        sb['o'] = [o_h + _dot(sb['qd'][:, ls], st.astype(BF16), NT) for o_h, ls, st in zip(sb['o'], heads, sts)]
        sts = [st * sb['ex'][c - 1:c, ls] + kv for st, ls, kv in zip(sts, heads, sb['kv'])]
        yield
    for hd in range(n_heads):
        st_ref[b, hd] = sts[hd]
    for sb in subs:
        on = [o_h * lax.rsqrt(jnp.mean(o_h * o_h, axis=-1, keepdims=True) + NORM_EPS) for o_h in sb['o']]
        o_full = jnp.concatenate(on, axis=1) * ng
        o_ref[b, sb['rows'], :] = (o_full * g_ref[b, sb['rows'], :]).astype(o_ref.dtype)
        yield


def _hgrn2(hg, norm_g, ts):
    bsz, s, n4 = hg.shape
    w = n4 // 4
    mst, lm, rm = _hgrn2_consts(w)
    n_heads = w // HG_HEAD
    const2 = lambda b, i: (0, 0)
    const3 = lambda b, i: (0, 0, 0)
    return pl.pallas_call(
        _hgrn2_kernel,
        out_shape=jax.ShapeDtypeStruct((bsz, s, w), BF16),
        grid=(bsz, s // ts),
        in_specs=[pl.BlockSpec((1, ts, w), lambda b, i: (b, i, 0)),
                  pl.BlockSpec((1, ts, w), lambda b, i: (b, i, 1)),
                  pl.BlockSpec((1, ts, w), lambda b, i: (b, i, 2)),
                  pl.BlockSpec((1, ts, w), lambda b, i: (b, i, 3)),
                  pl.BlockSpec((1, w), const2),
                  pl.BlockSpec(mst.shape, const2),
                  pl.BlockSpec(lm.shape, const3),
                  pl.BlockSpec(rm.shape, const3)],
        out_specs=pl.BlockSpec((1, ts, w), lambda b, i: (b, i, 0)),
        scratch_shapes=[pltpu.VMEM((1, n_heads, HG_HEAD, HG_HEAD), F32)],
        compiler_params=_cparams(("arbitrary", "arbitrary")),
        name="hgrn2",
    )(hg, hg, hg, hg, norm_g.reshape(1, w), mst, lm, rm)


def _rwkv_consts(width):
    c = CHUNK
    t = np.arange(c)[:, None]
    s = np.arange(c)[None, :]
    tri = np.tile((s <= t).astype(np.float32), (1, 3))
    tt = np.arange(2 * c)[:, None]
    ss = np.arange(2 * c)[None, :]
    same = (tt // c) == (ss // c)
    strict = same & ((ss % c) < (tt % c))
    incl = same & ((ss % c) <= (tt % c))
    hsum = (np.arange(MXU_K)[:, None] // RW_HEAD) == (np.arange(MXU_K)[None, :] // RW_HEAD)
    hsum = np.tile(hsum, (2, 1))
    return (jnp.asarray(tri, BF16), jnp.asarray(strict.astype(np.float32), F32),
            jnp.asarray(incl.astype(np.float32), F32), jnp.asarray(hsum.astype(np.float32), BF16))


def _rwkv7_kernel(p_ref, w0_ref, a0_ref, kk_ref, ka_ref, rk_ref, gnw_ref, gnb_ref,
                  w2_ref, a2_ref, g2_ref, tri_ref, sm_ref, im_ref, hs_ref,
                  o_ref, zt_ref):
    @pl.when(pl.program_id(0) == 0)
    def _():
        zt_ref[...] = jnp.zeros_like(zt_ref)

    for _ in _rwkv7_steps(p_ref, w0_ref, a0_ref, kk_ref, ka_ref, rk_ref, gnw_ref, gnb_ref,
                          w2_ref, a2_ref, g2_ref, tri_ref, sm_ref, im_ref, hs_ref, o_ref, zt_ref):
        pass


def _rwkv7_steps(p_ref, w0_ref, a0_ref, kk_ref, ka_ref, rk_ref, gnw_ref, gnb_ref,
                 w2_ref, a2_ref, g2_ref, tri_ref, sm_ref, im_ref, hs_ref, o_ref, zt_ref):
    c = CHUNK
    nb = p_ref.shape[0]
    nch = p_ref.shape[1] // c
    width = o_ref.shape[2]
    n_pairs = width // LANES

    hs = hs_ref[...]
    tri = tri_ref[...]
    smask = sm_ref[...] > 0
    imask = im_ref[...] > 0
    lane = lax.broadcasted_iota(jnp.int32, (c, LANES), 1)
    m0 = (lane < RW_HEAD).astype(F32)
    m1 = 1.0 - m0

    def stack(x):
        return jnp.concatenate([x * m0, x * m1], axis=0)

    xs = jnp.concatenate([p_ref[b] for b in range(nb)], axis=0)
    r_all = xs[:, 0:width]
    k_all = xs[:, width:2 * width]
    v_all = xs[:, 2 * width:3 * width]
    slab = xs[:, 3 * width:].astype(BF16)
    nz = -(w0_ref[...] + _dot(slab, w2_ref[...]))
    softplus = jnp.maximum(nz, 0.0) + jnp.log(1.0 + jnp.exp(-jnp.abs(nz)))
    ld_all = -jnp.exp(-softplus - 0.5)
    a_all = _sigmoid(a0_ref[...] + _dot(slab, a2_ref[...]))
    g_all = _dot(slab, g2_ref[...])
    kk0 = k_all * kk_ref[...]
    kk_all = kk0 * lax.rsqrt(jnp.maximum(_head_sums(kk0 * kk0, hs), 1e-24))
    k2_all = k_all * (1.0 + (a_all - 1.0) * ka_ref[...])
    yield

    units = []
    for b, j in [(b, j) for b in range(nb) for j in range(nch)]:
        rb = slice((b * nch + j) * c, (b * nch + j + 1) * c)
        r, k2, v, ld = r_all[rb], k2_all[rb], v_all[rb], ld_all[rb]
        a_in = -kk_all[rb]
        b_in = kk_all[rb] * a_all[rb]
        cum = _dot_exact_lhs(tri, ld)
        cum_t = cum[c - 1:c, :]
        e_c = jnp.exp(cum)
        e_nc = jnp.exp(-cum)
        e_rem = jnp.exp(cum_t - cum)
        at_f = a_in * jnp.exp(cum - ld)
        rt_f = r * e_c
        kt_f = k2 * e_nc
        bt_f = b_in * e_nc
        kh_f = k2 * e_rem
        bh_f = b_in * e_rem
        p_t = jnp.exp(cum_t)
        for pi in range(n_pairs):
            ls = slice(pi * LANES, (pi + 1) * LANES)
            units.append(dict(
                b=b, j=j, pi=pi,
                at=stack(at_f[:, ls]).astype(BF16), rt=stack(rt_f[:, ls]).astype(BF16),
                kt=stack(kt_f[:, ls]).astype(BF16), bt=stack(bt_f[:, ls]).astype(BF16),
                kh=stack(kh_f[:, ls]).astype(BF16), bh=stack(bh_f[:, ls]).astype(BF16),
                vs=stack(v[:, ls]).astype(BF16), p_t=p_t[:, ls]))
        yield

    for u in units:
        lhs = jnp.concatenate([u['at'], u['rt']], axis=0)
        u['g'] = _dot(lhs, jnp.concatenate([u['kt'], u['bt']], axis=0), NT)
    yield
    for u in units:
        g = u.pop('g')
        u['a_ak'] = jnp.where(smask, g[:2 * c, :2 * c], 0.0).astype(BF16)
        u['pw'] = jnp.where(smask, g[:2 * c, 2 * c:], 0.0).astype(BF16)
        u['a_r'] = jnp.where(jnp.concatenate([imask, imask], axis=1), g[2 * c:], 0.0).astype(BF16)
    for u in units:
        akv = _dot(u.pop('a_ak'), u['vs'])
        u['x'] = jnp.concatenate([u['at'].astype(F32), akv], axis=1)
    yield
    n_lvl = int(np.log2(c))
    for lvl in range(n_lvl):
        for u in units:
            u['x'] = u['x'] + _dot(u['pw'], u['x'].astype(BF16))
        yield
        if lvl + 1 < n_lvl:
            for u in units:
                u['pw'] = _dot(u['pw'], u['pw']).astype(BF16)
            yield
    for u in units:
        x = u.pop('x')
        u['wr'] = jnp.concatenate([x[:, :LANES].astype(BF16), u['rt']], axis=0)
        u['u_loc'] = x[:, LANES:]
    zt = {(b, pi): zt_ref[b, pi] for b in range(nb) for pi in range(n_pairs)}
    for j in range(nch):
        tail = [u for u in units if u['j'] == j]
        for u in tail:
            u['uy'] = _dot(u.pop('wr'), zt[u['b'], u['pi']].astype(BF16), NT)
        yield
        for u in tail:
            uy = u.pop('uy')
            u['u'] = (uy[:2 * c] + u.pop('u_loc')).astype(BF16)
            u['y0'] = uy[2 * c:]
        for u in tail:
            vu = jnp.concatenate([u['vs'], u['u']], axis=0)
            u['y'] = u.pop('y0') + _dot(u['a_r'], vu)
            upd = _dot(vu, jnp.concatenate([u['kh'], u['bh']], axis=0), TN)
            zt[u['b'], u['pi']] = zt[u['b'], u['pi']] * u['p_t'] + upd
        yield
    for (b, pi), z in zt.items():
        zt_ref[b, pi] = z

    inv_n = 1.0 / RW_HEAD
    y = jnp.concatenate(
        [jnp.concatenate([u['y'][:c] + u['y'][c:] for u in units if (u['b'], u['j']) == (b, j)], axis=1)
         for b in range(nb) for j in range(nch)], axis=0)
    mean = _head_sums(y, hs) * inv_n
    yield
    d = y - mean
    var = _head_sums(d * d, hs) * inv_n
    yield
    yn = d * lax.rsqrt(var + RW_GN_EPS) * gnw_ref[...] + gnb_ref[...]
    bonus = _head_sums(r_all * k2_all * rk_ref[...], hs) * v_all
    out = ((yn + bonus) * g_all).astype(o_ref.dtype)
    for b in range(nb):
        o_ref[b] = out[b * nch * c:(b + 1) * nch * c]


def _rwkv7(rw, w0, w2, a0, a2, g2, k_k, k_a, r_k, gn_w, gn_b, ts):
    bsz, s, cols = rw.shape
    width = w0.shape[-1]
    n_pairs = width // LANES
    slab = cols - 3 * width
    dl, al, gl = w2.shape[0], a2.shape[0], g2.shape[0]
    w2f = jnp.zeros((slab, width), F32).at[0:dl].set(w2).astype(BF16)
    a2f = jnp.zeros((slab, width), F32).at[dl:dl + al].set(a2).astype(BF16)
    g2f = jnp.zeros((slab, width), F32).at[dl + al:dl + al + gl].set(g2).astype(BF16)
    tri, sm, im, hs = _rwkv_consts(width)
    row = lambda x: x.reshape(1, width)
    const = lambda i: (0, 0)
    vec = pl.BlockSpec((1, width), const)
    return pl.pallas_call(
        _rwkv7_kernel,
        out_shape=jax.ShapeDtypeStruct((bsz, s, width), BF16),
        grid=(s // ts,),
        in_specs=[pl.BlockSpec((bsz, ts, cols), lambda i: (0, i, 0)),
                  vec, vec, vec, vec, vec, vec, vec,
                  pl.BlockSpec((slab, width), const),
                  pl.BlockSpec((slab, width), const),
                  pl.BlockSpec((slab, width), const),
                  pl.BlockSpec(tri.shape, const),
                  pl.BlockSpec(sm.shape, const),
                  pl.BlockSpec(im.shape, const),
                  pl.BlockSpec(hs.shape, const)],
        out_specs=pl.BlockSpec((bsz, ts, width), lambda i: (0, i, 0)),
        scratch_shapes=[pltpu.VMEM((bsz, n_pairs, LANES, LANES), F32)],
        compiler_params=_cparams(("arbitrary",)),
        name="rwkv7",
    )(rw, row(w0), row(a0), row(k_k), row(k_a), row(r_k), row(gn_w), row(gn_b),
      w2f, a2f, g2f, tri, sm, im, hs)


def _out_proj_kernel(n_groups, n_experts,
                     x_ref, oa_ref, ob_ref, ga_ref, gb_ref, gt1_ref, sc2_ref, sh2_ref, g2_ref,
                     wa_ref, wb_ref, wo_ref, wr_ref, wrl_ref, br_ref, upper_ref,
                     x1_ref, h2_ref, routet_ref, cnt_ref, carry_ref):
    first = (pl.program_id(0) == 0) & (pl.program_id(1) == 0)

    @pl.when(first)
    def _():
        carry_ref[...] = jnp.zeros_like(carry_ref)

    pa = _dot(oa_ref[0], wa_ref[...])
    pb = _dot(ob_ref[0], wb_ref[...])
    mixed = ga_ref[0].astype(F32) * pa + gb_ref[0].astype(F32) * pb
    x1 = x_ref[0] + gt1_ref[0] * _dot(mixed.astype(BF16), wo_ref[...])
    x1_ref[0] = x1
    ms = jnp.mean(x1 * x1, axis=-1, keepdims=True)
    h2 = (x1 * lax.rsqrt(ms + NORM_EPS) * g2_ref[...]) * (1.0 + sc2_ref[0]) + sh2_ref[0]
    _rows_to_tiles(h2_ref, h2)

    h2_hi = h2.astype(BF16)
    h2_lo = (h2 - h2_hi.astype(F32)).astype(BF16)
    logits = (_dot(wr_ref[...], h2_hi, NT) + _dot(wr_ref[...], h2_lo, NT) + _dot(wrl_ref[...], h2_hi, NT)
              + br_ref[...])
    row = lax.broadcasted_iota(jnp.int32, logits.shape, 0)
    neg = jnp.float32(-jnp.inf)
    big = jnp.int32(1 << 20)
    eg = n_experts // n_groups
    is_g = (row >= n_experts) & (row < n_experts + n_groups)
    lg = jnp.where(is_g, logits, neg)
    mg = jnp.max(lg, axis=0, keepdims=True)
    p_grp = 1.0 / jnp.sum(jnp.where(is_g, jnp.exp(lg - mg), 0.0), axis=0, keepdims=True)
    gidx = jnp.min(jnp.where(lg == mg, row, big), axis=0, keepdims=True) - n_experts
    sel = (row >= gidx * eg) & (row < gidx * eg + eg)
    le = jnp.where(sel, logits, neg)
    me = jnp.max(le, axis=0, keepdims=True)
    pe_un = jnp.where(sel, jnp.exp(le - me), 0.0)
    pe = jnp.where(sel, pe_un / jnp.sum(pe_un, axis=0, keepdims=True), -1.0)
    v1 = jnp.max(pe, axis=0, keepdims=True)
    i1 = jnp.min(jnp.where(pe == v1, row, big), axis=0, keepdims=True)
    pe2 = jnp.where(row == i1, -1.0, pe)
    v2 = jnp.max(pe2, axis=0, keepdims=True)
    i2 = jnp.min(jnp.where(pe2 == v2, row, big), axis=0, keepdims=True)
    wsum = v1 + v2
    w1 = p_grp * v1 / wsum
    w2 = p_grp * v2 / wsum

    oh1 = (row == i1).astype(F32)
    oh2 = (row == i2).astype(F32)
    both = oh1 + oh2
    before = _dot(both.astype(BF16), upper_ref[...]) + carry_ref[...]
    rank1 = jnp.sum(oh1 * before, axis=0, keepdims=True)
    rank2 = jnp.sum(oh2 * before, axis=0, keepdims=True)
    carry_ref[...] = carry_ref[...] + jnp.sum(both, axis=1, keepdims=True)
    cnt_ref[...] = carry_ref[...]
    zero = jnp.zeros_like(w1)
    routet_ref[...] = jnp.concatenate(
        [w1, w2, i1.astype(F32), i2.astype(F32), rank1, rank2, zero, zero], axis=0)


def _out_proj(x, o_a, o_b, gates, mod3, norm2_g, wa, wb, wo, wr, br, n_groups, n_experts, tm):
    bsz, s, d = x.shape
    wdt = o_a.shape[-1]
    upper = jnp.asarray(np.triu(np.ones((tm, tm), np.float32), 1), BF16)
    wrt = wr.T
    wr_hi = wrt.astype(BF16)
    wr_lo = (wrt - wr_hi.astype(F32)).astype(BF16)
    const = lambda b, i: (0, 0)
    tile = lambda b, i: (b, i, 0)
    kern = functools.partial(_out_proj_kernel, n_groups, n_experts)
    return pl.pallas_call(
        kern,
        out_shape=(jax.ShapeDtypeStruct((bsz, s, d), F32),
                   jax.ShapeDtypeStruct((bsz * s * SUB, LANES), U32),
                   jax.ShapeDtypeStruct((bsz * (s // tm) * SUBLANES, tm), F32),
                   jax.ShapeDtypeStruct((LANES, 1), F32)),
        grid=(bsz, s // tm),
        in_specs=[pl.BlockSpec((1, tm, d), tile),
                  pl.BlockSpec((1, tm, wdt), tile),
                  pl.BlockSpec((1, tm, wdt), tile),
                  pl.BlockSpec((1, tm, d), lambda b, i: (b, i, 0)),
                  pl.BlockSpec((1, tm, d), lambda b, i: (b, i, 1)),
                  pl.BlockSpec((1, 1, d), lambda b, i: (b * 6 + 2, 0, 0)),
                  pl.BlockSpec((1, 1, d), lambda b, i: (b * 6 + 4, 0, 0)),
                  pl.BlockSpec((1, 1, d), lambda b, i: (b * 6 + 3, 0, 0)),
                  pl.BlockSpec((1, d), const),
                  pl.BlockSpec(wa.shape, const),
                  pl.BlockSpec(wb.shape, const),
                  pl.BlockSpec(wo.shape, const),
                  pl.BlockSpec(wrt.shape, const),
                  pl.BlockSpec(wrt.shape, const),
                  pl.BlockSpec((LANES, 1), const),
                  pl.BlockSpec((tm, tm), const)],
        out_specs=(pl.BlockSpec((1, tm, d), tile),
                   pl.BlockSpec((tm * SUB, LANES), lambda b, i: (b * (s // tm) + i, 0)),
                   pl.BlockSpec((SUBLANES, tm), lambda b, i: (b * (s // tm) + i, 0)),
                   pl.BlockSpec((LANES, 1), const)),
        scratch_shapes=[pltpu.VMEM((LANES, 1), F32)],
        compiler_params=_cparams(("arbitrary", "arbitrary")),
        name="out_proj",
    )(x, o_a, o_b, gates, gates, mod3, mod3, mod3, norm2_g.reshape(1, d), wa, wb, wo, wr_hi, wr_lo,
      br.reshape(LANES, 1), upper)


def _moe_slotmap_kernel(tm_o, n_tok, blk, dest_ref, zstart_ref, zcnt_ref, nused_ref, smap_ref):
    i = pl.program_id(0)
    n_slots = smap_ref.shape[0] - blk
    n_experts = zcnt_ref.shape[0]

    @pl.when(i == 0)
    def _():
        unroll = 8

        def init(g, carry):
            for u in range(unroll):
                smap_ref[blk + g * unroll + u] = TOP_K * n_tok
            return carry
        lax.fori_loop(nused_ref[0] * (blk // unroll), n_slots // unroll, init, 0)
        for r in range(blk):
            smap_ref[r] = TOP_K * n_tok + n_experts * blk + r
        for e in range(n_experts):
            def pad(g, carry, e=e):
                for u in range(unroll):
                    j = jnp.maximum(zcnt_ref[e] - 1 - (g * unroll + u), 0)
                    smap_ref[blk + zstart_ref[e] + j] = TOP_K * n_tok + e * blk + j
                return carry
            lax.fori_loop(0, (zcnt_ref[e] + unroll - 1) // unroll, pad, 0)

    base = i * (TOP_K * tm_o)
    both = 1 + (1 << SMAP_BITS)
    for k in range(TOP_K):
        v0 = k * n_tok + (i * tm_o) * both
        for r in range(tm_o):
            smap_ref[dest_ref[base + k * tm_o + r]] = v0 + r * both


def _moe_slotmap(dest, zstart, zcnt, n_used, n_slots, blk, tm_o, n_tok):
    grid_spec = pltpu.PrefetchScalarGridSpec(
        num_scalar_prefetch=4,
        grid=(n_tok // tm_o,),
        in_specs=[],
        out_specs=pl.BlockSpec(memory_space=pltpu.SMEM),
    )
    n_experts = zcnt.shape[0]
    assert TOP_K * n_tok + (n_experts + 1) * blk <= (1 << SMAP_BITS) and n_tok <= (1 << (31 - SMAP_BITS))
    return pl.pallas_call(
        functools.partial(_moe_slotmap_kernel, tm_o, n_tok, blk),
        out_shape=jax.ShapeDtypeStruct((blk + n_slots,), jnp.int32),
        grid_spec=grid_spec,
        compiler_params=_cparams(("arbitrary",)),
        name="moe_slotmap",
    )(dest, zstart, zcnt, n_used)


def _moe_expert_gather_kernel(n_tok, blk_e_ref, nused_ref, smap_ref, h_ref, wg_ref, wu_ref, wd_ref, y_ref,
                              hv, xbuf, ystage, wgb, wub, wdb, sem, hsem):
    i = pl.program_id(0)
    nb = pl.num_programs(0)
    n_used = nused_ref[0]
    blk = xbuf.shape[1] // SUB
    first_real_blocks = TOP_K * n_tok // blk

    def gather(b, slot, rows):
        for r in rows:
            tok = lax.shift_right_logical(smap_ref[(b + 1) * blk + r], SMAP_BITS)
            src = pl.multiple_of(tok * SUB, SUB)
            xbuf[slot, pl.ds(r * SUB, SUB), :] = hv[pl.ds(src, SUB), :]

    def issue(b, slot, rows):
        for r in rows:
            t = smap_ref[(b + 1) * blk + r] & ((1 << SMAP_BITS) - 1)
            dst = pl.multiple_of(t * SUB, SUB)
            pltpu.make_async_copy(ystage.at[slot, pl.ds(r * SUB, SUB), :], y_ref.at[pl.ds(dst, SUB), :],
                                  sem.at[slot]).start(priority=r % 2)

    def wait_block(slot):
        pltpu.make_async_copy(ystage.at[slot], y_ref.at[pl.ds(0, blk * SUB), :], sem.at[slot]).wait()

    @pl.when(i == 0)
    def _():
        load = pltpu.make_async_copy(h_ref, hv, hsem)
        load.start()
        ystage[...] = jnp.zeros_like(ystage)
        n_spare_blocks = y_ref.shape[0] // (blk * SUB) - first_real_blocks

        def spare_copy(c):
            dst = (first_real_blocks + c) * blk * SUB
            return pltpu.make_async_copy(ystage.at[0], y_ref.at[pl.ds(dst, blk * SUB), :], sem.at[0])

        for c in range(n_spare_blocks):
            spare_copy(c).start()
        for c in range(n_spare_blocks):
            spare_copy(c).wait()
        load.wait()
        gather(0, 0, range(blk))

    new_expert = (i == 0) | (blk_e_ref[i] != blk_e_ref[jnp.maximum(i - 1, 0)])

    @pl.when((i < n_used) & new_expert)
    def _():
        wgb[...] = wg_ref[0].astype(BF16)
        wub[...] = wu_ref[0].astype(BF16)
        wdb[...] = wd_ref[0].astype(BF16)

    @pl.when(i < n_used)
    def _():
        slot = i % 2
        pslot = 1 - slot
        nxt = jnp.minimum(i + 1, nb - 1)

        @pl.when(i > 0)
        def _():
            wait_block(slot)

        q = blk // 4
        xb = _tiles_to_rows(xbuf.at[slot], blk).astype(BF16)
        issue(i - 1, pslot, range(0, q))
        gather(nxt, pslot, range(0, q))
        hg = _dot(xb, wgb[...])
        issue(i - 1, pslot, range(q, 2 * q))
        gather(nxt, pslot, range(q, 2 * q))
        hu = _dot(xb, wub[...])
        issue(i - 1, pslot, range(2 * q, 3 * q))
        gather(nxt, pslot, range(2 * q, 3 * q))
        hid = (_silu(hg) * hu).astype(BF16)
        y = _dot(hid, wdb[...])
        issue(i - 1, pslot, range(3 * q, blk))
        gather(nxt, pslot, range(3 * q, blk))
        _rows_to_tiles(ystage.at[slot], y)

        @pl.when(i == n_used - 1)
        def _():
            issue(i, slot, range(blk))
            wait_block(pslot)
            wait_block(slot)


def _moe_experts_gather(h2, smap, blk_e, n_used, w_gate, w_up, w_down, blk, n_tok):
    d = w_gate.shape[1]
    assert d == 2 * SUB * LANES and h2.shape == (n_tok * SUB, LANES)
    n_slots = smap.shape[0] - blk
    nb = n_slots // blk
    f = w_gate.shape[-1]
    n_experts = w_gate.shape[0]
    assert (TOP_K * n_tok) % blk == 0
    n_tiles = TOP_K * n_tok + (n_experts + 1) * blk
    grid_spec = pltpu.PrefetchScalarGridSpec(
        num_scalar_prefetch=3,
        grid=(nb,),
        in_specs=[pl.BlockSpec(memory_space=pl.ANY),
                  pl.BlockSpec((1, d, f), lambda i, be, nu, sm: (be[i], 0, 0)),
                  pl.BlockSpec((1, d, f), lambda i, be, nu, sm: (be[i], 0, 0)),
                  pl.BlockSpec((1, f, d), lambda i, be, nu, sm: (be[i], 0, 0))],
        out_specs=pl.BlockSpec(memory_space=pl.ANY),
        scratch_shapes=[pltpu.VMEM((n_tok * SUB, LANES), U32),
                        pltpu.VMEM((2, blk * SUB, LANES), U32),
                        pltpu.VMEM((2, blk * SUB, LANES), U32),
                        pltpu.VMEM((d, f), BF16),
                        pltpu.VMEM((d, f), BF16),
                        pltpu.VMEM((f, d), BF16),
                        pltpu.SemaphoreType.DMA((2,)),
                        pltpu.SemaphoreType.DMA],
    )
    return pl.pallas_call(
        functools.partial(_moe_expert_gather_kernel, n_tok),
        out_shape=jax.ShapeDtypeStruct((n_tiles * SUB, LANES), U32),
        grid_spec=grid_spec,
        compiler_params=_cparams(("arbitrary",)),
        name="moe_experts",
    )(blk_e, n_used, smap, h2, w_gate, w_up, w_down)


def _moe_combine_kernel(x1_ref, routet_ref, gt2_ref, fg_ref, y1_ref, y2_ref, o_ref):
    tm = x1_ref.shape[0]
    route = routet_ref[...].T
    moe = route[:, 0:1] * _tiles_to_rows(y1_ref, tm) + route[:, 1:2] * _tiles_to_rows(y2_ref, tm)
    xo = x1_ref[...] + gt2_ref[0] * moe
    ms = jnp.mean(xo * xo, axis=-1, keepdims=True)
    o_ref[...] = xo * lax.rsqrt(ms + NORM_EPS) * fg_ref[...]


def _moe_combine(x1, route_t, yt, mod3, final_g, s, tm):
    n, d = x1.shape
    tiles_per_batch = s // tm
    n_steps = n // tm
    per_o = route_t.shape[1] // tm
    return pl.pallas_call(
        _moe_combine_kernel,
        out_shape=jax.ShapeDtypeStruct((n, d), F32),
        grid=(n_steps,),
        in_specs=[pl.BlockSpec((tm, d), lambda i: (i, 0)),
                  pl.BlockSpec((SUBLANES, tm), lambda i: (i // per_o, i % per_o)),
                  pl.BlockSpec((1, 1, d), lambda i: ((i // tiles_per_batch) * 6 + 5, 0, 0)),
                  pl.BlockSpec((1, d), lambda i: (0, 0)),
                  pl.BlockSpec((tm * SUB, LANES), lambda i: (i, 0)),
                  pl.BlockSpec((tm * SUB, LANES), lambda i: (n_steps + i, 0))],
        out_specs=pl.BlockSpec((tm, d), lambda i: (i, 0)),
        compiler_params=_cparams(("arbitrary",)),
        name="moe_combine",
    )(x1, route_t, mod3, final_g.reshape(1, d), yt, yt)


def _pick(n, candidates):
    for t in candidates:
        if n % t == 0:
            return t
    raise ValueError(f"no tile in {candidates} divides {n}")


def kernel(x, c, ada_w, ada_b, norm1_g, w_in, hg_lb, hg_norm_g, rw_mu, rw_w0, rw_w2, rw_a0, rw_a2, rw_g2, rw_kk, rw_ka, rw_rk, rw_gn_w, rw_gn_b, w_proj_a, w_proj_b, w_out, norm2_g, router_g_w, router_g_b, router_e_w, router_e_b, exp_w_gate, exp_w_up, exp_w_down, final_g):
    bsz, s, d = x.shape
    depth = ada_w.shape[0]
    hg_f = hg_lb.shape[-1]
    hg_w = hg_norm_g.shape[-1]
    rw_w = rw_w0.shape[-1]
    rw_cols = rw_mu.shape[-1]
    n_groups = router_g_w.shape[-1]
    n_experts = router_e_w.shape[-1]
    assert hg_f == hg_w and s % CHUNK == 0 and n_experts + n_groups <= LANES and d == 2 * SUB * LANES

    lb_all = jnp.cumsum(jax.nn.softmax(hg_lb.astype(F32), axis=0), axis=0)
    n = bsz * s
    blk = 512
    n_blocks = (n * TOP_K + n_experts * blk) // blk
    for l in range(depth):
        mod = _ada_mod(c, ada_w[l], ada_b[l])
        mod3 = mod.reshape(bsz * 6, 1, d)

        hg_cols = 2 * hg_f + 2 * hg_w
        rw_pad = -(-rw_cols // 256) * 256
        w_hg, w_rw, w_gt = _w_in_split(w_in, l, hg_cols, rw_cols, rw_pad)
        lora = (rw_w2.shape[1], rw_a2.shape[1], rw_g2.shape[1])
        assert rw_pad - 3 * rw_w == 256 and rw_cols == 3 * rw_w + sum(lora)
        hg, rw, gates = _in_proj(x, mod3, norm1_g[l], lb_all[l], rw_mu[l], lora, w_hg, w_rw, w_gt,
                                 _pick(s, (512, 256, 128, 64)))

        o_a = _hgrn2(hg, hg_norm_g[l], _pick(s, (1024, 512, 256, 128, 64)))
        o_b = _rwkv7(rw, rw_w0[l], rw_w2[l], rw_a0[l], rw_a2[l], rw_g2[l],
                     rw_kk[l], rw_ka[l], rw_rk[l].reshape(-1), rw_gn_w[l], rw_gn_b[l],
                     _pick(s, (2 * CHUNK, CHUNK)))

        wr = jnp.zeros((d, LANES), F32).at[:, :n_experts].set(router_e_w[l])
        wr = wr.at[:, n_experts:n_experts + n_groups].set(router_g_w[l])
        br = jnp.zeros((1, LANES), F32).at[0, :n_experts].set(router_e_b[l])
        br = br.at[0, n_experts:n_experts + n_groups].set(router_g_b[l])
        tm_o = _pick(s, (1024, 512, 256, 128, 64))
        x1, h2, route_t, counts = _out_proj(
            x, o_a, o_b, gates, mod3, norm2_g[l],
            w_proj_a[l].astype(BF16), w_proj_b[l].astype(BF16), w_out[l].astype(BF16),
            wr, br, n_groups, n_experts, tm_o)

        rt = route_t.reshape(n // tm_o, SUBLANES, tm_o)
        eid = rt[:, 2:4, :].astype(jnp.int32)
        rank = rt[:, 4:6, :].astype(jnp.int32)
        cnt = counts[:n_experts, 0].astype(jnp.int32)
        padded = (cnt + blk - 1) // blk * blk
        pad_end = jnp.cumsum(padded)
        pad_start = pad_end - padded
        e_ax = jnp.arange(n_experts, dtype=jnp.int32)[:, None, None, None]
        dest = rank + jnp.sum(jnp.where(eid[None] == e_ax, pad_start[:, None, None, None], 0), axis=0)
        blk_start = jnp.arange(n_blocks, dtype=jnp.int32) * blk
        blk_e = jnp.minimum(jnp.sum((pad_end[None, :] <= blk_start[:, None]).astype(jnp.int32), axis=1),
                            n_experts - 1)
        n_used = (pad_end[-1:] // blk).astype(jnp.int32)
        tm = tm_o
        smap = _moe_slotmap(dest.reshape(-1) + blk, pad_start + cnt, padded - cnt, n_used, n_blocks * blk, blk, tm_o, n)
        yt = _moe_experts_gather(h2, smap, blk_e, n_used, exp_w_gate[l], exp_w_up[l], exp_w_down[l], blk, n)
        last = l == depth - 1
        assert last, "the final RMSNorm is fused into the last layer's combine"
        out = _moe_combine(x1.reshape(n, d), route_t, yt, mod3, final_g, s, tm)
        x = out.reshape(bsz, s, d)
    return x
```

```python
import functools

import numpy as np
import jax
import jax.numpy as jnp
from jax import lax
from jax.experimental import pallas as pl
from jax.experimental.pallas import tpu as pltpu

F32 = jnp.float32
BF16 = jnp.bfloat16

NORM_EPS = 1e-6
HG_HEAD = 128
RW_HEAD = 64
RW_GN_EPS = 64e-5
TOP_K = 2
CHUNK = 64
LANES = 128
SUB = 4
U32 = jnp.uint32
MXU_K = 256
SUBLANES = 8
SMAP_BITS = 16
VMEM_LIMIT = 56 * 1024 * 1024

NT = (((1,), (1,)), ((), ()))
TN = (((0,), (0,)), ((), ()))


def _dot(a, b, dims=None, precision=None):
    if dims is None:
        return jnp.dot(a, b, preferred_element_type=F32, precision=precision)
    return lax.dot_general(a, b, dims, preferred_element_type=F32, precision=precision)


def _split3(x):
    hi = x.astype(BF16)
    r1 = x - hi.astype(F32)
    mid = r1.astype(BF16)
    lo = (r1 - mid.astype(F32)).astype(BF16)
    return hi, mid, lo


def _dot_exact_lhs(m3_bf16, x):
    return _dot(m3_bf16, jnp.concatenate(_split3(x), axis=0))


def _head_sums(x, m2_bf16):
    outs = []
    for g0 in range(0, x.shape[1], MXU_K):
        xg = x[:, g0:g0 + MXU_K]
        hi = xg.astype(BF16)
        lo = (xg - hi.astype(F32)).astype(BF16)
        outs.append(_dot(jnp.concatenate([hi, lo], axis=1), m2_bf16))
    return jnp.concatenate(outs, axis=1)


def _sigmoid(x):
    return 1.0 / (1.0 + jnp.exp(-x))


def _silu(x):
    return x * _sigmoid(x)


def _rows_to_tiles(ref, val):
    m, half = val.shape[0], val.shape[1] // 2
    hi = lax.bitcast_convert_type(val[:, :half].astype(BF16).astype(F32), U32)
    lo = lax.bitcast_convert_type(val[:, half:].astype(BF16).astype(F32), U32)
    w = (hi & jnp.uint32(0xFFFF0000)) | (lo >> 16)
    for j in range(SUB):
        ref[pl.ds(j, m, stride=SUB), :] = w[:, j * LANES:(j + 1) * LANES]


def _tiles_to_rows(ref, m, base=0):
    w = jnp.concatenate([ref[pl.ds(base * SUB + j, m, stride=SUB), :] for j in range(SUB)], axis=1)
    hi = lax.bitcast_convert_type(w & jnp.uint32(0xFFFF0000), F32)
    lo = lax.bitcast_convert_type(w << 16, F32)
    return jnp.concatenate([hi, lo], axis=1)


def _cparams(sem):
    return pltpu.CompilerParams(dimension_semantics=sem, vmem_limit_bytes=VMEM_LIMIT)


def _ada_kernel(cb_ref, w_ref, b_ref, o_ref, sc_ref):
    bsz, d = cb_ref.shape[0], cb_ref.shape[1]
    tn = w_ref.shape[1]
    reps = tn // LANES

    @pl.when(pl.program_id(0) == 0)
    def _():
        sc_ref[...] = _silu(cb_ref[...])

    def body(i, accs):
        k0 = pl.multiple_of(i * SUBLANES, SUBLANES)
        w = w_ref[pl.ds(k0, SUBLANES), :]
        out = []
        for b in range(bsz):
            sc = sc_ref[b, pl.ds(k0, SUBLANES), :]
            out.append(accs[b] + w * jnp.concatenate([sc] * reps, axis=1))
        return tuple(out)

    accs = lax.fori_loop(0, d // SUBLANES, body, tuple(jnp.zeros((SUBLANES, tn), F32) for _ in range(bsz)),
                         unroll=4)
    o_ref[...] = jnp.zeros_like(o_ref)
    for b in range(bsz):
        o_ref[b:b + 1, :] = jnp.sum(accs[b], axis=0, keepdims=True) + b_ref[...]


def _ada_mod(c, w, b):
    bsz, d = c.shape
    n = w.shape[1]
    rows = -(-bsz // SUBLANES) * SUBLANES
    cb = jnp.broadcast_to(c[:, :, None], (bsz, d, LANES))
    tn = _pick(n, (1024, 512, 256, 128))
    out = pl.pallas_call(
        _ada_kernel,
        out_shape=jax.ShapeDtypeStruct((rows, n), F32),
        grid=(n // tn,),
        in_specs=[pl.BlockSpec((bsz, d, LANES), lambda j: (0, 0, 0)),
                  pl.BlockSpec((d, tn), lambda j: (0, j)),
                  pl.BlockSpec((1, tn), lambda j: (0, j))],
        out_specs=pl.BlockSpec((rows, tn), lambda j: (0, j)),
        scratch_shapes=[pltpu.VMEM((bsz, d, LANES), F32)],
        compiler_params=_cparams(("arbitrary",)),
        name="ada_mod",
    )(cb, w, b.reshape(1, n))
    return out[:bsz]


def _in_proj_kernel(lora, x_ref, sh_ref, sc_ref, g_ref, lb_ref, mu_ref, whg_ref, wrw_ref, wgt_ref,
                    hg_ref, rw_ref, gt_ref, carry_ref):
    @pl.when(pl.program_id(1) == 0)
    def _():
        carry_ref[...] = jnp.zeros_like(carry_ref)

    x = x_ref[0]
    ms = jnp.mean(x * x, axis=-1, keepdims=True)
    h = (x * lax.rsqrt(ms + NORM_EPS) * g_ref[...]) * (1.0 + sc_ref[0]) + sh_ref[0]
    hb = h.astype(BF16)
    lb = lb_ref[...]
    w = lb.shape[1]
    hg_maps = (_silu, lambda t: lb + (1.0 - lb) * _sigmoid(t), lambda t: t, _silu)
    for part, fn in enumerate(hg_maps):
        hg_ref[0, :, part * w:(part + 1) * w] = fn(_dot(hb, whg_ref[:, part * w:(part + 1) * w]))
    step = 512
    n_rw = wrw_ref.shape[1]
    row = lax.broadcasted_iota(jnp.int32, (x.shape[0], 256), 0)
    lane = lax.broadcasted_iota(jnp.int32, (x.shape[0], 256), 1)
    dl, al, gl = lora
    for n0 in range(0, n_rw, 256):
        cs = slice(n0, n0 + 256)
        p = _dot(hb, wrw_ref[:, cs])
        prev = jnp.where(row == 0, carry_ref[:, cs], pltpu.roll(p, 1, 0))
        carry_ref[:, cs] = p[x.shape[0] - 1:, :]
        xs = p + mu_ref[:, cs] * (prev - p)
        if n0 == n_rw - 256:
            xs = jnp.where(lane < dl, jnp.tanh(xs),
                           jnp.where(lane < dl + al, xs, jnp.where(lane < dl + al + gl, _sigmoid(xs), 0.0)))
        rw_ref[0, :, cs] = xs
    for n0 in range(0, wgt_ref.shape[1], step):
        gt_ref[0, :, n0:n0 + step] = _sigmoid(_dot(hb, wgt_ref[:, n0:n0 + step])).astype(BF16)


def _w_in_split_kernel(chunks, wt_ref, hg_ref, rw_ref, gt_ref, buf_ref, sem_ref):
    outs = (hg_ref, rw_ref, gt_ref)
    ch = buf_ref.shape[1]

    def copy(k):
        return pltpu.make_async_copy(wt_ref.at[pl.ds(chunks[k][2], ch), :], buf_ref.at[k % 2], sem_ref.at[k % 2])

    copy(0).start()
    for k, (oi, c0, _, valid) in enumerate(chunks):
        if k + 1 < len(chunks):
            copy(k + 1).start()
        copy(k).wait()
        t = buf_ref[k % 2].T
        if valid < ch:
            t = jnp.where(lax.broadcasted_iota(jnp.int32, t.shape, 1) < valid, t, 0.0)
        outs[oi][:, c0:c0 + ch] = t.astype(BF16)


def _w_in_split(w_in_t, n_hg, n_rw, n_rw_pad):
    n_in, d = w_in_t.shape
    n_gt = n_in - n_hg - n_rw
    ch = 256
    assert n_hg % ch == 0 and n_rw_pad % ch == 0 and n_gt % ch == 0 and n_gt >= ch
    chunks = []
    for oi, (r0, nr, npad) in enumerate(((0, n_hg, n_hg), (n_hg, n_rw, n_rw_pad), (n_hg + n_rw, n_gt, n_gt))):
        chunks += [(oi, c0, r0 + c0, min(ch, nr - c0)) for c0 in range(0, npad, ch)]
    assert all(v > 0 for _, _, _, v in chunks)
    return pl.pallas_call(
        functools.partial(_w_in_split_kernel, tuple(chunks)),
        out_shape=(jax.ShapeDtypeStruct((d, n_hg), BF16),
                   jax.ShapeDtypeStruct((d, n_rw_pad), BF16),
                   jax.ShapeDtypeStruct((d, n_gt), BF16)),
        in_specs=[pl.BlockSpec(memory_space=pl.ANY)],
        scratch_shapes=[pltpu.VMEM((2, ch, d), F32), pltpu.SemaphoreType.DMA((2,))],
        compiler_params=pltpu.CompilerParams(vmem_limit_bytes=VMEM_LIMIT),
        name="w_in_split",
    )(w_in_t)


def _in_proj(x, mod3, norm_g, lb, mu, lora, w_hg, w_rw, w_gt, tm):
    bsz, s, d = x.shape
    assert w_hg.shape[1] == 4 * lb.shape[0] and sum(lora) <= 256
    mup = jnp.zeros((1, w_rw.shape[1]), F32).at[0, :mu.shape[-1]].set(mu)
    n_hg, n_rw, n_gt = w_hg.shape[1], w_rw.shape[1], w_gt.shape[1]
    const = lambda b, i: (0, 0)
    return pl.pallas_call(
        functools.partial(_in_proj_kernel, lora),
        out_shape=(jax.ShapeDtypeStruct((bsz, s, n_hg), F32),
                   jax.ShapeDtypeStruct((bsz, s, n_rw), F32),
                   jax.ShapeDtypeStruct((bsz, s, n_gt), BF16)),
        grid=(bsz, s // tm),
        in_specs=[pl.BlockSpec((1, tm, d), lambda b, i: (b, i, 0)),
                  pl.BlockSpec((1, 1, d), lambda b, i: (b * 6 + 0, 0, 0)),
                  pl.BlockSpec((1, 1, d), lambda b, i: (b * 6 + 1, 0, 0)),
                  pl.BlockSpec((1, d), const),
                  pl.BlockSpec((1, lb.shape[0]), const),
                  pl.BlockSpec((1, n_rw), const),
                  pl.BlockSpec((d, n_hg), const),
                  pl.BlockSpec((d, n_rw), const),
                  pl.BlockSpec((d, n_gt), const)],
        out_specs=(pl.BlockSpec((1, tm, n_hg), lambda b, i: (b, i, 0)),
                   pl.BlockSpec((1, tm, n_rw), lambda b, i: (b, i, 0)),
                   pl.BlockSpec((1, tm, n_gt), lambda b, i: (b, i, 0))),
        scratch_shapes=[pltpu.VMEM((1, n_rw), F32)],
        compiler_params=_cparams(("arbitrary", "arbitrary")),
        name="in_proj",
    )(x, mod3, mod3, norm_g.reshape(1, d), lb.reshape(1, -1), mup, w_hg, w_rw, w_gt)


_HG_LEVELS = (32, 16, 8, 4, 2, 1)


def _hgrn2_consts(width):
    c = CHUNK
    t = np.arange(c)[:, None]
    s = np.arange(c)[None, :]
    blocks = [(s <= t), (s > t)]
    lvl_masks = []
    right = []
    for h in _HG_LEVELS:
        m = (t // (2 * h)) * 2 * h + h
        is_r = (t & h) != 0
        blk = np.where(is_r, (s >= m) & (s <= t), (s > t) & (s <= m - 1))
        blocks.append(blk)
        lvl_masks.append(is_r & ((s & h) == 0) & ((t // (2 * h)) == (s // (2 * h))))
        right.append(np.broadcast_to(is_r, (c, width)))
    mst = np.tile(np.concatenate(blocks, axis=0).astype(np.float32), (1, 3))
    lm = np.stack([np.eye(c, dtype=bool)] + lvl_masks).astype(np.float32)
    rm = np.stack(right).astype(np.float32)
    return jnp.asarray(mst, BF16), jnp.asarray(lm, F32), jnp.asarray(rm, F32)


def _hgrn2_kernel(q_ref, f_ref, i_ref, g_ref, ng_ref, mst_ref, lm_ref, rm_ref, o_ref, st_ref):
    c = CHUNK
    n_chunks = q_ref.shape[1] // c

    @pl.when(pl.program_id(1) == 0)
    def _():
        st_ref[...] = jnp.zeros_like(st_ref)

    nsub = next(n for n in (8, 4, 2, 1) if n_chunks % n == 0)

    def chunk_body(ci, carry):
        r0 = pl.multiple_of(ci * (nsub * c), nsub * c)
        for _ in _hgrn2_steps(q_ref, f_ref, i_ref, g_ref, ng_ref, mst_ref, lm_ref, rm_ref, o_ref, st_ref,
                              0, r0, nsub):
            pass
        return carry

    lax.fori_loop(0, n_chunks // nsub, chunk_body, 0)


def _hgrn2_steps(q_ref, f_ref, i_ref, g_ref, ng_ref, mst_ref, lm_ref, rm_ref, o_ref, st_ref, b, r0, nsub):
    c = CHUNK
    n_heads = q_ref.shape[2] // HG_HEAD
    mst = mst_ref[...]
    ng = ng_ref[...]
    heads = [slice(hd * HG_HEAD, (hd + 1) * HG_HEAD) for hd in range(n_heads)]
    subs = []
    for j in range(nsub):
        rows = pl.ds(r0 + j * c, c)
        q = q_ref[b, rows, :]
        f = f_ref[b, rows, :]
        k = 1.0 - f
        ex = jnp.exp2(_dot_exact_lhs(mst, jnp.log2(f)))
        subs.append(dict(rows=rows, q=q, k=k, ex=ex, vb=i_ref[b, rows, :].astype(BF16),
                         qd=(q * ex[0:c]).astype(BF16), kr=(k * ex[c:2 * c]).astype(BF16)))
        yield
    for sb in subs:
        qb, kb = sb['q'].astype(BF16), sb['k'].astype(BF16)
        sb['sc'] = [lm_ref[0] * _dot(qb[:, ls], kb[:, ls], NT) for ls in heads]
        sb['dqk'] = sb['q'] - sb['k']
    yield
    for li, h in enumerate(_HG_LEVELS):
        for sb in subs:
            if h % SUBLANES == 0:
                qk = jnp.concatenate([(sb['q'] if m % 2 else sb['k'])[m * h:(m + 1) * h] for m in range(c // h)],
                                     axis=0)
            else:
                qk = sb['k'] + rm_ref[li] * sb['dqk']
            g_l = (qk * sb['ex'][(2 + li) * c:(3 + li) * c]).astype(BF16)
            sb['sc'] = [s_h + lm_ref[li + 1] * _dot(g_l[:, ls], g_l[:, ls], NT)
                        for s_h, ls in zip(sb['sc'], heads)]
        yield
    for sb in subs:
        sb['kv'] = [_dot(sb['vb'][:, ls], sb['kr'][:, ls], TN) for ls in heads]
        sb['o'] = [_dot(s_h.astype(BF16), sb['vb'][:, ls]) for s_h, ls in zip(sb['sc'], heads)]
        yield
    sts = [st_ref[b, hd] for hd in range(n_heads)]
    for sb in subs:
        sb['o'] = [o_h + _dot(sb['qd'][:, ls], st.astype(BF16), NT) for o_h, ls, st in zip(sb['o'], heads, sts)]
        sts = [st * sb['ex'][c - 1:c, ls] + kv for st, ls, kv in zip(sts, heads, sb['kv'])]
        yield
    for hd in range(n_heads):
        st_ref[b, hd] = sts[hd]
    for sb in subs:
        on = [o_h * lax.rsqrt(jnp.mean(o_h * o_h, axis=-1, keepdims=True) + NORM_EPS) for o_h in sb['o']]
        o_full = jnp.concatenate(on, axis=1) * ng
        o_ref[b, sb['rows'], :] = (o_full * g_ref[b, sb['rows'], :]).astype(o_ref.dtype)
        yield


def _hgrn2(hg, norm_g, ts):
    bsz, s, n4 = hg.shape
    w = n4 // 4
    mst, lm, rm = _hgrn2_consts(w)
    n_heads = w // HG_HEAD
    const2 = lambda b, i: (0, 0)
    const3 = lambda b, i: (0, 0, 0)
    return pl.pallas_call(
        _hgrn2_kernel,
        out_shape=jax.ShapeDtypeStruct((bsz, s, w), BF16),
        grid=(bsz, s // ts),
        in_specs=[pl.BlockSpec((1, ts, w), lambda b, i: (b, i, 0)),
                  pl.BlockSpec((1, ts, w), lambda b, i: (b, i, 1)),
                  pl.BlockSpec((1, ts, w), lambda b, i: (b, i, 2)),
                  pl.BlockSpec((1, ts, w), lambda b, i: (b, i, 3)),
                  pl.BlockSpec((1, w), const2),
                  pl.BlockSpec(mst.shape, const2),
                  pl.BlockSpec(lm.shape, const3),
                  pl.BlockSpec(rm.shape, const3)],
        out_specs=pl.BlockSpec((1, ts, w), lambda b, i: (b, i, 0)),
        scratch_shapes=[pltpu.VMEM((1, n_heads, HG_HEAD, HG_HEAD), F32)],
        compiler_params=_cparams(("arbitrary", "arbitrary")),
        name="hgrn2",
    )(hg, hg, hg, hg, norm_g.reshape(1, w), mst, lm, rm)


def _rwkv_consts(width):
    c = CHUNK
    t = np.arange(c)[:, None]
    s = np.arange(c)[None, :]
    tri = np.tile((s <= t).astype(np.float32), (1, 3))
    tt = np.arange(2 * c)[:, None]
    ss = np.arange(2 * c)[None, :]
    same = (tt // c) == (ss // c)
    strict = same & ((ss % c) < (tt % c))
    incl = same & ((ss % c) <= (tt % c))
    hsum = (np.arange(MXU_K)[:, None] // RW_HEAD) == (np.arange(MXU_K)[None, :] // RW_HEAD)
    hsum = np.tile(hsum, (2, 1))
    return (jnp.asarray(tri, BF16), jnp.asarray(strict.astype(np.float32), F32),
            jnp.asarray(incl.astype(np.float32), F32), jnp.asarray(hsum.astype(np.float32), BF16))


def _rwkv7_kernel(p_ref, w0_ref, a0_ref, kk_ref, ka_ref, rk_ref, gnw_ref, gnb_ref,
                  w2_ref, a2_ref, g2_ref, tri_ref, sm_ref, im_ref, hs_ref,
                  o_ref, zt_ref):
    @pl.when(pl.program_id(0) == 0)
    def _():
        zt_ref[...] = jnp.zeros_like(zt_ref)

    for _ in _rwkv7_steps(p_ref, w0_ref, a0_ref, kk_ref, ka_ref, rk_ref, gnw_ref, gnb_ref,
                          w2_ref, a2_ref, g2_ref, tri_ref, sm_ref, im_ref, hs_ref, o_ref, zt_ref):
        pass


def _rwkv7_steps(p_ref, w0_ref, a0_ref, kk_ref, ka_ref, rk_ref, gnw_ref, gnb_ref,
                 w2_ref, a2_ref, g2_ref, tri_ref, sm_ref, im_ref, hs_ref, o_ref, zt_ref):
    c = CHUNK
    nb = p_ref.shape[0]
    nch = p_ref.shape[1] // c
    width = o_ref.shape[2]
    n_pairs = width // LANES

    hs = hs_ref[...]
    tri = tri_ref[...]
    smask = sm_ref[...] > 0
    imask = im_ref[...] > 0
    lane = lax.broadcasted_iota(jnp.int32, (c, LANES), 1)
    m0 = (lane < RW_HEAD).astype(F32)
    m1 = 1.0 - m0

    def stack(x):
        return jnp.concatenate([x * m0, x * m1], axis=0)

    xs = jnp.concatenate([p_ref[b] for b in range(nb)], axis=0)
    r_all = xs[:, 0:width]
    k_all = xs[:, width:2 * width]
    v_all = xs[:, 2 * width:3 * width]
    slab = xs[:, 3 * width:].astype(BF16)
    nz = -(w0_ref[...] + _dot(slab, w2_ref[...]))
    softplus = jnp.maximum(nz, 0.0) + jnp.log(1.0 + jnp.exp(-jnp.abs(nz)))
    ld_all = -jnp.exp(-softplus - 0.5)
    a_all = _sigmoid(a0_ref[...] + _dot(slab, a2_ref[...]))
    g_all = _dot(slab, g2_ref[...])
    kk0 = k_all * kk_ref[...]
    kk_all = kk0 * lax.rsqrt(jnp.maximum(_head_sums(kk0 * kk0, hs), 1e-24))
    k2_all = k_all * (1.0 + (a_all - 1.0) * ka_ref[...])
    yield

    units = []
    for b, j in [(b, j) for b in range(nb) for j in range(nch)]:
        rb = slice((b * nch + j) * c, (b * nch + j + 1) * c)
        r, k2, v, ld = r_all[rb], k2_all[rb], v_all[rb], ld_all[rb]
        a_in = -kk_all[rb]
        b_in = kk_all[rb] * a_all[rb]
        cum = _dot_exact_lhs(tri, ld)
        cum_t = cum[c - 1:c, :]
        e_c = jnp.exp(cum)
        e_nc = jnp.exp(-cum)
        e_rem = jnp.exp(cum_t - cum)
        at_f = a_in * jnp.exp(cum - ld)
        rt_f = r * e_c
        kt_f = k2 * e_nc
        bt_f = b_in * e_nc
        kh_f = k2 * e_rem
        bh_f = b_in * e_rem
        p_t = jnp.exp(cum_t)
        for pi in range(n_pairs):
            ls = slice(pi * LANES, (pi + 1) * LANES)
            units.append(dict(
                b=b, j=j, pi=pi,
                at=stack(at_f[:, ls]).astype(BF16), rt=stack(rt_f[:, ls]).astype(BF16),
                kt=stack(kt_f[:, ls]).astype(BF16), bt=stack(bt_f[:, ls]).astype(BF16),
                kh=stack(kh_f[:, ls]).astype(BF16), bh=stack(bh_f[:, ls]).astype(BF16),
                vs=stack(v[:, ls]).astype(BF16), p_t=p_t[:, ls]))
        yield

    for u in units:
        lhs = jnp.concatenate([u['at'], u['rt']], axis=0)
        u['g'] = _dot(lhs, jnp.concatenate([u['kt'], u['bt']], axis=0), NT)
    yield
    for u in units:
        g = u.pop('g')
        u['a_ak'] = jnp.where(smask, g[:2 * c, :2 * c], 0.0).astype(BF16)
        u['pw'] = jnp.where(smask, g[:2 * c, 2 * c:], 0.0).astype(BF16)
        u['a_r'] = jnp.where(jnp.concatenate([imask, imask], axis=1), g[2 * c:], 0.0).astype(BF16)
    for u in units:
        akv = _dot(u.pop('a_ak'), u['vs'])
        u['x'] = jnp.concatenate([u['at'].astype(F32), akv], axis=1)
    yield
    n_lvl = int(np.log2(c))
    for lvl in range(n_lvl):
        for u in units:
            u['x'] = u['x'] + _dot(u['pw'], u['x'].astype(BF16))
        yield
        if lvl + 1 < n_lvl:
            for u in units:
                u['pw'] = _dot(u['pw'], u['pw']).astype(BF16)
            yield
    for u in units:
        x = u.pop('x')
        u['wr'] = jnp.concatenate([x[:, :LANES].astype(BF16), u['rt']], axis=0)
        u['u_loc'] = x[:, LANES:]
    zt = {(b, pi): zt_ref[b, pi] for b in range(nb) for pi in range(n_pairs)}
    for j in range(nch):
        tail = [u for u in units if u['j'] == j]
        for u in tail:
            u['uy'] = _dot(u.pop('wr'), zt[u['b'], u['pi']].astype(BF16), NT)
        yield
        for u in tail:
            uy = u.pop('uy')
            u['u'] = (uy[:2 * c] + u.pop('u_loc')).astype(BF16)
            u['y0'] = uy[2 * c:]
        for u in tail:
            vu = jnp.concatenate([u['vs'], u['u']], axis=0)
            u['y'] = u.pop('y0') + _dot(u['a_r'], vu)
            upd = _dot(vu, jnp.concatenate([u['kh'], u['bh']], axis=0), TN)
            zt[u['b'], u['pi']] = zt[u['b'], u['pi']] * u['p_t'] + upd
        yield
    for (b, pi), z in zt.items():
        zt_ref[b, pi] = z

    inv_n = 1.0 / RW_HEAD
    y = jnp.concatenate(
        [jnp.concatenate([u['y'][:c] + u['y'][c:] for u in units if (u['b'], u['j']) == (b, j)], axis=1)
         for b in range(nb) for j in range(nch)], axis=0)
    mean = _head_sums(y, hs) * inv_n
    yield
    d = y - mean
    var = _head_sums(d * d, hs) * inv_n
    yield
    yn = d * lax.rsqrt(var + RW_GN_EPS) * gnw_ref[...] + gnb_ref[...]
    bonus = _head_sums(r_all * k2_all * rk_ref[...], hs) * v_all
    out = ((yn + bonus) * g_all).astype(o_ref.dtype)
    for b in range(nb):
        o_ref[b] = out[b * nch * c:(b + 1) * nch * c]


def _rwkv7(rw, w0, w2, a0, a2, g2, k_k, k_a, r_k, gn_w, gn_b, ts):
    bsz, s, cols = rw.shape
    width = w0.shape[-1]
    n_pairs = width // LANES
    slab = cols - 3 * width
    dl, al, gl = w2.shape[0], a2.shape[0], g2.shape[0]
    w2f = jnp.zeros((slab, width), F32).at[0:dl].set(w2).astype(BF16)
    a2f = jnp.zeros((slab, width), F32).at[dl:dl + al].set(a2).astype(BF16)
    g2f = jnp.zeros((slab, width), F32).at[dl + al:dl + al + gl].set(g2).astype(BF16)
    tri, sm, im, hs = _rwkv_consts(width)
    row = lambda x: x.reshape(1, width)
    const = lambda i: (0, 0)
    vec = pl.BlockSpec((1, width), const)
    return pl.pallas_call(
        _rwkv7_kernel,
        out_shape=jax.ShapeDtypeStruct((bsz, s, width), BF16),
        grid=(s // ts,),
        in_specs=[pl.BlockSpec((bsz, ts, cols), lambda i: (0, i, 0)),
                  vec, vec, vec, vec, vec, vec, vec,
                  pl.BlockSpec((slab, width), const),
                  pl.BlockSpec((slab, width), const),
                  pl.BlockSpec((slab, width), const),
                  pl.BlockSpec(tri.shape, const),
                  pl.BlockSpec(sm.shape, const),
                  pl.BlockSpec(im.shape, const),
                  pl.BlockSpec(hs.shape, const)],
        out_specs=pl.BlockSpec((bsz, ts, width), lambda i: (0, i, 0)),
        scratch_shapes=[pltpu.VMEM((bsz, n_pairs, LANES, LANES), F32)],
        compiler_params=_cparams(("arbitrary",)),
        name="rwkv7",
    )(rw, row(w0), row(a0), row(k_k), row(k_a), row(r_k), row(gn_w), row(gn_b),
      w2f, a2f, g2f, tri, sm, im, hs)


def _out_proj_kernel(n_groups, n_experts,
                     x_ref, oa_ref, ob_ref, ga_ref, gb_ref, gt1_ref, sc2_ref, sh2_ref, g2_ref,
                     wa_ref, wb_ref, wo_ref, wr_ref, wrl_ref, br_ref, upper_ref,
                     x1_ref, h2_ref, routet_ref, cnt_ref, carry_ref):
    first = (pl.program_id(0) == 0) & (pl.program_id(1) == 0)

    @pl.when(first)
    def _():
        carry_ref[...] = jnp.zeros_like(carry_ref)

    pa = _dot(oa_ref[0], wa_ref[...])
    pb = _dot(ob_ref[0], wb_ref[...])
    mixed = ga_ref[0].astype(F32) * pa + gb_ref[0].astype(F32) * pb
    x1 = x_ref[0] + gt1_ref[0] * _dot(mixed.astype(BF16), wo_ref[...])
    x1_ref[0] = x1
    ms = jnp.mean(x1 * x1, axis=-1, keepdims=True)
    h2 = (x1 * lax.rsqrt(ms + NORM_EPS) * g2_ref[...]) * (1.0 + sc2_ref[0]) + sh2_ref[0]
    _rows_to_tiles(h2_ref, h2)

    h2_hi = h2.astype(BF16)
    h2_lo = (h2 - h2_hi.astype(F32)).astype(BF16)
    logits = (_dot(wr_ref[...], h2_hi, NT) + _dot(wr_ref[...], h2_lo, NT) + _dot(wrl_ref[...], h2_hi, NT)
              + br_ref[...])
    row = lax.broadcasted_iota(jnp.int32, logits.shape, 0)
    neg = jnp.float32(-jnp.inf)
    big = jnp.int32(1 << 20)
    eg = n_experts // n_groups
    is_g = (row >= n_experts) & (row < n_experts + n_groups)
    lg = jnp.where(is_g, logits, neg)
    mg = jnp.max(lg, axis=0, keepdims=True)
    p_grp = 1.0 / jnp.sum(jnp.where(is_g, jnp.exp(lg - mg), 0.0), axis=0, keepdims=True)
    gidx = jnp.min(jnp.where(lg == mg, row, big), axis=0, keepdims=True) - n_experts
    sel = (row >= gidx * eg) & (row < gidx * eg + eg)
    le = jnp.where(sel, logits, neg)
    me = jnp.max(le, axis=0, keepdims=True)
    pe_un = jnp.where(sel, jnp.exp(le - me), 0.0)
    pe = jnp.where(sel, pe_un / jnp.sum(pe_un, axis=0, keepdims=True), -1.0)
    v1 = jnp.max(pe, axis=0, keepdims=True)
    i1 = jnp.min(jnp.where(pe == v1, row, big), axis=0, keepdims=True)
    pe2 = jnp.where(row == i1, -1.0, pe)
    v2 = jnp.max(pe2, axis=0, keepdims=True)
    i2 = jnp.min(jnp.where(pe2 == v2, row, big), axis=0, keepdims=True)
    wsum = v1 + v2
    w1 = p_grp * v1 / wsum
    w2 = p_grp * v2 / wsum

    oh1 = (row == i1).astype(F32)
    oh2 = (row == i2).astype(F32)
    both = oh1 + oh2
    before = _dot(both.astype(BF16), upper_ref[...]) + carry_ref[...]
    rank1 = jnp.sum(oh1 * before, axis=0, keepdims=True)
    rank2 = jnp.sum(oh2 * before, axis=0, keepdims=True)
    carry_ref[...] = carry_ref[...] + jnp.sum(both, axis=1, keepdims=True)
    cnt_ref[...] = carry_ref[...]
    zero = jnp.zeros_like(w1)
    routet_ref[...] = jnp.concatenate(
        [w1, w2, i1.astype(F32), i2.astype(F32), rank1, rank2, zero, zero], axis=0)


def _out_proj(x, o_a, o_b, gates, mod3, norm2_g, wa, wb, wo, wr, br, n_groups, n_experts, tm):
    bsz, s, d = x.shape
    wdt = o_a.shape[-1]
    upper = jnp.asarray(np.triu(np.ones((tm, tm), np.float32), 1), BF16)
    wrt = wr.T
    wr_hi = wrt.astype(BF16)
    wr_lo = (wrt - wr_hi.astype(F32)).astype(BF16)
    const = lambda b, i: (0, 0)
    tile = lambda b, i: (b, i, 0)
    kern = functools.partial(_out_proj_kernel, n_groups, n_experts)
    return pl.pallas_call(
        kern,
        out_shape=(jax.ShapeDtypeStruct((bsz, s, d), F32),
                   jax.ShapeDtypeStruct((bsz * s * SUB, LANES), U32),
                   jax.ShapeDtypeStruct((bsz * (s // tm) * SUBLANES, tm), F32),
                   jax.ShapeDtypeStruct((LANES, 1), F32)),
        grid=(bsz, s // tm),
        in_specs=[pl.BlockSpec((1, tm, d), tile),
                  pl.BlockSpec((1, tm, wdt), tile),
                  pl.BlockSpec((1, tm, wdt), tile),
                  pl.BlockSpec((1, tm, d), lambda b, i: (b, i, 0)),
                  pl.BlockSpec((1, tm, d), lambda b, i: (b, i, 1)),
                  pl.BlockSpec((1, 1, d), lambda b, i: (b * 6 + 2, 0, 0)),
                  pl.BlockSpec((1, 1, d), lambda b, i: (b * 6 + 4, 0, 0)),
                  pl.BlockSpec((1, 1, d), lambda b, i: (b * 6 + 3, 0, 0)),
                  pl.BlockSpec((1, d), const),
                  pl.BlockSpec(wa.shape, const),
                  pl.BlockSpec(wb.shape, const),
                  pl.BlockSpec(wo.shape, const),
                  pl.BlockSpec(wrt.shape, const),
                  pl.BlockSpec(wrt.shape, const),
                  pl.BlockSpec((LANES, 1), const),
                  pl.BlockSpec((tm, tm), const)],
        out_specs=(pl.BlockSpec((1, tm, d), tile),
                   pl.BlockSpec((tm * SUB, LANES), lambda b, i: (b * (s // tm) + i, 0)),
                   pl.BlockSpec((SUBLANES, tm), lambda b, i: (b * (s // tm) + i, 0)),
                   pl.BlockSpec((LANES, 1), const)),
        scratch_shapes=[pltpu.VMEM((LANES, 1), F32)],
        compiler_params=_cparams(("arbitrary", "arbitrary")),
        name="out_proj",
    )(x, o_a, o_b, gates, gates, mod3, mod3, mod3, norm2_g.reshape(1, d), wa, wb, wo, wr_hi, wr_lo,
      br.reshape(LANES, 1), upper)


def _moe_slotmap_kernel(tm_o, n_tok, blk, dest_ref, zstart_ref, zcnt_ref, nused_ref, smap_ref):
    i = pl.program_id(0)
    n_slots = smap_ref.shape[0] - blk
    n_experts = zcnt_ref.shape[0]

    @pl.when(i == 0)
    def _():
        unroll = 8

        def init(g, carry):
            for u in range(unroll):
                smap_ref[blk + g * unroll + u] = TOP_K * n_tok
            return carry
        lax.fori_loop(nused_ref[0] * (blk // unroll), n_slots // unroll, init, 0)
        for r in range(blk):
            smap_ref[r] = TOP_K * n_tok + n_experts * blk + r
        for e in range(n_experts):
            def pad(g, carry, e=e):
                for u in range(unroll):
                    j = jnp.maximum(zcnt_ref[e] - 1 - (g * unroll + u), 0)
                    smap_ref[blk + zstart_ref[e] + j] = TOP_K * n_tok + e * blk + j
                return carry
            lax.fori_loop(0, (zcnt_ref[e] + unroll - 1) // unroll, pad, 0)

    base = i * (TOP_K * tm_o)
    both = 1 + (1 << SMAP_BITS)
    for k in range(TOP_K):
        v0 = k * n_tok + (i * tm_o) * both
        for r in range(tm_o):
            smap_ref[dest_ref[base + k * tm_o + r]] = v0 + r * both


def _moe_slotmap(dest, zstart, zcnt, n_used, n_slots, blk, tm_o, n_tok):
    grid_spec = pltpu.PrefetchScalarGridSpec(
        num_scalar_prefetch=4,
        grid=(n_tok // tm_o,),
        in_specs=[],
        out_specs=pl.BlockSpec(memory_space=pltpu.SMEM),
    )
    n_experts = zcnt.shape[0]
    assert TOP_K * n_tok + (n_experts + 1) * blk <= (1 << SMAP_BITS) and n_tok <= (1 << (31 - SMAP_BITS))
    return pl.pallas_call(
        functools.partial(_moe_slotmap_kernel, tm_o, n_tok, blk),
        out_shape=jax.ShapeDtypeStruct((blk + n_slots,), jnp.int32),
        grid_spec=grid_spec,
        compiler_params=_cparams(("arbitrary",)),
        name="moe_slotmap",
    )(dest, zstart, zcnt, n_used)


def _moe_expert_gather_kernel(n_tok, blk_e_ref, nused_ref, smap_ref, h_ref, wg_ref, wu_ref, wd_ref, y_ref,
                              hv, xbuf, ystage, wgb, wub, wdb, sem, hsem):
    i = pl.program_id(0)
    nb = pl.num_programs(0)
    n_used = nused_ref[0]
    blk = xbuf.shape[1] // SUB
    first_real_blocks = TOP_K * n_tok // blk

    def gather(b, slot, rows):
        for r in rows:
            tok = lax.shift_right_logical(smap_ref[(b + 1) * blk + r], SMAP_BITS)
            src = pl.multiple_of(tok * SUB, SUB)
            xbuf[slot, pl.ds(r * SUB, SUB), :] = hv[pl.ds(src, SUB), :]

    def issue(b, slot, rows):
        for r in rows:
            t = smap_ref[(b + 1) * blk + r] & ((1 << SMAP_BITS) - 1)
            dst = pl.multiple_of(t * SUB, SUB)
            pltpu.make_async_copy(ystage.at[slot, pl.ds(r * SUB, SUB), :], y_ref.at[pl.ds(dst, SUB), :],
                                  sem.at[slot]).start(priority=r % 2)

    def wait_block(slot):
        pltpu.make_async_copy(ystage.at[slot], y_ref.at[pl.ds(0, blk * SUB), :], sem.at[slot]).wait()

    @pl.when(i == 0)
    def _():
        load = pltpu.make_async_copy(h_ref, hv, hsem)
        load.start()
        ystage[...] = jnp.zeros_like(ystage)
        n_spare_blocks = y_ref.shape[0] // (blk * SUB) - first_real_blocks

        def spare_copy(c):
            dst = (first_real_blocks + c) * blk * SUB
            return pltpu.make_async_copy(ystage.at[0], y_ref.at[pl.ds(dst, blk * SUB), :], sem.at[0])

        for c in range(n_spare_blocks):
            spare_copy(c).start()
        for c in range(n_spare_blocks):
            spare_copy(c).wait()
        load.wait()
        gather(0, 0, range(blk))

    new_expert = (i == 0) | (blk_e_ref[i] != blk_e_ref[jnp.maximum(i - 1, 0)])

    @pl.when((i < n_used) & new_expert)
    def _():
        wgb[...] = wg_ref[0].astype(BF16)
        wub[...] = wu_ref[0].astype(BF16)
        wdb[...] = wd_ref[0].astype(BF16)

    @pl.when(i < n_used)
    def _():
        slot = i % 2
        pslot = 1 - slot
        nxt = jnp.minimum(i + 1, nb - 1)

        @pl.when(i > 0)
        def _():
            wait_block(slot)

        q = blk // 4
        xb = _tiles_to_rows(xbuf.at[slot], blk).astype(BF16)
        issue(i - 1, pslot, range(0, q))
        gather(nxt, pslot, range(0, q))
        hg = _dot(xb, wgb[...])
        issue(i - 1, pslot, range(q, 2 * q))
        gather(nxt, pslot, range(q, 2 * q))
        hu = _dot(xb, wub[...])
        issue(i - 1, pslot, range(2 * q, 3 * q))
        gather(nxt, pslot, range(2 * q, 3 * q))
        hid = (_silu(hg) * hu).astype(BF16)
        y = _dot(hid, wdb[...])
        issue(i - 1, pslot, range(3 * q, blk))
        gather(nxt, pslot, range(3 * q, blk))
        _rows_to_tiles(ystage.at[slot], y)

        @pl.when(i == n_used - 1)
        def _():
            issue(i, slot, range(blk))
            wait_block(pslot)
            wait_block(slot)


def _moe_experts_gather(h2, smap, blk_e, n_used, w_gate, w_up, w_down, blk, n_tok):
    d = w_gate.shape[1]
    assert d == 2 * SUB * LANES and h2.shape == (n_tok * SUB, LANES)
    n_slots = smap.shape[0] - blk
    nb = n_slots // blk
    f = w_gate.shape[-1]
    n_experts = w_gate.shape[0]
    assert (TOP_K * n_tok) % blk == 0
    n_tiles = TOP_K * n_tok + (n_experts + 1) * blk
    grid_spec = pltpu.PrefetchScalarGridSpec(
        num_scalar_prefetch=3,
        grid=(nb,),
        in_specs=[pl.BlockSpec(memory_space=pl.ANY),
                  pl.BlockSpec((1, d, f), lambda i, be, nu, sm: (be[i], 0, 0)),
                  pl.BlockSpec((1, d, f), lambda i, be, nu, sm: (be[i], 0, 0)),
                  pl.BlockSpec((1, f, d), lambda i, be, nu, sm: (be[i], 0, 0))],
        out_specs=pl.BlockSpec(memory_space=pl.ANY),
        scratch_shapes=[pltpu.VMEM((n_tok * SUB, LANES), U32),
                        pltpu.VMEM((2, blk * SUB, LANES), U32),
                        pltpu.VMEM((2, blk * SUB, LANES), U32),
                        pltpu.VMEM((d, f), BF16),
                        pltpu.VMEM((d, f), BF16),
                        pltpu.VMEM((f, d), BF16),
                        pltpu.SemaphoreType.DMA((2,)),
                        pltpu.SemaphoreType.DMA],
    )
    return pl.pallas_call(
        functools.partial(_moe_expert_gather_kernel, n_tok),
        out_shape=jax.ShapeDtypeStruct((n_tiles * SUB, LANES), U32),
        grid_spec=grid_spec,
        compiler_params=_cparams(("arbitrary",)),
        name="moe_experts",
    )(blk_e, n_used, smap, h2, w_gate, w_up, w_down)


def _moe_combine_kernel(x1_ref, routet_ref, gt2_ref, fg_ref, y1_ref, y2_ref, o_ref):
    tm = x1_ref.shape[0]
    route = routet_ref[...].T
    moe = route[:, 0:1] * _tiles_to_rows(y1_ref, tm) + route[:, 1:2] * _tiles_to_rows(y2_ref, tm)
    xo = x1_ref[...] + gt2_ref[0] * moe
    ms = jnp.mean(xo * xo, axis=-1, keepdims=True)
    o_ref[...] = xo * lax.rsqrt(ms + NORM_EPS) * fg_ref[...]


def _moe_combine(x1, route_t, yt, mod3, final_g, s, tm):
    n, d = x1.shape
    tiles_per_batch = s // tm
    n_steps = n // tm
    per_o = route_t.shape[1] // tm
    return pl.pallas_call(
        _moe_combine_kernel,
        out_shape=jax.ShapeDtypeStruct((n, d), F32),
        grid=(n_steps,),
        in_specs=[pl.BlockSpec((tm, d), lambda i: (i, 0)),
                  pl.BlockSpec((SUBLANES, tm), lambda i: (i // per_o, i % per_o)),
                  pl.BlockSpec((1, 1, d), lambda i: ((i // tiles_per_batch) * 6 + 5, 0, 0)),
                  pl.BlockSpec((1, d), lambda i: (0, 0)),
                  pl.BlockSpec((tm * SUB, LANES), lambda i: (i, 0)),
                  pl.BlockSpec((tm * SUB, LANES), lambda i: (n_steps + i, 0))],
        out_specs=pl.BlockSpec((tm, d), lambda i: (i, 0)),
        compiler_params=_cparams(("arbitrary",)),
        name="moe_combine",
    )(x1, route_t, mod3, final_g.reshape(1, d), yt, yt)


def _pick(n, candidates):
    for t in candidates:
        if n % t == 0:
            return t
    raise ValueError(f"no tile in {candidates} divides {n}")


def kernel(x, c, ada_w, ada_b, norm1_g, w_in, hg_lb, hg_norm_g, rw_mu, rw_w0, rw_w2, rw_a0, rw_a2, rw_g2, rw_kk, rw_ka, rw_rk, rw_gn_w, rw_gn_b, w_proj_a, w_proj_b, w_out, norm2_g, router_g_w, router_g_b, router_e_w, router_e_b, exp_w_gate, exp_w_up, exp_w_down, final_g):
    bsz, s, d = x.shape
    depth = ada_w.shape[0]
    hg_f = hg_lb.shape[-1]
    hg_w = hg_norm_g.shape[-1]
    rw_w = rw_w0.shape[-1]
    rw_cols = rw_mu.shape[-1]
    n_groups = router_g_w.shape[-1]
    n_experts = router_e_w.shape[-1]
    assert hg_f == hg_w and s % CHUNK == 0 and n_experts + n_groups <= LANES and d == 2 * SUB * LANES

    lb_all = jnp.cumsum(jax.nn.softmax(hg_lb.astype(F32), axis=0), axis=0)
    n = bsz * s
    blk = 512
    n_blocks = (n * TOP_K + n_experts * blk) // blk
    for l in range(depth):
        mod = _ada_mod(c, ada_w[l], ada_b[l])
        mod3 = mod.reshape(bsz * 6, 1, d)

        hg_cols = 2 * hg_f + 2 * hg_w
        rw_pad = -(-rw_cols // 256) * 256
        w_hg, w_rw, w_gt = _w_in_split(jnp.swapaxes(w_in[l], 0, 1), hg_cols, rw_cols, rw_pad)
        lora = (rw_w2.shape[1], rw_a2.shape[1], rw_g2.shape[1])
        assert rw_pad - 3 * rw_w == 256 and rw_cols == 3 * rw_w + sum(lora)
        hg, rw, gates = _in_proj(x, mod3, norm1_g[l], lb_all[l], rw_mu[l], lora, w_hg, w_rw, w_gt,
                                 _pick(s, (512, 256, 128, 64)))

        o_a = _hgrn2(hg, hg_norm_g[l], _pick(s, (1024, 512, 256, 128, 64)))
        o_b = _rwkv7(rw, rw_w0[l], rw_w2[l], rw_a0[l], rw_a2[l], rw_g2[l],
                     rw_kk[l], rw_ka[l], rw_rk[l].reshape(-1), rw_gn_w[l], rw_gn_b[l],
                     _pick(s, (2 * CHUNK, CHUNK)))

        wr = jnp.zeros((d, LANES), F32).at[:, :n_experts].set(router_e_w[l])
        wr = wr.at[:, n_experts:n_experts + n_groups].set(router_g_w[l])
        br = jnp.zeros((1, LANES), F32).at[0, :n_experts].set(router_e_b[l])
        br = br.at[0, n_experts:n_experts + n_groups].set(router_g_b[l])
        tm_o = _pick(s, (1024, 512, 256, 128, 64))
        x1, h2, route_t, counts = _out_proj(
            x, o_a, o_b, gates, mod3, norm2_g[l],
            w_proj_a[l].astype(BF16), w_proj_b[l].astype(BF16), w_out[l].astype(BF16),
            wr, br, n_groups, n_experts, tm_o)

        rt = route_t.reshape(n // tm_o, SUBLANES, tm_o)
        eid = rt[:, 2:4, :].astype(jnp.int32)
        rank = rt[:, 4:6, :].astype(jnp.int32)
        cnt = counts[:n_experts, 0].astype(jnp.int32)
        padded = (cnt + blk - 1) // blk * blk
        pad_end = jnp.cumsum(padded)
        pad_start = pad_end - padded
        e_ax = jnp.arange(n_experts, dtype=jnp.int32)[:, None, None, None]
        dest = rank + jnp.sum(jnp.where(eid[None] == e_ax, pad_start[:, None, None, None], 0), axis=0)
        blk_start = jnp.arange(n_blocks, dtype=jnp.int32) * blk
        blk_e = jnp.minimum(jnp.sum((pad_end[None, :] <= blk_start[:, None]).astype(jnp.int32), axis=1),
                            n_experts - 1)
        n_used = (pad_end[-1:] // blk).astype(jnp.int32)
        tm = tm_o
        smap = _moe_slotmap(dest.reshape(-1) + blk, pad_start + cnt, padded - cnt, n_used, n_blocks * blk, blk, tm_o, n)
        yt = _moe_experts_gather(h2, smap, blk_e, n_used, exp_w_gate[l], exp_w_up[l], exp_w_down[l], blk, n)
        last = l == depth - 1
        assert last, "the final RMSNorm is fused into the last layer's combine"
        out = _moe_combine(x1.reshape(n, d), route_t, yt, mod3, final_g, s, tm)
        x = out.reshape(bsz, s, d)
    return x
```

```python
import functools

import numpy as np
import jax
import jax.numpy as jnp
from jax import lax
from jax.experimental import pallas as pl
from jax.experimental.pallas import tpu as pltpu

F32 = jnp.float32
BF16 = jnp.bfloat16

NORM_EPS = 1e-6
HG_HEAD = 128
RW_HEAD = 64
RW_GN_EPS = 64e-5
TOP_K = 2
CHUNK = 64
LANES = 128
SUB = 4
U32 = jnp.uint32
MXU_K = 256
SUBLANES = 8
SMAP_BITS = 16
VMEM_LIMIT = 56 * 1024 * 1024

NT = (((1,), (1,)), ((), ()))
TN = (((0,), (0,)), ((), ()))


def _dot(a, b, dims=None, precision=None):
    if dims is None:
        return jnp.dot(a, b, preferred_element_type=F32, precision=precision)
    return lax.dot_general(a, b, dims, preferred_element_type=F32, precision=precision)


def _split3(x):
    hi = x.astype(BF16)
    r1 = x - hi.astype(F32)
    mid = r1.astype(BF16)
    lo = (r1 - mid.astype(F32)).astype(BF16)
    return hi, mid, lo


def _dot_exact_lhs(m3_bf16, x):
    return _dot(m3_bf16, jnp.concatenate(_split3(x), axis=0))


def _head_sums(x, m2_bf16):
    outs = []
    for g0 in range(0, x.shape[1], MXU_K):
        xg = x[:, g0:g0 + MXU_K]
        hi = xg.astype(BF16)
        lo = (xg - hi.astype(F32)).astype(BF16)
        outs.append(_dot(jnp.concatenate([hi, lo], axis=1), m2_bf16))
    return jnp.concatenate(outs, axis=1)


def _sigmoid(x):
    return 1.0 / (1.0 + jnp.exp(-x))


def _silu(x):
    return x * _sigmoid(x)


def _rows_to_tiles(ref, val):
    m, half = val.shape[0], val.shape[1] // 2
    hi = lax.bitcast_convert_type(val[:, :half].astype(BF16).astype(F32), U32)
    lo = lax.bitcast_convert_type(val[:, half:].astype(BF16).astype(F32), U32)
    w = (hi & jnp.uint32(0xFFFF0000)) | (lo >> 16)
    for j in range(SUB):
        ref[pl.ds(j, m, stride=SUB), :] = w[:, j * LANES:(j + 1) * LANES]


def _tiles_to_rows(ref, m, base=0):
    w = jnp.concatenate([ref[pl.ds(base * SUB + j, m, stride=SUB), :] for j in range(SUB)], axis=1)
    hi = lax.bitcast_convert_type(w & jnp.uint32(0xFFFF0000), F32)
    lo = lax.bitcast_convert_type(w << 16, F32)
    return jnp.concatenate([hi, lo], axis=1)


def _cparams(sem):
    return pltpu.CompilerParams(dimension_semantics=sem, vmem_limit_bytes=VMEM_LIMIT)


def _ada_kernel(cb_ref, w_ref, b_ref, o_ref, sc_ref):
    bsz, d = cb_ref.shape[0], cb_ref.shape[1]
    tn = w_ref.shape[1]
    reps = tn // LANES

    @pl.when(pl.program_id(0) == 0)
    def _():
        sc_ref[...] = _silu(cb_ref[...])

    def body(i, accs):
        k0 = pl.multiple_of(i * SUBLANES, SUBLANES)
        w = w_ref[pl.ds(k0, SUBLANES), :]
        out = []
        for b in range(bsz):
            sc = sc_ref[b, pl.ds(k0, SUBLANES), :]
            out.append(accs[b] + w * jnp.concatenate([sc] * reps, axis=1))
        return tuple(out)

    accs = lax.fori_loop(0, d // SUBLANES, body, tuple(jnp.zeros((SUBLANES, tn), F32) for _ in range(bsz)),
                         unroll=4)
    o_ref[...] = jnp.zeros_like(o_ref)
    for b in range(bsz):
        o_ref[b:b + 1, :] = jnp.sum(accs[b], axis=0, keepdims=True) + b_ref[...]


def _ada_mod(c, w, b):
    bsz, d = c.shape
    n = w.shape[1]
    rows = -(-bsz // SUBLANES) * SUBLANES
    cb = jnp.broadcast_to(c[:, :, None], (bsz, d, LANES))
    tn = _pick(n, (1024, 512, 256, 128))
    out = pl.pallas_call(
        _ada_kernel,
        out_shape=jax.ShapeDtypeStruct((rows, n), F32),
        grid=(n // tn,),
        in_specs=[pl.BlockSpec((bsz, d, LANES), lambda j: (0, 0, 0)),
                  pl.BlockSpec((d, tn), lambda j: (0, j)),
                  pl.BlockSpec((1, tn), lambda j: (0, j))],
        out_specs=pl.BlockSpec((rows, tn), lambda j: (0, j)),
        scratch_shapes=[pltpu.VMEM((bsz, d, LANES), F32)],
        compiler_params=_cparams(("arbitrary",)),
        name="ada_mod",
    )(cb, w, b.reshape(1, n))
    return out[:bsz]


def _in_proj_kernel(lora, x_ref, sh_ref, sc_ref, g_ref, lb_ref, mu_ref, whg_ref, wrw_ref, wgt_ref,
                    hg_ref, rw_ref, gt_ref, carry_ref):
    @pl.when(pl.program_id(1) == 0)
    def _():
        carry_ref[...] = jnp.zeros_like(carry_ref)

    x = x_ref[0]
    ms = jnp.mean(x * x, axis=-1, keepdims=True)
    h = (x * lax.rsqrt(ms + NORM_EPS) * g_ref[...]) * (1.0 + sc_ref[0]) + sh_ref[0]
    hb = h.astype(BF16)
    lb = lb_ref[...]
    w = lb.shape[1]
    hg_maps = (_silu, lambda t: lb + (1.0 - lb) * _sigmoid(t), lambda t: t, _silu)
    for part, fn in enumerate(hg_maps):
        hg_ref[0, :, part * w:(part + 1) * w] = fn(_dot(hb, whg_ref[:, part * w:(part + 1) * w]))
    step = 512
    n_rw = wrw_ref.shape[1]
    row = lax.broadcasted_iota(jnp.int32, (x.shape[0], 256), 0)
    lane = lax.broadcasted_iota(jnp.int32, (x.shape[0], 256), 1)
    dl, al, gl = lora
    for n0 in range(0, n_rw, 256):
        cs = slice(n0, n0 + 256)
        p = _dot(hb, wrw_ref[:, cs])
        prev = jnp.where(row == 0, carry_ref[:, cs], pltpu.roll(p, 1, 0))
        carry_ref[:, cs] = p[x.shape[0] - 1:, :]
        xs = p + mu_ref[:, cs] * (prev - p)
        if n0 == n_rw - 256:
            xs = jnp.where(lane < dl, jnp.tanh(xs),
                           jnp.where(lane < dl + al, xs, jnp.where(lane < dl + al + gl, _sigmoid(xs), 0.0)))
        rw_ref[0, :, cs] = xs
    for n0 in range(0, wgt_ref.shape[1], step):
        gt_ref[0, :, n0:n0 + step] = _sigmoid(_dot(hb, wgt_ref[:, n0:n0 + step])).astype(BF16)


def _w_in_split_kernel(chunks, wt_ref, hg_ref, rw_ref, gt_ref, hgv_ref, rwv_ref, gtv_ref, buf_ref, sem_ref, osem_ref):
    outs = (hg_ref, rw_ref, gt_ref)
    stage = (hgv_ref, rwv_ref, gtv_ref)
    nbuf, ch = buf_ref.shape[0], buf_ref.shape[1]

    def copy(k):
        return pltpu.make_async_copy(wt_ref.at[pl.ds(chunks[k][2], ch), :], buf_ref.at[k % nbuf],
                                     sem_ref.at[k % nbuf])

    def flush(oi):
        return pltpu.make_async_copy(stage[oi], outs[oi], osem_ref.at[oi])

    for k in range(min(nbuf - 1, len(chunks))):
        copy(k).start()
    for k, (oi, c0, _, valid) in enumerate(chunks):
        if k + nbuf - 1 < len(chunks):
            copy(k + nbuf - 1).start()
        copy(k).wait()
        t = buf_ref[k % nbuf].T
        if valid < ch:
            t = jnp.where(lax.broadcasted_iota(jnp.int32, t.shape, 1) < valid, t, 0.0)
        stage[oi][:, c0:c0 + ch] = t.astype(BF16)
        if k + 1 == len(chunks) or chunks[k + 1][0] != oi:
            flush(oi).start()
    for oi in range(len(outs)):
        flush(oi).wait()


def _w_in_split(w_in_t, n_hg, n_rw, n_rw_pad):
    n_in, d = w_in_t.shape
    n_gt = n_in - n_hg - n_rw
    ch = 256
    assert n_hg % ch == 0 and n_rw_pad % ch == 0 and n_gt % ch == 0 and n_gt >= ch
    chunks = []
    for oi, (r0, nr, npad) in enumerate(((0, n_hg, n_hg), (n_hg, n_rw, n_rw_pad), (n_hg + n_rw, n_gt, n_gt))):
        chunks += [(oi, c0, r0 + c0, min(ch, nr - c0)) for c0 in range(0, npad, ch)]
    assert all(v > 0 for _, _, _, v in chunks)
    return pl.pallas_call(
        functools.partial(_w_in_split_kernel, tuple(chunks)),
        out_shape=(jax.ShapeDtypeStruct((d, n_hg), BF16),
                   jax.ShapeDtypeStruct((d, n_rw_pad), BF16),
                   jax.ShapeDtypeStruct((d, n_gt), BF16)),
        in_specs=[pl.BlockSpec(memory_space=pl.ANY)],
        out_specs=tuple(pl.BlockSpec(memory_space=pl.ANY) for _ in range(3)),
        scratch_shapes=[pltpu.VMEM((d, n_hg), BF16), pltpu.VMEM((d, n_rw_pad), BF16), pltpu.VMEM((d, n_gt), BF16),
                        pltpu.VMEM((4, ch, d), F32), pltpu.SemaphoreType.DMA((4,)), pltpu.SemaphoreType.DMA((3,))],
        compiler_params=pltpu.CompilerParams(vmem_limit_bytes=VMEM_LIMIT),
        name="w_in_split",
    )(w_in_t)


def _in_proj(x, mod3, norm_g, lb, mu, lora, w_hg, w_rw, w_gt, tm):
    bsz, s, d = x.shape
    assert w_hg.shape[1] == 4 * lb.shape[0] and sum(lora) <= 256
    mup = jnp.zeros((1, w_rw.shape[1]), F32).at[0, :mu.shape[-1]].set(mu)
    n_hg, n_rw, n_gt = w_hg.shape[1], w_rw.shape[1], w_gt.shape[1]
    const = lambda b, i: (0, 0)
    return pl.pallas_call(
        functools.partial(_in_proj_kernel, lora),
        out_shape=(jax.ShapeDtypeStruct((bsz, s, n_hg), F32),
                   jax.ShapeDtypeStruct((bsz, s, n_rw), F32),
                   jax.ShapeDtypeStruct((bsz, s, n_gt), BF16)),
        grid=(bsz, s // tm),
        in_specs=[pl.BlockSpec((1, tm, d), lambda b, i: (b, i, 0)),
                  pl.BlockSpec((1, 1, d), lambda b, i: (b * 6 + 0, 0, 0)),
                  pl.BlockSpec((1, 1, d), lambda b, i: (b * 6 + 1, 0, 0)),
                  pl.BlockSpec((1, d), const),
                  pl.BlockSpec((1, lb.shape[0]), const),
                  pl.BlockSpec((1, n_rw), const),
                  pl.BlockSpec((d, n_hg), const),
                  pl.BlockSpec((d, n_rw), const),
                  pl.BlockSpec((d, n_gt), const)],
        out_specs=(pl.BlockSpec((1, tm, n_hg), lambda b, i: (b, i, 0)),
                   pl.BlockSpec((1, tm, n_rw), lambda b, i: (b, i, 0)),
                   pl.BlockSpec((1, tm, n_gt), lambda b, i: (b, i, 0))),
        scratch_shapes=[pltpu.VMEM((1, n_rw), F32)],
        compiler_params=_cparams(("arbitrary", "arbitrary")),
        name="in_proj",
    )(x, mod3, mod3, norm_g.reshape(1, d), lb.reshape(1, -1), mup, w_hg, w_rw, w_gt)


_HG_LEVELS = (32, 16, 8, 4, 2, 1)


def _hgrn2_consts(width):
    c = CHUNK
    t = np.arange(c)[:, None]
    s = np.arange(c)[None, :]
    blocks = [(s <= t), (s > t)]
    lvl_masks = []
    right = []
    for h in _HG_LEVELS:
        m = (t // (2 * h)) * 2 * h + h
        is_r = (t & h) != 0
        blk = np.where(is_r, (s >= m) & (s <= t), (s > t) & (s <= m - 1))
        blocks.append(blk)
        lvl_masks.append(is_r & ((s & h) == 0) & ((t // (2 * h)) == (s // (2 * h))))
        right.append(np.broadcast_to(is_r, (c, width)))
    mst = np.tile(np.concatenate(blocks, axis=0).astype(np.float32), (1, 3))
    lm = np.stack([np.eye(c, dtype=bool)] + lvl_masks).astype(np.float32)
    rm = np.stack(right).astype(np.float32)
    return jnp.asarray(mst, BF16), jnp.asarray(lm, F32), jnp.asarray(rm, F32)


def _hgrn2_kernel(q_ref, f_ref, i_ref, g_ref, ng_ref, mst_ref, lm_ref, rm_ref, o_ref, st_ref):
    c = CHUNK
    n_chunks = q_ref.shape[1] // c

    @pl.when(pl.program_id(1) == 0)
    def _():
        st_ref[...] = jnp.zeros_like(st_ref)

    nsub = next(n for n in (8, 4, 2, 1) if n_chunks % n == 0)

    def chunk_body(ci, carry):
        r0 = pl.multiple_of(ci * (nsub * c), nsub * c)
        for _ in _hgrn2_steps(q_ref, f_ref, i_ref, g_ref, ng_ref, mst_ref, lm_ref, rm_ref, o_ref, st_ref,
                              0, r0, nsub):
            pass
        return carry

    lax.fori_loop(0, n_chunks // nsub, chunk_body, 0)


def _hgrn2_steps(q_ref, f_ref, i_ref, g_ref, ng_ref, mst_ref, lm_ref, rm_ref, o_ref, st_ref, b, r0, nsub):
    c = CHUNK
    n_heads = q_ref.shape[2] // HG_HEAD
    mst = mst_ref[...]
    ng = ng_ref[...]
    heads = [slice(hd * HG_HEAD, (hd + 1) * HG_HEAD) for hd in range(n_heads)]
    subs = []
    for j in range(nsub):
        rows = pl.ds(r0 + j * c, c)
        q = q_ref[b, rows, :]
        f = f_ref[b, rows, :]
        k = 1.0 - f
        ex = jnp.exp2(_dot_exact_lhs(mst, jnp.log2(f)))
        subs.append(dict(rows=rows, q=q, k=k, ex=ex, vb=i_ref[b, rows, :].astype(BF16),
                         qd=(q * ex[0:c]).astype(BF16), kr=(k * ex[c:2 * c]).astype(BF16)))
        yield
    for sb in subs:
        qb, kb = sb['q'].astype(BF16), sb['k'].astype(BF16)
        sb['sc'] = [lm_ref[0] * _dot(qb[:, ls], kb[:, ls], NT) for ls in heads]
        sb['dqk'] = sb['q'] - sb['k']
    yield
    for li, h in enumerate(_HG_LEVELS):
        for sb in subs:
            if h % SUBLANES == 0:
                qk = jnp.concatenate([(sb['q'] if m % 2 else sb['k'])[m * h:(m + 1) * h] for m in range(c // h)],
                                     axis=0)
            else:
                qk = sb['k'] + rm_ref[li] * sb['dqk']
            g_l = (qk * sb['ex'][(2 + li) * c:(3 + li) * c]).astype(BF16)
            sb['sc'] = [s_h + lm_ref[li + 1] * _dot(g_l[:, ls], g_l[:, ls], NT)
                        for s_h, ls in zip(sb['sc'], heads)]
        yield
    for sb in subs:
        sb['kv'] = [_dot(sb['vb'][:, ls], sb['kr'][:, ls], TN) for ls in heads]
        sb['o'] = [_dot(s_h.astype(BF16), sb['vb'][:, ls]) for s_h, ls in zip(sb['sc'], heads)]
        yield
    sts = [st_ref[b, hd] for hd in range(n_heads)]
    for sb in subs:
        sb['o'] = [o_h + _dot(sb['qd'][:, ls], st.astype(BF16), NT) for o_h, ls, st in zip(sb['o'], heads, sts)]
        sts = [st * sb['ex'][c - 1:c, ls] + kv for st, ls, kv in zip(sts, heads, sb['kv'])]
        yield
    for hd in range(n_heads):
        st_ref[b, hd] = sts[hd]
    for sb in subs:
        on = [o_h * lax.rsqrt(jnp.mean(o_h * o_h, axis=-1, keepdims=True) + NORM_EPS) for o_h in sb['o']]
        o_full = jnp.concatenate(on, axis=1) * ng
        o_ref[b, sb['rows'], :] = (o_full * g_ref[b, sb['rows'], :]).astype(o_ref.dtype)
        yield


def _hgrn2(hg, norm_g, ts):
    bsz, s, n4 = hg.shape
    w = n4 // 4
    mst, lm, rm = _hgrn2_consts(w)
    n_heads = w // HG_HEAD
    const2 = lambda b, i: (0, 0)
    const3 = lambda b, i: (0, 0, 0)
    return pl.pallas_call(
        _hgrn2_kernel,
        out_shape=jax.ShapeDtypeStruct((bsz, s, w), BF16),
        grid=(bsz, s // ts),
        in_specs=[pl.BlockSpec((1, ts, w), lambda b, i: (b, i, 0)),
                  pl.BlockSpec((1, ts, w), lambda b, i: (b, i, 1)),
                  pl.BlockSpec((1, ts, w), lambda b, i: (b, i, 2)),
                  pl.BlockSpec((1, ts, w), lambda b, i: (b, i, 3)),
                  pl.BlockSpec((1, w), const2),
                  pl.BlockSpec(mst.shape, const2),
                  pl.BlockSpec(lm.shape, const3),
                  pl.BlockSpec(rm.shape, const3)],
        out_specs=pl.BlockSpec((1, ts, w), lambda b, i: (b, i, 0)),
        scratch_shapes=[pltpu.VMEM((1, n_heads, HG_HEAD, HG_HEAD), F32)],
        compiler_params=_cparams(("arbitrary", "arbitrary")),
        name="hgrn2",
    )(hg, hg, hg, hg, norm_g.reshape(1, w), mst, lm, rm)


def _rwkv_consts(width):
    c = CHUNK
    t = np.arange(c)[:, None]
    s = np.arange(c)[None, :]
    tri = np.tile((s <= t).astype(np.float32), (1, 3))
    tt = np.arange(2 * c)[:, None]
    ss = np.arange(2 * c)[None, :]
    same = (tt // c) == (ss // c)
    strict = same & ((ss % c) < (tt % c))
    incl = same & ((ss % c) <= (tt % c))
    hsum = (np.arange(MXU_K)[:, None] // RW_HEAD) == (np.arange(MXU_K)[None, :] // RW_HEAD)
    hsum = np.tile(hsum, (2, 1))
    return (jnp.asarray(tri, BF16), jnp.asarray(strict.astype(np.float32), F32),
            jnp.asarray(incl.astype(np.float32), F32), jnp.asarray(hsum.astype(np.float32), BF16))


def _rwkv7_kernel(p_ref, w0_ref, a0_ref, kk_ref, ka_ref, rk_ref, gnw_ref, gnb_ref,
                  w2_ref, a2_ref, g2_ref, tri_ref, sm_ref, im_ref, hs_ref,
                  o_ref, zt_ref):
    @pl.when(pl.program_id(0) == 0)
    def _():
        zt_ref[...] = jnp.zeros_like(zt_ref)

    for _ in _rwkv7_steps(p_ref, w0_ref, a0_ref, kk_ref, ka_ref, rk_ref, gnw_ref, gnb_ref,
                          w2_ref, a2_ref, g2_ref, tri_ref, sm_ref, im_ref, hs_ref, o_ref, zt_ref):
        pass


def _rwkv7_steps(p_ref, w0_ref, a0_ref, kk_ref, ka_ref, rk_ref, gnw_ref, gnb_ref,
                 w2_ref, a2_ref, g2_ref, tri_ref, sm_ref, im_ref, hs_ref, o_ref, zt_ref):
    c = CHUNK
    nb = p_ref.shape[0]
    nch = p_ref.shape[1] // c
    width = o_ref.shape[2]
    n_pairs = width // LANES

    hs = hs_ref[...]
    tri = tri_ref[...]
    smask = sm_ref[...] > 0
    imask = im_ref[...] > 0
    lane = lax.broadcasted_iota(jnp.int32, (c, LANES), 1)
    m0 = (lane < RW_HEAD).astype(F32)
    m1 = 1.0 - m0

    def stack(x):
        return jnp.concatenate([x * m0, x * m1], axis=0)

    xs = jnp.concatenate([p_ref[b] for b in range(nb)], axis=0)
    r_all = xs[:, 0:width]
    k_all = xs[:, width:2 * width]
    v_all = xs[:, 2 * width:3 * width]
    slab = xs[:, 3 * width:].astype(BF16)
    nz = -(w0_ref[...] + _dot(slab, w2_ref[...]))
    softplus = jnp.maximum(nz, 0.0) + jnp.log(1.0 + jnp.exp(-jnp.abs(nz)))
    ld_all = -jnp.exp(-softplus - 0.5)
    a_all = _sigmoid(a0_ref[...] + _dot(slab, a2_ref[...]))
    g_all = _dot(slab, g2_ref[...])
    kk0 = k_all * kk_ref[...]
    kk_all = kk0 * lax.rsqrt(jnp.maximum(_head_sums(kk0 * kk0, hs), 1e-24))
    k2_all = k_all * (1.0 + (a_all - 1.0) * ka_ref[...])
    yield

    units = []
    for b, j in [(b, j) for b in range(nb) for j in range(nch)]:
        rb = slice((b * nch + j) * c, (b * nch + j + 1) * c)
        r, k2, v, ld = r_all[rb], k2_all[rb], v_all[rb], ld_all[rb]
        a_in = -kk_all[rb]
        b_in = kk_all[rb] * a_all[rb]
        cum = _dot_exact_lhs(tri, ld)
        cum_t = cum[c - 1:c, :]
        e_c = jnp.exp(cum)
        e_nc = jnp.exp(-cum)
        e_rem = jnp.exp(cum_t - cum)
        at_f = a_in * jnp.exp(cum - ld)
        rt_f = r * e_c
        kt_f = k2 * e_nc
        bt_f = b_in * e_nc
        kh_f = k2 * e_rem
        bh_f = b_in * e_rem
        p_t = jnp.exp(cum_t)
        for pi in range(n_pairs):
            ls = slice(pi * LANES, (pi + 1) * LANES)
            units.append(dict(
                b=b, j=j, pi=pi,
                at=stack(at_f[:, ls]).astype(BF16), rt=stack(rt_f[:, ls]).astype(BF16),
                kt=stack(kt_f[:, ls]).astype(BF16), bt=stack(bt_f[:, ls]).astype(BF16),
                kh=stack(kh_f[:, ls]).astype(BF16), bh=stack(bh_f[:, ls]).astype(BF16),
                vs=stack(v[:, ls]).astype(BF16), p_t=p_t[:, ls]))
        yield

    for u in units:
        lhs = jnp.concatenate([u['at'], u['rt']], axis=0)
        u['g'] = _dot(lhs, jnp.concatenate([u['kt'], u['bt']], axis=0), NT)
    yield
    for u in units:
        g = u.pop('g')
        u['a_ak'] = jnp.where(smask, g[:2 * c, :2 * c], 0.0).astype(BF16)
        u['pw'] = jnp.where(smask, g[:2 * c, 2 * c:], 0.0).astype(BF16)
        u['a_r'] = jnp.where(jnp.concatenate([imask, imask], axis=1), g[2 * c:], 0.0).astype(BF16)
    for u in units:
        akv = _dot(u.pop('a_ak'), u['vs'])
        u['x'] = jnp.concatenate([u['at'].astype(F32), akv], axis=1)
    yield
    n_lvl = int(np.log2(c))
    for lvl in range(n_lvl):
        for u in units:
            u['x'] = u['x'] + _dot(u['pw'], u['x'].astype(BF16))
        yield
        if lvl + 1 < n_lvl:
            for u in units:
                u['pw'] = _dot(u['pw'], u['pw']).astype(BF16)
            yield
    for u in units:
        x = u.pop('x')
        u['wr'] = jnp.concatenate([x[:, :LANES].astype(BF16), u['rt']], axis=0)
        u['u_loc'] = x[:, LANES:]
    zt = {(b, pi): zt_ref[b, pi] for b in range(nb) for pi in range(n_pairs)}
    for j in range(nch):
        tail = [u for u in units if u['j'] == j]
        for u in tail:
            u['uy'] = _dot(u.pop('wr'), zt[u['b'], u['pi']].astype(BF16), NT)
        yield
        for u in tail:
            uy = u.pop('uy')
            u['u'] = (uy[:2 * c] + u.pop('u_loc')).astype(BF16)
            u['y0'] = uy[2 * c:]
        for u in tail:
            vu = jnp.concatenate([u['vs'], u['u']], axis=0)
            u['y'] = u.pop('y0') + _dot(u['a_r'], vu)
            upd = _dot(vu, jnp.concatenate([u['kh'], u['bh']], axis=0), TN)
            zt[u['b'], u['pi']] = zt[u['b'], u['pi']] * u['p_t'] + upd
        yield
    for (b, pi), z in zt.items():
        zt_ref[b, pi] = z

    inv_n = 1.0 / RW_HEAD
    y = jnp.concatenate(
        [jnp.concatenate([u['y'][:c] + u['y'][c:] for u in units if (u['b'], u['j']) == (b, j)], axis=1)
         for b in range(nb) for j in range(nch)], axis=0)
    mean = _head_sums(y, hs) * inv_n
    yield
    d = y - mean
    var = _head_sums(d * d, hs) * inv_n
    yield
    yn = d * lax.rsqrt(var + RW_GN_EPS) * gnw_ref[...] + gnb_ref[...]
    bonus = _head_sums(r_all * k2_all * rk_ref[...], hs) * v_all
    out = ((yn + bonus) * g_all).astype(o_ref.dtype)
    for b in range(nb):
        o_ref[b] = out[b * nch * c:(b + 1) * nch * c]


def _rwkv7(rw, w0, w2, a0, a2, g2, k_k, k_a, r_k, gn_w, gn_b, ts):
    bsz, s, cols = rw.shape
    width = w0.shape[-1]
    n_pairs = width // LANES
    slab = cols - 3 * width
    dl, al, gl = w2.shape[0], a2.shape[0], g2.shape[0]
    w2f = jnp.zeros((slab, width), F32).at[0:dl].set(w2).astype(BF16)
    a2f = jnp.zeros((slab, width), F32).at[dl:dl + al].set(a2).astype(BF16)
    g2f = jnp.zeros((slab, width), F32).at[dl + al:dl + al + gl].set(g2).astype(BF16)
    tri, sm, im, hs = _rwkv_consts(width)
    row = lambda x: x.reshape(1, width)
    const = lambda i: (0, 0)
    vec = pl.BlockSpec((1, width), const)
    return pl.pallas_call(
        _rwkv7_kernel,
        out_shape=jax.ShapeDtypeStruct((bsz, s, width), BF16),
        grid=(s // ts,),
        in_specs=[pl.BlockSpec((bsz, ts, cols), lambda i: (0, i, 0)),
                  vec, vec, vec, vec, vec, vec, vec,
                  pl.BlockSpec((slab, width), const),
                  pl.BlockSpec((slab, width), const),
                  pl.BlockSpec((slab, width), const),
                  pl.BlockSpec(tri.shape, const),
                  pl.BlockSpec(sm.shape, const),
                  pl.BlockSpec(im.shape, const),
                  pl.BlockSpec(hs.shape, const)],
        out_specs=pl.BlockSpec((bsz, ts, width), lambda i: (0, i, 0)),
        scratch_shapes=[pltpu.VMEM((bsz, n_pairs, LANES, LANES), F32)],
        compiler_params=_cparams(("arbitrary",)),
        name="rwkv7",
    )(rw, row(w0), row(a0), row(k_k), row(k_a), row(r_k), row(gn_w), row(gn_b),
      w2f, a2f, g2f, tri, sm, im, hs)


def _out_proj_kernel(n_groups, n_experts,
                     x_ref, oa_ref, ob_ref, ga_ref, gb_ref, gt1_ref, sc2_ref, sh2_ref, g2_ref,
                     wa_ref, wb_ref, wo_ref, wr_ref, wrl_ref, br_ref, upper_ref,
                     x1_ref, h2_ref, routet_ref, cnt_ref, carry_ref):
    first = (pl.program_id(0) == 0) & (pl.program_id(1) == 0)

    @pl.when(first)
    def _():
        carry_ref[...] = jnp.zeros_like(carry_ref)

    pa = _dot(oa_ref[0], wa_ref[...])
    pb = _dot(ob_ref[0], wb_ref[...])
    mixed = ga_ref[0].astype(F32) * pa + gb_ref[0].astype(F32) * pb
    x1 = x_ref[0] + gt1_ref[0] * _dot(mixed.astype(BF16), wo_ref[...])
    x1_ref[0] = x1
    ms = jnp.mean(x1 * x1, axis=-1, keepdims=True)
    h2 = (x1 * lax.rsqrt(ms + NORM_EPS) * g2_ref[...]) * (1.0 + sc2_ref[0]) + sh2_ref[0]
    _rows_to_tiles(h2_ref, h2)

    h2_hi = h2.astype(BF16)
    h2_lo = (h2 - h2_hi.astype(F32)).astype(BF16)
    logits = (_dot(wr_ref[...], h2_hi, NT) + _dot(wr_ref[...], h2_lo, NT) + _dot(wrl_ref[...], h2_hi, NT)
              + br_ref[...])
    row = lax.broadcasted_iota(jnp.int32, logits.shape, 0)
    neg = jnp.float32(-jnp.inf)
    big = jnp.int32(1 << 20)
    eg = n_experts // n_groups
    is_g = (row >= n_experts) & (row < n_experts + n_groups)
    lg = jnp.where(is_g, logits, neg)
    mg = jnp.max(lg, axis=0, keepdims=True)
    p_grp = 1.0 / jnp.sum(jnp.where(is_g, jnp.exp(lg - mg), 0.0), axis=0, keepdims=True)
    gidx = jnp.min(jnp.where(lg == mg, row, big), axis=0, keepdims=True) - n_experts
    sel = (row >= gidx * eg) & (row < gidx * eg + eg)
    le = jnp.where(sel, logits, neg)
    me = jnp.max(le, axis=0, keepdims=True)
    pe_un = jnp.where(sel, jnp.exp(le - me), 0.0)
    pe = jnp.where(sel, pe_un / jnp.sum(pe_un, axis=0, keepdims=True), -1.0)
    v1 = jnp.max(pe, axis=0, keepdims=True)
    i1 = jnp.min(jnp.where(pe == v1, row, big), axis=0, keepdims=True)
    pe2 = jnp.where(row == i1, -1.0, pe)
    v2 = jnp.max(pe2, axis=0, keepdims=True)
    i2 = jnp.min(jnp.where(pe2 == v2, row, big), axis=0, keepdims=True)
    wsum = v1 + v2
    w1 = p_grp * v1 / wsum
    w2 = p_grp * v2 / wsum

    oh1 = (row == i1).astype(F32)
    oh2 = (row == i2).astype(F32)
    both = oh1 + oh2
    before = _dot(both.astype(BF16), upper_ref[...]) + carry_ref[...]
    rank1 = jnp.sum(oh1 * before, axis=0, keepdims=True)
    rank2 = jnp.sum(oh2 * before, axis=0, keepdims=True)
    carry_ref[...] = carry_ref[...] + jnp.sum(both, axis=1, keepdims=True)
    cnt_ref[...] = carry_ref[...]
    zero = jnp.zeros_like(w1)
    routet_ref[...] = jnp.concatenate(
        [w1, w2, i1.astype(F32), i2.astype(F32), rank1, rank2, zero, zero], axis=0)


def _out_proj(x, o_a, o_b, gates, mod3, norm2_g, wa, wb, wo, wr, br, n_groups, n_experts, tm):
    bsz, s, d = x.shape
    wdt = o_a.shape[-1]
    upper = jnp.asarray(np.triu(np.ones((tm, tm), np.float32), 1), BF16)
    wrt = wr.T
    wr_hi = wrt.astype(BF16)
    wr_lo = (wrt - wr_hi.astype(F32)).astype(BF16)
    const = lambda b, i: (0, 0)
    tile = lambda b, i: (b, i, 0)
    kern = functools.partial(_out_proj_kernel, n_groups, n_experts)
    return pl.pallas_call(
        kern,
        out_shape=(jax.ShapeDtypeStruct((bsz, s, d), F32),
                   jax.ShapeDtypeStruct((bsz * s * SUB, LANES), U32),
                   jax.ShapeDtypeStruct((bsz * (s // tm) * SUBLANES, tm), F32),
                   jax.ShapeDtypeStruct((LANES, 1), F32)),
        grid=(bsz, s // tm),
        in_specs=[pl.BlockSpec((1, tm, d), tile),
                  pl.BlockSpec((1, tm, wdt), tile),
                  pl.BlockSpec((1, tm, wdt), tile),
                  pl.BlockSpec((1, tm, d), lambda b, i: (b, i, 0)),
                  pl.BlockSpec((1, tm, d), lambda b, i: (b, i, 1)),
                  pl.BlockSpec((1, 1, d), lambda b, i: (b * 6 + 2, 0, 0)),
                  pl.BlockSpec((1, 1, d), lambda b, i: (b * 6 + 4, 0, 0)),
                  pl.BlockSpec((1, 1, d), lambda b, i: (b * 6 + 3, 0, 0)),
                  pl.BlockSpec((1, d), const),
                  pl.BlockSpec(wa.shape, const),
                  pl.BlockSpec(wb.shape, const),
                  pl.BlockSpec(wo.shape, const),
                  pl.BlockSpec(wrt.shape, const),
                  pl.BlockSpec(wrt.shape, const),
                  pl.BlockSpec((LANES, 1), const),
                  pl.BlockSpec((tm, tm), const)],
        out_specs=(pl.BlockSpec((1, tm, d), tile),
                   pl.BlockSpec((tm * SUB, LANES), lambda b, i: (b * (s // tm) + i, 0)),
                   pl.BlockSpec((SUBLANES, tm), lambda b, i: (b * (s // tm) + i, 0)),
                   pl.BlockSpec((LANES, 1), const)),
        scratch_shapes=[pltpu.VMEM((LANES, 1), F32)],
        compiler_params=_cparams(("arbitrary", "arbitrary")),
        name="out_proj",
    )(x, o_a, o_b, gates, gates, mod3, mod3, mod3, norm2_g.reshape(1, d), wa, wb, wo, wr_hi, wr_lo,
      br.reshape(LANES, 1), upper)


def _moe_slotmap_kernel(tm_o, n_tok, blk, dest_ref, zstart_ref, zcnt_ref, nused_ref, smap_ref):
    i = pl.program_id(0)
    n_slots = smap_ref.shape[0] - blk
    n_experts = zcnt_ref.shape[0]

    @pl.when(i == 0)
    def _():
        unroll = 8

        def init(g, carry):
            for u in range(unroll):
                smap_ref[blk + g * unroll + u] = TOP_K * n_tok
            return carry
        lax.fori_loop(nused_ref[0] * (blk // unroll), n_slots // unroll, init, 0)
        for r in range(blk):
            smap_ref[r] = TOP_K * n_tok + n_experts * blk + r
        for e in range(n_experts):
            def pad(g, carry, e=e):
                for u in range(unroll):
                    j = jnp.maximum(zcnt_ref[e] - 1 - (g * unroll + u), 0)
                    smap_ref[blk + zstart_ref[e] + j] = TOP_K * n_tok + e * blk + j
                return carry
            lax.fori_loop(0, (zcnt_ref[e] + unroll - 1) // unroll, pad, 0)

    base = i * (TOP_K * tm_o)
    both = 1 + (1 << SMAP_BITS)
    for k in range(TOP_K):
        v0 = k * n_tok + (i * tm_o) * both
        for r in range(tm_o):
            smap_ref[dest_ref[base + k * tm_o + r]] = v0 + r * both


def _moe_slotmap(dest, zstart, zcnt, n_used, n_slots, blk, tm_o, n_tok):
    grid_spec = pltpu.PrefetchScalarGridSpec(
        num_scalar_prefetch=4,
        grid=(n_tok // tm_o,),
        in_specs=[],
        out_specs=pl.BlockSpec(memory_space=pltpu.SMEM),
    )
    n_experts = zcnt.shape[0]
    assert TOP_K * n_tok + (n_experts + 1) * blk <= (1 << SMAP_BITS) and n_tok <= (1 << (31 - SMAP_BITS))
    return pl.pallas_call(
        functools.partial(_moe_slotmap_kernel, tm_o, n_tok, blk),
        out_shape=jax.ShapeDtypeStruct((blk + n_slots,), jnp.int32),
        grid_spec=grid_spec,
        compiler_params=_cparams(("arbitrary",)),
        name="moe_slotmap",
    )(dest, zstart, zcnt, n_used)


def _moe_expert_gather_kernel(n_tok, blk_e_ref, nused_ref, smap_ref, h_ref, wg_ref, wu_ref, wd_ref, y_ref,
                              hv, xbuf, ystage, wgb, wub, wdb, sem, hsem):
    i = pl.program_id(0)
    nb = pl.num_programs(0)
    n_used = nused_ref[0]
    blk = xbuf.shape[1] // SUB
    first_real_blocks = TOP_K * n_tok // blk

    def gather(b, slot, rows):
        for r in rows:
            tok = lax.shift_right_logical(smap_ref[(b + 1) * blk + r], SMAP_BITS)
            src = pl.multiple_of(tok * SUB, SUB)
            xbuf[slot, pl.ds(r * SUB, SUB), :] = hv[pl.ds(src, SUB), :]

    def issue(b, slot, rows):
        for r in rows:
            t = smap_ref[(b + 1) * blk + r] & ((1 << SMAP_BITS) - 1)
            dst = pl.multiple_of(t * SUB, SUB)
            pltpu.make_async_copy(ystage.at[slot, pl.ds(r * SUB, SUB), :], y_ref.at[pl.ds(dst, SUB), :],
                                  sem.at[slot]).start(priority=r % 2)

    def wait_block(slot):
        pltpu.make_async_copy(ystage.at[slot], y_ref.at[pl.ds(0, blk * SUB), :], sem.at[slot]).wait()

    @pl.when(i == 0)
    def _():
        load = pltpu.make_async_copy(h_ref, hv, hsem)
        load.start()
        ystage[...] = jnp.zeros_like(ystage)
        n_spare_blocks = y_ref.shape[0] // (blk * SUB) - first_real_blocks

        def spare_copy(c):
            dst = (first_real_blocks + c) * blk * SUB
            return pltpu.make_async_copy(ystage.at[0], y_ref.at[pl.ds(dst, blk * SUB), :], sem.at[0])

        for c in range(n_spare_blocks):
            spare_copy(c).start()
        for c in range(n_spare_blocks):
            spare_copy(c).wait()
        load.wait()
        gather(0, 0, range(blk))

    new_expert = (i == 0) | (blk_e_ref[i] != blk_e_ref[jnp.maximum(i - 1, 0)])

    @pl.when((i < n_used) & new_expert)
    def _():
        wgb[...] = wg_ref[0].astype(BF16)
        wub[...] = wu_ref[0].astype(BF16)
        wdb[...] = wd_ref[0].astype(BF16)

    @pl.when(i < n_used)
    def _():
        slot = i % 2
        pslot = 1 - slot
        nxt = jnp.minimum(i + 1, nb - 1)

        @pl.when(i > 0)
        def _():
            wait_block(slot)

        q = blk // 4
        xb = _tiles_to_rows(xbuf.at[slot], blk).astype(BF16)
        issue(i - 1, pslot, range(0, q))
        gather(nxt, pslot, range(0, q))
        hg = _dot(xb, wgb[...])
        issue(i - 1, pslot, range(q, 2 * q))
        gather(nxt, pslot, range(q, 2 * q))
        hu = _dot(xb, wub[...])
        issue(i - 1, pslot, range(2 * q, 3 * q))
        gather(nxt, pslot, range(2 * q, 3 * q))
        hid = (_silu(hg) * hu).astype(BF16)
        y = _dot(hid, wdb[...])
        issue(i - 1, pslot, range(3 * q, blk))
        gather(nxt, pslot, range(3 * q, blk))
        _rows_to_tiles(ystage.at[slot], y)

        @pl.when(i == n_used - 1)
        def _():
            issue(i, slot, range(blk))
            wait_block(pslot)
            wait_block(slot)


def _moe_experts_gather(h2, smap, blk_e, n_used, w_gate, w_up, w_down, blk, n_tok):
    d = w_gate.shape[1]
    assert d == 2 * SUB * LANES and h2.shape == (n_tok * SUB, LANES)
    n_slots = smap.shape[0] - blk
    nb = n_slots // blk
    f = w_gate.shape[-1]
    n_experts = w_gate.shape[0]
    assert (TOP_K * n_tok) % blk == 0
    n_tiles = TOP_K * n_tok + (n_experts + 1) * blk
    grid_spec = pltpu.PrefetchScalarGridSpec(
        num_scalar_prefetch=3,
        grid=(nb,),
        in_specs=[pl.BlockSpec(memory_space=pl.ANY),
                  pl.BlockSpec((1, d, f), lambda i, be, nu, sm: (be[i], 0, 0)),
                  pl.BlockSpec((1, d, f), lambda i, be, nu, sm: (be[i], 0, 0)),
                  pl.BlockSpec((1, f, d), lambda i, be, nu, sm: (be[i], 0, 0))],
        out_specs=pl.BlockSpec(memory_space=pl.ANY),
        scratch_shapes=[pltpu.VMEM((n_tok * SUB, LANES), U32),
                        pltpu.VMEM((2, blk * SUB, LANES), U32),
                        pltpu.VMEM((2, blk * SUB, LANES), U32),
                        pltpu.VMEM((d, f), BF16),
                        pltpu.VMEM((d, f), BF16),
                        pltpu.VMEM((f, d), BF16),
                        pltpu.SemaphoreType.DMA((2,)),
                        pltpu.SemaphoreType.DMA],
    )
    return pl.pallas_call(
        functools.partial(_moe_expert_gather_kernel, n_tok),
        out_shape=jax.ShapeDtypeStruct((n_tiles * SUB, LANES), U32),
        grid_spec=grid_spec,
        compiler_params=_cparams(("arbitrary",)),
        name="moe_experts",
    )(blk_e, n_used, smap, h2, w_gate, w_up, w_down)


def _moe_combine_kernel(x1_ref, routet_ref, gt2_ref, fg_ref, y1_ref, y2_ref, o_ref):
    tm = x1_ref.shape[0]
    route = routet_ref[...].T
    moe = route[:, 0:1] * _tiles_to_rows(y1_ref, tm) + route[:, 1:2] * _tiles_to_rows(y2_ref, tm)
    xo = x1_ref[...] + gt2_ref[0] * moe
    ms = jnp.mean(xo * xo, axis=-1, keepdims=True)
    o_ref[...] = xo * lax.rsqrt(ms + NORM_EPS) * fg_ref[...]


def _moe_combine(x1, route_t, yt, mod3, final_g, s, tm):
    n, d = x1.shape
    tiles_per_batch = s // tm
    n_steps = n // tm
    per_o = route_t.shape[1] // tm
    return pl.pallas_call(
        _moe_combine_kernel,
        out_shape=jax.ShapeDtypeStruct((n, d), F32),
        grid=(n_steps,),
        in_specs=[pl.BlockSpec((tm, d), lambda i: (i, 0)),
                  pl.BlockSpec((SUBLANES, tm), lambda i: (i // per_o, i % per_o)),
                  pl.BlockSpec((1, 1, d), lambda i: ((i // tiles_per_batch) * 6 + 5, 0, 0)),
                  pl.BlockSpec((1, d), lambda i: (0, 0)),
                  pl.BlockSpec((tm * SUB, LANES), lambda i: (i, 0)),
                  pl.BlockSpec((tm * SUB, LANES), lambda i: (n_steps + i, 0))],
        out_specs=pl.BlockSpec((tm, d), lambda i: (i, 0)),
        compiler_params=_cparams(("arbitrary",)),
        name="moe_combine",
    )(x1, route_t, mod3, final_g.reshape(1, d), yt, yt)


def _pick(n, candidates):
    for t in candidates:
        if n % t == 0:
            return t
    raise ValueError(f"no tile in {candidates} divides {n}")


def kernel(x, c, ada_w, ada_b, norm1_g, w_in, hg_lb, hg_norm_g, rw_mu, rw_w0, rw_w2, rw_a0, rw_a2, rw_g2, rw_kk, rw_ka, rw_rk, rw_gn_w, rw_gn_b, w_proj_a, w_proj_b, w_out, norm2_g, router_g_w, router_g_b, router_e_w, router_e_b, exp_w_gate, exp_w_up, exp_w_down, final_g):
    bsz, s, d = x.shape
    depth = ada_w.shape[0]
    hg_f = hg_lb.shape[-1]
    hg_w = hg_norm_g.shape[-1]
    rw_w = rw_w0.shape[-1]
    rw_cols = rw_mu.shape[-1]
    n_groups = router_g_w.shape[-1]
    n_experts = router_e_w.shape[-1]
    assert hg_f == hg_w and s % CHUNK == 0 and n_experts + n_groups <= LANES and d == 2 * SUB * LANES

    lb_all = jnp.cumsum(jax.nn.softmax(hg_lb.astype(F32), axis=0), axis=0)
    n = bsz * s
    blk = 512
    n_blocks = (n * TOP_K + n_experts * blk) // blk
    for l in range(depth):
        mod = _ada_mod(c, ada_w[l], ada_b[l])
        mod3 = mod.reshape(bsz * 6, 1, d)

        hg_cols = 2 * hg_f + 2 * hg_w
        rw_pad = -(-rw_cols // 256) * 256
        w_hg, w_rw, w_gt = _w_in_split(jnp.swapaxes(w_in[l], 0, 1), hg_cols, rw_cols, rw_pad)
        lora = (rw_w2.shape[1], rw_a2.shape[1], rw_g2.shape[1])
        assert rw_pad - 3 * rw_w == 256 and rw_cols == 3 * rw_w + sum(lora)
        hg, rw, gates = _in_proj(x, mod3, norm1_g[l], lb_all[l], rw_mu[l], lora, w_hg, w_rw, w_gt,
                                 _pick(s, (512, 256, 128, 64)))

        o_a = _hgrn2(hg, hg_norm_g[l], _pick(s, (1024, 512, 256, 128, 64)))
        o_b = _rwkv7(rw, rw_w0[l], rw_w2[l], rw_a0[l], rw_a2[l], rw_g2[l],
                     rw_kk[l], rw_ka[l], rw_rk[l].reshape(-1), rw_gn_w[l], rw_gn_b[l],
                     _pick(s, (2 * CHUNK, CHUNK)))

        wr = jnp.zeros((d, LANES), F32).at[:, :n_experts].set(router_e_w[l])
        wr = wr.at[:, n_experts:n_experts + n_groups].set(router_g_w[l])
        br = jnp.zeros((1, LANES), F32).at[0, :n_experts].set(router_e_b[l])
        br = br.at[0, n_experts:n_experts + n_groups].set(router_g_b[l])
        tm_o = _pick(s, (1024, 512, 256, 128, 64))
        x1, h2, route_t, counts = _out_proj(
            x, o_a, o_b, gates, mod3, norm2_g[l],
            w_proj_a[l].astype(BF16), w_proj_b[l].astype(BF16), w_out[l].astype(BF16),
            wr, br, n_groups, n_experts, tm_o)

        rt = route_t.reshape(n // tm_o, SUBLANES, tm_o)
        eid = rt[:, 2:4, :].astype(jnp.int32)
        rank = rt[:, 4:6, :].astype(jnp.int32)
        cnt = counts[:n_experts, 0].astype(jnp.int32)
        padded = (cnt + blk - 1) // blk * blk
        pad_end = jnp.cumsum(padded)
        pad_start = pad_end - padded
        e_ax = jnp.arange(n_experts, dtype=jnp.int32)[:, None, None, None]
        dest = rank + jnp.sum(jnp.where(eid[None] == e_ax, pad_start[:, None, None, None], 0), axis=0)
        blk_start = jnp.arange(n_blocks, dtype=jnp.int32) * blk
        blk_e = jnp.minimum(jnp.sum((pad_end[None, :] <= blk_start[:, None]).astype(jnp.int32), axis=1),
                            n_experts - 1)
        n_used = (pad_end[-1:] // blk).astype(jnp.int32)
        tm = tm_o
        smap = _moe_slotmap(dest.reshape(-1) + blk, pad_start + cnt, padded - cnt, n_used, n_blocks * blk, blk, tm_o, n)
        yt = _moe_experts_gather(h2, smap, blk_e, n_used, exp_w_gate[l], exp_w_up[l], exp_w_down[l], blk, n)
        last = l == depth - 1
        assert last, "the final RMSNorm is fused into the last layer's combine"
        out = _moe_combine(x1.reshape(n, d), route_t, yt, mod3, final_g, s, tm)
        x = out.reshape(bsz, s, d)
    return x
```

```python
import functools

import numpy as np
import jax
import jax.numpy as jnp
from jax import lax
from jax.experimental import pallas as pl
from jax.experimental.pallas import tpu as pltpu

F32 = jnp.float32
BF16 = jnp.bfloat16

NORM_EPS = 1e-6
HG_HEAD = 128
RW_HEAD = 64
RW_GN_EPS = 64e-5
TOP_K = 2
CHUNK = 64
LANES = 128
SUB = 4
U32 = jnp.uint32
MXU_K = 256
SUBLANES = 8
SMAP_BITS = 16
VMEM_LIMIT = 56 * 1024 * 1024

NT = (((1,), (1,)), ((), ()))
TN = (((0,), (0,)), ((), ()))


def _dot(a, b, dims=None, precision=None):
    if dims is None:
        return jnp.dot(a, b, preferred_element_type=F32, precision=precision)
    return lax.dot_general(a, b, dims, preferred_element_type=F32, precision=precision)


def _split3(x):
    hi = x.astype(BF16)
    r1 = x - hi.astype(F32)
    mid = r1.astype(BF16)
    lo = (r1 - mid.astype(F32)).astype(BF16)
    return hi, mid, lo


def _dot_exact_lhs(m3_bf16, x):
    return _dot(m3_bf16, jnp.concatenate(_split3(x), axis=0))


def _head_sums(x, m2_bf16):
    outs = []
    for g0 in range(0, x.shape[1], MXU_K):
        xg = x[:, g0:g0 + MXU_K]
        hi = xg.astype(BF16)
        lo = (xg - hi.astype(F32)).astype(BF16)
        outs.append(_dot(jnp.concatenate([hi, lo], axis=1), m2_bf16))
    return jnp.concatenate(outs, axis=1)


def _sigmoid(x):
    return 1.0 / (1.0 + jnp.exp(-x))


def _silu(x):
    return x * _sigmoid(x)


def _rows_to_tiles(ref, val):
    m, half = val.shape[0], val.shape[1] // 2
    hi = lax.bitcast_convert_type(val[:, :half].astype(BF16).astype(F32), U32)
    lo = lax.bitcast_convert_type(val[:, half:].astype(BF16).astype(F32), U32)
    w = (hi & jnp.uint32(0xFFFF0000)) | (lo >> 16)
    for j in range(SUB):
        ref[pl.ds(j, m, stride=SUB), :] = w[:, j * LANES:(j + 1) * LANES]


def _tiles_to_rows(ref, m, base=0):
    w = jnp.concatenate([ref[pl.ds(base * SUB + j, m, stride=SUB), :] for j in range(SUB)], axis=1)
    hi = lax.bitcast_convert_type(w & jnp.uint32(0xFFFF0000), F32)
    lo = lax.bitcast_convert_type(w << 16, F32)
    return jnp.concatenate([hi, lo], axis=1)


def _cparams(sem):
    return pltpu.CompilerParams(dimension_semantics=sem, vmem_limit_bytes=VMEM_LIMIT)


def _ada_kernel(cb_ref, w_ref, b_ref, o_ref, sc_ref, buf_ref, sem_ref):
    bsz, d = cb_ref.shape[0], cb_ref.shape[1]
    nbuf, tn = buf_ref.shape[0], buf_ref.shape[2]
    reps = tn // LANES
    n_tiles = o_ref.shape[1] // tn

    def copy(j):
        return pltpu.make_async_copy(w_ref.at[:, pl.ds(j * tn, tn)], buf_ref.at[j % nbuf], sem_ref.at[j % nbuf])

    for j in range(min(nbuf - 1, n_tiles)):
        copy(j).start()
    sc_ref[...] = _silu(cb_ref[...])
    o_ref[...] = jnp.zeros_like(o_ref)
    for j in range(n_tiles):
        if j + nbuf - 1 < n_tiles:
            copy(j + nbuf - 1).start()
        copy(j).wait()
        wv_ref = buf_ref.at[j % nbuf]

        def body(i, accs, wv_ref=wv_ref):
            k0 = pl.multiple_of(i * SUBLANES, SUBLANES)
            w = wv_ref[pl.ds(k0, SUBLANES), :]
            out = []
            for b in range(bsz):
                sc = sc_ref[b, pl.ds(k0, SUBLANES), :]
                out.append(accs[b] + w * jnp.concatenate([sc] * reps, axis=1))
            return tuple(out)

        accs = lax.fori_loop(0, d // SUBLANES, body, tuple(jnp.zeros((SUBLANES, tn), F32) for _ in range(bsz)),
                             unroll=4)
        cols = slice(j * tn, (j + 1) * tn)
        for b in range(bsz):
            o_ref[b:b + 1, cols] = jnp.sum(accs[b], axis=0, keepdims=True) + b_ref[:, cols]


def _ada_mod(c, w, b):
    bsz, d = c.shape
    n = w.shape[1]
    rows = -(-bsz // SUBLANES) * SUBLANES
    cb = jnp.broadcast_to(c[:, :, None], (bsz, d, LANES))
    tn = _pick(n, (512, 256, 128))
    out = pl.pallas_call(
        _ada_kernel,
        out_shape=jax.ShapeDtypeStruct((rows, n), F32),
        in_specs=[pl.BlockSpec(memory_space=pltpu.VMEM),
                  pl.BlockSpec(memory_space=pl.ANY),
                  pl.BlockSpec(memory_space=pltpu.VMEM)],
        out_specs=pl.BlockSpec(memory_space=pltpu.VMEM),
        scratch_shapes=[pltpu.VMEM((bsz, d, LANES), F32), pltpu.VMEM((4, d, tn), F32),
                        pltpu.SemaphoreType.DMA((4,))],
        compiler_params=pltpu.CompilerParams(vmem_limit_bytes=VMEM_LIMIT),
        name="ada_mod",
    )(cb, w, b.reshape(1, n))
    return out[:bsz]


def _in_proj_kernel(lora, x_ref, sh_ref, sc_ref, g_ref, lb_ref, mu_ref, whg_ref, wrw_ref, wgt_ref,
                    hg_ref, rw_ref, gt_ref, carry_ref):
    @pl.when(pl.program_id(1) == 0)
    def _():
        carry_ref[...] = jnp.zeros_like(carry_ref)

    x = x_ref[0]
    ms = jnp.mean(x * x, axis=-1, keepdims=True)
    h = (x * lax.rsqrt(ms + NORM_EPS) * g_ref[...]) * (1.0 + sc_ref[0]) + sh_ref[0]
    hb = h.astype(BF16)
    lb = lb_ref[...]
    w = lb.shape[1]
    hg_maps = (_silu, lambda t: lb + (1.0 - lb) * _sigmoid(t), lambda t: t, _silu)
    for part, fn in enumerate(hg_maps):
        hg_ref[0, :, part * w:(part + 1) * w] = fn(_dot(hb, whg_ref[:, part * w:(part + 1) * w]))
    step = 512
    n_rw = wrw_ref.shape[1]
    row = lax.broadcasted_iota(jnp.int32, (x.shape[0], 256), 0)
    lane = lax.broadcasted_iota(jnp.int32, (x.shape[0], 256), 1)
    dl, al, gl = lora
    for n0 in range(0, n_rw, 256):
        cs = slice(n0, n0 + 256)
        p = _dot(hb, wrw_ref[:, cs])
        prev = jnp.where(row == 0, carry_ref[:, cs], pltpu.roll(p, 1, 0))
        carry_ref[:, cs] = p[x.shape[0] - 1:, :]
        xs = p + mu_ref[:, cs] * (prev - p)
        if n0 == n_rw - 256:
            xs = jnp.where(lane < dl, jnp.tanh(xs),
                           jnp.where(lane < dl + al, xs, jnp.where(lane < dl + al + gl, _sigmoid(xs), 0.0)))
        rw_ref[0, :, cs] = xs
    for n0 in range(0, wgt_ref.shape[1], step):
        gt_ref[0, :, n0:n0 + step] = _sigmoid(_dot(hb, wgt_ref[:, n0:n0 + step])).astype(BF16)


def _w_in_split_kernel(chunks, wt_ref, hg_ref, rw_ref, gt_ref, hgv_ref, rwv_ref, gtv_ref, buf_ref, sem_ref, osem_ref):
    outs = (hg_ref, rw_ref, gt_ref)
    stage = (hgv_ref, rwv_ref, gtv_ref)
    nbuf, ch = buf_ref.shape[0], buf_ref.shape[1]

    def copy(k):
        return pltpu.make_async_copy(wt_ref.at[pl.ds(chunks[k][2], ch), :], buf_ref.at[k % nbuf],
                                     sem_ref.at[k % nbuf])

    def flush(oi):
        return pltpu.make_async_copy(stage[oi], outs[oi], osem_ref.at[oi])

    for k in range(min(nbuf - 1, len(chunks))):
        copy(k).start()
    for k, (oi, c0, _, valid) in enumerate(chunks):
        if k + nbuf - 1 < len(chunks):
            copy(k + nbuf - 1).start()
        copy(k).wait()
        t = buf_ref[k % nbuf].T
        if valid < ch:
            t = jnp.where(lax.broadcasted_iota(jnp.int32, t.shape, 1) < valid, t, 0.0)
        stage[oi][:, c0:c0 + ch] = t.astype(BF16)
        if k + 1 == len(chunks) or chunks[k + 1][0] != oi:
            flush(oi).start()
    for oi in range(len(outs)):
        flush(oi).wait()


def _w_in_split(w_in_t, n_hg, n_rw, n_rw_pad):
    n_in, d = w_in_t.shape
    n_gt = n_in - n_hg - n_rw
    ch = 256
    assert n_hg % ch == 0 and n_rw_pad % ch == 0 and n_gt % ch == 0 and n_gt >= ch
    chunks = []
    for oi, (r0, nr, npad) in enumerate(((0, n_hg, n_hg), (n_hg, n_rw, n_rw_pad), (n_hg + n_rw, n_gt, n_gt))):
        chunks += [(oi, c0, r0 + c0, min(ch, nr - c0)) for c0 in range(0, npad, ch)]
    assert all(v > 0 for _, _, _, v in chunks)
    return pl.pallas_call(
        functools.partial(_w_in_split_kernel, tuple(chunks)),
        out_shape=(jax.ShapeDtypeStruct((d, n_hg), BF16),
                   jax.ShapeDtypeStruct((d, n_rw_pad), BF16),
                   jax.ShapeDtypeStruct((d, n_gt), BF16)),
        in_specs=[pl.BlockSpec(memory_space=pl.ANY)],
        out_specs=tuple(pl.BlockSpec(memory_space=pl.ANY) for _ in range(3)),
        scratch_shapes=[pltpu.VMEM((d, n_hg), BF16), pltpu.VMEM((d, n_rw_pad), BF16), pltpu.VMEM((d, n_gt), BF16),
                        pltpu.VMEM((4, ch, d), F32), pltpu.SemaphoreType.DMA((4,)), pltpu.SemaphoreType.DMA((3,))],
        compiler_params=pltpu.CompilerParams(vmem_limit_bytes=VMEM_LIMIT),
        name="w_in_split",
    )(w_in_t)


def _in_proj(x, mod3, norm_g, lb, mu, lora, w_hg, w_rw, w_gt, tm):
    bsz, s, d = x.shape
    assert w_hg.shape[1] == 4 * lb.shape[0] and sum(lora) <= 256
    mup = jnp.zeros((1, w_rw.shape[1]), F32).at[0, :mu.shape[-1]].set(mu)
    n_hg, n_rw, n_gt = w_hg.shape[1], w_rw.shape[1], w_gt.shape[1]
    const = lambda b, i: (0, 0)
    return pl.pallas_call(
        functools.partial(_in_proj_kernel, lora),
        out_shape=(jax.ShapeDtypeStruct((bsz, s, n_hg), F32),
                   jax.ShapeDtypeStruct((bsz, s, n_rw), F32),
                   jax.ShapeDtypeStruct((bsz, s, n_gt), BF16)),
        grid=(bsz, s // tm),
        in_specs=[pl.BlockSpec((1, tm, d), lambda b, i: (b, i, 0)),
                  pl.BlockSpec((1, 1, d), lambda b, i: (b * 6 + 0, 0, 0)),
                  pl.BlockSpec((1, 1, d), lambda b, i: (b * 6 + 1, 0, 0)),
                  pl.BlockSpec((1, d), const),
                  pl.BlockSpec((1, lb.shape[0]), const),
                  pl.BlockSpec((1, n_rw), const),
                  pl.BlockSpec((d, n_hg), const),
                  pl.BlockSpec((d, n_rw), const),
                  pl.BlockSpec((d, n_gt), const)],
        out_specs=(pl.BlockSpec((1, tm, n_hg), lambda b, i: (b, i, 0)),
                   pl.BlockSpec((1, tm, n_rw), lambda b, i: (b, i, 0)),
                   pl.BlockSpec((1, tm, n_gt), lambda b, i: (b, i, 0))),
        scratch_shapes=[pltpu.VMEM((1, n_rw), F32)],
        compiler_params=_cparams(("arbitrary", "arbitrary")),
        name="in_proj",
    )(x, mod3, mod3, norm_g.reshape(1, d), lb.reshape(1, -1), mup, w_hg, w_rw, w_gt)


_HG_LEVELS = (32, 16, 8, 4, 2, 1)


def _hgrn2_consts(width):
    c = CHUNK
    t = np.arange(c)[:, None]
    s = np.arange(c)[None, :]
    blocks = [(s <= t), (s > t)]
    lvl_masks = []
    right = []
    for h in _HG_LEVELS:
        m = (t // (2 * h)) * 2 * h + h
        is_r = (t & h) != 0
        blk = np.where(is_r, (s >= m) & (s <= t), (s > t) & (s <= m - 1))
        blocks.append(blk)
        lvl_masks.append(is_r & ((s & h) == 0) & ((t // (2 * h)) == (s // (2 * h))))
        right.append(np.broadcast_to(is_r, (c, width)))
    mst = np.tile(np.concatenate(blocks, axis=0).astype(np.float32), (1, 3))
    lm = np.stack([np.eye(c, dtype=bool)] + lvl_masks).astype(np.float32)
    rm = np.stack(right).astype(np.float32)
    return jnp.asarray(mst, BF16), jnp.asarray(lm, F32), jnp.asarray(rm, F32)


def _hgrn2_kernel(q_ref, f_ref, i_ref, g_ref, ng_ref, mst_ref, lm_ref, rm_ref, o_ref, st_ref):
    c = CHUNK
    n_chunks = q_ref.shape[1] // c

    @pl.when(pl.program_id(1) == 0)
    def _():
        st_ref[...] = jnp.zeros_like(st_ref)

    nsub = next(n for n in (8, 4, 2, 1) if n_chunks % n == 0)

    def chunk_body(ci, carry):
        r0 = pl.multiple_of(ci * (nsub * c), nsub * c)
        for _ in _hgrn2_steps(q_ref, f_ref, i_ref, g_ref, ng_ref, mst_ref, lm_ref, rm_ref, o_ref, st_ref,
                              0, r0, nsub):
            pass
        return carry

    lax.fori_loop(0, n_chunks // nsub, chunk_body, 0)


def _hgrn2_steps(q_ref, f_ref, i_ref, g_ref, ng_ref, mst_ref, lm_ref, rm_ref, o_ref, st_ref, b, r0, nsub):
    c = CHUNK
    n_heads = q_ref.shape[2] // HG_HEAD
    mst = mst_ref[...]
    ng = ng_ref[...]
    heads = [slice(hd * HG_HEAD, (hd + 1) * HG_HEAD) for hd in range(n_heads)]
    subs = []
    for j in range(nsub):
        rows = pl.ds(r0 + j * c, c)
        q = q_ref[b, rows, :]
        f = f_ref[b, rows, :]
        k = 1.0 - f
        ex = jnp.exp2(_dot_exact_lhs(mst, jnp.log2(f)))
        subs.append(dict(rows=rows, q=q, k=k, ex=ex, vb=i_ref[b, rows, :].astype(BF16),
                         qd=(q * ex[0:c]).astype(BF16), kr=(k * ex[c:2 * c]).astype(BF16)))
        yield
    for sb in subs:
        qb, kb = sb['q'].astype(BF16), sb['k'].astype(BF16)
        sb['sc'] = [lm_ref[0] * _dot(qb[:, ls], kb[:, ls], NT) for ls in heads]
        sb['dqk'] = sb['q'] - sb['k']
    yield
    for li, h in enumerate(_HG_LEVELS):
        for sb in subs:
            if h % SUBLANES == 0:
                qk = jnp.concatenate([(sb['q'] if m % 2 else sb['k'])[m * h:(m + 1) * h] for m in range(c // h)],
                                     axis=0)
            else:
                qk = sb['k'] + rm_ref[li] * sb['dqk']
            g_l = (qk * sb['ex'][(2 + li) * c:(3 + li) * c]).astype(BF16)
            sb['sc'] = [s_h + lm_ref[li + 1] * _dot(g_l[:, ls], g_l[:, ls], NT)
                        for s_h, ls in zip(sb['sc'], heads)]
        yield
    for sb in subs:
        sb['kv'] = [_dot(sb['vb'][:, ls], sb['kr'][:, ls], TN) for ls in heads]
        sb['o'] = [_dot(s_h.astype(BF16), sb['vb'][:, ls]) for s_h, ls in zip(sb['sc'], heads)]
        yield
    sts = [st_ref[b, hd] for hd in range(n_heads)]
    for sb in subs:
        sb['o'] = [o_h + _dot(sb['qd'][:, ls], st.astype(BF16), NT) for o_h, ls, st in zip(sb['o'], heads, sts)]
        sts = [st * sb['ex'][c - 1:c, ls] + kv for st, ls, kv in zip(sts, heads, sb['kv'])]
        yield
    for hd in range(n_heads):
        st_ref[b, hd] = sts[hd]
    for sb in subs:
        on = [o_h * lax.rsqrt(jnp.mean(o_h * o_h, axis=-1, keepdims=True) + NORM_EPS) for o_h in sb['o']]
        o_full = jnp.concatenate(on, axis=1) * ng
        o_ref[b, sb['rows'], :] = (o_full * g_ref[b, sb['rows'], :]).astype(o_ref.dtype)
        yield


def _hgrn2(hg, norm_g, ts):
    bsz, s, n4 = hg.shape
    w = n4 // 4
    mst, lm, rm = _hgrn2_consts(w)
    n_heads = w // HG_HEAD
    const2 = lambda b, i: (0, 0)
    const3 = lambda b, i: (0, 0, 0)
    return pl.pallas_call(
        _hgrn2_kernel,
        out_shape=jax.ShapeDtypeStruct((bsz, s, w), BF16),
        grid=(bsz, s // ts),
        in_specs=[pl.BlockSpec((1, ts, w), lambda b, i: (b, i, 0)),
                  pl.BlockSpec((1, ts, w), lambda b, i: (b, i, 1)),
                  pl.BlockSpec((1, ts, w), lambda b, i: (b, i, 2)),
                  pl.BlockSpec((1, ts, w), lambda b, i: (b, i, 3)),
                  pl.BlockSpec((1, w), const2),
                  pl.BlockSpec(mst.shape, const2),
                  pl.BlockSpec(lm.shape, const3),
                  pl.BlockSpec(rm.shape, const3)],
        out_specs=pl.BlockSpec((1, ts, w), lambda b, i: (b, i, 0)),
        scratch_shapes=[pltpu.VMEM((1, n_heads, HG_HEAD, HG_HEAD), F32)],
        compiler_params=_cparams(("arbitrary", "arbitrary")),
        name="hgrn2",
    )(hg, hg, hg, hg, norm_g.reshape(1, w), mst, lm, rm)


def _rwkv_consts(width):
    c = CHUNK
    t = np.arange(c)[:, None]
    s = np.arange(c)[None, :]
    tri = np.tile((s <= t).astype(np.float32), (1, 3))
    tt = np.arange(2 * c)[:, None]
    ss = np.arange(2 * c)[None, :]
    same = (tt // c) == (ss // c)
    strict = same & ((ss % c) < (tt % c))
    incl = same & ((ss % c) <= (tt % c))
    hsum = (np.arange(MXU_K)[:, None] // RW_HEAD) == (np.arange(MXU_K)[None, :] // RW_HEAD)
    hsum = np.tile(hsum, (2, 1))
    return (jnp.asarray(tri, BF16), jnp.asarray(strict.astype(np.float32), F32),
            jnp.asarray(incl.astype(np.float32), F32), jnp.asarray(hsum.astype(np.float32), BF16))


def _rwkv7_kernel(p_ref, w0_ref, a0_ref, kk_ref, ka_ref, rk_ref, gnw_ref, gnb_ref,
                  w2_ref, a2_ref, g2_ref, tri_ref, sm_ref, im_ref, hs_ref,
                  o_ref, zt_ref):
    @pl.when(pl.program_id(0) == 0)
    def _():
        zt_ref[...] = jnp.zeros_like(zt_ref)

    for _ in _rwkv7_steps(p_ref, w0_ref, a0_ref, kk_ref, ka_ref, rk_ref, gnw_ref, gnb_ref,
                          w2_ref, a2_ref, g2_ref, tri_ref, sm_ref, im_ref, hs_ref, o_ref, zt_ref):
        pass


def _rwkv7_steps(p_ref, w0_ref, a0_ref, kk_ref, ka_ref, rk_ref, gnw_ref, gnb_ref,
                 w2_ref, a2_ref, g2_ref, tri_ref, sm_ref, im_ref, hs_ref, o_ref, zt_ref):
    c = CHUNK
    nb = p_ref.shape[0]
    nch = p_ref.shape[1] // c
    width = o_ref.shape[2]
    n_pairs = width // LANES

    hs = hs_ref[...]
    tri = tri_ref[...]
    smask = sm_ref[...] > 0
    imask = im_ref[...] > 0
    lane = lax.broadcasted_iota(jnp.int32, (c, LANES), 1)
    m0 = (lane < RW_HEAD).astype(F32)
    m1 = 1.0 - m0

    def stack(x):
        return jnp.concatenate([x * m0, x * m1], axis=0)

    xs = jnp.concatenate([p_ref[b] for b in range(nb)], axis=0)
    r_all = xs[:, 0:width]
    k_all = xs[:, width:2 * width]
    v_all = xs[:, 2 * width:3 * width]
    slab = xs[:, 3 * width:].astype(BF16)
    nz = -(w0_ref[...] + _dot(slab, w2_ref[...]))
    softplus = jnp.maximum(nz, 0.0) + jnp.log(1.0 + jnp.exp(-jnp.abs(nz)))
    ld_all = -jnp.exp(-softplus - 0.5)
    a_all = _sigmoid(a0_ref[...] + _dot(slab, a2_ref[...]))
    g_all = _dot(slab, g2_ref[...])
    kk0 = k_all * kk_ref[...]
    kk_all = kk0 * lax.rsqrt(jnp.maximum(_head_sums(kk0 * kk0, hs), 1e-24))
    k2_all = k_all * (1.0 + (a_all - 1.0) * ka_ref[...])
    yield

    units = []
    for b, j in [(b, j) for b in range(nb) for j in range(nch)]:
        rb = slice((b * nch + j) * c, (b * nch + j + 1) * c)
        r, k2, v, ld = r_all[rb], k2_all[rb], v_all[rb], ld_all[rb]
        a_in = -kk_all[rb]
        b_in = kk_all[rb] * a_all[rb]
        cum = _dot_exact_lhs(tri, ld)
        cum_t = cum[c - 1:c, :]
        e_c = jnp.exp(cum)
        e_nc = jnp.exp(-cum)
        e_rem = jnp.exp(cum_t - cum)
        at_f = a_in * jnp.exp(cum - ld)
        rt_f = r * e_c
        kt_f = k2 * e_nc
        bt_f = b_in * e_nc
        kh_f = k2 * e_rem
        bh_f = b_in * e_rem
        p_t = jnp.exp(cum_t)
        for pi in range(n_pairs):
            ls = slice(pi * LANES, (pi + 1) * LANES)
            units.append(dict(
                b=b, j=j, pi=pi,
                at=stack(at_f[:, ls]).astype(BF16), rt=stack(rt_f[:, ls]).astype(BF16),
                kt=stack(kt_f[:, ls]).astype(BF16), bt=stack(bt_f[:, ls]).astype(BF16),
                kh=stack(kh_f[:, ls]).astype(BF16), bh=stack(bh_f[:, ls]).astype(BF16),
                vs=stack(v[:, ls]).astype(BF16), p_t=p_t[:, ls]))
        yield

    for u in units:
        lhs = jnp.concatenate([u['at'], u['rt']], axis=0)
        u['g'] = _dot(lhs, jnp.concatenate([u['kt'], u['bt']], axis=0), NT)
    yield
    for u in units:
        g = u.pop('g')
        u['a_ak'] = jnp.where(smask, g[:2 * c, :2 * c], 0.0).astype(BF16)
        u['pw'] = jnp.where(smask, g[:2 * c, 2 * c:], 0.0).astype(BF16)
        u['a_r'] = jnp.where(jnp.concatenate([imask, imask], axis=1), g[2 * c:], 0.0).astype(BF16)
    for u in units:
        akv = _dot(u.pop('a_ak'), u['vs'])
        u['x'] = jnp.concatenate([u['at'].astype(F32), akv], axis=1)
    yield
    n_lvl = int(np.log2(c))
    for lvl in range(n_lvl):
        for u in units:
            u['x'] = u['x'] + _dot(u['pw'], u['x'].astype(BF16))
        yield
        if lvl + 1 < n_lvl:
            for u in units:
                u['pw'] = _dot(u['pw'], u['pw']).astype(BF16)
            yield
    for u in units:
        x = u.pop('x')
        u['wr'] = jnp.concatenate([x[:, :LANES].astype(BF16), u['rt']], axis=0)
        u['u_loc'] = x[:, LANES:]
    zt = {(b, pi): zt_ref[b, pi] for b in range(nb) for pi in range(n_pairs)}
    for j in range(nch):
        tail = [u for u in units if u['j'] == j]
        for u in tail:
            u['uy'] = _dot(u.pop('wr'), zt[u['b'], u['pi']].astype(BF16), NT)
        yield
        for u in tail:
            uy = u.pop('uy')
            u['u'] = (uy[:2 * c] + u.pop('u_loc')).astype(BF16)
            u['y0'] = uy[2 * c:]
        for u in tail:
            vu = jnp.concatenate([u['vs'], u['u']], axis=0)
            u['y'] = u.pop('y0') + _dot(u['a_r'], vu)
            upd = _dot(vu, jnp.concatenate([u['kh'], u['bh']], axis=0), TN)
            zt[u['b'], u['pi']] = zt[u['b'], u['pi']] * u['p_t'] + upd
        yield
    for (b, pi), z in zt.items():
        zt_ref[b, pi] = z

    inv_n = 1.0 / RW_HEAD
    y = jnp.concatenate(
        [jnp.concatenate([u['y'][:c] + u['y'][c:] for u in units if (u['b'], u['j']) == (b, j)], axis=1)
         for b in range(nb) for j in range(nch)], axis=0)
    mean = _head_sums(y, hs) * inv_n
    yield
    d = y - mean
    var = _head_sums(d * d, hs) * inv_n
    yield
    yn = d * lax.rsqrt(var + RW_GN_EPS) * gnw_ref[...] + gnb_ref[...]
    bonus = _head_sums(r_all * k2_all * rk_ref[...], hs) * v_all
    out = ((yn + bonus) * g_all).astype(o_ref.dtype)
    for b in range(nb):
        o_ref[b] = out[b * nch * c:(b + 1) * nch * c]


def _rwkv7(rw, w0, w2, a0, a2, g2, k_k, k_a, r_k, gn_w, gn_b, ts):
    bsz, s, cols = rw.shape
    width = w0.shape[-1]
    n_pairs = width // LANES
    slab = cols - 3 * width
    dl, al, gl = w2.shape[0], a2.shape[0], g2.shape[0]
    w2f = jnp.zeros((slab, width), F32).at[0:dl].set(w2).astype(BF16)
    a2f = jnp.zeros((slab, width), F32).at[dl:dl + al].set(a2).astype(BF16)
    g2f = jnp.zeros((slab, width), F32).at[dl + al:dl + al + gl].set(g2).astype(BF16)
    tri, sm, im, hs = _rwkv_consts(width)
    row = lambda x: x.reshape(1, width)
    const = lambda i: (0, 0)
    vec = pl.BlockSpec((1, width), const)
    return pl.pallas_call(
        _rwkv7_kernel,
        out_shape=jax.ShapeDtypeStruct((bsz, s, width), BF16),
        grid=(s // ts,),
        in_specs=[pl.BlockSpec((bsz, ts, cols), lambda i: (0, i, 0)),
                  vec, vec, vec, vec, vec, vec, vec,
                  pl.BlockSpec((slab, width), const),
                  pl.BlockSpec((slab, width), const),
                  pl.BlockSpec((slab, width), const),
                  pl.BlockSpec(tri.shape, const),
                  pl.BlockSpec(sm.shape, const),
                  pl.BlockSpec(im.shape, const),
                  pl.BlockSpec(hs.shape, const)],
        out_specs=pl.BlockSpec((bsz, ts, width), lambda i: (0, i, 0)),
        scratch_shapes=[pltpu.VMEM((bsz, n_pairs, LANES, LANES), F32)],
        compiler_params=_cparams(("arbitrary",)),
        name="rwkv7",
    )(rw, row(w0), row(a0), row(k_k), row(k_a), row(r_k), row(gn_w), row(gn_b),
      w2f, a2f, g2f, tri, sm, im, hs)


def _out_proj_kernel(n_groups, n_experts,
                     x_ref, oa_ref, ob_ref, ga_ref, gb_ref, gt1_ref, sc2_ref, sh2_ref, g2_ref,
                     wa_ref, wb_ref, wo_ref, wr_ref, wrl_ref, br_ref, upper_ref,
                     x1_ref, h2_ref, routet_ref, cnt_ref, carry_ref):
    first = (pl.program_id(0) == 0) & (pl.program_id(1) == 0)

    @pl.when(first)
    def _():
        carry_ref[...] = jnp.zeros_like(carry_ref)

    pa = _dot(oa_ref[0], wa_ref[...])
    pb = _dot(ob_ref[0], wb_ref[...])
    mixed = ga_ref[0].astype(F32) * pa + gb_ref[0].astype(F32) * pb
    x1 = x_ref[0] + gt1_ref[0] * _dot(mixed.astype(BF16), wo_ref[...])
    x1_ref[0] = x1
    ms = jnp.mean(x1 * x1, axis=-1, keepdims=True)
    h2 = (x1 * lax.rsqrt(ms + NORM_EPS) * g2_ref[...]) * (1.0 + sc2_ref[0]) + sh2_ref[0]
    _rows_to_tiles(h2_ref, h2)

    h2_hi = h2.astype(BF16)
    h2_lo = (h2 - h2_hi.astype(F32)).astype(BF16)
    logits = (_dot(wr_ref[...], h2_hi, NT) + _dot(wr_ref[...], h2_lo, NT) + _dot(wrl_ref[...], h2_hi, NT)
              + br_ref[...])
    row = lax.broadcasted_iota(jnp.int32, logits.shape, 0)
    neg = jnp.float32(-jnp.inf)
    big = jnp.int32(1 << 20)
    eg = n_experts // n_groups
    is_g = (row >= n_experts) & (row < n_experts + n_groups)
    lg = jnp.where(is_g, logits, neg)
    mg = jnp.max(lg, axis=0, keepdims=True)
    p_grp = 1.0 / jnp.sum(jnp.where(is_g, jnp.exp(lg - mg), 0.0), axis=0, keepdims=True)
    gidx = jnp.min(jnp.where(lg == mg, row, big), axis=0, keepdims=True) - n_experts
    sel = (row >= gidx * eg) & (row < gidx * eg + eg)
    le = jnp.where(sel, logits, neg)
    me = jnp.max(le, axis=0, keepdims=True)
    pe_un = jnp.where(sel, jnp.exp(le - me), 0.0)
    pe = jnp.where(sel, pe_un / jnp.sum(pe_un, axis=0, keepdims=True), -1.0)
    v1 = jnp.max(pe, axis=0, keepdims=True)
    i1 = jnp.min(jnp.where(pe == v1, row, big), axis=0, keepdims=True)
    pe2 = jnp.where(row == i1, -1.0, pe)
    v2 = jnp.max(pe2, axis=0, keepdims=True)
    i2 = jnp.min(jnp.where(pe2 == v2, row, big), axis=0, keepdims=True)
    wsum = v1 + v2
    w1 = p_grp * v1 / wsum
    w2 = p_grp * v2 / wsum

    oh1 = (row == i1).astype(F32)
    oh2 = (row == i2).astype(F32)
    both = oh1 + oh2
    before = _dot(both.astype(BF16), upper_ref[...]) + carry_ref[...]
    rank1 = jnp.sum(oh1 * before, axis=0, keepdims=True)
    rank2 = jnp.sum(oh2 * before, axis=0, keepdims=True)
    carry_ref[...] = carry_ref[...] + jnp.sum(both, axis=1, keepdims=True)
    cnt_ref[...] = carry_ref[...]
    zero = jnp.zeros_like(w1)
    routet_ref[...] = jnp.concatenate(
        [w1, w2, i1.astype(F32), i2.astype(F32), rank1, rank2, zero, zero], axis=0)


def _out_proj(x, o_a, o_b, gates, mod3, norm2_g, wa, wb, wo, wr, br, n_groups, n_experts, tm):
    bsz, s, d = x.shape
    wdt = o_a.shape[-1]
    upper = jnp.asarray(np.triu(np.ones((tm, tm), np.float32), 1), BF16)
    wrt = wr.T
    wr_hi = wrt.astype(BF16)
    wr_lo = (wrt - wr_hi.astype(F32)).astype(BF16)
    const = lambda b, i: (0, 0)
    tile = lambda b, i: (b, i, 0)
    kern = functools.partial(_out_proj_kernel, n_groups, n_experts)
    return pl.pallas_call(
        kern,
        out_shape=(jax.ShapeDtypeStruct((bsz, s, d), F32),
                   jax.ShapeDtypeStruct((bsz * s * SUB, LANES), U32),
                   jax.ShapeDtypeStruct((bsz * (s // tm) * SUBLANES, tm), F32),
                   jax.ShapeDtypeStruct((LANES, 1), F32)),
        grid=(bsz, s // tm),
        in_specs=[pl.BlockSpec((1, tm, d), tile),
                  pl.BlockSpec((1, tm, wdt), tile),
                  pl.BlockSpec((1, tm, wdt), tile),
                  pl.BlockSpec((1, tm, d), lambda b, i: (b, i, 0)),
                  pl.BlockSpec((1, tm, d), lambda b, i: (b, i, 1)),
                  pl.BlockSpec((1, 1, d), lambda b, i: (b * 6 + 2, 0, 0)),
                  pl.BlockSpec((1, 1, d), lambda b, i: (b * 6 + 4, 0, 0)),
                  pl.BlockSpec((1, 1, d), lambda b, i: (b * 6 + 3, 0, 0)),
                  pl.BlockSpec((1, d), const),
                  pl.BlockSpec(wa.shape, const),
                  pl.BlockSpec(wb.shape, const),
                  pl.BlockSpec(wo.shape, const),
                  pl.BlockSpec(wrt.shape, const),
                  pl.BlockSpec(wrt.shape, const),
                  pl.BlockSpec((LANES, 1), const),
                  pl.BlockSpec((tm, tm), const)],
        out_specs=(pl.BlockSpec((1, tm, d), tile),
                   pl.BlockSpec((tm * SUB, LANES), lambda b, i: (b * (s // tm) + i, 0)),
                   pl.BlockSpec((SUBLANES, tm), lambda b, i: (b * (s // tm) + i, 0)),
                   pl.BlockSpec((LANES, 1), const)),
        scratch_shapes=[pltpu.VMEM((LANES, 1), F32)],
        compiler_params=_cparams(("arbitrary", "arbitrary")),
        name="out_proj",
    )(x, o_a, o_b, gates, gates, mod3, mod3, mod3, norm2_g.reshape(1, d), wa, wb, wo, wr_hi, wr_lo,
      br.reshape(LANES, 1), upper)


def _moe_slotmap_kernel(tm_o, n_tok, blk, dest_ref, zstart_ref, zcnt_ref, nused_ref, smap_ref):
    i = pl.program_id(0)
    n_slots = smap_ref.shape[0] - blk
    n_experts = zcnt_ref.shape[0]

    @pl.when(i == 0)
    def _():
        unroll = 8

        def init(g, carry):
            for u in range(unroll):
                smap_ref[blk + g * unroll + u] = TOP_K * n_tok
            return carry
        lax.fori_loop(nused_ref[0] * (blk // unroll), n_slots // unroll, init, 0)
        for r in range(blk):
            smap_ref[r] = TOP_K * n_tok + n_experts * blk + r
        for e in range(n_experts):
            def pad(g, carry, e=e):
                for u in range(unroll):
                    j = jnp.maximum(zcnt_ref[e] - 1 - (g * unroll + u), 0)
                    smap_ref[blk + zstart_ref[e] + j] = TOP_K * n_tok + e * blk + j
                return carry
            lax.fori_loop(0, (zcnt_ref[e] + unroll - 1) // unroll, pad, 0)

    base = i * (TOP_K * tm_o)
    both = 1 + (1 << SMAP_BITS)
    for k in range(TOP_K):
        v0 = k * n_tok + (i * tm_o) * both
        for r in range(tm_o):
            smap_ref[dest_ref[base + k * tm_o + r]] = v0 + r * both


def _moe_slotmap(dest, zstart, zcnt, n_used, n_slots, blk, tm_o, n_tok):
    grid_spec = pltpu.PrefetchScalarGridSpec(
        num_scalar_prefetch=4,
        grid=(n_tok // tm_o,),
        in_specs=[],
        out_specs=pl.BlockSpec(memory_space=pltpu.SMEM),
    )
    n_experts = zcnt.shape[0]
    assert TOP_K * n_tok + (n_experts + 1) * blk <= (1 << SMAP_BITS) and n_tok <= (1 << (31 - SMAP_BITS))
    return pl.pallas_call(
        functools.partial(_moe_slotmap_kernel, tm_o, n_tok, blk),
        out_shape=jax.ShapeDtypeStruct((blk + n_slots,), jnp.int32),
        grid_spec=grid_spec,
        compiler_params=_cparams(("arbitrary",)),
        name="moe_slotmap",
    )(dest, zstart, zcnt, n_used)


def _moe_expert_gather_kernel(n_tok, blk_e_ref, nused_ref, smap_ref, h_ref, wg_ref, wu_ref, wd_ref, y_ref,
                              hv, xbuf, ystage, wgb, wub, wdb, sem, hsem):
    i = pl.program_id(0)
    nb = pl.num_programs(0)
    n_used = nused_ref[0]
    blk = xbuf.shape[1] // SUB
    first_real_blocks = TOP_K * n_tok // blk

    def gather(b, slot, rows):
        for r in rows:
            tok = lax.shift_right_logical(smap_ref[(b + 1) * blk + r], SMAP_BITS)
            src = pl.multiple_of(tok * SUB, SUB)
            xbuf[slot, pl.ds(r * SUB, SUB), :] = hv[pl.ds(src, SUB), :]

    def issue(b, slot, rows):
        for r in rows:
            t = smap_ref[(b + 1) * blk + r] & ((1 << SMAP_BITS) - 1)
            dst = pl.multiple_of(t * SUB, SUB)
            pltpu.make_async_copy(ystage.at[slot, pl.ds(r * SUB, SUB), :], y_ref.at[pl.ds(dst, SUB), :],
                                  sem.at[slot]).start(priority=r % 2)

    def wait_block(slot):
        pltpu.make_async_copy(ystage.at[slot], y_ref.at[pl.ds(0, blk * SUB), :], sem.at[slot]).wait()

    @pl.when(i == 0)
    def _():
        load = pltpu.make_async_copy(h_ref, hv, hsem)
        load.start()
        ystage[...] = jnp.zeros_like(ystage)
        n_spare_blocks = y_ref.shape[0] // (blk * SUB) - first_real_blocks

        def spare_copy(c):
            dst = (first_real_blocks + c) * blk * SUB
            return pltpu.make_async_copy(ystage.at[0], y_ref.at[pl.ds(dst, blk * SUB), :], sem.at[0])

        for c in range(n_spare_blocks):
            spare_copy(c).start()
        for c in range(n_spare_blocks):
            spare_copy(c).wait()
        load.wait()
        gather(0, 0, range(blk))

    new_expert = (i == 0) | (blk_e_ref[i] != blk_e_ref[jnp.maximum(i - 1, 0)])

    @pl.when((i < n_used) & new_expert)
    def _():
        wgb[...] = wg_ref[0].astype(BF16)
        wub[...] = wu_ref[0].astype(BF16)
        wdb[...] = wd_ref[0].astype(BF16)

    @pl.when(i < n_used)
    def _():
        slot = i % 2
        pslot = 1 - slot
        nxt = jnp.minimum(i + 1, nb - 1)

        @pl.when(i > 0)
        def _():
            wait_block(slot)

        q = blk // 4
        xb = _tiles_to_rows(xbuf.at[slot], blk).astype(BF16)
        issue(i - 1, pslot, range(0, q))
        gather(nxt, pslot, range(0, q))
        hg = _dot(xb, wgb[...])
        issue(i - 1, pslot, range(q, 2 * q))
        gather(nxt, pslot, range(q, 2 * q))
        hu = _dot(xb, wub[...])
        issue(i - 1, pslot, range(2 * q, 3 * q))
        gather(nxt, pslot, range(2 * q, 3 * q))
        hid = (_silu(hg) * hu).astype(BF16)
        y = _dot(hid, wdb[...])
        issue(i - 1, pslot, range(3 * q, blk))
        gather(nxt, pslot, range(3 * q, blk))
        _rows_to_tiles(ystage.at[slot], y)

        @pl.when(i == n_used - 1)
        def _():
            issue(i, slot, range(blk))
            wait_block(pslot)
            wait_block(slot)


def _moe_experts_gather(h2, smap, blk_e, n_used, w_gate, w_up, w_down, blk, n_tok):
    d = w_gate.shape[1]
    assert d == 2 * SUB * LANES and h2.shape == (n_tok * SUB, LANES)
    n_slots = smap.shape[0] - blk
    nb = n_slots // blk
    f = w_gate.shape[-1]
    n_experts = w_gate.shape[0]
    assert (TOP_K * n_tok) % blk == 0
    n_tiles = TOP_K * n_tok + (n_experts + 1) * blk
    grid_spec = pltpu.PrefetchScalarGridSpec(
        num_scalar_prefetch=3,
        grid=(nb,),
        in_specs=[pl.BlockSpec(memory_space=pl.ANY),
                  pl.BlockSpec((1, d, f), lambda i, be, nu, sm: (be[i], 0, 0)),
                  pl.BlockSpec((1, d, f), lambda i, be, nu, sm: (be[i], 0, 0)),
                  pl.BlockSpec((1, f, d), lambda i, be, nu, sm: (be[i], 0, 0))],
        out_specs=pl.BlockSpec(memory_space=pl.ANY),
        scratch_shapes=[pltpu.VMEM((n_tok * SUB, LANES), U32),
                        pltpu.VMEM((2, blk * SUB, LANES), U32),
                        pltpu.VMEM((2, blk * SUB, LANES), U32),
                        pltpu.VMEM((d, f), BF16),
                        pltpu.VMEM((d, f), BF16),
                        pltpu.VMEM((f, d), BF16),
                        pltpu.SemaphoreType.DMA((2,)),
                        pltpu.SemaphoreType.DMA],
    )
    return pl.pallas_call(
        functools.partial(_moe_expert_gather_kernel, n_tok),
        out_shape=jax.ShapeDtypeStruct((n_tiles * SUB, LANES), U32),
        grid_spec=grid_spec,
        compiler_params=_cparams(("arbitrary",)),
        name="moe_experts",
    )(blk_e, n_used, smap, h2, w_gate, w_up, w_down)


def _moe_combine_kernel(x1_ref, routet_ref, gt2_ref, fg_ref, y1_ref, y2_ref, o_ref):
    tm = x1_ref.shape[0]
    route = routet_ref[...].T
    moe = route[:, 0:1] * _tiles_to_rows(y1_ref, tm) + route[:, 1:2] * _tiles_to_rows(y2_ref, tm)
    xo = x1_ref[...] + gt2_ref[0] * moe
    ms = jnp.mean(xo * xo, axis=-1, keepdims=True)
    o_ref[...] = xo * lax.rsqrt(ms + NORM_EPS) * fg_ref[...]


def _moe_combine(x1, route_t, yt, mod3, final_g, s, tm):
    n, d = x1.shape
    tiles_per_batch = s // tm
    n_steps = n // tm
    per_o = route_t.shape[1] // tm
    return pl.pallas_call(
        _moe_combine_kernel,
        out_shape=jax.ShapeDtypeStruct((n, d), F32),
        grid=(n_steps,),
        in_specs=[pl.BlockSpec((tm, d), lambda i: (i, 0)),
                  pl.BlockSpec((SUBLANES, tm), lambda i: (i // per_o, i % per_o)),
                  pl.BlockSpec((1, 1, d), lambda i: ((i // tiles_per_batch) * 6 + 5, 0, 0)),
                  pl.BlockSpec((1, d), lambda i: (0, 0)),
                  pl.BlockSpec((tm * SUB, LANES), lambda i: (i, 0)),
                  pl.BlockSpec((tm * SUB, LANES), lambda i: (n_steps + i, 0))],
        out_specs=pl.BlockSpec((tm, d), lambda i: (i, 0)),
        compiler_params=_cparams(("arbitrary",)),
        name="moe_combine",
    )(x1, route_t, mod3, final_g.reshape(1, d), yt, yt)


def _pick(n, candidates):
    for t in candidates:
        if n % t == 0:
            return t
    raise ValueError(f"no tile in {candidates} divides {n}")


def kernel(x, c, ada_w, ada_b, norm1_g, w_in, hg_lb, hg_norm_g, rw_mu, rw_w0, rw_w2, rw_a0, rw_a2, rw_g2, rw_kk, rw_ka, rw_rk, rw_gn_w, rw_gn_b, w_proj_a, w_proj_b, w_out, norm2_g, router_g_w, router_g_b, router_e_w, router_e_b, exp_w_gate, exp_w_up, exp_w_down, final_g):
    bsz, s, d = x.shape
    depth = ada_w.shape[0]
    hg_f = hg_lb.shape[-1]
    hg_w = hg_norm_g.shape[-1]
    rw_w = rw_w0.shape[-1]
    rw_cols = rw_mu.shape[-1]
    n_groups = router_g_w.shape[-1]
    n_experts = router_e_w.shape[-1]
    assert hg_f == hg_w and s % CHUNK == 0 and n_experts + n_groups <= LANES and d == 2 * SUB * LANES

    lb_all = jnp.cumsum(jax.nn.softmax(hg_lb.astype(F32), axis=0), axis=0)
    n = bsz * s
    blk = 512
    n_blocks = (n * TOP_K + n_experts * blk) // blk
    for l in range(depth):
        mod = _ada_mod(c, ada_w[l], ada_b[l])
        mod3 = mod.reshape(bsz * 6, 1, d)

        hg_cols = 2 * hg_f + 2 * hg_w
        rw_pad = -(-rw_cols // 256) * 256
        w_hg, w_rw, w_gt = _w_in_split(jnp.swapaxes(w_in[l], 0, 1), hg_cols, rw_cols, rw_pad)
        lora = (rw_w2.shape[1], rw_a2.shape[1], rw_g2.shape[1])
        assert rw_pad - 3 * rw_w == 256 and rw_cols == 3 * rw_w + sum(lora)
        hg, rw, gates = _in_proj(x, mod3, norm1_g[l], lb_all[l], rw_mu[l], lora, w_hg, w_rw, w_gt,
                                 _pick(s, (512, 256, 128, 64)))

        o_a = _hgrn2(hg, hg_norm_g[l], _pick(s, (1024, 512, 256, 128, 64)))
        o_b = _rwkv7(rw, rw_w0[l], rw_w2[l], rw_a0[l], rw_a2[l], rw_g2[l],
                     rw_kk[l], rw_ka[l], rw_rk[l].reshape(-1), rw_gn_w[l], rw_gn_b[l],
                     _pick(s, (2 * CHUNK, CHUNK)))

        wr = jnp.zeros((d, LANES), F32).at[:, :n_experts].set(router_e_w[l])
        wr = wr.at[:, n_experts:n_experts + n_groups].set(router_g_w[l])
        br = jnp.zeros((1, LANES), F32).at[0, :n_experts].set(router_e_b[l])
        br = br.at[0, n_experts:n_experts + n_groups].set(router_g_b[l])
        tm_o = _pick(s, (1024, 512, 256, 128, 64))
        x1, h2, route_t, counts = _out_proj(
            x, o_a, o_b, gates, mod3, norm2_g[l],
            w_proj_a[l].astype(BF16), w_proj_b[l].astype(BF16), w_out[l].astype(BF16),
            wr, br, n_groups, n_experts, tm_o)

        rt = route_t.reshape(n // tm_o, SUBLANES, tm_o)
        eid = rt[:, 2:4, :].astype(jnp.int32)
        rank = rt[:, 4:6, :].astype(jnp.int32)
        cnt = counts[:n_experts, 0].astype(jnp.int32)
        padded = (cnt + blk - 1) // blk * blk
        pad_end = jnp.cumsum(padded)
        pad_start = pad_end - padded
        e_ax = jnp.arange(n_experts, dtype=jnp.int32)[:, None, None, None]
        dest = rank + jnp.sum(jnp.where(eid[None] == e_ax, pad_start[:, None, None, None], 0), axis=0)
        blk_start = jnp.arange(n_blocks, dtype=jnp.int32) * blk
        blk_e = jnp.minimum(jnp.sum((pad_end[None, :] <= blk_start[:, None]).astype(jnp.int32), axis=1),
                            n_experts - 1)
        n_used = (pad_end[-1:] // blk).astype(jnp.int32)
        tm = tm_o
        smap = _moe_slotmap(dest.reshape(-1) + blk, pad_start + cnt, padded - cnt, n_used, n_blocks * blk, blk, tm_o, n)
        yt = _moe_experts_gather(h2, smap, blk_e, n_used, exp_w_gate[l], exp_w_up[l], exp_w_down[l], blk, n)
        last = l == depth - 1
        assert last, "the final RMSNorm is fused into the last layer's combine"
        out = _moe_combine(x1.reshape(n, d), route_t, yt, mod3, final_g, s, tm)
        x = out.reshape(bsz, s, d)
    return x
```

```python
import functools

import numpy as np
import jax
import jax.numpy as jnp
from jax import lax
from jax.experimental import pallas as pl
from jax.experimental.pallas import tpu as pltpu

F32 = jnp.float32
BF16 = jnp.bfloat16

NORM_EPS = 1e-6
HG_HEAD = 128
RW_HEAD = 64
RW_GN_EPS = 64e-5
TOP_K = 2
CHUNK = 64
LANES = 128
SUB = 4
U32 = jnp.uint32
MXU_K = 256
SUBLANES = 8
SMAP_BITS = 16
VMEM_LIMIT = 56 * 1024 * 1024

NT = (((1,), (1,)), ((), ()))
TN = (((0,), (0,)), ((), ()))


def _dot(a, b, dims=None, precision=None):
    if dims is None:
        return jnp.dot(a, b, preferred_element_type=F32, precision=precision)
    return lax.dot_general(a, b, dims, preferred_element_type=F32, precision=precision)


def _split3(x):
    hi = x.astype(BF16)
    r1 = x - hi.astype(F32)
    mid = r1.astype(BF16)
    lo = (r1 - mid.astype(F32)).astype(BF16)
    return hi, mid, lo


def _dot_exact_lhs(m3_bf16, x):
    return _dot(m3_bf16, jnp.concatenate(_split3(x), axis=0))


def _head_sums(x, m2_bf16):
    outs = []
    for g0 in range(0, x.shape[1], MXU_K):
        xg = x[:, g0:g0 + MXU_K]
        hi = xg.astype(BF16)
        lo = (xg - hi.astype(F32)).astype(BF16)
        outs.append(_dot(jnp.concatenate([hi, lo], axis=1), m2_bf16))
    return jnp.concatenate(outs, axis=1)


def _sigmoid(x):
    return 1.0 / (1.0 + jnp.exp(-x))


def _silu(x):
    return x * _sigmoid(x)


def _rows_to_tiles(ref, val):
    m, half = val.shape[0], val.shape[1] // 2
    hi = lax.bitcast_convert_type(val[:, :half].astype(BF16).astype(F32), U32)
    lo = lax.bitcast_convert_type(val[:, half:].astype(BF16).astype(F32), U32)
    w = (hi & jnp.uint32(0xFFFF0000)) | (lo >> 16)
    for j in range(SUB):
        ref[pl.ds(j, m, stride=SUB), :] = w[:, j * LANES:(j + 1) * LANES]


def _tiles_to_rows(ref, m, base=0):
    w = jnp.concatenate([ref[pl.ds(base * SUB + j, m, stride=SUB), :] for j in range(SUB)], axis=1)
    hi = lax.bitcast_convert_type(w & jnp.uint32(0xFFFF0000), F32)
    lo = lax.bitcast_convert_type(w << 16, F32)
    return jnp.concatenate([hi, lo], axis=1)


def _cparams(sem):
    return pltpu.CompilerParams(dimension_semantics=sem, vmem_limit_bytes=VMEM_LIMIT)


def _ada_kernel(cb_ref, w_ref, b_ref, o_ref, sc_ref, buf_ref, sem_ref):
    bsz, d = cb_ref.shape[0], cb_ref.shape[1]
    nbuf, tn = buf_ref.shape[0], buf_ref.shape[2]
    reps = tn // LANES
    n_tiles = o_ref.shape[1] // tn

    def copy(j):
        return pltpu.make_async_copy(w_ref.at[:, pl.ds(j * tn, tn)], buf_ref.at[j % nbuf], sem_ref.at[j % nbuf])

    for j in range(min(nbuf - 1, n_tiles)):
        copy(j).start()
    sc_ref[...] = _silu(cb_ref[...])
    o_ref[...] = jnp.zeros_like(o_ref)
    for j in range(n_tiles):
        if j + nbuf - 1 < n_tiles:
            copy(j + nbuf - 1).start()
        copy(j).wait()
        wv_ref = buf_ref.at[j % nbuf]

        def body(i, accs, wv_ref=wv_ref):
            k0 = pl.multiple_of(i * SUBLANES, SUBLANES)
            w = wv_ref[pl.ds(k0, SUBLANES), :]
            out = []
            for b in range(bsz):
                sc = sc_ref[b, pl.ds(k0, SUBLANES), :]
                out.append(accs[b] + w * jnp.concatenate([sc] * reps, axis=1))
            return tuple(out)

        accs = lax.fori_loop(0, d // SUBLANES, body, tuple(jnp.zeros((SUBLANES, tn), F32) for _ in range(bsz)),
                             unroll=4)
        cols = slice(j * tn, (j + 1) * tn)
        for b in range(bsz):
            o_ref[b:b + 1, cols] = jnp.sum(accs[b], axis=0, keepdims=True) + b_ref[:, cols]


def _ada_mod(c, w, b):
    bsz, d = c.shape
    n = w.shape[1]
    rows = -(-bsz // SUBLANES) * SUBLANES
    cb = jnp.broadcast_to(c[:, :, None], (bsz, d, LANES))
    tn = _pick(n, (512, 256, 128))
    out = pl.pallas_call(
        _ada_kernel,
        out_shape=jax.ShapeDtypeStruct((rows, n), F32),
        in_specs=[pl.BlockSpec(memory_space=pltpu.VMEM),
                  pl.BlockSpec(memory_space=pl.ANY),
                  pl.BlockSpec(memory_space=pltpu.VMEM)],
        out_specs=pl.BlockSpec(memory_space=pltpu.VMEM),
        scratch_shapes=[pltpu.VMEM((bsz, d, LANES), F32), pltpu.VMEM((4, d, tn), F32),
                        pltpu.SemaphoreType.DMA((4,))],
        compiler_params=pltpu.CompilerParams(vmem_limit_bytes=VMEM_LIMIT),
        name="ada_mod",
    )(cb, w, b.reshape(1, n))
    return out[:bsz]


def _in_proj_kernel(lora, chunks, x_ref, sh_ref, sc_ref, g_ref, lb_ref, mu_ref, wt_ref,
                    hg_ref, rw_ref, gt_ref, carry_ref, whg_ref, wrw_ref, wgt_ref, buf_ref, sem_ref):
    @pl.when((pl.program_id(0) == 0) & (pl.program_id(1) == 0))
    def _():
        _w_in_load(chunks, wt_ref, (whg_ref, wrw_ref, wgt_ref), buf_ref, sem_ref)

    @pl.when(pl.program_id(1) == 0)
    def _():
        carry_ref[...] = jnp.zeros_like(carry_ref)

    x = x_ref[0]
    ms = jnp.mean(x * x, axis=-1, keepdims=True)
    h = (x * lax.rsqrt(ms + NORM_EPS) * g_ref[...]) * (1.0 + sc_ref[0]) + sh_ref[0]
    hb = h.astype(BF16)
    lb = lb_ref[...]
    w = lb.shape[1]
    hg_maps = (_silu, lambda t: lb + (1.0 - lb) * _sigmoid(t), lambda t: t, _silu)
    for part, fn in enumerate(hg_maps):
        hg_ref[0, :, part * w:(part + 1) * w] = fn(_dot(hb, whg_ref[:, part * w:(part + 1) * w]))
    step = 512
    n_rw = wrw_ref.shape[1]
    row = lax.broadcasted_iota(jnp.int32, (x.shape[0], 256), 0)
    lane = lax.broadcasted_iota(jnp.int32, (x.shape[0], 256), 1)
    dl, al, gl = lora
    for n0 in range(0, n_rw, 256):
        cs = slice(n0, n0 + 256)
        p = _dot(hb, wrw_ref[:, cs])
        prev = jnp.where(row == 0, carry_ref[:, cs], pltpu.roll(p, 1, 0))
        carry_ref[:, cs] = p[x.shape[0] - 1:, :]
        xs = p + mu_ref[:, cs] * (prev - p)
        if n0 == n_rw - 256:
            xs = jnp.where(lane < dl, jnp.tanh(xs),
                           jnp.where(lane < dl + al, xs, jnp.where(lane < dl + al + gl, _sigmoid(xs), 0.0)))
        rw_ref[0, :, cs] = xs
    for n0 in range(0, wgt_ref.shape[1], step):
        gt_ref[0, :, n0:n0 + step] = _sigmoid(_dot(hb, wgt_ref[:, n0:n0 + step])).astype(BF16)


W_CHUNK = 256


def _w_in_load(chunks, wt_ref, stage, buf_ref, sem_ref):
    nbuf, ch = buf_ref.shape[0], buf_ref.shape[1]

    def copy(k):
        return pltpu.make_async_copy(wt_ref.at[pl.ds(chunks[k][2], ch), :], buf_ref.at[k % nbuf],
                                     sem_ref.at[k % nbuf])

    for k in range(min(nbuf - 1, len(chunks))):
        copy(k).start()
    for k, (oi, c0, _, valid) in enumerate(chunks):
        if k + nbuf - 1 < len(chunks):
            copy(k + nbuf - 1).start()
        copy(k).wait()
        t = buf_ref[k % nbuf].T
        if valid < ch:
            t = jnp.where(lax.broadcasted_iota(jnp.int32, t.shape, 1) < valid, t, 0.0)
        stage[oi][:, c0:c0 + ch] = t.astype(BF16)


def _w_in_chunks(n_in, n_hg, n_rw, n_rw_pad):
    n_gt = n_in - n_hg - n_rw
    ch = W_CHUNK
    assert n_hg % ch == 0 and n_rw_pad % ch == 0 and n_gt % ch == 0 and n_gt >= ch
    chunks = []
    for oi, (r0, nr, npad) in enumerate(((0, n_hg, n_hg), (n_hg, n_rw, n_rw_pad), (n_hg + n_rw, n_gt, n_gt))):
        chunks += [(oi, c0, r0 + c0, min(ch, nr - c0)) for c0 in range(0, npad, ch)]
    assert all(v > 0 for _, _, _, v in chunks)
    return tuple(chunks)


def _in_proj(x, mod3, norm_g, lb, mu, lora, w_in_t, n_hg, n_rw_cols, n_rw, tm):
    bsz, s, d = x.shape
    assert n_hg == 4 * lb.shape[0] and sum(lora) <= 256
    mup = jnp.zeros((1, n_rw), F32).at[0, :mu.shape[-1]].set(mu)
    n_gt = w_in_t.shape[0] - n_hg - n_rw_cols
    chunks = _w_in_chunks(w_in_t.shape[0], n_hg, n_rw_cols, n_rw)
    const = lambda b, i: (0, 0)
    return pl.pallas_call(
        functools.partial(_in_proj_kernel, lora, chunks),
        out_shape=(jax.ShapeDtypeStruct((bsz, s, n_hg), F32),
                   jax.ShapeDtypeStruct((bsz, s, n_rw), F32),
                   jax.ShapeDtypeStruct((bsz, s, n_gt), BF16)),
        grid=(bsz, s // tm),
        in_specs=[pl.BlockSpec((1, tm, d), lambda b, i: (b, i, 0)),
                  pl.BlockSpec((1, 1, d), lambda b, i: (b * 6 + 0, 0, 0)),
                  pl.BlockSpec((1, 1, d), lambda b, i: (b * 6 + 1, 0, 0)),
                  pl.BlockSpec((1, d), const),
                  pl.BlockSpec((1, lb.shape[0]), const),
                  pl.BlockSpec((1, n_rw), const),
                  pl.BlockSpec(memory_space=pl.ANY)],
        out_specs=(pl.BlockSpec((1, tm, n_hg), lambda b, i: (b, i, 0)),
                   pl.BlockSpec((1, tm, n_rw), lambda b, i: (b, i, 0)),
                   pl.BlockSpec((1, tm, n_gt), lambda b, i: (b, i, 0))),
        scratch_shapes=[pltpu.VMEM((1, n_rw), F32),
                        pltpu.VMEM((d, n_hg), BF16), pltpu.VMEM((d, n_rw), BF16), pltpu.VMEM((d, n_gt), BF16),
                        pltpu.VMEM((4, W_CHUNK, d), F32), pltpu.SemaphoreType.DMA((4,))],
        compiler_params=_cparams(("arbitrary", "arbitrary")),
        name="in_proj",
    )(x, mod3, mod3, norm_g.reshape(1, d), lb.reshape(1, -1), mup, w_in_t)


_HG_LEVELS = (32, 16, 8, 4, 2, 1)


def _hgrn2_consts(width):
    c = CHUNK
    t = np.arange(c)[:, None]
    s = np.arange(c)[None, :]
    blocks = [(s <= t), (s > t)]
    lvl_masks = []
    right = []
    for h in _HG_LEVELS:
        m = (t // (2 * h)) * 2 * h + h
        is_r = (t & h) != 0
        blk = np.where(is_r, (s >= m) & (s <= t), (s > t) & (s <= m - 1))
        blocks.append(blk)
        lvl_masks.append(is_r & ((s & h) == 0) & ((t // (2 * h)) == (s // (2 * h))))
        right.append(np.broadcast_to(is_r, (c, width)))
    mst = np.tile(np.concatenate(blocks, axis=0).astype(np.float32), (1, 3))
    lm = np.stack([np.eye(c, dtype=bool)] + lvl_masks).astype(np.float32)
    rm = np.stack(right).astype(np.float32)
    return jnp.asarray(mst, BF16), jnp.asarray(lm, F32), jnp.asarray(rm, F32)


def _hgrn2_kernel(q_ref, f_ref, i_ref, g_ref, ng_ref, mst_ref, lm_ref, rm_ref, o_ref, st_ref):
    c = CHUNK
    n_chunks = q_ref.shape[1] // c

    @pl.when(pl.program_id(1) == 0)
    def _():
        st_ref[...] = jnp.zeros_like(st_ref)

    nsub = next(n for n in (8, 4, 2, 1) if n_chunks % n == 0)

    def chunk_body(ci, carry):
        r0 = pl.multiple_of(ci * (nsub * c), nsub * c)
        for _ in _hgrn2_steps(q_ref, f_ref, i_ref, g_ref, ng_ref, mst_ref, lm_ref, rm_ref, o_ref, st_ref,
                              0, r0, nsub):
            pass
        return carry

    lax.fori_loop(0, n_chunks // nsub, chunk_body, 0)


def _hgrn2_steps(q_ref, f_ref, i_ref, g_ref, ng_ref, mst_ref, lm_ref, rm_ref, o_ref, st_ref, b, r0, nsub):
    c = CHUNK
    n_heads = q_ref.shape[2] // HG_HEAD
    mst = mst_ref[...]
    ng = ng_ref[...]
    heads = [slice(hd * HG_HEAD, (hd + 1) * HG_HEAD) for hd in range(n_heads)]
    subs = []
    for j in range(nsub):
        rows = pl.ds(r0 + j * c, c)
        q = q_ref[b, rows, :]
        f = f_ref[b, rows, :]
        k = 1.0 - f
        ex = jnp.exp2(_dot_exact_lhs(mst, jnp.log2(f)))
        subs.append(dict(rows=rows, q=q, k=k, ex=ex, vb=i_ref[b, rows, :].astype(BF16),
                         qd=(q * ex[0:c]).astype(BF16), kr=(k * ex[c:2 * c]).astype(BF16)))
        yield
    for sb in subs:
        qb, kb = sb['q'].astype(BF16), sb['k'].astype(BF16)
        sb['sc'] = [lm_ref[0] * _dot(qb[:, ls], kb[:, ls], NT) for ls in heads]
        sb['dqk'] = sb['q'] - sb['k']
    yield
    for li, h in enumerate(_HG_LEVELS):
        for sb in subs:
            if h % SUBLANES == 0:
                qk = jnp.concatenate([(sb['q'] if m % 2 else sb['k'])[m * h:(m + 1) * h] for m in range(c // h)],
                                     axis=0)
            else:
                qk = sb['k'] + rm_ref[li] * sb['dqk']
            g_l = (qk * sb['ex'][(2 + li) * c:(3 + li) * c]).astype(BF16)
            sb['sc'] = [s_h + lm_ref[li + 1] * _dot(g_l[:, ls], g_l[:, ls], NT)
                        for s_h, ls in zip(sb['sc'], heads)]
        yield
    for sb in subs:
        sb['kv'] = [_dot(sb['vb'][:, ls], sb['kr'][:, ls], TN) for ls in heads]
        sb['o'] = [_dot(s_h.astype(BF16), sb['vb'][:, ls]) for s_h, ls in zip(sb['sc'], heads)]
        yield
    sts = [st_ref[b, hd] for hd in range(n_heads)]
    for sb in subs:
        sb['o'] = [o_h + _dot(sb['qd'][:, ls], st.astype(BF16), NT) for o_h, ls, st in zip(sb['o'], heads, sts)]
        sts = [st * sb['ex'][c - 1:c, ls] + kv for st, ls, kv in zip(sts, heads, sb['kv'])]
        yield
    for hd in range(n_heads):
        st_ref[b, hd] = sts[hd]
    for sb in subs:
        on = [o_h * lax.rsqrt(jnp.mean(o_h * o_h, axis=-1, keepdims=True) + NORM_EPS) for o_h in sb['o']]
        o_full = jnp.concatenate(on, axis=1) * ng
        o_ref[b, sb['rows'], :] = (o_full * g_ref[b, sb['rows'], :]).astype(o_ref.dtype)
        yield


def _hgrn2(hg, norm_g, ts):
    bsz, s, n4 = hg.shape
    w = n4 // 4
    mst, lm, rm = _hgrn2_consts(w)
    n_heads = w // HG_HEAD
    const2 = lambda b, i: (0, 0)
    const3 = lambda b, i: (0, 0, 0)
    return pl.pallas_call(
        _hgrn2_kernel,
        out_shape=jax.ShapeDtypeStruct((bsz, s, w), BF16),
        grid=(bsz, s // ts),
        in_specs=[pl.BlockSpec((1, ts, w), lambda b, i: (b, i, 0)),
                  pl.BlockSpec((1, ts, w), lambda b, i: (b, i, 1)),
                  pl.BlockSpec((1, ts, w), lambda b, i: (b, i, 2)),
                  pl.BlockSpec((1, ts, w), lambda b, i: (b, i, 3)),
                  pl.BlockSpec((1, w), const2),
                  pl.BlockSpec(mst.shape, const2),
                  pl.BlockSpec(lm.shape, const3),
                  pl.BlockSpec(rm.shape, const3)],
        out_specs=pl.BlockSpec((1, ts, w), lambda b, i: (b, i, 0)),
        scratch_shapes=[pltpu.VMEM((1, n_heads, HG_HEAD, HG_HEAD), F32)],
        compiler_params=_cparams(("arbitrary", "arbitrary")),
        name="hgrn2",
    )(hg, hg, hg, hg, norm_g.reshape(1, w), mst, lm, rm)


def _rwkv_consts(width):
    c = CHUNK
    t = np.arange(c)[:, None]
    s = np.arange(c)[None, :]
    tri = np.tile((s <= t).astype(np.float32), (1, 3))
    tt = np.arange(2 * c)[:, None]
    ss = np.arange(2 * c)[None, :]
    same = (tt // c) == (ss // c)
    strict = same & ((ss % c) < (tt % c))
    incl = same & ((ss % c) <= (tt % c))
    hsum = (np.arange(MXU_K)[:, None] // RW_HEAD) == (np.arange(MXU_K)[None, :] // RW_HEAD)
    hsum = np.tile(hsum, (2, 1))
    return (jnp.asarray(tri, BF16), jnp.asarray(strict.astype(np.float32), F32),
            jnp.asarray(incl.astype(np.float32), F32), jnp.asarray(hsum.astype(np.float32), BF16))


def _rwkv7_kernel(p_ref, w0_ref, a0_ref, kk_ref, ka_ref, rk_ref, gnw_ref, gnb_ref,
                  w2_ref, a2_ref, g2_ref, tri_ref, sm_ref, im_ref, hs_ref,
                  o_ref, zt_ref):
    @pl.when(pl.program_id(0) == 0)
    def _():
        zt_ref[...] = jnp.zeros_like(zt_ref)

    for _ in _rwkv7_steps(p_ref, w0_ref, a0_ref, kk_ref, ka_ref, rk_ref, gnw_ref, gnb_ref,
                          w2_ref, a2_ref, g2_ref, tri_ref, sm_ref, im_ref, hs_ref, o_ref, zt_ref):
        pass


def _rwkv7_steps(p_ref, w0_ref, a0_ref, kk_ref, ka_ref, rk_ref, gnw_ref, gnb_ref,
                 w2_ref, a2_ref, g2_ref, tri_ref, sm_ref, im_ref, hs_ref, o_ref, zt_ref):
    c = CHUNK
    nb = p_ref.shape[0]
    nch = p_ref.shape[1] // c
    width = o_ref.shape[2]
    n_pairs = width // LANES

    hs = hs_ref[...]
    tri = tri_ref[...]
    smask = sm_ref[...] > 0
    imask = im_ref[...] > 0
    lane = lax.broadcasted_iota(jnp.int32, (c, LANES), 1)
    m0 = (lane < RW_HEAD).astype(F32)
    m1 = 1.0 - m0

    def stack(x):
        return jnp.concatenate([x * m0, x * m1], axis=0)

    xs = jnp.concatenate([p_ref[b] for b in range(nb)], axis=0)
    r_all = xs[:, 0:width]
    k_all = xs[:, width:2 * width]
    v_all = xs[:, 2 * width:3 * width]
    slab = xs[:, 3 * width:].astype(BF16)
    nz = -(w0_ref[...] + _dot(slab, w2_ref[...]))
    softplus = jnp.maximum(nz, 0.0) + jnp.log(1.0 + jnp.exp(-jnp.abs(nz)))
    ld_all = -jnp.exp(-softplus - 0.5)
    a_all = _sigmoid(a0_ref[...] + _dot(slab, a2_ref[...]))
    g_all = _dot(slab, g2_ref[...])
    kk0 = k_all * kk_ref[...]
    kk_all = kk0 * lax.rsqrt(jnp.maximum(_head_sums(kk0 * kk0, hs), 1e-24))
    k2_all = k_all * (1.0 + (a_all - 1.0) * ka_ref[...])
    yield

    units = []
    for b, j in [(b, j) for b in range(nb) for j in range(nch)]:
        rb = slice((b * nch + j) * c, (b * nch + j + 1) * c)
        r, k2, v, ld = r_all[rb], k2_all[rb], v_all[rb], ld_all[rb]
        a_in = -kk_all[rb]
        b_in = kk_all[rb] * a_all[rb]
        cum = _dot_exact_lhs(tri, ld)
        cum_t = cum[c - 1:c, :]
        e_c = jnp.exp(cum)
        e_nc = jnp.exp(-cum)
        e_rem = jnp.exp(cum_t - cum)
        at_f = a_in * jnp.exp(cum - ld)
        rt_f = r * e_c
        kt_f = k2 * e_nc
        bt_f = b_in * e_nc
        kh_f = k2 * e_rem
        bh_f = b_in * e_rem
        p_t = jnp.exp(cum_t)
        for pi in range(n_pairs):
            ls = slice(pi * LANES, (pi + 1) * LANES)
            units.append(dict(
                b=b, j=j, pi=pi,
                at=stack(at_f[:, ls]).astype(BF16), rt=stack(rt_f[:, ls]).astype(BF16),
                kt=stack(kt_f[:, ls]).astype(BF16), bt=stack(bt_f[:, ls]).astype(BF16),
                kh=stack(kh_f[:, ls]).astype(BF16), bh=stack(bh_f[:, ls]).astype(BF16),
                vs=stack(v[:, ls]).astype(BF16), p_t=p_t[:, ls]))
        yield

    for u in units:
        lhs = jnp.concatenate([u['at'], u['rt']], axis=0)
        u['g'] = _dot(lhs, jnp.concatenate([u['kt'], u['bt']], axis=0), NT)
    yield
    for u in units:
        g = u.pop('g')
        u['a_ak'] = jnp.where(smask, g[:2 * c, :2 * c], 0.0).astype(BF16)
        u['pw'] = jnp.where(smask, g[:2 * c, 2 * c:], 0.0).astype(BF16)
        u['a_r'] = jnp.where(jnp.concatenate([imask, imask], axis=1), g[2 * c:], 0.0).astype(BF16)
    for u in units:
        akv = _dot(u.pop('a_ak'), u['vs'])
        u['x'] = jnp.concatenate([u['at'].astype(F32), akv], axis=1)
    yield
    n_lvl = int(np.log2(c))
    for lvl in range(n_lvl):
        for u in units:
            u['x'] = u['x'] + _dot(u['pw'], u['x'].astype(BF16))
        yield
        if lvl + 1 < n_lvl:
            for u in units:
                u['pw'] = _dot(u['pw'], u['pw']).astype(BF16)
            yield
    for u in units:
        x = u.pop('x')
        u['wr'] = jnp.concatenate([x[:, :LANES].astype(BF16), u['rt']], axis=0)
        u['u_loc'] = x[:, LANES:]
    zt = {(b, pi): zt_ref[b, pi] for b in range(nb) for pi in range(n_pairs)}
    for j in range(nch):
        tail = [u for u in units if u['j'] == j]
        for u in tail:
            u['uy'] = _dot(u.pop('wr'), zt[u['b'], u['pi']].astype(BF16), NT)
        yield
        for u in tail:
            uy = u.pop('uy')
            u['u'] = (uy[:2 * c] + u.pop('u_loc')).astype(BF16)
            u['y0'] = uy[2 * c:]
        for u in tail:
            vu = jnp.concatenate([u['vs'], u['u']], axis=0)
            u['y'] = u.pop('y0') + _dot(u['a_r'], vu)
            upd = _dot(vu, jnp.concatenate([u['kh'], u['bh']], axis=0), TN)
            zt[u['b'], u['pi']] = zt[u['b'], u['pi']] * u['p_t'] + upd
        yield
    for (b, pi), z in zt.items():
        zt_ref[b, pi] = z

    inv_n = 1.0 / RW_HEAD
    y = jnp.concatenate(
        [jnp.concatenate([u['y'][:c] + u['y'][c:] for u in units if (u['b'], u['j']) == (b, j)], axis=1)
         for b in range(nb) for j in range(nch)], axis=0)
    mean = _head_sums(y, hs) * inv_n
    yield
    d = y - mean
    var = _head_sums(d * d, hs) * inv_n
    yield
    yn = d * lax.rsqrt(var + RW_GN_EPS) * gnw_ref[...] + gnb_ref[...]
    bonus = _head_sums(r_all * k2_all * rk_ref[...], hs) * v_all
    out = ((yn + bonus) * g_all).astype(o_ref.dtype)
    for b in range(nb):
        o_ref[b] = out[b * nch * c:(b + 1) * nch * c]


def _rwkv7(rw, w0, w2, a0, a2, g2, k_k, k_a, r_k, gn_w, gn_b, ts):
    bsz, s, cols = rw.shape
    width = w0.shape[-1]
    n_pairs = width // LANES
    slab = cols - 3 * width
    dl, al, gl = w2.shape[0], a2.shape[0], g2.shape[0]
    w2f = jnp.zeros((slab, width), F32).at[0:dl].set(w2).astype(BF16)
    a2f = jnp.zeros((slab, width), F32).at[dl:dl + al].set(a2).astype(BF16)
    g2f = jnp.zeros((slab, width), F32).at[dl + al:dl + al + gl].set(g2).astype(BF16)
    tri, sm, im, hs = _rwkv_consts(width)
    row = lambda x: x.reshape(1, width)
    const = lambda i: (0, 0)
    vec = pl.BlockSpec((1, width), const)
    return pl.pallas_call(
        _rwkv7_kernel,
        out_shape=jax.ShapeDtypeStruct((bsz, s, width), BF16),
        grid=(s // ts,),
        in_specs=[pl.BlockSpec((bsz, ts, cols), lambda i: (0, i, 0)),
                  vec, vec, vec, vec, vec, vec, vec,
                  pl.BlockSpec((slab, width), const),
                  pl.BlockSpec((slab, width), const),
                  pl.BlockSpec((slab, width), const),
                  pl.BlockSpec(tri.shape, const),
                  pl.BlockSpec(sm.shape, const),
                  pl.BlockSpec(im.shape, const),
                  pl.BlockSpec(hs.shape, const)],
        out_specs=pl.BlockSpec((bsz, ts, width), lambda i: (0, i, 0)),
        scratch_shapes=[pltpu.VMEM((bsz, n_pairs, LANES, LANES), F32)],
        compiler_params=_cparams(("arbitrary",)),
        name="rwkv7",
    )(rw, row(w0), row(a0), row(k_k), row(k_a), row(r_k), row(gn_w), row(gn_b),
      w2f, a2f, g2f, tri, sm, im, hs)


def _out_proj_kernel(n_groups, n_experts,
                     x_ref, oa_ref, ob_ref, ga_ref, gb_ref, gt1_ref, sc2_ref, sh2_ref, g2_ref,
                     wa_ref, wb_ref, wo_ref, wr_ref, wrl_ref, br_ref, upper_ref,
                     x1_ref, h2_ref, routet_ref, cnt_ref, carry_ref):
    first = (pl.program_id(0) == 0) & (pl.program_id(1) == 0)

    @pl.when(first)
    def _():
        carry_ref[...] = jnp.zeros_like(carry_ref)

    pa = _dot(oa_ref[0], wa_ref[...])
    pb = _dot(ob_ref[0], wb_ref[...])
    mixed = ga_ref[0].astype(F32) * pa + gb_ref[0].astype(F32) * pb
    x1 = x_ref[0] + gt1_ref[0] * _dot(mixed.astype(BF16), wo_ref[...])
    x1_ref[0] = x1
    ms = jnp.mean(x1 * x1, axis=-1, keepdims=True)
    h2 = (x1 * lax.rsqrt(ms + NORM_EPS) * g2_ref[...]) * (1.0 + sc2_ref[0]) + sh2_ref[0]
    _rows_to_tiles(h2_ref, h2)

    h2_hi = h2.astype(BF16)
    h2_lo = (h2 - h2_hi.astype(F32)).astype(BF16)
    logits = (_dot(wr_ref[...], h2_hi, NT) + _dot(wr_ref[...], h2_lo, NT) + _dot(wrl_ref[...], h2_hi, NT)
              + br_ref[...])
    row = lax.broadcasted_iota(jnp.int32, logits.shape, 0)
    neg = jnp.float32(-jnp.inf)
    big = jnp.int32(1 << 20)
    eg = n_experts // n_groups
    is_g = (row >= n_experts) & (row < n_experts + n_groups)
    lg = jnp.where(is_g, logits, neg)
    mg = jnp.max(lg, axis=0, keepdims=True)
    p_grp = 1.0 / jnp.sum(jnp.where(is_g, jnp.exp(lg - mg), 0.0), axis=0, keepdims=True)
    gidx = jnp.min(jnp.where(lg == mg, row, big), axis=0, keepdims=True) - n_experts
    sel = (row >= gidx * eg) & (row < gidx * eg + eg)
    le = jnp.where(sel, logits, neg)
    me = jnp.max(le, axis=0, keepdims=True)
    pe_un = jnp.where(sel, jnp.exp(le - me), 0.0)
    pe = jnp.where(sel, pe_un / jnp.sum(pe_un, axis=0, keepdims=True), -1.0)
    v1 = jnp.max(pe, axis=0, keepdims=True)
    i1 = jnp.min(jnp.where(pe == v1, row, big), axis=0, keepdims=True)
    pe2 = jnp.where(row == i1, -1.0, pe)
    v2 = jnp.max(pe2, axis=0, keepdims=True)
    i2 = jnp.min(jnp.where(pe2 == v2, row, big), axis=0, keepdims=True)
    wsum = v1 + v2
    w1 = p_grp * v1 / wsum
    w2 = p_grp * v2 / wsum

    oh1 = (row == i1).astype(F32)
    oh2 = (row == i2).astype(F32)
    both = oh1 + oh2
    before = _dot(both.astype(BF16), upper_ref[...]) + carry_ref[...]
    rank1 = jnp.sum(oh1 * before, axis=0, keepdims=True)
    rank2 = jnp.sum(oh2 * before, axis=0, keepdims=True)
    carry_ref[...] = carry_ref[...] + jnp.sum(both, axis=1, keepdims=True)
    cnt_ref[...] = carry_ref[...]
    zero = jnp.zeros_like(w1)
    routet_ref[...] = jnp.concatenate(
        [w1, w2, i1.astype(F32), i2.astype(F32), rank1, rank2, zero, zero], axis=0)


def _out_proj(x, o_a, o_b, gates, mod3, norm2_g, wa, wb, wo, wr, br, n_groups, n_experts, tm):
    bsz, s, d = x.shape
    wdt = o_a.shape[-1]
    upper = jnp.asarray(np.triu(np.ones((tm, tm), np.float32), 1), BF16)
    wrt = wr.T
    wr_hi = wrt.astype(BF16)
    wr_lo = (wrt - wr_hi.astype(F32)).astype(BF16)
    const = lambda b, i: (0, 0)
    tile = lambda b, i: (b, i, 0)
    kern = functools.partial(_out_proj_kernel, n_groups, n_experts)
    return pl.pallas_call(
        kern,
        out_shape=(jax.ShapeDtypeStruct((bsz, s, d), F32),
                   jax.ShapeDtypeStruct((bsz * s * SUB, LANES), U32),
                   jax.ShapeDtypeStruct((bsz * (s // tm) * SUBLANES, tm), F32),
                   jax.ShapeDtypeStruct((LANES, 1), F32)),
        grid=(bsz, s // tm),
        in_specs=[pl.BlockSpec((1, tm, d), tile),
                  pl.BlockSpec((1, tm, wdt), tile),
                  pl.BlockSpec((1, tm, wdt), tile),
                  pl.BlockSpec((1, tm, d), lambda b, i: (b, i, 0)),
                  pl.BlockSpec((1, tm, d), lambda b, i: (b, i, 1)),
                  pl.BlockSpec((1, 1, d), lambda b, i: (b * 6 + 2, 0, 0)),
                  pl.BlockSpec((1, 1, d), lambda b, i: (b * 6 + 4, 0, 0)),
                  pl.BlockSpec((1, 1, d), lambda b, i: (b * 6 + 3, 0, 0)),
                  pl.BlockSpec((1, d), const),
                  pl.BlockSpec(wa.shape, const),
                  pl.BlockSpec(wb.shape, const),
                  pl.BlockSpec(wo.shape, const),
                  pl.BlockSpec(wrt.shape, const),
                  pl.BlockSpec(wrt.shape, const),
                  pl.BlockSpec((LANES, 1), const),
                  pl.BlockSpec((tm, tm), const)],
        out_specs=(pl.BlockSpec((1, tm, d), tile),
                   pl.BlockSpec((tm * SUB, LANES), lambda b, i: (b * (s // tm) + i, 0)),
                   pl.BlockSpec((SUBLANES, tm), lambda b, i: (b * (s // tm) + i, 0)),
                   pl.BlockSpec((LANES, 1), const)),
        scratch_shapes=[pltpu.VMEM((LANES, 1), F32)],
        compiler_params=_cparams(("arbitrary", "arbitrary")),
        name="out_proj",
    )(x, o_a, o_b, gates, gates, mod3, mod3, mod3, norm2_g.reshape(1, d), wa, wb, wo, wr_hi, wr_lo,
      br.reshape(LANES, 1), upper)


def _moe_slotmap_kernel(tm_o, n_tok, blk, dest_ref, zstart_ref, zcnt_ref, nused_ref, smap_ref):
    i = pl.program_id(0)
    n_slots = smap_ref.shape[0] - blk
    n_experts = zcnt_ref.shape[0]

    @pl.when(i == 0)
    def _():
        unroll = 8

        def init(g, carry):
            for u in range(unroll):
                smap_ref[blk + g * unroll + u] = TOP_K * n_tok
            return carry
        lax.fori_loop(nused_ref[0] * (blk // unroll), n_slots // unroll, init, 0)
        for r in range(blk):
            smap_ref[r] = TOP_K * n_tok + n_experts * blk + r
        for e in range(n_experts):
            def pad(g, carry, e=e):
                for u in range(unroll):
                    j = jnp.maximum(zcnt_ref[e] - 1 - (g * unroll + u), 0)
                    smap_ref[blk + zstart_ref[e] + j] = TOP_K * n_tok + e * blk + j
                return carry
            lax.fori_loop(0, (zcnt_ref[e] + unroll - 1) // unroll, pad, 0)

    base = i * (TOP_K * tm_o)
    both = 1 + (1 << SMAP_BITS)
    for k in range(TOP_K):
        v0 = k * n_tok + (i * tm_o) * both
        for r in range(tm_o):
            smap_ref[dest_ref[base + k * tm_o + r]] = v0 + r * both


def _moe_slotmap(dest, zstart, zcnt, n_used, n_slots, blk, tm_o, n_tok):
    grid_spec = pltpu.PrefetchScalarGridSpec(
        num_scalar_prefetch=4,
        grid=(n_tok // tm_o,),
        in_specs=[],
        out_specs=pl.BlockSpec(memory_space=pltpu.SMEM),
    )
    n_experts = zcnt.shape[0]
    assert TOP_K * n_tok + (n_experts + 1) * blk <= (1 << SMAP_BITS) and n_tok <= (1 << (31 - SMAP_BITS))
    return pl.pallas_call(
        functools.partial(_moe_slotmap_kernel, tm_o, n_tok, blk),
        out_shape=jax.ShapeDtypeStruct((blk + n_slots,), jnp.int32),
        grid_spec=grid_spec,
        compiler_params=_cparams(("arbitrary",)),
        name="moe_slotmap",
    )(dest, zstart, zcnt, n_used)


def _moe_expert_gather_kernel(n_tok, blk_e_ref, nused_ref, smap_ref, h_ref, wg_ref, wu_ref, wd_ref, y_ref,
                              hv, xbuf, ystage, wgb, wub, wdb, sem, hsem):
    i = pl.program_id(0)
    nb = pl.num_programs(0)
    n_used = nused_ref[0]
    blk = xbuf.shape[1] // SUB
    first_real_blocks = TOP_K * n_tok // blk

    def gather(b, slot, rows):
        for r in rows:
            tok = lax.shift_right_logical(smap_ref[(b + 1) * blk + r], SMAP_BITS)
            src = pl.multiple_of(tok * SUB, SUB)
            xbuf[slot, pl.ds(r * SUB, SUB), :] = hv[pl.ds(src, SUB), :]

    def issue(b, slot, rows):
        for r in rows:
            t = smap_ref[(b + 1) * blk + r] & ((1 << SMAP_BITS) - 1)
            dst = pl.multiple_of(t * SUB, SUB)
            pltpu.make_async_copy(ystage.at[slot, pl.ds(r * SUB, SUB), :], y_ref.at[pl.ds(dst, SUB), :],
                                  sem.at[slot]).start(priority=r % 2)

    def wait_block(slot):
        pltpu.make_async_copy(ystage.at[slot], y_ref.at[pl.ds(0, blk * SUB), :], sem.at[slot]).wait()

    @pl.when(i == 0)
    def _():
        load = pltpu.make_async_copy(h_ref, hv, hsem)
        load.start()
        ystage[...] = jnp.zeros_like(ystage)
        n_spare_blocks = y_ref.shape[0] // (blk * SUB) - first_real_blocks

        def spare_copy(c):
            dst = (first_real_blocks + c) * blk * SUB
            return pltpu.make_async_copy(ystage.at[0], y_ref.at[pl.ds(dst, blk * SUB), :], sem.at[0])

        for c in range(n_spare_blocks):
            spare_copy(c).start()
        for c in range(n_spare_blocks):
            spare_copy(c).wait()
        load.wait()
        gather(0, 0, range(blk))

    new_expert = (i == 0) | (blk_e_ref[i] != blk_e_ref[jnp.maximum(i - 1, 0)])

    @pl.when((i < n_used) & new_expert)
    def _():
        wgb[...] = wg_ref[0].astype(BF16)
        wub[...] = wu_ref[0].astype(BF16)
        wdb[...] = wd_ref[0].astype(BF16)

    @pl.when(i < n_used)
    def _():
        slot = i % 2
        pslot = 1 - slot
        nxt = jnp.minimum(i + 1, nb - 1)

        @pl.when(i > 0)
        def _():
            wait_block(slot)

        q = blk // 4
        xb = _tiles_to_rows(xbuf.at[slot], blk).astype(BF16)
        issue(i - 1, pslot, range(0, q))
        gather(nxt, pslot, range(0, q))
        hg = _dot(xb, wgb[...])
        issue(i - 1, pslot, range(q, 2 * q))
        gather(nxt, pslot, range(q, 2 * q))
        hu = _dot(xb, wub[...])
        issue(i - 1, pslot, range(2 * q, 3 * q))
        gather(nxt, pslot, range(2 * q, 3 * q))
        hid = (_silu(hg) * hu).astype(BF16)
        y = _dot(hid, wdb[...])
        issue(i - 1, pslot, range(3 * q, blk))
        gather(nxt, pslot, range(3 * q, blk))
        _rows_to_tiles(ystage.at[slot], y)

        @pl.when(i == n_used - 1)
        def _():
            issue(i, slot, range(blk))
            wait_block(pslot)
            wait_block(slot)


def _moe_experts_gather(h2, smap, blk_e, n_used, w_gate, w_up, w_down, blk, n_tok):
    d = w_gate.shape[1]
    assert d == 2 * SUB * LANES and h2.shape == (n_tok * SUB, LANES)
    n_slots = smap.shape[0] - blk
    nb = n_slots // blk
    f = w_gate.shape[-1]
    n_experts = w_gate.shape[0]
    assert (TOP_K * n_tok) % blk == 0
    n_tiles = TOP_K * n_tok + (n_experts + 1) * blk
    grid_spec = pltpu.PrefetchScalarGridSpec(
        num_scalar_prefetch=3,
        grid=(nb,),
        in_specs=[pl.BlockSpec(memory_space=pl.ANY),
                  pl.BlockSpec((1, d, f), lambda i, be, nu, sm: (be[i], 0, 0)),
                  pl.BlockSpec((1, d, f), lambda i, be, nu, sm: (be[i], 0, 0)),
                  pl.BlockSpec((1, f, d), lambda i, be, nu, sm: (be[i], 0, 0))],
        out_specs=pl.BlockSpec(memory_space=pl.ANY),
        scratch_shapes=[pltpu.VMEM((n_tok * SUB, LANES), U32),
                        pltpu.VMEM((2, blk * SUB, LANES), U32),
                        pltpu.VMEM((2, blk * SUB, LANES), U32),
                        pltpu.VMEM((d, f), BF16),
                        pltpu.VMEM((d, f), BF16),
                        pltpu.VMEM((f, d), BF16),
                        pltpu.SemaphoreType.DMA((2,)),
                        pltpu.SemaphoreType.DMA],
    )
    return pl.pallas_call(
        functools.partial(_moe_expert_gather_kernel, n_tok),
        out_shape=jax.ShapeDtypeStruct((n_tiles * SUB, LANES), U32),
        grid_spec=grid_spec,
        compiler_params=_cparams(("arbitrary",)),
        name="moe_experts",
    )(blk_e, n_used, smap, h2, w_gate, w_up, w_down)


def _moe_combine_kernel(x1_ref, routet_ref, gt2_ref, fg_ref, y1_ref, y2_ref, o_ref):
    tm = x1_ref.shape[0]
    route = routet_ref[...].T
    moe = route[:, 0:1] * _tiles_to_rows(y1_ref, tm) + route[:, 1:2] * _tiles_to_rows(y2_ref, tm)
    xo = x1_ref[...] + gt2_ref[0] * moe
    ms = jnp.mean(xo * xo, axis=-1, keepdims=True)
    o_ref[...] = xo * lax.rsqrt(ms + NORM_EPS) * fg_ref[...]


def _moe_combine(x1, route_t, yt, mod3, final_g, s, tm):
    n, d = x1.shape
    tiles_per_batch = s // tm
    n_steps = n // tm
    per_o = route_t.shape[1] // tm
    return pl.pallas_call(
        _moe_combine_kernel,
        out_shape=jax.ShapeDtypeStruct((n, d), F32),
        grid=(n_steps,),
        in_specs=[pl.BlockSpec((tm, d), lambda i: (i, 0)),
                  pl.BlockSpec((SUBLANES, tm), lambda i: (i // per_o, i % per_o)),
                  pl.BlockSpec((1, 1, d), lambda i: ((i // tiles_per_batch) * 6 + 5, 0, 0)),
                  pl.BlockSpec((1, d), lambda i: (0, 0)),
                  pl.BlockSpec((tm * SUB, LANES), lambda i: (i, 0)),
                  pl.BlockSpec((tm * SUB, LANES), lambda i: (n_steps + i, 0))],
        out_specs=pl.BlockSpec((tm, d), lambda i: (i, 0)),
        compiler_params=_cparams(("arbitrary",)),
        name="moe_combine",
    )(x1, route_t, mod3, final_g.reshape(1, d), yt, yt)


def _pick(n, candidates):
    for t in candidates:
        if n % t == 0:
            return t
    raise ValueError(f"no tile in {candidates} divides {n}")


def kernel(x, c, ada_w, ada_b, norm1_g, w_in, hg_lb, hg_norm_g, rw_mu, rw_w0, rw_w2, rw_a0, rw_a2, rw_g2, rw_kk, rw_ka, rw_rk, rw_gn_w, rw_gn_b, w_proj_a, w_proj_b, w_out, norm2_g, router_g_w, router_g_b, router_e_w, router_e_b, exp_w_gate, exp_w_up, exp_w_down, final_g):
    bsz, s, d = x.shape
    depth = ada_w.shape[0]
    hg_f = hg_lb.shape[-1]
    hg_w = hg_norm_g.shape[-1]
    rw_w = rw_w0.shape[-1]
    rw_cols = rw_mu.shape[-1]
    n_groups = router_g_w.shape[-1]
    n_experts = router_e_w.shape[-1]
    assert hg_f == hg_w and s % CHUNK == 0 and n_experts + n_groups <= LANES and d == 2 * SUB * LANES

    lb_all = jnp.cumsum(jax.nn.softmax(hg_lb.astype(F32), axis=0), axis=0)
    n = bsz * s
    blk = 512
    n_blocks = (n * TOP_K + n_experts * blk) // blk
    for l in range(depth):
        mod = _ada_mod(c, ada_w[l], ada_b[l])
        mod3 = mod.reshape(bsz * 6, 1, d)

        hg_cols = 2 * hg_f + 2 * hg_w
        rw_pad = -(-rw_cols // 256) * 256
        lora = (rw_w2.shape[1], rw_a2.shape[1], rw_g2.shape[1])
        assert rw_pad - 3 * rw_w == 256 and rw_cols == 3 * rw_w + sum(lora)
        hg, rw, gates = _in_proj(x, mod3, norm1_g[l], lb_all[l], rw_mu[l], lora, jnp.swapaxes(w_in[l], 0, 1),
                                 hg_cols, rw_cols, rw_pad, _pick(s, (512, 256, 128, 64)))

        o_a = _hgrn2(hg, hg_norm_g[l], _pick(s, (1024, 512, 256, 128, 64)))
        o_b = _rwkv7(rw, rw_w0[l], rw_w2[l], rw_a0[l], rw_a2[l], rw_g2[l],
                     rw_kk[l], rw_ka[l], rw_rk[l].reshape(-1), rw_gn_w[l], rw_gn_b[l],
                     _pick(s, (2 * CHUNK, CHUNK)))

        wr = jnp.zeros((d, LANES), F32).at[:, :n_experts].set(router_e_w[l])
        wr = wr.at[:, n_experts:n_experts + n_groups].set(router_g_w[l])
        br = jnp.zeros((1, LANES), F32).at[0, :n_experts].set(router_e_b[l])
        br = br.at[0, n_experts:n_experts + n_groups].set(router_g_b[l])
        tm_o = _pick(s, (1024, 512, 256, 128, 64))
        x1, h2, route_t, counts = _out_proj(
            x, o_a, o_b, gates, mod3, norm2_g[l],
            w_proj_a[l].astype(BF16), w_proj_b[l].astype(BF16), w_out[l].astype(BF16),
            wr, br, n_groups, n_experts, tm_o)

        rt = route_t.reshape(n // tm_o, SUBLANES, tm_o)
        eid = rt[:, 2:4, :].astype(jnp.int32)
        rank = rt[:, 4:6, :].astype(jnp.int32)
        cnt = counts[:n_experts, 0].astype(jnp.int32)
        padded = (cnt + blk - 1) // blk * blk
        pad_end = jnp.cumsum(padded)
        pad_start = pad_end - padded
        e_ax = jnp.arange(n_experts, dtype=jnp.int32)[:, None, None, None]
        dest = rank + jnp.sum(jnp.where(eid[None] == e_ax, pad_start[:, None, None, None], 0), axis=0)
        blk_start = jnp.arange(n_blocks, dtype=jnp.int32) * blk
        blk_e = jnp.minimum(jnp.sum((pad_end[None, :] <= blk_start[:, None]).astype(jnp.int32), axis=1),
                            n_experts - 1)
        n_used = (pad_end[-1:] // blk).astype(jnp.int32)
        tm = tm_o
        smap = _moe_slotmap(dest.reshape(-1) + blk, pad_start + cnt, padded - cnt, n_used, n_blocks * blk, blk, tm_o, n)
        yt = _moe_experts_gather(h2, smap, blk_e, n_used, exp_w_gate[l], exp_w_up[l], exp_w_down[l], blk, n)
        last = l == depth - 1
        assert last, "the final RMSNorm is fused into the last layer's combine"
        out = _moe_combine(x1.reshape(n, d), route_t, yt, mod3, final_g, s, tm)
        x = out.reshape(bsz, s, d)
    return x
```
